```python
import jax, jax.numpy as jnp
from jax import lax
import numpy as np

D_MODEL = 1024
BATCH = 32
SEQ = 2048
DEPTH = 1

MEM_LEN = 256
CONV_WIDTH = 31
CONV_DIM = 1024
SGU_DIM = 1024
SGU_GROUPS = 8
SGU_CHUNK = 128
XATTN_HEADS = 4
XATTN_HEAD_DIM = D_MODEL // XATTN_HEADS
FFN_HIDDEN = ((-(-8 * D_MODEL // 3) + 255) // 256) * 256
IN_COLS = 2 * CONV_DIM + 2 * SGU_DIM + 2 * D_MODEL
RMS_EPS = 1e-6
LN_EPS = 1e-5

kernel_name = "hybrid_conv_sgu_gated_block"


def rmsnorm(x, g):
    xf = x.astype(jnp.float32)
    y = xf * lax.rsqrt(jnp.mean(xf * xf, axis=-1, keepdims=True) + RMS_EPS)
    return (y * g.astype(jnp.float32)).astype(x.dtype)


def layernorm(x, g, b):
    xf = x.astype(jnp.float32)
    mu = jnp.mean(xf, axis=-1, keepdims=True)
    var = jnp.mean(jnp.square(xf - mu), axis=-1, keepdims=True)
    y = (xf - mu) * lax.rsqrt(var + LN_EPS) * g.astype(jnp.float32) + b.astype(jnp.float32)
    return y.astype(x.dtype)


def causal_depthwise_conv(x, w, b):
    y = lax.conv_general_dilated(
        x, w[:, None, :], window_strides=(1,), padding=[(CONV_WIDTH - 1, 0)],
        dimension_numbers=("NWC", "WIO", "NWC"), feature_group_count=x.shape[-1])
    return y + b


def chunked_spatial_gating(u, v, w_s, b_s):
    B, S, C = v.shape
    n_chunks = S // SGU_CHUNK
    gd = C // SGU_GROUPS
    mask = jnp.tril(jnp.ones((SGU_CHUNK, SGU_CHUNK), dtype=bool))
    w = jnp.where(mask[None], w_s, jnp.zeros_like(w_s))
    vc = v.reshape(B, n_chunks, SGU_CHUNK, SGU_GROUPS, gd)
    z = jnp.einsum('gts,bnsgc->bntgc', w, vc) + jnp.transpose(b_s)[None, None, :, :, None]
    return u * z.reshape(B, S, C)


def mixer_block(h, w_in, b_gate, conv_w, conv_b, conv_ln_g, conv_ln_b, w_conv_out,
                sgu_ln_g, sgu_ln_b, sgu_w, sgu_b, w_sgu_out, w_mix_out):
    p = jnp.einsum('bsd,de->bse', h, w_in)
    a_val, a_gate, b_u, b_v, g_a, g_b = jnp.split(
        p, np.cumsum([CONV_DIM, CONV_DIM, SGU_DIM, SGU_DIM, D_MODEL]).tolist(), axis=-1)
    a = a_val * jax.nn.sigmoid(a_gate)
    a = causal_depthwise_conv(a, conv_w, conv_b)
    a = jax.nn.silu(layernorm(a, conv_ln_g, conv_ln_b))
    y_a = jnp.einsum('bsc,cd->bsd', a, w_conv_out)
    u = jax.nn.gelu(b_u)
    v = layernorm(jax.nn.gelu(b_v), sgu_ln_g, sgu_ln_b)
    y_b = jnp.einsum('bsc,cd->bsd', chunked_spatial_gating(u, v, sgu_w, sgu_b), w_sgu_out)
    merged = jax.nn.sigmoid(g_a + b_gate[0]) * y_a + jax.nn.sigmoid(g_b + b_gate[1]) * y_b
    return jnp.einsum('bsd,de->bse', merged, w_mix_out)


def memory_cross_attention(h, mem_n, w_q, w_kv, w_xo):
    B, S, _ = h.shape
    M = mem_n.shape[1]
    q = jnp.einsum('bsd,de->bse', h, w_q).reshape(B, S, XATTN_HEADS, XATTN_HEAD_DIM)
    kv = jnp.einsum('bmd,de->bme', mem_n, w_kv)
    k, v = jnp.split(kv, 2, axis=-1)
    k = k.reshape(B, M, XATTN_HEADS, XATTN_HEAD_DIM)
    v = v.reshape(B, M, XATTN_HEADS, XATTN_HEAD_DIM)
    s = jnp.einsum('bshd,bmhd->bhsm', q, k).astype(jnp.float32) * (XATTN_HEAD_DIM ** -0.5)
    pr = jax.nn.softmax(s, axis=-1).astype(v.dtype)
    o = jnp.einsum('bhsm,bmhd->bshd', pr, v).reshape(B, S, D_MODEL)
    return jnp.einsum('bsd,de->bse', o, w_xo)


def swiglu_ffn(h, w_gu, w_down):
    gu = jnp.einsum('bsd,df->bsf', h, w_gu)
    gt, up = jnp.split(gu, 2, axis=-1)
    return jnp.einsum('bsf,fd->bsd', jax.nn.silu(gt) * up, w_down)


def _fwd_setup_inputs(seed: int = 0) -> dict:
    key = jax.random.key(seed)
    ks = jax.random.split(key, 32)
    L, D = DEPTH, D_MODEL
    f32 = jnp.float32

    def nrm(k, shape, scale):
        return jax.random.normal(k, shape, f32) * scale

    def gain(k, shape):
        return 1.0 + 0.02 * jax.random.normal(k, shape, f32)

    return {
        "x": jax.random.normal(ks[0], (BATCH, SEQ, D), f32),
        "mem": jax.random.normal(ks[1], (BATCH, MEM_LEN, D), f32),
        "norm_mix": gain(ks[2], (L, D)),
        "w_in": nrm(ks[3], (L, D, IN_COLS), D ** -0.5),
        "b_gate": nrm(ks[4], (L, 2, D), 0.02),
        "conv_w": nrm(ks[5], (L, CONV_WIDTH, CONV_DIM), CONV_WIDTH ** -0.5),
        "conv_b": nrm(ks[6], (L, CONV_DIM), 0.02),
        "conv_ln_g": gain(ks[7], (L, CONV_DIM)),
        "conv_ln_b": nrm(ks[8], (L, CONV_DIM), 0.02),
        "w_conv_out": nrm(ks[9], (L, CONV_DIM, D), CONV_DIM ** -0.5),
        "sgu_ln_g": gain(ks[10], (L, SGU_DIM)),
        "sgu_ln_b": nrm(ks[11], (L, SGU_DIM), 0.02),
        "sgu_w": nrm(ks[12], (L, SGU_GROUPS, SGU_CHUNK, SGU_CHUNK), SGU_CHUNK ** -0.5),
        "sgu_b": gain(ks[13], (L, SGU_GROUPS, SGU_CHUNK)),
        "w_sgu_out": nrm(ks[14], (L, SGU_DIM, D), SGU_DIM ** -0.5),
        "w_mix_out": nrm(ks[15], (L, D, D), D ** -0.5),
        "norm_xattn": gain(ks[16], (L, D)),
        "norm_mem": gain(ks[17], (L, D)),
        "w_q": nrm(ks[18], (L, D, D), D ** -0.5),
        "w_kv": nrm(ks[19], (L, D, 2 * D), D ** -0.5),
        "w_xo": nrm(ks[20], (L, D, D), D ** -0.5),
        "norm_ffn": gain(ks[21], (L, D)),
        "w_gu": nrm(ks[22], (L, D, 2 * FFN_HIDDEN), D ** -0.5),
        "w_down": nrm(ks[23], (L, FFN_HIDDEN, D), FFN_HIDDEN ** -0.5),
        "norm_final": gain(ks[24], (D,)),
    }


def _fwd_reference(x, mem, norm_mix, w_in, b_gate, conv_w, conv_b, conv_ln_g, conv_ln_b, w_conv_out,
              sgu_ln_g, sgu_ln_b, sgu_w, sgu_b, w_sgu_out, w_mix_out,
              norm_xattn, norm_mem, w_q, w_kv, w_xo,
              norm_ffn, w_gu, w_down, norm_final):
    for l in range(DEPTH):
        h = rmsnorm(x, norm_mix[l])
        x = x + mixer_block(h, w_in[l], b_gate[l], conv_w[l], conv_b[l], conv_ln_g[l], conv_ln_b[l],
                            w_conv_out[l], sgu_ln_g[l], sgu_ln_b[l], sgu_w[l], sgu_b[l],
                            w_sgu_out[l], w_mix_out[l])
        h = rmsnorm(x, norm_xattn[l])
        mem_n = rmsnorm(mem, norm_mem[l])
        x = x + memory_cross_attention(h, mem_n, w_q[l], w_kv[l], w_xo[l])
        h = rmsnorm(x, norm_ffn[l])
        x = x + swiglu_ffn(h, w_gu[l], w_down[l])
    return rmsnorm(x, norm_final)


import jax as _jax
import jax.numpy as _jnp

TWIN_FORMAT = 'train_step'
FWD_PARAMS = ['x', 'mem', 'norm_mix', 'w_in', 'b_gate', 'conv_w', 'conv_b', 'conv_ln_g', 'conv_ln_b', 'w_conv_out', 'sgu_ln_g', 'sgu_ln_b', 'sgu_w', 'sgu_b', 'w_sgu_out', 'w_mix_out', 'norm_xattn', 'norm_mem', 'w_q', 'w_kv', 'w_xo', 'norm_ffn', 'w_gu', 'w_down', 'norm_final']
TWIN_WEIGHTS = ['norm_mix', 'w_in', 'b_gate', 'conv_w', 'conv_b', 'conv_ln_g', 'conv_ln_b', 'w_conv_out', 'sgu_ln_g', 'sgu_ln_b', 'sgu_w', 'sgu_b', 'w_sgu_out', 'w_mix_out', 'norm_xattn', 'norm_mem', 'w_q', 'w_kv', 'w_xo', 'norm_ffn', 'w_gu', 'w_down', 'norm_final']
TWIN_DIFF_INPUT = 'x'
TWIN_INPUTS = ['x', 'mem', 'norm_mix', 'w_in', 'b_gate', 'conv_w', 'conv_b', 'conv_ln_g', 'conv_ln_b', 'w_conv_out', 'sgu_ln_g', 'sgu_ln_b', 'sgu_w', 'sgu_b', 'w_sgu_out', 'w_mix_out', 'norm_xattn', 'norm_mem', 'w_q', 'w_kv', 'w_xo', 'norm_ffn', 'w_gu', 'w_down', 'norm_final', 'loss_target', 'm_norm_mix', 'm_w_in', 'm_b_gate', 'm_conv_w', 'm_conv_b', 'm_conv_ln_g', 'm_conv_ln_b', 'm_w_conv_out', 'm_sgu_ln_g', 'm_sgu_ln_b', 'm_sgu_w', 'm_sgu_b', 'm_w_sgu_out', 'm_w_mix_out', 'm_norm_xattn', 'm_norm_mem', 'm_w_q', 'm_w_kv', 'm_w_xo', 'm_norm_ffn', 'm_w_gu', 'm_w_down', 'm_norm_final', 'v_norm_mix', 'v_w_in', 'v_b_gate', 'v_conv_w', 'v_conv_b', 'v_conv_ln_g', 'v_conv_ln_b', 'v_w_conv_out', 'v_sgu_ln_g', 'v_sgu_ln_b', 'v_sgu_w', 'v_sgu_b', 'v_w_sgu_out', 'v_w_mix_out', 'v_norm_xattn', 'v_norm_mem', 'v_w_q', 'v_w_kv', 'v_w_xo', 'v_norm_ffn', 'v_w_gu', 'v_w_down', 'v_norm_final']
TWIN_OUTPUTS = ['loss', 'grad_x', 'grad_norm_mix', 'grad_w_in', 'grad_b_gate', 'grad_conv_w', 'grad_conv_b', 'grad_conv_ln_g', 'grad_conv_ln_b', 'grad_w_conv_out', 'grad_sgu_ln_g', 'grad_sgu_ln_b', 'grad_sgu_w', 'grad_sgu_b', 'grad_w_sgu_out', 'grad_w_mix_out', 'grad_norm_xattn', 'grad_norm_mem', 'grad_w_q', 'grad_w_kv', 'grad_w_xo', 'grad_norm_ffn', 'grad_w_gu', 'grad_w_down', 'grad_norm_final', 'delta_norm_mix', 'delta_w_in', 'delta_b_gate', 'delta_conv_w', 'delta_conv_b', 'delta_conv_ln_g', 'delta_conv_ln_b', 'delta_w_conv_out', 'delta_sgu_ln_g', 'delta_sgu_ln_b', 'delta_sgu_w', 'delta_sgu_b', 'delta_w_sgu_out', 'delta_w_mix_out', 'delta_norm_xattn', 'delta_norm_mem', 'delta_w_q', 'delta_w_kv', 'delta_w_xo', 'delta_norm_ffn', 'delta_w_gu', 'delta_w_down', 'delta_norm_final', 'new_m_norm_mix', 'new_m_w_in', 'new_m_b_gate', 'new_m_conv_w', 'new_m_conv_b', 'new_m_conv_ln_g', 'new_m_conv_ln_b', 'new_m_w_conv_out', 'new_m_sgu_ln_g', 'new_m_sgu_ln_b', 'new_m_sgu_w', 'new_m_sgu_b', 'new_m_w_sgu_out', 'new_m_w_mix_out', 'new_m_norm_xattn', 'new_m_norm_mem', 'new_m_w_q', 'new_m_w_kv', 'new_m_w_xo', 'new_m_norm_ffn', 'new_m_w_gu', 'new_m_w_down', 'new_m_norm_final', 'new_v_norm_mix', 'new_v_w_in', 'new_v_b_gate', 'new_v_conv_w', 'new_v_conv_b', 'new_v_conv_ln_g', 'new_v_conv_ln_b', 'new_v_w_conv_out', 'new_v_sgu_ln_g', 'new_v_sgu_ln_b', 'new_v_sgu_w', 'new_v_sgu_b', 'new_v_w_sgu_out', 'new_v_w_mix_out', 'new_v_norm_xattn', 'new_v_norm_mem', 'new_v_w_q', 'new_v_w_kv', 'new_v_w_xo', 'new_v_norm_ffn', 'new_v_w_gu', 'new_v_w_down', 'new_v_norm_final']
TWIN_LEAF_KINDS = {'loss': 'loss', 'grad_x': 'grad_x', 'grad_norm_mix': 'grad_w', 'grad_w_in': 'grad_w', 'grad_b_gate': 'grad_w', 'grad_conv_w': 'grad_w', 'grad_conv_b': 'grad_w', 'grad_conv_ln_g': 'grad_w', 'grad_conv_ln_b': 'grad_w', 'grad_w_conv_out': 'grad_w', 'grad_sgu_ln_g': 'grad_w', 'grad_sgu_ln_b': 'grad_w', 'grad_sgu_w': 'grad_w', 'grad_sgu_b': 'grad_w', 'grad_w_sgu_out': 'grad_w', 'grad_w_mix_out': 'grad_w', 'grad_norm_xattn': 'grad_w', 'grad_norm_mem': 'grad_w', 'grad_w_q': 'grad_w', 'grad_w_kv': 'grad_w', 'grad_w_xo': 'grad_w', 'grad_norm_ffn': 'grad_w', 'grad_w_gu': 'grad_w', 'grad_w_down': 'grad_w', 'grad_norm_final': 'grad_w', 'delta_norm_mix': 'delta_w', 'delta_w_in': 'delta_w', 'delta_b_gate': 'delta_w', 'delta_conv_w': 'delta_w', 'delta_conv_b': 'delta_w', 'delta_conv_ln_g': 'delta_w', 'delta_conv_ln_b': 'delta_w', 'delta_w_conv_out': 'delta_w', 'delta_sgu_ln_g': 'delta_w', 'delta_sgu_ln_b': 'delta_w', 'delta_sgu_w': 'delta_w', 'delta_sgu_b': 'delta_w', 'delta_w_sgu_out': 'delta_w', 'delta_w_mix_out': 'delta_w', 'delta_norm_xattn': 'delta_w', 'delta_norm_mem': 'delta_w', 'delta_w_q': 'delta_w', 'delta_w_kv': 'delta_w', 'delta_w_xo': 'delta_w', 'delta_norm_ffn': 'delta_w', 'delta_w_gu': 'delta_w', 'delta_w_down': 'delta_w', 'delta_norm_final': 'delta_w', 'new_m_norm_mix': 'new_m', 'new_m_w_in': 'new_m', 'new_m_b_gate': 'new_m', 'new_m_conv_w': 'new_m', 'new_m_conv_b': 'new_m', 'new_m_conv_ln_g': 'new_m', 'new_m_conv_ln_b': 'new_m', 'new_m_w_conv_out': 'new_m', 'new_m_sgu_ln_g': 'new_m', 'new_m_sgu_ln_b': 'new_m', 'new_m_sgu_w': 'new_m', 'new_m_sgu_b': 'new_m', 'new_m_w_sgu_out': 'new_m', 'new_m_w_mix_out': 'new_m', 'new_m_norm_xattn': 'new_m', 'new_m_norm_mem': 'new_m', 'new_m_w_q': 'new_m', 'new_m_w_kv': 'new_m', 'new_m_w_xo': 'new_m', 'new_m_norm_ffn': 'new_m', 'new_m_w_gu': 'new_m', 'new_m_w_down': 'new_m', 'new_m_norm_final': 'new_m', 'new_v_norm_mix': 'new_v', 'new_v_w_in': 'new_v', 'new_v_b_gate': 'new_v', 'new_v_conv_w': 'new_v', 'new_v_conv_b': 'new_v', 'new_v_conv_ln_g': 'new_v', 'new_v_conv_ln_b': 'new_v', 'new_v_w_conv_out': 'new_v', 'new_v_sgu_ln_g': 'new_v', 'new_v_sgu_ln_b': 'new_v', 'new_v_sgu_w': 'new_v', 'new_v_sgu_b': 'new_v', 'new_v_w_sgu_out': 'new_v', 'new_v_w_mix_out': 'new_v', 'new_v_norm_xattn': 'new_v', 'new_v_norm_mem': 'new_v', 'new_v_w_q': 'new_v', 'new_v_w_kv': 'new_v', 'new_v_w_xo': 'new_v', 'new_v_norm_ffn': 'new_v', 'new_v_w_gu': 'new_v', 'new_v_w_down': 'new_v', 'new_v_norm_final': 'new_v'}


def _forward(args):
    return _fwd_reference(*[args[k] for k in FWD_PARAMS])


def _output_shape():
    out = _jax.eval_shape(lambda: _forward(_fwd_setup_inputs(0)))
    return out.shape, out.dtype

N_MICROBATCH = 1
ADAM_LR = 0.001
ADAM_B1 = 0.9
ADAM_B2 = 0.999
ADAM_EPS = 1e-08
ADAM_WD = 0.01
ADAM_STEP = 10
PER_EXAMPLE_BATCH_AXIS = {'x': 0, 'mem': 0, 'loss_target': 0}
SHARED_INPUTS = []
_WEIGHT_DTYPES = {'norm_mix': _jnp.float32, 'w_in': _jnp.float32, 'b_gate': _jnp.float32, 'conv_w': _jnp.float32, 'conv_b': _jnp.float32, 'conv_ln_g': _jnp.float32, 'conv_ln_b': _jnp.float32, 'w_conv_out': _jnp.float32, 'sgu_ln_g': _jnp.float32, 'sgu_ln_b': _jnp.float32, 'sgu_w': _jnp.float32, 'sgu_b': _jnp.float32, 'w_sgu_out': _jnp.float32, 'w_mix_out': _jnp.float32, 'norm_xattn': _jnp.float32, 'norm_mem': _jnp.float32, 'w_q': _jnp.float32, 'w_kv': _jnp.float32, 'w_xo': _jnp.float32, 'norm_ffn': _jnp.float32, 'w_gu': _jnp.float32, 'w_down': _jnp.float32, 'norm_final': _jnp.float32}
MOMENT_SCALE = {'norm_mix': 1.592276e-01, 'w_in': 6.769307e-02, 'b_gate': 4.154733e-02, 'conv_w': 8.429042e-02, 'conv_b': 1.728083e-01, 'conv_ln_g': 9.728579e-02, 'conv_ln_b': 8.991834e-02, 'w_conv_out': 8.071084e-02, 'sgu_ln_g': 6.136825e-02, 'sgu_ln_b': 6.193188e-02, 'sgu_w': 6.130681e-02, 'sgu_b': 8.935798e-02, 'w_sgu_out': 1.063776e-01, 'w_mix_out': 1.343661e-01, 'norm_xattn': 2.860251e-02, 'norm_mem': 4.310456e-02, 'w_q': 2.655555e-02, 'w_kv': 2.624570e-02, 'w_xo': 2.628036e-02, 'norm_ffn': 1.679091e-01, 'w_gu': 7.147149e-02, 'w_down': 1.163147e-01, 'norm_final': 6.405989e+01}


def _to_microbatches(a, axis):
    t = _jnp.moveaxis(a, axis, 0)
    t = t.reshape((N_MICROBATCH, t.shape[0] // N_MICROBATCH) + t.shape[1:])
    return _jnp.moveaxis(t, 1, axis + 1)


def setup_inputs(seed: int = 0) -> dict:
    inp = _fwd_setup_inputs(seed)
    key = _jax.random.fold_in(_jax.random.key(seed), 7919)
    shape, _ = _output_shape()
    out = dict(inp)
    out["loss_target"] = _jax.random.normal(_jax.random.fold_in(key, 0), shape, _jnp.float32)
    for i, name in enumerate(TWIN_WEIGHTS):
        w = inp[name].astype(_jnp.float32)
        if MOMENT_SCALE is None:
            s = _jnp.sqrt(_jnp.mean(_jnp.square(w)) + 1e-30)
        else:
            s = MOMENT_SCALE[name]
        km, kv = _jax.random.split(_jax.random.fold_in(key, i + 1))
        out[name] = w
        out["m_" + name] = s * _jax.random.normal(km, w.shape, _jnp.float32)
        out["v_" + name] = (s * s) * _jax.random.uniform(kv, w.shape, _jnp.float32, 0.5, 1.5)
    if N_MICROBATCH > 1:
        for name, axis in PER_EXAMPLE_BATCH_AXIS.items():
            out[name] = _to_microbatches(out[name], axis)
    return {'x': out['x'], 'mem': out['mem'], 'norm_mix': out['norm_mix'], 'w_in': out['w_in'], 'b_gate': out['b_gate'], 'conv_w': out['conv_w'], 'conv_b': out['conv_b'], 'conv_ln_g': out['conv_ln_g'], 'conv_ln_b': out['conv_ln_b'], 'w_conv_out': out['w_conv_out'], 'sgu_ln_g': out['sgu_ln_g'], 'sgu_ln_b': out['sgu_ln_b'], 'sgu_w': out['sgu_w'], 'sgu_b': out['sgu_b'], 'w_sgu_out': out['w_sgu_out'], 'w_mix_out': out['w_mix_out'], 'norm_xattn': out['norm_xattn'], 'norm_mem': out['norm_mem'], 'w_q': out['w_q'], 'w_kv': out['w_kv'], 'w_xo': out['w_xo'], 'norm_ffn': out['norm_ffn'], 'w_gu': out['w_gu'], 'w_down': out['w_down'], 'norm_final': out['norm_final'], 'loss_target': out['loss_target'], 'm_norm_mix': out['m_norm_mix'], 'm_w_in': out['m_w_in'], 'm_b_gate': out['m_b_gate'], 'm_conv_w': out['m_conv_w'], 'm_conv_b': out['m_conv_b'], 'm_conv_ln_g': out['m_conv_ln_g'], 'm_conv_ln_b': out['m_conv_ln_b'], 'm_w_conv_out': out['m_w_conv_out'], 'm_sgu_ln_g': out['m_sgu_ln_g'], 'm_sgu_ln_b': out['m_sgu_ln_b'], 'm_sgu_w': out['m_sgu_w'], 'm_sgu_b': out['m_sgu_b'], 'm_w_sgu_out': out['m_w_sgu_out'], 'm_w_mix_out': out['m_w_mix_out'], 'm_norm_xattn': out['m_norm_xattn'], 'm_norm_mem': out['m_norm_mem'], 'm_w_q': out['m_w_q'], 'm_w_kv': out['m_w_kv'], 'm_w_xo': out['m_w_xo'], 'm_norm_ffn': out['m_norm_ffn'], 'm_w_gu': out['m_w_gu'], 'm_w_down': out['m_w_down'], 'm_norm_final': out['m_norm_final'], 'v_norm_mix': out['v_norm_mix'], 'v_w_in': out['v_w_in'], 'v_b_gate': out['v_b_gate'], 'v_conv_w': out['v_conv_w'], 'v_conv_b': out['v_conv_b'], 'v_conv_ln_g': out['v_conv_ln_g'], 'v_conv_ln_b': out['v_conv_ln_b'], 'v_w_conv_out': out['v_w_conv_out'], 'v_sgu_ln_g': out['v_sgu_ln_g'], 'v_sgu_ln_b': out['v_sgu_ln_b'], 'v_sgu_w': out['v_sgu_w'], 'v_sgu_b': out['v_sgu_b'], 'v_w_sgu_out': out['v_w_sgu_out'], 'v_w_mix_out': out['v_w_mix_out'], 'v_norm_xattn': out['v_norm_xattn'], 'v_norm_mem': out['v_norm_mem'], 'v_w_q': out['v_w_q'], 'v_w_kv': out['v_w_kv'], 'v_w_xo': out['v_w_xo'], 'v_norm_ffn': out['v_norm_ffn'], 'v_w_gu': out['v_w_gu'], 'v_w_down': out['v_w_down'], 'v_norm_final': out['v_norm_final']}


def _loss(weights, diff, rest, loss_target):
    with _jax.named_scope("forward"):
        args = {**rest, TWIN_DIFF_INPUT: diff, **{k: w.astype(_WEIGHT_DTYPES[k]) for k, w in weights.items()}}
        y = _forward(args)
    with _jax.named_scope("loss_head"):
        err = _jnp.square(y.astype(_jnp.float32) - loss_target)
        return 0.5 * _jnp.sum(_jnp.mean(err, axis=-1)) if err.ndim else 0.5 * err


def _adamw(w, g, m, v):
    m = ADAM_B1 * m + (1.0 - ADAM_B1) * g
    v = ADAM_B2 * v + (1.0 - ADAM_B2) * _jnp.square(g)
    m_hat = m / (1.0 - ADAM_B1 ** ADAM_STEP)
    v_hat = v / (1.0 - ADAM_B2 ** ADAM_STEP)
    delta = -ADAM_LR * (m_hat / (_jnp.sqrt(v_hat) + ADAM_EPS) + ADAM_WD * w)
    return delta, m, v


def reference(x, mem, norm_mix, w_in, b_gate, conv_w, conv_b, conv_ln_g, conv_ln_b, w_conv_out, sgu_ln_g, sgu_ln_b, sgu_w, sgu_b, w_sgu_out, w_mix_out, norm_xattn, norm_mem, w_q, w_kv, w_xo, norm_ffn, w_gu, w_down, norm_final, loss_target, m_norm_mix, m_w_in, m_b_gate, m_conv_w, m_conv_b, m_conv_ln_g, m_conv_ln_b, m_w_conv_out, m_sgu_ln_g, m_sgu_ln_b, m_sgu_w, m_sgu_b, m_w_sgu_out, m_w_mix_out, m_norm_xattn, m_norm_mem, m_w_q, m_w_kv, m_w_xo, m_norm_ffn, m_w_gu, m_w_down, m_norm_final, v_norm_mix, v_w_in, v_b_gate, v_conv_w, v_conv_b, v_conv_ln_g, v_conv_ln_b, v_w_conv_out, v_sgu_ln_g, v_sgu_ln_b, v_sgu_w, v_sgu_b, v_w_sgu_out, v_w_mix_out, v_norm_xattn, v_norm_mem, v_w_q, v_w_kv, v_w_xo, v_norm_ffn, v_w_gu, v_w_down, v_norm_final):
    given = dict(x=x, mem=mem, norm_mix=norm_mix, w_in=w_in, b_gate=b_gate, conv_w=conv_w, conv_b=conv_b, conv_ln_g=conv_ln_g, conv_ln_b=conv_ln_b, w_conv_out=w_conv_out, sgu_ln_g=sgu_ln_g, sgu_ln_b=sgu_ln_b, sgu_w=sgu_w, sgu_b=sgu_b, w_sgu_out=w_sgu_out, w_mix_out=w_mix_out, norm_xattn=norm_xattn, norm_mem=norm_mem, w_q=w_q, w_kv=w_kv, w_xo=w_xo, norm_ffn=norm_ffn, w_gu=w_gu, w_down=w_down, norm_final=norm_final, loss_target=loss_target, m_norm_mix=m_norm_mix, m_w_in=m_w_in, m_b_gate=m_b_gate, m_conv_w=m_conv_w, m_conv_b=m_conv_b, m_conv_ln_g=m_conv_ln_g, m_conv_ln_b=m_conv_ln_b, m_w_conv_out=m_w_conv_out, m_sgu_ln_g=m_sgu_ln_g, m_sgu_ln_b=m_sgu_ln_b, m_sgu_w=m_sgu_w, m_sgu_b=m_sgu_b, m_w_sgu_out=m_w_sgu_out, m_w_mix_out=m_w_mix_out, m_norm_xattn=m_norm_xattn, m_norm_mem=m_norm_mem, m_w_q=m_w_q, m_w_kv=m_w_kv, m_w_xo=m_w_xo, m_norm_ffn=m_norm_ffn, m_w_gu=m_w_gu, m_w_down=m_w_down, m_norm_final=m_norm_final, v_norm_mix=v_norm_mix, v_w_in=v_w_in, v_b_gate=v_b_gate, v_conv_w=v_conv_w, v_conv_b=v_conv_b, v_conv_ln_g=v_conv_ln_g, v_conv_ln_b=v_conv_ln_b, v_w_conv_out=v_w_conv_out, v_sgu_ln_g=v_sgu_ln_g, v_sgu_ln_b=v_sgu_ln_b, v_sgu_w=v_sgu_w, v_sgu_b=v_sgu_b, v_w_sgu_out=v_w_sgu_out, v_w_mix_out=v_w_mix_out, v_norm_xattn=v_norm_xattn, v_norm_mem=v_norm_mem, v_w_q=v_w_q, v_w_kv=v_w_kv, v_w_xo=v_w_xo, v_norm_ffn=v_norm_ffn, v_w_gu=v_w_gu, v_w_down=v_w_down, v_norm_final=v_norm_final)
    weights = {n: given[n] for n in TWIN_WEIGHTS}
    shared = {n: given[n] for n in SHARED_INPUTS}
    per_example = {n: given[n] for n in ['x', 'mem']}
    grad_fn = _jax.value_and_grad(_loss, argnums=(0, 1))

    def one_microbatch(ex, loss_target):
        ex = dict(ex)
        diff = ex.pop(TWIN_DIFF_INPUT)
        return grad_fn(weights, diff, {**shared, **ex}, loss_target)

    if N_MICROBATCH == 1:
        loss, (grad_w, grad_x) = one_microbatch(per_example, given["loss_target"])
    else:
        def body(carry, xs):
            loss_sum, grad_sum = carry
            l_k, (gw_k, gx_k) = one_microbatch(xs[0], xs[1])
            with _jax.named_scope("update"):
                return (loss_sum + l_k, _jax.tree.map(_jnp.add, grad_sum, gw_k)), gx_k

        init = (_jnp.zeros((), _jnp.float32), _jax.tree.map(_jnp.zeros_like, weights))
        (loss, grad_w), grad_x = _jax.lax.scan(body, init, (per_example, given["loss_target"]))
    with _jax.named_scope("update"):
        delta_w, new_m, new_v = {}, {}, {}
        for n in TWIN_WEIGHTS:
            delta_w[n], new_m[n], new_v[n] = _adamw(weights[n], grad_w[n], given["m_" + n], given["v_" + n])
    return (loss, grad_x, *[grad_w[n] for n in TWIN_WEIGHTS], *[delta_w[n] for n in TWIN_WEIGHTS],
            *[new_m[n] for n in TWIN_WEIGHTS], *[new_v[n] for n in TWIN_WEIGHTS])
```

```python
import functools

import jax
import jax.numpy as jnp
from jax import lax
from jax.experimental import pallas as pl
from jax.experimental.pallas import tpu as pltpu

F32 = jnp.float32
BF16 = jnp.bfloat16
RMS_EPS = 1e-6
LN_EPS = 1e-5
CONV_WIDTH = 31
CONV_HALO = 32
CONV_ROWS = 32
CONV_COLS = 256
SGU_CHUNK = 128
SGU_GROUPS = 8
HEADS = 4
N_DEV = 8
ADAM_LR, ADAM_B1, ADAM_B2, ADAM_EPS, ADAM_WD, ADAM_STEP = 0.001, 0.9, 0.999, 1e-08, 0.01, 10
VMEM_LIMIT = 56 * 1024 * 1024
TOKEN_TILE = 256
MESH_ID = pl.DeviceIdType.MESH

_GELU_K = 0.7978845608028654
_GELU_C = 0.044715


def _cparams(sem=None):
    return pltpu.CompilerParams(dimension_semantics=sem, vmem_limit_bytes=VMEM_LIMIT)


def _sigmoid(v):
    return 1.0 / (1.0 + jnp.exp(-v))


def _gelu(v):
    return 0.5 * v * (1.0 + jnp.tanh(_GELU_K * (v + _GELU_C * v * v * v)))


def _gelu_grad(v):
    th = jnp.tanh(_GELU_K * (v + _GELU_C * v * v * v))
    return 0.5 * (1.0 + th) + 0.5 * v * (1.0 - th * th) * _GELU_K * (1.0 + 3.0 * _GELU_C * v * v)


def _dot(a, b, dims):
    return lax.dot_general(a, b, (dims, ((), ())), preferred_element_type=F32)


_NN = ((1,), (0,))
_NT = ((1,), (1,))
_TN = ((0,), (0,))


def _matmul(a, b, *, mode, out_dtype, name, tm=512, tn=512, tk=512, residual=None, rms_gain=None):
    if mode == "nn":
        (m, k), (_, n) = a.shape, b.shape
    elif mode == "nt":
        (m, k), (n, _) = a.shape, b.shape
    else:
        (k, m), (_, n) = a.shape, b.shape
    tm, tn, tk = min(tm, m), min(tn, n), min(tk, k)
    assert m % tm == 0 and n % tn == 0 and k % tk == 0, (name, a.shape, b.shape, tm, tn, tk)
    nk = k // tk
    dims = {"nn": _NN, "nt": _NT, "tn": _TN}[mode]
    if rms_gain is not None:
        assert tn == n

    def body(*refs):
        refs = list(refs)
        a_ref, b_ref = refs[:2]
        pos = 2
        r_ref = g_ref = None
        if residual is not None:
            r_ref = refs[pos]
            pos += 1
        if rms_gain is not None:
            g_ref = refs[pos]
            pos += 1
        o_ref = refs[pos]
        pos += 1
        h_ref = None
        if rms_gain is not None:
            h_ref = refs[pos]
            pos += 1
        acc_ref = refs[pos] if nk > 1 else None
        part = _dot(a_ref[...].astype(BF16), b_ref[...].astype(BF16), dims)

        def finish(res):
            if r_ref is not None:
                res = res + r_ref[...].astype(F32)
            o_ref[...] = res.astype(out_dtype)
            if h_ref is not None:
                r = lax.rsqrt(jnp.mean(res * res, axis=-1, keepdims=True) + RMS_EPS)
                h_ref[...] = (res * r * g_ref[...]).astype(BF16)

        if nk == 1:
            finish(part)
        else:
            kk = pl.program_id(2)

            @pl.when(kk == 0)
            def _():
                acc_ref[...] = part

            @pl.when(kk > 0)
            def _():
                acc_ref[...] += part

            @pl.when(kk == nk - 1)
            def _():
                finish(acc_ref[...])

    if mode == "nn":
        a_spec = pl.BlockSpec((tm, tk), lambda i, j, kk: (i, kk))
        b_spec = pl.BlockSpec((tk, tn), lambda i, j, kk: (kk, j))
    elif mode == "nt":
        a_spec = pl.BlockSpec((tm, tk), lambda i, j, kk: (i, kk))
        b_spec = pl.BlockSpec((tn, tk), lambda i, j, kk: (j, kk))
    else:
        a_spec = pl.BlockSpec((tk, tm), lambda i, j, kk: (kk, i))
        b_spec = pl.BlockSpec((tk, tn), lambda i, j, kk: (kk, j))
    o_spec = pl.BlockSpec((tm, tn), lambda i, j, kk: (i, j))
    in_specs, args = [a_spec, b_spec], [a, b]
    if residual is not None:
        in_specs.append(o_spec)
        args.append(residual)
    out_shape, out_specs = [jax.ShapeDtypeStruct((m, n), out_dtype)], [o_spec]
    if rms_gain is not None:
        in_specs.append(pl.BlockSpec((1, n), lambda i, j, kk: (0, 0)))
        args.append(rms_gain)
        out_shape.append(jax.ShapeDtypeStruct((m, n), BF16))
        out_specs.append(o_spec)
    res = pl.pallas_call(
        body, name=name, grid=(m // tm, n // tn, nk), in_specs=in_specs, out_specs=out_specs, out_shape=out_shape,
        scratch_shapes=[pltpu.VMEM((tm, tn), F32)] if nk > 1 else [],
        compiler_params=_cparams(("parallel", "parallel", "arbitrary")),
    )(*args)
    return res if rms_gain is not None else res[0]


def _row_spec(tt, cols, col_block=0):
    return pl.BlockSpec((tt, cols), lambda i: (i, col_block))


def _const_spec(shape):
    return pl.BlockSpec(shape, lambda *_: (0,) * len(shape))


def _rms_fwd(x, gain, *, name):
    t, d = x.shape
    tt = min(TOKEN_TILE, t)

    def body(x_ref, g_ref, h_ref):
        xv = x_ref[...]
        r = lax.rsqrt(jnp.mean(xv * xv, axis=-1, keepdims=True) + RMS_EPS)
        h_ref[...] = (xv * r * g_ref[...]).astype(BF16)

    return pl.pallas_call(
        body, name=name, grid=(t // tt,), in_specs=[_row_spec(tt, d), _const_spec((1, d))],
        out_specs=_row_spec(tt, d), out_shape=jax.ShapeDtypeStruct((t, d), BF16),
        compiler_params=_cparams(("parallel",)))(x, gain)


def _rms_bwd(dres, dh, x, gain, *, name, need_dx=True):
    t, d = x.shape
    tt = min(TOKEN_TILE, t)

    def body(*refs):
        if need_dx:
            dres_ref, dh_ref, x_ref, g_ref, dx_ref, dg_ref = refs
        else:
            dh_ref, x_ref, g_ref, dg_ref = refs

        @pl.when(pl.program_id(0) == 0)
        def _():
            dg_ref[...] = jnp.zeros_like(dg_ref)

        xv = x_ref[...]
        dhv = dh_ref[...].astype(F32)
        r = lax.rsqrt(jnp.mean(xv * xv, axis=-1, keepdims=True) + RMS_EPS)
        xhat = xv * r
        dg_ref[...] += jnp.sum(dhv * xhat, axis=0, keepdims=True)
        if need_dx:
            dxh = dhv * g_ref[...]
            dx_ref[...] = dres_ref[...] + r * (dxh - xhat * jnp.mean(dxh * xhat, axis=-1, keepdims=True))

    rs = _row_spec(tt, d)
    if need_dx:
        in_specs, args = [rs, rs, rs, _const_spec((1, d))], (dres, dh, x, gain)
        out_specs = [rs, _const_spec((1, d))]
        out_shape = [jax.ShapeDtypeStruct((t, d), F32), jax.ShapeDtypeStruct((1, d), F32)]
    else:
        in_specs, args = [rs, rs, _const_spec((1, d))], (dh, x, gain)
        out_specs = [_const_spec((1, d))]
        out_shape = [jax.ShapeDtypeStruct((1, d), F32)]
    res = pl.pallas_call(body, name=name, grid=(t // tt,), in_specs=in_specs, out_specs=out_specs, out_shape=out_shape,
                         compiler_params=_cparams(("arbitrary",)))(*args)
    return res if need_dx else res[0]


def _final_loss(x3, target, gain, *, name):
    t, d = x3.shape
    tt = min(TOKEN_TILE, t)

    def body(x_ref, t_ref, g_ref, loss_ref, dx_ref, dg_ref):
        @pl.when(pl.program_id(0) == 0)
        def _():
            loss_ref[...] = jnp.zeros_like(loss_ref)
            dg_ref[...] = jnp.zeros_like(dg_ref)

        xv = x_ref[...]
        g = g_ref[...]
        r = lax.rsqrt(jnp.mean(xv * xv, axis=-1, keepdims=True) + RMS_EPS)
        xhat = xv * r
        err = xhat * g - t_ref[...]
        loss_ref[...] += 0.5 * jnp.sum(jnp.mean(err * err, axis=-1, keepdims=True), axis=0, keepdims=True)
        dy = err * (1.0 / d)
        dg_ref[...] += jnp.sum(dy * xhat, axis=0, keepdims=True)
        dxh = dy * g
        dx_ref[...] = r * (dxh - xhat * jnp.mean(dxh * xhat, axis=-1, keepdims=True))

    rs = _row_spec(tt, d)
    return pl.pallas_call(
        body, name=name, grid=(t // tt,), in_specs=[rs, rs, _const_spec((1, d))],
        out_specs=[_const_spec((1, 1)), rs, _const_spec((1, d))],
        out_shape=[jax.ShapeDtypeStruct((1, 1), F32), jax.ShapeDtypeStruct((t, d), F32), jax.ShapeDtypeStruct((1, d), F32)],
        compiler_params=_cparams(("arbitrary",)))(x3, target, gain)


def _conv_chunks(buf_ref, tt, fn):
    d = buf_ref.shape[1]

    def row_body(r, carry):
        r0 = pl.multiple_of(r * CONV_ROWS, CONV_ROWS)
        for cc in range(d // CONV_COLS):
            cs = slice(cc * CONV_COLS, (cc + 1) * CONV_COLS)
            fn(r0, cs, buf_ref[pl.ds(r0, 2 * CONV_ROWS), cs])
        return carry

    lax.fori_loop(0, tt // CONV_ROWS, row_body, 0)


def _conv_specs(bl, s, tt, d, col_a, col_g):
    nj = s // tt
    per = tt // CONV_HALO
    main_a = pl.BlockSpec((tt, d), lambda b, j: (b * nj + j, col_a))
    main_g = pl.BlockSpec((tt, d), lambda b, j: (b * nj + j, col_g))
    prev = lambda b, j: jnp.maximum((b * nj + j) * per - 1, 0)
    halo_a = pl.BlockSpec((CONV_HALO, d), lambda b, j: (prev(b, j), col_a))
    halo_g = pl.BlockSpec((CONV_HALO, d), lambda b, j: (prev(b, j), col_g))
    return main_a, main_g, halo_a, halo_g


def _fill_glu(buf_ref, a_ref, g_ref, ha_ref, hg_ref, tt):
    first = pl.program_id(1) == 0
    ha = ha_ref[...].astype(F32)
    hg = hg_ref[...].astype(F32)
    buf_ref[pl.ds(0, CONV_HALO), :] = jnp.where(first, 0.0, ha * _sigmoid(hg))
    av = a_ref[...].astype(F32)
    gv = g_ref[...].astype(F32)
    buf_ref[pl.ds(CONV_HALO, tt), :] = av * _sigmoid(gv)


def _conv_fwd(p, conv_w, conv_b, ln_g, ln_b, *, bl, s, name):
    t = p.shape[0]
    d = conv_w.shape[1]
    tt = min(TOKEN_TILE, s)
    off = CONV_HALO - (CONV_WIDTH - 1)

    def body(a_ref, g_ref, ha_ref, hg_ref, w_ref, b_ref, lg_ref, lb_ref, c_ref, act_ref, buf_ref, cbuf_ref):
        _fill_glu(buf_ref, a_ref, g_ref, ha_ref, hg_ref, tt)

        def chunk(r0, cs, win):
            acc = jnp.zeros((CONV_ROWS, CONV_COLS), F32)
            for k in range(CONV_WIDTH):
                acc = acc + win[off + k:off + k + CONV_ROWS, :] * w_ref[k:k + 1, cs]
            cbuf_ref[pl.ds(r0, CONV_ROWS), cs] = acc + b_ref[:, cs]

        _conv_chunks(buf_ref, tt, chunk)
        cv = cbuf_ref[...]
        c_ref[...] = cv.astype(BF16)
        mu = jnp.mean(cv, axis=-1, keepdims=True)
        dv = cv - mu
        rstd = lax.rsqrt(jnp.mean(dv * dv, axis=-1, keepdims=True) + LN_EPS)
        aln = dv * rstd * lg_ref[...] + lb_ref[...]
        act_ref[...] = (aln * _sigmoid(aln)).astype(BF16)

    main_a, main_g, halo_a, halo_g = _conv_specs(bl, s, tt, d, 0, 1)
    out_spec = pl.BlockSpec((tt, d), lambda b, j: (b * (s // tt) + j, 0))
    return pl.pallas_call(
        body, name=name, grid=(bl, s // tt),
        in_specs=[main_a, main_g, halo_a, halo_g, _const_spec((CONV_HALO, d)), _const_spec((1, d)), _const_spec((1, d)),
                  _const_spec((1, d))],
        out_specs=[out_spec, out_spec],
        out_shape=[jax.ShapeDtypeStruct((t, d), BF16), jax.ShapeDtypeStruct((t, d), BF16)],
        scratch_shapes=[pltpu.VMEM((tt + CONV_HALO, d), F32), pltpu.VMEM((tt, d), F32)],
        compiler_params=_cparams(("parallel", "parallel")))(p, p, p, p, conv_w, conv_b, ln_g, ln_b)


def _conv_ln_bwd(dact, c, ln_g, ln_b, *, name):
    t, d = c.shape
    tt = min(TOKEN_TILE, t)

    def body(da_ref, c_ref, lg_ref, lb_ref, dc_ref, dlg_ref, dlb_ref):
        @pl.when(pl.program_id(0) == 0)
        def _():
            dlg_ref[...] = jnp.zeros_like(dlg_ref)
            dlb_ref[...] = jnp.zeros_like(dlb_ref)

        cv = c_ref[...].astype(F32)
        g = lg_ref[...]
        mu = jnp.mean(cv, axis=-1, keepdims=True)
        dv = cv - mu
        rstd = lax.rsqrt(jnp.mean(dv * dv, axis=-1, keepdims=True) + LN_EPS)
        chat = dv * rstd
        aln = chat * g + lb_ref[...]
        sg = _sigmoid(aln)
        daln = da_ref[...].astype(F32) * (sg * (1.0 + aln * (1.0 - sg)))
        dlb_ref[...] += jnp.sum(daln, axis=0, keepdims=True)
        dlg_ref[...] += jnp.sum(daln * chat, axis=0, keepdims=True)
        dchat = daln * g
        dc = rstd * (dchat - jnp.mean(dchat, axis=-1, keepdims=True)
                     - chat * jnp.mean(dchat * chat, axis=-1, keepdims=True))
        dc_ref[...] = dc.astype(BF16)

    rs = _row_spec(tt, d)
    cs = _const_spec((1, d))
    return pl.pallas_call(
        body, name=name, grid=(t // tt,), in_specs=[rs, rs, cs, cs], out_specs=[rs, cs, cs],
        out_shape=[jax.ShapeDtypeStruct((t, d), BF16), jax.ShapeDtypeStruct((1, d), F32), jax.ShapeDtypeStruct((1, d), F32)],
        compiler_params=_cparams(("arbitrary",)))(dact, c, ln_g, ln_b)


def _conv_bwd(dp, dc, p, conv_w, *, bl, s, name):
    t = p.shape[0]
    d = conv_w.shape[1]
    tt = min(TOKEN_TILE, s)
    nj = s // tt
    per = tt // CONV_HALO
    off = CONV_HALO - (CONV_WIDTH - 1)
    last_blk = t // CONV_HALO - 1

    def body(dp_in, dc_ref, dcn_ref, a_ref, g_ref, ha_ref, hg_ref, w_ref, dp_ref, dw_ref, db_ref,
             gbuf_ref, dbuf_ref, dglu_ref, acc_ref):
        del dp_in
        b, j = pl.program_id(0), pl.program_id(1)
        start = jnp.logical_and(b == 0, j == 0)
        end = jnp.logical_and(b == bl - 1, j == nj - 1)

        @pl.when(start)
        def _():
            acc_ref[...] = jnp.zeros_like(acc_ref)
            db_ref[...] = jnp.zeros_like(db_ref)

        _fill_glu(gbuf_ref, a_ref, g_ref, ha_ref, hg_ref, tt)
        dcv = dc_ref[...].astype(F32)
        dbuf_ref[pl.ds(0, tt), :] = dcv
        dbuf_ref[pl.ds(tt, CONV_HALO), :] = jnp.where(j == nj - 1, 0.0, dcn_ref[...].astype(F32))
        db_ref[...] += jnp.sum(dcv, axis=0, keepdims=True)

        def wgrad(r0, cs, win):
            dcw = dbuf_ref[pl.ds(r0, CONV_ROWS), cs]
            for k in range(CONV_WIDTH):
                prod = dcw * win[off + k:off + k + CONV_ROWS, :]
                part = prod[0:8]
                for q in range(1, CONV_ROWS // 8):
                    part = part + prod[8 * q:8 * q + 8]
                acc_ref[k, :, cs] += part

        _conv_chunks(gbuf_ref, tt, wgrad)

        def xgrad(r0, cs, win):
            acc = jnp.zeros((CONV_ROWS, CONV_COLS), F32)
            for k in range(CONV_WIDTH):
                o = CONV_WIDTH - 1 - k
                acc = acc + win[o:o + CONV_ROWS, :] * w_ref[k:k + 1, cs]
            dglu_ref[pl.ds(r0, CONV_ROWS), cs] = acc

        _conv_chunks(dbuf_ref, tt, xgrad)
        dglu = dglu_ref[...]
        av = a_ref[...].astype(F32)
        sg = _sigmoid(g_ref[...].astype(F32))
        dp_ref[:, 0:d] = (dglu * sg).astype(BF16)
        dp_ref[:, d:2 * d] = (dglu * av * sg * (1.0 - sg)).astype(BF16)

        @pl.when(end)
        def _():
            for k in range(CONV_WIDTH):
                dw_ref[k:k + 1, :] = jnp.sum(acc_ref[k], axis=0, keepdims=True)
            dw_ref[CONV_WIDTH:CONV_HALO, :] = jnp.zeros((CONV_HALO - CONV_WIDTH, d), F32)

    main_a, main_g, halo_a, halo_g = _conv_specs(bl, s, tt, d, 0, 1)
    dc_main = pl.BlockSpec((tt, d), lambda b, j: (b * nj + j, 0))
    dc_next = pl.BlockSpec((CONV_HALO, d), lambda b, j: (jnp.minimum((b * nj + j + 1) * per, last_blk), 0))
    return pl.pallas_call(
        body, name=name, grid=(bl, nj),
        in_specs=[pl.BlockSpec(memory_space=pl.ANY), dc_main, dc_next, main_a, main_g, halo_a, halo_g,
                  _const_spec((CONV_HALO, d))],
        out_specs=[pl.BlockSpec((tt, 2 * d), lambda b, j: (b * nj + j, 0)), _const_spec((CONV_HALO, d)), _const_spec((1, d))],
        out_shape=[jax.ShapeDtypeStruct(dp.shape, dp.dtype), jax.ShapeDtypeStruct((CONV_HALO, d), F32),
                   jax.ShapeDtypeStruct((1, d), F32)],
        scratch_shapes=[pltpu.VMEM((tt + CONV_HALO, d), F32), pltpu.VMEM((tt + CONV_HALO, d), F32),
                        pltpu.VMEM((tt, d), F32), pltpu.VMEM((CONV_HALO, 8, d), F32)],
        input_output_aliases={0: 0},
        compiler_params=_cparams(("arbitrary", "arbitrary")))(dp, dc, dc, p, p, p, p, conv_w)


def _sgu_stats(bv):
    gv = _gelu(bv)
    mu = jnp.mean(gv, axis=-1, keepdims=True)
    dv = gv - mu
    rstd = lax.rsqrt(jnp.mean(dv * dv, axis=-1, keepdims=True) + LN_EPS)
    return dv * rstd, rstd


def _sgu_fwd(p, wm, bias, ln_g, ln_b, *, name):
    t = p.shape[0]
    d = ln_g.shape[1]
    tt = SGU_CHUNK
    gd = d // SGU_GROUPS

    def body(u_ref, v_ref, wm_ref, bias_ref, lg_ref, lb_ref, sg_ref, vn_ref):
        u = _gelu(u_ref[...].astype(F32))
        vhat, _ = _sgu_stats(v_ref[...].astype(F32))
        vb = (vhat * lg_ref[...] + lb_ref[...]).astype(BF16)
        vn_ref[...] = vb
        for g in range(SGU_GROUPS):
            gs = slice(g * gd, (g + 1) * gd)
            z = _dot(wm_ref[g], vb[:, gs], _NN) + bias_ref[g]
            sg_ref[:, gs] = (u[:, gs] * z).astype(BF16)

    rs = _row_spec(tt, d)
    return pl.pallas_call(
        body, name=name, grid=(t // tt,),
        in_specs=[_row_spec(tt, d, 2), _row_spec(tt, d, 3), _const_spec(wm.shape), _const_spec(bias.shape),
                  _const_spec((1, d)), _const_spec((1, d))],
        out_specs=[rs, rs], out_shape=[jax.ShapeDtypeStruct((t, d), BF16), jax.ShapeDtypeStruct((t, d), BF16)],
        compiler_params=_cparams(("parallel",)))(p, p, wm, bias, ln_g, ln_b)


def _sgu_bwd(dp, dsg, p, vn, wm, wmt, bias, ln_g, *, name):
    t = p.shape[0]
    d = ln_g.shape[1]
    tt = SGU_CHUNK
    gd = d // SGU_GROUPS
    nsteps = t // tt

    def body(dp_in, dsg_ref, u_ref, v_ref, vn_ref, wm_ref, wmt_ref, bias_ref, lg_ref,
             dp_ref, dw_ref, dbs_ref, dlg_ref, dlb_ref, dz_acc):
        del dp_in
        i = pl.program_id(0)

        @pl.when(i == 0)
        def _():
            dw_ref[...] = jnp.zeros_like(dw_ref)
            dlg_ref[...] = jnp.zeros_like(dlg_ref)
            dlb_ref[...] = jnp.zeros_like(dlb_ref)
            dz_acc[...] = jnp.zeros_like(dz_acc)

        bu = u_ref[...].astype(F32)
        bv = v_ref[...].astype(F32)
        u = _gelu(bu)
        vhat, rstd = _sgu_stats(bv)
        vb = vn_ref[...]
        dsg = dsg_ref[...].astype(F32)
        row = lax.broadcasted_iota(jnp.int32, (tt, tt), 0)
        col = lax.broadcasted_iota(jnp.int32, (tt, tt), 1)
        causal = col <= row
        du_parts, dv_parts = [], []
        for g in range(SGU_GROUPS):
            gs = slice(g * gd, (g + 1) * gd)
            z = _dot(wm_ref[g], vb[:, gs], _NN) + bias_ref[g]
            du_parts.append(dsg[:, gs] * z)
            dz = dsg[:, gs] * u[:, gs]
            dz_acc[:, gs] += dz
            dzb = dz.astype(BF16)
            dw_ref[g] += jnp.where(causal, _dot(dzb, vb[:, gs], _NT), 0.0)
            dv_parts.append(_dot(wmt_ref[g], dzb, _NN))
        du = jnp.concatenate(du_parts, axis=1)
        dv = jnp.concatenate(dv_parts, axis=1)
        dp_ref[:, 0:d] = (du * _gelu_grad(bu)).astype(BF16)
        dlb_ref[...] += jnp.sum(dv, axis=0, keepdims=True)
        dlg_ref[...] += jnp.sum(dv * vhat, axis=0, keepdims=True)
        dvh = dv * lg_ref[...]
        dgv = rstd * (dvh - jnp.mean(dvh, axis=-1, keepdims=True) - vhat * jnp.mean(dvh * vhat, axis=-1, keepdims=True))
        dp_ref[:, d:2 * d] = (dgv * _gelu_grad(bv)).astype(BF16)

        @pl.when(i == nsteps - 1)
        def _():
            ones = jnp.ones((8, gd), F32)
            for g in range(SGU_GROUPS):
                gs = slice(g * gd, (g + 1) * gd)
                tot = lax.dot_general(ones, dz_acc[:, gs], (_NT, ((), ())), preferred_element_type=F32,
                                      precision=lax.Precision.HIGHEST)
                dbs_ref[g:g + 1, :] = tot[0:1, :]

    rs = _row_spec(tt, d)
    c1 = _const_spec((1, d))
    return pl.pallas_call(
        body, name=name, grid=(nsteps,),
        in_specs=[pl.BlockSpec(memory_space=pl.ANY), rs, _row_spec(tt, d, 2), _row_spec(tt, d, 3), rs,
                  _const_spec(wm.shape), _const_spec(wmt.shape), _const_spec(bias.shape), c1],
        out_specs=[pl.BlockSpec((tt, 2 * d), lambda i: (i, 1)), _const_spec(wm.shape), _const_spec((SGU_GROUPS, tt)), c1, c1],
        out_shape=[jax.ShapeDtypeStruct(dp.shape, dp.dtype), jax.ShapeDtypeStruct(wm.shape, F32),
                   jax.ShapeDtypeStruct((SGU_GROUPS, tt), F32), jax.ShapeDtypeStruct((1, d), F32),
                   jax.ShapeDtypeStruct((1, d), F32)],
        scratch_shapes=[pltpu.VMEM((tt, d), F32)],
        input_output_aliases={0: 0},
        compiler_params=_cparams(("arbitrary",)))(dp, dsg, p, p, vn, wm, wmt, bias, ln_g)


def _gates_fwd(p, ya, yb, b_gate, *, name):
    t, d = ya.shape
    tt = min(TOKEN_TILE, t)

    def body(ga_ref, gb_ref, ya_ref, yb_ref, bg_ref, o_ref):
        sa = _sigmoid(ga_ref[...].astype(F32) + bg_ref[0:1, :])
        sb = _sigmoid(gb_ref[...].astype(F32) + bg_ref[1:2, :])
        o_ref[...] = (sa * ya_ref[...].astype(F32) + sb * yb_ref[...].astype(F32)).astype(BF16)

    rs = _row_spec(tt, d)
    return pl.pallas_call(
        body, name=name, grid=(t // tt,),
        in_specs=[_row_spec(tt, d, 4), _row_spec(tt, d, 5), rs, rs, _const_spec(b_gate.shape)],
        out_specs=rs, out_shape=jax.ShapeDtypeStruct((t, d), BF16),
        compiler_params=_cparams(("parallel",)))(p, p, ya, yb, b_gate)


def _gates_bwd(dmerged, p, ya, yb, b_gate, *, name):
    t, d = ya.shape
    tt = min(TOKEN_TILE, t)

    def body(dm_ref, ga_ref, gb_ref, ya_ref, yb_ref, bg_ref, dp_ref, dya_ref, dyb_ref, dbg_ref):
        @pl.when(pl.program_id(0) == 0)
        def _():
            dbg_ref[...] = jnp.zeros_like(dbg_ref)

        dm = dm_ref[...].astype(F32)
        sa = _sigmoid(ga_ref[...].astype(F32) + bg_ref[0:1, :])
        sb = _sigmoid(gb_ref[...].astype(F32) + bg_ref[1:2, :])
        dya_ref[...] = (dm * sa).astype(BF16)
        dyb_ref[...] = (dm * sb).astype(BF16)
        dga = dm * ya_ref[...].astype(F32) * sa * (1.0 - sa)
        dgb = dm * yb_ref[...].astype(F32) * sb * (1.0 - sb)
        dp_ref[:, 0:d] = dga.astype(BF16)
        dp_ref[:, d:2 * d] = dgb.astype(BF16)
        dbg_ref[0:1, :] += jnp.sum(dga, axis=0, keepdims=True)
        dbg_ref[1:2, :] += jnp.sum(dgb, axis=0, keepdims=True)

    rs = _row_spec(tt, d)
    return pl.pallas_call(
        body, name=name, grid=(t // tt,),
        in_specs=[rs, _row_spec(tt, d, 4), _row_spec(tt, d, 5), rs, rs, _const_spec(b_gate.shape)],
        out_specs=[pl.BlockSpec((tt, 2 * d), lambda i: (i, 2)), rs, rs, _const_spec((8, d))],
        out_shape=[jax.ShapeDtypeStruct(p.shape, BF16), jax.ShapeDtypeStruct((t, d), BF16),
                   jax.ShapeDtypeStruct((t, d), BF16), jax.ShapeDtypeStruct((8, d), F32)],
        compiler_params=_cparams(("arbitrary",)))(dmerged, p, p, ya, yb, b_gate)


def _softmax_rows(s):
    e = jnp.exp(s - jnp.max(s, axis=-1, keepdims=True))
    return e / jnp.sum(e, axis=-1, keepdims=True)


def _attn_fwd(q, kv, *, bl, s, name):
    t, d = q.shape
    mlen = kv.shape[0] // bl
    hd = d // HEADS
    tq = min(TOKEN_TILE, s)
    nq = s // tq
    scale = hd ** -0.5

    def body(q_ref, kv_ref, o_ref):
        for h in range(HEADS):
            hs = slice(h * hd, (h + 1) * hd)
            vs = slice(d + h * hd, d + (h + 1) * hd)
            pr = _softmax_rows(_dot(q_ref[:, hs], kv_ref[:, hs], _NT) * scale)
            o_ref[:, hs] = _dot(pr.astype(BF16), kv_ref[:, vs], _NN).astype(BF16)

    qs = pl.BlockSpec((tq, d), lambda b, j: (b * nq + j, 0))
    return pl.pallas_call(
        body, name=name, grid=(bl, nq), in_specs=[qs, pl.BlockSpec((mlen, 2 * d), lambda b, j: (b, 0))],
        out_specs=qs, out_shape=jax.ShapeDtypeStruct((t, d), BF16),
        compiler_params=_cparams(("parallel", "parallel")))(q, kv)


def _attn_bwd(q, kv, do, *, bl, s, name):
    t, d = q.shape
    mlen = kv.shape[0] // bl
    hd = d // HEADS
    tq = min(TOKEN_TILE, s)
    nq = s // tq
    scale = hd ** -0.5

    def body(q_ref, kv_ref, do_ref, dq_ref, dkv_ref):
        @pl.when(pl.program_id(1) == 0)
        def _():
            dkv_ref[...] = jnp.zeros_like(dkv_ref)

        for h in range(HEADS):
            hs = slice(h * hd, (h + 1) * hd)
            vs = slice(d + h * hd, d + (h + 1) * hd)
            qh, kh, vh, doh = q_ref[:, hs], kv_ref[:, hs], kv_ref[:, vs], do_ref[:, hs]
            pr = _softmax_rows(_dot(qh, kh, _NT) * scale)
            dpr = _dot(doh, vh, _NT)
            dkv_ref[:, vs] += _dot(pr.astype(BF16), doh, _TN)
            ds = (pr * (dpr - jnp.sum(dpr * pr, axis=-1, keepdims=True)) * scale).astype(BF16)
            dq_ref[:, hs] = _dot(ds, kh, _NN).astype(BF16)
            dkv_ref[:, hs] += _dot(ds, qh, _TN)

    qs = pl.BlockSpec((tq, d), lambda b, j: (b * nq + j, 0))
    ks = pl.BlockSpec((mlen, 2 * d), lambda b, j: (b, 0))
    return pl.pallas_call(
        body, name=name, grid=(bl, nq), in_specs=[qs, ks, qs], out_specs=[qs, ks],
        out_shape=[jax.ShapeDtypeStruct((t, d), BF16), jax.ShapeDtypeStruct(kv.shape, F32)],
        compiler_params=_cparams(("parallel", "arbitrary")))(q, kv, do)


def _swiglu_fwd(gu, *, name):
    t, f2 = gu.shape
    f = f2 // 2
    tt = min(TOKEN_TILE, t)

    def body(gu_ref, o_ref):
        gt = gu_ref[:, 0:f].astype(F32)
        up = gu_ref[:, f:f2].astype(F32)
        o_ref[...] = (gt * _sigmoid(gt) * up).astype(BF16)

    return pl.pallas_call(
        body, name=name, grid=(t // tt,), in_specs=[_row_spec(tt, f2)], out_specs=_row_spec(tt, f),
        out_shape=jax.ShapeDtypeStruct((t, f), BF16), compiler_params=_cparams(("parallel",)))(gu)


def _swiglu_bwd(gu, dact, *, name):
    t, f2 = gu.shape
    f = f2 // 2
    tt = min(TOKEN_TILE, t)

    def body(gu_ref, da_ref, o_ref):
        gt = gu_ref[:, 0:f].astype(F32)
        up = gu_ref[:, f:f2].astype(F32)
        da = da_ref[...].astype(F32)
        sg = _sigmoid(gt)
        o_ref[:, 0:f] = (da * up * sg * (1.0 + gt * (1.0 - sg))).astype(BF16)
        o_ref[:, f:f2] = (da * gt * sg).astype(BF16)

    return pl.pallas_call(
        body, name=name, grid=(t // tt,), in_specs=[_row_spec(tt, f2), _row_spec(tt, f)], out_specs=_row_spec(tt, f2),
        out_shape=jax.ShapeDtypeStruct((t, f2), BF16), compiler_params=_cparams(("parallel",)))(gu, dact)


def _mesh_pos():
    return lax.axis_index("x"), lax.axis_index("y"), lax.axis_index("c")


def _all_gather(arrs, *, name):
    n = len(arrs)
    hbm = pl.BlockSpec(memory_space=pl.ANY)

    def body(*refs):
        ins, outs = refs[:n], refs[n:2 * n]
        send_sems, recv_sems, loc_sems = refs[2 * n:]
        x, y, c = _mesh_pos()
        me, sib = (x, y, c), (x, y, 1 - c)
        chips = [(1 - x, y), (x, 1 - y), (1 - x, 1 - y)]

        def idx(dev):
            return 4 * dev[0] + 2 * dev[1] + dev[2]

        def copy(w, k, block, to, from_input=False):
            return pltpu.make_async_remote_copy(
                src_ref=ins[w] if from_input else outs[w].at[idx(block)], dst_ref=outs[w].at[idx(block)],
                send_sem=send_sems.at[w, k], recv_sem=recv_sems.at[w, k], device_id=to, device_id_type=MESH_ID)

        own = [pltpu.make_async_copy(ins[w], outs[w].at[idx(me)], loc_sems.at[w]) for w in range(n)]
        for cp in own:
            cp.start()
        first = []
        for w in range(n):
            first.append(copy(w, 0, me, sib, True))
            first += [copy(w, 1 + j, me, (*chip, c), True) for j, chip in enumerate(chips)]
        for cp in first:
            cp.start()
        passed = []
        for j, chip in enumerate(chips):
            for w in range(n):
                copy(w, 1 + j, (*chip, c), me).wait_recv()
                fwd = copy(w, 4 + j, (*chip, c), sib)
                fwd.start()
                passed.append(fwd)
        for w in range(n):
            copy(w, 0, sib, me).wait_recv()
            for j, chip in enumerate(chips):
                copy(w, 4 + j, (*chip, 1 - c), me).wait_recv()
        for cp in first + passed:
            cp.wait_send()
        for cp in own:
            cp.wait()

    return pl.pallas_call(
        body, name=name, in_specs=[hbm] * n, out_specs=[hbm] * n,
        out_shape=[jax.ShapeDtypeStruct((N_DEV, *a.shape), a.dtype) for a in arrs],
        scratch_shapes=[pltpu.SemaphoreType.DMA((n, 7)), pltpu.SemaphoreType.DMA((n, 7)), pltpu.SemaphoreType.DMA((n,))],
    )(*arrs)


def _exchange_sibling(grads, *, name):
    n = len(grads)
    hbm = pl.BlockSpec(memory_space=pl.ANY)

    def body(*refs):
        ins, outs = refs[:n], refs[n:2 * n]
        send_sems, recv_sems = refs[2 * n:]
        x, y, c = _mesh_pos()
        cps = [pltpu.make_async_remote_copy(
            src_ref=ins[w].at[:, 1 - c], dst_ref=outs[w], send_sem=send_sems.at[w], recv_sem=recv_sems.at[w],
            device_id=(x, y, 1 - c), device_id_type=MESH_ID) for w in range(n)]
        for cp in cps:
            cp.start()
        for cp in cps:
            cp.wait()

    return pl.pallas_call(
        body, name=name, in_specs=[hbm] * n, out_specs=[hbm] * n,
        out_shape=[jax.ShapeDtypeStruct((g.shape[0], *g.shape[2:]), g.dtype) for g in grads],
        scratch_shapes=[pltpu.SemaphoreType.DMA((n,)), pltpu.SemaphoreType.DMA((n,))],
    )(*grads)


def _exchange_chips(sums, *, name):
    n = len(sums)
    hbm = pl.BlockSpec(memory_space=pl.ANY)

    def body(*refs):
        ins, outs = refs[:n], refs[n:2 * n]
        send_sems, recv_sems = refs[2 * n:]
        x, y, c = _mesh_pos()
        chips = [(1 - x, y), (x, 1 - y), (1 - x, 1 - y)]
        cps = [pltpu.make_async_remote_copy(
            src_ref=ins[w].at[2 * chip[0] + chip[1]], dst_ref=outs[w].at[j], send_sem=send_sems.at[w, j],
            recv_sem=recv_sems.at[w, j], device_id=(*chip, c), device_id_type=MESH_ID)
            for w in range(n) for j, chip in enumerate(chips)]
        for cp in cps:
            cp.start()
        for cp in cps:
            cp.wait()

    return pl.pallas_call(
        body, name=name, in_specs=[hbm] * n, out_specs=[hbm] * n,
        out_shape=[jax.ShapeDtypeStruct((3, *g.shape[1:]), g.dtype) for g in sums],
        scratch_shapes=[pltpu.SemaphoreType.DMA((n, 3)), pltpu.SemaphoreType.DMA((n, 3))],
    )(*sums)


def _row_tile(rows):
    return rows if rows <= 512 else 256


def _add_sibling(g, landed, core, *, name):
    _, _, r, c = g.shape
    tr = _row_tile(r)

    def body(core_ref, g_ref, l_ref, s_ref, sb_ref):
        del core_ref
        tot = g_ref[...] + l_ref[...]
        s_ref[...] = tot
        sb_ref[...] = tot.astype(BF16)

    blk = pl.BlockSpec((None, tr, c), lambda k, i, core_ref: (k, i, 0))
    gs = pltpu.PrefetchScalarGridSpec(
        num_scalar_prefetch=1, grid=(4, r // tr),
        in_specs=[pl.BlockSpec((None, None, tr, c), lambda k, i, core_ref: (k, core_ref[0], i, 0)), blk],
        out_specs=[blk, blk])
    return pl.pallas_call(
        body, name=name, grid_spec=gs,
        out_shape=[jax.ShapeDtypeStruct((4, r, c), F32), jax.ShapeDtypeStruct((4, r, c), BF16)],
        compiler_params=_cparams(("parallel", "parallel")))(core, g, landed)


def _adamw_math(w, g, m, v):
    m2 = ADAM_B1 * m + (1.0 - ADAM_B1) * g
    v2 = ADAM_B2 * v + (1.0 - ADAM_B2) * (g * g)
    m_hat = m2 / (1.0 - ADAM_B1 ** ADAM_STEP)
    v_hat = v2 / (1.0 - ADAM_B2 ** ADAM_STEP)
    delta = -ADAM_LR * (m_hat / (jnp.sqrt(v_hat) + ADAM_EPS) + ADAM_WD * w)
    return delta, m2, v2


def _adamw_shard(own_sum, landed, chip, w, m, v, *, name):
    r, c = w.shape
    tr = _row_tile(r)

    def body(chip_ref, s_ref, l_ref, w_ref, m_ref, v_ref, g_out, d_out, m_out, v_out):
        del chip_ref
        g = s_ref[...] + l_ref[0].astype(F32) + l_ref[1].astype(F32) + l_ref[2].astype(F32)
        delta, m2, v2 = _adamw_math(w_ref[...], g, m_ref[...], v_ref[...])
        g_out[...] = g
        d_out[...] = delta
        m_out[...] = m2
        v_out[...] = v2

    blk = pl.BlockSpec((tr, c), lambda i, chip_ref: (i, 0))
    gs = pltpu.PrefetchScalarGridSpec(
        num_scalar_prefetch=1, grid=(r // tr,),
        in_specs=[pl.BlockSpec((None, tr, c), lambda i, chip_ref: (chip_ref[0], i, 0)),
                  pl.BlockSpec((3, tr, c), lambda i, chip_ref: (0, i, 0)), blk, blk, blk],
        out_specs=[blk] * 4)
    return pl.pallas_call(
        body, name=name, grid_spec=gs, out_shape=[jax.ShapeDtypeStruct((r, c), F32)] * 4,
        compiler_params=_cparams(("parallel",)))(chip, own_sum, landed, w, m, v)


def _adamw_small(parts, dev, w, m, v, *, name, col_block):
    _, r, d = parts.shape
    cols = w.shape[1]

    def body(dev_ref, p_ref, w_ref, m_ref, v_ref, g_out, d_out, m_out, v_out):
        del dev_ref
        g = p_ref[0]
        for k in range(1, N_DEV):
            g = g + p_ref[k]
        delta, m2, v2 = _adamw_math(w_ref[...], g, m_ref[...], v_ref[...])
        g_out[...] = g
        d_out[...] = delta
        m_out[...] = m2
        v_out[...] = v2

    blk = pl.BlockSpec((r, cols), lambda i, dev_ref: (0, 0))
    pidx = (lambda i, dev_ref: (0, 0, dev_ref[0])) if col_block else (lambda i, dev_ref: (0, 0, 0))
    gs = pltpu.PrefetchScalarGridSpec(
        num_scalar_prefetch=1, grid=(1,),
        in_specs=[pl.BlockSpec((N_DEV, r, cols), pidx), blk, blk, blk], out_specs=[blk] * 4)
    return pl.pallas_call(
        body, name=name, grid_spec=gs, out_shape=[jax.ShapeDtypeStruct((r, cols), F32)] * 4,
        compiler_params=_cparams(("arbitrary",)))(dev, parts, w, m, v)


def _pad_rows(a, rows):
    return jnp.pad(a, ((0, rows - a.shape[0]), (0, 0)))


def _unblock_cols(g):
    return jnp.transpose(g, (1, 0, 2)).reshape(g.shape[1], N_DEV * g.shape[2])


def _block_cols(full):
    r, c8 = full.shape
    return jnp.transpose(full.reshape(r, N_DEV, c8 // N_DEV), (1, 0, 2))


def kernel(x, mem, norm_mix, w_in, b_gate, conv_w, conv_b, conv_ln_g, conv_ln_b, w_conv_out, sgu_ln_g, sgu_ln_b, sgu_w, sgu_b, w_sgu_out, w_mix_out, norm_xattn, norm_mem, w_q, w_kv, w_xo, norm_ffn, w_gu, w_down, norm_final, loss_target, m_norm_mix, m_w_in, m_b_gate, m_conv_w, m_conv_b, m_conv_ln_g, m_conv_ln_b, m_w_conv_out, m_sgu_ln_g, m_sgu_ln_b, m_sgu_w, m_sgu_b, m_w_sgu_out, m_w_mix_out, m_norm_xattn, m_norm_mem, m_w_q, m_w_kv, m_w_xo, m_norm_ffn, m_w_gu, m_w_down, m_norm_final, v_norm_mix, v_w_in, v_b_gate, v_conv_w, v_conv_b, v_conv_ln_g, v_conv_ln_b, v_w_conv_out, v_sgu_ln_g, v_sgu_ln_b, v_sgu_w, v_sgu_b, v_w_sgu_out, v_w_mix_out, v_norm_xattn, v_norm_mem, v_w_q, v_w_kv, v_w_xo, v_norm_ffn, v_w_gu, v_w_down, v_norm_final):
    given = dict(locals())
    bl, s, d = x.shape
    t = bl * s
    xf = x.reshape(t, d)
    tgt = loss_target.reshape(t, d)
    memf = mem.reshape(bl * mem.shape[1], d)
    cx, cy, cc = lax.axis_index("x"), lax.axis_index("y"), lax.axis_index("c")
    core_id = cc.astype(jnp.int32).reshape(1)
    chip_id = (2 * cx + cy).astype(jnp.int32).reshape(1)
    dev_id = (4 * cx + 2 * cy + cc).astype(jnp.int32).reshape(1)

    col_sharded = ["w_in", "w_kv", "w_gu"]
    row_sharded = ["w_conv_out", "w_sgu_out", "w_mix_out", "w_q", "w_xo", "w_down"]
    big = col_sharded + row_sharded
    shards = [given[n][0].astype(BF16) for n in big]
    shards.append(_pad_rows(b_gate[0], 8))
    shards.append(_pad_rows(conv_w[0], CONV_HALO))
    gathered = _all_gather(shards, name="gather_weights")
    wfull = {}
    for n, g in zip(big, gathered):
        wfull[n] = _unblock_cols(g) if n in col_sharded else g.reshape(N_DEV * g.shape[1], g.shape[2])
    bg_full = _unblock_cols(gathered[-2])
    cw_full = _unblock_cols(gathered[-1])

    tri = jnp.tril(jnp.ones((SGU_CHUNK, SGU_CHUNK), bool))
    wm32 = jnp.where(tri[None], sgu_w[0], 0.0)
    wm = wm32.astype(BF16)
    wmt = jnp.transpose(wm32, (0, 2, 1)).astype(BF16)
    sgu_bias = jnp.broadcast_to(sgu_b[0][:, :, None], (SGU_GROUPS, SGU_CHUNK, d // SGU_GROUPS))

    h1 = _rms_fwd(xf, norm_mix, name="rms_mix")
    p = _matmul(h1, wfull["w_in"], mode="nn", out_dtype=BF16, name="mm_in", tm=1024, tn=512, tk=1024)
    c_conv, a_act = _conv_fwd(p, cw_full, conv_b, conv_ln_g, conv_ln_b, bl=bl, s=s, name="conv_fwd")
    y_a = _matmul(a_act, wfull["w_conv_out"], mode="nn", out_dtype=BF16, name="mm_conv_out", tm=1024, tn=1024, tk=1024)
    sg, vn = _sgu_fwd(p, wm, sgu_bias, sgu_ln_g, sgu_ln_b, name="sgu_fwd")
    y_b = _matmul(sg, wfull["w_sgu_out"], mode="nn", out_dtype=BF16, name="mm_sgu_out", tm=1024, tn=1024, tk=1024)
    merged = _gates_fwd(p, y_a, y_b, bg_full, name="gates_fwd")
    x1, h2 = _matmul(merged, wfull["w_mix_out"], mode="nn", out_dtype=F32, name="mm_mix_out", tm=512, tn=1024, tk=1024,
                     residual=xf, rms_gain=norm_xattn)
    mem_n = _rms_fwd(memf, norm_mem, name="rms_mem")
    q = _matmul(h2, wfull["w_q"], mode="nn", out_dtype=BF16, name="mm_q", tm=1024, tn=1024, tk=1024)
    kv = _matmul(mem_n, wfull["w_kv"], mode="nn", out_dtype=BF16, name="mm_kv", tm=1024, tn=1024, tk=1024)
    o = _attn_fwd(q, kv, bl=bl, s=s, name="attn_fwd")
    x2, h3 = _matmul(o, wfull["w_xo"], mode="nn", out_dtype=F32, name="mm_xo", tm=512, tn=1024, tk=1024,
                     residual=x1, rms_gain=norm_ffn)
    gu = _matmul(h3, wfull["w_gu"], mode="nn", out_dtype=BF16, name="mm_gu", tm=1024, tn=512, tk=1024)
    act = _swiglu_fwd(gu, name="swiglu_fwd")
    x3 = _matmul(act, wfull["w_down"], mode="nn", out_dtype=F32, name="mm_down", tm=512, tn=1024, tk=1408, residual=x2)
    loss_part, dx3, d_norm_final = _final_loss(x3, tgt, norm_final.reshape(1, d), name="final_loss")
    loss = lax.psum(loss_part[0, 0], ("x", "y", "c"))

    grads = {}
    dact = _matmul(dx3, wfull["w_down"], mode="nt", out_dtype=BF16, name="mm_d_act", tm=512, tn=1408, tk=1024)
    grads["w_down"] = _matmul(act, dx3, mode="tn", out_dtype=F32, name="mm_dw_down", tm=1408, tn=1024, tk=512)
    dgu = _swiglu_bwd(gu, dact, name="swiglu_bwd")
    grads["w_gu"] = _matmul(h3, dgu, mode="tn", out_dtype=F32, name="mm_dw_gu", tm=1024, tn=512, tk=512)
    dh3 = _matmul(dgu, wfull["w_gu"], mode="nt", out_dtype=F32, name="mm_d_h3", tm=512, tn=1024, tk=512)
    dx2, d_norm_ffn = _rms_bwd(dx3, dh3, x2, norm_ffn, name="rms_ffn_bwd")
    do = _matmul(dx2, wfull["w_xo"], mode="nt", out_dtype=BF16, name="mm_d_o", tm=512, tn=1024, tk=1024)
    grads["w_xo"] = _matmul(o, dx2, mode="tn", out_dtype=F32, name="mm_dw_xo", tm=1024, tn=1024, tk=512)
    dq, dkv = _attn_bwd(q, kv, do, bl=bl, s=s, name="attn_bwd")
    grads["w_q"] = _matmul(h2, dq, mode="tn", out_dtype=F32, name="mm_dw_q", tm=1024, tn=1024, tk=512)
    dh2 = _matmul(dq, wfull["w_q"], mode="nt", out_dtype=F32, name="mm_d_h2", tm=512, tn=1024, tk=1024)
    grads["w_kv"] = _matmul(mem_n, dkv, mode="tn", out_dtype=F32, name="mm_dw_kv", tm=1024, tn=1024, tk=512)
    dmem_n = _matmul(dkv, wfull["w_kv"], mode="nt", out_dtype=F32, name="mm_d_mem", tm=512, tn=1024, tk=1024)
    d_norm_mem = _rms_bwd(None, dmem_n, memf, norm_mem, name="rms_mem_bwd", need_dx=False)
    dx1, d_norm_xattn = _rms_bwd(dx2, dh2, x1, norm_xattn, name="rms_xattn_bwd")
    dmerged = _matmul(dx1, wfull["w_mix_out"], mode="nt", out_dtype=BF16, name="mm_d_merged", tm=512, tn=1024, tk=1024)
    grads["w_mix_out"] = _matmul(merged, dx1, mode="tn", out_dtype=F32, name="mm_dw_mix", tm=1024, tn=1024, tk=512)
    dp, dy_a, dy_b, d_b_gate = _gates_bwd(dmerged, p, y_a, y_b, bg_full, name="gates_bwd")
    dsg = _matmul(dy_b, wfull["w_sgu_out"], mode="nt", out_dtype=BF16, name="mm_d_sg", tm=512, tn=1024, tk=1024)
    grads["w_sgu_out"] = _matmul(sg, dy_b, mode="tn", out_dtype=F32, name="mm_dw_sgu", tm=1024, tn=1024, tk=512)
    dp, d_sgu_w, d_sgu_b, d_sgu_ln_g, d_sgu_ln_b = _sgu_bwd(dp, dsg, p, vn, wm, wmt, sgu_bias, sgu_ln_g, name="sgu_bwd")
    da_act = _matmul(dy_a, wfull["w_conv_out"], mode="nt", out_dtype=BF16, name="mm_d_aact", tm=512, tn=1024, tk=1024)
    grads["w_conv_out"] = _matmul(a_act, dy_a, mode="tn", out_dtype=F32, name="mm_dw_conv", tm=1024, tn=1024, tk=512)
    dc, d_conv_ln_g, d_conv_ln_b = _conv_ln_bwd(da_act, c_conv, conv_ln_g, conv_ln_b, name="conv_ln_bwd")
    dp, d_conv_w, d_conv_b = _conv_bwd(dp, dc, p, cw_full, bl=bl, s=s, name="conv_bwd")
    grads["w_in"] = _matmul(h1, dp, mode="tn", out_dtype=F32, name="mm_dw_in", tm=1024, tn=1024, tk=512)
    dh1 = _matmul(dp, wfull["w_in"], mode="nt", out_dtype=F32, name="mm_d_h1", tm=512, tn=1024, tk=1024)
    grad_x, d_norm_mix = _rms_bwd(dx1, dh1, xf, norm_mix, name="rms_mix_bwd")

    blocks = []
    for n in big:
        g = _block_cols(grads[n]) if n in col_sharded else grads[n].reshape(N_DEV, -1, grads[n].shape[1])
        blocks.append(g.reshape(4, 2, *g.shape[1:]))
    landed1 = _exchange_sibling(blocks, name="grads_to_sibling")
    sums = [_add_sibling(g, l, core_id, name=f"add_sibling_{n}") for n, g, l in zip(big, blocks, landed1)]
    landed2 = _exchange_chips([sb for _, sb in sums], name="grads_to_chips")
    out = {}
    for n, (s32, _), l2 in zip(big, sums, landed2):
        res = _adamw_shard(s32, l2, chip_id, given[n][0], given["m_" + n][0], given["v_" + n][0], name=f"adamw_{n}")
        out[n] = [r[None] for r in res]

    rep_names = ["norm_mix", "conv_b", "conv_ln_g", "conv_ln_b", "sgu_ln_g", "sgu_ln_b", "norm_xattn", "norm_mem",
                 "norm_ffn", "norm_final", "sgu_b"]
    rep_grads = [d_norm_mix, d_conv_b, d_conv_ln_g, d_conv_ln_b, d_sgu_ln_g, d_sgu_ln_b, d_norm_xattn, d_norm_mem,
                 d_norm_ffn, d_norm_final, d_sgu_b.reshape(1, d)]
    nrep = len(rep_names)
    pad = jnp.zeros((16 - nrep, d), F32)
    sgw_rows = SGU_GROUPS * SGU_CHUNK * SGU_CHUNK // d

    def pack_rep(vecs, sgw):
        return jnp.concatenate([v.reshape(1, d) for v in vecs] + [pad, sgw.reshape(sgw_rows, d)], axis=0)

    def pack_col(bg, cw):
        return jnp.concatenate([_pad_rows(bg, 8), _pad_rows(cw, CONV_HALO)], axis=0)

    small_a = pack_rep(rep_grads, d_sgu_w)
    small_b = jnp.concatenate([d_b_gate, d_conv_w], axis=0)
    parts_a, parts_b = _all_gather([small_a, small_b], name="gather_small_grads")
    res_a = _adamw_small(parts_a, dev_id, pack_rep([given[n] for n in rep_names], sgu_w),
                         pack_rep([given["m_" + n] for n in rep_names], m_sgu_w),
                         pack_rep([given["v_" + n] for n in rep_names], v_sgu_w), name="adamw_small", col_block=False)
    res_b = _adamw_small(parts_b, dev_id, pack_col(b_gate[0], conv_w[0]), pack_col(m_b_gate[0], m_conv_w[0]),
                         pack_col(v_b_gate[0], v_conv_w[0]), name="adamw_small_cols", col_block=True)
    for i, n in enumerate(rep_names):
        out[n] = [r[i].reshape(given[n].shape) for r in res_a]
    out["sgu_w"] = [r[16:16 + sgw_rows].reshape(sgu_w.shape) for r in res_a]
    out["b_gate"] = [r[0:2][None] for r in res_b]
    out["conv_w"] = [r[8:8 + CONV_WIDTH][None] for r in res_b]

    order = ["norm_mix", "w_in", "b_gate", "conv_w", "conv_b", "conv_ln_g", "conv_ln_b", "w_conv_out", "sgu_ln_g",
             "sgu_ln_b", "sgu_w", "sgu_b", "w_sgu_out", "w_mix_out", "norm_xattn", "norm_mem", "w_q", "w_kv", "w_xo",
             "norm_ffn", "w_gu", "w_down", "norm_final"]
    return (loss, grad_x.reshape(x.shape), *[out[n][0] for n in order], *[out[n][1] for n in order],
            *[out[n][2] for n in order], *[out[n][3] for n in order])
```

```python
import functools

import jax
import jax.numpy as jnp
from jax import lax
from jax.experimental import pallas as pl
from jax.experimental.pallas import tpu as pltpu

F32 = jnp.float32
BF16 = jnp.bfloat16
RMS_EPS = 1e-6
LN_EPS = 1e-5
CONV_WIDTH = 31
CONV_HALO = 32
CONV_ROWS = 32
CONV_COLS = 256
SGU_CHUNK = 128
SGU_GROUPS = 8
HEADS = 4
N_DEV = 8
ADAM_LR, ADAM_B1, ADAM_B2, ADAM_EPS, ADAM_WD, ADAM_STEP = 0.001, 0.9, 0.999, 1e-08, 0.01, 10
VMEM_LIMIT = 56 * 1024 * 1024
TOKEN_TILE = 256
MESH_ID = pl.DeviceIdType.MESH

_GELU_K = 0.7978845608028654
_GELU_C = 0.044715


def _cparams(sem=None):
    return pltpu.CompilerParams(dimension_semantics=sem, vmem_limit_bytes=VMEM_LIMIT)


def _sigmoid(v):
    return 1.0 / (1.0 + jnp.exp(-v))


def _gelu(v):
    return 0.5 * v * (1.0 + jnp.tanh(_GELU_K * (v + _GELU_C * v * v * v)))


def _gelu_grad(v):
    th = jnp.tanh(_GELU_K * (v + _GELU_C * v * v * v))
    return 0.5 * (1.0 + th) + 0.5 * v * (1.0 - th * th) * _GELU_K * (1.0 + 3.0 * _GELU_C * v * v)


def _dot(a, b, dims):
    return lax.dot_general(a, b, (dims, ((), ())), preferred_element_type=F32)


_NN = ((1,), (0,))
_NT = ((1,), (1,))
_TN = ((0,), (0,))


def _matmul(a, b, *, mode, out_dtype, name, tm=512, tn=512, tk=512, residual=None, rms_gain=None, col_blocks=None):
    if mode == "nn":
        (m, k), (_, n) = a.shape, b.shape
    elif mode == "nt":
        (m, k), (n, _) = a.shape, b.shape
    else:
        (k, m), (_, n) = a.shape, b.shape
    tm, tn, tk = min(tm, m), min(tn, n), min(tk, k)
    assert m % tm == 0 and n % tn == 0 and k % tk == 0, (name, a.shape, b.shape, tm, tn, tk)
    nk = k // tk
    dims = {"nn": _NN, "nt": _NT, "tn": _TN}[mode]
    if rms_gain is not None:
        assert tn == n

    def body(*refs):
        refs = list(refs)
        a_ref, b_ref = refs[:2]
        pos = 2
        r_ref = g_ref = None
        if residual is not None:
            r_ref = refs[pos]
            pos += 1
        if rms_gain is not None:
            g_ref = refs[pos]
            pos += 1
        o_ref = refs[pos]
        pos += 1
        h_ref = None
        if rms_gain is not None:
            h_ref = refs[pos]
            pos += 1
        acc_ref = refs[pos] if nk > 1 else None
        part = _dot(a_ref[...].astype(BF16), b_ref[...].astype(BF16), dims)

        def finish(res):
            if r_ref is not None:
                res = res + r_ref[...].astype(F32)
            o_ref[...] = res.astype(out_dtype)
            if h_ref is not None:
                r = lax.rsqrt(jnp.mean(res * res, axis=-1, keepdims=True) + RMS_EPS)
                h_ref[...] = (res * r * g_ref[...]).astype(BF16)

        if nk == 1:
            finish(part)
        else:
            kk = pl.program_id(2)

            @pl.when(kk == 0)
            def _():
                acc_ref[...] = part

            @pl.when(kk > 0)
            def _():
                acc_ref[...] += part

            @pl.when(kk == nk - 1)
            def _():
                finish(acc_ref[...])

    if mode == "nn":
        a_spec = pl.BlockSpec((tm, tk), lambda i, j, kk: (i, kk))
        b_spec = pl.BlockSpec((tk, tn), lambda i, j, kk: (kk, j))
    elif mode == "nt":
        a_spec = pl.BlockSpec((tm, tk), lambda i, j, kk: (i, kk))
        b_spec = pl.BlockSpec((tn, tk), lambda i, j, kk: (j, kk))
    else:
        a_spec = pl.BlockSpec((tk, tm), lambda i, j, kk: (kk, i))
        b_spec = pl.BlockSpec((tk, tn), lambda i, j, kk: (kk, j))
    o_spec = pl.BlockSpec((tm, tn), lambda i, j, kk: (i, j))
    in_specs, args = [a_spec, b_spec], [a, b]
    if residual is not None:
        in_specs.append(o_spec)
        args.append(residual)
    out_shape, out_specs = [jax.ShapeDtypeStruct((m, n), out_dtype)], [o_spec]
    if col_blocks is not None:
        assert residual is None and rms_gain is None and (n // col_blocks) % tn == 0
        per = n // col_blocks // tn
        out_shape = [jax.ShapeDtypeStruct((col_blocks, m, n // col_blocks), out_dtype)]
        out_specs = [pl.BlockSpec((None, tm, tn), lambda i, j, kk: (j // per, i, j % per))]
    if rms_gain is not None:
        in_specs.append(pl.BlockSpec((1, n), lambda i, j, kk: (0, 0)))
        args.append(rms_gain)
        out_shape.append(jax.ShapeDtypeStruct((m, n), BF16))
        out_specs.append(o_spec)
    res = pl.pallas_call(
        body, name=name, grid=(m // tm, n // tn, nk), in_specs=in_specs, out_specs=out_specs, out_shape=out_shape,
        scratch_shapes=[pltpu.VMEM((tm, tn), F32)] if nk > 1 else [],
        compiler_params=_cparams(("parallel", "parallel", "arbitrary")),
    )(*args)
    return res if rms_gain is not None else res[0]


def _row_spec(tt, cols, col_block=0):
    return pl.BlockSpec((tt, cols), lambda i: (i, col_block))


def _const_spec(shape):
    return pl.BlockSpec(shape, lambda *_: (0,) * len(shape))


def _rms_fwd(x, gain, *, name):
    t, d = x.shape
    tt = min(TOKEN_TILE, t)

    def body(x_ref, g_ref, h_ref):
        xv = x_ref[...]
        r = lax.rsqrt(jnp.mean(xv * xv, axis=-1, keepdims=True) + RMS_EPS)
        h_ref[...] = (xv * r * g_ref[...]).astype(BF16)

    return pl.pallas_call(
        body, name=name, grid=(t // tt,), in_specs=[_row_spec(tt, d), _const_spec((1, d))],
        out_specs=_row_spec(tt, d), out_shape=jax.ShapeDtypeStruct((t, d), BF16),
        compiler_params=_cparams(("parallel",)))(x, gain)


def _rms_bwd(dres, dh, x, gain, *, name, need_dx=True):
    t, d = x.shape
    tt = min(TOKEN_TILE, t)

    def body(*refs):
        if need_dx:
            dres_ref, dh_ref, x_ref, g_ref, dx_ref, dg_ref = refs
        else:
            dh_ref, x_ref, g_ref, dg_ref = refs

        @pl.when(pl.program_id(0) == 0)
        def _():
            dg_ref[...] = jnp.zeros_like(dg_ref)

        xv = x_ref[...]
        dhv = dh_ref[...].astype(F32)
        r = lax.rsqrt(jnp.mean(xv * xv, axis=-1, keepdims=True) + RMS_EPS)
        xhat = xv * r
        dg_ref[...] += jnp.sum(dhv * xhat, axis=0, keepdims=True)
        if need_dx:
            dxh = dhv * g_ref[...]
            dx_ref[...] = dres_ref[...] + r * (dxh - xhat * jnp.mean(dxh * xhat, axis=-1, keepdims=True))

    rs = _row_spec(tt, d)
    if need_dx:
        in_specs, args = [rs, rs, rs, _const_spec((1, d))], (dres, dh, x, gain)
        out_specs = [rs, _const_spec((1, d))]
        out_shape = [jax.ShapeDtypeStruct((t, d), F32), jax.ShapeDtypeStruct((1, d), F32)]
    else:
        in_specs, args = [rs, rs, _const_spec((1, d))], (dh, x, gain)
        out_specs = [_const_spec((1, d))]
        out_shape = [jax.ShapeDtypeStruct((1, d), F32)]
    res = pl.pallas_call(body, name=name, grid=(t // tt,), in_specs=in_specs, out_specs=out_specs, out_shape=out_shape,
                         compiler_params=_cparams(("arbitrary",)))(*args)
    return res if need_dx else res[0]


def _final_loss(x3, target, gain, *, name):
    t, d = x3.shape
    tt = min(TOKEN_TILE, t)

    def body(x_ref, t_ref, g_ref, loss_ref, dx_ref, dg_ref):
        @pl.when(pl.program_id(0) == 0)
        def _():
            loss_ref[...] = jnp.zeros_like(loss_ref)
            dg_ref[...] = jnp.zeros_like(dg_ref)

        xv = x_ref[...]
        g = g_ref[...]
        r = lax.rsqrt(jnp.mean(xv * xv, axis=-1, keepdims=True) + RMS_EPS)
        xhat = xv * r
        err = xhat * g - t_ref[...]
        loss_ref[...] += 0.5 * jnp.sum(jnp.mean(err * err, axis=-1, keepdims=True), axis=0, keepdims=True)
        dy = err * (1.0 / d)
        dg_ref[...] += jnp.sum(dy * xhat, axis=0, keepdims=True)
        dxh = dy * g
        dx_ref[...] = r * (dxh - xhat * jnp.mean(dxh * xhat, axis=-1, keepdims=True))

    rs = _row_spec(tt, d)
    return pl.pallas_call(
        body, name=name, grid=(t // tt,), in_specs=[rs, rs, _const_spec((1, d))],
        out_specs=[_const_spec((1, 1)), rs, _const_spec((1, d))],
        out_shape=[jax.ShapeDtypeStruct((1, 1), F32), jax.ShapeDtypeStruct((t, d), F32), jax.ShapeDtypeStruct((1, d), F32)],
        compiler_params=_cparams(("arbitrary",)))(x3, target, gain)


def _conv_chunks(buf_ref, tt, fn):
    d = buf_ref.shape[1]

    def row_body(r, carry):
        r0 = pl.multiple_of(r * CONV_ROWS, CONV_ROWS)
        for cc in range(d // CONV_COLS):
            cs = slice(cc * CONV_COLS, (cc + 1) * CONV_COLS)
            fn(r0, cs, buf_ref[pl.ds(r0, 2 * CONV_ROWS), cs])
        return carry

    lax.fori_loop(0, tt // CONV_ROWS, row_body, 0)


def _conv_specs(bl, s, tt, d, col_a, col_g):
    nj = s // tt
    per = tt // CONV_HALO
    main_a = pl.BlockSpec((tt, d), lambda b, j: (b * nj + j, col_a))
    main_g = pl.BlockSpec((tt, d), lambda b, j: (b * nj + j, col_g))
    prev = lambda b, j: jnp.maximum((b * nj + j) * per - 1, 0)
    halo_a = pl.BlockSpec((CONV_HALO, d), lambda b, j: (prev(b, j), col_a))
    halo_g = pl.BlockSpec((CONV_HALO, d), lambda b, j: (prev(b, j), col_g))
    return main_a, main_g, halo_a, halo_g


def _fill_glu(buf_ref, a_ref, g_ref, ha_ref, hg_ref, tt):
    first = pl.program_id(1) == 0
    ha = ha_ref[...].astype(F32)
    hg = hg_ref[...].astype(F32)
    buf_ref[pl.ds(0, CONV_HALO), :] = jnp.where(first, 0.0, ha * _sigmoid(hg))
    av = a_ref[...].astype(F32)
    gv = g_ref[...].astype(F32)
    buf_ref[pl.ds(CONV_HALO, tt), :] = av * _sigmoid(gv)


def _conv_fwd(p, conv_w, conv_b, ln_g, ln_b, *, bl, s, name):
    t = p.shape[0]
    d = conv_w.shape[1]
    tt = min(TOKEN_TILE, s)
    off = CONV_HALO - (CONV_WIDTH - 1)

    def body(a_ref, g_ref, ha_ref, hg_ref, w_ref, b_ref, lg_ref, lb_ref, c_ref, act_ref, buf_ref, cbuf_ref):
        _fill_glu(buf_ref, a_ref, g_ref, ha_ref, hg_ref, tt)

        def chunk(r0, cs, win):
            acc = jnp.zeros((CONV_ROWS, CONV_COLS), F32)
            for k in range(CONV_WIDTH):
                acc = acc + win[off + k:off + k + CONV_ROWS, :] * w_ref[k:k + 1, cs]
            cbuf_ref[pl.ds(r0, CONV_ROWS), cs] = acc + b_ref[:, cs]

        _conv_chunks(buf_ref, tt, chunk)
        cv = cbuf_ref[...]
        c_ref[...] = cv.astype(BF16)
        mu = jnp.mean(cv, axis=-1, keepdims=True)
        dv = cv - mu
        rstd = lax.rsqrt(jnp.mean(dv * dv, axis=-1, keepdims=True) + LN_EPS)
        aln = dv * rstd * lg_ref[...] + lb_ref[...]
        act_ref[...] = (aln * _sigmoid(aln)).astype(BF16)

    main_a, main_g, halo_a, halo_g = _conv_specs(bl, s, tt, d, 0, 1)
    out_spec = pl.BlockSpec((tt, d), lambda b, j: (b * (s // tt) + j, 0))
    return pl.pallas_call(
        body, name=name, grid=(bl, s // tt),
        in_specs=[main_a, main_g, halo_a, halo_g, _const_spec((CONV_HALO, d)), _const_spec((1, d)), _const_spec((1, d)),
                  _const_spec((1, d))],
        out_specs=[out_spec, out_spec],
        out_shape=[jax.ShapeDtypeStruct((t, d), BF16), jax.ShapeDtypeStruct((t, d), BF16)],
        scratch_shapes=[pltpu.VMEM((tt + CONV_HALO, d), F32), pltpu.VMEM((tt, d), F32)],
        compiler_params=_cparams(("parallel", "parallel")))(p, p, p, p, conv_w, conv_b, ln_g, ln_b)


def _conv_ln_bwd(dact, c, ln_g, ln_b, *, name):
    t, d = c.shape
    tt = min(TOKEN_TILE, t)

    def body(da_ref, c_ref, lg_ref, lb_ref, dc_ref, dlg_ref, dlb_ref):
        @pl.when(pl.program_id(0) == 0)
        def _():
            dlg_ref[...] = jnp.zeros_like(dlg_ref)
            dlb_ref[...] = jnp.zeros_like(dlb_ref)

        cv = c_ref[...].astype(F32)
        g = lg_ref[...]
        mu = jnp.mean(cv, axis=-1, keepdims=True)
        dv = cv - mu
        rstd = lax.rsqrt(jnp.mean(dv * dv, axis=-1, keepdims=True) + LN_EPS)
        chat = dv * rstd
        aln = chat * g + lb_ref[...]
        sg = _sigmoid(aln)
        daln = da_ref[...].astype(F32) * (sg * (1.0 + aln * (1.0 - sg)))
        dlb_ref[...] += jnp.sum(daln, axis=0, keepdims=True)
        dlg_ref[...] += jnp.sum(daln * chat, axis=0, keepdims=True)
        dchat = daln * g
        dc = rstd * (dchat - jnp.mean(dchat, axis=-1, keepdims=True)
                     - chat * jnp.mean(dchat * chat, axis=-1, keepdims=True))
        dc_ref[...] = dc.astype(BF16)

    rs = _row_spec(tt, d)
    cs = _const_spec((1, d))
    return pl.pallas_call(
        body, name=name, grid=(t // tt,), in_specs=[rs, rs, cs, cs], out_specs=[rs, cs, cs],
        out_shape=[jax.ShapeDtypeStruct((t, d), BF16), jax.ShapeDtypeStruct((1, d), F32), jax.ShapeDtypeStruct((1, d), F32)],
        compiler_params=_cparams(("arbitrary",)))(dact, c, ln_g, ln_b)


def _conv_bwd(dp, dc, p, conv_w, *, bl, s, name):
    t = p.shape[0]
    d = conv_w.shape[1]
    tt = min(TOKEN_TILE, s)
    nj = s // tt
    per = tt // CONV_HALO
    off = CONV_HALO - (CONV_WIDTH - 1)
    last_blk = t // CONV_HALO - 1

    def body(dp_in, dc_ref, dcn_ref, a_ref, g_ref, ha_ref, hg_ref, w_ref, dp_ref, dw_ref, db_ref,
             gbuf_ref, dbuf_ref, dglu_ref, acc_ref):
        del dp_in
        b, j = pl.program_id(0), pl.program_id(1)
        start = jnp.logical_and(b == 0, j == 0)
        end = jnp.logical_and(b == bl - 1, j == nj - 1)

        @pl.when(start)
        def _():
            acc_ref[...] = jnp.zeros_like(acc_ref)
            db_ref[...] = jnp.zeros_like(db_ref)

        _fill_glu(gbuf_ref, a_ref, g_ref, ha_ref, hg_ref, tt)
        dcv = dc_ref[...].astype(F32)
        dbuf_ref[pl.ds(0, tt), :] = dcv
        dbuf_ref[pl.ds(tt, CONV_HALO), :] = jnp.where(j == nj - 1, 0.0, dcn_ref[...].astype(F32))
        db_ref[...] += jnp.sum(dcv, axis=0, keepdims=True)

        def wgrad(r0, cs, win):
            dcw = dbuf_ref[pl.ds(r0, CONV_ROWS), cs]
            for k in range(CONV_WIDTH):
                prod = dcw * win[off + k:off + k + CONV_ROWS, :]
                part = prod[0:8]
                for q in range(1, CONV_ROWS // 8):
                    part = part + prod[8 * q:8 * q + 8]
                acc_ref[k, :, cs] += part

        _conv_chunks(gbuf_ref, tt, wgrad)

        def xgrad(r0, cs, win):
            acc = jnp.zeros((CONV_ROWS, CONV_COLS), F32)
            for k in range(CONV_WIDTH):
                o = CONV_WIDTH - 1 - k
                acc = acc + win[o:o + CONV_ROWS, :] * w_ref[k:k + 1, cs]
            dglu_ref[pl.ds(r0, CONV_ROWS), cs] = acc

        _conv_chunks(dbuf_ref, tt, xgrad)
        dglu = dglu_ref[...]
        av = a_ref[...].astype(F32)
        sg = _sigmoid(g_ref[...].astype(F32))
        dp_ref[:, 0:d] = (dglu * sg).astype(BF16)
        dp_ref[:, d:2 * d] = (dglu * av * sg * (1.0 - sg)).astype(BF16)

        @pl.when(end)
        def _():
            for k in range(CONV_WIDTH):
                dw_ref[k:k + 1, :] = jnp.sum(acc_ref[k], axis=0, keepdims=True)
            dw_ref[CONV_WIDTH:CONV_HALO, :] = jnp.zeros((CONV_HALO - CONV_WIDTH, d), F32)

    main_a, main_g, halo_a, halo_g = _conv_specs(bl, s, tt, d, 0, 1)
    dc_main = pl.BlockSpec((tt, d), lambda b, j: (b * nj + j, 0))
    dc_next = pl.BlockSpec((CONV_HALO, d), lambda b, j: (jnp.minimum((b * nj + j + 1) * per, last_blk), 0))
    return pl.pallas_call(
        body, name=name, grid=(bl, nj),
        in_specs=[pl.BlockSpec(memory_space=pl.ANY), dc_main, dc_next, main_a, main_g, halo_a, halo_g,
                  _const_spec((CONV_HALO, d))],
        out_specs=[pl.BlockSpec((tt, 2 * d), lambda b, j: (b * nj + j, 0)), _const_spec((CONV_HALO, d)), _const_spec((1, d))],
        out_shape=[jax.ShapeDtypeStruct(dp.shape, dp.dtype), jax.ShapeDtypeStruct((CONV_HALO, d), F32),
                   jax.ShapeDtypeStruct((1, d), F32)],
        scratch_shapes=[pltpu.VMEM((tt + CONV_HALO, d), F32), pltpu.VMEM((tt + CONV_HALO, d), F32),
                        pltpu.VMEM((tt, d), F32), pltpu.VMEM((CONV_HALO, 8, d), F32)],
        input_output_aliases={0: 0},
        compiler_params=_cparams(("arbitrary", "arbitrary")))(dp, dc, dc, p, p, p, p, conv_w)


def _sgu_stats(bv):
    gv = _gelu(bv)
    mu = jnp.mean(gv, axis=-1, keepdims=True)
    dv = gv - mu
    rstd = lax.rsqrt(jnp.mean(dv * dv, axis=-1, keepdims=True) + LN_EPS)
    return dv * rstd, rstd


def _sgu_fwd(p, wm, bias, ln_g, ln_b, *, name):
    t = p.shape[0]
    d = ln_g.shape[1]
    tt = SGU_CHUNK
    gd = d // SGU_GROUPS

    def body(u_ref, v_ref, wm_ref, bias_ref, lg_ref, lb_ref, sg_ref, vn_ref):
        u = _gelu(u_ref[...].astype(F32))
        vhat, _ = _sgu_stats(v_ref[...].astype(F32))
        vb = (vhat * lg_ref[...] + lb_ref[...]).astype(BF16)
        vn_ref[...] = vb
        for g in range(SGU_GROUPS):
            gs = slice(g * gd, (g + 1) * gd)
            z = _dot(wm_ref[g], vb[:, gs], _NN) + bias_ref[g]
            sg_ref[:, gs] = (u[:, gs] * z).astype(BF16)

    rs = _row_spec(tt, d)
    return pl.pallas_call(
        body, name=name, grid=(t // tt,),
        in_specs=[_row_spec(tt, d, 2), _row_spec(tt, d, 3), _const_spec(wm.shape), _const_spec(bias.shape),
                  _const_spec((1, d)), _const_spec((1, d))],
        out_specs=[rs, rs], out_shape=[jax.ShapeDtypeStruct((t, d), BF16), jax.ShapeDtypeStruct((t, d), BF16)],
        compiler_params=_cparams(("parallel",)))(p, p, wm, bias, ln_g, ln_b)


def _sgu_bwd(dp, dsg, p, vn, wm, wmt, bias, ln_g, *, name):
    t = p.shape[0]
    d = ln_g.shape[1]
    tt = SGU_CHUNK
    gd = d // SGU_GROUPS
    nsteps = t // tt

    def body(dp_in, dsg_ref, u_ref, v_ref, vn_ref, wm_ref, wmt_ref, bias_ref, lg_ref,
             dp_ref, dw_ref, dbs_ref, dlg_ref, dlb_ref, dz_acc):
        del dp_in
        i = pl.program_id(0)

        @pl.when(i == 0)
        def _():
            dw_ref[...] = jnp.zeros_like(dw_ref)
            dlg_ref[...] = jnp.zeros_like(dlg_ref)
            dlb_ref[...] = jnp.zeros_like(dlb_ref)
            dz_acc[...] = jnp.zeros_like(dz_acc)

        bu = u_ref[...].astype(F32)
        bv = v_ref[...].astype(F32)
        u = _gelu(bu)
        vhat, rstd = _sgu_stats(bv)
        vb = vn_ref[...]
        dsg = dsg_ref[...].astype(F32)
        row = lax.broadcasted_iota(jnp.int32, (tt, tt), 0)
        col = lax.broadcasted_iota(jnp.int32, (tt, tt), 1)
        causal = col <= row
        du_parts, dv_parts = [], []
        for g in range(SGU_GROUPS):
            gs = slice(g * gd, (g + 1) * gd)
            z = _dot(wm_ref[g], vb[:, gs], _NN) + bias_ref[g]
            du_parts.append(dsg[:, gs] * z)
            dz = dsg[:, gs] * u[:, gs]
            dz_acc[:, gs] += dz
            dzb = dz.astype(BF16)
            dw_ref[g] += jnp.where(causal, _dot(dzb, vb[:, gs], _NT), 0.0)
            dv_parts.append(_dot(wmt_ref[g], dzb, _NN))
        du = jnp.concatenate(du_parts, axis=1)
        dv = jnp.concatenate(dv_parts, axis=1)
        dp_ref[:, 0:d] = (du * _gelu_grad(bu)).astype(BF16)
        dlb_ref[...] += jnp.sum(dv, axis=0, keepdims=True)
        dlg_ref[...] += jnp.sum(dv * vhat, axis=0, keepdims=True)
        dvh = dv * lg_ref[...]
        dgv = rstd * (dvh - jnp.mean(dvh, axis=-1, keepdims=True) - vhat * jnp.mean(dvh * vhat, axis=-1, keepdims=True))
        dp_ref[:, d:2 * d] = (dgv * _gelu_grad(bv)).astype(BF16)

        @pl.when(i == nsteps - 1)
        def _():
            ones = jnp.ones((8, gd), F32)
            for g in range(SGU_GROUPS):
                gs = slice(g * gd, (g + 1) * gd)
                tot = lax.dot_general(ones, dz_acc[:, gs], (_NT, ((), ())), preferred_element_type=F32,
                                      precision=lax.Precision.HIGHEST)
                dbs_ref[g:g + 1, :] = tot[0:1, :]

    rs = _row_spec(tt, d)
    c1 = _const_spec((1, d))
    return pl.pallas_call(
        body, name=name, grid=(nsteps,),
        in_specs=[pl.BlockSpec(memory_space=pl.ANY), rs, _row_spec(tt, d, 2), _row_spec(tt, d, 3), rs,
                  _const_spec(wm.shape), _const_spec(wmt.shape), _const_spec(bias.shape), c1],
        out_specs=[pl.BlockSpec((tt, 2 * d), lambda i: (i, 1)), _const_spec(wm.shape), _const_spec((SGU_GROUPS, tt)), c1, c1],
        out_shape=[jax.ShapeDtypeStruct(dp.shape, dp.dtype), jax.ShapeDtypeStruct(wm.shape, F32),
                   jax.ShapeDtypeStruct((SGU_GROUPS, tt), F32), jax.ShapeDtypeStruct((1, d), F32),
                   jax.ShapeDtypeStruct((1, d), F32)],
        scratch_shapes=[pltpu.VMEM((tt, d), F32)],
        input_output_aliases={0: 0},
        compiler_params=_cparams(("arbitrary",)))(dp, dsg, p, p, vn, wm, wmt, bias, ln_g)


def _gates_fwd(p, ya, yb, b_gate, *, name):
    t, d = ya.shape
    tt = min(TOKEN_TILE, t)

    def body(ga_ref, gb_ref, ya_ref, yb_ref, bg_ref, o_ref):
        sa = _sigmoid(ga_ref[...].astype(F32) + bg_ref[0:1, :])
        sb = _sigmoid(gb_ref[...].astype(F32) + bg_ref[1:2, :])
        o_ref[...] = (sa * ya_ref[...].astype(F32) + sb * yb_ref[...].astype(F32)).astype(BF16)

    rs = _row_spec(tt, d)
    return pl.pallas_call(
        body, name=name, grid=(t // tt,),
        in_specs=[_row_spec(tt, d, 4), _row_spec(tt, d, 5), rs, rs, _const_spec(b_gate.shape)],
        out_specs=rs, out_shape=jax.ShapeDtypeStruct((t, d), BF16),
        compiler_params=_cparams(("parallel",)))(p, p, ya, yb, b_gate)


def _gates_bwd(dmerged, p, ya, yb, b_gate, *, name):
    t, d = ya.shape
    tt = min(TOKEN_TILE, t)

    def body(dm_ref, ga_ref, gb_ref, ya_ref, yb_ref, bg_ref, dp_ref, dya_ref, dyb_ref, dbg_ref):
        @pl.when(pl.program_id(0) == 0)
        def _():
            dbg_ref[...] = jnp.zeros_like(dbg_ref)

        dm = dm_ref[...].astype(F32)
        sa = _sigmoid(ga_ref[...].astype(F32) + bg_ref[0:1, :])
        sb = _sigmoid(gb_ref[...].astype(F32) + bg_ref[1:2, :])
        dya_ref[...] = (dm * sa).astype(BF16)
        dyb_ref[...] = (dm * sb).astype(BF16)
        dga = dm * ya_ref[...].astype(F32) * sa * (1.0 - sa)
        dgb = dm * yb_ref[...].astype(F32) * sb * (1.0 - sb)
        dp_ref[:, 0:d] = dga.astype(BF16)
        dp_ref[:, d:2 * d] = dgb.astype(BF16)
        dbg_ref[0:1, :] += jnp.sum(dga, axis=0, keepdims=True)
        dbg_ref[1:2, :] += jnp.sum(dgb, axis=0, keepdims=True)

    rs = _row_spec(tt, d)
    return pl.pallas_call(
        body, name=name, grid=(t // tt,),
        in_specs=[rs, _row_spec(tt, d, 4), _row_spec(tt, d, 5), rs, rs, _const_spec(b_gate.shape)],
        out_specs=[pl.BlockSpec((tt, 2 * d), lambda i: (i, 2)), rs, rs, _const_spec((8, d))],
        out_shape=[jax.ShapeDtypeStruct(p.shape, BF16), jax.ShapeDtypeStruct((t, d), BF16),
                   jax.ShapeDtypeStruct((t, d), BF16), jax.ShapeDtypeStruct((8, d), F32)],
        compiler_params=_cparams(("arbitrary",)))(dmerged, p, p, ya, yb, b_gate)


def _softmax_rows(s):
    e = jnp.exp(s - jnp.max(s, axis=-1, keepdims=True))
    return e / jnp.sum(e, axis=-1, keepdims=True)


def _attn_fwd(q, kv, *, bl, s, name):
    t, d = q.shape
    mlen = kv.shape[0] // bl
    hd = d // HEADS
    tq = min(TOKEN_TILE, s)
    nq = s // tq
    scale = hd ** -0.5

    def body(q_ref, kv_ref, o_ref):
        for h in range(HEADS):
            hs = slice(h * hd, (h + 1) * hd)
            vs = slice(d + h * hd, d + (h + 1) * hd)
            pr = _softmax_rows(_dot(q_ref[:, hs], kv_ref[:, hs], _NT) * scale)
            o_ref[:, hs] = _dot(pr.astype(BF16), kv_ref[:, vs], _NN).astype(BF16)

    qs = pl.BlockSpec((tq, d), lambda b, j: (b * nq + j, 0))
    return pl.pallas_call(
        body, name=name, grid=(bl, nq), in_specs=[qs, pl.BlockSpec((mlen, 2 * d), lambda b, j: (b, 0))],
        out_specs=qs, out_shape=jax.ShapeDtypeStruct((t, d), BF16),
        compiler_params=_cparams(("parallel", "parallel")))(q, kv)


def _attn_bwd(q, kv, do, *, bl, s, name):
    t, d = q.shape
    mlen = kv.shape[0] // bl
    hd = d // HEADS
    tq = min(TOKEN_TILE, s)
    nq = s // tq
    scale = hd ** -0.5

    def body(q_ref, kv_ref, do_ref, dq_ref, dkv_ref):
        @pl.when(pl.program_id(1) == 0)
        def _():
            dkv_ref[...] = jnp.zeros_like(dkv_ref)

        for h in range(HEADS):
            hs = slice(h * hd, (h + 1) * hd)
            vs = slice(d + h * hd, d + (h + 1) * hd)
            qh, kh, vh, doh = q_ref[:, hs], kv_ref[:, hs], kv_ref[:, vs], do_ref[:, hs]
            pr = _softmax_rows(_dot(qh, kh, _NT) * scale)
            dpr = _dot(doh, vh, _NT)
            dkv_ref[:, vs] += _dot(pr.astype(BF16), doh, _TN)
            ds = (pr * (dpr - jnp.sum(dpr * pr, axis=-1, keepdims=True)) * scale).astype(BF16)
            dq_ref[:, hs] = _dot(ds, kh, _NN).astype(BF16)
            dkv_ref[:, hs] += _dot(ds, qh, _TN)

    qs = pl.BlockSpec((tq, d), lambda b, j: (b * nq + j, 0))
    ks = pl.BlockSpec((mlen, 2 * d), lambda b, j: (b, 0))
    return pl.pallas_call(
        body, name=name, grid=(bl, nq), in_specs=[qs, ks, qs], out_specs=[qs, ks],
        out_shape=[jax.ShapeDtypeStruct((t, d), BF16), jax.ShapeDtypeStruct(kv.shape, F32)],
        compiler_params=_cparams(("parallel", "arbitrary")))(q, kv, do)


def _swiglu_fwd(gu, *, name):
    t, f2 = gu.shape
    f = f2 // 2
    tt = min(TOKEN_TILE, t)

    def body(gu_ref, o_ref):
        gt = gu_ref[:, 0:f].astype(F32)
        up = gu_ref[:, f:f2].astype(F32)
        o_ref[...] = (gt * _sigmoid(gt) * up).astype(BF16)

    return pl.pallas_call(
        body, name=name, grid=(t // tt,), in_specs=[_row_spec(tt, f2)], out_specs=_row_spec(tt, f),
        out_shape=jax.ShapeDtypeStruct((t, f), BF16), compiler_params=_cparams(("parallel",)))(gu)


def _swiglu_bwd(gu, dact, *, name):
    t, f2 = gu.shape
    f = f2 // 2
    tt = min(TOKEN_TILE, t)

    def body(gu_ref, da_ref, o_ref):
        gt = gu_ref[:, 0:f].astype(F32)
        up = gu_ref[:, f:f2].astype(F32)
        da = da_ref[...].astype(F32)
        sg = _sigmoid(gt)
        o_ref[:, 0:f] = (da * up * sg * (1.0 + gt * (1.0 - sg))).astype(BF16)
        o_ref[:, f:f2] = (da * gt * sg).astype(BF16)

    return pl.pallas_call(
        body, name=name, grid=(t // tt,), in_specs=[_row_spec(tt, f2), _row_spec(tt, f)], out_specs=_row_spec(tt, f2),
        out_shape=jax.ShapeDtypeStruct((t, f2), BF16), compiler_params=_cparams(("parallel",)))(gu, dact)


def _mesh_pos():
    return lax.axis_index("x"), lax.axis_index("y"), lax.axis_index("c")


def _all_gather(arrs, *, name):
    n = len(arrs)
    hbm = pl.BlockSpec(memory_space=pl.ANY)

    def body(*refs):
        ins, outs = refs[:n], refs[n:2 * n]
        send_sems, recv_sems, loc_sems = refs[2 * n:]
        x, y, c = _mesh_pos()
        me, sib = (x, y, c), (x, y, 1 - c)
        chips = [(1 - x, y), (x, 1 - y), (1 - x, 1 - y)]

        def idx(dev):
            return 4 * dev[0] + 2 * dev[1] + dev[2]

        def copy(w, k, block, to, from_input=False):
            return pltpu.make_async_remote_copy(
                src_ref=ins[w] if from_input else outs[w].at[idx(block)], dst_ref=outs[w].at[idx(block)],
                send_sem=send_sems.at[w, k], recv_sem=recv_sems.at[w, k], device_id=to, device_id_type=MESH_ID)

        own = [pltpu.make_async_copy(ins[w], outs[w].at[idx(me)], loc_sems.at[w]) for w in range(n)]
        for cp in own:
            cp.start()
        first = []
        for w in range(n):
            first.append(copy(w, 0, me, sib, True))
            first += [copy(w, 1 + j, me, (*chip, c), True) for j, chip in enumerate(chips)]
        for cp in first:
            cp.start()
        passed = []
        for j, chip in enumerate(chips):
            for w in range(n):
                copy(w, 1 + j, (*chip, c), me).wait_recv()
                fwd = copy(w, 4 + j, (*chip, c), sib)
                fwd.start()
                passed.append(fwd)
        for w in range(n):
            copy(w, 0, sib, me).wait_recv()
            for j, chip in enumerate(chips):
                copy(w, 4 + j, (*chip, 1 - c), me).wait_recv()
        for cp in first + passed:
            cp.wait_send()
        for cp in own:
            cp.wait()

    return pl.pallas_call(
        body, name=name, in_specs=[hbm] * n, out_specs=[hbm] * n,
        out_shape=[jax.ShapeDtypeStruct((N_DEV, *a.shape), a.dtype) for a in arrs],
        scratch_shapes=[pltpu.SemaphoreType.DMA((n, 7)), pltpu.SemaphoreType.DMA((n, 7)), pltpu.SemaphoreType.DMA((n,))],
    )(*arrs)


_HBM = pl.BlockSpec(memory_space=pltpu.HBM)
_SEM = pl.BlockSpec(memory_space=pltpu.SEMAPHORE)
_ANY = pl.BlockSpec(memory_space=pl.ANY)
_EFFECT = pltpu.SideEffectType.DATAFLOW_SIDE_EFFECTING
N_PEERS = N_DEV - 1


def _related(pos, r):
    x, y, c = pos
    return (1 - x if r & 4 else x, 1 - y if r & 2 else y, 1 - c if r & 1 else c)


def _dev_index(dev):
    return 4 * dev[0] + 2 * dev[1] + dev[2]


def _in_hbm(a):
    return pltpu.with_memory_space_constraint(a, pltpu.HBM)


def _split_copies(kind, srcs, lands, send_sems, recv_sems):
    pos = _mesh_pos()
    me = _dev_index(pos)
    out = []
    for w in range(len(srcs)):
        for r in range(1, N_DEV):
            peer = _related(pos, r)
            if kind == "gather":
                src, dst_here, dst_there = srcs[w], lands[w].at[_dev_index(peer)], lands[w].at[me]
            else:
                src, dst_here, dst_there = srcs[w].at[_dev_index(peer)], lands[w].at[r - 1], lands[w].at[r - 1]
            out.append((src, dst_here, dst_there, send_sems.at[w * N_PEERS + r - 1], recv_sems.at[w * N_PEERS + r - 1], peer))
    return out


def _copy_start(kind, srcs, land_shapes, *, name):
    n = len(srcs)

    def body(*refs):
        src_refs, land_refs = refs[:n], refs[n:2 * n]
        send_sems, recv_sems = refs[2 * n], refs[2 * n + 1]
        token = refs[-1]
        for src, _, dst, ssem, rsem, peer in _split_copies(kind, src_refs, land_refs, send_sems, recv_sems):
            pltpu.make_async_remote_copy(src_ref=src, dst_ref=dst, send_sem=ssem, recv_sem=rsem, device_id=peer,
                                         device_id_type=MESH_ID).start()
        token[...] = jnp.zeros_like(token)

    lands = [_in_hbm(lax.empty(shape, s.dtype)) for s, shape in zip(srcs, land_shapes)]
    res = pl.pallas_call(
        body, name=name,
        out_shape=(pltpu.SemaphoreType.DMA((n * N_PEERS,)), pltpu.SemaphoreType.DMA((n * N_PEERS,)),
                   *[pltpu.HBM(s.shape, s.dtype) for s in srcs], *[pltpu.HBM(l.shape, l.dtype) for l in lands],
                   jax.ShapeDtypeStruct((8, 128), F32)),
        in_specs=[_HBM] * (2 * n), out_specs=(_SEM, _SEM, *[_HBM] * (2 * n), pl.BlockSpec(memory_space=pltpu.VMEM)),
        input_output_aliases={i: 2 + i for i in range(2 * n)},
        compiler_params=pltpu.CompilerParams(has_side_effects=_EFFECT),
    )(*[_in_hbm(s) for s in srcs], *lands)
    return res[0], res[1], list(res[2:2 + n]), list(res[2 + n:2 + 2 * n]), res[-1]


def _copy_wait(kind, send_sems, recv_sems, srcs, lands, after, *, name):
    n = len(srcs)

    def body(*refs):
        src_refs, land_refs = refs[:n], refs[n:2 * n]
        ssems, rsems = refs[2 * n], refs[2 * n + 1]
        for src, dst, _, ssem, rsem, peer in _split_copies(kind, src_refs, land_refs, ssems, rsems):
            cp = pltpu.make_async_remote_copy(src_ref=src, dst_ref=dst, send_sem=ssem, recv_sem=rsem, device_id=peer,
                                              device_id_type=MESH_ID)
            cp.wait_send()
            cp.wait_recv()

    res = pl.pallas_call(
        body, name=name,
        out_shape=(*[pltpu.HBM(s.shape, s.dtype) for s in srcs], *[pltpu.HBM(l.shape, l.dtype) for l in lands]),
        in_specs=[_HBM] * (2 * n) + [_SEM, _SEM, _ANY], out_specs=tuple([_HBM] * (2 * n)),
        input_output_aliases={i: i for i in range(2 * n)},
        compiler_params=pltpu.CompilerParams(has_side_effects=_EFFECT),
    )(*srcs, *lands, send_sems, recv_sems, after)
    return list(res[:n]), list(res[n:])


def _row_tile(rows):
    return rows if rows <= 512 else 256


def _adamw_math(w, g, m, v):
    m2 = ADAM_B1 * m + (1.0 - ADAM_B1) * g
    v2 = ADAM_B2 * v + (1.0 - ADAM_B2) * (g * g)
    m_hat = m2 / (1.0 - ADAM_B1 ** ADAM_STEP)
    v_hat = v2 / (1.0 - ADAM_B2 ** ADAM_STEP)
    delta = -ADAM_LR * (m_hat / (jnp.sqrt(v_hat) + ADAM_EPS) + ADAM_WD * w)
    return delta, m2, v2


def _adamw_shard(partials, landed, dev, w, m, v, *, name):
    r, c = w.shape
    tr = _row_tile(r)

    def body(dev_ref, p_ref, l_ref, w_ref, m_ref, v_ref, g_out, d_out, m_out, v_out):
        del dev_ref
        g = p_ref[...].astype(F32)
        for k in range(N_PEERS):
            g = g + l_ref[k].astype(F32)
        delta, m2, v2 = _adamw_math(w_ref[...], g, m_ref[...], v_ref[...])
        g_out[...] = g
        d_out[...] = delta
        m_out[...] = m2
        v_out[...] = v2

    blk = pl.BlockSpec((tr, c), lambda i, dev_ref: (i, 0))
    gs = pltpu.PrefetchScalarGridSpec(
        num_scalar_prefetch=1, grid=(r // tr,),
        in_specs=[pl.BlockSpec((None, tr, c), lambda i, dev_ref: (dev_ref[0], i, 0)),
                  pl.BlockSpec((N_PEERS, tr, c), lambda i, dev_ref: (0, i, 0)), blk, blk, blk],
        out_specs=[blk] * 4)
    return pl.pallas_call(
        body, name=name, grid_spec=gs, out_shape=[jax.ShapeDtypeStruct((r, c), F32)] * 4,
        compiler_params=_cparams(("parallel",)))(dev, partials, landed, w, m, v)


def _adamw_small(parts, dev, w, m, v, *, name, col_block):
    _, r, d = parts.shape
    cols = w.shape[1]

    def body(dev_ref, p_ref, w_ref, m_ref, v_ref, g_out, d_out, m_out, v_out):
        del dev_ref
        g = p_ref[0]
        for k in range(1, N_DEV):
            g = g + p_ref[k]
        delta, m2, v2 = _adamw_math(w_ref[...], g, m_ref[...], v_ref[...])
        g_out[...] = g
        d_out[...] = delta
        m_out[...] = m2
        v_out[...] = v2

    blk = pl.BlockSpec((r, cols), lambda i, dev_ref: (0, 0))
    pidx = (lambda i, dev_ref: (0, 0, dev_ref[0])) if col_block else (lambda i, dev_ref: (0, 0, 0))
    gs = pltpu.PrefetchScalarGridSpec(
        num_scalar_prefetch=1, grid=(1,),
        in_specs=[pl.BlockSpec((N_DEV, r, cols), pidx), blk, blk, blk], out_specs=[blk] * 4)
    return pl.pallas_call(
        body, name=name, grid_spec=gs, out_shape=[jax.ShapeDtypeStruct((r, cols), F32)] * 4,
        compiler_params=_cparams(("arbitrary",)))(dev, parts, w, m, v)


def _pad_rows(a, rows):
    return jnp.pad(a, ((0, rows - a.shape[0]), (0, 0)))


def _unblock_cols(g):
    return jnp.transpose(g, (1, 0, 2)).reshape(g.shape[1], N_DEV * g.shape[2])


def _block_cols(full):
    r, c8 = full.shape
    return jnp.transpose(full.reshape(r, N_DEV, c8 // N_DEV), (1, 0, 2))


def kernel(x, mem, norm_mix, w_in, b_gate, conv_w, conv_b, conv_ln_g, conv_ln_b, w_conv_out, sgu_ln_g, sgu_ln_b, sgu_w, sgu_b, w_sgu_out, w_mix_out, norm_xattn, norm_mem, w_q, w_kv, w_xo, norm_ffn, w_gu, w_down, norm_final, loss_target, m_norm_mix, m_w_in, m_b_gate, m_conv_w, m_conv_b, m_conv_ln_g, m_conv_ln_b, m_w_conv_out, m_sgu_ln_g, m_sgu_ln_b, m_sgu_w, m_sgu_b, m_w_sgu_out, m_w_mix_out, m_norm_xattn, m_norm_mem, m_w_q, m_w_kv, m_w_xo, m_norm_ffn, m_w_gu, m_w_down, m_norm_final, v_norm_mix, v_w_in, v_b_gate, v_conv_w, v_conv_b, v_conv_ln_g, v_conv_ln_b, v_w_conv_out, v_sgu_ln_g, v_sgu_ln_b, v_sgu_w, v_sgu_b, v_w_sgu_out, v_w_mix_out, v_norm_xattn, v_norm_mem, v_w_q, v_w_kv, v_w_xo, v_norm_ffn, v_w_gu, v_w_down, v_norm_final):
    given = dict(locals())
    bl, s, d = x.shape
    t = bl * s
    xf = x.reshape(t, d)
    tgt = loss_target.reshape(t, d)
    memf = mem.reshape(bl * mem.shape[1], d)
    cx, cy, cc = lax.axis_index("x"), lax.axis_index("y"), lax.axis_index("c")
    dev = 4 * cx + 2 * cy + cc
    dev_id = dev.astype(jnp.int32).reshape(1)
    col_sharded = ["w_in", "w_kv", "w_gu"]

    def full_weight(name, blocks):
        return _unblock_cols(blocks) if name in col_sharded else blocks.reshape(N_DEV * blocks.shape[1], blocks.shape[2])

    g_in, g_bg, g_cw = _all_gather([w_in[0].astype(BF16), _pad_rows(b_gate[0], 8), _pad_rows(conv_w[0], CONV_HALO)],
                                   name="gather_w_in")
    early = ["w_conv_out", "w_sgu_out", "w_mix_out", "w_q", "w_kv", "w_xo"]
    late = ["w_gu", "w_down"]
    shards = {n: given[n][0].astype(BF16) for n in early + late}
    started = {}
    for grp, names in (("early", early), ("late", late)):
        srcs = [shards[n] for n in names]
        started[grp] = _copy_start("gather", srcs, [(N_DEV, *a.shape) for a in srcs], name=f"gather_{grp}_start")
    token = started["early"][4][0:1, 0:1] + started["late"][4][0:1, 0:1]
    wfull = {"w_in": _unblock_cols(g_in)}
    bg_full = _unblock_cols(g_bg)
    cw_full = _unblock_cols(g_cw)

    def finish_gather(grp, names, after):
        ssem, rsem, srcs, lands, _ = started[grp]
        _, lands = _copy_wait("gather", ssem, rsem, srcs, lands, after, name=f"gather_{grp}_wait")
        for n, land in zip(names, lands):
            wfull[n] = full_weight(n, lax.dynamic_update_index_in_dim(land, shards[n], dev, 0))

    tri = jnp.tril(jnp.ones((SGU_CHUNK, SGU_CHUNK), bool))
    wm32 = jnp.where(tri[None], sgu_w[0], 0.0)
    wm = wm32.astype(BF16)
    wmt = jnp.transpose(wm32, (0, 2, 1)).astype(BF16)
    sgu_bias = jnp.broadcast_to(sgu_b[0][:, :, None], (SGU_GROUPS, SGU_CHUNK, d // SGU_GROUPS))

    h1 = _rms_fwd(xf, norm_mix + token, name="rms_mix")
    p = _matmul(h1, wfull["w_in"], mode="nn", out_dtype=BF16, name="mm_in", tm=1024, tn=512, tk=1024)
    c_conv, a_act = _conv_fwd(p, cw_full, conv_b, conv_ln_g, conv_ln_b, bl=bl, s=s, name="conv_fwd")
    finish_gather("early", early, a_act)
    y_a = _matmul(a_act, wfull["w_conv_out"], mode="nn", out_dtype=BF16, name="mm_conv_out", tm=1024, tn=1024, tk=1024)
    sg, vn = _sgu_fwd(p, wm, sgu_bias, sgu_ln_g, sgu_ln_b, name="sgu_fwd")
    y_b = _matmul(sg, wfull["w_sgu_out"], mode="nn", out_dtype=BF16, name="mm_sgu_out", tm=1024, tn=1024, tk=1024)
    merged = _gates_fwd(p, y_a, y_b, bg_full, name="gates_fwd")
    x1, h2 = _matmul(merged, wfull["w_mix_out"], mode="nn", out_dtype=F32, name="mm_mix_out", tm=512, tn=1024, tk=1024,
                     residual=xf, rms_gain=norm_xattn)
    mem_n = _rms_fwd(memf, norm_mem, name="rms_mem")
    q = _matmul(h2, wfull["w_q"], mode="nn", out_dtype=BF16, name="mm_q", tm=1024, tn=1024, tk=1024)
    kv = _matmul(mem_n, wfull["w_kv"], mode="nn", out_dtype=BF16, name="mm_kv", tm=1024, tn=1024, tk=1024)
    o = _attn_fwd(q, kv, bl=bl, s=s, name="attn_fwd")
    x2, h3 = _matmul(o, wfull["w_xo"], mode="nn", out_dtype=F32, name="mm_xo", tm=512, tn=1024, tk=1024,
                     residual=x1, rms_gain=norm_ffn)
    finish_gather("late", late, h3)
    gu = _matmul(h3, wfull["w_gu"], mode="nn", out_dtype=BF16, name="mm_gu", tm=1024, tn=512, tk=1024)
    act = _swiglu_fwd(gu, name="swiglu_fwd")
    x3 = _matmul(act, wfull["w_down"], mode="nn", out_dtype=F32, name="mm_down", tm=512, tn=1024, tk=1408, residual=x2)
    loss_part, dx3, d_norm_final = _final_loss(x3, tgt, norm_final.reshape(1, d), name="final_loss")
    loss = lax.psum(loss_part[0, 0], ("x", "y", "c"))

    grads = {}
    sent = []

    def send_grads(names, tag):
        blocks = []
        for n in names:
            g = grads[n]
            if g.ndim == 2:
                g = _block_cols(g) if n in col_sharded else g.reshape(N_DEV, -1, g.shape[1])
            blocks.append(g)
        ssem, rsem, srcs, lands, tok = _copy_start("scatter", blocks, [(N_PEERS, *g.shape[1:]) for g in blocks],
                                                   name=f"grads_{tag}_start")
        sent.append((names, ssem, rsem, srcs, lands))
        return tok[0:1, 0:1]

    dact = _matmul(dx3, wfull["w_down"], mode="nt", out_dtype=BF16, name="mm_d_act", tm=512, tn=1408, tk=1024)
    grads["w_down"] = _matmul(act, dx3, mode="tn", out_dtype=BF16, name="mm_dw_down", tm=1408, tn=1024, tk=512)
    dgu = _swiglu_bwd(gu, dact, name="swiglu_bwd")
    grads["w_gu"] = _matmul(h3, dgu, mode="tn", out_dtype=BF16, name="mm_dw_gu", tm=1024, tn=512, tk=512)
    tok = send_grads(["w_down", "w_gu"], "ffn")
    dh3 = _matmul(dgu, wfull["w_gu"], mode="nt", out_dtype=F32, name="mm_d_h3", tm=512, tn=1024, tk=512)
    dx2, d_norm_ffn = _rms_bwd(dx3, dh3, x2, norm_ffn + tok, name="rms_ffn_bwd")
    do = _matmul(dx2, wfull["w_xo"], mode="nt", out_dtype=BF16, name="mm_d_o", tm=512, tn=1024, tk=1024)
    grads["w_xo"] = _matmul(o, dx2, mode="tn", out_dtype=BF16, name="mm_dw_xo", tm=1024, tn=1024, tk=512)
    dq, dkv = _attn_bwd(q, kv, do, bl=bl, s=s, name="attn_bwd")
    grads["w_q"] = _matmul(h2, dq, mode="tn", out_dtype=BF16, name="mm_dw_q", tm=1024, tn=1024, tk=512)
    grads["w_kv"] = _matmul(mem_n, dkv, mode="tn", out_dtype=BF16, name="mm_dw_kv", tm=1024, tn=256, tk=512,
                            col_blocks=N_DEV)
    tok = send_grads(["w_xo", "w_q", "w_kv"], "attn")
    dh2 = _matmul(dq, wfull["w_q"], mode="nt", out_dtype=F32, name="mm_d_h2", tm=512, tn=1024, tk=1024)
    dmem_n = _matmul(dkv, wfull["w_kv"], mode="nt", out_dtype=F32, name="mm_d_mem", tm=512, tn=1024, tk=1024)
    d_norm_mem = _rms_bwd(None, dmem_n, memf, norm_mem, name="rms_mem_bwd", need_dx=False)
    dx1, d_norm_xattn = _rms_bwd(dx2, dh2, x1, norm_xattn + tok, name="rms_xattn_bwd")
    dmerged = _matmul(dx1, wfull["w_mix_out"], mode="nt", out_dtype=BF16, name="mm_d_merged", tm=512, tn=1024, tk=1024)
    grads["w_mix_out"] = _matmul(merged, dx1, mode="tn", out_dtype=BF16, name="mm_dw_mix", tm=1024, tn=1024, tk=512)
    dp, dy_a, dy_b, d_b_gate = _gates_bwd(dmerged, p, y_a, y_b, bg_full, name="gates_bwd")
    dsg = _matmul(dy_b, wfull["w_sgu_out"], mode="nt", out_dtype=BF16, name="mm_d_sg", tm=512, tn=1024, tk=1024)
    grads["w_sgu_out"] = _matmul(sg, dy_b, mode="tn", out_dtype=BF16, name="mm_dw_sgu", tm=1024, tn=1024, tk=512)
    da_act = _matmul(dy_a, wfull["w_conv_out"], mode="nt", out_dtype=BF16, name="mm_d_aact", tm=512, tn=1024, tk=1024)
    grads["w_conv_out"] = _matmul(a_act, dy_a, mode="tn", out_dtype=BF16, name="mm_dw_conv", tm=1024, tn=1024, tk=512)
    tok = send_grads(["w_mix_out", "w_sgu_out", "w_conv_out"], "mixer")
    dp, d_sgu_w, d_sgu_b, d_sgu_ln_g, d_sgu_ln_b = _sgu_bwd(dp, dsg, p, vn, wm, wmt, sgu_bias, sgu_ln_g + tok,
                                                             name="sgu_bwd")
    dc, d_conv_ln_g, d_conv_ln_b = _conv_ln_bwd(da_act, c_conv, conv_ln_g, conv_ln_b, name="conv_ln_bwd")
    dp, d_conv_w, d_conv_b = _conv_bwd(dp, dc, p, cw_full, bl=bl, s=s, name="conv_bwd")
    grads["w_in"] = _matmul(h1, dp, mode="tn", out_dtype=BF16, name="mm_dw_in", tm=1024, tn=768, tk=512,
                            col_blocks=N_DEV)
    tok = send_grads(["w_in"], "in")
    dh1 = _matmul(dp, wfull["w_in"], mode="nt", out_dtype=F32, name="mm_d_h1", tm=512, tn=1024, tk=1024)
    grad_x, d_norm_mix = _rms_bwd(dx1, dh1, xf, norm_mix + tok, name="rms_mix_bwd")
    out = {}

    rep_names = ["norm_mix", "conv_b", "conv_ln_g", "conv_ln_b", "sgu_ln_g", "sgu_ln_b", "norm_xattn", "norm_mem",
                 "norm_ffn", "norm_final", "sgu_b"]
    rep_grads = [d_norm_mix, d_conv_b, d_conv_ln_g, d_conv_ln_b, d_sgu_ln_g, d_sgu_ln_b, d_norm_xattn, d_norm_mem,
                 d_norm_ffn, d_norm_final, d_sgu_b.reshape(1, d)]
    nrep = len(rep_names)
    pad = jnp.zeros((16 - nrep, d), F32)
    sgw_rows = SGU_GROUPS * SGU_CHUNK * SGU_CHUNK // d

    def pack_rep(vecs, sgw):
        return jnp.concatenate([v.reshape(1, d) for v in vecs] + [pad, sgw.reshape(sgw_rows, d)], axis=0)

    def pack_col(bg, cw):
        return jnp.concatenate([_pad_rows(bg, 8), _pad_rows(cw, CONV_HALO)], axis=0)

    small_a = pack_rep(rep_grads, d_sgu_w)
    small_b = jnp.concatenate([d_b_gate, d_conv_w], axis=0)
    parts_a, parts_b = _all_gather([small_a, small_b], name="gather_small_grads")
    res_a = _adamw_small(parts_a, dev_id, pack_rep([given[n] for n in rep_names], sgu_w),
                         pack_rep([given["m_" + n] for n in rep_names], m_sgu_w),
                         pack_rep([given["v_" + n] for n in rep_names], v_sgu_w), name="adamw_small", col_block=False)
    res_b = _adamw_small(parts_b, dev_id, pack_col(b_gate[0], conv_w[0]), pack_col(m_b_gate[0], m_conv_w[0]),
                         pack_col(v_b_gate[0], v_conv_w[0]), name="adamw_small_cols", col_block=True)
    for i, n in enumerate(rep_names):
        out[n] = [r[i].reshape(given[n].shape) for r in res_a]
    out["sgu_w"] = [r[16:16 + sgw_rows].reshape(sgu_w.shape) for r in res_a]
    out["b_gate"] = [r[0:2][None] for r in res_b]
    out["conv_w"] = [r[8:8 + CONV_WIDTH][None] for r in res_b]

    for names, ssem, rsem, srcs, lands in sent:
        srcs, lands = _copy_wait("scatter", ssem, rsem, srcs, lands, res_a[0], name=f"grads_{names[0]}_wait")
        for n, partials, landed in zip(names, srcs, lands):
            res = _adamw_shard(partials, landed, dev_id, given[n][0], given["m_" + n][0], given["v_" + n][0],
                               name=f"adamw_{n}")
            out[n] = [r[None] for r in res]

    order = ["norm_mix", "w_in", "b_gate", "conv_w", "conv_b", "conv_ln_g", "conv_ln_b", "w_conv_out", "sgu_ln_g",
             "sgu_ln_b", "sgu_w", "sgu_b", "w_sgu_out", "w_mix_out", "norm_xattn", "norm_mem", "w_q", "w_kv", "w_xo",
             "norm_ffn", "w_gu", "w_down", "norm_final"]
    return (loss, grad_x.reshape(x.shape), *[out[n][0] for n in order], *[out[n][1] for n in order],
            *[out[n][2] for n in order], *[out[n][3] for n in order])
```

```python
import functools

import jax
import jax.numpy as jnp
from jax import lax
from jax.experimental import pallas as pl
from jax.experimental.pallas import tpu as pltpu

F32 = jnp.float32
BF16 = jnp.bfloat16
RMS_EPS = 1e-6
LN_EPS = 1e-5
CONV_WIDTH = 31
CONV_HALO = 32
CONV_ROWS = 64
CONV_COLS = 256
LANES = 128
SGU_CHUNK = 128
SGU_GROUPS = 8
HEADS = 4
N_DEV = 8
ADAM_LR, ADAM_B1, ADAM_B2, ADAM_EPS, ADAM_WD, ADAM_STEP = 0.001, 0.9, 0.999, 1e-08, 0.01, 10
VMEM_LIMIT = 56 * 1024 * 1024
TOKEN_TILE = 256
MESH_ID = pl.DeviceIdType.MESH

_GELU_K = 0.7978845608028654
_GELU_C = 0.044715


def _cparams(sem=None):
    return pltpu.CompilerParams(dimension_semantics=sem, vmem_limit_bytes=VMEM_LIMIT)


def _sigmoid(v):
    return 1.0 / (1.0 + jnp.exp(-v))


def _gelu(v):
    return 0.5 * v * (1.0 + jnp.tanh(_GELU_K * (v + _GELU_C * v * v * v)))


def _gelu_grad(v):
    th = jnp.tanh(_GELU_K * (v + _GELU_C * v * v * v))
    return 0.5 * (1.0 + th) + 0.5 * v * (1.0 - th * th) * _GELU_K * (1.0 + 3.0 * _GELU_C * v * v)


def _dot(a, b, dims):
    return lax.dot_general(a, b, (dims, ((), ())), preferred_element_type=F32)


_NN = ((1,), (0,))
_NT = ((1,), (1,))
_TN = ((0,), (0,))


def _matmul(a, b, *, mode, out_dtype, name, tm=512, tn=512, tk=512, chunk=None, residual=None, rms_gain=None,
            col_blocks=None):
    if mode == "nn":
        (m, k), (_, n) = a.shape, b.shape
    elif mode == "nt":
        (m, k), (n, _) = a.shape, b.shape
    else:
        (k, m), (_, n) = a.shape, b.shape
    tm, tn, tk = min(tm, m), min(tn, n), min(tk, k)
    assert m % tm == 0 and n % tn == 0 and k % tk == 0, (name, a.shape, b.shape, tm, tn, tk)
    nk = k // tk
    dims = {"nn": _NN, "nt": _NT, "tn": _TN}[mode]
    chunk = tn if chunk is None else min(chunk, tn)
    assert tn % chunk == 0
    if rms_gain is not None:
        assert tn == n and chunk == n

    def body(*refs):
        refs = list(refs)
        a_ref, b_ref = refs[:2]
        pos = 2
        r_ref = g_ref = None
        if residual is not None:
            r_ref = refs[pos]
            pos += 1
        if rms_gain is not None:
            g_ref = refs[pos]
            pos += 1
        o_ref = refs[pos]
        pos += 1
        h_ref = None
        if rms_gain is not None:
            h_ref = refs[pos]
            pos += 1
        acc_ref = refs[pos] if nk > 1 else None
        av = a_ref[...].astype(BF16)
        for c0 in range(0, tn, chunk):
            cs = slice(c0, c0 + chunk)
            bv = (b_ref[cs, :] if mode == "nt" else b_ref[:, cs]).astype(BF16)
            part = _dot(av, bv, dims)

            def finish(res, cs=cs):
                if r_ref is not None:
                    res = res + r_ref[:, cs].astype(F32)
                o_ref[:, cs] = res.astype(out_dtype)
                if h_ref is not None:
                    r = lax.rsqrt(jnp.mean(res * res, axis=-1, keepdims=True) + RMS_EPS)
                    h_ref[...] = (res * r * g_ref[...]).astype(BF16)

            if nk == 1:
                finish(part)
            else:
                kk = pl.program_id(2)

                @pl.when(kk == 0)
                def _(part=part, cs=cs):
                    acc_ref[:, cs] = part

                @pl.when(kk > 0)
                def _(part=part, cs=cs):
                    acc_ref[:, cs] += part

                @pl.when(kk == nk - 1)
                def _(finish=finish, cs=cs):
                    finish(acc_ref[:, cs])

    resident = dict(pipeline_mode=pl.Buffered(1)) if (n == tn and nk == 1 and mode != "tn" and m > tm) else {}
    if mode == "nn":
        a_spec = pl.BlockSpec((tm, tk), lambda i, j, kk: (i, kk))
        b_spec = pl.BlockSpec((tk, tn), lambda i, j, kk: (kk, j), **resident)
    elif mode == "nt":
        a_spec = pl.BlockSpec((tm, tk), lambda i, j, kk: (i, kk))
        b_spec = pl.BlockSpec((tn, tk), lambda i, j, kk: (j, kk), **resident)
    else:
        a_spec = pl.BlockSpec((tk, tm), lambda i, j, kk: (kk, i))
        b_spec = pl.BlockSpec((tk, tn), lambda i, j, kk: (kk, j))
    o_spec = pl.BlockSpec((tm, tn), lambda i, j, kk: (i, j))
    in_specs, args = [a_spec, b_spec], [a, b]
    if residual is not None:
        in_specs.append(o_spec)
        args.append(residual)
    out_shape, out_specs = [jax.ShapeDtypeStruct((m, n), out_dtype)], [o_spec]
    if col_blocks is not None:
        assert residual is None and rms_gain is None and (n // col_blocks) % tn == 0
        per = n // col_blocks // tn
        out_shape = [jax.ShapeDtypeStruct((col_blocks, m, n // col_blocks), out_dtype)]
        out_specs = [pl.BlockSpec((None, tm, tn), lambda i, j, kk: (j // per, i, j % per))]
    if rms_gain is not None:
        in_specs.append(pl.BlockSpec((1, n), lambda i, j, kk: (0, 0)))
        args.append(rms_gain)
        out_shape.append(jax.ShapeDtypeStruct((m, n), BF16))
        out_specs.append(o_spec)
    res = pl.pallas_call(
        body, name=name, grid=(m // tm, n // tn, nk), in_specs=in_specs, out_specs=out_specs, out_shape=out_shape,
        scratch_shapes=[pltpu.VMEM((tm, tn), F32)] if nk > 1 else [],
        compiler_params=_cparams(("parallel", "parallel", "arbitrary")),
    )(*args)
    return res if rms_gain is not None else res[0]


def _row_spec(tt, cols, col_block=0):
    return pl.BlockSpec((tt, cols), lambda i: (i, col_block))


def _const_spec(shape):
    return pl.BlockSpec(shape, lambda *_: (0,) * len(shape))


def _rms_fwd(x, gain, *, name):
    t, d = x.shape
    tt = min(TOKEN_TILE, t)

    def body(x_ref, g_ref, h_ref):
        xv = x_ref[...]
        r = lax.rsqrt(jnp.mean(xv * xv, axis=-1, keepdims=True) + RMS_EPS)
        h_ref[...] = (xv * r * g_ref[...]).astype(BF16)

    return pl.pallas_call(
        body, name=name, grid=(t // tt,), in_specs=[_row_spec(tt, d), _const_spec((1, d))],
        out_specs=_row_spec(tt, d), out_shape=jax.ShapeDtypeStruct((t, d), BF16),
        compiler_params=_cparams(("parallel",)))(x, gain)


def _rms_bwd(dres, dh, x, gain, *, name, need_dx=True):
    t, d = x.shape
    tt = min(TOKEN_TILE, t)

    def body(*refs):
        if need_dx:
            dres_ref, dh_ref, x_ref, g_ref, dx_ref, dg_ref = refs
        else:
            dh_ref, x_ref, g_ref, dg_ref = refs

        @pl.when(pl.program_id(0) == 0)
        def _():
            dg_ref[...] = jnp.zeros_like(dg_ref)

        xv = x_ref[...]
        dhv = dh_ref[...].astype(F32)
        r = lax.rsqrt(jnp.mean(xv * xv, axis=-1, keepdims=True) + RMS_EPS)
        xhat = xv * r
        dg_ref[...] += jnp.sum(dhv * xhat, axis=0, keepdims=True)
        if need_dx:
            dxh = dhv * g_ref[...]
            dx_ref[...] = dres_ref[...] + r * (dxh - xhat * jnp.mean(dxh * xhat, axis=-1, keepdims=True))

    rs = _row_spec(tt, d)
    if need_dx:
        in_specs, args = [rs, rs, rs, _const_spec((1, d))], (dres, dh, x, gain)
        out_specs = [rs, _const_spec((1, d))]
        out_shape = [jax.ShapeDtypeStruct((t, d), F32), jax.ShapeDtypeStruct((1, d), F32)]
    else:
        in_specs, args = [rs, rs, _const_spec((1, d))], (dh, x, gain)
        out_specs = [_const_spec((1, d))]
        out_shape = [jax.ShapeDtypeStruct((1, d), F32)]
    res = pl.pallas_call(body, name=name, grid=(t // tt,), in_specs=in_specs, out_specs=out_specs, out_shape=out_shape,
                         compiler_params=_cparams(("arbitrary",)))(*args)
    return res if need_dx else res[0]


def _final_loss(x3, target, gain, *, name):
    t, d = x3.shape
    tt = min(TOKEN_TILE, t)

    def body(x_ref, t_ref, g_ref, loss_ref, dx_ref, dg_ref):
        @pl.when(pl.program_id(0) == 0)
        def _():
            loss_ref[...] = jnp.zeros_like(loss_ref)
            dg_ref[...] = jnp.zeros_like(dg_ref)

        xv = x_ref[...]
        g = g_ref[...]
        r = lax.rsqrt(jnp.mean(xv * xv, axis=-1, keepdims=True) + RMS_EPS)
        xhat = xv * r
        err = xhat * g - t_ref[...]
        loss_ref[...] += 0.5 * jnp.sum(jnp.mean(err * err, axis=-1, keepdims=True), axis=0, keepdims=True)
        dy = err * (1.0 / d)
        dg_ref[...] += jnp.sum(dy * xhat, axis=0, keepdims=True)
        dxh = dy * g
        dx_ref[...] = r * (dxh - xhat * jnp.mean(dxh * xhat, axis=-1, keepdims=True))

    rs = _row_spec(tt, d)
    return pl.pallas_call(
        body, name=name, grid=(t // tt,), in_specs=[rs, rs, _const_spec((1, d))],
        out_specs=[_const_spec((1, 1)), rs, _const_spec((1, d))],
        out_shape=[jax.ShapeDtypeStruct((1, 1), F32), jax.ShapeDtypeStruct((t, d), F32), jax.ShapeDtypeStruct((1, d), F32)],
        compiler_params=_cparams(("arbitrary",)))(x3, target, gain)


SUBLANES = 8
SHIFT_ROWS = 40


def _conv_apply(sbuf_ref, w_ref, out_ref, tt, offsets, bias_ref=None):
    d = out_ref.shape[1]
    for cc in range(d // LANES):
        cs = slice(cc * LANES, (cc + 1) * LANES)
        taps = [jnp.broadcast_to(w_ref[k:k + 1, cs], (SUBLANES, LANES)) for k in range(CONV_WIDTH)]
        bias = None if bias_ref is None else jnp.broadcast_to(bias_ref[:, cs], (SUBLANES, LANES))

        def row_body(r, carry, cs=cs, taps=taps, bias=bias):
            r0 = pl.multiple_of(r * CONV_ROWS, CONV_ROWS)
            for q in range(CONV_ROWS // SUBLANES):
                acc = _tap(sbuf_ref, r0 + q * SUBLANES, cs, offsets[0]) * taps[0]
                for k in range(1, CONV_WIDTH):
                    acc = acc + _tap(sbuf_ref, r0 + q * SUBLANES, cs, offsets[k]) * taps[k]
                if bias is not None:
                    acc = acc + bias
                out_ref[pl.ds(r0 + q * SUBLANES, SUBLANES), cs] = acc
            return carry

        lax.fori_loop(0, tt // CONV_ROWS, row_body, 0)


def _fill_shifts(sbuf_ref, rows):
    d = sbuf_ref.shape[2]
    assert rows % SHIFT_ROWS == 0

    def row_body(i, carry):
        r0 = pl.multiple_of(i * SHIFT_ROWS, SUBLANES)
        for cc in range(d // CONV_COLS):
            cs = slice(cc * CONV_COLS, (cc + 1) * CONV_COLS)
            win = sbuf_ref[0, pl.ds(r0, SHIFT_ROWS + SUBLANES), cs]
            for sh in range(1, SUBLANES):
                sbuf_ref[sh, pl.ds(r0, SHIFT_ROWS), cs] = win[sh:sh + SHIFT_ROWS, :]
        return carry

    lax.fori_loop(0, rows // SHIFT_ROWS, row_body, 0)


def _tap(sbuf_ref, r0, cs, offset):
    sh = offset % SUBLANES
    return sbuf_ref[sh, pl.ds(pl.multiple_of(r0 + (offset - sh), SUBLANES), SUBLANES), cs]


def _conv_specs(bl, s, tt, d, col_a, col_g):
    nj = s // tt
    per = tt // CONV_HALO
    main_a = pl.BlockSpec((tt, d), lambda b, j: (b * nj + j, col_a))
    main_g = pl.BlockSpec((tt, d), lambda b, j: (b * nj + j, col_g))
    prev = lambda b, j: jnp.maximum((b * nj + j) * per - 1, 0)
    halo_a = pl.BlockSpec((CONV_HALO, d), lambda b, j: (prev(b, j), col_a))
    halo_g = pl.BlockSpec((CONV_HALO, d), lambda b, j: (prev(b, j), col_g))
    return main_a, main_g, halo_a, halo_g


def _fill_glu(sbuf_ref, a_ref, g_ref, ha_ref, hg_ref, tt):
    first = pl.program_id(1) == 0
    ha = ha_ref[...].astype(F32)
    hg = hg_ref[...].astype(F32)
    sbuf_ref[0, pl.ds(0, CONV_HALO), :] = jnp.where(first, 0.0, ha * _sigmoid(hg))
    av = a_ref[...].astype(F32)
    gv = g_ref[...].astype(F32)
    sbuf_ref[0, pl.ds(CONV_HALO, tt), :] = av * _sigmoid(gv)
    _fill_shifts(sbuf_ref, tt + CONV_HALO - SUBLANES)


def _conv_fwd(p, conv_w, conv_b, ln_g, ln_b, *, bl, s, name):
    t = p.shape[0]
    d = conv_w.shape[1]
    tt = min(TOKEN_TILE, s)
    off = CONV_HALO - (CONV_WIDTH - 1)

    def body(a_ref, g_ref, ha_ref, hg_ref, w_ref, b_ref, lg_ref, lb_ref, c_ref, act_ref, sbuf_ref, cbuf_ref):
        _fill_glu(sbuf_ref, a_ref, g_ref, ha_ref, hg_ref, tt)

        _conv_apply(sbuf_ref, w_ref, cbuf_ref, tt, [off + k for k in range(CONV_WIDTH)], bias_ref=b_ref)
        cv = cbuf_ref[...]
        c_ref[...] = cv.astype(BF16)
        mu = jnp.mean(cv, axis=-1, keepdims=True)
        dv = cv - mu
        rstd = lax.rsqrt(jnp.mean(dv * dv, axis=-1, keepdims=True) + LN_EPS)
        aln = dv * rstd * lg_ref[...] + lb_ref[...]
        act_ref[...] = (aln * _sigmoid(aln)).astype(BF16)

    main_a, main_g, halo_a, halo_g = _conv_specs(bl, s, tt, d, 0, 1)
    out_spec = pl.BlockSpec((tt, d), lambda b, j: (b * (s // tt) + j, 0))
    return pl.pallas_call(
        body, name=name, grid=(bl, s // tt),
        in_specs=[main_a, main_g, halo_a, halo_g, _const_spec((CONV_HALO, d)), _const_spec((1, d)), _const_spec((1, d)),
                  _const_spec((1, d))],
        out_specs=[out_spec, out_spec],
        out_shape=[jax.ShapeDtypeStruct((t, d), BF16), jax.ShapeDtypeStruct((t, d), BF16)],
        scratch_shapes=[pltpu.VMEM((SUBLANES, tt + CONV_HALO, d), F32), pltpu.VMEM((tt, d), F32)],
        compiler_params=_cparams(("parallel", "parallel")))(p, p, p, p, conv_w, conv_b, ln_g, ln_b)


def _conv_ln_bwd(dact, c, ln_g, ln_b, *, name):
    t, d = c.shape
    tt = min(TOKEN_TILE, t)

    def body(da_ref, c_ref, lg_ref, lb_ref, dc_ref, dlg_ref, dlb_ref):
        @pl.when(pl.program_id(0) == 0)
        def _():
            dlg_ref[...] = jnp.zeros_like(dlg_ref)
            dlb_ref[...] = jnp.zeros_like(dlb_ref)

        cv = c_ref[...].astype(F32)
        g = lg_ref[...]
        mu = jnp.mean(cv, axis=-1, keepdims=True)
        dv = cv - mu
        rstd = lax.rsqrt(jnp.mean(dv * dv, axis=-1, keepdims=True) + LN_EPS)
        chat = dv * rstd
        aln = chat * g + lb_ref[...]
        sg = _sigmoid(aln)
        daln = da_ref[...].astype(F32) * (sg * (1.0 + aln * (1.0 - sg)))
        dlb_ref[...] += jnp.sum(daln, axis=0, keepdims=True)
        dlg_ref[...] += jnp.sum(daln * chat, axis=0, keepdims=True)
        dchat = daln * g
        dc = rstd * (dchat - jnp.mean(dchat, axis=-1, keepdims=True)
                     - chat * jnp.mean(dchat * chat, axis=-1, keepdims=True))
        dc_ref[...] = dc.astype(BF16)

    rs = _row_spec(tt, d)
    cs = _const_spec((1, d))
    return pl.pallas_call(
        body, name=name, grid=(t // tt,), in_specs=[rs, rs, cs, cs], out_specs=[rs, cs, cs],
        out_shape=[jax.ShapeDtypeStruct((t, d), BF16), jax.ShapeDtypeStruct((1, d), F32), jax.ShapeDtypeStruct((1, d), F32)],
        compiler_params=_cparams(("arbitrary",)))(dact, c, ln_g, ln_b)


def _conv_bwd(dp, dc, p, conv_w, *, bl, s, name):
    t = p.shape[0]
    d = conv_w.shape[1]
    tt = min(TOKEN_TILE, s)
    nj = s // tt
    per = tt // CONV_HALO
    off = CONV_HALO - (CONV_WIDTH - 1)
    last_blk = t // CONV_HALO - 1

    def body(dp_in, dc_ref, dcn_ref, a_ref, g_ref, ha_ref, hg_ref, w_ref, dp_ref, dw_ref, db_ref,
             gbuf_ref, dbuf_ref, dglu_ref, acc_ref):
        del dp_in
        b, j = pl.program_id(0), pl.program_id(1)
        start = jnp.logical_and(b == 0, j == 0)
        end = jnp.logical_and(b == bl - 1, j == nj - 1)

        @pl.when(start)
        def _():
            acc_ref[...] = jnp.zeros_like(acc_ref)
            db_ref[...] = jnp.zeros_like(db_ref)

        _fill_glu(gbuf_ref, a_ref, g_ref, ha_ref, hg_ref, tt)
        dcv = dc_ref[...].astype(F32)
        dbuf_ref[0, pl.ds(0, tt), :] = dcv
        dbuf_ref[0, pl.ds(tt, CONV_HALO), :] = jnp.where(j == nj - 1, 0.0, dcn_ref[...].astype(F32))
        _fill_shifts(dbuf_ref, tt + CONV_HALO - SUBLANES)
        db_ref[...] += jnp.sum(dcv, axis=0, keepdims=True)

        for cc in range(d // LANES):
            cs = slice(cc * LANES, (cc + 1) * LANES)

            def row_body(r, accs, cs=cs):
                r0 = pl.multiple_of(r * CONV_ROWS, CONV_ROWS)
                accs = list(accs)
                for q in range(CONV_ROWS // SUBLANES):
                    dcw = dbuf_ref[0, pl.ds(r0 + q * SUBLANES, SUBLANES), cs]
                    for k in range(CONV_WIDTH):
                        accs[k] = accs[k] + dcw * _tap(gbuf_ref, r0 + q * SUBLANES, cs, off + k)
                return tuple(accs)

            zero = jnp.zeros((SUBLANES, LANES), F32)
            accs = lax.fori_loop(0, tt // CONV_ROWS, row_body, (zero,) * CONV_WIDTH)
            for k in range(CONV_WIDTH):
                acc_ref[k, :, cs] += accs[k]

        _conv_apply(dbuf_ref, w_ref, dglu_ref, tt, [CONV_WIDTH - 1 - k for k in range(CONV_WIDTH)])
        dglu = dglu_ref[...]
        av = a_ref[...].astype(F32)
        sg = _sigmoid(g_ref[...].astype(F32))
        dp_ref[:, 0:d] = (dglu * sg).astype(BF16)
        dp_ref[:, d:2 * d] = (dglu * av * sg * (1.0 - sg)).astype(BF16)

        @pl.when(end)
        def _():
            for k in range(CONV_WIDTH):
                dw_ref[k:k + 1, :] = jnp.sum(acc_ref[k], axis=0, keepdims=True)
            dw_ref[CONV_WIDTH:CONV_HALO, :] = jnp.zeros((CONV_HALO - CONV_WIDTH, d), F32)

    main_a, main_g, halo_a, halo_g = _conv_specs(bl, s, tt, d, 0, 1)
    dc_main = pl.BlockSpec((tt, d), lambda b, j: (b * nj + j, 0))
    dc_next = pl.BlockSpec((CONV_HALO, d), lambda b, j: (jnp.minimum((b * nj + j + 1) * per, last_blk), 0))
    return pl.pallas_call(
        body, name=name, grid=(bl, nj),
        in_specs=[pl.BlockSpec(memory_space=pl.ANY), dc_main, dc_next, main_a, main_g, halo_a, halo_g,
                  _const_spec((CONV_HALO, d))],
        out_specs=[pl.BlockSpec((tt, 2 * d), lambda b, j: (b * nj + j, 0)), _const_spec((CONV_HALO, d)), _const_spec((1, d))],
        out_shape=[jax.ShapeDtypeStruct(dp.shape, dp.dtype), jax.ShapeDtypeStruct((CONV_HALO, d), F32),
                   jax.ShapeDtypeStruct((1, d), F32)],
        scratch_shapes=[pltpu.VMEM((SUBLANES, tt + CONV_HALO, d), F32), pltpu.VMEM((SUBLANES, tt + CONV_HALO, d), F32),
                        pltpu.VMEM((tt, d), F32), pltpu.VMEM((CONV_HALO, SUBLANES, d), F32)],
        input_output_aliases={0: 0},
        compiler_params=_cparams(("arbitrary", "arbitrary")))(dp, dc, dc, p, p, p, p, conv_w)


def _sgu_stats(bv):
    gv = _gelu(bv)
    mu = jnp.mean(gv, axis=-1, keepdims=True)
    dv = gv - mu
    rstd = lax.rsqrt(jnp.mean(dv * dv, axis=-1, keepdims=True) + LN_EPS)
    return dv * rstd, rstd


def _sgu_fwd(p, wm, bias, ln_g, ln_b, *, name):
    t = p.shape[0]
    d = ln_g.shape[1]
    tt = SGU_CHUNK
    gd = d // SGU_GROUPS

    def body(u_ref, v_ref, wm_ref, bias_ref, lg_ref, lb_ref, sg_ref, vn_ref):
        u = _gelu(u_ref[...].astype(F32))
        vhat, _ = _sgu_stats(v_ref[...].astype(F32))
        vb = (vhat * lg_ref[...] + lb_ref[...]).astype(BF16)
        vn_ref[...] = vb
        for g in range(SGU_GROUPS):
            gs = slice(g * gd, (g + 1) * gd)
            z = _dot(wm_ref[g], vb[:, gs], _NN) + bias_ref[g]
            sg_ref[:, gs] = (u[:, gs] * z).astype(BF16)

    rs = _row_spec(tt, d)
    return pl.pallas_call(
        body, name=name, grid=(t // tt,),
        in_specs=[_row_spec(tt, d, 2), _row_spec(tt, d, 3), _const_spec(wm.shape), _const_spec(bias.shape),
                  _const_spec((1, d)), _const_spec((1, d))],
        out_specs=[rs, rs], out_shape=[jax.ShapeDtypeStruct((t, d), BF16), jax.ShapeDtypeStruct((t, d), BF16)],
        compiler_params=_cparams(("parallel",)))(p, p, wm, bias, ln_g, ln_b)


def _sgu_bwd(dp, dsg, p, vn, wm, wmt, bias, ln_g, *, name):
    t = p.shape[0]
    d = ln_g.shape[1]
    tt = SGU_CHUNK
    gd = d // SGU_GROUPS
    nsteps = t // tt

    def body(dp_in, dsg_ref, u_ref, v_ref, vn_ref, wm_ref, wmt_ref, bias_ref, lg_ref,
             dp_ref, dw_ref, dbs_ref, dlg_ref, dlb_ref, dz_acc):
        del dp_in
        i = pl.program_id(0)

        @pl.when(i == 0)
        def _():
            dw_ref[...] = jnp.zeros_like(dw_ref)
            dlg_ref[...] = jnp.zeros_like(dlg_ref)
            dlb_ref[...] = jnp.zeros_like(dlb_ref)
            dz_acc[...] = jnp.zeros_like(dz_acc)

        bu = u_ref[...].astype(F32)
        bv = v_ref[...].astype(F32)
        u = _gelu(bu)
        vhat, rstd = _sgu_stats(bv)
        vb = vn_ref[...]
        dsg = dsg_ref[...].astype(F32)
        row = lax.broadcasted_iota(jnp.int32, (tt, tt), 0)
        col = lax.broadcasted_iota(jnp.int32, (tt, tt), 1)
        causal = col <= row
        du_parts, dv_parts = [], []
        for g in range(SGU_GROUPS):
            gs = slice(g * gd, (g + 1) * gd)
            z = _dot(wm_ref[g], vb[:, gs], _NN) + bias_ref[g]
            du_parts.append(dsg[:, gs] * z)
            dz = dsg[:, gs] * u[:, gs]
            dz_acc[:, gs] += dz
            dzb = dz.astype(BF16)
            dw_ref[g] += jnp.where(causal, _dot(dzb, vb[:, gs], _NT), 0.0)
            dv_parts.append(_dot(wmt_ref[g], dzb, _NN))
        du = jnp.concatenate(du_parts, axis=1)
        dv = jnp.concatenate(dv_parts, axis=1)
        dp_ref[:, 0:d] = (du * _gelu_grad(bu)).astype(BF16)
        dlb_ref[...] += jnp.sum(dv, axis=0, keepdims=True)
        dlg_ref[...] += jnp.sum(dv * vhat, axis=0, keepdims=True)
        dvh = dv * lg_ref[...]
        dgv = rstd * (dvh - jnp.mean(dvh, axis=-1, keepdims=True) - vhat * jnp.mean(dvh * vhat, axis=-1, keepdims=True))
        dp_ref[:, d:2 * d] = (dgv * _gelu_grad(bv)).astype(BF16)

        @pl.when(i == nsteps - 1)
        def _():
            ones = jnp.ones((8, gd), F32)
            for g in range(SGU_GROUPS):
                gs = slice(g * gd, (g + 1) * gd)
                tot = lax.dot_general(ones, dz_acc[:, gs], (_NT, ((), ())), preferred_element_type=F32,
                                      precision=lax.Precision.HIGHEST)
                dbs_ref[g:g + 1, :] = tot[0:1, :]

    rs = _row_spec(tt, d)
    c1 = _const_spec((1, d))
    return pl.pallas_call(
        body, name=name, grid=(nsteps,),
        in_specs=[pl.BlockSpec(memory_space=pl.ANY), rs, _row_spec(tt, d, 2), _row_spec(tt, d, 3), rs,
                  _const_spec(wm.shape), _const_spec(wmt.shape), _const_spec(bias.shape), c1],
        out_specs=[pl.BlockSpec((tt, 2 * d), lambda i: (i, 1)), _const_spec(wm.shape), _const_spec((SGU_GROUPS, tt)), c1, c1],
        out_shape=[jax.ShapeDtypeStruct(dp.shape, dp.dtype), jax.ShapeDtypeStruct(wm.shape, F32),
                   jax.ShapeDtypeStruct((SGU_GROUPS, tt), F32), jax.ShapeDtypeStruct((1, d), F32),
                   jax.ShapeDtypeStruct((1, d), F32)],
        scratch_shapes=[pltpu.VMEM((tt, d), F32)],
        input_output_aliases={0: 0},
        compiler_params=_cparams(("arbitrary",)))(dp, dsg, p, p, vn, wm, wmt, bias, ln_g)


def _gates_fwd(p, ya, yb, b_gate, *, name):
    t, d = ya.shape
    tt = min(TOKEN_TILE, t)

    def body(ga_ref, gb_ref, ya_ref, yb_ref, bg_ref, o_ref):
        sa = _sigmoid(ga_ref[...].astype(F32) + bg_ref[0:1, :])
        sb = _sigmoid(gb_ref[...].astype(F32) + bg_ref[1:2, :])
        o_ref[...] = (sa * ya_ref[...].astype(F32) + sb * yb_ref[...].astype(F32)).astype(BF16)

    rs = _row_spec(tt, d)
    return pl.pallas_call(
        body, name=name, grid=(t // tt,),
        in_specs=[_row_spec(tt, d, 4), _row_spec(tt, d, 5), rs, rs, _const_spec(b_gate.shape)],
        out_specs=rs, out_shape=jax.ShapeDtypeStruct((t, d), BF16),
        compiler_params=_cparams(("parallel",)))(p, p, ya, yb, b_gate)


def _gates_bwd(dmerged, p, ya, yb, b_gate, *, name):
    t, d = ya.shape
    tt = min(TOKEN_TILE, t)

    def body(dm_ref, ga_ref, gb_ref, ya_ref, yb_ref, bg_ref, dp_ref, dya_ref, dyb_ref, dbg_ref):
        @pl.when(pl.program_id(0) == 0)
        def _():
            dbg_ref[...] = jnp.zeros_like(dbg_ref)

        dm = dm_ref[...].astype(F32)
        sa = _sigmoid(ga_ref[...].astype(F32) + bg_ref[0:1, :])
        sb = _sigmoid(gb_ref[...].astype(F32) + bg_ref[1:2, :])
        dya_ref[...] = (dm * sa).astype(BF16)
        dyb_ref[...] = (dm * sb).astype(BF16)
        dga = dm * ya_ref[...].astype(F32) * sa * (1.0 - sa)
        dgb = dm * yb_ref[...].astype(F32) * sb * (1.0 - sb)
        dp_ref[:, 0:d] = dga.astype(BF16)
        dp_ref[:, d:2 * d] = dgb.astype(BF16)
        dbg_ref[0:1, :] += jnp.sum(dga, axis=0, keepdims=True)
        dbg_ref[1:2, :] += jnp.sum(dgb, axis=0, keepdims=True)

    rs = _row_spec(tt, d)
    return pl.pallas_call(
        body, name=name, grid=(t // tt,),
        in_specs=[rs, _row_spec(tt, d, 4), _row_spec(tt, d, 5), rs, rs, _const_spec(b_gate.shape)],
        out_specs=[pl.BlockSpec((tt, 2 * d), lambda i: (i, 2)), rs, rs, _const_spec((8, d))],
        out_shape=[jax.ShapeDtypeStruct(p.shape, BF16), jax.ShapeDtypeStruct((t, d), BF16),
                   jax.ShapeDtypeStruct((t, d), BF16), jax.ShapeDtypeStruct((8, d), F32)],
        compiler_params=_cparams(("arbitrary",)))(dmerged, p, p, ya, yb, b_gate)


def _softmax_rows(s):
    e = jnp.exp(s - jnp.max(s, axis=-1, keepdims=True))
    return e / jnp.sum(e, axis=-1, keepdims=True)


def _attn_fwd(q, kv, *, bl, s, name):
    t, d = q.shape
    mlen = kv.shape[0] // bl
    hd = d // HEADS
    tq = min(TOKEN_TILE, s)
    nq = s // tq
    scale = hd ** -0.5

    def body(q_ref, kv_ref, o_ref):
        for h in range(HEADS):
            hs = slice(h * hd, (h + 1) * hd)
            vs = slice(d + h * hd, d + (h + 1) * hd)
            pr = _softmax_rows(_dot(q_ref[:, hs], kv_ref[:, hs], _NT) * scale)
            o_ref[:, hs] = _dot(pr.astype(BF16), kv_ref[:, vs], _NN).astype(BF16)

    qs = pl.BlockSpec((tq, d), lambda b, j: (b * nq + j, 0))
    return pl.pallas_call(
        body, name=name, grid=(bl, nq), in_specs=[qs, pl.BlockSpec((mlen, 2 * d), lambda b, j: (b, 0))],
        out_specs=qs, out_shape=jax.ShapeDtypeStruct((t, d), BF16),
        compiler_params=_cparams(("parallel", "parallel")))(q, kv)


def _attn_bwd(q, kv, do, *, bl, s, name):
    t, d = q.shape
    mlen = kv.shape[0] // bl
    hd = d // HEADS
    tq = min(TOKEN_TILE, s)
    nq = s // tq
    scale = hd ** -0.5

    def body(q_ref, kv_ref, do_ref, dq_ref, dkv_ref):
        @pl.when(pl.program_id(1) == 0)
        def _():
            dkv_ref[...] = jnp.zeros_like(dkv_ref)

        for h in range(HEADS):
            hs = slice(h * hd, (h + 1) * hd)
            vs = slice(d + h * hd, d + (h + 1) * hd)
            qh, kh, vh, doh = q_ref[:, hs], kv_ref[:, hs], kv_ref[:, vs], do_ref[:, hs]
            pr = _softmax_rows(_dot(qh, kh, _NT) * scale)
            dpr = _dot(doh, vh, _NT)
            dkv_ref[:, vs] += _dot(pr.astype(BF16), doh, _TN)
            ds = (pr * (dpr - jnp.sum(dpr * pr, axis=-1, keepdims=True)) * scale).astype(BF16)
            dq_ref[:, hs] = _dot(ds, kh, _NN).astype(BF16)
            dkv_ref[:, hs] += _dot(ds, qh, _TN)

    qs = pl.BlockSpec((tq, d), lambda b, j: (b * nq + j, 0))
    ks = pl.BlockSpec((mlen, 2 * d), lambda b, j: (b, 0))
    return pl.pallas_call(
        body, name=name, grid=(bl, nq), in_specs=[qs, ks, qs], out_specs=[qs, ks],
        out_shape=[jax.ShapeDtypeStruct((t, d), BF16), jax.ShapeDtypeStruct(kv.shape, F32)],
        compiler_params=_cparams(("parallel", "arbitrary")))(q, kv, do)


def _swiglu_fwd(gu, *, name):
    t, f2 = gu.shape
    f = f2 // 2
    tt = min(TOKEN_TILE, t)

    def body(gu_ref, o_ref):
        gt = gu_ref[:, 0:f].astype(F32)
        up = gu_ref[:, f:f2].astype(F32)
        o_ref[...] = (gt * _sigmoid(gt) * up).astype(BF16)

    return pl.pallas_call(
        body, name=name, grid=(t // tt,), in_specs=[_row_spec(tt, f2)], out_specs=_row_spec(tt, f),
        out_shape=jax.ShapeDtypeStruct((t, f), BF16), compiler_params=_cparams(("parallel",)))(gu)


def _swiglu_bwd(gu, dact, *, name):
    t, f2 = gu.shape
    f = f2 // 2
    tt = min(TOKEN_TILE, t)

    def body(gu_ref, da_ref, o_ref):
        gt = gu_ref[:, 0:f].astype(F32)
        up = gu_ref[:, f:f2].astype(F32)
        da = da_ref[...].astype(F32)
        sg = _sigmoid(gt)
        o_ref[:, 0:f] = (da * up * sg * (1.0 + gt * (1.0 - sg))).astype(BF16)
        o_ref[:, f:f2] = (da * gt * sg).astype(BF16)

    return pl.pallas_call(
        body, name=name, grid=(t // tt,), in_specs=[_row_spec(tt, f2), _row_spec(tt, f)], out_specs=_row_spec(tt, f2),
        out_shape=jax.ShapeDtypeStruct((t, f2), BF16), compiler_params=_cparams(("parallel",)))(gu, dact)


def _mesh_pos():
    return lax.axis_index("x"), lax.axis_index("y"), lax.axis_index("c")


def _all_gather(arrs, *, name):
    n = len(arrs)
    hbm = pl.BlockSpec(memory_space=pl.ANY)

    def body(*refs):
        ins, outs = refs[:n], refs[n:2 * n]
        send_sems, recv_sems, loc_sems = refs[2 * n:]
        x, y, c = _mesh_pos()
        me, sib = (x, y, c), (x, y, 1 - c)
        chips = [(1 - x, y), (x, 1 - y), (1 - x, 1 - y)]

        def idx(dev):
            return 4 * dev[0] + 2 * dev[1] + dev[2]

        def copy(w, k, block, to, from_input=False):
            return pltpu.make_async_remote_copy(
                src_ref=ins[w] if from_input else outs[w].at[idx(block)], dst_ref=outs[w].at[idx(block)],
                send_sem=send_sems.at[w, k], recv_sem=recv_sems.at[w, k], device_id=to, device_id_type=MESH_ID)

        own = [pltpu.make_async_copy(ins[w], outs[w].at[idx(me)], loc_sems.at[w]) for w in range(n)]
        for cp in own:
            cp.start()
        first = []
        for w in range(n):
            first.append(copy(w, 0, me, sib, True))
            first += [copy(w, 1 + j, me, (*chip, c), True) for j, chip in enumerate(chips)]
        for cp in first:
            cp.start()
        passed = []
        for j, chip in enumerate(chips):
            for w in range(n):
                copy(w, 1 + j, (*chip, c), me).wait_recv()
                fwd = copy(w, 4 + j, (*chip, c), sib)
                fwd.start()
                passed.append(fwd)
        for w in range(n):
            copy(w, 0, sib, me).wait_recv()
            for j, chip in enumerate(chips):
                copy(w, 4 + j, (*chip, 1 - c), me).wait_recv()
        for cp in first + passed:
            cp.wait_send()
        for cp in own:
            cp.wait()

    return pl.pallas_call(
        body, name=name, in_specs=[hbm] * n, out_specs=[hbm] * n,
        out_shape=[jax.ShapeDtypeStruct((N_DEV, *a.shape), a.dtype) for a in arrs],
        scratch_shapes=[pltpu.SemaphoreType.DMA((n, 7)), pltpu.SemaphoreType.DMA((n, 7)), pltpu.SemaphoreType.DMA((n,))],
    )(*arrs)


_HBM = pl.BlockSpec(memory_space=pltpu.HBM)
_SEM = pl.BlockSpec(memory_space=pltpu.SEMAPHORE)
_ANY = pl.BlockSpec(memory_space=pl.ANY)
_EFFECT = pltpu.SideEffectType.DATAFLOW_SIDE_EFFECTING
N_PEERS = N_DEV - 1


def _related(pos, r):
    x, y, c = pos
    return (1 - x if r & 4 else x, 1 - y if r & 2 else y, 1 - c if r & 1 else c)


def _dev_index(dev):
    return 4 * dev[0] + 2 * dev[1] + dev[2]


def _in_hbm(a):
    return pltpu.with_memory_space_constraint(a, pltpu.HBM)


def _split_copies(kind, srcs, lands, send_sems, recv_sems):
    pos = _mesh_pos()
    me = _dev_index(pos)
    out = []
    for w in range(len(srcs)):
        for r in range(1, N_DEV):
            peer = _related(pos, r)
            if kind == "gather":
                src, dst_here, dst_there = srcs[w], lands[w].at[_dev_index(peer)], lands[w].at[me]
            else:
                src, dst_here, dst_there = srcs[w].at[_dev_index(peer)], lands[w].at[r - 1], lands[w].at[r - 1]
            out.append((src, dst_here, dst_there, send_sems.at[w * N_PEERS + r - 1], recv_sems.at[w * N_PEERS + r - 1], peer))
    return out


def _copy_start(kind, srcs, land_shapes, *, name):
    n = len(srcs)

    def body(*refs):
        src_refs, land_refs = refs[:n], refs[n:2 * n]
        send_sems, recv_sems = refs[2 * n], refs[2 * n + 1]
        token = refs[-1]
        for src, _, dst, ssem, rsem, peer in _split_copies(kind, src_refs, land_refs, send_sems, recv_sems):
            pltpu.make_async_remote_copy(src_ref=src, dst_ref=dst, send_sem=ssem, recv_sem=rsem, device_id=peer,
                                         device_id_type=MESH_ID).start()
        token[...] = jnp.zeros_like(token)

    lands = [_in_hbm(lax.empty(shape, s.dtype)) for s, shape in zip(srcs, land_shapes)]
    res = pl.pallas_call(
        body, name=name,
        out_shape=(pltpu.SemaphoreType.DMA((n * N_PEERS,)), pltpu.SemaphoreType.DMA((n * N_PEERS,)),
                   *[pltpu.HBM(s.shape, s.dtype) for s in srcs], *[pltpu.HBM(l.shape, l.dtype) for l in lands],
                   jax.ShapeDtypeStruct((8, 128), F32)),
        in_specs=[_HBM] * (2 * n), out_specs=(_SEM, _SEM, *[_HBM] * (2 * n), pl.BlockSpec(memory_space=pltpu.VMEM)),
        input_output_aliases={i: 2 + i for i in range(2 * n)},
        compiler_params=pltpu.CompilerParams(has_side_effects=_EFFECT),
    )(*[_in_hbm(s) for s in srcs], *lands)
    return res[0], res[1], list(res[2:2 + n]), list(res[2 + n:2 + 2 * n]), res[-1]


def _copy_wait(kind, send_sems, recv_sems, srcs, lands, after, *, name):
    n = len(srcs)

    def body(*refs):
        src_refs, land_refs = refs[:n], refs[n:2 * n]
        ssems, rsems = refs[2 * n], refs[2 * n + 1]
        for src, dst, _, ssem, rsem, peer in _split_copies(kind, src_refs, land_refs, ssems, rsems):
            cp = pltpu.make_async_remote_copy(src_ref=src, dst_ref=dst, send_sem=ssem, recv_sem=rsem, device_id=peer,
                                              device_id_type=MESH_ID)
            cp.wait_send()
            cp.wait_recv()

    res = pl.pallas_call(
        body, name=name,
        out_shape=(*[pltpu.HBM(s.shape, s.dtype) for s in srcs], *[pltpu.HBM(l.shape, l.dtype) for l in lands]),
        in_specs=[_HBM] * (2 * n) + [_SEM, _SEM, _ANY], out_specs=tuple([_HBM] * (2 * n)),
        input_output_aliases={i: i for i in range(2 * n)},
        compiler_params=pltpu.CompilerParams(has_side_effects=_EFFECT),
    )(*srcs, *lands, send_sems, recv_sems, after)
    return list(res[:n]), list(res[n:])


def _row_tile(rows):
    return rows if rows <= 512 else 256


def _adamw_math(w, g, m, v):
    m2 = ADAM_B1 * m + (1.0 - ADAM_B1) * g
    v2 = ADAM_B2 * v + (1.0 - ADAM_B2) * (g * g)
    m_hat = m2 / (1.0 - ADAM_B1 ** ADAM_STEP)
    v_hat = v2 / (1.0 - ADAM_B2 ** ADAM_STEP)
    delta = -ADAM_LR * (m_hat / (jnp.sqrt(v_hat) + ADAM_EPS) + ADAM_WD * w)
    return delta, m2, v2


def _adamw_shard(partials, landed, dev, w, m, v, *, name):
    r, c = w.shape
    tr = _row_tile(r)

    def body(dev_ref, p_ref, l_ref, w_ref, m_ref, v_ref, g_out, d_out, m_out, v_out):
        del dev_ref
        g = p_ref[...].astype(F32)
        for k in range(N_PEERS):
            g = g + l_ref[k].astype(F32)
        delta, m2, v2 = _adamw_math(w_ref[...], g, m_ref[...], v_ref[...])
        g_out[...] = g
        d_out[...] = delta
        m_out[...] = m2
        v_out[...] = v2

    blk = pl.BlockSpec((tr, c), lambda i, dev_ref: (i, 0))
    gs = pltpu.PrefetchScalarGridSpec(
        num_scalar_prefetch=1, grid=(r // tr,),
        in_specs=[pl.BlockSpec((None, tr, c), lambda i, dev_ref: (dev_ref[0], i, 0)),
                  pl.BlockSpec((N_PEERS, tr, c), lambda i, dev_ref: (0, i, 0)), blk, blk, blk],
        out_specs=[blk] * 4)
    return pl.pallas_call(
        body, name=name, grid_spec=gs, out_shape=[jax.ShapeDtypeStruct((r, c), F32)] * 4,
        compiler_params=_cparams(("parallel",)))(dev, partials, landed, w, m, v)


def _adamw_small(parts, dev, w, m, v, *, name, col_block):
    _, r, d = parts.shape
    cols = w.shape[1]

    def body(dev_ref, p_ref, w_ref, m_ref, v_ref, g_out, d_out, m_out, v_out):
        del dev_ref
        g = p_ref[0]
        for k in range(1, N_DEV):
            g = g + p_ref[k]
        delta, m2, v2 = _adamw_math(w_ref[...], g, m_ref[...], v_ref[...])
        g_out[...] = g
        d_out[...] = delta
        m_out[...] = m2
        v_out[...] = v2

    blk = pl.BlockSpec((r, cols), lambda i, dev_ref: (0, 0))
    pidx = (lambda i, dev_ref: (0, 0, dev_ref[0])) if col_block else (lambda i, dev_ref: (0, 0, 0))
    gs = pltpu.PrefetchScalarGridSpec(
        num_scalar_prefetch=1, grid=(1,),
        in_specs=[pl.BlockSpec((N_DEV, r, cols), pidx), blk, blk, blk], out_specs=[blk] * 4)
    return pl.pallas_call(
        body, name=name, grid_spec=gs, out_shape=[jax.ShapeDtypeStruct((r, cols), F32)] * 4,
        compiler_params=_cparams(("arbitrary",)))(dev, parts, w, m, v)


def _pad_rows(a, rows):
    return jnp.pad(a, ((0, rows - a.shape[0]), (0, 0)))


def _unblock_cols(g):
    return jnp.transpose(g, (1, 0, 2)).reshape(g.shape[1], N_DEV * g.shape[2])


def _block_cols(full):
    r, c8 = full.shape
    return jnp.transpose(full.reshape(r, N_DEV, c8 // N_DEV), (1, 0, 2))


def kernel(x, mem, norm_mix, w_in, b_gate, conv_w, conv_b, conv_ln_g, conv_ln_b, w_conv_out, sgu_ln_g, sgu_ln_b, sgu_w, sgu_b, w_sgu_out, w_mix_out, norm_xattn, norm_mem, w_q, w_kv, w_xo, norm_ffn, w_gu, w_down, norm_final, loss_target, m_norm_mix, m_w_in, m_b_gate, m_conv_w, m_conv_b, m_conv_ln_g, m_conv_ln_b, m_w_conv_out, m_sgu_ln_g, m_sgu_ln_b, m_sgu_w, m_sgu_b, m_w_sgu_out, m_w_mix_out, m_norm_xattn, m_norm_mem, m_w_q, m_w_kv, m_w_xo, m_norm_ffn, m_w_gu, m_w_down, m_norm_final, v_norm_mix, v_w_in, v_b_gate, v_conv_w, v_conv_b, v_conv_ln_g, v_conv_ln_b, v_w_conv_out, v_sgu_ln_g, v_sgu_ln_b, v_sgu_w, v_sgu_b, v_w_sgu_out, v_w_mix_out, v_norm_xattn, v_norm_mem, v_w_q, v_w_kv, v_w_xo, v_norm_ffn, v_w_gu, v_w_down, v_norm_final):
    given = dict(locals())
    bl, s, d = x.shape
    t = bl * s
    xf = x.reshape(t, d)
    tgt = loss_target.reshape(t, d)
    memf = mem.reshape(bl * mem.shape[1], d)
    cx, cy, cc = lax.axis_index("x"), lax.axis_index("y"), lax.axis_index("c")
    dev = 4 * cx + 2 * cy + cc
    dev_id = dev.astype(jnp.int32).reshape(1)
    col_sharded = ["w_in", "w_kv", "w_gu"]

    def full_weight(name, blocks):
        return _unblock_cols(blocks) if name in col_sharded else blocks.reshape(N_DEV * blocks.shape[1], blocks.shape[2])

    g_in, g_bg, g_cw = _all_gather([w_in[0].astype(BF16), _pad_rows(b_gate[0], 8), _pad_rows(conv_w[0], CONV_HALO)],
                                   name="gather_w_in")
    early = ["w_conv_out", "w_sgu_out", "w_mix_out", "w_q", "w_kv", "w_xo"]
    late = ["w_gu", "w_down"]
    shards = {n: given[n][0].astype(BF16) for n in early + late}
    started = {}
    for grp, names in (("early", early), ("late", late)):
        srcs = [shards[n] for n in names]
        started[grp] = _copy_start("gather", srcs, [(N_DEV, *a.shape) for a in srcs], name=f"gather_{grp}_start")
    token = started["early"][4][0:1, 0:1] + started["late"][4][0:1, 0:1]
    wfull = {"w_in": _unblock_cols(g_in)}
    bg_full = _unblock_cols(g_bg)
    cw_full = _unblock_cols(g_cw)

    def finish_gather(grp, names, after):
        ssem, rsem, srcs, lands, _ = started[grp]
        _, lands = _copy_wait("gather", ssem, rsem, srcs, lands, after, name=f"gather_{grp}_wait")
        for n, land in zip(names, lands):
            wfull[n] = full_weight(n, lax.dynamic_update_index_in_dim(land, shards[n], dev, 0))

    tri = jnp.tril(jnp.ones((SGU_CHUNK, SGU_CHUNK), bool))
    wm32 = jnp.where(tri[None], sgu_w[0], 0.0)
    wm = wm32.astype(BF16)
    wmt = jnp.transpose(wm32, (0, 2, 1)).astype(BF16)
    sgu_bias = jnp.broadcast_to(sgu_b[0][:, :, None], (SGU_GROUPS, SGU_CHUNK, d // SGU_GROUPS))

    h1 = _rms_fwd(xf, norm_mix + token, name="rms_mix")
    p = _matmul(h1, wfull["w_in"], mode="nn", out_dtype=BF16, name="mm_in", tm=512, tn=6144, tk=1024, chunk=1536)
    c_conv, a_act = _conv_fwd(p, cw_full, conv_b, conv_ln_g, conv_ln_b, bl=bl, s=s, name="conv_fwd")
    finish_gather("early", early, a_act)
    y_a = _matmul(a_act, wfull["w_conv_out"], mode="nn", out_dtype=BF16, name="mm_conv_out", tm=1024, tn=1024, tk=1024)
    sg, vn = _sgu_fwd(p, wm, sgu_bias, sgu_ln_g, sgu_ln_b, name="sgu_fwd")
    y_b = _matmul(sg, wfull["w_sgu_out"], mode="nn", out_dtype=BF16, name="mm_sgu_out", tm=1024, tn=1024, tk=1024)
    merged = _gates_fwd(p, y_a, y_b, bg_full, name="gates_fwd")
    x1, h2 = _matmul(merged, wfull["w_mix_out"], mode="nn", out_dtype=F32, name="mm_mix_out", tm=512, tn=1024, tk=1024,
                     residual=xf, rms_gain=norm_xattn)
    mem_n = _rms_fwd(memf, norm_mem, name="rms_mem")
    q = _matmul(h2, wfull["w_q"], mode="nn", out_dtype=BF16, name="mm_q", tm=1024, tn=1024, tk=1024)
    kv = _matmul(mem_n, wfull["w_kv"], mode="nn", out_dtype=BF16, name="mm_kv", tm=1024, tn=1024, tk=1024)
    o = _attn_fwd(q, kv, bl=bl, s=s, name="attn_fwd")
    x2, h3 = _matmul(o, wfull["w_xo"], mode="nn", out_dtype=F32, name="mm_xo", tm=512, tn=1024, tk=1024,
                     residual=x1, rms_gain=norm_ffn)
    finish_gather("late", late, h3)
    gu = _matmul(h3, wfull["w_gu"], mode="nn", out_dtype=BF16, name="mm_gu", tm=512, tn=5632, tk=1024, chunk=1408)
    act = _swiglu_fwd(gu, name="swiglu_fwd")
    x3 = _matmul(act, wfull["w_down"], mode="nn", out_dtype=F32, name="mm_down", tm=512, tn=1024, tk=2816, residual=x2)
    loss_part, dx3, d_norm_final = _final_loss(x3, tgt, norm_final.reshape(1, d), name="final_loss")
    loss = lax.psum(loss_part[0, 0], ("x", "y", "c"))

    grads = {}
    sent = []

    def send_grads(names, tag):
        blocks = []
        for n in names:
            g = grads[n]
            if g.ndim == 2:
                g = _block_cols(g) if n in col_sharded else g.reshape(N_DEV, -1, g.shape[1])
            blocks.append(g)
        ssem, rsem, srcs, lands, tok = _copy_start("scatter", blocks, [(N_PEERS, *g.shape[1:]) for g in blocks],
                                                   name=f"grads_{tag}_start")
        sent.append((names, ssem, rsem, srcs, lands))
        return tok[0:1, 0:1]

    dact = _matmul(dx3, wfull["w_down"], mode="nt", out_dtype=BF16, name="mm_d_act", tm=512, tn=2816, tk=1024, chunk=1408)
    grads["w_down"] = _matmul(act, dx3, mode="tn", out_dtype=BF16, name="mm_dw_down", tm=1408, tn=1024, tk=1024)
    dgu = _swiglu_bwd(gu, dact, name="swiglu_bwd")
    grads["w_gu"] = _matmul(h3, dgu, mode="tn", out_dtype=BF16, name="mm_dw_gu", tm=1024, tn=1408, tk=1024)
    tok = send_grads(["w_down", "w_gu"], "ffn")
    dh3 = _matmul(dgu, wfull["w_gu"], mode="nt", out_dtype=F32, name="mm_d_h3", tm=512, tn=1024, tk=5632)
    dx2, d_norm_ffn = _rms_bwd(dx3, dh3, x2, norm_ffn + tok, name="rms_ffn_bwd")
    do = _matmul(dx2, wfull["w_xo"], mode="nt", out_dtype=BF16, name="mm_d_o", tm=512, tn=1024, tk=1024)
    grads["w_xo"] = _matmul(o, dx2, mode="tn", out_dtype=BF16, name="mm_dw_xo", tm=1024, tn=1024, tk=1024)
    dq, dkv = _attn_bwd(q, kv, do, bl=bl, s=s, name="attn_bwd")
    grads["w_q"] = _matmul(h2, dq, mode="tn", out_dtype=BF16, name="mm_dw_q", tm=1024, tn=1024, tk=1024)
    grads["w_kv"] = _matmul(mem_n, dkv, mode="tn", out_dtype=BF16, name="mm_dw_kv", tm=1024, tn=256, tk=1024,
                            col_blocks=N_DEV)
    tok = send_grads(["w_xo", "w_q", "w_kv"], "attn")
    dh2 = _matmul(dq, wfull["w_q"], mode="nt", out_dtype=F32, name="mm_d_h2", tm=512, tn=1024, tk=1024)
    dmem_n = _matmul(dkv, wfull["w_kv"], mode="nt", out_dtype=F32, name="mm_d_mem", tm=512, tn=1024, tk=2048)
    d_norm_mem = _rms_bwd(None, dmem_n, memf, norm_mem, name="rms_mem_bwd", need_dx=False)
    dx1, d_norm_xattn = _rms_bwd(dx2, dh2, x1, norm_xattn + tok, name="rms_xattn_bwd")
    dmerged = _matmul(dx1, wfull["w_mix_out"], mode="nt", out_dtype=BF16, name="mm_d_merged", tm=512, tn=1024, tk=1024)
    grads["w_mix_out"] = _matmul(merged, dx1, mode="tn", out_dtype=BF16, name="mm_dw_mix", tm=1024, tn=1024, tk=1024)
    dp, dy_a, dy_b, d_b_gate = _gates_bwd(dmerged, p, y_a, y_b, bg_full, name="gates_bwd")
    dsg = _matmul(dy_b, wfull["w_sgu_out"], mode="nt", out_dtype=BF16, name="mm_d_sg", tm=512, tn=1024, tk=1024)
    grads["w_sgu_out"] = _matmul(sg, dy_b, mode="tn", out_dtype=BF16, name="mm_dw_sgu", tm=1024, tn=1024, tk=1024)
    da_act = _matmul(dy_a, wfull["w_conv_out"], mode="nt", out_dtype=BF16, name="mm_d_aact", tm=512, tn=1024, tk=1024)
    grads["w_conv_out"] = _matmul(a_act, dy_a, mode="tn", out_dtype=BF16, name="mm_dw_conv", tm=1024, tn=1024, tk=1024)
    tok = send_grads(["w_mix_out", "w_sgu_out", "w_conv_out"], "mixer")
    dp, d_sgu_w, d_sgu_b, d_sgu_ln_g, d_sgu_ln_b = _sgu_bwd(dp, dsg, p, vn, wm, wmt, sgu_bias, sgu_ln_g + tok,
                                                             name="sgu_bwd")
    dc, d_conv_ln_g, d_conv_ln_b = _conv_ln_bwd(da_act, c_conv, conv_ln_g, conv_ln_b, name="conv_ln_bwd")
    dp, d_conv_w, d_conv_b = _conv_bwd(dp, dc, p, cw_full, bl=bl, s=s, name="conv_bwd")
    grads["w_in"] = _matmul(h1, dp, mode="tn", out_dtype=BF16, name="mm_dw_in", tm=1024, tn=768, tk=2048,
                            col_blocks=N_DEV)
    tok = send_grads(["w_in"], "in")
    dh1 = _matmul(dp, wfull["w_in"], mode="nt", out_dtype=F32, name="mm_d_h1", tm=512, tn=1024, tk=6144)
    grad_x, d_norm_mix = _rms_bwd(dx1, dh1, xf, norm_mix + tok, name="rms_mix_bwd")
    out = {}

    rep_names = ["norm_mix", "conv_b", "conv_ln_g", "conv_ln_b", "sgu_ln_g", "sgu_ln_b", "norm_xattn", "norm_mem",
                 "norm_ffn", "norm_final", "sgu_b"]
    rep_grads = [d_norm_mix, d_conv_b, d_conv_ln_g, d_conv_ln_b, d_sgu_ln_g, d_sgu_ln_b, d_norm_xattn, d_norm_mem,
                 d_norm_ffn, d_norm_final, d_sgu_b.reshape(1, d)]
    nrep = len(rep_names)
    pad = jnp.zeros((16 - nrep, d), F32)
    sgw_rows = SGU_GROUPS * SGU_CHUNK * SGU_CHUNK // d

    def pack_rep(vecs, sgw):
        return jnp.concatenate([v.reshape(1, d) for v in vecs] + [pad, sgw.reshape(sgw_rows, d)], axis=0)

    def pack_col(bg, cw):
        return jnp.concatenate([_pad_rows(bg, 8), _pad_rows(cw, CONV_HALO)], axis=0)

    small_a = pack_rep(rep_grads, d_sgu_w)
    small_b = jnp.concatenate([d_b_gate, d_conv_w], axis=0)
    parts_a, parts_b = _all_gather([small_a, small_b], name="gather_small_grads")
    res_a = _adamw_small(parts_a, dev_id, pack_rep([given[n] for n in rep_names], sgu_w),
                         pack_rep([given["m_" + n] for n in rep_names], m_sgu_w),
                         pack_rep([given["v_" + n] for n in rep_names], v_sgu_w), name="adamw_small", col_block=False)
    res_b = _adamw_small(parts_b, dev_id, pack_col(b_gate[0], conv_w[0]), pack_col(m_b_gate[0], m_conv_w[0]),
                         pack_col(v_b_gate[0], v_conv_w[0]), name="adamw_small_cols", col_block=True)
    for i, n in enumerate(rep_names):
        out[n] = [r[i].reshape(given[n].shape) for r in res_a]
    out["sgu_w"] = [r[16:16 + sgw_rows].reshape(sgu_w.shape) for r in res_a]
    out["b_gate"] = [r[0:2][None] for r in res_b]
    out["conv_w"] = [r[8:8 + CONV_WIDTH][None] for r in res_b]

    for names, ssem, rsem, srcs, lands in sent:
        srcs, lands = _copy_wait("scatter", ssem, rsem, srcs, lands, res_a[0], name=f"grads_{names[0]}_wait")
        for n, partials, landed in zip(names, srcs, lands):
            res = _adamw_shard(partials, landed, dev_id, given[n][0], given["m_" + n][0], given["v_" + n][0],
                               name=f"adamw_{n}")
            out[n] = [r[None] for r in res]

    order = ["norm_mix", "w_in", "b_gate", "conv_w", "conv_b", "conv_ln_g", "conv_ln_b", "w_conv_out", "sgu_ln_g",
             "sgu_ln_b", "sgu_w", "sgu_b", "w_sgu_out", "w_mix_out", "norm_xattn", "norm_mem", "w_q", "w_kv", "w_xo",
             "norm_ffn", "w_gu", "w_down", "norm_final"]
    return (loss, grad_x.reshape(x.shape), *[out[n][0] for n in order], *[out[n][1] for n in order],
            *[out[n][2] for n in order], *[out[n][3] for n in order])
```

```python
import functools

import jax
import jax.numpy as jnp
from jax import lax
from jax.experimental import pallas as pl
from jax.experimental.pallas import tpu as pltpu

F32 = jnp.float32
BF16 = jnp.bfloat16
RMS_EPS = 1e-6
LN_EPS = 1e-5
CONV_WIDTH = 31
CONV_HALO = 32
CONV_ROWS = 64
CONV_COLS = 256
LANES = 128
SGU_CHUNK = 128
SGU_GROUPS = 8
HEADS = 4
N_DEV = 8
ADAM_LR, ADAM_B1, ADAM_B2, ADAM_EPS, ADAM_WD, ADAM_STEP = 0.001, 0.9, 0.999, 1e-08, 0.01, 10
VMEM_LIMIT = 56 * 1024 * 1024
TOKEN_TILE = 256
MESH_ID = pl.DeviceIdType.MESH

_GELU_K = 0.7978845608028654
_GELU_C = 0.044715


def _cparams(sem=None):
    return pltpu.CompilerParams(dimension_semantics=sem, vmem_limit_bytes=VMEM_LIMIT)


def _sigmoid(v):
    return 1.0 / (1.0 + jnp.exp(-v))


def _gelu(v):
    return 0.5 * v * (1.0 + jnp.tanh(_GELU_K * (v + _GELU_C * v * v * v)))


def _gelu_grad(v):
    th = jnp.tanh(_GELU_K * (v + _GELU_C * v * v * v))
    return 0.5 * (1.0 + th) + 0.5 * v * (1.0 - th * th) * _GELU_K * (1.0 + 3.0 * _GELU_C * v * v)


def _dot(a, b, dims):
    return lax.dot_general(a, b, (dims, ((), ())), preferred_element_type=F32)


_NN = ((1,), (0,))
_NT = ((1,), (1,))
_TN = ((0,), (0,))


def _matmul(a, b, *, mode, out_dtype, name, tm=512, tn=512, tk=512, chunk=None, residual=None, rms_gain=None,
            col_blocks=None):
    if mode == "nn":
        (m, k), (_, n) = a.shape, b.shape
    elif mode == "nt":
        (m, k), (n, _) = a.shape, b.shape
    else:
        (k, m), (_, n) = a.shape, b.shape
    tm, tn, tk = min(tm, m), min(tn, n), min(tk, k)
    assert m % tm == 0 and n % tn == 0 and k % tk == 0, (name, a.shape, b.shape, tm, tn, tk)
    nk = k // tk
    dims = {"nn": _NN, "nt": _NT, "tn": _TN}[mode]
    chunk = tn if chunk is None else min(chunk, tn)
    assert tn % chunk == 0
    if rms_gain is not None:
        assert tn == n and chunk == n

    def body(*refs):
        refs = list(refs)
        a_ref, b_ref = refs[:2]
        pos = 2
        r_ref = g_ref = None
        if residual is not None:
            r_ref = refs[pos]
            pos += 1
        if rms_gain is not None:
            g_ref = refs[pos]
            pos += 1
        o_ref = refs[pos]
        pos += 1
        h_ref = None
        if rms_gain is not None:
            h_ref = refs[pos]
            pos += 1
        acc_ref = refs[pos] if nk > 1 else None
        av = a_ref[...].astype(BF16)
        for c0 in range(0, tn, chunk):
            cs = slice(c0, c0 + chunk)
            bv = (b_ref[cs, :] if mode == "nt" else b_ref[:, cs]).astype(BF16)
            part = _dot(av, bv, dims)

            def finish(res, cs=cs):
                if r_ref is not None:
                    res = res + r_ref[:, cs].astype(F32)
                o_ref[:, cs] = res.astype(out_dtype)
                if h_ref is not None:
                    r = lax.rsqrt(jnp.mean(res * res, axis=-1, keepdims=True) + RMS_EPS)
                    h_ref[...] = (res * r * g_ref[...]).astype(BF16)

            if nk == 1:
                finish(part)
            else:
                kk = pl.program_id(2)

                @pl.when(kk == 0)
                def _(part=part, cs=cs):
                    acc_ref[:, cs] = part

                @pl.when(kk > 0)
                def _(part=part, cs=cs):
                    acc_ref[:, cs] += part

                @pl.when(kk == nk - 1)
                def _(finish=finish, cs=cs):
                    finish(acc_ref[:, cs])

    resident = dict(pipeline_mode=pl.Buffered(1)) if (n == tn and nk == 1 and mode != "tn" and m > tm) else {}
    if mode == "nn":
        a_spec = pl.BlockSpec((tm, tk), lambda i, j, kk: (i, kk))
        b_spec = pl.BlockSpec((tk, tn), lambda i, j, kk: (kk, j), **resident)
    elif mode == "nt":
        a_spec = pl.BlockSpec((tm, tk), lambda i, j, kk: (i, kk))
        b_spec = pl.BlockSpec((tn, tk), lambda i, j, kk: (j, kk), **resident)
    else:
        a_spec = pl.BlockSpec((tk, tm), lambda i, j, kk: (kk, i))
        b_spec = pl.BlockSpec((tk, tn), lambda i, j, kk: (kk, j))
    o_spec = pl.BlockSpec((tm, tn), lambda i, j, kk: (i, j))
    in_specs, args = [a_spec, b_spec], [a, b]
    if residual is not None:
        in_specs.append(o_spec)
        args.append(residual)
    out_shape, out_specs = [jax.ShapeDtypeStruct((m, n), out_dtype)], [o_spec]
    if col_blocks is not None:
        assert residual is None and rms_gain is None and (n // col_blocks) % tn == 0
        per = n // col_blocks // tn
        out_shape = [jax.ShapeDtypeStruct((col_blocks, m, n // col_blocks), out_dtype)]
        out_specs = [pl.BlockSpec((None, tm, tn), lambda i, j, kk: (j // per, i, j % per))]
    if rms_gain is not None:
        in_specs.append(pl.BlockSpec((1, n), lambda i, j, kk: (0, 0)))
        args.append(rms_gain)
        out_shape.append(jax.ShapeDtypeStruct((m, n), BF16))
        out_specs.append(o_spec)
    res = pl.pallas_call(
        body, name=name, grid=(m // tm, n // tn, nk), in_specs=in_specs, out_specs=out_specs, out_shape=out_shape,
        scratch_shapes=[pltpu.VMEM((tm, tn), F32)] if nk > 1 else [],
        compiler_params=_cparams(("parallel", "parallel", "arbitrary")),
    )(*args)
    return res if rms_gain is not None else res[0]


def _row_call(name, t, tm, rows_in, residents, rows_out, accs, body):
    n_in, n_res, n_out = len(rows_in), len(residents), len(rows_out)
    steps = t // tm
    assert t % tm == 0

    def kernel_body(*refs):
        in_refs, res_refs = refs[:n_in], refs[n_in:n_in + n_res]
        out_refs, acc_refs = refs[n_in + n_res:n_in + n_res + n_out], refs[n_in + n_res + n_out:]
        if accs:
            @pl.when(pl.program_id(0) == 0)
            def _():
                for acc in acc_refs:
                    acc[...] = jnp.zeros_like(acc)
        body(in_refs, res_refs, out_refs, acc_refs)

    once = dict(pipeline_mode=pl.Buffered(1)) if steps > 1 else {}
    in_specs = [pl.BlockSpec((tm, cols), lambda i, cb=cb: (i, cb)) for _, cols, cb in rows_in]
    in_specs += [pl.BlockSpec(r.shape, lambda i, nd=r.ndim: (0,) * nd, **once) for r in residents]
    out_specs = [pl.BlockSpec((tm, cols), lambda i, cb=cb: (i, cb)) for _, cols, cb, _ in rows_out]
    out_specs += [pl.BlockSpec(shape, lambda i, nd=len(shape): (0,) * nd) for shape, _ in accs]
    out_shape = [jax.ShapeDtypeStruct((t, total), dt) for total, _, _, dt in rows_out]
    out_shape += [jax.ShapeDtypeStruct(shape, dt) for shape, dt in accs]
    return pl.pallas_call(
        kernel_body, name=name, grid=(steps,), in_specs=in_specs, out_specs=out_specs, out_shape=out_shape,
        compiler_params=_cparams(("arbitrary",) if accs else ("parallel",)),
    )(*[a for a, _, _ in rows_in], *residents)


def _rms_apply(xv, gain):
    return xv * lax.rsqrt(jnp.mean(xv * xv, axis=-1, keepdims=True) + RMS_EPS) * gain


def _rms_grad(dres, dh, xv, gain):
    r = lax.rsqrt(jnp.mean(xv * xv, axis=-1, keepdims=True) + RMS_EPS)
    xhat = xv * r
    dxh = dh * gain
    dx = dres + r * (dxh - xhat * jnp.mean(dxh * xhat, axis=-1, keepdims=True))
    return dx, jnp.sum(dh * xhat, axis=0, keepdims=True)


def _in_proj(xf, gain, w_in, *, name):
    t, d = xf.shape
    n = w_in.shape[1]
    chunk = n // 4

    def body(ins, res, outs, accs):
        (x_ref,), (g_ref, w_ref), (h_ref, p_ref) = ins, res, outs
        h = _rms_apply(x_ref[...], g_ref[...]).astype(BF16)
        h_ref[...] = h
        for c0 in range(0, n, chunk):
            p_ref[:, c0:c0 + chunk] = _dot(h, w_ref[:, c0:c0 + chunk], _NN).astype(BF16)

    return _row_call(name, t, min(512, t), [(xf, d, 0)], [gain, w_in], [(d, d, 0, BF16), (n, n, 0, BF16)], [], body)


def _mix_out(p, y_a, y_b, b_gate, xf, w_mix, gain, w_q, *, name):
    t, d = xf.shape

    def body(ins, res, outs, accs):
        ga_ref, gb_ref, ya_ref, yb_ref, x_ref = ins
        bg_ref, wm_ref, g_ref, wq_ref = res
        m_ref, x1_ref, h_ref, q_ref = outs
        sa = _sigmoid(ga_ref[...].astype(F32) + bg_ref[0:1, :])
        sb = _sigmoid(gb_ref[...].astype(F32) + bg_ref[1:2, :])
        merged = (sa * ya_ref[...].astype(F32) + sb * yb_ref[...].astype(F32)).astype(BF16)
        m_ref[...] = merged
        x1 = x_ref[...] + _dot(merged, wm_ref[...], _NN)
        x1_ref[...] = x1
        h = _rms_apply(x1, g_ref[...]).astype(BF16)
        h_ref[...] = h
        q_ref[...] = _dot(h, wq_ref[...], _NN).astype(BF16)

    return _row_call(name, t, min(512, t), [(p, d, 4), (p, d, 5), (y_a, d, 0), (y_b, d, 0), (xf, d, 0)],
                     [b_gate, w_mix, gain, w_q], [(d, d, 0, BF16), (d, d, 0, F32), (d, d, 0, BF16), (d, d, 0, BF16)], [], body)


def _ffn_fwd(h3, x2, target, w_gu, w_down, gain, *, name):
    t, d = x2.shape
    f2 = w_gu.shape[1]
    f = f2 // 2
    half = f // 2

    def body(ins, res, outs, accs):
        h_ref, x2_ref, t_ref = ins
        wgu_ref, wd_ref, g_ref = res
        gu_ref, act_ref, dx_ref = outs
        loss_ref, dg_ref = accs
        h = h_ref[...]
        x3 = x2_ref[...]
        for c0 in (0, half):
            gt = _dot(h, wgu_ref[:, c0:c0 + half], _NN).astype(BF16)
            up = _dot(h, wgu_ref[:, f + c0:f + c0 + half], _NN).astype(BF16)
            gu_ref[:, c0:c0 + half] = gt
            gu_ref[:, f + c0:f + c0 + half] = up
            gtf = gt.astype(F32)
            act = (gtf * _sigmoid(gtf) * up.astype(F32)).astype(BF16)
            act_ref[:, c0:c0 + half] = act
            x3 = x3 + _dot(act, wd_ref[c0:c0 + half, :], _NN)
        g = g_ref[...]
        r = lax.rsqrt(jnp.mean(x3 * x3, axis=-1, keepdims=True) + RMS_EPS)
        xhat = x3 * r
        err = xhat * g - t_ref[...]
        loss_ref[...] += 0.5 * jnp.sum(jnp.mean(err * err, axis=-1, keepdims=True), axis=0, keepdims=True)
        dy = err * (1.0 / d)
        dg_ref[...] += jnp.sum(dy * xhat, axis=0, keepdims=True)
        dxh = dy * g
        dx_ref[...] = r * (dxh - xhat * jnp.mean(dxh * xhat, axis=-1, keepdims=True))

    return _row_call(name, t, min(256, t), [(h3, d, 0), (x2, d, 0), (target, d, 0)], [w_gu, w_down, gain],
                     [(f2, f2, 0, BF16), (f, f, 0, BF16), (d, d, 0, F32)], [((1, 1), F32), ((1, d), F32)], body)


def _ffn_bwd(dx3, gu, x2, w_down, w_gu, gain, w_xo, *, name):
    t, d = x2.shape
    f2 = w_gu.shape[1]
    f = f2 // 2
    half = f // 2

    def body(ins, res, outs, accs):
        dx3_ref, gu_ref, x2_ref = ins
        wd_ref, wgu_ref, g_ref, wxo_ref = res
        dgu_ref, dx2_ref, do_ref = outs
        (dg_ref,) = accs
        dx3v = dx3_ref[...]
        dxb = dx3v.astype(BF16)
        dh = jnp.zeros(dx3v.shape, F32)
        for c0 in (0, half):
            dact = _dot(dxb, wd_ref[c0:c0 + half, :], _NT)
            gt = gu_ref[:, c0:c0 + half].astype(F32)
            up = gu_ref[:, f + c0:f + c0 + half].astype(F32)
            sg = _sigmoid(gt)
            dgt = (dact * up * sg * (1.0 + gt * (1.0 - sg))).astype(BF16)
            dup = (dact * gt * sg).astype(BF16)
            dgu_ref[:, c0:c0 + half] = dgt
            dgu_ref[:, f + c0:f + c0 + half] = dup
            dh = dh + _dot(dgt, wgu_ref[:, c0:c0 + half], _NT) + _dot(dup, wgu_ref[:, f + c0:f + c0 + half], _NT)
        dx2, dg = _rms_grad(dx3v, dh, x2_ref[...], g_ref[...])
        dx2_ref[...] = dx2
        dg_ref[...] += dg
        do_ref[...] = _dot(dx2.astype(BF16), wxo_ref[...], _NT).astype(BF16)

    return _row_call(name, t, min(256, t), [(dx3, d, 0), (gu, f2, 0), (x2, d, 0)], [w_down, w_gu, gain, w_xo],
                     [(f2, f2, 0, BF16), (d, d, 0, F32), (d, d, 0, BF16)], [((1, d), F32)], body)


def _proj_rms_bwd(dy, dres, x, w, gain, *, name):
    t, d = x.shape
    k = dy.shape[1]

    def body(ins, res, outs, accs):
        dy_ref, dres_ref, x_ref = ins
        w_ref, g_ref = res
        dh = _dot(dy_ref[...], w_ref[...], _NT)
        dx, dg = _rms_grad(dres_ref[...], dh, x_ref[...], g_ref[...])
        outs[0][...] = dx
        accs[0][...] += dg

    return _row_call(name, t, min(512, t), [(dy, k, 0), (dres, d, 0), (x, d, 0)], [w, gain], [(d, d, 0, F32)],
                     [((1, d), F32)], body)


def _gates_bwd_fused(dx1, p, y_a, y_b, b_gate, w_mix, *, name):
    t, d = y_a.shape

    def body(ins, res, outs, accs):
        dx_ref, ga_ref, gb_ref, ya_ref, yb_ref = ins
        bg_ref, wm_ref = res
        dp_ref, dya_ref, dyb_ref = outs
        (dbg_ref,) = accs
        dm = _dot(dx_ref[...].astype(BF16), wm_ref[...], _NT)
        sa = _sigmoid(ga_ref[...].astype(F32) + bg_ref[0:1, :])
        sb = _sigmoid(gb_ref[...].astype(F32) + bg_ref[1:2, :])
        dya_ref[...] = (dm * sa).astype(BF16)
        dyb_ref[...] = (dm * sb).astype(BF16)
        dga = dm * ya_ref[...].astype(F32) * sa * (1.0 - sa)
        dgb = dm * yb_ref[...].astype(F32) * sb * (1.0 - sb)
        dp_ref[:, 0:d] = dga.astype(BF16)
        dp_ref[:, d:2 * d] = dgb.astype(BF16)
        dbg_ref[0:1, :] += jnp.sum(dga, axis=0, keepdims=True)
        dbg_ref[1:2, :] += jnp.sum(dgb, axis=0, keepdims=True)

    return _row_call(name, t, min(512, t), [(dx1, d, 0), (p, d, 4), (p, d, 5), (y_a, d, 0), (y_b, d, 0)], [b_gate, w_mix],
                     [(p.shape[1], 2 * d, 2, BF16), (d, d, 0, BF16), (d, d, 0, BF16)], [((8, d), F32)], body)


def _conv_ln_bwd_fused(dy_a, c, w_conv_out, ln_g, ln_b, *, name):
    t, d = c.shape

    def body(ins, res, outs, accs):
        dy_ref, c_ref = ins
        w_ref, lg_ref, lb_ref = res
        dlg_ref, dlb_ref = accs
        dact = _dot(dy_ref[...], w_ref[...], _NT)
        cv = c_ref[...].astype(F32)
        g = lg_ref[...]
        mu = jnp.mean(cv, axis=-1, keepdims=True)
        dv = cv - mu
        rstd = lax.rsqrt(jnp.mean(dv * dv, axis=-1, keepdims=True) + LN_EPS)
        chat = dv * rstd
        aln = chat * g + lb_ref[...]
        sg = _sigmoid(aln)
        daln = dact * (sg * (1.0 + aln * (1.0 - sg)))
        dlb_ref[...] += jnp.sum(daln, axis=0, keepdims=True)
        dlg_ref[...] += jnp.sum(daln * chat, axis=0, keepdims=True)
        dchat = daln * g
        dc = rstd * (dchat - jnp.mean(dchat, axis=-1, keepdims=True)
                     - chat * jnp.mean(dchat * chat, axis=-1, keepdims=True))
        outs[0][...] = dc.astype(BF16)

    return _row_call(name, t, min(512, t), [(dy_a, d, 0), (c, d, 0)], [w_conv_out, ln_g, ln_b], [(d, d, 0, BF16)],
                     [((1, d), F32), ((1, d), F32)], body)


def _row_spec(tt, cols, col_block=0):
    return pl.BlockSpec((tt, cols), lambda i: (i, col_block))


def _const_spec(shape):
    return pl.BlockSpec(shape, lambda *_: (0,) * len(shape))


def _rms_fwd(x, gain, *, name):
    t, d = x.shape
    tt = min(TOKEN_TILE, t)

    def body(x_ref, g_ref, h_ref):
        xv = x_ref[...]
        r = lax.rsqrt(jnp.mean(xv * xv, axis=-1, keepdims=True) + RMS_EPS)
        h_ref[...] = (xv * r * g_ref[...]).astype(BF16)

    return pl.pallas_call(
        body, name=name, grid=(t // tt,), in_specs=[_row_spec(tt, d), _const_spec((1, d))],
        out_specs=_row_spec(tt, d), out_shape=jax.ShapeDtypeStruct((t, d), BF16),
        compiler_params=_cparams(("parallel",)))(x, gain)


def _rms_bwd(dres, dh, x, gain, *, name, need_dx=True):
    t, d = x.shape
    tt = min(TOKEN_TILE, t)

    def body(*refs):
        if need_dx:
            dres_ref, dh_ref, x_ref, g_ref, dx_ref, dg_ref = refs
        else:
            dh_ref, x_ref, g_ref, dg_ref = refs

        @pl.when(pl.program_id(0) == 0)
        def _():
            dg_ref[...] = jnp.zeros_like(dg_ref)

        xv = x_ref[...]
        dhv = dh_ref[...].astype(F32)
        r = lax.rsqrt(jnp.mean(xv * xv, axis=-1, keepdims=True) + RMS_EPS)
        xhat = xv * r
        dg_ref[...] += jnp.sum(dhv * xhat, axis=0, keepdims=True)
        if need_dx:
            dxh = dhv * g_ref[...]
            dx_ref[...] = dres_ref[...] + r * (dxh - xhat * jnp.mean(dxh * xhat, axis=-1, keepdims=True))

    rs = _row_spec(tt, d)
    if need_dx:
        in_specs, args = [rs, rs, rs, _const_spec((1, d))], (dres, dh, x, gain)
        out_specs = [rs, _const_spec((1, d))]
        out_shape = [jax.ShapeDtypeStruct((t, d), F32), jax.ShapeDtypeStruct((1, d), F32)]
    else:
        in_specs, args = [rs, rs, _const_spec((1, d))], (dh, x, gain)
        out_specs = [_const_spec((1, d))]
        out_shape = [jax.ShapeDtypeStruct((1, d), F32)]
    res = pl.pallas_call(body, name=name, grid=(t // tt,), in_specs=in_specs, out_specs=out_specs, out_shape=out_shape,
                         compiler_params=_cparams(("arbitrary",)))(*args)
    return res if need_dx else res[0]


def _final_loss(x3, target, gain, *, name):
    t, d = x3.shape
    tt = min(TOKEN_TILE, t)

    def body(x_ref, t_ref, g_ref, loss_ref, dx_ref, dg_ref):
        @pl.when(pl.program_id(0) == 0)
        def _():
            loss_ref[...] = jnp.zeros_like(loss_ref)
            dg_ref[...] = jnp.zeros_like(dg_ref)

        xv = x_ref[...]
        g = g_ref[...]
        r = lax.rsqrt(jnp.mean(xv * xv, axis=-1, keepdims=True) + RMS_EPS)
        xhat = xv * r
        err = xhat * g - t_ref[...]
        loss_ref[...] += 0.5 * jnp.sum(jnp.mean(err * err, axis=-1, keepdims=True), axis=0, keepdims=True)
        dy = err * (1.0 / d)
        dg_ref[...] += jnp.sum(dy * xhat, axis=0, keepdims=True)
        dxh = dy * g
        dx_ref[...] = r * (dxh - xhat * jnp.mean(dxh * xhat, axis=-1, keepdims=True))

    rs = _row_spec(tt, d)
    return pl.pallas_call(
        body, name=name, grid=(t // tt,), in_specs=[rs, rs, _const_spec((1, d))],
        out_specs=[_const_spec((1, 1)), rs, _const_spec((1, d))],
        out_shape=[jax.ShapeDtypeStruct((1, 1), F32), jax.ShapeDtypeStruct((t, d), F32), jax.ShapeDtypeStruct((1, d), F32)],
        compiler_params=_cparams(("arbitrary",)))(x3, target, gain)


SUBLANES = 8
SHIFT_ROWS = 40


def _conv_apply(sbuf_ref, w_ref, out_ref, tt, offsets, bias_ref=None):
    d = out_ref.shape[1]
    for cc in range(d // LANES):
        cs = slice(cc * LANES, (cc + 1) * LANES)
        taps = [jnp.broadcast_to(w_ref[k:k + 1, cs], (SUBLANES, LANES)) for k in range(CONV_WIDTH)]
        bias = None if bias_ref is None else jnp.broadcast_to(bias_ref[:, cs], (SUBLANES, LANES))

        def row_body(r, carry, cs=cs, taps=taps, bias=bias):
            r0 = pl.multiple_of(r * CONV_ROWS, CONV_ROWS)
            for q in range(CONV_ROWS // SUBLANES):
                acc = _tap(sbuf_ref, r0 + q * SUBLANES, cs, offsets[0]) * taps[0]
                for k in range(1, CONV_WIDTH):
                    acc = acc + _tap(sbuf_ref, r0 + q * SUBLANES, cs, offsets[k]) * taps[k]
                if bias is not None:
                    acc = acc + bias
                out_ref[pl.ds(r0 + q * SUBLANES, SUBLANES), cs] = acc
            return carry

        lax.fori_loop(0, tt // CONV_ROWS, row_body, 0)


def _fill_shifts(sbuf_ref, rows):
    d = sbuf_ref.shape[2]
    assert rows % SHIFT_ROWS == 0

    def row_body(i, carry):
        r0 = pl.multiple_of(i * SHIFT_ROWS, SUBLANES)
        for cc in range(d // CONV_COLS):
            cs = slice(cc * CONV_COLS, (cc + 1) * CONV_COLS)
            win = sbuf_ref[0, pl.ds(r0, SHIFT_ROWS + SUBLANES), cs]
            for sh in range(1, SUBLANES):
                sbuf_ref[sh, pl.ds(r0, SHIFT_ROWS), cs] = win[sh:sh + SHIFT_ROWS, :]
        return carry

    lax.fori_loop(0, rows // SHIFT_ROWS, row_body, 0)


def _tap(sbuf_ref, r0, cs, offset):
    sh = offset % SUBLANES
    return sbuf_ref[sh, pl.ds(pl.multiple_of(r0 + (offset - sh), SUBLANES), SUBLANES), cs]


def _conv_specs(bl, s, tt, d, col_a, col_g):
    nj = s // tt
    per = tt // CONV_HALO
    main_a = pl.BlockSpec((tt, d), lambda b, j: (b * nj + j, col_a))
    main_g = pl.BlockSpec((tt, d), lambda b, j: (b * nj + j, col_g))
    prev = lambda b, j: jnp.maximum((b * nj + j) * per - 1, 0)
    halo_a = pl.BlockSpec((CONV_HALO, d), lambda b, j: (prev(b, j), col_a))
    halo_g = pl.BlockSpec((CONV_HALO, d), lambda b, j: (prev(b, j), col_g))
    return main_a, main_g, halo_a, halo_g


def _fill_glu(sbuf_ref, a_ref, g_ref, ha_ref, hg_ref, tt):
    first = pl.program_id(1) == 0
    ha = ha_ref[...].astype(F32)
    hg = hg_ref[...].astype(F32)
    sbuf_ref[0, pl.ds(0, CONV_HALO), :] = jnp.where(first, 0.0, ha * _sigmoid(hg))
    av = a_ref[...].astype(F32)
    gv = g_ref[...].astype(F32)
    sbuf_ref[0, pl.ds(CONV_HALO, tt), :] = av * _sigmoid(gv)
    _fill_shifts(sbuf_ref, tt + CONV_HALO - SUBLANES)


def _conv_fwd(p, conv_w, conv_b, ln_g, ln_b, *, bl, s, name):
    t = p.shape[0]
    d = conv_w.shape[1]
    tt = min(TOKEN_TILE, s)
    off = CONV_HALO - (CONV_WIDTH - 1)

    def body(a_ref, g_ref, ha_ref, hg_ref, w_ref, b_ref, lg_ref, lb_ref, c_ref, act_ref, sbuf_ref, cbuf_ref):
        _fill_glu(sbuf_ref, a_ref, g_ref, ha_ref, hg_ref, tt)

        _conv_apply(sbuf_ref, w_ref, cbuf_ref, tt, [off + k for k in range(CONV_WIDTH)], bias_ref=b_ref)
        cv = cbuf_ref[...]
        c_ref[...] = cv.astype(BF16)
        mu = jnp.mean(cv, axis=-1, keepdims=True)
        dv = cv - mu
        rstd = lax.rsqrt(jnp.mean(dv * dv, axis=-1, keepdims=True) + LN_EPS)
        aln = dv * rstd * lg_ref[...] + lb_ref[...]
        act_ref[...] = (aln * _sigmoid(aln)).astype(BF16)

    main_a, main_g, halo_a, halo_g = _conv_specs(bl, s, tt, d, 0, 1)
    out_spec = pl.BlockSpec((tt, d), lambda b, j: (b * (s // tt) + j, 0))
    return pl.pallas_call(
        body, name=name, grid=(bl, s // tt),
        in_specs=[main_a, main_g, halo_a, halo_g, _const_spec((CONV_HALO, d)), _const_spec((1, d)), _const_spec((1, d)),
                  _const_spec((1, d))],
        out_specs=[out_spec, out_spec],
        out_shape=[jax.ShapeDtypeStruct((t, d), BF16), jax.ShapeDtypeStruct((t, d), BF16)],
        scratch_shapes=[pltpu.VMEM((SUBLANES, tt + CONV_HALO, d), F32), pltpu.VMEM((tt, d), F32)],
        compiler_params=_cparams(("parallel", "parallel")))(p, p, p, p, conv_w, conv_b, ln_g, ln_b)


def _conv_ln_bwd(dact, c, ln_g, ln_b, *, name):
    t, d = c.shape
    tt = min(TOKEN_TILE, t)

    def body(da_ref, c_ref, lg_ref, lb_ref, dc_ref, dlg_ref, dlb_ref):
        @pl.when(pl.program_id(0) == 0)
        def _():
            dlg_ref[...] = jnp.zeros_like(dlg_ref)
            dlb_ref[...] = jnp.zeros_like(dlb_ref)

        cv = c_ref[...].astype(F32)
        g = lg_ref[...]
        mu = jnp.mean(cv, axis=-1, keepdims=True)
        dv = cv - mu
        rstd = lax.rsqrt(jnp.mean(dv * dv, axis=-1, keepdims=True) + LN_EPS)
        chat = dv * rstd
        aln = chat * g + lb_ref[...]
        sg = _sigmoid(aln)
        daln = da_ref[...].astype(F32) * (sg * (1.0 + aln * (1.0 - sg)))
        dlb_ref[...] += jnp.sum(daln, axis=0, keepdims=True)
        dlg_ref[...] += jnp.sum(daln * chat, axis=0, keepdims=True)
        dchat = daln * g
        dc = rstd * (dchat - jnp.mean(dchat, axis=-1, keepdims=True)
                     - chat * jnp.mean(dchat * chat, axis=-1, keepdims=True))
        dc_ref[...] = dc.astype(BF16)

    rs = _row_spec(tt, d)
    cs = _const_spec((1, d))
    return pl.pallas_call(
        body, name=name, grid=(t // tt,), in_specs=[rs, rs, cs, cs], out_specs=[rs, cs, cs],
        out_shape=[jax.ShapeDtypeStruct((t, d), BF16), jax.ShapeDtypeStruct((1, d), F32), jax.ShapeDtypeStruct((1, d), F32)],
        compiler_params=_cparams(("arbitrary",)))(dact, c, ln_g, ln_b)


def _conv_bwd(dp, dc, p, conv_w, *, bl, s, name):
    t = p.shape[0]
    d = conv_w.shape[1]
    tt = min(TOKEN_TILE, s)
    nj = s // tt
    per = tt // CONV_HALO
    off = CONV_HALO - (CONV_WIDTH - 1)
    last_blk = t // CONV_HALO - 1

    def body(dp_in, dc_ref, dcn_ref, a_ref, g_ref, ha_ref, hg_ref, w_ref, dp_ref, dw_ref, db_ref,
             gbuf_ref, dbuf_ref, dglu_ref, acc_ref):
        del dp_in
        b, j = pl.program_id(0), pl.program_id(1)
        start = jnp.logical_and(b == 0, j == 0)
        end = jnp.logical_and(b == bl - 1, j == nj - 1)

        @pl.when(start)
        def _():
            acc_ref[...] = jnp.zeros_like(acc_ref)
            db_ref[...] = jnp.zeros_like(db_ref)

        _fill_glu(gbuf_ref, a_ref, g_ref, ha_ref, hg_ref, tt)
        dcv = dc_ref[...].astype(F32)
        dbuf_ref[0, pl.ds(0, tt), :] = dcv
        dbuf_ref[0, pl.ds(tt, CONV_HALO), :] = jnp.where(j == nj - 1, 0.0, dcn_ref[...].astype(F32))
        _fill_shifts(dbuf_ref, tt + CONV_HALO - SUBLANES)
        db_ref[...] += jnp.sum(dcv, axis=0, keepdims=True)

        for cc in range(d // LANES):
            cs = slice(cc * LANES, (cc + 1) * LANES)

            def row_body(r, accs, cs=cs):
                r0 = pl.multiple_of(r * CONV_ROWS, CONV_ROWS)
                accs = list(accs)
                for q in range(CONV_ROWS // SUBLANES):
                    dcw = dbuf_ref[0, pl.ds(r0 + q * SUBLANES, SUBLANES), cs]
                    for k in range(CONV_WIDTH):
                        accs[k] = accs[k] + dcw * _tap(gbuf_ref, r0 + q * SUBLANES, cs, off + k)
                return tuple(accs)

            zero = jnp.zeros((SUBLANES, LANES), F32)
            accs = lax.fori_loop(0, tt // CONV_ROWS, row_body, (zero,) * CONV_WIDTH)
            for k in range(CONV_WIDTH):
                acc_ref[k, :, cs] += accs[k]

        _conv_apply(dbuf_ref, w_ref, dglu_ref, tt, [CONV_WIDTH - 1 - k for k in range(CONV_WIDTH)])
        dglu = dglu_ref[...]
        av = a_ref[...].astype(F32)
        sg = _sigmoid(g_ref[...].astype(F32))
        dp_ref[:, 0:d] = (dglu * sg).astype(BF16)
        dp_ref[:, d:2 * d] = (dglu * av * sg * (1.0 - sg)).astype(BF16)

        @pl.when(end)
        def _():
            for k in range(CONV_WIDTH):
                dw_ref[k:k + 1, :] = jnp.sum(acc_ref[k], axis=0, keepdims=True)
            dw_ref[CONV_WIDTH:CONV_HALO, :] = jnp.zeros((CONV_HALO - CONV_WIDTH, d), F32)

    main_a, main_g, halo_a, halo_g = _conv_specs(bl, s, tt, d, 0, 1)
    dc_main = pl.BlockSpec((tt, d), lambda b, j: (b * nj + j, 0))
    dc_next = pl.BlockSpec((CONV_HALO, d), lambda b, j: (jnp.minimum((b * nj + j + 1) * per, last_blk), 0))
    return pl.pallas_call(
        body, name=name, grid=(bl, nj),
        in_specs=[pl.BlockSpec(memory_space=pl.ANY), dc_main, dc_next, main_a, main_g, halo_a, halo_g,
                  _const_spec((CONV_HALO, d))],
        out_specs=[pl.BlockSpec((tt, 2 * d), lambda b, j: (b * nj + j, 0)), _const_spec((CONV_HALO, d)), _const_spec((1, d))],
        out_shape=[jax.ShapeDtypeStruct(dp.shape, dp.dtype), jax.ShapeDtypeStruct((CONV_HALO, d), F32),
                   jax.ShapeDtypeStruct((1, d), F32)],
        scratch_shapes=[pltpu.VMEM((SUBLANES, tt + CONV_HALO, d), F32), pltpu.VMEM((SUBLANES, tt + CONV_HALO, d), F32),
                        pltpu.VMEM((tt, d), F32), pltpu.VMEM((CONV_HALO, SUBLANES, d), F32)],
        input_output_aliases={0: 0},
        compiler_params=_cparams(("arbitrary", "arbitrary")))(dp, dc, dc, p, p, p, p, conv_w)


def _sgu_stats(bv):
    gv = _gelu(bv)
    mu = jnp.mean(gv, axis=-1, keepdims=True)
    dv = gv - mu
    rstd = lax.rsqrt(jnp.mean(dv * dv, axis=-1, keepdims=True) + LN_EPS)
    return dv * rstd, rstd


def _sgu_fwd(p, wm, bias, ln_g, ln_b, *, name):
    t = p.shape[0]
    d = ln_g.shape[1]
    tt = SGU_CHUNK
    gd = d // SGU_GROUPS

    def body(u_ref, v_ref, wm_ref, bias_ref, lg_ref, lb_ref, sg_ref, vn_ref):
        u = _gelu(u_ref[...].astype(F32))
        vhat, _ = _sgu_stats(v_ref[...].astype(F32))
        vb = (vhat * lg_ref[...] + lb_ref[...]).astype(BF16)
        vn_ref[...] = vb
        for g in range(SGU_GROUPS):
            gs = slice(g * gd, (g + 1) * gd)
            z = _dot(wm_ref[g], vb[:, gs], _NN) + bias_ref[g]
            sg_ref[:, gs] = (u[:, gs] * z).astype(BF16)

    rs = _row_spec(tt, d)
    return pl.pallas_call(
        body, name=name, grid=(t // tt,),
        in_specs=[_row_spec(tt, d, 2), _row_spec(tt, d, 3), _const_spec(wm.shape), _const_spec(bias.shape),
                  _const_spec((1, d)), _const_spec((1, d))],
        out_specs=[rs, rs], out_shape=[jax.ShapeDtypeStruct((t, d), BF16), jax.ShapeDtypeStruct((t, d), BF16)],
        compiler_params=_cparams(("parallel",)))(p, p, wm, bias, ln_g, ln_b)


def _sgu_bwd(dp, dy_b, w_out, p, vn, wm, wmt, bias, ln_g, *, name):
    t = p.shape[0]
    d = ln_g.shape[1]
    tt = SGU_CHUNK
    gd = d // SGU_GROUPS
    nsteps = t // tt

    def body(dp_in, dyb_ref, wout_ref, u_ref, v_ref, vn_ref, wm_ref, wmt_ref, bias_ref, lg_ref,
             dp_ref, dw_ref, dbs_ref, dlg_ref, dlb_ref, dz_acc):
        del dp_in
        i = pl.program_id(0)

        @pl.when(i == 0)
        def _():
            dw_ref[...] = jnp.zeros_like(dw_ref)
            dlg_ref[...] = jnp.zeros_like(dlg_ref)
            dlb_ref[...] = jnp.zeros_like(dlb_ref)
            dz_acc[...] = jnp.zeros_like(dz_acc)

        bu = u_ref[...].astype(F32)
        bv = v_ref[...].astype(F32)
        u = _gelu(bu)
        vhat, rstd = _sgu_stats(bv)
        vb = vn_ref[...]
        dsg = _dot(dyb_ref[...], wout_ref[...], _NT)
        row = lax.broadcasted_iota(jnp.int32, (tt, tt), 0)
        col = lax.broadcasted_iota(jnp.int32, (tt, tt), 1)
        causal = col <= row
        du_parts, dv_parts = [], []
        for g in range(SGU_GROUPS):
            gs = slice(g * gd, (g + 1) * gd)
            z = _dot(wm_ref[g], vb[:, gs], _NN) + bias_ref[g]
            du_parts.append(dsg[:, gs] * z)
            dz = dsg[:, gs] * u[:, gs]
            dz_acc[:, gs] += dz
            dzb = dz.astype(BF16)
            dw_ref[g] += jnp.where(causal, _dot(dzb, vb[:, gs], _NT), 0.0)
            dv_parts.append(_dot(wmt_ref[g], dzb, _NN))
        du = jnp.concatenate(du_parts, axis=1)
        dv = jnp.concatenate(dv_parts, axis=1)
        dp_ref[:, 0:d] = (du * _gelu_grad(bu)).astype(BF16)
        dlb_ref[...] += jnp.sum(dv, axis=0, keepdims=True)
        dlg_ref[...] += jnp.sum(dv * vhat, axis=0, keepdims=True)
        dvh = dv * lg_ref[...]
        dgv = rstd * (dvh - jnp.mean(dvh, axis=-1, keepdims=True) - vhat * jnp.mean(dvh * vhat, axis=-1, keepdims=True))
        dp_ref[:, d:2 * d] = (dgv * _gelu_grad(bv)).astype(BF16)

        @pl.when(i == nsteps - 1)
        def _():
            ones = jnp.ones((8, gd), F32)
            for g in range(SGU_GROUPS):
                gs = slice(g * gd, (g + 1) * gd)
                tot = lax.dot_general(ones, dz_acc[:, gs], (_NT, ((), ())), preferred_element_type=F32,
                                      precision=lax.Precision.HIGHEST)
                dbs_ref[g:g + 1, :] = tot[0:1, :]

    rs = _row_spec(tt, d)
    c1 = _const_spec((1, d))
    return pl.pallas_call(
        body, name=name, grid=(nsteps,),
        in_specs=[pl.BlockSpec(memory_space=pl.ANY), rs, _const_spec(w_out.shape), _row_spec(tt, d, 2), _row_spec(tt, d, 3),
                  rs, _const_spec(wm.shape), _const_spec(wmt.shape), _const_spec(bias.shape), c1],
        out_specs=[pl.BlockSpec((tt, 2 * d), lambda i: (i, 1)), _const_spec(wm.shape), _const_spec((SGU_GROUPS, tt)), c1, c1],
        out_shape=[jax.ShapeDtypeStruct(dp.shape, dp.dtype), jax.ShapeDtypeStruct(wm.shape, F32),
                   jax.ShapeDtypeStruct((SGU_GROUPS, tt), F32), jax.ShapeDtypeStruct((1, d), F32),
                   jax.ShapeDtypeStruct((1, d), F32)],
        scratch_shapes=[pltpu.VMEM((tt, d), F32)],
        input_output_aliases={0: 0},
        compiler_params=_cparams(("arbitrary",)))(dp, dy_b, w_out, p, p, vn, wm, wmt, bias, ln_g)


def _gates_fwd(p, ya, yb, b_gate, *, name):
    t, d = ya.shape
    tt = min(TOKEN_TILE, t)

    def body(ga_ref, gb_ref, ya_ref, yb_ref, bg_ref, o_ref):
        sa = _sigmoid(ga_ref[...].astype(F32) + bg_ref[0:1, :])
        sb = _sigmoid(gb_ref[...].astype(F32) + bg_ref[1:2, :])
        o_ref[...] = (sa * ya_ref[...].astype(F32) + sb * yb_ref[...].astype(F32)).astype(BF16)

    rs = _row_spec(tt, d)
    return pl.pallas_call(
        body, name=name, grid=(t // tt,),
        in_specs=[_row_spec(tt, d, 4), _row_spec(tt, d, 5), rs, rs, _const_spec(b_gate.shape)],
        out_specs=rs, out_shape=jax.ShapeDtypeStruct((t, d), BF16),
        compiler_params=_cparams(("parallel",)))(p, p, ya, yb, b_gate)


def _gates_bwd(dmerged, p, ya, yb, b_gate, *, name):
    t, d = ya.shape
    tt = min(TOKEN_TILE, t)

    def body(dm_ref, ga_ref, gb_ref, ya_ref, yb_ref, bg_ref, dp_ref, dya_ref, dyb_ref, dbg_ref):
        @pl.when(pl.program_id(0) == 0)
        def _():
            dbg_ref[...] = jnp.zeros_like(dbg_ref)

        dm = dm_ref[...].astype(F32)
        sa = _sigmoid(ga_ref[...].astype(F32) + bg_ref[0:1, :])
        sb = _sigmoid(gb_ref[...].astype(F32) + bg_ref[1:2, :])
        dya_ref[...] = (dm * sa).astype(BF16)
        dyb_ref[...] = (dm * sb).astype(BF16)
        dga = dm * ya_ref[...].astype(F32) * sa * (1.0 - sa)
        dgb = dm * yb_ref[...].astype(F32) * sb * (1.0 - sb)
        dp_ref[:, 0:d] = dga.astype(BF16)
        dp_ref[:, d:2 * d] = dgb.astype(BF16)
        dbg_ref[0:1, :] += jnp.sum(dga, axis=0, keepdims=True)
        dbg_ref[1:2, :] += jnp.sum(dgb, axis=0, keepdims=True)

    rs = _row_spec(tt, d)
    return pl.pallas_call(
        body, name=name, grid=(t // tt,),
        in_specs=[rs, _row_spec(tt, d, 4), _row_spec(tt, d, 5), rs, rs, _const_spec(b_gate.shape)],
        out_specs=[pl.BlockSpec((tt, 2 * d), lambda i: (i, 2)), rs, rs, _const_spec((8, d))],
        out_shape=[jax.ShapeDtypeStruct(p.shape, BF16), jax.ShapeDtypeStruct((t, d), BF16),
                   jax.ShapeDtypeStruct((t, d), BF16), jax.ShapeDtypeStruct((8, d), F32)],
        compiler_params=_cparams(("arbitrary",)))(dmerged, p, p, ya, yb, b_gate)


def _softmax_rows(s):
    e = jnp.exp(s - jnp.max(s, axis=-1, keepdims=True))
    return e / jnp.sum(e, axis=-1, keepdims=True)


def _attn_fwd(q, kv, x1, w_xo, gain, *, bl, s, name):
    t, d = q.shape
    mlen = kv.shape[0] // bl
    hd = d // HEADS
    tq = min(TOKEN_TILE, s)
    nq = s // tq
    scale = hd ** -0.5

    def body(q_ref, kv_ref, x1_ref, w_ref, g_ref, o_ref, x2_ref, h_ref):
        for h in range(HEADS):
            hs = slice(h * hd, (h + 1) * hd)
            vs = slice(d + h * hd, d + (h + 1) * hd)
            pr = _softmax_rows(_dot(q_ref[:, hs], kv_ref[:, hs], _NT) * scale)
            o_ref[:, hs] = _dot(pr.astype(BF16), kv_ref[:, vs], _NN).astype(BF16)
        x2 = x1_ref[...] + _dot(o_ref[...], w_ref[...], _NN)
        x2_ref[...] = x2
        h_ref[...] = _rms_apply(x2, g_ref[...]).astype(BF16)

    qs = pl.BlockSpec((tq, d), lambda b, j: (b * nq + j, 0))
    return pl.pallas_call(
        body, name=name, grid=(bl, nq),
        in_specs=[qs, pl.BlockSpec((mlen, 2 * d), lambda b, j: (b, 0)), qs, _const_spec(w_xo.shape), _const_spec((1, d))],
        out_specs=[qs, qs, qs],
        out_shape=[jax.ShapeDtypeStruct((t, d), BF16), jax.ShapeDtypeStruct((t, d), F32), jax.ShapeDtypeStruct((t, d), BF16)],
        compiler_params=_cparams(("parallel", "parallel")))(q, kv, x1, w_xo, gain)


def _attn_bwd(q, kv, do, *, bl, s, name):
    t, d = q.shape
    mlen = kv.shape[0] // bl
    hd = d // HEADS
    tq = min(TOKEN_TILE, s)
    nq = s // tq
    scale = hd ** -0.5

    def body(q_ref, kv_ref, do_ref, dq_ref, dkv_ref):
        @pl.when(pl.program_id(1) == 0)
        def _():
            dkv_ref[...] = jnp.zeros_like(dkv_ref)

        for h in range(HEADS):
            hs = slice(h * hd, (h + 1) * hd)
            vs = slice(d + h * hd, d + (h + 1) * hd)
            qh, kh, vh, doh = q_ref[:, hs], kv_ref[:, hs], kv_ref[:, vs], do_ref[:, hs]
            pr = _softmax_rows(_dot(qh, kh, _NT) * scale)
            dpr = _dot(doh, vh, _NT)
            dkv_ref[:, vs] += _dot(pr.astype(BF16), doh, _TN)
            ds = (pr * (dpr - jnp.sum(dpr * pr, axis=-1, keepdims=True)) * scale).astype(BF16)
            dq_ref[:, hs] = _dot(ds, kh, _NN).astype(BF16)
            dkv_ref[:, hs] += _dot(ds, qh, _TN)

    qs = pl.BlockSpec((tq, d), lambda b, j: (b * nq + j, 0))
    ks = pl.BlockSpec((mlen, 2 * d), lambda b, j: (b, 0))
    return pl.pallas_call(
        body, name=name, grid=(bl, nq), in_specs=[qs, ks, qs], out_specs=[qs, ks],
        out_shape=[jax.ShapeDtypeStruct((t, d), BF16), jax.ShapeDtypeStruct(kv.shape, F32)],
        compiler_params=_cparams(("parallel", "arbitrary")))(q, kv, do)


def _swiglu_fwd(gu, *, name):
    t, f2 = gu.shape
    f = f2 // 2
    tt = min(TOKEN_TILE, t)

    def body(gu_ref, o_ref):
        gt = gu_ref[:, 0:f].astype(F32)
        up = gu_ref[:, f:f2].astype(F32)
        o_ref[...] = (gt * _sigmoid(gt) * up).astype(BF16)

    return pl.pallas_call(
        body, name=name, grid=(t // tt,), in_specs=[_row_spec(tt, f2)], out_specs=_row_spec(tt, f),
        out_shape=jax.ShapeDtypeStruct((t, f), BF16), compiler_params=_cparams(("parallel",)))(gu)


def _swiglu_bwd(gu, dact, *, name):
    t, f2 = gu.shape
    f = f2 // 2
    tt = min(TOKEN_TILE, t)

    def body(gu_ref, da_ref, o_ref):
        gt = gu_ref[:, 0:f].astype(F32)
        up = gu_ref[:, f:f2].astype(F32)
        da = da_ref[...].astype(F32)
        sg = _sigmoid(gt)
        o_ref[:, 0:f] = (da * up * sg * (1.0 + gt * (1.0 - sg))).astype(BF16)
        o_ref[:, f:f2] = (da * gt * sg).astype(BF16)

    return pl.pallas_call(
        body, name=name, grid=(t // tt,), in_specs=[_row_spec(tt, f2), _row_spec(tt, f)], out_specs=_row_spec(tt, f2),
        out_shape=jax.ShapeDtypeStruct((t, f2), BF16), compiler_params=_cparams(("parallel",)))(gu, dact)


def _mesh_pos():
    return lax.axis_index("x"), lax.axis_index("y"), lax.axis_index("c")


def _all_gather(arrs, *, name):
    n = len(arrs)
    hbm = pl.BlockSpec(memory_space=pl.ANY)

    def body(*refs):
        ins, outs = refs[:n], refs[n:2 * n]
        send_sems, recv_sems, loc_sems = refs[2 * n:]
        x, y, c = _mesh_pos()
        me, sib = (x, y, c), (x, y, 1 - c)
        chips = [(1 - x, y), (x, 1 - y), (1 - x, 1 - y)]

        def idx(dev):
            return 4 * dev[0] + 2 * dev[1] + dev[2]

        def copy(w, k, block, to, from_input=False):
            return pltpu.make_async_remote_copy(
                src_ref=ins[w] if from_input else outs[w].at[idx(block)], dst_ref=outs[w].at[idx(block)],
                send_sem=send_sems.at[w, k], recv_sem=recv_sems.at[w, k], device_id=to, device_id_type=MESH_ID)

        own = [pltpu.make_async_copy(ins[w], outs[w].at[idx(me)], loc_sems.at[w]) for w in range(n)]
        for cp in own:
            cp.start()
        first = []
        for w in range(n):
            first.append(copy(w, 0, me, sib, True))
            first += [copy(w, 1 + j, me, (*chip, c), True) for j, chip in enumerate(chips)]
        for cp in first:
            cp.start()
        passed = []
        for j, chip in enumerate(chips):
            for w in range(n):
                copy(w, 1 + j, (*chip, c), me).wait_recv()
                fwd = copy(w, 4 + j, (*chip, c), sib)
                fwd.start()
                passed.append(fwd)
        for w in range(n):
            copy(w, 0, sib, me).wait_recv()
            for j, chip in enumerate(chips):
                copy(w, 4 + j, (*chip, 1 - c), me).wait_recv()
        for cp in first + passed:
            cp.wait_send()
        for cp in own:
            cp.wait()

    return pl.pallas_call(
        body, name=name, in_specs=[hbm] * n, out_specs=[hbm] * n,
        out_shape=[jax.ShapeDtypeStruct((N_DEV, *a.shape), a.dtype) for a in arrs],
        scratch_shapes=[pltpu.SemaphoreType.DMA((n, 7)), pltpu.SemaphoreType.DMA((n, 7)), pltpu.SemaphoreType.DMA((n,))],
    )(*arrs)


_HBM = pl.BlockSpec(memory_space=pltpu.HBM)
_SEM = pl.BlockSpec(memory_space=pltpu.SEMAPHORE)
_ANY = pl.BlockSpec(memory_space=pl.ANY)
_EFFECT = pltpu.SideEffectType.DATAFLOW_SIDE_EFFECTING
N_PEERS = N_DEV - 1


def _related(pos, r):
    x, y, c = pos
    return (1 - x if r & 4 else x, 1 - y if r & 2 else y, 1 - c if r & 1 else c)


def _dev_index(dev):
    return 4 * dev[0] + 2 * dev[1] + dev[2]


def _in_hbm(a):
    return pltpu.with_memory_space_constraint(a, pltpu.HBM)


def _split_copies(kind, srcs, lands, send_sems, recv_sems):
    pos = _mesh_pos()
    me = _dev_index(pos)
    out = []
    for w in range(len(srcs)):
        for r in range(1, N_DEV):
            peer = _related(pos, r)
            if kind == "gather":
                src, dst_here, dst_there = srcs[w], lands[w].at[_dev_index(peer)], lands[w].at[me]
            else:
                src, dst_here, dst_there = srcs[w].at[_dev_index(peer)], lands[w].at[r - 1], lands[w].at[r - 1]
            out.append((src, dst_here, dst_there, send_sems.at[w * N_PEERS + r - 1], recv_sems.at[w * N_PEERS + r - 1], peer))
    return out


def _copy_start(kind, srcs, land_shapes, *, name):
    n = len(srcs)

    def body(*refs):
        src_refs, land_refs = refs[:n], refs[n:2 * n]
        send_sems, recv_sems = refs[2 * n], refs[2 * n + 1]
        token = refs[-1]
        for src, _, dst, ssem, rsem, peer in _split_copies(kind, src_refs, land_refs, send_sems, recv_sems):
            pltpu.make_async_remote_copy(src_ref=src, dst_ref=dst, send_sem=ssem, recv_sem=rsem, device_id=peer,
                                         device_id_type=MESH_ID).start()
        token[...] = jnp.zeros_like(token)

    lands = [_in_hbm(lax.empty(shape, s.dtype)) for s, shape in zip(srcs, land_shapes)]
    res = pl.pallas_call(
        body, name=name,
        out_shape=(pltpu.SemaphoreType.DMA((n * N_PEERS,)), pltpu.SemaphoreType.DMA((n * N_PEERS,)),
                   *[pltpu.HBM(s.shape, s.dtype) for s in srcs], *[pltpu.HBM(l.shape, l.dtype) for l in lands],
                   jax.ShapeDtypeStruct((8, 128), F32)),
        in_specs=[_HBM] * (2 * n), out_specs=(_SEM, _SEM, *[_HBM] * (2 * n), pl.BlockSpec(memory_space=pltpu.VMEM)),
        input_output_aliases={i: 2 + i for i in range(2 * n)},
        compiler_params=pltpu.CompilerParams(has_side_effects=_EFFECT),
    )(*[_in_hbm(s) for s in srcs], *lands)
    return res[0], res[1], list(res[2:2 + n]), list(res[2 + n:2 + 2 * n]), res[-1]


def _copy_wait(kind, send_sems, recv_sems, srcs, lands, after, *, name):
    n = len(srcs)

    def body(*refs):
        src_refs, land_refs = refs[:n], refs[n:2 * n]
        ssems, rsems = refs[2 * n], refs[2 * n + 1]
        for src, dst, _, ssem, rsem, peer in _split_copies(kind, src_refs, land_refs, ssems, rsems):
            cp = pltpu.make_async_remote_copy(src_ref=src, dst_ref=dst, send_sem=ssem, recv_sem=rsem, device_id=peer,
                                              device_id_type=MESH_ID)
            cp.wait_send()
            cp.wait_recv()

    res = pl.pallas_call(
        body, name=name,
        out_shape=(*[pltpu.HBM(s.shape, s.dtype) for s in srcs], *[pltpu.HBM(l.shape, l.dtype) for l in lands]),
        in_specs=[_HBM] * (2 * n) + [_SEM, _SEM, _ANY], out_specs=tuple([_HBM] * (2 * n)),
        input_output_aliases={i: i for i in range(2 * n)},
        compiler_params=pltpu.CompilerParams(has_side_effects=_EFFECT),
    )(*srcs, *lands, send_sems, recv_sems, after)
    return list(res[:n]), list(res[n:])


def _row_tile(rows):
    return rows if rows <= 512 else 256


def _adamw_math(w, g, m, v):
    m2 = ADAM_B1 * m + (1.0 - ADAM_B1) * g
    v2 = ADAM_B2 * v + (1.0 - ADAM_B2) * (g * g)
    m_hat = m2 / (1.0 - ADAM_B1 ** ADAM_STEP)
    v_hat = v2 / (1.0 - ADAM_B2 ** ADAM_STEP)
    delta = -ADAM_LR * (m_hat / (jnp.sqrt(v_hat) + ADAM_EPS) + ADAM_WD * w)
    return delta, m2, v2


def _adamw_shard(partials, landed, dev, w, m, v, *, name):
    r, c = w.shape
    tr = _row_tile(r)

    def body(dev_ref, p_ref, l_ref, w_ref, m_ref, v_ref, g_out, d_out, m_out, v_out):
        del dev_ref
        g = p_ref[...].astype(F32)
        for k in range(N_PEERS):
            g = g + l_ref[k].astype(F32)
        delta, m2, v2 = _adamw_math(w_ref[...], g, m_ref[...], v_ref[...])
        g_out[...] = g
        d_out[...] = delta
        m_out[...] = m2
        v_out[...] = v2

    blk = pl.BlockSpec((tr, c), lambda i, dev_ref: (i, 0))
    gs = pltpu.PrefetchScalarGridSpec(
        num_scalar_prefetch=1, grid=(r // tr,),
        in_specs=[pl.BlockSpec((None, tr, c), lambda i, dev_ref: (dev_ref[0], i, 0)),
                  pl.BlockSpec((N_PEERS, tr, c), lambda i, dev_ref: (0, i, 0)), blk, blk, blk],
        out_specs=[blk] * 4)
    return pl.pallas_call(
        body, name=name, grid_spec=gs, out_shape=[jax.ShapeDtypeStruct((r, c), F32)] * 4,
        compiler_params=_cparams(("parallel",)))(dev, partials, landed, w, m, v)


def _adamw_small(parts, dev, w, m, v, *, name, col_block):
    _, r, d = parts.shape
    cols = w.shape[1]

    def body(dev_ref, p_ref, w_ref, m_ref, v_ref, g_out, d_out, m_out, v_out):
        del dev_ref
        g = p_ref[0]
        for k in range(1, N_DEV):
            g = g + p_ref[k]
        delta, m2, v2 = _adamw_math(w_ref[...], g, m_ref[...], v_ref[...])
        g_out[...] = g
        d_out[...] = delta
        m_out[...] = m2
        v_out[...] = v2

    blk = pl.BlockSpec((r, cols), lambda i, dev_ref: (0, 0))
    pidx = (lambda i, dev_ref: (0, 0, dev_ref[0])) if col_block else (lambda i, dev_ref: (0, 0, 0))
    gs = pltpu.PrefetchScalarGridSpec(
        num_scalar_prefetch=1, grid=(1,),
        in_specs=[pl.BlockSpec((N_DEV, r, cols), pidx), blk, blk, blk], out_specs=[blk] * 4)
    return pl.pallas_call(
        body, name=name, grid_spec=gs, out_shape=[jax.ShapeDtypeStruct((r, cols), F32)] * 4,
        compiler_params=_cparams(("arbitrary",)))(dev, parts, w, m, v)


def _pad_rows(a, rows):
    return jnp.pad(a, ((0, rows - a.shape[0]), (0, 0)))


def _unblock_cols(g):
    return jnp.transpose(g, (1, 0, 2)).reshape(g.shape[1], N_DEV * g.shape[2])


def _block_cols(full):
    r, c8 = full.shape
    return jnp.transpose(full.reshape(r, N_DEV, c8 // N_DEV), (1, 0, 2))


def kernel(x, mem, norm_mix, w_in, b_gate, conv_w, conv_b, conv_ln_g, conv_ln_b, w_conv_out, sgu_ln_g, sgu_ln_b, sgu_w, sgu_b, w_sgu_out, w_mix_out, norm_xattn, norm_mem, w_q, w_kv, w_xo, norm_ffn, w_gu, w_down, norm_final, loss_target, m_norm_mix, m_w_in, m_b_gate, m_conv_w, m_conv_b, m_conv_ln_g, m_conv_ln_b, m_w_conv_out, m_sgu_ln_g, m_sgu_ln_b, m_sgu_w, m_sgu_b, m_w_sgu_out, m_w_mix_out, m_norm_xattn, m_norm_mem, m_w_q, m_w_kv, m_w_xo, m_norm_ffn, m_w_gu, m_w_down, m_norm_final, v_norm_mix, v_w_in, v_b_gate, v_conv_w, v_conv_b, v_conv_ln_g, v_conv_ln_b, v_w_conv_out, v_sgu_ln_g, v_sgu_ln_b, v_sgu_w, v_sgu_b, v_w_sgu_out, v_w_mix_out, v_norm_xattn, v_norm_mem, v_w_q, v_w_kv, v_w_xo, v_norm_ffn, v_w_gu, v_w_down, v_norm_final):
    given = dict(locals())
    bl, s, d = x.shape
    t = bl * s
    xf = x.reshape(t, d)
    tgt = loss_target.reshape(t, d)
    memf = mem.reshape(bl * mem.shape[1], d)
    cx, cy, cc = lax.axis_index("x"), lax.axis_index("y"), lax.axis_index("c")
    dev = 4 * cx + 2 * cy + cc
    dev_id = dev.astype(jnp.int32).reshape(1)
    col_sharded = ["w_in", "w_kv", "w_gu"]

    def full_weight(name, blocks):
        return _unblock_cols(blocks) if name in col_sharded else blocks.reshape(N_DEV * blocks.shape[1], blocks.shape[2])

    g_in, g_bg, g_cw = _all_gather([w_in[0].astype(BF16), _pad_rows(b_gate[0], 8), _pad_rows(conv_w[0], CONV_HALO)],
                                   name="gather_w_in")
    early = ["w_conv_out", "w_sgu_out", "w_mix_out", "w_q", "w_kv", "w_xo"]
    late = ["w_gu", "w_down"]
    shards = {n: given[n][0].astype(BF16) for n in early + late}
    started = {}
    for grp, names in (("early", early), ("late", late)):
        srcs = [shards[n] for n in names]
        started[grp] = _copy_start("gather", srcs, [(N_DEV, *a.shape) for a in srcs], name=f"gather_{grp}_start")
    token = started["early"][4][0:1, 0:1] + started["late"][4][0:1, 0:1]
    wfull = {"w_in": _unblock_cols(g_in)}
    bg_full = _unblock_cols(g_bg)
    cw_full = _unblock_cols(g_cw)

    def finish_gather(grp, names, after):
        ssem, rsem, srcs, lands, _ = started[grp]
        _, lands = _copy_wait("gather", ssem, rsem, srcs, lands, after, name=f"gather_{grp}_wait")
        for n, land in zip(names, lands):
            wfull[n] = full_weight(n, lax.dynamic_update_index_in_dim(land, shards[n], dev, 0))

    tri = jnp.tril(jnp.ones((SGU_CHUNK, SGU_CHUNK), bool))
    wm32 = jnp.where(tri[None], sgu_w[0], 0.0)
    wm = wm32.astype(BF16)
    wmt = jnp.transpose(wm32, (0, 2, 1)).astype(BF16)
    sgu_bias = jnp.broadcast_to(sgu_b[0][:, :, None], (SGU_GROUPS, SGU_CHUNK, d // SGU_GROUPS))

    h1, p = _in_proj(xf, norm_mix + token, wfull["w_in"], name="in_proj")
    c_conv, a_act = _conv_fwd(p, cw_full, conv_b, conv_ln_g, conv_ln_b, bl=bl, s=s, name="conv_fwd")
    sg, vn = _sgu_fwd(p, wm, sgu_bias, sgu_ln_g, sgu_ln_b, name="sgu_fwd")
    finish_gather("early", early, a_act[0:16, 0:128] + sg[0:16, 0:128])
    y_a = _matmul(a_act, wfull["w_conv_out"], mode="nn", out_dtype=BF16, name="mm_conv_out", tm=1024, tn=1024, tk=1024)
    y_b = _matmul(sg, wfull["w_sgu_out"], mode="nn", out_dtype=BF16, name="mm_sgu_out", tm=1024, tn=1024, tk=1024)
    merged, x1, h2, q = _mix_out(p, y_a, y_b, bg_full, xf, wfull["w_mix_out"], norm_xattn, wfull["w_q"], name="mix_out")
    mem_n = _rms_fwd(memf, norm_mem, name="rms_mem")
    kv = _matmul(mem_n, wfull["w_kv"], mode="nn", out_dtype=BF16, name="mm_kv", tm=1024, tn=1024, tk=1024)
    o, x2, h3 = _attn_fwd(q, kv, x1, wfull["w_xo"], norm_ffn, bl=bl, s=s, name="attn_fwd")
    finish_gather("late", late, h3)
    gu, act, dx3, loss_part, d_norm_final = _ffn_fwd(h3, x2, tgt, wfull["w_gu"], wfull["w_down"],
                                                     norm_final.reshape(1, d), name="ffn_fwd")
    loss = lax.psum(loss_part[0, 0], ("x", "y", "c"))

    grads = {}
    sent = []

    def send_grads(names, tag):
        blocks = []
        for n in names:
            g = grads[n]
            if g.ndim == 2:
                g = _block_cols(g) if n in col_sharded else g.reshape(N_DEV, -1, g.shape[1])
            blocks.append(g)
        ssem, rsem, srcs, lands, tok = _copy_start("scatter", blocks, [(N_PEERS, *g.shape[1:]) for g in blocks],
                                                   name=f"grads_{tag}_start")
        sent.append((names, ssem, rsem, srcs, lands))
        return tok[0:1, 0:1]

    dgu, dx2, do, d_norm_ffn = _ffn_bwd(dx3, gu, x2, wfull["w_down"], wfull["w_gu"], norm_ffn, wfull["w_xo"], name="ffn_bwd")
    grads["w_down"] = _matmul(act, dx3, mode="tn", out_dtype=BF16, name="mm_dw_down", tm=1408, tn=1024, tk=1024)
    grads["w_gu"] = _matmul(h3, dgu, mode="tn", out_dtype=BF16, name="mm_dw_gu", tm=1024, tn=1408, tk=1024)
    tok = send_grads(["w_down", "w_gu"], "ffn")
    grads["w_xo"] = _matmul(o, dx2, mode="tn", out_dtype=BF16, name="mm_dw_xo", tm=1024, tn=1024, tk=1024)
    dq, dkv = _attn_bwd(q, kv, do, bl=bl, s=s, name="attn_bwd")
    grads["w_q"] = _matmul(h2, dq, mode="tn", out_dtype=BF16, name="mm_dw_q", tm=1024, tn=1024, tk=1024)
    grads["w_kv"] = _matmul(mem_n, dkv, mode="tn", out_dtype=BF16, name="mm_dw_kv", tm=1024, tn=256, tk=1024,
                            col_blocks=N_DEV)
    tok2 = send_grads(["w_xo", "w_q", "w_kv"], "attn")
    dmem_n = _matmul(dkv, wfull["w_kv"], mode="nt", out_dtype=F32, name="mm_d_mem", tm=512, tn=1024, tk=2048)
    d_norm_mem = _rms_bwd(None, dmem_n, memf, norm_mem, name="rms_mem_bwd", need_dx=False)
    dx1, d_norm_xattn = _proj_rms_bwd(dq, dx2, x1, wfull["w_q"], norm_xattn + (tok + tok2), name="q_rms_bwd")
    dp, dy_a, dy_b, d_b_gate = _gates_bwd_fused(dx1, p, y_a, y_b, bg_full, wfull["w_mix_out"], name="gates_bwd")
    grads["w_mix_out"] = _matmul(merged, dx1, mode="tn", out_dtype=BF16, name="mm_dw_mix", tm=1024, tn=1024, tk=1024)
    grads["w_sgu_out"] = _matmul(sg, dy_b, mode="tn", out_dtype=BF16, name="mm_dw_sgu", tm=1024, tn=1024, tk=1024)
    grads["w_conv_out"] = _matmul(a_act, dy_a, mode="tn", out_dtype=BF16, name="mm_dw_conv", tm=1024, tn=1024, tk=1024)
    tok = send_grads(["w_mix_out", "w_sgu_out", "w_conv_out"], "mixer")
    dp, d_sgu_w, d_sgu_b, d_sgu_ln_g, d_sgu_ln_b = _sgu_bwd(dp, dy_b, wfull["w_sgu_out"], p, vn, wm, wmt, sgu_bias,
                                                             sgu_ln_g + tok, name="sgu_bwd")
    dc, d_conv_ln_g, d_conv_ln_b = _conv_ln_bwd_fused(dy_a, c_conv, wfull["w_conv_out"], conv_ln_g, conv_ln_b,
                                                      name="conv_ln_bwd")
    dp, d_conv_w, d_conv_b = _conv_bwd(dp, dc, p, cw_full, bl=bl, s=s, name="conv_bwd")
    grads["w_in"] = _matmul(h1, dp, mode="tn", out_dtype=BF16, name="mm_dw_in", tm=1024, tn=768, tk=2048,
                            col_blocks=N_DEV)
    tok = send_grads(["w_in"], "in")
    grad_x, d_norm_mix = _proj_rms_bwd(dp, dx1, xf, wfull["w_in"], norm_mix + tok, name="in_proj_bwd")
    out = {}

    rep_names = ["norm_mix", "conv_b", "conv_ln_g", "conv_ln_b", "sgu_ln_g", "sgu_ln_b", "norm_xattn", "norm_mem",
                 "norm_ffn", "norm_final", "sgu_b"]
    rep_grads = [d_norm_mix, d_conv_b, d_conv_ln_g, d_conv_ln_b, d_sgu_ln_g, d_sgu_ln_b, d_norm_xattn, d_norm_mem,
                 d_norm_ffn, d_norm_final, d_sgu_b.reshape(1, d)]
    nrep = len(rep_names)
    pad = jnp.zeros((16 - nrep, d), F32)
    sgw_rows = SGU_GROUPS * SGU_CHUNK * SGU_CHUNK // d

    def pack_rep(vecs, sgw):
        return jnp.concatenate([v.reshape(1, d) for v in vecs] + [pad, sgw.reshape(sgw_rows, d)], axis=0)

    def pack_col(bg, cw):
        return jnp.concatenate([_pad_rows(bg, 8), _pad_rows(cw, CONV_HALO)], axis=0)

    small_a = pack_rep(rep_grads, d_sgu_w)
    small_b = jnp.concatenate([d_b_gate, d_conv_w], axis=0)
    parts_a, parts_b = _all_gather([small_a, small_b], name="gather_small_grads")
    res_a = _adamw_small(parts_a, dev_id, pack_rep([given[n] for n in rep_names], sgu_w),
                         pack_rep([given["m_" + n] for n in rep_names], m_sgu_w),
                         pack_rep([given["v_" + n] for n in rep_names], v_sgu_w), name="adamw_small", col_block=False)
    res_b = _adamw_small(parts_b, dev_id, pack_col(b_gate[0], conv_w[0]), pack_col(m_b_gate[0], m_conv_w[0]),
                         pack_col(v_b_gate[0], v_conv_w[0]), name="adamw_small_cols", col_block=True)
    for i, n in enumerate(rep_names):
        out[n] = [r[i].reshape(given[n].shape) for r in res_a]
    out["sgu_w"] = [r[16:16 + sgw_rows].reshape(sgu_w.shape) for r in res_a]
    out["b_gate"] = [r[0:2][None] for r in res_b]
    out["conv_w"] = [r[8:8 + CONV_WIDTH][None] for r in res_b]

    for names, ssem, rsem, srcs, lands in sent:
        srcs, lands = _copy_wait("scatter", ssem, rsem, srcs, lands, res_a[0], name=f"grads_{names[0]}_wait")
        for n, partials, landed in zip(names, srcs, lands):
            res = _adamw_shard(partials, landed, dev_id, given[n][0], given["m_" + n][0], given["v_" + n][0],
                               name=f"adamw_{n}")
            out[n] = [r[None] for r in res]

    order = ["norm_mix", "w_in", "b_gate", "conv_w", "conv_b", "conv_ln_g", "conv_ln_b", "w_conv_out", "sgu_ln_g",
             "sgu_ln_b", "sgu_w", "sgu_b", "w_sgu_out", "w_mix_out", "norm_xattn", "norm_mem", "w_q", "w_kv", "w_xo",
             "norm_ffn", "w_gu", "w_down", "norm_final"]
    return (loss, grad_x.reshape(x.shape), *[out[n][0] for n in order], *[out[n][1] for n in order],
            *[out[n][2] for n in order], *[out[n][3] for n in order])
```

```python
import functools

import jax
import jax.numpy as jnp
from jax import lax
from jax.experimental import pallas as pl
from jax.experimental.pallas import tpu as pltpu

F32 = jnp.float32
BF16 = jnp.bfloat16
RMS_EPS = 1e-6
LN_EPS = 1e-5
CONV_WIDTH = 31
CONV_HALO = 32
CONV_ROWS = 64
CONV_COLS = 256
LANES = 128
SGU_CHUNK = 128
SGU_GROUPS = 8
HEADS = 4
N_DEV = 8
ADAM_LR, ADAM_B1, ADAM_B2, ADAM_EPS, ADAM_WD, ADAM_STEP = 0.001, 0.9, 0.999, 1e-08, 0.01, 10
VMEM_LIMIT = 56 * 1024 * 1024
TOKEN_TILE = 256
ATTN_TILE = 512
MESH_ID = pl.DeviceIdType.MESH

_GELU_K = 0.7978845608028654
_GELU_C = 0.044715


def _cparams(sem=None):
    return pltpu.CompilerParams(dimension_semantics=sem, vmem_limit_bytes=VMEM_LIMIT)


def _sigmoid(v):
    return 1.0 / (1.0 + jnp.exp(-v))


def _gelu(v):
    return 0.5 * v * (1.0 + jnp.tanh(_GELU_K * (v + _GELU_C * v * v * v)))


def _gelu_grad(v):
    th = jnp.tanh(_GELU_K * (v + _GELU_C * v * v * v))
    return 0.5 * (1.0 + th) + 0.5 * v * (1.0 - th * th) * _GELU_K * (1.0 + 3.0 * _GELU_C * v * v)


def _dot(a, b, dims):
    return lax.dot_general(a, b, (dims, ((), ())), preferred_element_type=F32)


_NN = ((1,), (0,))
_NT = ((1,), (1,))
_TN = ((0,), (0,))


def _matmul(a, b, *, mode, out_dtype, name, tm=512, tn=512, tk=512, chunk=None, residual=None, rms_gain=None,
            col_blocks=None, b_cols=None):
    if mode == "nn":
        (m, k), (_, n) = a.shape, b.shape
    elif mode == "nt":
        (m, k), (n, _) = a.shape, b.shape
    else:
        (k, m), (_, n) = a.shape, b.shape
    b_first = 0
    if b_cols is not None:
        assert mode == "tn"
        b_first, n = b_cols
    tm, tn, tk = min(tm, m), min(tn, n), min(tk, k)
    assert b_first % tn == 0
    b_first //= tn
    assert m % tm == 0 and n % tn == 0 and k % tk == 0, (name, a.shape, b.shape, tm, tn, tk)
    nk = k // tk
    dims = {"nn": _NN, "nt": _NT, "tn": _TN}[mode]
    chunk = tn if chunk is None else min(chunk, tn)
    assert tn % chunk == 0
    if rms_gain is not None:
        assert tn == n and chunk == n

    def body(*refs):
        refs = list(refs)
        a_ref, b_ref = refs[:2]
        pos = 2
        r_ref = g_ref = None
        if residual is not None:
            r_ref = refs[pos]
            pos += 1
        if rms_gain is not None:
            g_ref = refs[pos]
            pos += 1
        o_ref = refs[pos]
        pos += 1
        h_ref = None
        if rms_gain is not None:
            h_ref = refs[pos]
            pos += 1
        acc_ref = refs[pos] if nk > 1 else None
        av = a_ref[...].astype(BF16)
        for c0 in range(0, tn, chunk):
            cs = slice(c0, c0 + chunk)
            bv = (b_ref[cs, :] if mode == "nt" else b_ref[:, cs]).astype(BF16)
            part = _dot(av, bv, dims)

            def finish(res, cs=cs):
                if r_ref is not None:
                    res = res + r_ref[:, cs].astype(F32)
                o_ref[:, cs] = res.astype(out_dtype)
                if h_ref is not None:
                    r = lax.rsqrt(jnp.mean(res * res, axis=-1, keepdims=True) + RMS_EPS)
                    h_ref[...] = (res * r * g_ref[...]).astype(BF16)

            if nk == 1:
                finish(part)
            else:
                kk = pl.program_id(2)

                @pl.when(kk == 0)
                def _(part=part, cs=cs):
                    acc_ref[:, cs] = part

                @pl.when(kk > 0)
                def _(part=part, cs=cs):
                    acc_ref[:, cs] += part

                @pl.when(kk == nk - 1)
                def _(finish=finish, cs=cs):
                    finish(acc_ref[:, cs])

    resident = dict(pipeline_mode=pl.Buffered(1)) if (n == tn and nk == 1 and mode != "tn" and m > tm) else {}
    if mode == "nn":
        a_spec = pl.BlockSpec((tm, tk), lambda i, j, kk: (i, kk))
        b_spec = pl.BlockSpec((tk, tn), lambda i, j, kk: (kk, j), **resident)
    elif mode == "nt":
        a_spec = pl.BlockSpec((tm, tk), lambda i, j, kk: (i, kk))
        b_spec = pl.BlockSpec((tn, tk), lambda i, j, kk: (j, kk), **resident)
    else:
        a_spec = pl.BlockSpec((tk, tm), lambda i, j, kk: (kk, i))
        b_spec = pl.BlockSpec((tk, tn), lambda i, j, kk: (kk, j + b_first))
    o_spec = pl.BlockSpec((tm, tn), lambda i, j, kk: (i, j))
    in_specs, args = [a_spec, b_spec], [a, b]
    if residual is not None:
        in_specs.append(o_spec)
        args.append(residual)
    out_shape, out_specs = [jax.ShapeDtypeStruct((m, n), out_dtype)], [o_spec]
    if col_blocks is not None:
        assert residual is None and rms_gain is None and (n // col_blocks) % tn == 0
        per = n // col_blocks // tn
        out_shape = [jax.ShapeDtypeStruct((col_blocks, m, n // col_blocks), out_dtype)]
        out_specs = [pl.BlockSpec((None, tm, tn), lambda i, j, kk: (j // per, i, j % per))]
    if rms_gain is not None:
        in_specs.append(pl.BlockSpec((1, n), lambda i, j, kk: (0, 0)))
        args.append(rms_gain)
        out_shape.append(jax.ShapeDtypeStruct((m, n), BF16))
        out_specs.append(o_spec)
    res = pl.pallas_call(
        body, name=name, grid=(m // tm, n // tn, nk), in_specs=in_specs, out_specs=out_specs, out_shape=out_shape,
        scratch_shapes=[pltpu.VMEM((tm, tn), F32)] if nk > 1 else [],
        compiler_params=_cparams(("parallel", "parallel", "arbitrary")),
    )(*args)
    return res if rms_gain is not None else res[0]


def _row_call(name, t, tm, rows_in, residents, rows_out, accs, body):
    n_in, n_res, n_out = len(rows_in), len(residents), len(rows_out)
    steps = t // tm
    assert t % tm == 0

    def kernel_body(*refs):
        in_refs, res_refs = refs[:n_in], refs[n_in:n_in + n_res]
        out_refs, acc_refs = refs[n_in + n_res:n_in + n_res + n_out], refs[n_in + n_res + n_out:]
        if accs:
            @pl.when(pl.program_id(0) == 0)
            def _():
                for acc in acc_refs:
                    acc[...] = jnp.zeros_like(acc)
        body(in_refs, res_refs, out_refs, acc_refs)

    once = dict(pipeline_mode=pl.Buffered(1)) if steps > 1 else {}
    in_specs = [pl.BlockSpec((tm, cols), lambda i, cb=cb: (i, cb)) for _, cols, cb in rows_in]
    in_specs += [pl.BlockSpec(r.shape, lambda i, nd=r.ndim: (0,) * nd, **once) for r in residents]
    out_specs = [pl.BlockSpec((tm, cols), lambda i, cb=cb: (i, cb)) for _, cols, cb, _ in rows_out]
    out_specs += [pl.BlockSpec(shape, lambda i, nd=len(shape): (0,) * nd) for shape, _ in accs]
    out_shape = [jax.ShapeDtypeStruct((t, total), dt) for total, _, _, dt in rows_out]
    out_shape += [jax.ShapeDtypeStruct(shape, dt) for shape, dt in accs]
    return pl.pallas_call(
        kernel_body, name=name, grid=(steps,), in_specs=in_specs, out_specs=out_specs, out_shape=out_shape,
        compiler_params=_cparams(("arbitrary",) if accs else ("parallel",)),
    )(*[a for a, _, _ in rows_in], *residents)


def _rms_apply(xv, gain):
    return xv * lax.rsqrt(jnp.mean(xv * xv, axis=-1, keepdims=True) + RMS_EPS) * gain


def _rms_grad(dres, dh, xv, gain):
    r = lax.rsqrt(jnp.mean(xv * xv, axis=-1, keepdims=True) + RMS_EPS)
    xhat = xv * r
    dxh = dh * gain
    dx = dres + r * (dxh - xhat * jnp.mean(dxh * xhat, axis=-1, keepdims=True))
    return dx, jnp.sum(dh * xhat, axis=0, keepdims=True)


def _in_proj(xf, gain, w_in, *, name):
    t, d = xf.shape
    n = w_in.shape[1]
    chunk = n // 4

    def body(ins, res, outs, accs):
        (x_ref,), (g_ref, w_ref), (h_ref, p_ref) = ins, res, outs
        h = _rms_apply(x_ref[...], g_ref[...]).astype(BF16)
        h_ref[...] = h
        for c0 in range(0, n, chunk):
            p_ref[:, c0:c0 + chunk] = _dot(h, w_ref[:, c0:c0 + chunk], _NN).astype(BF16)

    return _row_call(name, t, min(512, t), [(xf, d, 0)], [gain, w_in], [(d, d, 0, BF16), (n, n, 0, BF16)], [], body)


def _mix_out(p, y_a, y_b, b_gate, xf, w_mix, gain, w_q, *, name):
    t, d = xf.shape

    def body(ins, res, outs, accs):
        ga_ref, gb_ref, ya_ref, yb_ref, x_ref = ins
        bg_ref, wm_ref, g_ref, wq_ref = res
        m_ref, x1_ref, h_ref, q_ref = outs
        sa = _sigmoid(ga_ref[...].astype(F32) + bg_ref[0:1, :])
        sb = _sigmoid(gb_ref[...].astype(F32) + bg_ref[1:2, :])
        merged = (sa * ya_ref[...].astype(F32) + sb * yb_ref[...].astype(F32)).astype(BF16)
        m_ref[...] = merged
        x1 = x_ref[...] + _dot(merged, wm_ref[...], _NN)
        x1_ref[...] = x1
        h = _rms_apply(x1, g_ref[...]).astype(BF16)
        h_ref[...] = h
        q_ref[...] = _dot(h, wq_ref[...], _NN).astype(BF16)

    return _row_call(name, t, min(512, t), [(p, d, 4), (p, d, 5), (y_a, d, 0), (y_b, d, 0), (xf, d, 0)],
                     [b_gate, w_mix, gain, w_q], [(d, d, 0, BF16), (d, d, 0, F32), (d, d, 0, BF16), (d, d, 0, BF16)], [], body)


def _ffn_fwd(h3, x2, target, w_gu, w_down, gain, *, name):
    t, d = x2.shape
    f2 = w_gu.shape[1]
    f = f2 // 2
    half = f // 2

    def body(ins, res, outs, accs):
        h_ref, x2_ref, t_ref = ins
        wgu_ref, wd_ref, g_ref = res
        gu_ref, act_ref, dx_ref = outs
        loss_ref, dg_ref = accs
        h = h_ref[...]
        x3 = x2_ref[...]
        for c0 in (0, half):
            gt = _dot(h, wgu_ref[:, c0:c0 + half], _NN).astype(BF16)
            up = _dot(h, wgu_ref[:, f + c0:f + c0 + half], _NN).astype(BF16)
            gu_ref[:, c0:c0 + half] = gt
            gu_ref[:, f + c0:f + c0 + half] = up
            gtf = gt.astype(F32)
            act = (gtf * _sigmoid(gtf) * up.astype(F32)).astype(BF16)
            act_ref[:, c0:c0 + half] = act
            x3 = x3 + _dot(act, wd_ref[c0:c0 + half, :], _NN)
        g = g_ref[...]
        r = lax.rsqrt(jnp.mean(x3 * x3, axis=-1, keepdims=True) + RMS_EPS)
        xhat = x3 * r
        err = xhat * g - t_ref[...]
        loss_ref[...] += 0.5 * jnp.sum(jnp.mean(err * err, axis=-1, keepdims=True), axis=0, keepdims=True)
        dy = err * (1.0 / d)
        dg_ref[...] += jnp.sum(dy * xhat, axis=0, keepdims=True)
        dxh = dy * g
        dx_ref[...] = r * (dxh - xhat * jnp.mean(dxh * xhat, axis=-1, keepdims=True))

    return _row_call(name, t, min(256, t), [(h3, d, 0), (x2, d, 0), (target, d, 0)], [w_gu, w_down, gain],
                     [(f2, f2, 0, BF16), (f, f, 0, BF16), (d, d, 0, F32)], [((1, 1), F32), ((1, d), F32)], body)


def _ffn_bwd(dx3, gu, x2, w_down, w_gu, gain, w_xo, *, name):
    t, d = x2.shape
    f2 = w_gu.shape[1]
    f = f2 // 2
    half = f // 2

    def body(ins, res, outs, accs):
        dx3_ref, gu_ref, x2_ref = ins
        wd_ref, wgu_ref, g_ref, wxo_ref = res
        dgu_ref, dx2_ref, do_ref = outs
        (dg_ref,) = accs
        dx3v = dx3_ref[...]
        dxb = dx3v.astype(BF16)
        dh = jnp.zeros(dx3v.shape, F32)
        for c0 in (0, half):
            dact = _dot(dxb, wd_ref[c0:c0 + half, :], _NT)
            gt = gu_ref[:, c0:c0 + half].astype(F32)
            up = gu_ref[:, f + c0:f + c0 + half].astype(F32)
            sg = _sigmoid(gt)
            dgt = (dact * up * sg * (1.0 + gt * (1.0 - sg))).astype(BF16)
            dup = (dact * gt * sg).astype(BF16)
            dgu_ref[:, c0:c0 + half] = dgt
            dgu_ref[:, f + c0:f + c0 + half] = dup
            dh = dh + _dot(dgt, wgu_ref[:, c0:c0 + half], _NT) + _dot(dup, wgu_ref[:, f + c0:f + c0 + half], _NT)
        dx2, dg = _rms_grad(dx3v, dh, x2_ref[...], g_ref[...])
        dx2_ref[...] = dx2
        dg_ref[...] += dg
        do_ref[...] = _dot(dx2.astype(BF16), wxo_ref[...], _NT).astype(BF16)

    return _row_call(name, t, min(256, t), [(dx3, d, 0), (gu, f2, 0), (x2, d, 0)], [w_down, w_gu, gain, w_xo],
                     [(f2, f2, 0, BF16), (d, d, 0, F32), (d, d, 0, BF16)], [((1, d), F32)], body)


def _proj_rms_bwd(dy, dres, x, w, gain, *, name, h=None):
    t, d = x.shape
    k = dy.shape[1]

    def body(ins, res, outs, accs):
        dy_ref, dres_ref, x_ref = ins[:3]
        w_ref, g_ref = res
        if h is not None:
            accs[1][...] += _dot(ins[3][...], dy_ref[...], _TN)
        dh = _dot(dy_ref[...], w_ref[...], _NT)
        dx, dg = _rms_grad(dres_ref[...], dh, x_ref[...], g_ref[...])
        outs[0][...] = dx
        accs[0][...] += dg

    rows_in = [(dy, k, 0), (dres, d, 0), (x, d, 0)] + ([(h, d, 0)] if h is not None else [])
    accs = [((1, d), F32)] + ([((d, k), F32)] if h is not None else [])
    return _row_call(name, t, min(512, t), rows_in, [w, gain], [(d, d, 0, F32)], accs, body)


def _gates_bwd_fused(dx1, p, y_a, y_b, b_gate, w_mix, merged, h1, *, name):
    t, d = y_a.shape

    def body(ins, res, outs, accs):
        dx_ref, ga_ref, gb_ref, ya_ref, yb_ref, m_ref, h1_ref = ins
        bg_ref, wm_ref = res
        dp_ref, dya_ref, dyb_ref = outs
        dbg_ref, dwm_ref, dwin_ref = accs
        dxb = dx_ref[...].astype(BF16)
        dwm_ref[...] += _dot(m_ref[...], dxb, _TN)
        dm = _dot(dxb, wm_ref[...], _NT)
        sa = _sigmoid(ga_ref[...].astype(F32) + bg_ref[0:1, :])
        sb = _sigmoid(gb_ref[...].astype(F32) + bg_ref[1:2, :])
        dya_ref[...] = (dm * sa).astype(BF16)
        dyb_ref[...] = (dm * sb).astype(BF16)
        dga = dm * ya_ref[...].astype(F32) * sa * (1.0 - sa)
        dgb = dm * yb_ref[...].astype(F32) * sb * (1.0 - sb)
        dp_ref[:, 0:d] = dga.astype(BF16)
        dp_ref[:, d:2 * d] = dgb.astype(BF16)
        dbg_ref[0:1, :] += jnp.sum(dga, axis=0, keepdims=True)
        dbg_ref[1:2, :] += jnp.sum(dgb, axis=0, keepdims=True)
        dwin_ref[...] += _dot(h1_ref[...], dp_ref[...], _TN)

    return _row_call(name, t, min(256, t),
                     [(dx1, d, 0), (p, d, 4), (p, d, 5), (y_a, d, 0), (y_b, d, 0), (merged, d, 0), (h1, d, 0)],
                     [b_gate, w_mix], [(p.shape[1], 2 * d, 2, BF16), (d, d, 0, BF16), (d, d, 0, BF16)],
                     [((8, d), F32), ((d, d), F32), ((d, 2 * d), F32)], body)


def _conv_ln_bwd_fused(dy_a, c, a_act, w_conv_out, ln_g, ln_b, *, name):
    t, d = c.shape

    def body(ins, res, outs, accs):
        dy_ref, c_ref, act_ref = ins
        w_ref, lg_ref, lb_ref = res
        dlg_ref, dlb_ref, dw_ref = accs
        dw_ref[...] += _dot(act_ref[...], dy_ref[...], _TN)
        dact = _dot(dy_ref[...], w_ref[...], _NT)
        cv = c_ref[...].astype(F32)
        g = lg_ref[...]
        mu = jnp.mean(cv, axis=-1, keepdims=True)
        dv = cv - mu
        rstd = lax.rsqrt(jnp.mean(dv * dv, axis=-1, keepdims=True) + LN_EPS)
        chat = dv * rstd
        aln = chat * g + lb_ref[...]
        sg = _sigmoid(aln)
        daln = dact * (sg * (1.0 + aln * (1.0 - sg)))
        dlb_ref[...] += jnp.sum(daln, axis=0, keepdims=True)
        dlg_ref[...] += jnp.sum(daln * chat, axis=0, keepdims=True)
        dchat = daln * g
        dc = rstd * (dchat - jnp.mean(dchat, axis=-1, keepdims=True)
                     - chat * jnp.mean(dchat * chat, axis=-1, keepdims=True))
        outs[0][...] = dc.astype(BF16)

    return _row_call(name, t, min(512, t), [(dy_a, d, 0), (c, d, 0), (a_act, d, 0)], [w_conv_out, ln_g, ln_b],
                     [(d, d, 0, BF16)], [((1, d), F32), ((1, d), F32), ((d, d), F32)], body)


def _row_spec(tt, cols, col_block=0):
    return pl.BlockSpec((tt, cols), lambda i: (i, col_block))


def _const_spec(shape):
    return pl.BlockSpec(shape, lambda *_: (0,) * len(shape))


def _rms_fwd(x, gain, *, name):
    t, d = x.shape
    tt = min(TOKEN_TILE, t)

    def body(x_ref, g_ref, h_ref):
        xv = x_ref[...]
        r = lax.rsqrt(jnp.mean(xv * xv, axis=-1, keepdims=True) + RMS_EPS)
        h_ref[...] = (xv * r * g_ref[...]).astype(BF16)

    return pl.pallas_call(
        body, name=name, grid=(t // tt,), in_specs=[_row_spec(tt, d), _const_spec((1, d))],
        out_specs=_row_spec(tt, d), out_shape=jax.ShapeDtypeStruct((t, d), BF16),
        compiler_params=_cparams(("parallel",)))(x, gain)


def _rms_bwd(dres, dh, x, gain, *, name, need_dx=True):
    t, d = x.shape
    tt = min(TOKEN_TILE, t)

    def body(*refs):
        if need_dx:
            dres_ref, dh_ref, x_ref, g_ref, dx_ref, dg_ref = refs
        else:
            dh_ref, x_ref, g_ref, dg_ref = refs

        @pl.when(pl.program_id(0) == 0)
        def _():
            dg_ref[...] = jnp.zeros_like(dg_ref)

        xv = x_ref[...]
        dhv = dh_ref[...].astype(F32)
        r = lax.rsqrt(jnp.mean(xv * xv, axis=-1, keepdims=True) + RMS_EPS)
        xhat = xv * r
        dg_ref[...] += jnp.sum(dhv * xhat, axis=0, keepdims=True)
        if need_dx:
            dxh = dhv * g_ref[...]
            dx_ref[...] = dres_ref[...] + r * (dxh - xhat * jnp.mean(dxh * xhat, axis=-1, keepdims=True))

    rs = _row_spec(tt, d)
    if need_dx:
        in_specs, args = [rs, rs, rs, _const_spec((1, d))], (dres, dh, x, gain)
        out_specs = [rs, _const_spec((1, d))]
        out_shape = [jax.ShapeDtypeStruct((t, d), F32), jax.ShapeDtypeStruct((1, d), F32)]
    else:
        in_specs, args = [rs, rs, _const_spec((1, d))], (dh, x, gain)
        out_specs = [_const_spec((1, d))]
        out_shape = [jax.ShapeDtypeStruct((1, d), F32)]
    res = pl.pallas_call(body, name=name, grid=(t // tt,), in_specs=in_specs, out_specs=out_specs, out_shape=out_shape,
                         compiler_params=_cparams(("arbitrary",)))(*args)
    return res if need_dx else res[0]


def _final_loss(x3, target, gain, *, name):
    t, d = x3.shape
    tt = min(TOKEN_TILE, t)

    def body(x_ref, t_ref, g_ref, loss_ref, dx_ref, dg_ref):
        @pl.when(pl.program_id(0) == 0)
        def _():
            loss_ref[...] = jnp.zeros_like(loss_ref)
            dg_ref[...] = jnp.zeros_like(dg_ref)

        xv = x_ref[...]
        g = g_ref[...]
        r = lax.rsqrt(jnp.mean(xv * xv, axis=-1, keepdims=True) + RMS_EPS)
        xhat = xv * r
        err = xhat * g - t_ref[...]
        loss_ref[...] += 0.5 * jnp.sum(jnp.mean(err * err, axis=-1, keepdims=True), axis=0, keepdims=True)
        dy = err * (1.0 / d)
        dg_ref[...] += jnp.sum(dy * xhat, axis=0, keepdims=True)
        dxh = dy * g
        dx_ref[...] = r * (dxh - xhat * jnp.mean(dxh * xhat, axis=-1, keepdims=True))

    rs = _row_spec(tt, d)
    return pl.pallas_call(
        body, name=name, grid=(t // tt,), in_specs=[rs, rs, _const_spec((1, d))],
        out_specs=[_const_spec((1, 1)), rs, _const_spec((1, d))],
        out_shape=[jax.ShapeDtypeStruct((1, 1), F32), jax.ShapeDtypeStruct((t, d), F32), jax.ShapeDtypeStruct((1, d), F32)],
        compiler_params=_cparams(("arbitrary",)))(x3, target, gain)


SUBLANES = 8
SHIFT_ROWS = 40


def _conv_apply(sbuf_ref, w_ref, out_ref, tt, offsets, bias_ref=None):
    d = out_ref.shape[1]
    for cc in range(d // LANES):
        cs = slice(cc * LANES, (cc + 1) * LANES)
        taps = [jnp.broadcast_to(w_ref[k:k + 1, cs], (SUBLANES, LANES)) for k in range(CONV_WIDTH)]
        bias = None if bias_ref is None else jnp.broadcast_to(bias_ref[:, cs], (SUBLANES, LANES))

        def row_body(r, carry, cs=cs, taps=taps, bias=bias):
            r0 = pl.multiple_of(r * CONV_ROWS, CONV_ROWS)
            for q in range(CONV_ROWS // SUBLANES):
                acc = _tap(sbuf_ref, r0 + q * SUBLANES, cs, offsets[0]) * taps[0]
                for k in range(1, CONV_WIDTH):
                    acc = acc + _tap(sbuf_ref, r0 + q * SUBLANES, cs, offsets[k]) * taps[k]
                if bias is not None:
                    acc = acc + bias
                out_ref[pl.ds(r0 + q * SUBLANES, SUBLANES), cs] = acc
            return carry

        lax.fori_loop(0, tt // CONV_ROWS, row_body, 0)


def _fill_shifts(sbuf_ref, rows):
    d = sbuf_ref.shape[2]
    assert rows % SHIFT_ROWS == 0

    def row_body(i, carry):
        r0 = pl.multiple_of(i * SHIFT_ROWS, SUBLANES)
        for cc in range(d // CONV_COLS):
            cs = slice(cc * CONV_COLS, (cc + 1) * CONV_COLS)
            win = sbuf_ref[0, pl.ds(r0, SHIFT_ROWS + SUBLANES), cs]
            for sh in range(1, SUBLANES):
                sbuf_ref[sh, pl.ds(r0, SHIFT_ROWS), cs] = win[sh:sh + SHIFT_ROWS, :]
        return carry

    lax.fori_loop(0, rows // SHIFT_ROWS, row_body, 0)


def _tap(sbuf_ref, r0, cs, offset):
    sh = offset % SUBLANES
    return sbuf_ref[sh, pl.ds(pl.multiple_of(r0 + (offset - sh), SUBLANES), SUBLANES), cs]


def _conv_specs(bl, s, tt, d, col_a, col_g):
    nj = s // tt
    per = tt // CONV_HALO
    main_a = pl.BlockSpec((tt, d), lambda b, j: (b * nj + j, col_a))
    main_g = pl.BlockSpec((tt, d), lambda b, j: (b * nj + j, col_g))
    prev = lambda b, j: jnp.maximum((b * nj + j) * per - 1, 0)
    halo_a = pl.BlockSpec((CONV_HALO, d), lambda b, j: (prev(b, j), col_a))
    halo_g = pl.BlockSpec((CONV_HALO, d), lambda b, j: (prev(b, j), col_g))
    return main_a, main_g, halo_a, halo_g


def _fill_glu(sbuf_ref, a_ref, g_ref, ha_ref, hg_ref, tt):
    first = pl.program_id(1) == 0
    ha = ha_ref[...].astype(F32)
    hg = hg_ref[...].astype(F32)
    sbuf_ref[0, pl.ds(0, CONV_HALO), :] = jnp.where(first, 0.0, ha * _sigmoid(hg))
    av = a_ref[...].astype(F32)
    gv = g_ref[...].astype(F32)
    sbuf_ref[0, pl.ds(CONV_HALO, tt), :] = av * _sigmoid(gv)
    _fill_shifts(sbuf_ref, tt + CONV_HALO - SUBLANES)


def _conv_fwd(p, conv_w, conv_b, ln_g, ln_b, *, bl, s, name):
    t = p.shape[0]
    d = conv_w.shape[1]
    tt = min(TOKEN_TILE, s)
    off = CONV_HALO - (CONV_WIDTH - 1)

    def body(a_ref, g_ref, ha_ref, hg_ref, w_ref, b_ref, lg_ref, lb_ref, c_ref, act_ref, sbuf_ref, cbuf_ref):
        _fill_glu(sbuf_ref, a_ref, g_ref, ha_ref, hg_ref, tt)

        _conv_apply(sbuf_ref, w_ref, cbuf_ref, tt, [off + k for k in range(CONV_WIDTH)], bias_ref=b_ref)
        cv = cbuf_ref[...]
        c_ref[...] = cv.astype(BF16)
        mu = jnp.mean(cv, axis=-1, keepdims=True)
        dv = cv - mu
        rstd = lax.rsqrt(jnp.mean(dv * dv, axis=-1, keepdims=True) + LN_EPS)
        aln = dv * rstd * lg_ref[...] + lb_ref[...]
        act_ref[...] = (aln * _sigmoid(aln)).astype(BF16)

    main_a, main_g, halo_a, halo_g = _conv_specs(bl, s, tt, d, 0, 1)
    out_spec = pl.BlockSpec((tt, d), lambda b, j: (b * (s // tt) + j, 0))
    return pl.pallas_call(
        body, name=name, grid=(bl, s // tt),
        in_specs=[main_a, main_g, halo_a, halo_g, _const_spec((CONV_HALO, d)), _const_spec((1, d)), _const_spec((1, d)),
                  _const_spec((1, d))],
        out_specs=[out_spec, out_spec],
        out_shape=[jax.ShapeDtypeStruct((t, d), BF16), jax.ShapeDtypeStruct((t, d), BF16)],
        scratch_shapes=[pltpu.VMEM((SUBLANES, tt + CONV_HALO, d), F32), pltpu.VMEM((tt, d), F32)],
        compiler_params=_cparams(("parallel", "parallel")))(p, p, p, p, conv_w, conv_b, ln_g, ln_b)


def _conv_ln_bwd(dact, c, ln_g, ln_b, *, name):
    t, d = c.shape
    tt = min(TOKEN_TILE, t)

    def body(da_ref, c_ref, lg_ref, lb_ref, dc_ref, dlg_ref, dlb_ref):
        @pl.when(pl.program_id(0) == 0)
        def _():
            dlg_ref[...] = jnp.zeros_like(dlg_ref)
            dlb_ref[...] = jnp.zeros_like(dlb_ref)

        cv = c_ref[...].astype(F32)
        g = lg_ref[...]
        mu = jnp.mean(cv, axis=-1, keepdims=True)
        dv = cv - mu
        rstd = lax.rsqrt(jnp.mean(dv * dv, axis=-1, keepdims=True) + LN_EPS)
        chat = dv * rstd
        aln = chat * g + lb_ref[...]
        sg = _sigmoid(aln)
        daln = da_ref[...].astype(F32) * (sg * (1.0 + aln * (1.0 - sg)))
        dlb_ref[...] += jnp.sum(daln, axis=0, keepdims=True)
        dlg_ref[...] += jnp.sum(daln * chat, axis=0, keepdims=True)
        dchat = daln * g
        dc = rstd * (dchat - jnp.mean(dchat, axis=-1, keepdims=True)
                     - chat * jnp.mean(dchat * chat, axis=-1, keepdims=True))
        dc_ref[...] = dc.astype(BF16)

    rs = _row_spec(tt, d)
    cs = _const_spec((1, d))
    return pl.pallas_call(
        body, name=name, grid=(t // tt,), in_specs=[rs, rs, cs, cs], out_specs=[rs, cs, cs],
        out_shape=[jax.ShapeDtypeStruct((t, d), BF16), jax.ShapeDtypeStruct((1, d), F32), jax.ShapeDtypeStruct((1, d), F32)],
        compiler_params=_cparams(("arbitrary",)))(dact, c, ln_g, ln_b)


def _conv_bwd(dp, dc, p, conv_w, h1, *, bl, s, name):
    t = p.shape[0]
    d = conv_w.shape[1]
    tt = min(TOKEN_TILE, s)
    nj = s // tt
    per = tt // CONV_HALO
    off = CONV_HALO - (CONV_WIDTH - 1)
    last_blk = t // CONV_HALO - 1

    def body(dp_in, dc_ref, dcn_ref, a_ref, g_ref, ha_ref, hg_ref, w_ref, h1_ref, dp_ref, dw_ref, db_ref, dwin_ref,
             gbuf_ref, dbuf_ref, dglu_ref, acc_ref):
        del dp_in
        b, j = pl.program_id(0), pl.program_id(1)
        start = jnp.logical_and(b == 0, j == 0)
        end = jnp.logical_and(b == bl - 1, j == nj - 1)

        @pl.when(start)
        def _():
            acc_ref[...] = jnp.zeros_like(acc_ref)
            db_ref[...] = jnp.zeros_like(db_ref)
            dwin_ref[...] = jnp.zeros_like(dwin_ref)

        _fill_glu(gbuf_ref, a_ref, g_ref, ha_ref, hg_ref, tt)
        dcv = dc_ref[...].astype(F32)
        dbuf_ref[0, pl.ds(0, tt), :] = dcv
        dbuf_ref[0, pl.ds(tt, CONV_HALO), :] = jnp.where(j == nj - 1, 0.0, dcn_ref[...].astype(F32))
        _fill_shifts(dbuf_ref, tt + CONV_HALO - SUBLANES)
        db_ref[...] += jnp.sum(dcv, axis=0, keepdims=True)

        for cc in range(d // LANES):
            cs = slice(cc * LANES, (cc + 1) * LANES)

            def row_body(r, accs, cs=cs):
                r0 = pl.multiple_of(r * CONV_ROWS, CONV_ROWS)
                accs = list(accs)
                for q in range(CONV_ROWS // SUBLANES):
                    dcw = dbuf_ref[0, pl.ds(r0 + q * SUBLANES, SUBLANES), cs]
                    for k in range(CONV_WIDTH):
                        accs[k] = accs[k] + dcw * _tap(gbuf_ref, r0 + q * SUBLANES, cs, off + k)
                return tuple(accs)

            zero = jnp.zeros((SUBLANES, LANES), F32)
            accs = lax.fori_loop(0, tt // CONV_ROWS, row_body, (zero,) * CONV_WIDTH)
            for k in range(CONV_WIDTH):
                acc_ref[k, :, cs] += accs[k]

        _conv_apply(dbuf_ref, w_ref, dglu_ref, tt, [CONV_WIDTH - 1 - k for k in range(CONV_WIDTH)])
        dglu = dglu_ref[...]
        av = a_ref[...].astype(F32)
        sg = _sigmoid(g_ref[...].astype(F32))
        dp_ref[:, 0:d] = (dglu * sg).astype(BF16)
        dp_ref[:, d:2 * d] = (dglu * av * sg * (1.0 - sg)).astype(BF16)
        dwin_ref[...] += _dot(h1_ref[...], dp_ref[...], _TN)

        @pl.when(end)
        def _():
            for k in range(CONV_WIDTH):
                dw_ref[k:k + 1, :] = jnp.sum(acc_ref[k], axis=0, keepdims=True)
            dw_ref[CONV_WIDTH:CONV_HALO, :] = jnp.zeros((CONV_HALO - CONV_WIDTH, d), F32)

    main_a, main_g, halo_a, halo_g = _conv_specs(bl, s, tt, d, 0, 1)
    dc_main = pl.BlockSpec((tt, d), lambda b, j: (b * nj + j, 0))
    dc_next = pl.BlockSpec((CONV_HALO, d), lambda b, j: (jnp.minimum((b * nj + j + 1) * per, last_blk), 0))
    return pl.pallas_call(
        body, name=name, grid=(bl, nj),
        in_specs=[pl.BlockSpec(memory_space=pl.ANY), dc_main, dc_next, main_a, main_g, halo_a, halo_g,
                  _const_spec((CONV_HALO, d)), dc_main],
        out_specs=[pl.BlockSpec((tt, 2 * d), lambda b, j: (b * nj + j, 0)), _const_spec((CONV_HALO, d)), _const_spec((1, d)),
                   _const_spec((d, 2 * d))],
        out_shape=[jax.ShapeDtypeStruct(dp.shape, dp.dtype), jax.ShapeDtypeStruct((CONV_HALO, d), F32),
                   jax.ShapeDtypeStruct((1, d), F32), jax.ShapeDtypeStruct((d, 2 * d), F32)],
        scratch_shapes=[pltpu.VMEM((SUBLANES, tt + CONV_HALO, d), F32), pltpu.VMEM((SUBLANES, tt + CONV_HALO, d), F32),
                        pltpu.VMEM((tt, d), F32), pltpu.VMEM((CONV_HALO, SUBLANES, d), F32)],
        input_output_aliases={0: 0},
        compiler_params=_cparams(("arbitrary", "arbitrary")))(dp, dc, dc, p, p, p, p, conv_w, h1)


def _sgu_stats(bv):
    gv = _gelu(bv)
    mu = jnp.mean(gv, axis=-1, keepdims=True)
    dv = gv - mu
    rstd = lax.rsqrt(jnp.mean(dv * dv, axis=-1, keepdims=True) + LN_EPS)
    return dv * rstd, rstd


def _sgu_fwd(p, wm, bias, ln_g, ln_b, *, name):
    t = p.shape[0]
    d = ln_g.shape[1]
    tt = SGU_CHUNK
    gd = d // SGU_GROUPS

    def body(u_ref, v_ref, wm_ref, bias_ref, lg_ref, lb_ref, sg_ref, vn_ref):
        u = _gelu(u_ref[...].astype(F32))
        vhat, _ = _sgu_stats(v_ref[...].astype(F32))
        vb = (vhat * lg_ref[...] + lb_ref[...]).astype(BF16)
        vn_ref[...] = vb
        for g in range(SGU_GROUPS):
            gs = slice(g * gd, (g + 1) * gd)
            z = _dot(wm_ref[g], vb[:, gs], _NN) + bias_ref[g]
            sg_ref[:, gs] = (u[:, gs] * z).astype(BF16)

    rs = _row_spec(tt, d)
    return pl.pallas_call(
        body, name=name, grid=(t // tt,),
        in_specs=[_row_spec(tt, d, 2), _row_spec(tt, d, 3), _const_spec(wm.shape), _const_spec(bias.shape),
                  _const_spec((1, d)), _const_spec((1, d))],
        out_specs=[rs, rs], out_shape=[jax.ShapeDtypeStruct((t, d), BF16), jax.ShapeDtypeStruct((t, d), BF16)],
        compiler_params=_cparams(("parallel",)))(p, p, wm, bias, ln_g, ln_b)


def _sgu_bwd(dp, dy_b, w_out, p, vn, wm, wmt, bias, ln_g, *, name):
    t = p.shape[0]
    d = ln_g.shape[1]
    tt = SGU_CHUNK
    gd = d // SGU_GROUPS
    nsteps = t // tt

    def body(dp_in, dyb_ref, wout_ref, u_ref, v_ref, vn_ref, wm_ref, wmt_ref, bias_ref, lg_ref,
             dp_ref, dw_ref, dbs_ref, dlg_ref, dlb_ref, dz_acc):
        del dp_in
        i = pl.program_id(0)

        @pl.when(i == 0)
        def _():
            dw_ref[...] = jnp.zeros_like(dw_ref)
            dlg_ref[...] = jnp.zeros_like(dlg_ref)
            dlb_ref[...] = jnp.zeros_like(dlb_ref)
            dz_acc[...] = jnp.zeros_like(dz_acc)

        bu = u_ref[...].astype(F32)
        bv = v_ref[...].astype(F32)
        u = _gelu(bu)
        vhat, rstd = _sgu_stats(bv)
        vb = vn_ref[...]
        dsg = _dot(dyb_ref[...], wout_ref[...], _NT)
        row = lax.broadcasted_iota(jnp.int32, (tt, tt), 0)
        col = lax.broadcasted_iota(jnp.int32, (tt, tt), 1)
        causal = col <= row
        du_parts, dv_parts = [], []
        for g in range(SGU_GROUPS):
            gs = slice(g * gd, (g + 1) * gd)
            z = _dot(wm_ref[g], vb[:, gs], _NN) + bias_ref[g]
            du_parts.append(dsg[:, gs] * z)
            dz = dsg[:, gs] * u[:, gs]
            dz_acc[:, gs] += dz
            dzb = dz.astype(BF16)
            dw_ref[g] += jnp.where(causal, _dot(dzb, vb[:, gs], _NT), 0.0)
            dv_parts.append(_dot(wmt_ref[g], dzb, _NN))
        du = jnp.concatenate(du_parts, axis=1)
        dv = jnp.concatenate(dv_parts, axis=1)
        dp_ref[:, 0:d] = (du * _gelu_grad(bu)).astype(BF16)
        dlb_ref[...] += jnp.sum(dv, axis=0, keepdims=True)
        dlg_ref[...] += jnp.sum(dv * vhat, axis=0, keepdims=True)
        dvh = dv * lg_ref[...]
        dgv = rstd * (dvh - jnp.mean(dvh, axis=-1, keepdims=True) - vhat * jnp.mean(dvh * vhat, axis=-1, keepdims=True))
        dp_ref[:, d:2 * d] = (dgv * _gelu_grad(bv)).astype(BF16)

        @pl.when(i == nsteps - 1)
        def _():
            ones = jnp.ones((8, gd), F32)
            for g in range(SGU_GROUPS):
                gs = slice(g * gd, (g + 1) * gd)
                tot = lax.dot_general(ones, dz_acc[:, gs], (_NT, ((), ())), preferred_element_type=F32,
                                      precision=lax.Precision.HIGHEST)
                dbs_ref[g:g + 1, :] = tot[0:1, :]

    rs = _row_spec(tt, d)
    c1 = _const_spec((1, d))
    return pl.pallas_call(
        body, name=name, grid=(nsteps,),
        in_specs=[pl.BlockSpec(memory_space=pl.ANY), rs, _const_spec(w_out.shape), _row_spec(tt, d, 2), _row_spec(tt, d, 3),
                  rs, _const_spec(wm.shape), _const_spec(wmt.shape), _const_spec(bias.shape), c1],
        out_specs=[pl.BlockSpec((tt, 2 * d), lambda i: (i, 1)), _const_spec(wm.shape), _const_spec((SGU_GROUPS, tt)), c1, c1],
        out_shape=[jax.ShapeDtypeStruct(dp.shape, dp.dtype), jax.ShapeDtypeStruct(wm.shape, F32),
                   jax.ShapeDtypeStruct((SGU_GROUPS, tt), F32), jax.ShapeDtypeStruct((1, d), F32),
                   jax.ShapeDtypeStruct((1, d), F32)],
        scratch_shapes=[pltpu.VMEM((tt, d), F32)],
        input_output_aliases={0: 0},
        compiler_params=_cparams(("arbitrary",)))(dp, dy_b, w_out, p, p, vn, wm, wmt, bias, ln_g)


def _gates_fwd(p, ya, yb, b_gate, *, name):
    t, d = ya.shape
    tt = min(TOKEN_TILE, t)

    def body(ga_ref, gb_ref, ya_ref, yb_ref, bg_ref, o_ref):
        sa = _sigmoid(ga_ref[...].astype(F32) + bg_ref[0:1, :])
        sb = _sigmoid(gb_ref[...].astype(F32) + bg_ref[1:2, :])
        o_ref[...] = (sa * ya_ref[...].astype(F32) + sb * yb_ref[...].astype(F32)).astype(BF16)

    rs = _row_spec(tt, d)
    return pl.pallas_call(
        body, name=name, grid=(t // tt,),
        in_specs=[_row_spec(tt, d, 4), _row_spec(tt, d, 5), rs, rs, _const_spec(b_gate.shape)],
        out_specs=rs, out_shape=jax.ShapeDtypeStruct((t, d), BF16),
        compiler_params=_cparams(("parallel",)))(p, p, ya, yb, b_gate)


def _gates_bwd(dmerged, p, ya, yb, b_gate, *, name):
    t, d = ya.shape
    tt = min(TOKEN_TILE, t)

    def body(dm_ref, ga_ref, gb_ref, ya_ref, yb_ref, bg_ref, dp_ref, dya_ref, dyb_ref, dbg_ref):
        @pl.when(pl.program_id(0) == 0)
        def _():
            dbg_ref[...] = jnp.zeros_like(dbg_ref)

        dm = dm_ref[...].astype(F32)
        sa = _sigmoid(ga_ref[...].astype(F32) + bg_ref[0:1, :])
        sb = _sigmoid(gb_ref[...].astype(F32) + bg_ref[1:2, :])
        dya_ref[...] = (dm * sa).astype(BF16)
        dyb_ref[...] = (dm * sb).astype(BF16)
        dga = dm * ya_ref[...].astype(F32) * sa * (1.0 - sa)
        dgb = dm * yb_ref[...].astype(F32) * sb * (1.0 - sb)
        dp_ref[:, 0:d] = dga.astype(BF16)
        dp_ref[:, d:2 * d] = dgb.astype(BF16)
        dbg_ref[0:1, :] += jnp.sum(dga, axis=0, keepdims=True)
        dbg_ref[1:2, :] += jnp.sum(dgb, axis=0, keepdims=True)

    rs = _row_spec(tt, d)
    return pl.pallas_call(
        body, name=name, grid=(t // tt,),
        in_specs=[rs, _row_spec(tt, d, 4), _row_spec(tt, d, 5), rs, rs, _const_spec(b_gate.shape)],
        out_specs=[pl.BlockSpec((tt, 2 * d), lambda i: (i, 2)), rs, rs, _const_spec((8, d))],
        out_shape=[jax.ShapeDtypeStruct(p.shape, BF16), jax.ShapeDtypeStruct((t, d), BF16),
                   jax.ShapeDtypeStruct((t, d), BF16), jax.ShapeDtypeStruct((8, d), F32)],
        compiler_params=_cparams(("arbitrary",)))(dmerged, p, p, ya, yb, b_gate)


def _softmax_rows(s):
    e = jnp.exp(s - jnp.max(s, axis=-1, keepdims=True))
    return e / jnp.sum(e, axis=-1, keepdims=True)


def _attn_fwd(q, kv, x1, w_xo, gain, *, bl, s, name):
    t, d = q.shape
    mlen = kv.shape[0] // bl
    hd = d // HEADS
    tq = min(ATTN_TILE, s)
    nq = s // tq
    scale = hd ** -0.5

    def body(q_ref, kv_ref, x1_ref, w_ref, g_ref, o_ref, x2_ref, h_ref):
        for h in range(HEADS):
            hs = slice(h * hd, (h + 1) * hd)
            vs = slice(d + h * hd, d + (h + 1) * hd)
            pr = _softmax_rows(_dot(q_ref[:, hs], kv_ref[:, hs], _NT) * scale)
            o_ref[:, hs] = _dot(pr.astype(BF16), kv_ref[:, vs], _NN).astype(BF16)
        x2 = x1_ref[...] + _dot(o_ref[...], w_ref[...], _NN)
        x2_ref[...] = x2
        h_ref[...] = _rms_apply(x2, g_ref[...]).astype(BF16)

    qs = pl.BlockSpec((tq, d), lambda b, j: (b * nq + j, 0))
    return pl.pallas_call(
        body, name=name, grid=(bl, nq),
        in_specs=[qs, pl.BlockSpec((mlen, 2 * d), lambda b, j: (b, 0)), qs, _const_spec(w_xo.shape), _const_spec((1, d))],
        out_specs=[qs, qs, qs],
        out_shape=[jax.ShapeDtypeStruct((t, d), BF16), jax.ShapeDtypeStruct((t, d), F32), jax.ShapeDtypeStruct((t, d), BF16)],
        compiler_params=_cparams(("parallel", "parallel")))(q, kv, x1, w_xo, gain)


def _attn_bwd(q, kv, do, *, bl, s, name):
    t, d = q.shape
    mlen = kv.shape[0] // bl
    hd = d // HEADS
    tq = min(ATTN_TILE, s)
    nq = s // tq
    scale = hd ** -0.5

    def body(q_ref, kv_ref, do_ref, dq_ref, dkv_ref):
        @pl.when(pl.program_id(1) == 0)
        def _():
            dkv_ref[...] = jnp.zeros_like(dkv_ref)

        for h in range(HEADS):
            hs = slice(h * hd, (h + 1) * hd)
            vs = slice(d + h * hd, d + (h + 1) * hd)
            qh, kh, vh, doh = q_ref[:, hs], kv_ref[:, hs], kv_ref[:, vs], do_ref[:, hs]
            pr = _softmax_rows(_dot(qh, kh, _NT) * scale)
            dpr = _dot(doh, vh, _NT)
            dkv_ref[:, vs] += _dot(pr.astype(BF16), doh, _TN)
            ds = (pr * (dpr - jnp.sum(dpr * pr, axis=-1, keepdims=True)) * scale).astype(BF16)
            dq_ref[:, hs] = _dot(ds, kh, _NN).astype(BF16)
            dkv_ref[:, hs] += _dot(ds, qh, _TN)

    qs = pl.BlockSpec((tq, d), lambda b, j: (b * nq + j, 0))
    ks = pl.BlockSpec((mlen, 2 * d), lambda b, j: (b, 0))
    return pl.pallas_call(
        body, name=name, grid=(bl, nq), in_specs=[qs, ks, qs], out_specs=[qs, ks],
        out_shape=[jax.ShapeDtypeStruct((t, d), BF16), jax.ShapeDtypeStruct(kv.shape, F32)],
        compiler_params=_cparams(("parallel", "arbitrary")))(q, kv, do)


def _swiglu_fwd(gu, *, name):
    t, f2 = gu.shape
    f = f2 // 2
    tt = min(TOKEN_TILE, t)

    def body(gu_ref, o_ref):
        gt = gu_ref[:, 0:f].astype(F32)
        up = gu_ref[:, f:f2].astype(F32)
        o_ref[...] = (gt * _sigmoid(gt) * up).astype(BF16)

    return pl.pallas_call(
        body, name=name, grid=(t // tt,), in_specs=[_row_spec(tt, f2)], out_specs=_row_spec(tt, f),
        out_shape=jax.ShapeDtypeStruct((t, f), BF16), compiler_params=_cparams(("parallel",)))(gu)


def _swiglu_bwd(gu, dact, *, name):
    t, f2 = gu.shape
    f = f2 // 2
    tt = min(TOKEN_TILE, t)

    def body(gu_ref, da_ref, o_ref):
        gt = gu_ref[:, 0:f].astype(F32)
        up = gu_ref[:, f:f2].astype(F32)
        da = da_ref[...].astype(F32)
        sg = _sigmoid(gt)
        o_ref[:, 0:f] = (da * up * sg * (1.0 + gt * (1.0 - sg))).astype(BF16)
        o_ref[:, f:f2] = (da * gt * sg).astype(BF16)

    return pl.pallas_call(
        body, name=name, grid=(t // tt,), in_specs=[_row_spec(tt, f2), _row_spec(tt, f)], out_specs=_row_spec(tt, f2),
        out_shape=jax.ShapeDtypeStruct((t, f2), BF16), compiler_params=_cparams(("parallel",)))(gu, dact)


def _mesh_pos():
    return lax.axis_index("x"), lax.axis_index("y"), lax.axis_index("c")


def _all_gather(arrs, *, name):
    n = len(arrs)
    hbm = pl.BlockSpec(memory_space=pl.ANY)

    def body(*refs):
        ins, outs = refs[:n], refs[n:2 * n]
        send_sems, recv_sems, loc_sems = refs[2 * n:]
        x, y, c = _mesh_pos()
        me, sib = (x, y, c), (x, y, 1 - c)
        chips = [(1 - x, y), (x, 1 - y), (1 - x, 1 - y)]

        def idx(dev):
            return 4 * dev[0] + 2 * dev[1] + dev[2]

        def copy(w, k, block, to, from_input=False):
            return pltpu.make_async_remote_copy(
                src_ref=ins[w] if from_input else outs[w].at[idx(block)], dst_ref=outs[w].at[idx(block)],
                send_sem=send_sems.at[w, k], recv_sem=recv_sems.at[w, k], device_id=to, device_id_type=MESH_ID)

        own = [pltpu.make_async_copy(ins[w], outs[w].at[idx(me)], loc_sems.at[w]) for w in range(n)]
        for cp in own:
            cp.start()
        first = []
        for w in range(n):
            first.append(copy(w, 0, me, sib, True))
            first += [copy(w, 1 + j, me, (*chip, c), True) for j, chip in enumerate(chips)]
        for cp in first:
            cp.start()
        passed = []
        for j, chip in enumerate(chips):
            for w in range(n):
                copy(w, 1 + j, (*chip, c), me).wait_recv()
                fwd = copy(w, 4 + j, (*chip, c), sib)
                fwd.start()
                passed.append(fwd)
        for w in range(n):
            copy(w, 0, sib, me).wait_recv()
            for j, chip in enumerate(chips):
                copy(w, 4 + j, (*chip, 1 - c), me).wait_recv()
        for cp in first + passed:
            cp.wait_send()
        for cp in own:
            cp.wait()

    return pl.pallas_call(
        body, name=name, in_specs=[hbm] * n, out_specs=[hbm] * n,
        out_shape=[jax.ShapeDtypeStruct((N_DEV, *a.shape), a.dtype) for a in arrs],
        scratch_shapes=[pltpu.SemaphoreType.DMA((n, 7)), pltpu.SemaphoreType.DMA((n, 7)), pltpu.SemaphoreType.DMA((n,))],
    )(*arrs)


_HBM = pl.BlockSpec(memory_space=pltpu.HBM)
_SEM = pl.BlockSpec(memory_space=pltpu.SEMAPHORE)
_ANY = pl.BlockSpec(memory_space=pl.ANY)
_EFFECT = pltpu.SideEffectType.DATAFLOW_SIDE_EFFECTING
N_PEERS = N_DEV - 1


def _related(pos, r):
    x, y, c = pos
    return (1 - x if r & 4 else x, 1 - y if r & 2 else y, 1 - c if r & 1 else c)


def _dev_index(dev):
    return 4 * dev[0] + 2 * dev[1] + dev[2]


def _in_hbm(a):
    return pltpu.with_memory_space_constraint(a, pltpu.HBM)


def _split_copies(kind, srcs, lands, send_sems, recv_sems):
    pos = _mesh_pos()
    me = _dev_index(pos)
    out = []
    for w in range(len(srcs)):
        for r in range(1, N_DEV):
            peer = _related(pos, r)
            if kind == "gather":
                src, dst_here, dst_there = srcs[w], lands[w].at[_dev_index(peer)], lands[w].at[me]
            else:
                src, dst_here, dst_there = srcs[w].at[_dev_index(peer)], lands[w].at[r - 1], lands[w].at[r - 1]
            out.append((src, dst_here, dst_there, send_sems.at[w * N_PEERS + r - 1], recv_sems.at[w * N_PEERS + r - 1], peer))
    return out


def _copy_start(kind, srcs, land_shapes, *, name):
    n = len(srcs)

    def body(*refs):
        src_refs, land_refs = refs[:n], refs[n:2 * n]
        send_sems, recv_sems = refs[2 * n], refs[2 * n + 1]
        token = refs[-1]
        for src, _, dst, ssem, rsem, peer in _split_copies(kind, src_refs, land_refs, send_sems, recv_sems):
            pltpu.make_async_remote_copy(src_ref=src, dst_ref=dst, send_sem=ssem, recv_sem=rsem, device_id=peer,
                                         device_id_type=MESH_ID).start()
        token[...] = jnp.zeros_like(token)

    lands = [_in_hbm(lax.empty(shape, s.dtype)) for s, shape in zip(srcs, land_shapes)]
    res = pl.pallas_call(
        body, name=name,
        out_shape=(pltpu.SemaphoreType.DMA((n * N_PEERS,)), pltpu.SemaphoreType.DMA((n * N_PEERS,)),
                   *[pltpu.HBM(s.shape, s.dtype) for s in srcs], *[pltpu.HBM(l.shape, l.dtype) for l in lands],
                   jax.ShapeDtypeStruct((8, 128), F32)),
        in_specs=[_HBM] * (2 * n), out_specs=(_SEM, _SEM, *[_HBM] * (2 * n), pl.BlockSpec(memory_space=pltpu.VMEM)),
        input_output_aliases={i: 2 + i for i in range(2 * n)},
        compiler_params=pltpu.CompilerParams(has_side_effects=_EFFECT),
    )(*[_in_hbm(s) for s in srcs], *lands)
    return res[0], res[1], list(res[2:2 + n]), list(res[2 + n:2 + 2 * n]), res[-1]


def _copy_wait(kind, send_sems, recv_sems, srcs, lands, after, *, name):
    n = len(srcs)

    def body(*refs):
        src_refs, land_refs = refs[:n], refs[n:2 * n]
        ssems, rsems = refs[2 * n], refs[2 * n + 1]
        for src, dst, _, ssem, rsem, peer in _split_copies(kind, src_refs, land_refs, ssems, rsems):
            cp = pltpu.make_async_remote_copy(src_ref=src, dst_ref=dst, send_sem=ssem, recv_sem=rsem, device_id=peer,
                                              device_id_type=MESH_ID)
            cp.wait_send()
            cp.wait_recv()

    res = pl.pallas_call(
        body, name=name,
        out_shape=(*[pltpu.HBM(s.shape, s.dtype) for s in srcs], *[pltpu.HBM(l.shape, l.dtype) for l in lands]),
        in_specs=[_HBM] * (2 * n) + [_SEM, _SEM, _ANY], out_specs=tuple([_HBM] * (2 * n)),
        input_output_aliases={i: i for i in range(2 * n)},
        compiler_params=pltpu.CompilerParams(has_side_effects=_EFFECT),
    )(*srcs, *lands, send_sems, recv_sems, after)
    return list(res[:n]), list(res[n:])


def _row_tile(rows):
    return rows if rows <= 512 else 256


def _adamw_math(w, g, m, v):
    m2 = ADAM_B1 * m + (1.0 - ADAM_B1) * g
    v2 = ADAM_B2 * v + (1.0 - ADAM_B2) * (g * g)
    m_hat = m2 / (1.0 - ADAM_B1 ** ADAM_STEP)
    v_hat = v2 / (1.0 - ADAM_B2 ** ADAM_STEP)
    delta = -ADAM_LR * (m_hat / (jnp.sqrt(v_hat) + ADAM_EPS) + ADAM_WD * w)
    return delta, m2, v2


def _adamw_shard(partials, landed, dev, w, m, v, *, name):
    r, c = w.shape
    tr = _row_tile(r)

    def body(dev_ref, p_ref, l_ref, w_ref, m_ref, v_ref, g_out, d_out, m_out, v_out):
        del dev_ref
        g = p_ref[...].astype(F32)
        for k in range(N_PEERS):
            g = g + l_ref[k].astype(F32)
        delta, m2, v2 = _adamw_math(w_ref[...], g, m_ref[...], v_ref[...])
        g_out[...] = g
        d_out[...] = delta
        m_out[...] = m2
        v_out[...] = v2

    blk = pl.BlockSpec((tr, c), lambda i, dev_ref: (i, 0))
    gs = pltpu.PrefetchScalarGridSpec(
        num_scalar_prefetch=1, grid=(r // tr,),
        in_specs=[pl.BlockSpec((None, tr, c), lambda i, dev_ref: (dev_ref[0], i, 0)),
                  pl.BlockSpec((N_PEERS, tr, c), lambda i, dev_ref: (0, i, 0)), blk, blk, blk],
        out_specs=[blk] * 4)
    return pl.pallas_call(
        body, name=name, grid_spec=gs, out_shape=[jax.ShapeDtypeStruct((r, c), F32)] * 4,
        compiler_params=_cparams(("parallel",)))(dev, partials, landed, w, m, v)


def _adamw_small(parts, dev, w, m, v, *, name, col_block):
    _, r, d = parts.shape
    cols = w.shape[1]

    def body(dev_ref, p_ref, w_ref, m_ref, v_ref, g_out, d_out, m_out, v_out):
        del dev_ref
        g = p_ref[0]
        for k in range(1, N_DEV):
            g = g + p_ref[k]
        delta, m2, v2 = _adamw_math(w_ref[...], g, m_ref[...], v_ref[...])
        g_out[...] = g
        d_out[...] = delta
        m_out[...] = m2
        v_out[...] = v2

    blk = pl.BlockSpec((r, cols), lambda i, dev_ref: (0, 0))
    pidx = (lambda i, dev_ref: (0, 0, dev_ref[0])) if col_block else (lambda i, dev_ref: (0, 0, 0))
    gs = pltpu.PrefetchScalarGridSpec(
        num_scalar_prefetch=1, grid=(1,),
        in_specs=[pl.BlockSpec((N_DEV, r, cols), pidx), blk, blk, blk], out_specs=[blk] * 4)
    return pl.pallas_call(
        body, name=name, grid_spec=gs, out_shape=[jax.ShapeDtypeStruct((r, cols), F32)] * 4,
        compiler_params=_cparams(("arbitrary",)))(dev, parts, w, m, v)


def _pad_rows(a, rows):
    return jnp.pad(a, ((0, rows - a.shape[0]), (0, 0)))


def _unblock_cols(g):
    return jnp.transpose(g, (1, 0, 2)).reshape(g.shape[1], N_DEV * g.shape[2])


def _block_cols(full):
    r, c8 = full.shape
    return jnp.transpose(full.reshape(r, N_DEV, c8 // N_DEV), (1, 0, 2))


def kernel(x, mem, norm_mix, w_in, b_gate, conv_w, conv_b, conv_ln_g, conv_ln_b, w_conv_out, sgu_ln_g, sgu_ln_b, sgu_w, sgu_b, w_sgu_out, w_mix_out, norm_xattn, norm_mem, w_q, w_kv, w_xo, norm_ffn, w_gu, w_down, norm_final, loss_target, m_norm_mix, m_w_in, m_b_gate, m_conv_w, m_conv_b, m_conv_ln_g, m_conv_ln_b, m_w_conv_out, m_sgu_ln_g, m_sgu_ln_b, m_sgu_w, m_sgu_b, m_w_sgu_out, m_w_mix_out, m_norm_xattn, m_norm_mem, m_w_q, m_w_kv, m_w_xo, m_norm_ffn, m_w_gu, m_w_down, m_norm_final, v_norm_mix, v_w_in, v_b_gate, v_conv_w, v_conv_b, v_conv_ln_g, v_conv_ln_b, v_w_conv_out, v_sgu_ln_g, v_sgu_ln_b, v_sgu_w, v_sgu_b, v_w_sgu_out, v_w_mix_out, v_norm_xattn, v_norm_mem, v_w_q, v_w_kv, v_w_xo, v_norm_ffn, v_w_gu, v_w_down, v_norm_final):
    given = dict(locals())
    bl, s, d = x.shape
    t = bl * s
    xf = x.reshape(t, d)
    tgt = loss_target.reshape(t, d)
    memf = mem.reshape(bl * mem.shape[1], d)
    cx, cy, cc = lax.axis_index("x"), lax.axis_index("y"), lax.axis_index("c")
    dev = 4 * cx + 2 * cy + cc
    dev_id = dev.astype(jnp.int32).reshape(1)
    col_sharded = ["w_in", "w_kv", "w_gu"]

    def full_weight(name, blocks):
        return _unblock_cols(blocks) if name in col_sharded else blocks.reshape(N_DEV * blocks.shape[1], blocks.shape[2])

    g_in, g_bg, g_cw = _all_gather([w_in[0].astype(BF16), _pad_rows(b_gate[0], 8), _pad_rows(conv_w[0], CONV_HALO)],
                                   name="gather_w_in")
    early = ["w_conv_out", "w_sgu_out", "w_mix_out", "w_q", "w_kv", "w_xo"]
    late = ["w_gu", "w_down"]
    shards = {n: given[n][0].astype(BF16) for n in early + late}
    started = {}
    for grp, names in (("early", early), ("late", late)):
        srcs = [shards[n] for n in names]
        started[grp] = _copy_start("gather", srcs, [(N_DEV, *a.shape) for a in srcs], name=f"gather_{grp}_start")
    token = started["early"][4][0:1, 0:1] + started["late"][4][0:1, 0:1]
    wfull = {"w_in": _unblock_cols(g_in)}
    bg_full = _unblock_cols(g_bg)
    cw_full = _unblock_cols(g_cw)

    def finish_gather(grp, names, after):
        ssem, rsem, srcs, lands, _ = started[grp]
        _, lands = _copy_wait("gather", ssem, rsem, srcs, lands, after, name=f"gather_{grp}_wait")
        for n, land in zip(names, lands):
            wfull[n] = full_weight(n, lax.dynamic_update_index_in_dim(land, shards[n], dev, 0))

    tri = jnp.tril(jnp.ones((SGU_CHUNK, SGU_CHUNK), bool))
    wm32 = jnp.where(tri[None], sgu_w[0], 0.0)
    wm = wm32.astype(BF16)
    wmt = jnp.transpose(wm32, (0, 2, 1)).astype(BF16)
    sgu_bias = jnp.broadcast_to(sgu_b[0][:, :, None], (SGU_GROUPS, SGU_CHUNK, d // SGU_GROUPS))

    h1, p = _in_proj(xf, norm_mix + token, wfull["w_in"], name="in_proj")
    c_conv, a_act = _conv_fwd(p, cw_full, conv_b, conv_ln_g, conv_ln_b, bl=bl, s=s, name="conv_fwd")
    sg, vn = _sgu_fwd(p, wm, sgu_bias, sgu_ln_g, sgu_ln_b, name="sgu_fwd")
    finish_gather("early", early, a_act[0:16, 0:128] + sg[0:16, 0:128])
    y_a = _matmul(a_act, wfull["w_conv_out"], mode="nn", out_dtype=BF16, name="mm_conv_out", tm=1024, tn=1024, tk=1024)
    y_b = _matmul(sg, wfull["w_sgu_out"], mode="nn", out_dtype=BF16, name="mm_sgu_out", tm=1024, tn=1024, tk=1024)
    merged, x1, h2, q = _mix_out(p, y_a, y_b, bg_full, xf, wfull["w_mix_out"], norm_xattn, wfull["w_q"], name="mix_out")
    mem_n = _rms_fwd(memf, norm_mem, name="rms_mem")
    kv = _matmul(mem_n, wfull["w_kv"], mode="nn", out_dtype=BF16, name="mm_kv", tm=1024, tn=1024, tk=1024)
    o, x2, h3 = _attn_fwd(q, kv, x1, wfull["w_xo"], norm_ffn, bl=bl, s=s, name="attn_fwd")
    finish_gather("late", late, h3)
    gu, act, dx3, loss_part, d_norm_final = _ffn_fwd(h3, x2, tgt, wfull["w_gu"], wfull["w_down"],
                                                     norm_final.reshape(1, d), name="ffn_fwd")
    loss = lax.psum(loss_part[0, 0], ("x", "y", "c"))

    grads = {}
    sent = []

    def send_grads(names, tag):
        blocks = []
        for n in names:
            g = grads[n]
            if g.ndim == 2:
                g = _block_cols(g) if n in col_sharded else g.reshape(N_DEV, -1, g.shape[1])
            blocks.append(g)
        ssem, rsem, srcs, lands, tok = _copy_start("scatter", blocks, [(N_PEERS, *g.shape[1:]) for g in blocks],
                                                   name=f"grads_{tag}_start")
        sent.append((names, ssem, rsem, srcs, lands))
        return tok[0:1, 0:1]

    dgu, dx2, do, d_norm_ffn = _ffn_bwd(dx3, gu, x2, wfull["w_down"], wfull["w_gu"], norm_ffn, wfull["w_xo"], name="ffn_bwd")
    grads["w_down"] = _matmul(act, dx3, mode="tn", out_dtype=BF16, name="mm_dw_down", tm=1408, tn=1024, tk=1024)
    grads["w_gu"] = _matmul(h3, dgu, mode="tn", out_dtype=BF16, name="mm_dw_gu", tm=1024, tn=1408, tk=1024)
    tok = send_grads(["w_down", "w_gu"], "ffn")
    grads["w_xo"] = _matmul(o, dx2, mode="tn", out_dtype=BF16, name="mm_dw_xo", tm=1024, tn=1024, tk=1024)
    dq, dkv = _attn_bwd(q, kv, do, bl=bl, s=s, name="attn_bwd")
    grads["w_kv"] = _matmul(mem_n, dkv, mode="tn", out_dtype=BF16, name="mm_dw_kv", tm=1024, tn=256, tk=1024,
                            col_blocks=N_DEV)
    tok2 = send_grads(["w_xo", "w_kv"], "attn")
    dmem_n = _matmul(dkv, wfull["w_kv"], mode="nt", out_dtype=F32, name="mm_d_mem", tm=512, tn=1024, tk=2048)
    d_norm_mem = _rms_bwd(None, dmem_n, memf, norm_mem, name="rms_mem_bwd", need_dx=False)
    dx1, d_norm_xattn, dw_q = _proj_rms_bwd(dq, dx2, x1, wfull["w_q"], norm_xattn + (tok + tok2), name="q_rms_bwd", h=h2)
    dp, dy_a, dy_b, d_b_gate, dw_mix, dw_in_gates = _gates_bwd_fused(dx1, p, y_a, y_b, bg_full, wfull["w_mix_out"],
                                                                    merged, h1, name="gates_bwd")
    grads["w_q"] = dw_q.astype(BF16)
    grads["w_mix_out"] = dw_mix.astype(BF16)
    grads["w_sgu_out"] = _matmul(sg, dy_b, mode="tn", out_dtype=BF16, name="mm_dw_sgu", tm=1024, tn=1024, tk=1024)
    dc, d_conv_ln_g, d_conv_ln_b, dw_conv = _conv_ln_bwd_fused(dy_a, c_conv, a_act, wfull["w_conv_out"], conv_ln_g,
                                                               conv_ln_b, name="conv_ln_bwd")
    grads["w_conv_out"] = dw_conv.astype(BF16)
    tok = send_grads(["w_q", "w_mix_out", "w_sgu_out", "w_conv_out"], "mixer")
    dp, d_sgu_w, d_sgu_b, d_sgu_ln_g, d_sgu_ln_b = _sgu_bwd(dp, dy_b, wfull["w_sgu_out"], p, vn, wm, wmt, sgu_bias,
                                                             sgu_ln_g + tok, name="sgu_bwd")
    dw_in_sgu = _matmul(h1, dp, mode="tn", out_dtype=BF16, name="mm_dw_in_sgu", tm=1024, tn=1024, tk=2048,
                        b_cols=(2 * d, 2 * d))
    dp, d_conv_w, d_conv_b, dw_in_conv = _conv_bwd(dp, dc, p, cw_full, h1, bl=bl, s=s, name="conv_bwd")
    grads["w_in"] = _block_cols(jnp.concatenate([dw_in_conv.astype(BF16), dw_in_sgu, dw_in_gates.astype(BF16)], axis=1))
    tok = send_grads(["w_in"], "in")
    grad_x, d_norm_mix = _proj_rms_bwd(dp, dx1, xf, wfull["w_in"], norm_mix + tok, name="in_proj_bwd")
    out = {}

    rep_names = ["norm_mix", "conv_b", "conv_ln_g", "conv_ln_b", "sgu_ln_g", "sgu_ln_b", "norm_xattn", "norm_mem",
                 "norm_ffn", "norm_final", "sgu_b"]
    rep_grads = [d_norm_mix, d_conv_b, d_conv_ln_g, d_conv_ln_b, d_sgu_ln_g, d_sgu_ln_b, d_norm_xattn, d_norm_mem,
                 d_norm_ffn, d_norm_final, d_sgu_b.reshape(1, d)]
    nrep = len(rep_names)
    pad = jnp.zeros((16 - nrep, d), F32)
    sgw_rows = SGU_GROUPS * SGU_CHUNK * SGU_CHUNK // d

    def pack_rep(vecs, sgw):
        return jnp.concatenate([v.reshape(1, d) for v in vecs] + [pad, sgw.reshape(sgw_rows, d)], axis=0)

    def pack_col(bg, cw):
        return jnp.concatenate([_pad_rows(bg, 8), _pad_rows(cw, CONV_HALO)], axis=0)

    small_a = pack_rep(rep_grads, d_sgu_w)
    small_b = jnp.concatenate([d_b_gate, d_conv_w], axis=0)
    parts_a, parts_b = _all_gather([small_a, small_b], name="gather_small_grads")
    res_a = _adamw_small(parts_a, dev_id, pack_rep([given[n] for n in rep_names], sgu_w),
                         pack_rep([given["m_" + n] for n in rep_names], m_sgu_w),
                         pack_rep([given["v_" + n] for n in rep_names], v_sgu_w), name="adamw_small", col_block=False)
    res_b = _adamw_small(parts_b, dev_id, pack_col(b_gate[0], conv_w[0]), pack_col(m_b_gate[0], m_conv_w[0]),
                         pack_col(v_b_gate[0], v_conv_w[0]), name="adamw_small_cols", col_block=True)
    for i, n in enumerate(rep_names):
        out[n] = [r[i].reshape(given[n].shape) for r in res_a]
    out["sgu_w"] = [r[16:16 + sgw_rows].reshape(sgu_w.shape) for r in res_a]
    out["b_gate"] = [r[0:2][None] for r in res_b]
    out["conv_w"] = [r[8:8 + CONV_WIDTH][None] for r in res_b]

    for names, ssem, rsem, srcs, lands in sent:
        srcs, lands = _copy_wait("scatter", ssem, rsem, srcs, lands, res_a[0], name=f"grads_{names[0]}_wait")
        for n, partials, landed in zip(names, srcs, lands):
            res = _adamw_shard(partials, landed, dev_id, given[n][0], given["m_" + n][0], given["v_" + n][0],
                               name=f"adamw_{n}")
            out[n] = [r[None] for r in res]

    order = ["norm_mix", "w_in", "b_gate", "conv_w", "conv_b", "conv_ln_g", "conv_ln_b", "w_conv_out", "sgu_ln_g",
             "sgu_ln_b", "sgu_w", "sgu_b", "w_sgu_out", "w_mix_out", "norm_xattn", "norm_mem", "w_q", "w_kv", "w_xo",
             "norm_ffn", "w_gu", "w_down", "norm_final"]
    return (loss, grad_x.reshape(x.shape), *[out[n][0] for n in order], *[out[n][1] for n in order],
            *[out[n][2] for n in order], *[out[n][3] for n in order])
```

```python
import functools

import jax
import jax.numpy as jnp
from jax import lax
from jax.experimental import pallas as pl
from jax.experimental.pallas import tpu as pltpu

F32 = jnp.float32
BF16 = jnp.bfloat16
RMS_EPS = 1e-6
LN_EPS = 1e-5
CONV_WIDTH = 31
CONV_HALO = 32
CONV_ROWS = 64
CONV_COLS = 256
LANES = 128
SGU_CHUNK = 128
SGU_GROUPS = 8
HEADS = 4
N_DEV = 8
ADAM_LR, ADAM_B1, ADAM_B2, ADAM_EPS, ADAM_WD, ADAM_STEP = 0.001, 0.9, 0.999, 1e-08, 0.01, 10
VMEM_LIMIT = 56 * 1024 * 1024
TOKEN_TILE = 256
ATTN_TILE = 1024
MESH_ID = pl.DeviceIdType.MESH

_GELU_K = 0.7978845608028654
_GELU_C = 0.044715


def _cparams(sem=None):
    return pltpu.CompilerParams(dimension_semantics=sem, vmem_limit_bytes=VMEM_LIMIT)


def _sigmoid(v):
    return 1.0 / (1.0 + jnp.exp(-v))


def _gelu(v):
    return 0.5 * v * (1.0 + jnp.tanh(_GELU_K * (v + _GELU_C * v * v * v)))


def _gelu_grad(v):
    th = jnp.tanh(_GELU_K * (v + _GELU_C * v * v * v))
    return 0.5 * (1.0 + th) + 0.5 * v * (1.0 - th * th) * _GELU_K * (1.0 + 3.0 * _GELU_C * v * v)


def _dot(a, b, dims):
    return lax.dot_general(a, b, (dims, ((), ())), preferred_element_type=F32)


_NN = ((1,), (0,))
_NT = ((1,), (1,))
_TN = ((0,), (0,))


def _matmul(a, b, *, mode, out_dtype, name, tm=512, tn=512, tk=512, chunk=None, residual=None, rms_gain=None,
            col_blocks=None, b_cols=None):
    if mode == "nn":
        (m, k), (_, n) = a.shape, b.shape
    elif mode == "nt":
        (m, k), (n, _) = a.shape, b.shape
    else:
        (k, m), (_, n) = a.shape, b.shape
    b_first = 0
    if b_cols is not None:
        assert mode == "tn"
        b_first, n = b_cols
    tm, tn, tk = min(tm, m), min(tn, n), min(tk, k)
    assert b_first % tn == 0
    b_first //= tn
    assert m % tm == 0 and n % tn == 0 and k % tk == 0, (name, a.shape, b.shape, tm, tn, tk)
    nk = k // tk
    dims = {"nn": _NN, "nt": _NT, "tn": _TN}[mode]
    chunk = tn if chunk is None else min(chunk, tn)
    assert tn % chunk == 0
    if rms_gain is not None:
        assert tn == n and chunk == n

    def body(*refs):
        refs = list(refs)
        a_ref, b_ref = refs[:2]
        pos = 2
        r_ref = g_ref = None
        if residual is not None:
            r_ref = refs[pos]
            pos += 1
        if rms_gain is not None:
            g_ref = refs[pos]
            pos += 1
        o_ref = refs[pos]
        pos += 1
        h_ref = None
        if rms_gain is not None:
            h_ref = refs[pos]
            pos += 1
        acc_ref = refs[pos] if nk > 1 else None
        av = a_ref[...].astype(BF16)
        for c0 in range(0, tn, chunk):
            cs = slice(c0, c0 + chunk)
            bv = (b_ref[cs, :] if mode == "nt" else b_ref[:, cs]).astype(BF16)
            part = _dot(av, bv, dims)

            def finish(res, cs=cs):
                if r_ref is not None:
                    res = res + r_ref[:, cs].astype(F32)
                o_ref[:, cs] = res.astype(out_dtype)
                if h_ref is not None:
                    r = lax.rsqrt(jnp.mean(res * res, axis=-1, keepdims=True) + RMS_EPS)
                    h_ref[...] = (res * r * g_ref[...]).astype(BF16)

            if nk == 1:
                finish(part)
            else:
                kk = pl.program_id(2)

                @pl.when(kk == 0)
                def _(part=part, cs=cs):
                    acc_ref[:, cs] = part

                @pl.when(kk > 0)
                def _(part=part, cs=cs):
                    acc_ref[:, cs] += part

                @pl.when(kk == nk - 1)
                def _(finish=finish, cs=cs):
                    finish(acc_ref[:, cs])

    resident = dict(pipeline_mode=pl.Buffered(1)) if (n == tn and nk == 1 and mode != "tn" and m > tm) else {}
    if mode == "nn":
        a_spec = pl.BlockSpec((tm, tk), lambda i, j, kk: (i, kk))
        b_spec = pl.BlockSpec((tk, tn), lambda i, j, kk: (kk, j), **resident)
    elif mode == "nt":
        a_spec = pl.BlockSpec((tm, tk), lambda i, j, kk: (i, kk))
        b_spec = pl.BlockSpec((tn, tk), lambda i, j, kk: (j, kk), **resident)
    else:
        a_spec = pl.BlockSpec((tk, tm), lambda i, j, kk: (kk, i))
        b_spec = pl.BlockSpec((tk, tn), lambda i, j, kk: (kk, j + b_first))
    o_spec = pl.BlockSpec((tm, tn), lambda i, j, kk: (i, j))
    in_specs, args = [a_spec, b_spec], [a, b]
    if residual is not None:
        in_specs.append(o_spec)
        args.append(residual)
    out_shape, out_specs = [jax.ShapeDtypeStruct((m, n), out_dtype)], [o_spec]
    if col_blocks is not None:
        assert residual is None and rms_gain is None and (n // col_blocks) % tn == 0
        per = n // col_blocks // tn
        out_shape = [jax.ShapeDtypeStruct((col_blocks, m, n // col_blocks), out_dtype)]
        out_specs = [pl.BlockSpec((None, tm, tn), lambda i, j, kk: (j // per, i, j % per))]
    if rms_gain is not None:
        in_specs.append(pl.BlockSpec((1, n), lambda i, j, kk: (0, 0)))
        args.append(rms_gain)
        out_shape.append(jax.ShapeDtypeStruct((m, n), BF16))
        out_specs.append(o_spec)
    res = pl.pallas_call(
        body, name=name, grid=(m // tm, n // tn, nk), in_specs=in_specs, out_specs=out_specs, out_shape=out_shape,
        scratch_shapes=[pltpu.VMEM((tm, tn), F32)] if nk > 1 else [],
        compiler_params=_cparams(("parallel", "parallel", "arbitrary")),
    )(*args)
    return res if rms_gain is not None else res[0]


def _row_call(name, t, tm, rows_in, residents, rows_out, accs, body):
    n_in, n_res, n_out, n_acc = len(rows_in), len(residents), len(rows_out), len(accs)
    steps = t // tm
    assert t % tm == 0
    narrow = [i for i, (_, dt) in enumerate(accs) if dt != F32]

    def kernel_body(*refs):
        in_refs, res_refs = refs[:n_in], refs[n_in:n_in + n_res]
        out_refs = refs[n_in + n_res:n_in + n_res + n_out]
        acc_out = list(refs[n_in + n_res + n_out:n_in + n_res + n_out + n_acc])
        scratch = refs[n_in + n_res + n_out + n_acc:]
        acc_refs = list(acc_out)
        for s_ref, i in zip(scratch, narrow):
            acc_refs[i] = s_ref
        if accs:
            @pl.when(pl.program_id(0) == 0)
            def _():
                for acc in acc_refs:
                    acc[...] = jnp.zeros_like(acc)
        body(in_refs, res_refs, out_refs, acc_refs)
        if narrow:
            @pl.when(pl.program_id(0) == steps - 1)
            def _():
                for i in narrow:
                    acc_out[i][...] = acc_refs[i][...].astype(acc_out[i].dtype)

    once = dict(pipeline_mode=pl.Buffered(1)) if steps > 1 else {}
    in_specs = [pl.BlockSpec((tm, cols), lambda i, cb=cb: (i, cb)) for _, cols, cb in rows_in]
    in_specs += [pl.BlockSpec(r.shape, lambda i, nd=r.ndim: (0,) * nd, **once) for r in residents]
    out_specs = [pl.BlockSpec((tm, cols), lambda i, cb=cb: (i, cb)) for _, cols, cb, _ in rows_out]
    out_specs += [pl.BlockSpec(shape, lambda i, nd=len(shape): (0,) * nd) for shape, _ in accs]
    out_shape = [jax.ShapeDtypeStruct((t, total), dt) for total, _, _, dt in rows_out]
    out_shape += [jax.ShapeDtypeStruct(shape, dt) for shape, dt in accs]
    return pl.pallas_call(
        kernel_body, name=name, grid=(steps,), in_specs=in_specs, out_specs=out_specs, out_shape=out_shape,
        scratch_shapes=[pltpu.VMEM(accs[i][0], F32) for i in narrow],
        compiler_params=_cparams(("arbitrary",) if accs else ("parallel",)),
    )(*[a for a, _, _ in rows_in], *residents)


def _rms_apply(xv, gain):
    return xv * lax.rsqrt(jnp.mean(xv * xv, axis=-1, keepdims=True) + RMS_EPS) * gain


def _rms_grad(dres, dh, xv, gain):
    r = lax.rsqrt(jnp.mean(xv * xv, axis=-1, keepdims=True) + RMS_EPS)
    xhat = xv * r
    dxh = dh * gain
    dx = dres + r * (dxh - xhat * jnp.mean(dxh * xhat, axis=-1, keepdims=True))
    return dx, jnp.sum(dh * xhat, axis=0, keepdims=True)


def _in_proj(xf, gain, w_in, *, name):
    t, d = xf.shape
    n = w_in.shape[1]
    chunk = n // 4

    def body(ins, res, outs, accs):
        (x_ref,), (g_ref, w_ref), (h_ref, p_ref) = ins, res, outs
        h = _rms_apply(x_ref[...], g_ref[...]).astype(BF16)
        h_ref[...] = h
        for c0 in range(0, n, chunk):
            p_ref[:, c0:c0 + chunk] = _dot(h, w_ref[:, c0:c0 + chunk], _NN).astype(BF16)

    return _row_call(name, t, min(512, t), [(xf, d, 0)], [gain, w_in], [(d, d, 0, BF16), (n, n, 0, BF16)], [], body)


def _mix_out(p, y_a, y_b, b_gate, xf, w_mix, gain, w_q, *, name):
    t, d = xf.shape

    def body(ins, res, outs, accs):
        ga_ref, gb_ref, ya_ref, yb_ref, x_ref = ins
        bg_ref, wm_ref, g_ref, wq_ref = res
        m_ref, x1_ref, h_ref, q_ref = outs
        sa = _sigmoid(ga_ref[...].astype(F32) + bg_ref[0:1, :])
        sb = _sigmoid(gb_ref[...].astype(F32) + bg_ref[1:2, :])
        merged = (sa * ya_ref[...].astype(F32) + sb * yb_ref[...].astype(F32)).astype(BF16)
        m_ref[...] = merged
        x1 = x_ref[...] + _dot(merged, wm_ref[...], _NN)
        x1_ref[...] = x1
        h = _rms_apply(x1, g_ref[...]).astype(BF16)
        h_ref[...] = h
        q_ref[...] = _dot(h, wq_ref[...], _NN).astype(BF16)

    return _row_call(name, t, min(512, t), [(p, d, 4), (p, d, 5), (y_a, d, 0), (y_b, d, 0), (xf, d, 0)],
                     [b_gate, w_mix, gain, w_q], [(d, d, 0, BF16), (d, d, 0, F32), (d, d, 0, BF16), (d, d, 0, BF16)], [], body)


def _ffn_fwd(h3, x2, target, w_gu_t, w_down, gain, *, name):
    t, d = x2.shape
    f2 = w_gu_t.shape[0]
    f = f2 // 2
    half = f // 2

    def body(ins, res, outs, accs):
        h_ref, x2_ref, t_ref = ins
        wgu_ref, wd_ref, g_ref = res
        gu_ref, act_ref, dx_ref = outs
        loss_ref, dg_ref = accs
        h = h_ref[...]
        x3 = x2_ref[...]
        for c0 in (0, half):
            gt = _dot(h, wgu_ref[c0:c0 + half, :], _NT).astype(BF16)
            up = _dot(h, wgu_ref[f + c0:f + c0 + half, :], _NT).astype(BF16)
            gu_ref[:, c0:c0 + half] = gt
            gu_ref[:, f + c0:f + c0 + half] = up
            gtf = gt.astype(F32)
            act = (gtf * _sigmoid(gtf) * up.astype(F32)).astype(BF16)
            act_ref[:, c0:c0 + half] = act
            x3 = x3 + _dot(act, wd_ref[c0:c0 + half, :], _NN)
        g = g_ref[...]
        r = lax.rsqrt(jnp.mean(x3 * x3, axis=-1, keepdims=True) + RMS_EPS)
        xhat = x3 * r
        err = xhat * g - t_ref[...]
        loss_ref[...] += 0.5 * jnp.sum(jnp.mean(err * err, axis=-1, keepdims=True), axis=0, keepdims=True)
        dy = err * (1.0 / d)
        dg_ref[...] += jnp.sum(dy * xhat, axis=0, keepdims=True)
        dxh = dy * g
        dx_ref[...] = r * (dxh - xhat * jnp.mean(dxh * xhat, axis=-1, keepdims=True))

    return _row_call(name, t, min(256, t), [(h3, d, 0), (x2, d, 0), (target, d, 0)], [w_gu_t, w_down, gain],
                     [(f2, f2, 0, BF16), (f, f, 0, BF16), (d, d, 0, F32)], [((1, 1), F32), ((1, d), F32)], body)


def _ffn_bwd(dx3, gu, x2, w_down, w_gu_t, gain, w_xo, *, name):
    t, d = x2.shape
    f2 = w_gu_t.shape[0]
    f = f2 // 2
    half = f // 2

    def body(ins, res, outs, accs):
        dx3_ref, gu_ref, x2_ref = ins
        wd_ref, wgu_ref, g_ref, wxo_ref = res
        dgu_ref, dx2_ref, do_ref = outs
        (dg_ref,) = accs
        dx3v = dx3_ref[...]
        dxb = dx3v.astype(BF16)
        dh = jnp.zeros(dx3v.shape, F32)
        for c0 in (0, half):
            dact = _dot(dxb, wd_ref[c0:c0 + half, :], _NT)
            gt = gu_ref[:, c0:c0 + half].astype(F32)
            up = gu_ref[:, f + c0:f + c0 + half].astype(F32)
            sg = _sigmoid(gt)
            dgt = (dact * up * sg * (1.0 + gt * (1.0 - sg))).astype(BF16)
            dup = (dact * gt * sg).astype(BF16)
            dgu_ref[:, c0:c0 + half] = dgt
            dgu_ref[:, f + c0:f + c0 + half] = dup
            dh = dh + _dot(dgt, wgu_ref[c0:c0 + half, :], _NN) + _dot(dup, wgu_ref[f + c0:f + c0 + half, :], _NN)
        dx2, dg = _rms_grad(dx3v, dh, x2_ref[...], g_ref[...])
        dx2_ref[...] = dx2
        dg_ref[...] += dg
        do_ref[...] = _dot(dx2.astype(BF16), wxo_ref[...], _NT).astype(BF16)

    return _row_call(name, t, min(256, t), [(dx3, d, 0), (gu, f2, 0), (x2, d, 0)], [w_down, w_gu_t, gain, w_xo],
                     [(f2, f2, 0, BF16), (d, d, 0, F32), (d, d, 0, BF16)], [((1, d), F32)], body)


def _proj_rms_bwd(dy, dres, x, w, gain, *, name, h=None):
    t, d = x.shape
    k = dy.shape[1]

    def body(ins, res, outs, accs):
        dy_ref, dres_ref, x_ref = ins[:3]
        w_ref, g_ref = res
        if h is not None:
            accs[1][...] += _dot(ins[3][...], dy_ref[...], _TN)
        dh = _dot(dy_ref[...], w_ref[...], _NT)
        dx, dg = _rms_grad(dres_ref[...], dh, x_ref[...], g_ref[...])
        outs[0][...] = dx
        accs[0][...] += dg

    rows_in = [(dy, k, 0), (dres, d, 0), (x, d, 0)] + ([(h, d, 0)] if h is not None else [])
    accs = [((1, d), F32)] + ([((d, k), BF16)] if h is not None else [])
    return _row_call(name, t, min(512, t), rows_in, [w, gain], [(d, d, 0, F32)], accs, body)


def _gates_bwd_fused(dx1, p, y_a, y_b, b_gate, w_mix, merged, h1, *, name):
    t, d = y_a.shape

    def body(ins, res, outs, accs):
        dx_ref, ga_ref, gb_ref, ya_ref, yb_ref, m_ref, h1_ref = ins
        bg_ref, wm_ref = res
        dp_ref, dya_ref, dyb_ref = outs
        dbg_ref, dwm_ref, dwin_ref = accs
        dxb = dx_ref[...].astype(BF16)
        dwm_ref[...] += _dot(m_ref[...], dxb, _TN)
        dm = _dot(dxb, wm_ref[...], _NT)
        sa = _sigmoid(ga_ref[...].astype(F32) + bg_ref[0:1, :])
        sb = _sigmoid(gb_ref[...].astype(F32) + bg_ref[1:2, :])
        dya_ref[...] = (dm * sa).astype(BF16)
        dyb_ref[...] = (dm * sb).astype(BF16)
        dga = dm * ya_ref[...].astype(F32) * sa * (1.0 - sa)
        dgb = dm * yb_ref[...].astype(F32) * sb * (1.0 - sb)
        dp_ref[:, 0:d] = dga.astype(BF16)
        dp_ref[:, d:2 * d] = dgb.astype(BF16)
        dbg_ref[0:1, :] += jnp.sum(dga, axis=0, keepdims=True)
        dbg_ref[1:2, :] += jnp.sum(dgb, axis=0, keepdims=True)
        dwin_ref[...] += _dot(h1_ref[...], dp_ref[...], _TN)

    return _row_call(name, t, min(256, t),
                     [(dx1, d, 0), (p, d, 4), (p, d, 5), (y_a, d, 0), (y_b, d, 0), (merged, d, 0), (h1, d, 0)],
                     [b_gate, w_mix], [(p.shape[1], 2 * d, 2, BF16), (d, d, 0, BF16), (d, d, 0, BF16)],
                     [((8, d), F32), ((d, d), BF16), ((d, 2 * d), BF16)], body)


def _conv_ln_bwd_fused(dy_a, c, a_act, w_conv_out, ln_g, ln_b, *, name):
    t, d = c.shape

    def body(ins, res, outs, accs):
        dy_ref, c_ref, act_ref = ins
        w_ref, lg_ref, lb_ref = res
        dlg_ref, dlb_ref, dw_ref = accs
        dw_ref[...] += _dot(act_ref[...], dy_ref[...], _TN)
        dact = _dot(dy_ref[...], w_ref[...], _NT)
        cv = c_ref[...].astype(F32)
        g = lg_ref[...]
        mu = jnp.mean(cv, axis=-1, keepdims=True)
        dv = cv - mu
        rstd = lax.rsqrt(jnp.mean(dv * dv, axis=-1, keepdims=True) + LN_EPS)
        chat = dv * rstd
        aln = chat * g + lb_ref[...]
        sg = _sigmoid(aln)
        daln = dact * (sg * (1.0 + aln * (1.0 - sg)))
        dlb_ref[...] += jnp.sum(daln, axis=0, keepdims=True)
        dlg_ref[...] += jnp.sum(daln * chat, axis=0, keepdims=True)
        dchat = daln * g
        dc = rstd * (dchat - jnp.mean(dchat, axis=-1, keepdims=True)
                     - chat * jnp.mean(dchat * chat, axis=-1, keepdims=True))
        outs[0][...] = dc.astype(BF16)

    return _row_call(name, t, min(512, t), [(dy_a, d, 0), (c, d, 0), (a_act, d, 0)], [w_conv_out, ln_g, ln_b],
                     [(d, d, 0, BF16)], [((1, d), F32), ((1, d), F32), ((d, d), BF16)], body)


def _row_spec(tt, cols, col_block=0):
    return pl.BlockSpec((tt, cols), lambda i: (i, col_block))


def _const_spec(shape):
    return pl.BlockSpec(shape, lambda *_: (0,) * len(shape))


def _rms_fwd(x, gain, *, name):
    t, d = x.shape
    tt = min(TOKEN_TILE, t)

    def body(x_ref, g_ref, h_ref):
        xv = x_ref[...]
        r = lax.rsqrt(jnp.mean(xv * xv, axis=-1, keepdims=True) + RMS_EPS)
        h_ref[...] = (xv * r * g_ref[...]).astype(BF16)

    return pl.pallas_call(
        body, name=name, grid=(t // tt,), in_specs=[_row_spec(tt, d), _const_spec((1, d))],
        out_specs=_row_spec(tt, d), out_shape=jax.ShapeDtypeStruct((t, d), BF16),
        compiler_params=_cparams(("parallel",)))(x, gain)


def _rms_bwd(dres, dh, x, gain, *, name, need_dx=True):
    t, d = x.shape
    tt = min(TOKEN_TILE, t)

    def body(*refs):
        if need_dx:
            dres_ref, dh_ref, x_ref, g_ref, dx_ref, dg_ref = refs
        else:
            dh_ref, x_ref, g_ref, dg_ref = refs

        @pl.when(pl.program_id(0) == 0)
        def _():
            dg_ref[...] = jnp.zeros_like(dg_ref)

        xv = x_ref[...]
        dhv = dh_ref[...].astype(F32)
        r = lax.rsqrt(jnp.mean(xv * xv, axis=-1, keepdims=True) + RMS_EPS)
        xhat = xv * r
        dg_ref[...] += jnp.sum(dhv * xhat, axis=0, keepdims=True)
        if need_dx:
            dxh = dhv * g_ref[...]
            dx_ref[...] = dres_ref[...] + r * (dxh - xhat * jnp.mean(dxh * xhat, axis=-1, keepdims=True))

    rs = _row_spec(tt, d)
    if need_dx:
        in_specs, args = [rs, rs, rs, _const_spec((1, d))], (dres, dh, x, gain)
        out_specs = [rs, _const_spec((1, d))]
        out_shape = [jax.ShapeDtypeStruct((t, d), F32), jax.ShapeDtypeStruct((1, d), F32)]
    else:
        in_specs, args = [rs, rs, _const_spec((1, d))], (dh, x, gain)
        out_specs = [_const_spec((1, d))]
        out_shape = [jax.ShapeDtypeStruct((1, d), F32)]
    res = pl.pallas_call(body, name=name, grid=(t // tt,), in_specs=in_specs, out_specs=out_specs, out_shape=out_shape,
                         compiler_params=_cparams(("arbitrary",)))(*args)
    return res if need_dx else res[0]


def _final_loss(x3, target, gain, *, name):
    t, d = x3.shape
    tt = min(TOKEN_TILE, t)

    def body(x_ref, t_ref, g_ref, loss_ref, dx_ref, dg_ref):
        @pl.when(pl.program_id(0) == 0)
        def _():
            loss_ref[...] = jnp.zeros_like(loss_ref)
            dg_ref[...] = jnp.zeros_like(dg_ref)

        xv = x_ref[...]
        g = g_ref[...]
        r = lax.rsqrt(jnp.mean(xv * xv, axis=-1, keepdims=True) + RMS_EPS)
        xhat = xv * r
        err = xhat * g - t_ref[...]
        loss_ref[...] += 0.5 * jnp.sum(jnp.mean(err * err, axis=-1, keepdims=True), axis=0, keepdims=True)
        dy = err * (1.0 / d)
        dg_ref[...] += jnp.sum(dy * xhat, axis=0, keepdims=True)
        dxh = dy * g
        dx_ref[...] = r * (dxh - xhat * jnp.mean(dxh * xhat, axis=-1, keepdims=True))

    rs = _row_spec(tt, d)
    return pl.pallas_call(
        body, name=name, grid=(t // tt,), in_specs=[rs, rs, _const_spec((1, d))],
        out_specs=[_const_spec((1, 1)), rs, _const_spec((1, d))],
        out_shape=[jax.ShapeDtypeStruct((1, 1), F32), jax.ShapeDtypeStruct((t, d), F32), jax.ShapeDtypeStruct((1, d), F32)],
        compiler_params=_cparams(("arbitrary",)))(x3, target, gain)


SUBLANES = 8
SHIFT_ROWS = 40


def _conv_apply(sbuf_ref, w_ref, out_ref, tt, offsets, bias_ref=None):
    d = out_ref.shape[1]
    for cc in range(d // LANES):
        cs = slice(cc * LANES, (cc + 1) * LANES)
        taps = [jnp.broadcast_to(w_ref[k:k + 1, cs], (SUBLANES, LANES)) for k in range(CONV_WIDTH)]
        bias = None if bias_ref is None else jnp.broadcast_to(bias_ref[:, cs], (SUBLANES, LANES))

        def row_body(r, carry, cs=cs, taps=taps, bias=bias):
            r0 = pl.multiple_of(r * CONV_ROWS, CONV_ROWS)
            for q in range(CONV_ROWS // SUBLANES):
                acc = _tap(sbuf_ref, r0 + q * SUBLANES, cs, offsets[0]) * taps[0]
                for k in range(1, CONV_WIDTH):
                    acc = acc + _tap(sbuf_ref, r0 + q * SUBLANES, cs, offsets[k]) * taps[k]
                if bias is not None:
                    acc = acc + bias
                out_ref[pl.ds(r0 + q * SUBLANES, SUBLANES), cs] = acc
            return carry

        lax.fori_loop(0, tt // CONV_ROWS, row_body, 0)


def _fill_shifts(sbuf_ref, rows):
    d = sbuf_ref.shape[2]
    assert rows % SHIFT_ROWS == 0

    def row_body(i, carry):
        r0 = pl.multiple_of(i * SHIFT_ROWS, SUBLANES)
        for cc in range(d // CONV_COLS):
            cs = slice(cc * CONV_COLS, (cc + 1) * CONV_COLS)
            win = sbuf_ref[0, pl.ds(r0, SHIFT_ROWS + SUBLANES), cs]
            for sh in range(1, SUBLANES):
                sbuf_ref[sh, pl.ds(r0, SHIFT_ROWS), cs] = win[sh:sh + SHIFT_ROWS, :]
        return carry

    lax.fori_loop(0, rows // SHIFT_ROWS, row_body, 0)


def _tap(sbuf_ref, r0, cs, offset):
    sh = offset % SUBLANES
    return sbuf_ref[sh, pl.ds(pl.multiple_of(r0 + (offset - sh), SUBLANES), SUBLANES), cs]


def _conv_specs(bl, s, tt, d, col_a, col_g):
    nj = s // tt
    per = tt // CONV_HALO
    main_a = pl.BlockSpec((tt, d), lambda b, j: (b * nj + j, col_a))
    main_g = pl.BlockSpec((tt, d), lambda b, j: (b * nj + j, col_g))
    prev = lambda b, j: jnp.maximum((b * nj + j) * per - 1, 0)
    halo_a = pl.BlockSpec((CONV_HALO, d), lambda b, j: (prev(b, j), col_a))
    halo_g = pl.BlockSpec((CONV_HALO, d), lambda b, j: (prev(b, j), col_g))
    return main_a, main_g, halo_a, halo_g


def _fill_glu(sbuf_ref, a_ref, g_ref, ha_ref, hg_ref, tt):
    first = pl.program_id(1) == 0
    ha = ha_ref[...].astype(F32)
    hg = hg_ref[...].astype(F32)
    sbuf_ref[0, pl.ds(0, CONV_HALO), :] = jnp.where(first, 0.0, ha * _sigmoid(hg))
    av = a_ref[...].astype(F32)
    gv = g_ref[...].astype(F32)
    sbuf_ref[0, pl.ds(CONV_HALO, tt), :] = av * _sigmoid(gv)
    _fill_shifts(sbuf_ref, tt + CONV_HALO - SUBLANES)


def _conv_fwd(p, conv_w, conv_b, ln_g, ln_b, *, bl, s, name):
    t = p.shape[0]
    d = conv_w.shape[1]
    tt = min(TOKEN_TILE, s)
    off = CONV_HALO - (CONV_WIDTH - 1)

    def body(a_ref, g_ref, ha_ref, hg_ref, w_ref, b_ref, lg_ref, lb_ref, c_ref, act_ref, sbuf_ref, cbuf_ref):
        _fill_glu(sbuf_ref, a_ref, g_ref, ha_ref, hg_ref, tt)

        _conv_apply(sbuf_ref, w_ref, cbuf_ref, tt, [off + k for k in range(CONV_WIDTH)], bias_ref=b_ref)
        cv = cbuf_ref[...]
        c_ref[...] = cv.astype(BF16)
        mu = jnp.mean(cv, axis=-1, keepdims=True)
        dv = cv - mu
        rstd = lax.rsqrt(jnp.mean(dv * dv, axis=-1, keepdims=True) + LN_EPS)
        aln = dv * rstd * lg_ref[...] + lb_ref[...]
        act_ref[...] = (aln * _sigmoid(aln)).astype(BF16)

    main_a, main_g, halo_a, halo_g = _conv_specs(bl, s, tt, d, 0, 1)
    out_spec = pl.BlockSpec((tt, d), lambda b, j: (b * (s // tt) + j, 0))
    return pl.pallas_call(
        body, name=name, grid=(bl, s // tt),
        in_specs=[main_a, main_g, halo_a, halo_g, _const_spec((CONV_HALO, d)), _const_spec((1, d)), _const_spec((1, d)),
                  _const_spec((1, d))],
        out_specs=[out_spec, out_spec],
        out_shape=[jax.ShapeDtypeStruct((t, d), BF16), jax.ShapeDtypeStruct((t, d), BF16)],
        scratch_shapes=[pltpu.VMEM((SUBLANES, tt + CONV_HALO, d), F32), pltpu.VMEM((tt, d), F32)],
        compiler_params=_cparams(("parallel", "parallel")))(p, p, p, p, conv_w, conv_b, ln_g, ln_b)


def _conv_ln_bwd(dact, c, ln_g, ln_b, *, name):
    t, d = c.shape
    tt = min(TOKEN_TILE, t)

    def body(da_ref, c_ref, lg_ref, lb_ref, dc_ref, dlg_ref, dlb_ref):
        @pl.when(pl.program_id(0) == 0)
        def _():
            dlg_ref[...] = jnp.zeros_like(dlg_ref)
            dlb_ref[...] = jnp.zeros_like(dlb_ref)

        cv = c_ref[...].astype(F32)
        g = lg_ref[...]
        mu = jnp.mean(cv, axis=-1, keepdims=True)
        dv = cv - mu
        rstd = lax.rsqrt(jnp.mean(dv * dv, axis=-1, keepdims=True) + LN_EPS)
        chat = dv * rstd
        aln = chat * g + lb_ref[...]
        sg = _sigmoid(aln)
        daln = da_ref[...].astype(F32) * (sg * (1.0 + aln * (1.0 - sg)))
        dlb_ref[...] += jnp.sum(daln, axis=0, keepdims=True)
        dlg_ref[...] += jnp.sum(daln * chat, axis=0, keepdims=True)
        dchat = daln * g
        dc = rstd * (dchat - jnp.mean(dchat, axis=-1, keepdims=True)
                     - chat * jnp.mean(dchat * chat, axis=-1, keepdims=True))
        dc_ref[...] = dc.astype(BF16)

    rs = _row_spec(tt, d)
    cs = _const_spec((1, d))
    return pl.pallas_call(
        body, name=name, grid=(t // tt,), in_specs=[rs, rs, cs, cs], out_specs=[rs, cs, cs],
        out_shape=[jax.ShapeDtypeStruct((t, d), BF16), jax.ShapeDtypeStruct((1, d), F32), jax.ShapeDtypeStruct((1, d), F32)],
        compiler_params=_cparams(("arbitrary",)))(dact, c, ln_g, ln_b)


def _conv_bwd(dp, dc, p, conv_w, h1, *, bl, s, name):
    t = p.shape[0]
    d = conv_w.shape[1]
    tt = min(TOKEN_TILE, s)
    nj = s // tt
    per = tt // CONV_HALO
    off = CONV_HALO - (CONV_WIDTH - 1)
    last_blk = t // CONV_HALO - 1

    def body(dp_in, dc_ref, dcn_ref, a_ref, g_ref, ha_ref, hg_ref, w_ref, h1_ref, dp_ref, dw_ref, db_ref, dwin_ref,
             gbuf_ref, dbuf_ref, dglu_ref, acc_ref):
        del dp_in
        b, j = pl.program_id(0), pl.program_id(1)
        start = jnp.logical_and(b == 0, j == 0)
        end = jnp.logical_and(b == bl - 1, j == nj - 1)

        @pl.when(start)
        def _():
            acc_ref[...] = jnp.zeros_like(acc_ref)
            db_ref[...] = jnp.zeros_like(db_ref)
            dwin_ref[...] = jnp.zeros_like(dwin_ref)

        _fill_glu(gbuf_ref, a_ref, g_ref, ha_ref, hg_ref, tt)
        dcv = dc_ref[...].astype(F32)
        dbuf_ref[0, pl.ds(0, tt), :] = dcv
        dbuf_ref[0, pl.ds(tt, CONV_HALO), :] = jnp.where(j == nj - 1, 0.0, dcn_ref[...].astype(F32))
        _fill_shifts(dbuf_ref, tt + CONV_HALO - SUBLANES)
        db_ref[...] += jnp.sum(dcv, axis=0, keepdims=True)

        for cc in range(d // LANES):
            cs = slice(cc * LANES, (cc + 1) * LANES)

            def row_body(r, accs, cs=cs):
                r0 = pl.multiple_of(r * CONV_ROWS, CONV_ROWS)
                accs = list(accs)
                for q in range(CONV_ROWS // SUBLANES):
                    dcw = dbuf_ref[0, pl.ds(r0 + q * SUBLANES, SUBLANES), cs]
                    for k in range(CONV_WIDTH):
                        accs[k] = accs[k] + dcw * _tap(gbuf_ref, r0 + q * SUBLANES, cs, off + k)
                return tuple(accs)

            zero = jnp.zeros((SUBLANES, LANES), F32)
            accs = lax.fori_loop(0, tt // CONV_ROWS, row_body, (zero,) * CONV_WIDTH)
            for k in range(CONV_WIDTH):
                acc_ref[k, :, cs] += accs[k]

        _conv_apply(dbuf_ref, w_ref, dglu_ref, tt, [CONV_WIDTH - 1 - k for k in range(CONV_WIDTH)])
        dglu = dglu_ref[...]
        av = a_ref[...].astype(F32)
        sg = _sigmoid(g_ref[...].astype(F32))
        dp_ref[:, 0:d] = (dglu * sg).astype(BF16)
        dp_ref[:, d:2 * d] = (dglu * av * sg * (1.0 - sg)).astype(BF16)
        dwin_ref[...] += _dot(h1_ref[...], dp_ref[...], _TN)

        @pl.when(end)
        def _():
            for k in range(CONV_WIDTH):
                dw_ref[k:k + 1, :] = jnp.sum(acc_ref[k], axis=0, keepdims=True)
            dw_ref[CONV_WIDTH:CONV_HALO, :] = jnp.zeros((CONV_HALO - CONV_WIDTH, d), F32)

    main_a, main_g, halo_a, halo_g = _conv_specs(bl, s, tt, d, 0, 1)
    dc_main = pl.BlockSpec((tt, d), lambda b, j: (b * nj + j, 0))
    dc_next = pl.BlockSpec((CONV_HALO, d), lambda b, j: (jnp.minimum((b * nj + j + 1) * per, last_blk), 0))
    return pl.pallas_call(
        body, name=name, grid=(bl, nj),
        in_specs=[pl.BlockSpec(memory_space=pl.ANY), dc_main, dc_next, main_a, main_g, halo_a, halo_g,
                  _const_spec((CONV_HALO, d)), dc_main],
        out_specs=[pl.BlockSpec((tt, 2 * d), lambda b, j: (b * nj + j, 0)), _const_spec((CONV_HALO, d)), _const_spec((1, d)),
                   _const_spec((d, 2 * d))],
        out_shape=[jax.ShapeDtypeStruct(dp.shape, dp.dtype), jax.ShapeDtypeStruct((CONV_HALO, d), F32),
                   jax.ShapeDtypeStruct((1, d), F32), jax.ShapeDtypeStruct((d, 2 * d), F32)],
        scratch_shapes=[pltpu.VMEM((SUBLANES, tt + CONV_HALO, d), F32), pltpu.VMEM((SUBLANES, tt + CONV_HALO, d), F32),
                        pltpu.VMEM((tt, d), F32), pltpu.VMEM((CONV_HALO, SUBLANES, d), F32)],
        input_output_aliases={0: 0},
        compiler_params=_cparams(("arbitrary", "arbitrary")))(dp, dc, dc, p, p, p, p, conv_w, h1)


def _sgu_stats(bv):
    gv = _gelu(bv)
    mu = jnp.mean(gv, axis=-1, keepdims=True)
    dv = gv - mu
    rstd = lax.rsqrt(jnp.mean(dv * dv, axis=-1, keepdims=True) + LN_EPS)
    return dv * rstd, rstd


def _sgu_fwd(p, wm, bias, ln_g, ln_b, *, name):
    t = p.shape[0]
    d = ln_g.shape[1]
    tt = SGU_CHUNK
    gd = d // SGU_GROUPS

    def body(u_ref, v_ref, wm_ref, bias_ref, lg_ref, lb_ref, sg_ref, vn_ref):
        u = _gelu(u_ref[...].astype(F32))
        vhat, _ = _sgu_stats(v_ref[...].astype(F32))
        vb = (vhat * lg_ref[...] + lb_ref[...]).astype(BF16)
        vn_ref[...] = vb
        for g in range(SGU_GROUPS):
            gs = slice(g * gd, (g + 1) * gd)
            z = _dot(wm_ref[g], vb[:, gs], _NN) + bias_ref[g]
            sg_ref[:, gs] = (u[:, gs] * z).astype(BF16)

    rs = _row_spec(tt, d)
    return pl.pallas_call(
        body, name=name, grid=(t // tt,),
        in_specs=[_row_spec(tt, d, 2), _row_spec(tt, d, 3), _const_spec(wm.shape), _const_spec(bias.shape),
                  _const_spec((1, d)), _const_spec((1, d))],
        out_specs=[rs, rs], out_shape=[jax.ShapeDtypeStruct((t, d), BF16), jax.ShapeDtypeStruct((t, d), BF16)],
        compiler_params=_cparams(("parallel",)))(p, p, wm, bias, ln_g, ln_b)


def _sgu_bwd(dp, dy_b, w_out, p, vn, wm, wmt, bias, ln_g, *, name):
    t = p.shape[0]
    d = ln_g.shape[1]
    tt = SGU_CHUNK
    gd = d // SGU_GROUPS
    nsteps = t // tt

    def body(dp_in, dyb_ref, wout_ref, u_ref, v_ref, vn_ref, wm_ref, wmt_ref, bias_ref, lg_ref,
             dp_ref, dw_ref, dbs_ref, dlg_ref, dlb_ref, dz_acc):
        del dp_in
        i = pl.program_id(0)

        @pl.when(i == 0)
        def _():
            dw_ref[...] = jnp.zeros_like(dw_ref)
            dlg_ref[...] = jnp.zeros_like(dlg_ref)
            dlb_ref[...] = jnp.zeros_like(dlb_ref)
            dz_acc[...] = jnp.zeros_like(dz_acc)

        bu = u_ref[...].astype(F32)
        bv = v_ref[...].astype(F32)
        u = _gelu(bu)
        vhat, rstd = _sgu_stats(bv)
        vb = vn_ref[...]
        dsg = _dot(dyb_ref[...], wout_ref[...], _NT)
        row = lax.broadcasted_iota(jnp.int32, (tt, tt), 0)
        col = lax.broadcasted_iota(jnp.int32, (tt, tt), 1)
        causal = col <= row
        du_parts, dv_parts = [], []
        for g in range(SGU_GROUPS):
            gs = slice(g * gd, (g + 1) * gd)
            z = _dot(wm_ref[g], vb[:, gs], _NN) + bias_ref[g]
            du_parts.append(dsg[:, gs] * z)
            dz = dsg[:, gs] * u[:, gs]
            dz_acc[:, gs] += dz
            dzb = dz.astype(BF16)
            dw_ref[g] += jnp.where(causal, _dot(dzb, vb[:, gs], _NT), 0.0)
            dv_parts.append(_dot(wmt_ref[g], dzb, _NN))
        du = jnp.concatenate(du_parts, axis=1)
        dv = jnp.concatenate(dv_parts, axis=1)
        dp_ref[:, 0:d] = (du * _gelu_grad(bu)).astype(BF16)
        dlb_ref[...] += jnp.sum(dv, axis=0, keepdims=True)
        dlg_ref[...] += jnp.sum(dv * vhat, axis=0, keepdims=True)
        dvh = dv * lg_ref[...]
        dgv = rstd * (dvh - jnp.mean(dvh, axis=-1, keepdims=True) - vhat * jnp.mean(dvh * vhat, axis=-1, keepdims=True))
        dp_ref[:, d:2 * d] = (dgv * _gelu_grad(bv)).astype(BF16)

        @pl.when(i == nsteps - 1)
        def _():
            ones = jnp.ones((8, gd), F32)
            for g in range(SGU_GROUPS):
                gs = slice(g * gd, (g + 1) * gd)
                tot = lax.dot_general(ones, dz_acc[:, gs], (_NT, ((), ())), preferred_element_type=F32,
                                      precision=lax.Precision.HIGHEST)
                dbs_ref[g:g + 1, :] = tot[0:1, :]

    rs = _row_spec(tt, d)
    c1 = _const_spec((1, d))
    return pl.pallas_call(
        body, name=name, grid=(nsteps,),
        in_specs=[pl.BlockSpec(memory_space=pl.ANY), rs, _const_spec(w_out.shape), _row_spec(tt, d, 2), _row_spec(tt, d, 3),
                  rs, _const_spec(wm.shape), _const_spec(wmt.shape), _const_spec(bias.shape), c1],
        out_specs=[pl.BlockSpec((tt, 2 * d), lambda i: (i, 1)), _const_spec(wm.shape), _const_spec((SGU_GROUPS, tt)), c1, c1],
        out_shape=[jax.ShapeDtypeStruct(dp.shape, dp.dtype), jax.ShapeDtypeStruct(wm.shape, F32),
                   jax.ShapeDtypeStruct((SGU_GROUPS, tt), F32), jax.ShapeDtypeStruct((1, d), F32),
                   jax.ShapeDtypeStruct((1, d), F32)],
        scratch_shapes=[pltpu.VMEM((tt, d), F32)],
        input_output_aliases={0: 0},
        compiler_params=_cparams(("arbitrary",)))(dp, dy_b, w_out, p, p, vn, wm, wmt, bias, ln_g)


def _gates_fwd(p, ya, yb, b_gate, *, name):
    t, d = ya.shape
    tt = min(TOKEN_TILE, t)

    def body(ga_ref, gb_ref, ya_ref, yb_ref, bg_ref, o_ref):
        sa = _sigmoid(ga_ref[...].astype(F32) + bg_ref[0:1, :])
        sb = _sigmoid(gb_ref[...].astype(F32) + bg_ref[1:2, :])
        o_ref[...] = (sa * ya_ref[...].astype(F32) + sb * yb_ref[...].astype(F32)).astype(BF16)

    rs = _row_spec(tt, d)
    return pl.pallas_call(
        body, name=name, grid=(t // tt,),
        in_specs=[_row_spec(tt, d, 4), _row_spec(tt, d, 5), rs, rs, _const_spec(b_gate.shape)],
        out_specs=rs, out_shape=jax.ShapeDtypeStruct((t, d), BF16),
        compiler_params=_cparams(("parallel",)))(p, p, ya, yb, b_gate)


def _gates_bwd(dmerged, p, ya, yb, b_gate, *, name):
    t, d = ya.shape
    tt = min(TOKEN_TILE, t)

    def body(dm_ref, ga_ref, gb_ref, ya_ref, yb_ref, bg_ref, dp_ref, dya_ref, dyb_ref, dbg_ref):
        @pl.when(pl.program_id(0) == 0)
        def _():
            dbg_ref[...] = jnp.zeros_like(dbg_ref)

        dm = dm_ref[...].astype(F32)
        sa = _sigmoid(ga_ref[...].astype(F32) + bg_ref[0:1, :])
        sb = _sigmoid(gb_ref[...].astype(F32) + bg_ref[1:2, :])
        dya_ref[...] = (dm * sa).astype(BF16)
        dyb_ref[...] = (dm * sb).astype(BF16)
        dga = dm * ya_ref[...].astype(F32) * sa * (1.0 - sa)
        dgb = dm * yb_ref[...].astype(F32) * sb * (1.0 - sb)
        dp_ref[:, 0:d] = dga.astype(BF16)
        dp_ref[:, d:2 * d] = dgb.astype(BF16)
        dbg_ref[0:1, :] += jnp.sum(dga, axis=0, keepdims=True)
        dbg_ref[1:2, :] += jnp.sum(dgb, axis=0, keepdims=True)

    rs = _row_spec(tt, d)
    return pl.pallas_call(
        body, name=name, grid=(t // tt,),
        in_specs=[rs, _row_spec(tt, d, 4), _row_spec(tt, d, 5), rs, rs, _const_spec(b_gate.shape)],
        out_specs=[pl.BlockSpec((tt, 2 * d), lambda i: (i, 2)), rs, rs, _const_spec((8, d))],
        out_shape=[jax.ShapeDtypeStruct(p.shape, BF16), jax.ShapeDtypeStruct((t, d), BF16),
                   jax.ShapeDtypeStruct((t, d), BF16), jax.ShapeDtypeStruct((8, d), F32)],
        compiler_params=_cparams(("arbitrary",)))(dmerged, p, p, ya, yb, b_gate)


def _softmax_rows(s):
    e = jnp.exp(s - jnp.max(s, axis=-1, keepdims=True))
    return e / jnp.sum(e, axis=-1, keepdims=True)


def _attn_fwd(q, kv, x1, w_xo, gain, *, bl, s, name):
    t, d = q.shape
    mlen = kv.shape[0] // bl
    hd = d // HEADS
    tq = min(ATTN_TILE, s)
    nq = s // tq
    scale = hd ** -0.5

    def body(q_ref, kv_ref, x1_ref, w_ref, g_ref, o_ref, x2_ref, h_ref):
        for h in range(HEADS):
            hs = slice(h * hd, (h + 1) * hd)
            vs = slice(d + h * hd, d + (h + 1) * hd)
            pr = _softmax_rows(_dot(q_ref[:, hs], kv_ref[:, hs], _NT) * scale)
            o_ref[:, hs] = _dot(pr.astype(BF16), kv_ref[:, vs], _NN).astype(BF16)
        x2 = x1_ref[...] + _dot(o_ref[...], w_ref[...], _NN)
        x2_ref[...] = x2
        h_ref[...] = _rms_apply(x2, g_ref[...]).astype(BF16)

    qs = pl.BlockSpec((tq, d), lambda b, j: (b * nq + j, 0))
    return pl.pallas_call(
        body, name=name, grid=(bl, nq),
        in_specs=[qs, pl.BlockSpec((mlen, 2 * d), lambda b, j: (b, 0)), qs, _const_spec(w_xo.shape), _const_spec((1, d))],
        out_specs=[qs, qs, qs],
        out_shape=[jax.ShapeDtypeStruct((t, d), BF16), jax.ShapeDtypeStruct((t, d), F32), jax.ShapeDtypeStruct((t, d), BF16)],
        compiler_params=_cparams(("parallel", "parallel")))(q, kv, x1, w_xo, gain)


def _attn_bwd(q, kv, do, *, bl, s, name):
    t, d = q.shape
    mlen = kv.shape[0] // bl
    hd = d // HEADS
    tq = min(ATTN_TILE, s)
    nq = s // tq
    scale = hd ** -0.5

    def body(q_ref, kv_ref, do_ref, dq_ref, dkv_ref):
        @pl.when(pl.program_id(1) == 0)
        def _():
            dkv_ref[...] = jnp.zeros_like(dkv_ref)

        for h in range(HEADS):
            hs = slice(h * hd, (h + 1) * hd)
            vs = slice(d + h * hd, d + (h + 1) * hd)
            qh, kh, vh, doh = q_ref[:, hs], kv_ref[:, hs], kv_ref[:, vs], do_ref[:, hs]
            pr = _softmax_rows(_dot(qh, kh, _NT) * scale)
            dpr = _dot(doh, vh, _NT)
            dkv_ref[:, vs] += _dot(pr.astype(BF16), doh, _TN)
            ds = (pr * (dpr - jnp.sum(dpr * pr, axis=-1, keepdims=True)) * scale).astype(BF16)
            dq_ref[:, hs] = _dot(ds, kh, _NN).astype(BF16)
            dkv_ref[:, hs] += _dot(ds, qh, _TN)

    qs = pl.BlockSpec((tq, d), lambda b, j: (b * nq + j, 0))
    ks = pl.BlockSpec((mlen, 2 * d), lambda b, j: (b, 0))
    return pl.pallas_call(
        body, name=name, grid=(bl, nq), in_specs=[qs, ks, qs], out_specs=[qs, ks],
        out_shape=[jax.ShapeDtypeStruct((t, d), BF16), jax.ShapeDtypeStruct(kv.shape, F32)],
        compiler_params=_cparams(("parallel", "arbitrary")))(q, kv, do)


def _swiglu_fwd(gu, *, name):
    t, f2 = gu.shape
    f = f2 // 2
    tt = min(TOKEN_TILE, t)

    def body(gu_ref, o_ref):
        gt = gu_ref[:, 0:f].astype(F32)
        up = gu_ref[:, f:f2].astype(F32)
        o_ref[...] = (gt * _sigmoid(gt) * up).astype(BF16)

    return pl.pallas_call(
        body, name=name, grid=(t // tt,), in_specs=[_row_spec(tt, f2)], out_specs=_row_spec(tt, f),
        out_shape=jax.ShapeDtypeStruct((t, f), BF16), compiler_params=_cparams(("parallel",)))(gu)


def _swiglu_bwd(gu, dact, *, name):
    t, f2 = gu.shape
    f = f2 // 2
    tt = min(TOKEN_TILE, t)

    def body(gu_ref, da_ref, o_ref):
        gt = gu_ref[:, 0:f].astype(F32)
        up = gu_ref[:, f:f2].astype(F32)
        da = da_ref[...].astype(F32)
        sg = _sigmoid(gt)
        o_ref[:, 0:f] = (da * up * sg * (1.0 + gt * (1.0 - sg))).astype(BF16)
        o_ref[:, f:f2] = (da * gt * sg).astype(BF16)

    return pl.pallas_call(
        body, name=name, grid=(t // tt,), in_specs=[_row_spec(tt, f2), _row_spec(tt, f)], out_specs=_row_spec(tt, f2),
        out_shape=jax.ShapeDtypeStruct((t, f2), BF16), compiler_params=_cparams(("parallel",)))(gu, dact)


def _mesh_pos():
    return lax.axis_index("x"), lax.axis_index("y"), lax.axis_index("c")


def _all_gather(arrs, *, name):
    n = len(arrs)
    hbm = pl.BlockSpec(memory_space=pl.ANY)

    def body(*refs):
        ins, outs = refs[:n], refs[n:2 * n]
        send_sems, recv_sems, loc_sems = refs[2 * n:]
        x, y, c = _mesh_pos()
        me, sib = (x, y, c), (x, y, 1 - c)
        chips = [(1 - x, y), (x, 1 - y), (1 - x, 1 - y)]

        def idx(dev):
            return 4 * dev[0] + 2 * dev[1] + dev[2]

        def copy(w, k, block, to, from_input=False):
            return pltpu.make_async_remote_copy(
                src_ref=ins[w] if from_input else outs[w].at[idx(block)], dst_ref=outs[w].at[idx(block)],
                send_sem=send_sems.at[w, k], recv_sem=recv_sems.at[w, k], device_id=to, device_id_type=MESH_ID)

        own = [pltpu.make_async_copy(ins[w], outs[w].at[idx(me)], loc_sems.at[w]) for w in range(n)]
        for cp in own:
            cp.start()
        first = []
        for w in range(n):
            first.append(copy(w, 0, me, sib, True))
            first += [copy(w, 1 + j, me, (*chip, c), True) for j, chip in enumerate(chips)]
        for cp in first:
            cp.start()
        passed = []
        for j, chip in enumerate(chips):
            for w in range(n):
                copy(w, 1 + j, (*chip, c), me).wait_recv()
                fwd = copy(w, 4 + j, (*chip, c), sib)
                fwd.start()
                passed.append(fwd)
        for w in range(n):
            copy(w, 0, sib, me).wait_recv()
            for j, chip in enumerate(chips):
                copy(w, 4 + j, (*chip, 1 - c), me).wait_recv()
        for cp in first + passed:
            cp.wait_send()
        for cp in own:
            cp.wait()

    return pl.pallas_call(
        body, name=name, in_specs=[hbm] * n, out_specs=[hbm] * n,
        out_shape=[jax.ShapeDtypeStruct((N_DEV, *a.shape), a.dtype) for a in arrs],
        scratch_shapes=[pltpu.SemaphoreType.DMA((n, 7)), pltpu.SemaphoreType.DMA((n, 7)), pltpu.SemaphoreType.DMA((n,))],
    )(*arrs)


_HBM = pl.BlockSpec(memory_space=pltpu.HBM)
_SEM = pl.BlockSpec(memory_space=pltpu.SEMAPHORE)
_ANY = pl.BlockSpec(memory_space=pl.ANY)
_EFFECT = pltpu.SideEffectType.DATAFLOW_SIDE_EFFECTING
N_PEERS = N_DEV - 1


def _related(pos, r):
    x, y, c = pos
    return (1 - x if r & 4 else x, 1 - y if r & 2 else y, 1 - c if r & 1 else c)


def _dev_index(dev):
    return 4 * dev[0] + 2 * dev[1] + dev[2]


def _in_hbm(a):
    return pltpu.with_memory_space_constraint(a, pltpu.HBM)


def _split_copies(kind, srcs, lands, send_sems, recv_sems):
    pos = _mesh_pos()
    me = _dev_index(pos)
    out = []
    for w in range(len(srcs)):
        for r in range(1, N_DEV):
            peer = _related(pos, r)
            if kind == "gather":
                src, dst_here, dst_there = srcs[w], lands[w].at[_dev_index(peer)], lands[w].at[me]
            else:
                src, dst_here, dst_there = srcs[w].at[_dev_index(peer)], lands[w].at[r - 1], lands[w].at[r - 1]
            out.append((src, dst_here, dst_there, send_sems.at[w * N_PEERS + r - 1], recv_sems.at[w * N_PEERS + r - 1], peer))
    return out


def _copy_start(kind, srcs, land_shapes, *, name):
    n = len(srcs)

    def body(*refs):
        src_refs, land_refs = refs[:n], refs[n:2 * n]
        send_sems, recv_sems = refs[2 * n], refs[2 * n + 1]
        token = refs[-1]
        for src, _, dst, ssem, rsem, peer in _split_copies(kind, src_refs, land_refs, send_sems, recv_sems):
            pltpu.make_async_remote_copy(src_ref=src, dst_ref=dst, send_sem=ssem, recv_sem=rsem, device_id=peer,
                                         device_id_type=MESH_ID).start()
        token[...] = jnp.zeros_like(token)

    lands = [_in_hbm(lax.empty(shape, s.dtype)) for s, shape in zip(srcs, land_shapes)]
    res = pl.pallas_call(
        body, name=name,
        out_shape=(pltpu.SemaphoreType.DMA((n * N_PEERS,)), pltpu.SemaphoreType.DMA((n * N_PEERS,)),
                   *[pltpu.HBM(s.shape, s.dtype) for s in srcs], *[pltpu.HBM(l.shape, l.dtype) for l in lands],
                   jax.ShapeDtypeStruct((8, 128), F32)),
        in_specs=[_HBM] * (2 * n), out_specs=(_SEM, _SEM, *[_HBM] * (2 * n), pl.BlockSpec(memory_space=pltpu.VMEM)),
        input_output_aliases={i: 2 + i for i in range(2 * n)},
        compiler_params=pltpu.CompilerParams(has_side_effects=_EFFECT),
    )(*[_in_hbm(s) for s in srcs], *lands)
    return res[0], res[1], list(res[2:2 + n]), list(res[2 + n:2 + 2 * n]), res[-1]


def _copy_wait(kind, send_sems, recv_sems, srcs, lands, after, *, name):
    n = len(srcs)

    def body(*refs):
        src_refs, land_refs = refs[:n], refs[n:2 * n]
        ssems, rsems = refs[2 * n], refs[2 * n + 1]
        for src, dst, _, ssem, rsem, peer in _split_copies(kind, src_refs, land_refs, ssems, rsems):
            cp = pltpu.make_async_remote_copy(src_ref=src, dst_ref=dst, send_sem=ssem, recv_sem=rsem, device_id=peer,
                                              device_id_type=MESH_ID)
            cp.wait_send()
            cp.wait_recv()

    res = pl.pallas_call(
        body, name=name,
        out_shape=(*[pltpu.HBM(s.shape, s.dtype) for s in srcs], *[pltpu.HBM(l.shape, l.dtype) for l in lands]),
        in_specs=[_HBM] * (2 * n) + [_SEM, _SEM, _ANY], out_specs=tuple([_HBM] * (2 * n)),
        input_output_aliases={i: i for i in range(2 * n)},
        compiler_params=pltpu.CompilerParams(has_side_effects=_EFFECT),
    )(*srcs, *lands, send_sems, recv_sems, after)
    return list(res[:n]), list(res[n:])


def _row_tile(rows):
    return max(tr for tr in range(16, min(rows, 512) + 1, 16) if rows % tr == 0)


def _adamw_math(w, g, m, v):
    m2 = ADAM_B1 * m + (1.0 - ADAM_B1) * g
    v2 = ADAM_B2 * v + (1.0 - ADAM_B2) * (g * g)
    m_hat = m2 / (1.0 - ADAM_B1 ** ADAM_STEP)
    v_hat = v2 / (1.0 - ADAM_B2 ** ADAM_STEP)
    delta = -ADAM_LR * (m_hat / (jnp.sqrt(v_hat) + ADAM_EPS) + ADAM_WD * w)
    return delta, m2, v2


def _adamw_shard(partials, landed, dev, w, m, v, *, name):
    r, c = w.shape
    tr = _row_tile(r)

    def body(dev_ref, p_ref, l_ref, w_ref, m_ref, v_ref, g_out, d_out, m_out, v_out):
        del dev_ref
        g = p_ref[...].astype(F32)
        for k in range(N_PEERS):
            g = g + l_ref[k].astype(F32)
        delta, m2, v2 = _adamw_math(w_ref[...], g, m_ref[...], v_ref[...])
        g_out[...] = g
        d_out[...] = delta
        m_out[...] = m2
        v_out[...] = v2

    blk = pl.BlockSpec((tr, c), lambda i, dev_ref: (i, 0))
    gs = pltpu.PrefetchScalarGridSpec(
        num_scalar_prefetch=1, grid=(r // tr,),
        in_specs=[pl.BlockSpec((None, tr, c), lambda i, dev_ref: (dev_ref[0], i, 0)),
                  pl.BlockSpec((N_PEERS, tr, c), lambda i, dev_ref: (0, i, 0)), blk, blk, blk],
        out_specs=[blk] * 4)
    return pl.pallas_call(
        body, name=name, grid_spec=gs, out_shape=[jax.ShapeDtypeStruct((r, c), F32)] * 4,
        compiler_params=_cparams(("parallel",)))(dev, partials, landed, w, m, v)


def _adamw_small(parts, dev, w, m, v, *, name, col_block):
    _, r, d = parts.shape
    cols = w.shape[1]

    def body(dev_ref, p_ref, w_ref, m_ref, v_ref, g_out, d_out, m_out, v_out):
        del dev_ref
        g = p_ref[0]
        for k in range(1, N_DEV):
            g = g + p_ref[k]
        delta, m2, v2 = _adamw_math(w_ref[...], g, m_ref[...], v_ref[...])
        g_out[...] = g
        d_out[...] = delta
        m_out[...] = m2
        v_out[...] = v2

    blk = pl.BlockSpec((r, cols), lambda i, dev_ref: (0, 0))
    pidx = (lambda i, dev_ref: (0, 0, dev_ref[0])) if col_block else (lambda i, dev_ref: (0, 0, 0))
    gs = pltpu.PrefetchScalarGridSpec(
        num_scalar_prefetch=1, grid=(1,),
        in_specs=[pl.BlockSpec((N_DEV, r, cols), pidx), blk, blk, blk], out_specs=[blk] * 4)
    return pl.pallas_call(
        body, name=name, grid_spec=gs, out_shape=[jax.ShapeDtypeStruct((r, cols), F32)] * 4,
        compiler_params=_cparams(("arbitrary",)))(dev, parts, w, m, v)


def _pad_rows(a, rows):
    return jnp.pad(a, ((0, rows - a.shape[0]), (0, 0)))


def _unblock_cols(g):
    return jnp.transpose(g, (1, 0, 2)).reshape(g.shape[1], N_DEV * g.shape[2])


def _block_cols(full):
    r, c8 = full.shape
    return jnp.transpose(full.reshape(r, N_DEV, c8 // N_DEV), (1, 0, 2))


def kernel(x, mem, norm_mix, w_in, b_gate, conv_w, conv_b, conv_ln_g, conv_ln_b, w_conv_out, sgu_ln_g, sgu_ln_b, sgu_w, sgu_b, w_sgu_out, w_mix_out, norm_xattn, norm_mem, w_q, w_kv, w_xo, norm_ffn, w_gu, w_down, norm_final, loss_target, m_norm_mix, m_w_in, m_b_gate, m_conv_w, m_conv_b, m_conv_ln_g, m_conv_ln_b, m_w_conv_out, m_sgu_ln_g, m_sgu_ln_b, m_sgu_w, m_sgu_b, m_w_sgu_out, m_w_mix_out, m_norm_xattn, m_norm_mem, m_w_q, m_w_kv, m_w_xo, m_norm_ffn, m_w_gu, m_w_down, m_norm_final, v_norm_mix, v_w_in, v_b_gate, v_conv_w, v_conv_b, v_conv_ln_g, v_conv_ln_b, v_w_conv_out, v_sgu_ln_g, v_sgu_ln_b, v_sgu_w, v_sgu_b, v_w_sgu_out, v_w_mix_out, v_norm_xattn, v_norm_mem, v_w_q, v_w_kv, v_w_xo, v_norm_ffn, v_w_gu, v_w_down, v_norm_final):
    given = dict(locals())
    bl, s, d = x.shape
    t = bl * s
    xf = x.reshape(t, d)
    tgt = loss_target.reshape(t, d)
    memf = mem.reshape(bl * mem.shape[1], d)
    cx, cy, cc = lax.axis_index("x"), lax.axis_index("y"), lax.axis_index("c")
    dev = 4 * cx + 2 * cy + cc
    dev_id = dev.astype(jnp.int32).reshape(1)
    col_sharded = ["w_in", "w_kv"]
    transposed = ["w_gu"]

    def shard_of(name, prefix=""):
        a = given[prefix + name][0]
        return jnp.transpose(a) if name in transposed else a

    def full_weight(name, blocks):
        return _unblock_cols(blocks) if name in col_sharded else blocks.reshape(N_DEV * blocks.shape[1], blocks.shape[2])

    g_in, g_bg, g_cw = _all_gather([w_in[0].astype(BF16), _pad_rows(b_gate[0], 8), _pad_rows(conv_w[0], CONV_HALO)],
                                   name="gather_w_in")
    early = ["w_conv_out", "w_sgu_out", "w_mix_out", "w_q", "w_kv", "w_xo"]
    late = ["w_gu", "w_down"]
    shards = {n: shard_of(n).astype(BF16) for n in early + late}
    started = {}
    for grp, names in (("early", early), ("late", late)):
        srcs = [shards[n] for n in names]
        started[grp] = _copy_start("gather", srcs, [(N_DEV, *a.shape) for a in srcs], name=f"gather_{grp}_start")
    token = started["early"][4][0:1, 0:1] + started["late"][4][0:1, 0:1]
    wfull = {"w_in": _unblock_cols(g_in)}
    bg_full = _unblock_cols(g_bg)
    cw_full = _unblock_cols(g_cw)

    def finish_gather(grp, names, after):
        ssem, rsem, srcs, lands, _ = started[grp]
        _, lands = _copy_wait("gather", ssem, rsem, srcs, lands, after, name=f"gather_{grp}_wait")
        for n, land in zip(names, lands):
            wfull[n] = full_weight(n, lax.dynamic_update_index_in_dim(land, shards[n], dev, 0))

    tri = jnp.tril(jnp.ones((SGU_CHUNK, SGU_CHUNK), bool))
    wm32 = jnp.where(tri[None], sgu_w[0], 0.0)
    wm = wm32.astype(BF16)
    wmt = jnp.transpose(wm32, (0, 2, 1)).astype(BF16)
    sgu_bias = jnp.broadcast_to(sgu_b[0][:, :, None], (SGU_GROUPS, SGU_CHUNK, d // SGU_GROUPS))

    h1, p = _in_proj(xf, norm_mix + token, wfull["w_in"], name="in_proj")
    c_conv, a_act = _conv_fwd(p, cw_full, conv_b, conv_ln_g, conv_ln_b, bl=bl, s=s, name="conv_fwd")
    sg, vn = _sgu_fwd(p, wm, sgu_bias, sgu_ln_g, sgu_ln_b, name="sgu_fwd")
    finish_gather("early", early, a_act[0:16, 0:128] + sg[0:16, 0:128])
    y_a = _matmul(a_act, wfull["w_conv_out"], mode="nn", out_dtype=BF16, name="mm_conv_out", tm=1024, tn=1024, tk=1024)
    y_b = _matmul(sg, wfull["w_sgu_out"], mode="nn", out_dtype=BF16, name="mm_sgu_out", tm=1024, tn=1024, tk=1024)
    merged, x1, h2, q = _mix_out(p, y_a, y_b, bg_full, xf, wfull["w_mix_out"], norm_xattn, wfull["w_q"], name="mix_out")
    mem_n = _rms_fwd(memf, norm_mem, name="rms_mem")
    kv = _matmul(mem_n, wfull["w_kv"], mode="nn", out_dtype=BF16, name="mm_kv", tm=1024, tn=1024, tk=1024)
    o, x2, h3 = _attn_fwd(q, kv, x1, wfull["w_xo"], norm_ffn, bl=bl, s=s, name="attn_fwd")
    finish_gather("late", late, h3)
    gu, act, dx3, loss_part, d_norm_final = _ffn_fwd(h3, x2, tgt, wfull["w_gu"], wfull["w_down"],
                                                     norm_final.reshape(1, d), name="ffn_fwd")
    loss = lax.psum(loss_part[0, 0], ("x", "y", "c"))

    grads = {}
    sent = []

    def send_grads(names, tag):
        blocks = []
        for n in names:
            g = grads[n]
            if g.ndim == 2:
                g = _block_cols(g) if n in col_sharded else g.reshape(N_DEV, -1, g.shape[1])
            blocks.append(g)
        ssem, rsem, srcs, lands, tok = _copy_start("scatter", blocks, [(N_PEERS, *g.shape[1:]) for g in blocks],
                                                   name=f"grads_{tag}_start")
        sent.append((names, ssem, rsem, srcs, lands))
        return tok[0:1, 0:1]

    dgu, dx2, do, d_norm_ffn = _ffn_bwd(dx3, gu, x2, wfull["w_down"], wfull["w_gu"], norm_ffn, wfull["w_xo"], name="ffn_bwd")
    grads["w_down"] = _matmul(act, dx3, mode="tn", out_dtype=BF16, name="mm_dw_down", tm=1408, tn=1024, tk=1024)
    grads["w_gu"] = _matmul(dgu, h3, mode="tn", out_dtype=BF16, name="mm_dw_gu", tm=1408, tn=1024, tk=1024)
    tok = send_grads(["w_down", "w_gu"], "ffn")
    grads["w_xo"] = _matmul(o, dx2, mode="tn", out_dtype=BF16, name="mm_dw_xo", tm=1024, tn=1024, tk=1024)
    dq, dkv = _attn_bwd(q, kv, do, bl=bl, s=s, name="attn_bwd")
    grads["w_kv"] = _matmul(mem_n, dkv, mode="tn", out_dtype=BF16, name="mm_dw_kv", tm=1024, tn=256, tk=1024,
                            col_blocks=N_DEV)
    tok2 = send_grads(["w_xo", "w_kv"], "attn")
    dmem_n = _matmul(dkv, wfull["w_kv"], mode="nt", out_dtype=F32, name="mm_d_mem", tm=512, tn=1024, tk=2048)
    d_norm_mem = _rms_bwd(None, dmem_n, memf, norm_mem, name="rms_mem_bwd", need_dx=False)
    dx1, d_norm_xattn, dw_q = _proj_rms_bwd(dq, dx2, x1, wfull["w_q"], norm_xattn + (tok + tok2), name="q_rms_bwd", h=h2)
    dp, dy_a, dy_b, d_b_gate, dw_mix, dw_in_gates = _gates_bwd_fused(dx1, p, y_a, y_b, bg_full, wfull["w_mix_out"],
                                                                    merged, h1, name="gates_bwd")
    grads["w_q"] = dw_q.astype(BF16)
    grads["w_mix_out"] = dw_mix.astype(BF16)
    grads["w_sgu_out"] = _matmul(sg, dy_b, mode="tn", out_dtype=BF16, name="mm_dw_sgu", tm=1024, tn=1024, tk=1024)
    dc, d_conv_ln_g, d_conv_ln_b, dw_conv = _conv_ln_bwd_fused(dy_a, c_conv, a_act, wfull["w_conv_out"], conv_ln_g,
                                                               conv_ln_b, name="conv_ln_bwd")
    grads["w_conv_out"] = dw_conv.astype(BF16)
    tok = send_grads(["w_q", "w_mix_out", "w_sgu_out", "w_conv_out"], "mixer")
    dp, d_sgu_w, d_sgu_b, d_sgu_ln_g, d_sgu_ln_b = _sgu_bwd(dp, dy_b, wfull["w_sgu_out"], p, vn, wm, wmt, sgu_bias,
                                                             sgu_ln_g + tok, name="sgu_bwd")
    dw_in_sgu = _matmul(h1, dp, mode="tn", out_dtype=BF16, name="mm_dw_in_sgu", tm=1024, tn=1024, tk=2048,
                        b_cols=(2 * d, 2 * d))
    dp, d_conv_w, d_conv_b, dw_in_conv = _conv_bwd(dp, dc, p, cw_full, h1, bl=bl, s=s, name="conv_bwd")
    grads["w_in"] = _block_cols(jnp.concatenate([dw_in_conv.astype(BF16), dw_in_sgu, dw_in_gates.astype(BF16)], axis=1))
    tok = send_grads(["w_in"], "in")
    grad_x, d_norm_mix = _proj_rms_bwd(dp, dx1, xf, wfull["w_in"], norm_mix + tok, name="in_proj_bwd")
    out = {}

    rep_names = ["norm_mix", "conv_b", "conv_ln_g", "conv_ln_b", "sgu_ln_g", "sgu_ln_b", "norm_xattn", "norm_mem",
                 "norm_ffn", "norm_final", "sgu_b"]
    rep_grads = [d_norm_mix, d_conv_b, d_conv_ln_g, d_conv_ln_b, d_sgu_ln_g, d_sgu_ln_b, d_norm_xattn, d_norm_mem,
                 d_norm_ffn, d_norm_final, d_sgu_b.reshape(1, d)]
    nrep = len(rep_names)
    pad = jnp.zeros((16 - nrep, d), F32)
    sgw_rows = SGU_GROUPS * SGU_CHUNK * SGU_CHUNK // d

    def pack_rep(vecs, sgw):
        return jnp.concatenate([v.reshape(1, d) for v in vecs] + [pad, sgw.reshape(sgw_rows, d)], axis=0)

    def pack_col(bg, cw):
        return jnp.concatenate([_pad_rows(bg, 8), _pad_rows(cw, CONV_HALO)], axis=0)

    small_a = pack_rep(rep_grads, d_sgu_w)
    small_b = jnp.concatenate([d_b_gate, d_conv_w], axis=0)
    parts_a, parts_b = _all_gather([small_a, small_b], name="gather_small_grads")
    res_a = _adamw_small(parts_a, dev_id, pack_rep([given[n] for n in rep_names], sgu_w),
                         pack_rep([given["m_" + n] for n in rep_names], m_sgu_w),
                         pack_rep([given["v_" + n] for n in rep_names], v_sgu_w), name="adamw_small", col_block=False)
    res_b = _adamw_small(parts_b, dev_id, pack_col(b_gate[0], conv_w[0]), pack_col(m_b_gate[0], m_conv_w[0]),
                         pack_col(v_b_gate[0], v_conv_w[0]), name="adamw_small_cols", col_block=True)
    for i, n in enumerate(rep_names):
        out[n] = [r[i].reshape(given[n].shape) for r in res_a]
    out["sgu_w"] = [r[16:16 + sgw_rows].reshape(sgu_w.shape) for r in res_a]
    out["b_gate"] = [r[0:2][None] for r in res_b]
    out["conv_w"] = [r[8:8 + CONV_WIDTH][None] for r in res_b]

    for names, ssem, rsem, srcs, lands in sent:
        srcs, lands = _copy_wait("scatter", ssem, rsem, srcs, lands, res_a[0], name=f"grads_{names[0]}_wait")
        for n, partials, landed in zip(names, srcs, lands):
            res = _adamw_shard(partials, landed, dev_id, shard_of(n), shard_of(n, "m_"), shard_of(n, "v_"),
                               name=f"adamw_{n}")
            out[n] = [(jnp.transpose(r) if n in transposed else r)[None] for r in res]

    order = ["norm_mix", "w_in", "b_gate", "conv_w", "conv_b", "conv_ln_g", "conv_ln_b", "w_conv_out", "sgu_ln_g",
             "sgu_ln_b", "sgu_w", "sgu_b", "w_sgu_out", "w_mix_out", "norm_xattn", "norm_mem", "w_q", "w_kv", "w_xo",
             "norm_ffn", "w_gu", "w_down", "norm_final"]
    return (loss, grad_x.reshape(x.shape), *[out[n][0] for n in order], *[out[n][1] for n in order],
            *[out[n][2] for n in order], *[out[n][3] for n in order])
```

```python
import functools

import jax
import jax.numpy as jnp
from jax import lax
from jax.experimental import pallas as pl
from jax.experimental.pallas import tpu as pltpu

F32 = jnp.float32
BF16 = jnp.bfloat16
RMS_EPS = 1e-6
LN_EPS = 1e-5
CONV_WIDTH = 31
CONV_HALO = 32
CONV_ROWS = 64
CONV_COLS = 256
LANES = 128
SGU_CHUNK = 128
SGU_GROUPS = 8
HEADS = 4
N_DEV = 8
ADAM_LR, ADAM_B1, ADAM_B2, ADAM_EPS, ADAM_WD, ADAM_STEP = 0.001, 0.9, 0.999, 1e-08, 0.01, 10
VMEM_LIMIT = 56 * 1024 * 1024
TOKEN_TILE = 256
ATTN_TILE = 1024
MESH_ID = pl.DeviceIdType.MESH

_GELU_K = 0.7978845608028654
_GELU_C = 0.044715


def _cparams(sem=None):
    return pltpu.CompilerParams(dimension_semantics=sem, vmem_limit_bytes=VMEM_LIMIT)


def _sigmoid(v):
    return 0.5 * jnp.tanh(0.5 * v) + 0.5


def _gelu(v):
    return 0.5 * v * (1.0 + jnp.tanh(_GELU_K * (v + _GELU_C * v * v * v)))


def _gelu_grad(v):
    th = jnp.tanh(_GELU_K * (v + _GELU_C * v * v * v))
    return 0.5 * (1.0 + th) + 0.5 * v * (1.0 - th * th) * _GELU_K * (1.0 + 3.0 * _GELU_C * v * v)


def _dot(a, b, dims):
    return lax.dot_general(a, b, (dims, ((), ())), preferred_element_type=F32)


_NN = ((1,), (0,))
_NT = ((1,), (1,))
_TN = ((0,), (0,))


def _matmul(a, b, *, mode, out_dtype, name, tm=512, tn=512, tk=512, chunk=None, residual=None, rms_gain=None,
            col_blocks=None, b_cols=None):
    if mode == "nn":
        (m, k), (_, n) = a.shape, b.shape
    elif mode == "nt":
        (m, k), (n, _) = a.shape, b.shape
    else:
        (k, m), (_, n) = a.shape, b.shape
    b_first = 0
    if b_cols is not None:
        assert mode == "tn"
        b_first, n = b_cols
    tm, tn, tk = min(tm, m), min(tn, n), min(tk, k)
    assert b_first % tn == 0
    b_first //= tn
    assert m % tm == 0 and n % tn == 0 and k % tk == 0, (name, a.shape, b.shape, tm, tn, tk)
    nk = k // tk
    dims = {"nn": _NN, "nt": _NT, "tn": _TN}[mode]
    chunk = tn if chunk is None else min(chunk, tn)
    assert tn % chunk == 0
    if rms_gain is not None:
        assert tn == n and chunk == n

    def body(*refs):
        refs = list(refs)
        a_ref, b_ref = refs[:2]
        pos = 2
        r_ref = g_ref = None
        if residual is not None:
            r_ref = refs[pos]
            pos += 1
        if rms_gain is not None:
            g_ref = refs[pos]
            pos += 1
        o_ref = refs[pos]
        pos += 1
        h_ref = None
        if rms_gain is not None:
            h_ref = refs[pos]
            pos += 1
        acc_ref = refs[pos] if nk > 1 else None
        av = a_ref[...].astype(BF16)
        for c0 in range(0, tn, chunk):
            cs = slice(c0, c0 + chunk)
            bv = (b_ref[cs, :] if mode == "nt" else b_ref[:, cs]).astype(BF16)
            part = _dot(av, bv, dims)

            def finish(res, cs=cs):
                if r_ref is not None:
                    res = res + r_ref[:, cs].astype(F32)
                o_ref[:, cs] = res.astype(out_dtype)
                if h_ref is not None:
                    r = lax.rsqrt(jnp.mean(res * res, axis=-1, keepdims=True) + RMS_EPS)
                    h_ref[...] = (res * r * g_ref[...]).astype(BF16)

            if nk == 1:
                finish(part)
            else:
                kk = pl.program_id(2)

                @pl.when(kk == 0)
                def _(part=part, cs=cs):
                    acc_ref[:, cs] = part

                @pl.when(kk > 0)
                def _(part=part, cs=cs):
                    acc_ref[:, cs] += part

                @pl.when(kk == nk - 1)
                def _(finish=finish, cs=cs):
                    finish(acc_ref[:, cs])

    resident = dict(pipeline_mode=pl.Buffered(1)) if (n == tn and nk == 1 and mode != "tn" and m > tm) else {}
    if mode == "nn":
        a_spec = pl.BlockSpec((tm, tk), lambda i, j, kk: (i, kk))
        b_spec = pl.BlockSpec((tk, tn), lambda i, j, kk: (kk, j), **resident)
    elif mode == "nt":
        a_spec = pl.BlockSpec((tm, tk), lambda i, j, kk: (i, kk))
        b_spec = pl.BlockSpec((tn, tk), lambda i, j, kk: (j, kk), **resident)
    else:
        a_spec = pl.BlockSpec((tk, tm), lambda i, j, kk: (kk, i))
        b_spec = pl.BlockSpec((tk, tn), lambda i, j, kk: (kk, j + b_first))
    o_spec = pl.BlockSpec((tm, tn), lambda i, j, kk: (i, j))
    in_specs, args = [a_spec, b_spec], [a, b]
    if residual is not None:
        in_specs.append(o_spec)
        args.append(residual)
    out_shape, out_specs = [jax.ShapeDtypeStruct((m, n), out_dtype)], [o_spec]
    if col_blocks is not None:
        assert residual is None and rms_gain is None and (n // col_blocks) % tn == 0
        per = n // col_blocks // tn
        out_shape = [jax.ShapeDtypeStruct((col_blocks, m, n // col_blocks), out_dtype)]
        out_specs = [pl.BlockSpec((None, tm, tn), lambda i, j, kk: (j // per, i, j % per))]
    if rms_gain is not None:
        in_specs.append(pl.BlockSpec((1, n), lambda i, j, kk: (0, 0)))
        args.append(rms_gain)
        out_shape.append(jax.ShapeDtypeStruct((m, n), BF16))
        out_specs.append(o_spec)
    res = pl.pallas_call(
        body, name=name, grid=(m // tm, n // tn, nk), in_specs=in_specs, out_specs=out_specs, out_shape=out_shape,
        scratch_shapes=[pltpu.VMEM((tm, tn), F32)] if nk > 1 else [],
        compiler_params=_cparams(("parallel", "parallel", "arbitrary")),
    )(*args)
    return res if rms_gain is not None else res[0]


def _row_call(name, t, tm, rows_in, residents, rows_out, accs, body):
    n_in, n_res, n_out, n_acc = len(rows_in), len(residents), len(rows_out), len(accs)
    steps = t // tm
    assert t % tm == 0
    narrow = [i for i, (_, dt) in enumerate(accs) if dt != F32]

    def kernel_body(*refs):
        in_refs, res_refs = refs[:n_in], refs[n_in:n_in + n_res]
        out_refs = refs[n_in + n_res:n_in + n_res + n_out]
        acc_out = list(refs[n_in + n_res + n_out:n_in + n_res + n_out + n_acc])
        scratch = refs[n_in + n_res + n_out + n_acc:]
        acc_refs = list(acc_out)
        for s_ref, i in zip(scratch, narrow):
            acc_refs[i] = s_ref
        if accs:
            @pl.when(pl.program_id(0) == 0)
            def _():
                for acc in acc_refs:
                    acc[...] = jnp.zeros_like(acc)
        body(in_refs, res_refs, out_refs, acc_refs)
        if narrow:
            @pl.when(pl.program_id(0) == steps - 1)
            def _():
                for i in narrow:
                    acc_out[i][...] = acc_refs[i][...].astype(acc_out[i].dtype)

    once = dict(pipeline_mode=pl.Buffered(1)) if steps > 1 else {}
    in_specs = [pl.BlockSpec((tm, cols), lambda i, cb=cb: (i, cb)) for _, cols, cb in rows_in]
    in_specs += [pl.BlockSpec(r.shape, lambda i, nd=r.ndim: (0,) * nd, **once) for r in residents]
    out_specs = [pl.BlockSpec((tm, cols), lambda i, cb=cb: (i, cb)) for _, cols, cb, _ in rows_out]
    out_specs += [pl.BlockSpec(shape, lambda i, nd=len(shape): (0,) * nd) for shape, _ in accs]
    out_shape = [jax.ShapeDtypeStruct((t, total), dt) for total, _, _, dt in rows_out]
    out_shape += [jax.ShapeDtypeStruct(shape, dt) for shape, dt in accs]
    return pl.pallas_call(
        kernel_body, name=name, grid=(steps,), in_specs=in_specs, out_specs=out_specs, out_shape=out_shape,
        scratch_shapes=[pltpu.VMEM(accs[i][0], F32) for i in narrow],
        compiler_params=_cparams(("arbitrary",) if accs else ("parallel",)),
    )(*[a for a, _, _ in rows_in], *residents)


def _rms_apply(xv, gain):
    return xv * lax.rsqrt(jnp.mean(xv * xv, axis=-1, keepdims=True) + RMS_EPS) * gain


def _rms_grad(dres, dh, xv, gain):
    r = lax.rsqrt(jnp.mean(xv * xv, axis=-1, keepdims=True) + RMS_EPS)
    xhat = xv * r
    dxh = dh * gain
    dx = dres + r * (dxh - xhat * jnp.mean(dxh * xhat, axis=-1, keepdims=True))
    return dx, jnp.sum(dh * xhat, axis=0, keepdims=True)


def _in_proj(xf, gain, w_in, *, name):
    t, d = xf.shape
    n = w_in.shape[1]
    chunk = n // 4

    def body(ins, res, outs, accs):
        (x_ref,), (g_ref, w_ref), (h_ref, p_ref) = ins, res, outs
        h = _rms_apply(x_ref[...], g_ref[...]).astype(BF16)
        h_ref[...] = h
        for c0 in range(0, n, chunk):
            p_ref[:, c0:c0 + chunk] = _dot(h, w_ref[:, c0:c0 + chunk], _NN).astype(BF16)

    return _row_call(name, t, min(512, t), [(xf, d, 0)], [gain, w_in], [(d, d, 0, BF16), (n, n, 0, BF16)], [], body)


def _in_proj_gather(xf, gain, w_shard, *, name):
    t, d = xf.shape
    cb = w_shard.shape[1]
    tm = min(512, t)
    steps = t // tm
    mx, my, _ = _mesh_pos()
    order = jnp.stack([2 * mx + my, 2 * (1 - mx) + my, 2 * mx + (1 - my), 2 * (1 - mx) + (1 - my)]).astype(jnp.int32)

    def body(order_ref, x_ref, g_ref, ws_ref, h_ref, p_ref, wout_ref, w_ref, send_sems, recv_sems, own_sem):
        ps, i = pl.program_id(0), pl.program_id(1)
        x, y, c = _mesh_pos()
        me, sib = (x, y, c), (x, y, 1 - c)
        chips = [(1 - x, y), (x, 1 - y), (1 - x, 1 - y)]

        def copy(k, block, to, from_shard=False):
            return pltpu.make_async_remote_copy(
                src_ref=ws_ref if from_shard else w_ref.at[_dev_index(block)], dst_ref=w_ref.at[_dev_index(block)],
                send_sem=send_sems.at[k], recv_sem=recv_sems.at[k], device_id=to, device_id_type=MESH_ID)

        own = pltpu.make_async_copy(ws_ref, w_ref.at[_dev_index(me)], own_sem)
        first = [copy(0, me, sib, True)] + [copy(1 + j, me, (*chip, c), True) for j, chip in enumerate(chips)]
        passed = [copy(4 + j, (*chip, c), sib) for j, chip in enumerate(chips)]

        @pl.when(jnp.logical_and(ps == 0, i == 0))
        def _():
            own.start()
            for cp in first:
                cp.start()
            own.wait()
            copy(0, sib, me).wait_recv()

        for j, chip in enumerate(chips):
            @pl.when(jnp.logical_and(ps == j + 1, i == 0))
            def _(j=j, chip=chip):
                copy(1 + j, (*chip, c), me).wait_recv()
                passed[j].start()
                copy(4 + j, (*chip, 1 - c), me).wait_recv()

        h = _rms_apply(x_ref[...], g_ref[...]).astype(BF16)
        h_ref[...] = h
        chip_id = order_ref[ps]
        p_ref[:, 0:cb] = _dot(h, w_ref[2 * chip_id], _NN).astype(BF16)
        p_ref[:, cb:2 * cb] = _dot(h, w_ref[2 * chip_id + 1], _NN).astype(BF16)

        @pl.when(jnp.logical_and(ps == 3, i == steps - 1))
        def _():
            for cp in first + passed:
                cp.wait_send()
            keep = pltpu.make_async_copy(w_ref, wout_ref, own_sem)
            keep.start()
            keep.wait()

    gs = pltpu.PrefetchScalarGridSpec(
        num_scalar_prefetch=1, grid=(4, steps),
        in_specs=[pl.BlockSpec((tm, d), lambda ps, i, o: (i, 0)), pl.BlockSpec((1, d), lambda ps, i, o: (0, 0)),
                  pl.BlockSpec(memory_space=pl.ANY)],
        out_specs=[pl.BlockSpec((tm, d), lambda ps, i, o: (jnp.where(ps == 0, i, steps - 1), 0)),
                   pl.BlockSpec((tm, 2 * cb), lambda ps, i, o: (i, o[ps])), pl.BlockSpec(memory_space=pl.ANY)],
        scratch_shapes=[pltpu.VMEM((N_DEV, d, cb), BF16), pltpu.SemaphoreType.DMA((7,)), pltpu.SemaphoreType.DMA((7,)),
                        pltpu.SemaphoreType.DMA(())])
    return pl.pallas_call(
        body, name=name, grid_spec=gs,
        out_shape=[jax.ShapeDtypeStruct((t, d), BF16), jax.ShapeDtypeStruct((t, N_DEV * cb), BF16),
                   jax.ShapeDtypeStruct((N_DEV, d, cb), BF16)],
        compiler_params=_cparams(("arbitrary", "arbitrary")))(order, xf, gain, w_shard)


def _mix_out(p, y_a, y_b, b_gate, xf, w_mix, gain, w_q, *, name):
    t, d = xf.shape

    def body(ins, res, outs, accs):
        ga_ref, gb_ref, ya_ref, yb_ref, x_ref = ins
        bg_ref, wm_ref, g_ref, wq_ref = res
        m_ref, x1_ref, h_ref, q_ref = outs
        sa = _sigmoid(ga_ref[...].astype(F32) + bg_ref[0:1, :])
        sb = _sigmoid(gb_ref[...].astype(F32) + bg_ref[1:2, :])
        merged = (sa * ya_ref[...].astype(F32) + sb * yb_ref[...].astype(F32)).astype(BF16)
        m_ref[...] = merged
        x1 = x_ref[...] + _dot(merged, wm_ref[...], _NN)
        x1_ref[...] = x1
        h = _rms_apply(x1, g_ref[...]).astype(BF16)
        h_ref[...] = h
        q_ref[...] = _dot(h, wq_ref[...], _NN).astype(BF16)

    return _row_call(name, t, min(512, t), [(p, d, 4), (p, d, 5), (y_a, d, 0), (y_b, d, 0), (xf, d, 0)],
                     [b_gate, w_mix, gain, w_q], [(d, d, 0, BF16), (d, d, 0, F32), (d, d, 0, BF16), (d, d, 0, BF16)], [], body)


def _ffn_fwd(h3, x2, target, w_gu_t, w_down, gain, *, name):
    t, d = x2.shape
    f2 = w_gu_t.shape[0]
    f = f2 // 2
    half = f // 2

    def body(ins, res, outs, accs):
        h_ref, x2_ref, t_ref = ins
        wgu_ref, wd_ref, g_ref = res
        gu_ref, act_ref, dx_ref = outs
        loss_ref, dg_ref = accs
        h = h_ref[...]
        x3 = x2_ref[...]
        for c0 in (0, half):
            gt = _dot(h, wgu_ref[c0:c0 + half, :], _NT).astype(BF16)
            up = _dot(h, wgu_ref[f + c0:f + c0 + half, :], _NT).astype(BF16)
            gu_ref[:, c0:c0 + half] = gt
            gu_ref[:, f + c0:f + c0 + half] = up
            gtf = gt.astype(F32)
            act = (gtf * _sigmoid(gtf) * up.astype(F32)).astype(BF16)
            act_ref[:, c0:c0 + half] = act
            x3 = x3 + _dot(act, wd_ref[c0:c0 + half, :], _NN)
        g = g_ref[...]
        r = lax.rsqrt(jnp.mean(x3 * x3, axis=-1, keepdims=True) + RMS_EPS)
        xhat = x3 * r
        err = xhat * g - t_ref[...]
        loss_ref[...] += 0.5 * jnp.sum(jnp.mean(err * err, axis=-1, keepdims=True), axis=0, keepdims=True)
        dy = err * (1.0 / d)
        dg_ref[...] += jnp.sum(dy * xhat, axis=0, keepdims=True)
        dxh = dy * g
        dx_ref[...] = r * (dxh - xhat * jnp.mean(dxh * xhat, axis=-1, keepdims=True))

    return _row_call(name, t, min(256, t), [(h3, d, 0), (x2, d, 0), (target, d, 0)], [w_gu_t, w_down, gain],
                     [(f2, f2, 0, BF16), (f, f, 0, BF16), (d, d, 0, F32)], [((1, 1), F32), ((1, d), F32)], body)


def _ffn_bwd(dx3, gu, x2, w_down, w_gu_t, gain, w_xo, *, name):
    t, d = x2.shape
    f2 = w_gu_t.shape[0]
    f = f2 // 2
    half = f // 2

    def body(ins, res, outs, accs):
        dx3_ref, gu_ref, x2_ref = ins
        wd_ref, wgu_ref, g_ref, wxo_ref = res
        dgu_ref, dx2_ref, do_ref = outs
        (dg_ref,) = accs
        dx3v = dx3_ref[...]
        dxb = dx3v.astype(BF16)
        dh = jnp.zeros(dx3v.shape, F32)
        for c0 in (0, half):
            dact = _dot(dxb, wd_ref[c0:c0 + half, :], _NT)
            gt = gu_ref[:, c0:c0 + half].astype(F32)
            up = gu_ref[:, f + c0:f + c0 + half].astype(F32)
            sg = _sigmoid(gt)
            dgt = (dact * up * sg * (1.0 + gt * (1.0 - sg))).astype(BF16)
            dup = (dact * gt * sg).astype(BF16)
            dgu_ref[:, c0:c0 + half] = dgt
            dgu_ref[:, f + c0:f + c0 + half] = dup
            dh = dh + _dot(dgt, wgu_ref[c0:c0 + half, :], _NN) + _dot(dup, wgu_ref[f + c0:f + c0 + half, :], _NN)
        dx2, dg = _rms_grad(dx3v, dh, x2_ref[...], g_ref[...])
        dx2_ref[...] = dx2
        dg_ref[...] += dg
        do_ref[...] = _dot(dx2.astype(BF16), wxo_ref[...], _NT).astype(BF16)

    return _row_call(name, t, min(256, t), [(dx3, d, 0), (gu, f2, 0), (x2, d, 0)], [w_down, w_gu_t, gain, w_xo],
                     [(f2, f2, 0, BF16), (d, d, 0, F32), (d, d, 0, BF16)], [((1, d), F32)], body)


def _proj_rms_bwd(dy, dres, x, w, gain, *, name, h=None):
    t, d = x.shape
    k = dy.shape[1]

    def body(ins, res, outs, accs):
        dy_ref, dres_ref, x_ref = ins[:3]
        w_ref, g_ref = res
        if h is not None:
            accs[1][...] += _dot(ins[3][...], dy_ref[...], _TN)
        if w.ndim == 3:
            cb = w.shape[2]
            dh = _dot(dy_ref[:, 0:cb], w_ref[0], _NT)
            for j in range(1, w.shape[0]):
                dh = dh + _dot(dy_ref[:, j * cb:(j + 1) * cb], w_ref[j], _NT)
        else:
            dh = _dot(dy_ref[...], w_ref[...], _NT)
        dx, dg = _rms_grad(dres_ref[...], dh, x_ref[...], g_ref[...])
        outs[0][...] = dx
        accs[0][...] += dg

    rows_in = [(dy, k, 0), (dres, d, 0), (x, d, 0)] + ([(h, d, 0)] if h is not None else [])
    accs = [((1, d), F32)] + ([((d, k), BF16)] if h is not None else [])
    return _row_call(name, t, min(512, t), rows_in, [w, gain], [(d, d, 0, F32)], accs, body)


def _gates_bwd_fused(dx1, p, y_a, y_b, b_gate, w_mix, merged, h1, *, name):
    t, d = y_a.shape

    def body(ins, res, outs, accs):
        dx_ref, ga_ref, gb_ref, ya_ref, yb_ref, m_ref, h1_ref = ins
        bg_ref, wm_ref = res
        dp_ref, dya_ref, dyb_ref = outs
        dbg_ref, dwm_ref, dwin_ref = accs
        dxb = dx_ref[...].astype(BF16)
        dwm_ref[...] += _dot(m_ref[...], dxb, _TN)
        dm = _dot(dxb, wm_ref[...], _NT)
        sa = _sigmoid(ga_ref[...].astype(F32) + bg_ref[0:1, :])
        sb = _sigmoid(gb_ref[...].astype(F32) + bg_ref[1:2, :])
        dya_ref[...] = (dm * sa).astype(BF16)
        dyb_ref[...] = (dm * sb).astype(BF16)
        dga = dm * ya_ref[...].astype(F32) * sa * (1.0 - sa)
        dgb = dm * yb_ref[...].astype(F32) * sb * (1.0 - sb)
        dp_ref[:, 0:d] = dga.astype(BF16)
        dp_ref[:, d:2 * d] = dgb.astype(BF16)
        dbg_ref[0:1, :] += jnp.sum(dga, axis=0, keepdims=True)
        dbg_ref[1:2, :] += jnp.sum(dgb, axis=0, keepdims=True)
        dwin_ref[...] += _dot(h1_ref[...], dp_ref[...], _TN)

    return _row_call(name, t, min(256, t),
                     [(dx1, d, 0), (p, d, 4), (p, d, 5), (y_a, d, 0), (y_b, d, 0), (merged, d, 0), (h1, d, 0)],
                     [b_gate, w_mix], [(p.shape[1], 2 * d, 2, BF16), (d, d, 0, BF16), (d, d, 0, BF16)],
                     [((8, d), F32), ((d, d), BF16), ((d, 2 * d), BF16)], body)


def _conv_ln_bwd_fused(dy_a, c, a_act, w_conv_out, ln_g, ln_b, *, name):
    t, d = c.shape

    def body(ins, res, outs, accs):
        dy_ref, c_ref, act_ref = ins
        w_ref, lg_ref, lb_ref = res
        dlg_ref, dlb_ref, dw_ref = accs
        dw_ref[...] += _dot(act_ref[...], dy_ref[...], _TN)
        dact = _dot(dy_ref[...], w_ref[...], _NT)
        cv = c_ref[...].astype(F32)
        g = lg_ref[...]
        mu = jnp.mean(cv, axis=-1, keepdims=True)
        dv = cv - mu
        rstd = lax.rsqrt(jnp.mean(dv * dv, axis=-1, keepdims=True) + LN_EPS)
        chat = dv * rstd
        aln = chat * g + lb_ref[...]
        sg = _sigmoid(aln)
        daln = dact * (sg * (1.0 + aln * (1.0 - sg)))
        dlb_ref[...] += jnp.sum(daln, axis=0, keepdims=True)
        dlg_ref[...] += jnp.sum(daln * chat, axis=0, keepdims=True)
        dchat = daln * g
        dc = rstd * (dchat - jnp.mean(dchat, axis=-1, keepdims=True)
                     - chat * jnp.mean(dchat * chat, axis=-1, keepdims=True))
        outs[0][...] = dc.astype(BF16)

    return _row_call(name, t, min(512, t), [(dy_a, d, 0), (c, d, 0), (a_act, d, 0)], [w_conv_out, ln_g, ln_b],
                     [(d, d, 0, BF16)], [((1, d), F32), ((1, d), F32), ((d, d), BF16)], body)


def _row_spec(tt, cols, col_block=0):
    return pl.BlockSpec((tt, cols), lambda i: (i, col_block))


def _const_spec(shape):
    return pl.BlockSpec(shape, lambda *_: (0,) * len(shape))


def _rms_fwd(x, gain, *, name):
    t, d = x.shape
    tt = min(TOKEN_TILE, t)

    def body(x_ref, g_ref, h_ref):
        xv = x_ref[...]
        r = lax.rsqrt(jnp.mean(xv * xv, axis=-1, keepdims=True) + RMS_EPS)
        h_ref[...] = (xv * r * g_ref[...]).astype(BF16)

    return pl.pallas_call(
        body, name=name, grid=(t // tt,), in_specs=[_row_spec(tt, d), _const_spec((1, d))],
        out_specs=_row_spec(tt, d), out_shape=jax.ShapeDtypeStruct((t, d), BF16),
        compiler_params=_cparams(("parallel",)))(x, gain)


def _rms_bwd(dres, dh, x, gain, *, name, need_dx=True):
    t, d = x.shape
    tt = min(TOKEN_TILE, t)

    def body(*refs):
        if need_dx:
            dres_ref, dh_ref, x_ref, g_ref, dx_ref, dg_ref = refs
        else:
            dh_ref, x_ref, g_ref, dg_ref = refs

        @pl.when(pl.program_id(0) == 0)
        def _():
            dg_ref[...] = jnp.zeros_like(dg_ref)

        xv = x_ref[...]
        dhv = dh_ref[...].astype(F32)
        r = lax.rsqrt(jnp.mean(xv * xv, axis=-1, keepdims=True) + RMS_EPS)
        xhat = xv * r
        dg_ref[...] += jnp.sum(dhv * xhat, axis=0, keepdims=True)
        if need_dx:
            dxh = dhv * g_ref[...]
            dx_ref[...] = dres_ref[...] + r * (dxh - xhat * jnp.mean(dxh * xhat, axis=-1, keepdims=True))

    rs = _row_spec(tt, d)
    if need_dx:
        in_specs, args = [rs, rs, rs, _const_spec((1, d))], (dres, dh, x, gain)
        out_specs = [rs, _const_spec((1, d))]
        out_shape = [jax.ShapeDtypeStruct((t, d), F32), jax.ShapeDtypeStruct((1, d), F32)]
    else:
        in_specs, args = [rs, rs, _const_spec((1, d))], (dh, x, gain)
        out_specs = [_const_spec((1, d))]
        out_shape = [jax.ShapeDtypeStruct((1, d), F32)]
    res = pl.pallas_call(body, name=name, grid=(t // tt,), in_specs=in_specs, out_specs=out_specs, out_shape=out_shape,
                         compiler_params=_cparams(("arbitrary",)))(*args)
    return res if need_dx else res[0]


def _final_loss(x3, target, gain, *, name):
    t, d = x3.shape
    tt = min(TOKEN_TILE, t)

    def body(x_ref, t_ref, g_ref, loss_ref, dx_ref, dg_ref):
        @pl.when(pl.program_id(0) == 0)
        def _():
            loss_ref[...] = jnp.zeros_like(loss_ref)
            dg_ref[...] = jnp.zeros_like(dg_ref)

        xv = x_ref[...]
        g = g_ref[...]
        r = lax.rsqrt(jnp.mean(xv * xv, axis=-1, keepdims=True) + RMS_EPS)
        xhat = xv * r
        err = xhat * g - t_ref[...]
        loss_ref[...] += 0.5 * jnp.sum(jnp.mean(err * err, axis=-1, keepdims=True), axis=0, keepdims=True)
        dy = err * (1.0 / d)
        dg_ref[...] += jnp.sum(dy * xhat, axis=0, keepdims=True)
        dxh = dy * g
        dx_ref[...] = r * (dxh - xhat * jnp.mean(dxh * xhat, axis=-1, keepdims=True))

    rs = _row_spec(tt, d)
    return pl.pallas_call(
        body, name=name, grid=(t // tt,), in_specs=[rs, rs, _const_spec((1, d))],
        out_specs=[_const_spec((1, 1)), rs, _const_spec((1, d))],
        out_shape=[jax.ShapeDtypeStruct((1, 1), F32), jax.ShapeDtypeStruct((t, d), F32), jax.ShapeDtypeStruct((1, d), F32)],
        compiler_params=_cparams(("arbitrary",)))(x3, target, gain)


SUBLANES = 8
SHIFT_ROWS = 40


def _conv_apply(sbuf_ref, w_ref, out_ref, tt, offsets, bias_ref=None):
    d = out_ref.shape[1]
    for cc in range(d // LANES):
        cs = slice(cc * LANES, (cc + 1) * LANES)
        taps = [jnp.broadcast_to(w_ref[k:k + 1, cs], (SUBLANES, LANES)) for k in range(CONV_WIDTH)]
        bias = None if bias_ref is None else jnp.broadcast_to(bias_ref[:, cs], (SUBLANES, LANES))

        def row_body(r, carry, cs=cs, taps=taps, bias=bias):
            r0 = pl.multiple_of(r * CONV_ROWS, CONV_ROWS)
            for q in range(CONV_ROWS // SUBLANES):
                acc = _tap(sbuf_ref, r0 + q * SUBLANES, cs, offsets[0]) * taps[0]
                for k in range(1, CONV_WIDTH):
                    acc = acc + _tap(sbuf_ref, r0 + q * SUBLANES, cs, offsets[k]) * taps[k]
                if bias is not None:
                    acc = acc + bias
                out_ref[pl.ds(r0 + q * SUBLANES, SUBLANES), cs] = acc
            return carry

        lax.fori_loop(0, tt // CONV_ROWS, row_body, 0)


def _fill_shifts(sbuf_ref, rows):
    d = sbuf_ref.shape[2]
    assert rows % SHIFT_ROWS == 0

    def row_body(i, carry):
        r0 = pl.multiple_of(i * SHIFT_ROWS, SUBLANES)
        for cc in range(d // CONV_COLS):
            cs = slice(cc * CONV_COLS, (cc + 1) * CONV_COLS)
            win = sbuf_ref[0, pl.ds(r0, SHIFT_ROWS + SUBLANES), cs]
            for sh in range(1, SUBLANES):
                sbuf_ref[sh, pl.ds(r0, SHIFT_ROWS), cs] = win[sh:sh + SHIFT_ROWS, :]
        return carry

    lax.fori_loop(0, rows // SHIFT_ROWS, row_body, 0)


def _tap(sbuf_ref, r0, cs, offset):
    sh = offset % SUBLANES
    return sbuf_ref[sh, pl.ds(pl.multiple_of(r0 + (offset - sh), SUBLANES), SUBLANES), cs]


def _conv_specs(bl, s, tt, d, col_a, col_g):
    nj = s // tt
    per = tt // CONV_HALO
    main_a = pl.BlockSpec((tt, d), lambda b, j: (b * nj + j, col_a))
    main_g = pl.BlockSpec((tt, d), lambda b, j: (b * nj + j, col_g))
    prev = lambda b, j: jnp.maximum((b * nj + j) * per - 1, 0)
    halo_a = pl.BlockSpec((CONV_HALO, d), lambda b, j: (prev(b, j), col_a))
    halo_g = pl.BlockSpec((CONV_HALO, d), lambda b, j: (prev(b, j), col_g))
    return main_a, main_g, halo_a, halo_g


def _fill_glu(sbuf_ref, a_ref, g_ref, ha_ref, hg_ref, tt):
    first = pl.program_id(1) == 0
    ha = ha_ref[...].astype(F32)
    hg = hg_ref[...].astype(F32)
    sbuf_ref[0, pl.ds(0, CONV_HALO), :] = jnp.where(first, 0.0, ha * _sigmoid(hg))
    av = a_ref[...].astype(F32)
    gv = g_ref[...].astype(F32)
    sbuf_ref[0, pl.ds(CONV_HALO, tt), :] = av * _sigmoid(gv)
    _fill_shifts(sbuf_ref, tt + CONV_HALO - SUBLANES)


def _conv_fwd(p, conv_w, conv_b, ln_g, ln_b, *, bl, s, name):
    t = p.shape[0]
    d = conv_w.shape[1]
    tt = min(TOKEN_TILE, s)
    off = CONV_HALO - (CONV_WIDTH - 1)

    def body(a_ref, g_ref, ha_ref, hg_ref, w_ref, b_ref, lg_ref, lb_ref, c_ref, act_ref, sbuf_ref, cbuf_ref):
        _fill_glu(sbuf_ref, a_ref, g_ref, ha_ref, hg_ref, tt)

        _conv_apply(sbuf_ref, w_ref, cbuf_ref, tt, [off + k for k in range(CONV_WIDTH)], bias_ref=b_ref)
        cv = cbuf_ref[...]
        c_ref[...] = cv.astype(BF16)
        mu = jnp.mean(cv, axis=-1, keepdims=True)
        dv = cv - mu
        rstd = lax.rsqrt(jnp.mean(dv * dv, axis=-1, keepdims=True) + LN_EPS)
        aln = dv * rstd * lg_ref[...] + lb_ref[...]
        act_ref[...] = (aln * _sigmoid(aln)).astype(BF16)

    main_a, main_g, halo_a, halo_g = _conv_specs(bl, s, tt, d, 0, 1)
    out_spec = pl.BlockSpec((tt, d), lambda b, j: (b * (s // tt) + j, 0))
    return pl.pallas_call(
        body, name=name, grid=(bl, s // tt),
        in_specs=[main_a, main_g, halo_a, halo_g, _const_spec((CONV_HALO, d)), _const_spec((1, d)), _const_spec((1, d)),
                  _const_spec((1, d))],
        out_specs=[out_spec, out_spec],
        out_shape=[jax.ShapeDtypeStruct((t, d), BF16), jax.ShapeDtypeStruct((t, d), BF16)],
        scratch_shapes=[pltpu.VMEM((SUBLANES, tt + CONV_HALO, d), F32), pltpu.VMEM((tt, d), F32)],
        compiler_params=_cparams(("parallel", "parallel")))(p, p, p, p, conv_w, conv_b, ln_g, ln_b)


def _conv_ln_bwd(dact, c, ln_g, ln_b, *, name):
    t, d = c.shape
    tt = min(TOKEN_TILE, t)

    def body(da_ref, c_ref, lg_ref, lb_ref, dc_ref, dlg_ref, dlb_ref):
        @pl.when(pl.program_id(0) == 0)
        def _():
            dlg_ref[...] = jnp.zeros_like(dlg_ref)
            dlb_ref[...] = jnp.zeros_like(dlb_ref)

        cv = c_ref[...].astype(F32)
        g = lg_ref[...]
        mu = jnp.mean(cv, axis=-1, keepdims=True)
        dv = cv - mu
        rstd = lax.rsqrt(jnp.mean(dv * dv, axis=-1, keepdims=True) + LN_EPS)
        chat = dv * rstd
        aln = chat * g + lb_ref[...]
        sg = _sigmoid(aln)
        daln = da_ref[...].astype(F32) * (sg * (1.0 + aln * (1.0 - sg)))
        dlb_ref[...] += jnp.sum(daln, axis=0, keepdims=True)
        dlg_ref[...] += jnp.sum(daln * chat, axis=0, keepdims=True)
        dchat = daln * g
        dc = rstd * (dchat - jnp.mean(dchat, axis=-1, keepdims=True)
                     - chat * jnp.mean(dchat * chat, axis=-1, keepdims=True))
        dc_ref[...] = dc.astype(BF16)

    rs = _row_spec(tt, d)
    cs = _const_spec((1, d))
    return pl.pallas_call(
        body, name=name, grid=(t // tt,), in_specs=[rs, rs, cs, cs], out_specs=[rs, cs, cs],
        out_shape=[jax.ShapeDtypeStruct((t, d), BF16), jax.ShapeDtypeStruct((1, d), F32), jax.ShapeDtypeStruct((1, d), F32)],
        compiler_params=_cparams(("arbitrary",)))(dact, c, ln_g, ln_b)


def _conv_bwd(dp, dc, p, conv_w, h1, *, bl, s, name):
    t = p.shape[0]
    d = conv_w.shape[1]
    tt = min(TOKEN_TILE, s)
    nj = s // tt
    per = tt // CONV_HALO
    off = CONV_HALO - (CONV_WIDTH - 1)
    last_blk = t // CONV_HALO - 1

    def body(dp_in, dc_ref, dcn_ref, a_ref, g_ref, ha_ref, hg_ref, w_ref, h1_ref, dp_ref, dw_ref, db_ref, dwin_ref,
             gbuf_ref, dbuf_ref, dglu_ref, acc_ref):
        del dp_in
        b, j = pl.program_id(0), pl.program_id(1)
        start = jnp.logical_and(b == 0, j == 0)
        end = jnp.logical_and(b == bl - 1, j == nj - 1)

        @pl.when(start)
        def _():
            acc_ref[...] = jnp.zeros_like(acc_ref)
            db_ref[...] = jnp.zeros_like(db_ref)
            dwin_ref[...] = jnp.zeros_like(dwin_ref)

        _fill_glu(gbuf_ref, a_ref, g_ref, ha_ref, hg_ref, tt)
        dcv = dc_ref[...].astype(F32)
        dbuf_ref[0, pl.ds(0, tt), :] = dcv
        dbuf_ref[0, pl.ds(tt, CONV_HALO), :] = jnp.where(j == nj - 1, 0.0, dcn_ref[...].astype(F32))
        _fill_shifts(dbuf_ref, tt + CONV_HALO - SUBLANES)
        db_ref[...] += jnp.sum(dcv, axis=0, keepdims=True)

        for cc in range(d // LANES):
            cs = slice(cc * LANES, (cc + 1) * LANES)

            def row_body(r, accs, cs=cs):
                r0 = pl.multiple_of(r * CONV_ROWS, CONV_ROWS)
                accs = list(accs)
                for q in range(CONV_ROWS // SUBLANES):
                    dcw = dbuf_ref[0, pl.ds(r0 + q * SUBLANES, SUBLANES), cs]
                    for k in range(CONV_WIDTH):
                        accs[k] = accs[k] + dcw * _tap(gbuf_ref, r0 + q * SUBLANES, cs, off + k)
                return tuple(accs)

            zero = jnp.zeros((SUBLANES, LANES), F32)
            accs = lax.fori_loop(0, tt // CONV_ROWS, row_body, (zero,) * CONV_WIDTH)
            for k in range(CONV_WIDTH):
                acc_ref[k, :, cs] += accs[k]

        _conv_apply(dbuf_ref, w_ref, dglu_ref, tt, [CONV_WIDTH - 1 - k for k in range(CONV_WIDTH)])
        dglu = dglu_ref[...]
        av = a_ref[...].astype(F32)
        sg = _sigmoid(g_ref[...].astype(F32))
        dp_ref[:, 0:d] = (dglu * sg).astype(BF16)
        dp_ref[:, d:2 * d] = (dglu * av * sg * (1.0 - sg)).astype(BF16)
        dwin_ref[...] += _dot(h1_ref[...], dp_ref[...], _TN)

        @pl.when(end)
        def _():
            for k in range(CONV_WIDTH):
                dw_ref[k:k + 1, :] = jnp.sum(acc_ref[k], axis=0, keepdims=True)
            dw_ref[CONV_WIDTH:CONV_HALO, :] = jnp.zeros((CONV_HALO - CONV_WIDTH, d), F32)

    main_a, main_g, halo_a, halo_g = _conv_specs(bl, s, tt, d, 0, 1)
    dc_main = pl.BlockSpec((tt, d), lambda b, j: (b * nj + j, 0))
    dc_next = pl.BlockSpec((CONV_HALO, d), lambda b, j: (jnp.minimum((b * nj + j + 1) * per, last_blk), 0))
    return pl.pallas_call(
        body, name=name, grid=(bl, nj),
        in_specs=[pl.BlockSpec(memory_space=pl.ANY), dc_main, dc_next, main_a, main_g, halo_a, halo_g,
                  _const_spec((CONV_HALO, d)), dc_main],
        out_specs=[pl.BlockSpec((tt, 2 * d), lambda b, j: (b * nj + j, 0)), _const_spec((CONV_HALO, d)), _const_spec((1, d)),
                   _const_spec((d, 2 * d))],
        out_shape=[jax.ShapeDtypeStruct(dp.shape, dp.dtype), jax.ShapeDtypeStruct((CONV_HALO, d), F32),
                   jax.ShapeDtypeStruct((1, d), F32), jax.ShapeDtypeStruct((d, 2 * d), F32)],
        scratch_shapes=[pltpu.VMEM((SUBLANES, tt + CONV_HALO, d), F32), pltpu.VMEM((SUBLANES, tt + CONV_HALO, d), F32),
                        pltpu.VMEM((tt, d), F32), pltpu.VMEM((CONV_HALO, SUBLANES, d), F32)],
        input_output_aliases={0: 0},
        compiler_params=_cparams(("arbitrary", "arbitrary")))(dp, dc, dc, p, p, p, p, conv_w, h1)


def _sgu_stats(bv):
    gv = _gelu(bv)
    mu = jnp.mean(gv, axis=-1, keepdims=True)
    dv = gv - mu
    rstd = lax.rsqrt(jnp.mean(dv * dv, axis=-1, keepdims=True) + LN_EPS)
    return dv * rstd, rstd


def _sgu_fwd(p, wm, bias, ln_g, ln_b, *, name):
    t = p.shape[0]
    d = ln_g.shape[1]
    tt = SGU_CHUNK
    gd = d // SGU_GROUPS

    def body(u_ref, v_ref, wm_ref, bias_ref, lg_ref, lb_ref, sg_ref, vn_ref):
        u = _gelu(u_ref[...].astype(F32))
        vhat, _ = _sgu_stats(v_ref[...].astype(F32))
        vb = (vhat * lg_ref[...] + lb_ref[...]).astype(BF16)
        vn_ref[...] = vb
        for g in range(SGU_GROUPS):
            gs = slice(g * gd, (g + 1) * gd)
            z = _dot(wm_ref[g], vb[:, gs], _NN) + bias_ref[g]
            sg_ref[:, gs] = (u[:, gs] * z).astype(BF16)

    rs = _row_spec(tt, d)
    return pl.pallas_call(
        body, name=name, grid=(t // tt,),
        in_specs=[_row_spec(tt, d, 2), _row_spec(tt, d, 3), _const_spec(wm.shape), _const_spec(bias.shape),
                  _const_spec((1, d)), _const_spec((1, d))],
        out_specs=[rs, rs], out_shape=[jax.ShapeDtypeStruct((t, d), BF16), jax.ShapeDtypeStruct((t, d), BF16)],
        compiler_params=_cparams(("parallel",)))(p, p, wm, bias, ln_g, ln_b)


def _sgu_bwd(dp, dy_b, w_out, p, vn, wm, wmt, bias, ln_g, *, name):
    t = p.shape[0]
    d = ln_g.shape[1]
    tt = SGU_CHUNK
    gd = d // SGU_GROUPS
    nsteps = t // tt

    def body(dp_in, dyb_ref, wout_ref, u_ref, v_ref, vn_ref, wm_ref, wmt_ref, bias_ref, lg_ref,
             dp_ref, dw_ref, dbs_ref, dlg_ref, dlb_ref, dz_acc):
        del dp_in
        i = pl.program_id(0)

        @pl.when(i == 0)
        def _():
            dw_ref[...] = jnp.zeros_like(dw_ref)
            dlg_ref[...] = jnp.zeros_like(dlg_ref)
            dlb_ref[...] = jnp.zeros_like(dlb_ref)
            dz_acc[...] = jnp.zeros_like(dz_acc)

        bu = u_ref[...].astype(F32)
        bv = v_ref[...].astype(F32)
        u = _gelu(bu)
        vhat, rstd = _sgu_stats(bv)
        vb = vn_ref[...]
        dsg = _dot(dyb_ref[...], wout_ref[...], _NT)
        row = lax.broadcasted_iota(jnp.int32, (tt, tt), 0)
        col = lax.broadcasted_iota(jnp.int32, (tt, tt), 1)
        causal = col <= row
        du_parts, dv_parts = [], []
        for g in range(SGU_GROUPS):
            gs = slice(g * gd, (g + 1) * gd)
            z = _dot(wm_ref[g], vb[:, gs], _NN) + bias_ref[g]
            du_parts.append(dsg[:, gs] * z)
            dz = dsg[:, gs] * u[:, gs]
            dz_acc[:, gs] += dz
            dzb = dz.astype(BF16)
            dw_ref[g] += jnp.where(causal, _dot(dzb, vb[:, gs], _NT), 0.0)
            dv_parts.append(_dot(wmt_ref[g], dzb, _NN))
        du = jnp.concatenate(du_parts, axis=1)
        dv = jnp.concatenate(dv_parts, axis=1)
        dp_ref[:, 0:d] = (du * _gelu_grad(bu)).astype(BF16)
        dlb_ref[...] += jnp.sum(dv, axis=0, keepdims=True)
        dlg_ref[...] += jnp.sum(dv * vhat, axis=0, keepdims=True)
        dvh = dv * lg_ref[...]
        dgv = rstd * (dvh - jnp.mean(dvh, axis=-1, keepdims=True) - vhat * jnp.mean(dvh * vhat, axis=-1, keepdims=True))
        dp_ref[:, d:2 * d] = (dgv * _gelu_grad(bv)).astype(BF16)

        @pl.when(i == nsteps - 1)
        def _():
            ones = jnp.ones((8, gd), F32)
            for g in range(SGU_GROUPS):
                gs = slice(g * gd, (g + 1) * gd)
                tot = lax.dot_general(ones, dz_acc[:, gs], (_NT, ((), ())), preferred_element_type=F32,
                                      precision=lax.Precision.HIGHEST)
                dbs_ref[g:g + 1, :] = tot[0:1, :]

    rs = _row_spec(tt, d)
    c1 = _const_spec((1, d))
    return pl.pallas_call(
        body, name=name, grid=(nsteps,),
        in_specs=[pl.BlockSpec(memory_space=pl.ANY), rs, _const_spec(w_out.shape), _row_spec(tt, d, 2), _row_spec(tt, d, 3),
                  rs, _const_spec(wm.shape), _const_spec(wmt.shape), _const_spec(bias.shape), c1],
        out_specs=[pl.BlockSpec((tt, 2 * d), lambda i: (i, 1)), _const_spec(wm.shape), _const_spec((SGU_GROUPS, tt)), c1, c1],
        out_shape=[jax.ShapeDtypeStruct(dp.shape, dp.dtype), jax.ShapeDtypeStruct(wm.shape, F32),
                   jax.ShapeDtypeStruct((SGU_GROUPS, tt), F32), jax.ShapeDtypeStruct((1, d), F32),
                   jax.ShapeDtypeStruct((1, d), F32)],
        scratch_shapes=[pltpu.VMEM((tt, d), F32)],
        input_output_aliases={0: 0},
        compiler_params=_cparams(("arbitrary",)))(dp, dy_b, w_out, p, p, vn, wm, wmt, bias, ln_g)


def _gates_fwd(p, ya, yb, b_gate, *, name):
    t, d = ya.shape
    tt = min(TOKEN_TILE, t)

    def body(ga_ref, gb_ref, ya_ref, yb_ref, bg_ref, o_ref):
        sa = _sigmoid(ga_ref[...].astype(F32) + bg_ref[0:1, :])
        sb = _sigmoid(gb_ref[...].astype(F32) + bg_ref[1:2, :])
        o_ref[...] = (sa * ya_ref[...].astype(F32) + sb * yb_ref[...].astype(F32)).astype(BF16)

    rs = _row_spec(tt, d)
    return pl.pallas_call(
        body, name=name, grid=(t // tt,),
        in_specs=[_row_spec(tt, d, 4), _row_spec(tt, d, 5), rs, rs, _const_spec(b_gate.shape)],
        out_specs=rs, out_shape=jax.ShapeDtypeStruct((t, d), BF16),
        compiler_params=_cparams(("parallel",)))(p, p, ya, yb, b_gate)


def _gates_bwd(dmerged, p, ya, yb, b_gate, *, name):
    t, d = ya.shape
    tt = min(TOKEN_TILE, t)

    def body(dm_ref, ga_ref, gb_ref, ya_ref, yb_ref, bg_ref, dp_ref, dya_ref, dyb_ref, dbg_ref):
        @pl.when(pl.program_id(0) == 0)
        def _():
            dbg_ref[...] = jnp.zeros_like(dbg_ref)

        dm = dm_ref[...].astype(F32)
        sa = _sigmoid(ga_ref[...].astype(F32) + bg_ref[0:1, :])
        sb = _sigmoid(gb_ref[...].astype(F32) + bg_ref[1:2, :])
        dya_ref[...] = (dm * sa).astype(BF16)
        dyb_ref[...] = (dm * sb).astype(BF16)
        dga = dm * ya_ref[...].astype(F32) * sa * (1.0 - sa)
        dgb = dm * yb_ref[...].astype(F32) * sb * (1.0 - sb)
        dp_ref[:, 0:d] = dga.astype(BF16)
        dp_ref[:, d:2 * d] = dgb.astype(BF16)
        dbg_ref[0:1, :] += jnp.sum(dga, axis=0, keepdims=True)
        dbg_ref[1:2, :] += jnp.sum(dgb, axis=0, keepdims=True)

    rs = _row_spec(tt, d)
    return pl.pallas_call(
        body, name=name, grid=(t // tt,),
        in_specs=[rs, _row_spec(tt, d, 4), _row_spec(tt, d, 5), rs, rs, _const_spec(b_gate.shape)],
        out_specs=[pl.BlockSpec((tt, 2 * d), lambda i: (i, 2)), rs, rs, _const_spec((8, d))],
        out_shape=[jax.ShapeDtypeStruct(p.shape, BF16), jax.ShapeDtypeStruct((t, d), BF16),
                   jax.ShapeDtypeStruct((t, d), BF16), jax.ShapeDtypeStruct((8, d), F32)],
        compiler_params=_cparams(("arbitrary",)))(dmerged, p, p, ya, yb, b_gate)


def _softmax_rows(s):
    e = jnp.exp(s - jnp.max(s, axis=-1, keepdims=True))
    return e / jnp.sum(e, axis=-1, keepdims=True)


def _attn_fwd(q, kv, x1, w_xo, gain, *, bl, s, name):
    t, d = q.shape
    mlen = kv.shape[0] // bl
    hd = d // HEADS
    tq = min(ATTN_TILE, s)
    nq = s // tq
    scale = hd ** -0.5

    def body(q_ref, kv_ref, x1_ref, w_ref, g_ref, o_ref, x2_ref, h_ref):
        for h in range(HEADS):
            hs = slice(h * hd, (h + 1) * hd)
            vs = slice(d + h * hd, d + (h + 1) * hd)
            pr = _softmax_rows(_dot(q_ref[:, hs], kv_ref[:, hs], _NT) * scale)
            o_ref[:, hs] = _dot(pr.astype(BF16), kv_ref[:, vs], _NN).astype(BF16)
        x2 = x1_ref[...] + _dot(o_ref[...], w_ref[...], _NN)
        x2_ref[...] = x2
        h_ref[...] = _rms_apply(x2, g_ref[...]).astype(BF16)

    qs = pl.BlockSpec((tq, d), lambda b, j: (b * nq + j, 0))
    return pl.pallas_call(
        body, name=name, grid=(bl, nq),
        in_specs=[qs, pl.BlockSpec((mlen, 2 * d), lambda b, j: (b, 0)), qs, _const_spec(w_xo.shape), _const_spec((1, d))],
        out_specs=[qs, qs, qs],
        out_shape=[jax.ShapeDtypeStruct((t, d), BF16), jax.ShapeDtypeStruct((t, d), F32), jax.ShapeDtypeStruct((t, d), BF16)],
        compiler_params=_cparams(("parallel", "parallel")))(q, kv, x1, w_xo, gain)


def _attn_bwd(q, kv, do, *, bl, s, name):
    t, d = q.shape
    mlen = kv.shape[0] // bl
    hd = d // HEADS
    tq = min(ATTN_TILE, s)
    nq = s // tq
    scale = hd ** -0.5

    def body(q_ref, kv_ref, do_ref, dq_ref, dkv_ref):
        @pl.when(pl.program_id(1) == 0)
        def _():
            dkv_ref[...] = jnp.zeros_like(dkv_ref)

        for h in range(HEADS):
            hs = slice(h * hd, (h + 1) * hd)
            vs = slice(d + h * hd, d + (h + 1) * hd)
            qh, kh, vh, doh = q_ref[:, hs], kv_ref[:, hs], kv_ref[:, vs], do_ref[:, hs]
            pr = _softmax_rows(_dot(qh, kh, _NT) * scale)
            dpr = _dot(doh, vh, _NT)
            dkv_ref[:, vs] += _dot(pr.astype(BF16), doh, _TN)
            ds = (pr * (dpr - jnp.sum(dpr * pr, axis=-1, keepdims=True)) * scale).astype(BF16)
            dq_ref[:, hs] = _dot(ds, kh, _NN).astype(BF16)
            dkv_ref[:, hs] += _dot(ds, qh, _TN)

    qs = pl.BlockSpec((tq, d), lambda b, j: (b * nq + j, 0))
    ks = pl.BlockSpec((mlen, 2 * d), lambda b, j: (b, 0))
    return pl.pallas_call(
        body, name=name, grid=(bl, nq), in_specs=[qs, ks, qs], out_specs=[qs, ks],
        out_shape=[jax.ShapeDtypeStruct((t, d), BF16), jax.ShapeDtypeStruct(kv.shape, F32)],
        compiler_params=_cparams(("parallel", "arbitrary")))(q, kv, do)


def _swiglu_fwd(gu, *, name):
    t, f2 = gu.shape
    f = f2 // 2
    tt = min(TOKEN_TILE, t)

    def body(gu_ref, o_ref):
        gt = gu_ref[:, 0:f].astype(F32)
        up = gu_ref[:, f:f2].astype(F32)
        o_ref[...] = (gt * _sigmoid(gt) * up).astype(BF16)

    return pl.pallas_call(
        body, name=name, grid=(t // tt,), in_specs=[_row_spec(tt, f2)], out_specs=_row_spec(tt, f),
        out_shape=jax.ShapeDtypeStruct((t, f), BF16), compiler_params=_cparams(("parallel",)))(gu)


def _swiglu_bwd(gu, dact, *, name):
    t, f2 = gu.shape
    f = f2 // 2
    tt = min(TOKEN_TILE, t)

    def body(gu_ref, da_ref, o_ref):
        gt = gu_ref[:, 0:f].astype(F32)
        up = gu_ref[:, f:f2].astype(F32)
        da = da_ref[...].astype(F32)
        sg = _sigmoid(gt)
        o_ref[:, 0:f] = (da * up * sg * (1.0 + gt * (1.0 - sg))).astype(BF16)
        o_ref[:, f:f2] = (da * gt * sg).astype(BF16)

    return pl.pallas_call(
        body, name=name, grid=(t // tt,), in_specs=[_row_spec(tt, f2), _row_spec(tt, f)], out_specs=_row_spec(tt, f2),
        out_shape=jax.ShapeDtypeStruct((t, f2), BF16), compiler_params=_cparams(("parallel",)))(gu, dact)


def _mesh_pos():
    return lax.axis_index("x"), lax.axis_index("y"), lax.axis_index("c")


def _all_gather(arrs, *, name):
    n = len(arrs)
    hbm = pl.BlockSpec(memory_space=pl.ANY)

    def body(*refs):
        ins, outs = refs[:n], refs[n:2 * n]
        send_sems, recv_sems, loc_sems = refs[2 * n:]
        x, y, c = _mesh_pos()
        me, sib = (x, y, c), (x, y, 1 - c)
        chips = [(1 - x, y), (x, 1 - y), (1 - x, 1 - y)]

        def idx(dev):
            return 4 * dev[0] + 2 * dev[1] + dev[2]

        def copy(w, k, block, to, from_input=False):
            return pltpu.make_async_remote_copy(
                src_ref=ins[w] if from_input else outs[w].at[idx(block)], dst_ref=outs[w].at[idx(block)],
                send_sem=send_sems.at[w, k], recv_sem=recv_sems.at[w, k], device_id=to, device_id_type=MESH_ID)

        own = [pltpu.make_async_copy(ins[w], outs[w].at[idx(me)], loc_sems.at[w]) for w in range(n)]
        for cp in own:
            cp.start()
        first = []
        for w in range(n):
            first.append(copy(w, 0, me, sib, True))
            first += [copy(w, 1 + j, me, (*chip, c), True) for j, chip in enumerate(chips)]
        for cp in first:
            cp.start()
        passed = []
        for j, chip in enumerate(chips):
            for w in range(n):
                copy(w, 1 + j, (*chip, c), me).wait_recv()
                fwd = copy(w, 4 + j, (*chip, c), sib)
                fwd.start()
                passed.append(fwd)
        for w in range(n):
            copy(w, 0, sib, me).wait_recv()
            for j, chip in enumerate(chips):
                copy(w, 4 + j, (*chip, 1 - c), me).wait_recv()
        for cp in first + passed:
            cp.wait_send()
        for cp in own:
            cp.wait()

    return pl.pallas_call(
        body, name=name, in_specs=[hbm] * n, out_specs=[hbm] * n,
        out_shape=[jax.ShapeDtypeStruct((N_DEV, *a.shape), a.dtype) for a in arrs],
        scratch_shapes=[pltpu.SemaphoreType.DMA((n, 7)), pltpu.SemaphoreType.DMA((n, 7)), pltpu.SemaphoreType.DMA((n,))],
    )(*arrs)


_HBM = pl.BlockSpec(memory_space=pltpu.HBM)
_SEM = pl.BlockSpec(memory_space=pltpu.SEMAPHORE)
_ANY = pl.BlockSpec(memory_space=pl.ANY)
_EFFECT = pltpu.SideEffectType.DATAFLOW_SIDE_EFFECTING
N_PEERS = N_DEV - 1


def _related(pos, r):
    x, y, c = pos
    return (1 - x if r & 4 else x, 1 - y if r & 2 else y, 1 - c if r & 1 else c)


def _dev_index(dev):
    return 4 * dev[0] + 2 * dev[1] + dev[2]


def _in_hbm(a):
    return pltpu.with_memory_space_constraint(a, pltpu.HBM)


def _split_copies(kind, srcs, lands, send_sems, recv_sems):
    pos = _mesh_pos()
    me = _dev_index(pos)
    out = []
    for w in range(len(srcs)):
        for r in range(1, N_DEV):
            peer = _related(pos, r)
            if kind == "gather":
                src, dst_here, dst_there = srcs[w], lands[w].at[_dev_index(peer)], lands[w].at[me]
            else:
                src, dst_here, dst_there = srcs[w].at[_dev_index(peer)], lands[w].at[r - 1], lands[w].at[r - 1]
            out.append((src, dst_here, dst_there, send_sems.at[w * N_PEERS + r - 1], recv_sems.at[w * N_PEERS + r - 1], peer))
    return out


def _copy_start(kind, srcs, land_shapes, *, name, after=None):
    n = len(srcs)
    n_after = 0 if after is None else 1

    def body(*refs):
        src_refs, land_refs = refs[:n], refs[n:2 * n]
        send_sems, recv_sems = refs[2 * n + n_after], refs[2 * n + n_after + 1]
        token = refs[-1]
        for src, _, dst, ssem, rsem, peer in _split_copies(kind, src_refs, land_refs, send_sems, recv_sems):
            pltpu.make_async_remote_copy(src_ref=src, dst_ref=dst, send_sem=ssem, recv_sem=rsem, device_id=peer,
                                         device_id_type=MESH_ID).start()
        token[...] = jnp.zeros_like(token)

    lands = [_in_hbm(lax.empty(shape, s.dtype)) for s, shape in zip(srcs, land_shapes)]
    res = pl.pallas_call(
        body, name=name,
        out_shape=(pltpu.SemaphoreType.DMA((n * N_PEERS,)), pltpu.SemaphoreType.DMA((n * N_PEERS,)),
                   *[pltpu.HBM(s.shape, s.dtype) for s in srcs], *[pltpu.HBM(l.shape, l.dtype) for l in lands],
                   jax.ShapeDtypeStruct((8, 128), F32)),
        in_specs=[_HBM] * (2 * n) + [_ANY] * n_after,
        out_specs=(_SEM, _SEM, *[_HBM] * (2 * n), pl.BlockSpec(memory_space=pltpu.VMEM)),
        input_output_aliases={i: 2 + i for i in range(2 * n)},
        compiler_params=pltpu.CompilerParams(has_side_effects=_EFFECT),
    )(*[_in_hbm(s) for s in srcs], *lands, *([] if after is None else [after]))
    return res[0], res[1], list(res[2:2 + n]), list(res[2 + n:2 + 2 * n]), res[-1]


def _copy_wait(kind, send_sems, recv_sems, srcs, lands, after, *, name):
    n = len(srcs)

    def body(*refs):
        src_refs, land_refs = refs[:n], refs[n:2 * n]
        ssems, rsems = refs[2 * n], refs[2 * n + 1]
        for src, dst, _, ssem, rsem, peer in _split_copies(kind, src_refs, land_refs, ssems, rsems):
            cp = pltpu.make_async_remote_copy(src_ref=src, dst_ref=dst, send_sem=ssem, recv_sem=rsem, device_id=peer,
                                              device_id_type=MESH_ID)
            cp.wait_send()
            cp.wait_recv()

    res = pl.pallas_call(
        body, name=name,
        out_shape=(*[pltpu.HBM(s.shape, s.dtype) for s in srcs], *[pltpu.HBM(l.shape, l.dtype) for l in lands]),
        in_specs=[_HBM] * (2 * n) + [_SEM, _SEM, _ANY], out_specs=tuple([_HBM] * (2 * n)),
        input_output_aliases={i: i for i in range(2 * n)},
        compiler_params=pltpu.CompilerParams(has_side_effects=_EFFECT),
    )(*srcs, *lands, send_sems, recv_sems, after)
    return list(res[:n]), list(res[n:])


def _row_tile(rows):
    return max(tr for tr in range(16, min(rows, 512) + 1, 16) if rows % tr == 0)


def _adamw_math(w, g, m, v):
    m2 = ADAM_B1 * m + (1.0 - ADAM_B1) * g
    v2 = ADAM_B2 * v + (1.0 - ADAM_B2) * (g * g)
    m_hat = m2 / (1.0 - ADAM_B1 ** ADAM_STEP)
    v_hat = v2 / (1.0 - ADAM_B2 ** ADAM_STEP)
    delta = -ADAM_LR * (m_hat / (jnp.sqrt(v_hat) + ADAM_EPS) + ADAM_WD * w)
    return delta, m2, v2


def _adamw_shard(partials, landed, dev, w, m, v, *, name):
    r, c = w.shape
    tr = _row_tile(r)

    def body(dev_ref, p_ref, l_ref, w_ref, m_ref, v_ref, g_out, d_out, m_out, v_out):
        del dev_ref
        g = p_ref[...].astype(F32)
        for k in range(N_PEERS):
            g = g + l_ref[k].astype(F32)
        delta, m2, v2 = _adamw_math(w_ref[...], g, m_ref[...], v_ref[...])
        g_out[...] = g
        d_out[...] = delta
        m_out[...] = m2
        v_out[...] = v2

    blk = pl.BlockSpec((tr, c), lambda i, dev_ref: (i, 0))
    gs = pltpu.PrefetchScalarGridSpec(
        num_scalar_prefetch=1, grid=(r // tr,),
        in_specs=[pl.BlockSpec((None, tr, c), lambda i, dev_ref: (dev_ref[0], i, 0)),
                  pl.BlockSpec((N_PEERS, tr, c), lambda i, dev_ref: (0, i, 0)), blk, blk, blk],
        out_specs=[blk] * 4)
    return pl.pallas_call(
        body, name=name, grid_spec=gs, out_shape=[jax.ShapeDtypeStruct((r, c), F32)] * 4,
        compiler_params=_cparams(("parallel",)))(dev, partials, landed, w, m, v)


def _adamw_small(parts, dev, w, m, v, *, name, col_block):
    _, r, d = parts.shape
    cols = w.shape[1]

    def body(dev_ref, p_ref, w_ref, m_ref, v_ref, g_out, d_out, m_out, v_out):
        del dev_ref
        g = p_ref[0]
        for k in range(1, N_DEV):
            g = g + p_ref[k]
        delta, m2, v2 = _adamw_math(w_ref[...], g, m_ref[...], v_ref[...])
        g_out[...] = g
        d_out[...] = delta
        m_out[...] = m2
        v_out[...] = v2

    blk = pl.BlockSpec((r, cols), lambda i, dev_ref: (0, 0))
    pidx = (lambda i, dev_ref: (0, 0, dev_ref[0])) if col_block else (lambda i, dev_ref: (0, 0, 0))
    gs = pltpu.PrefetchScalarGridSpec(
        num_scalar_prefetch=1, grid=(1,),
        in_specs=[pl.BlockSpec((N_DEV, r, cols), pidx), blk, blk, blk], out_specs=[blk] * 4)
    return pl.pallas_call(
        body, name=name, grid_spec=gs, out_shape=[jax.ShapeDtypeStruct((r, cols), F32)] * 4,
        compiler_params=_cparams(("arbitrary",)))(dev, parts, w, m, v)


def _pad_rows(a, rows):
    return jnp.pad(a, ((0, rows - a.shape[0]), (0, 0)))


def _unblock_cols(g):
    return jnp.transpose(g, (1, 0, 2)).reshape(g.shape[1], N_DEV * g.shape[2])


def _block_cols(full):
    r, c8 = full.shape
    return jnp.transpose(full.reshape(r, N_DEV, c8 // N_DEV), (1, 0, 2))


def kernel(x, mem, norm_mix, w_in, b_gate, conv_w, conv_b, conv_ln_g, conv_ln_b, w_conv_out, sgu_ln_g, sgu_ln_b, sgu_w, sgu_b, w_sgu_out, w_mix_out, norm_xattn, norm_mem, w_q, w_kv, w_xo, norm_ffn, w_gu, w_down, norm_final, loss_target, m_norm_mix, m_w_in, m_b_gate, m_conv_w, m_conv_b, m_conv_ln_g, m_conv_ln_b, m_w_conv_out, m_sgu_ln_g, m_sgu_ln_b, m_sgu_w, m_sgu_b, m_w_sgu_out, m_w_mix_out, m_norm_xattn, m_norm_mem, m_w_q, m_w_kv, m_w_xo, m_norm_ffn, m_w_gu, m_w_down, m_norm_final, v_norm_mix, v_w_in, v_b_gate, v_conv_w, v_conv_b, v_conv_ln_g, v_conv_ln_b, v_w_conv_out, v_sgu_ln_g, v_sgu_ln_b, v_sgu_w, v_sgu_b, v_w_sgu_out, v_w_mix_out, v_norm_xattn, v_norm_mem, v_w_q, v_w_kv, v_w_xo, v_norm_ffn, v_w_gu, v_w_down, v_norm_final):
    given = dict(locals())
    bl, s, d = x.shape
    t = bl * s
    xf = x.reshape(t, d)
    tgt = loss_target.reshape(t, d)
    memf = mem.reshape(bl * mem.shape[1], d)
    cx, cy, cc = lax.axis_index("x"), lax.axis_index("y"), lax.axis_index("c")
    dev = 4 * cx + 2 * cy + cc
    dev_id = dev.astype(jnp.int32).reshape(1)
    col_sharded = ["w_in", "w_kv"]
    transposed = ["w_gu"]

    def shard_of(name, prefix=""):
        a = given[prefix + name][0]
        return jnp.transpose(a) if name in transposed else a

    def full_weight(name, blocks):
        return _unblock_cols(blocks) if name in col_sharded else blocks.reshape(N_DEV * blocks.shape[1], blocks.shape[2])

    h1, p, w_in_blocks = _in_proj_gather(xf, norm_mix, w_in[0].astype(BF16), name="in_proj")
    g_bg, g_cw = _all_gather([_pad_rows(b_gate[0], 8), _pad_rows(conv_w[0], CONV_HALO)], name="gather_small_params")
    early = ["w_conv_out", "w_sgu_out", "w_mix_out", "w_q", "w_kv", "w_xo"]
    late = ["w_gu", "w_down"]
    shards = {n: shard_of(n).astype(BF16) for n in early + late}
    started = {}
    for grp, names in (("early", early), ("late", late)):
        srcs = [shards[n] for n in names]
        started[grp] = _copy_start("gather", srcs, [(N_DEV, *a.shape) for a in srcs], name=f"gather_{grp}_start", after=p)
    token = started["early"][4][0:1, 0:1] + started["late"][4][0:1, 0:1]
    wfull = {}
    bg_full = _unblock_cols(g_bg)
    cw_full = _unblock_cols(g_cw)

    def finish_gather(grp, names, after):
        ssem, rsem, srcs, lands, _ = started[grp]
        _, lands = _copy_wait("gather", ssem, rsem, srcs, lands, after, name=f"gather_{grp}_wait")
        for n, land in zip(names, lands):
            wfull[n] = full_weight(n, lax.dynamic_update_index_in_dim(land, shards[n], dev, 0))

    tri = jnp.tril(jnp.ones((SGU_CHUNK, SGU_CHUNK), bool))
    wm32 = jnp.where(tri[None], sgu_w[0], 0.0)
    wm = wm32.astype(BF16)
    wmt = jnp.transpose(wm32, (0, 2, 1)).astype(BF16)
    sgu_bias = jnp.broadcast_to(sgu_b[0][:, :, None], (SGU_GROUPS, SGU_CHUNK, d // SGU_GROUPS))

    c_conv, a_act = _conv_fwd(p, cw_full, conv_b + token, conv_ln_g, conv_ln_b, bl=bl, s=s, name="conv_fwd")
    sg, vn = _sgu_fwd(p, wm, sgu_bias, sgu_ln_g, sgu_ln_b + token, name="sgu_fwd")
    finish_gather("early", early, a_act[0:16, 0:128] + sg[0:16, 0:128])
    y_a = _matmul(a_act, wfull["w_conv_out"], mode="nn", out_dtype=BF16, name="mm_conv_out", tm=1024, tn=1024, tk=1024)
    y_b = _matmul(sg, wfull["w_sgu_out"], mode="nn", out_dtype=BF16, name="mm_sgu_out", tm=1024, tn=1024, tk=1024)
    merged, x1, h2, q = _mix_out(p, y_a, y_b, bg_full, xf, wfull["w_mix_out"], norm_xattn, wfull["w_q"], name="mix_out")
    mem_n = _rms_fwd(memf, norm_mem, name="rms_mem")
    kv = _matmul(mem_n, wfull["w_kv"], mode="nn", out_dtype=BF16, name="mm_kv", tm=1024, tn=1024, tk=1024)
    o, x2, h3 = _attn_fwd(q, kv, x1, wfull["w_xo"], norm_ffn, bl=bl, s=s, name="attn_fwd")
    finish_gather("late", late, h3)
    gu, act, dx3, loss_part, d_norm_final = _ffn_fwd(h3, x2, tgt, wfull["w_gu"], wfull["w_down"],
                                                     norm_final.reshape(1, d), name="ffn_fwd")

    grads = {}
    sent = []

    def send_grads(names, tag):
        blocks = []
        for n in names:
            g = grads[n]
            if g.ndim == 2:
                g = _block_cols(g) if n in col_sharded else g.reshape(N_DEV, -1, g.shape[1])
            blocks.append(g)
        ssem, rsem, srcs, lands, tok = _copy_start("scatter", blocks, [(N_PEERS, *g.shape[1:]) for g in blocks],
                                                   name=f"grads_{tag}_start")
        sent.append((names, ssem, rsem, srcs, lands))
        return tok[0:1, 0:1]

    dgu, dx2, do, d_norm_ffn = _ffn_bwd(dx3, gu, x2, wfull["w_down"], wfull["w_gu"], norm_ffn, wfull["w_xo"], name="ffn_bwd")
    grads["w_down"] = _matmul(act, dx3, mode="tn", out_dtype=BF16, name="mm_dw_down", tm=1408, tn=1024, tk=1024)
    grads["w_gu"] = _matmul(dgu, h3, mode="tn", out_dtype=BF16, name="mm_dw_gu", tm=1408, tn=1024, tk=1024)
    tok = send_grads(["w_down", "w_gu"], "ffn")
    grads["w_xo"] = _matmul(o, dx2, mode="tn", out_dtype=BF16, name="mm_dw_xo", tm=1024, tn=1024, tk=1024)
    dq, dkv = _attn_bwd(q, kv, do, bl=bl, s=s, name="attn_bwd")
    grads["w_kv"] = _matmul(mem_n, dkv, mode="tn", out_dtype=BF16, name="mm_dw_kv", tm=1024, tn=256, tk=1024,
                            col_blocks=N_DEV)
    tok2 = send_grads(["w_xo", "w_kv"], "attn")
    dmem_n = _matmul(dkv, wfull["w_kv"], mode="nt", out_dtype=F32, name="mm_d_mem", tm=512, tn=1024, tk=2048)
    d_norm_mem = _rms_bwd(None, dmem_n, memf, norm_mem, name="rms_mem_bwd", need_dx=False)
    dx1, d_norm_xattn, dw_q = _proj_rms_bwd(dq, dx2, x1, wfull["w_q"], norm_xattn + (tok + tok2), name="q_rms_bwd", h=h2)
    dp, dy_a, dy_b, d_b_gate, dw_mix, dw_in_gates = _gates_bwd_fused(dx1, p, y_a, y_b, bg_full, wfull["w_mix_out"],
                                                                    merged, h1, name="gates_bwd")
    grads["w_q"] = dw_q.astype(BF16)
    grads["w_mix_out"] = dw_mix.astype(BF16)
    grads["w_sgu_out"] = _matmul(sg, dy_b, mode="tn", out_dtype=BF16, name="mm_dw_sgu", tm=1024, tn=1024, tk=1024)
    dc, d_conv_ln_g, d_conv_ln_b, dw_conv = _conv_ln_bwd_fused(dy_a, c_conv, a_act, wfull["w_conv_out"], conv_ln_g,
                                                               conv_ln_b, name="conv_ln_bwd")
    grads["w_conv_out"] = dw_conv.astype(BF16)
    tok = send_grads(["w_q", "w_mix_out", "w_sgu_out", "w_conv_out"], "mixer")
    dp, d_sgu_w, d_sgu_b, d_sgu_ln_g, d_sgu_ln_b = _sgu_bwd(dp, dy_b, wfull["w_sgu_out"], p, vn, wm, wmt, sgu_bias,
                                                             sgu_ln_g + tok, name="sgu_bwd")
    dw_in_sgu = _matmul(h1, dp, mode="tn", out_dtype=BF16, name="mm_dw_in_sgu", tm=1024, tn=1024, tk=2048,
                        b_cols=(2 * d, 2 * d))
    dp, d_conv_w, d_conv_b, dw_in_conv = _conv_bwd(dp, dc, p, cw_full, h1, bl=bl, s=s, name="conv_bwd")
    grads["w_in"] = _block_cols(jnp.concatenate([dw_in_conv.astype(BF16), dw_in_sgu, dw_in_gates.astype(BF16)], axis=1))
    tok = send_grads(["w_in"], "in")
    grad_x, d_norm_mix = _proj_rms_bwd(dp, dx1, xf, w_in_blocks, norm_mix + tok, name="in_proj_bwd")
    out = {}

    rep_names = ["norm_mix", "conv_b", "conv_ln_g", "conv_ln_b", "sgu_ln_g", "sgu_ln_b", "norm_xattn", "norm_mem",
                 "norm_ffn", "norm_final", "sgu_b"]
    rep_grads = [d_norm_mix, d_conv_b, d_conv_ln_g, d_conv_ln_b, d_sgu_ln_g, d_sgu_ln_b, d_norm_xattn, d_norm_mem,
                 d_norm_ffn, d_norm_final, d_sgu_b.reshape(1, d)]
    nrep = len(rep_names)
    pad = jnp.zeros((16 - nrep, d), F32)
    sgw_rows = SGU_GROUPS * SGU_CHUNK * SGU_CHUNK // d

    def pack_rep(vecs, sgw, extra=None):
        fill = pad if extra is None else jnp.concatenate([extra, pad[1:]], axis=0)
        return jnp.concatenate([v.reshape(1, d) for v in vecs] + [fill, sgw.reshape(sgw_rows, d)], axis=0)

    def pack_col(bg, cw):
        return jnp.concatenate([_pad_rows(bg, 8), _pad_rows(cw, CONV_HALO)], axis=0)

    small_a = pack_rep(rep_grads, d_sgu_w, extra=jnp.broadcast_to(loss_part, (1, d)))
    small_b = jnp.concatenate([d_b_gate, d_conv_w], axis=0)
    parts_a, parts_b = _all_gather([small_a, small_b], name="gather_small_grads")
    res_a = _adamw_small(parts_a, dev_id, pack_rep([given[n] for n in rep_names], sgu_w),
                         pack_rep([given["m_" + n] for n in rep_names], m_sgu_w),
                         pack_rep([given["v_" + n] for n in rep_names], v_sgu_w), name="adamw_small", col_block=False)
    res_b = _adamw_small(parts_b, dev_id, pack_col(b_gate[0], conv_w[0]), pack_col(m_b_gate[0], m_conv_w[0]),
                         pack_col(v_b_gate[0], v_conv_w[0]), name="adamw_small_cols", col_block=True)
    for i, n in enumerate(rep_names):
        out[n] = [r[i].reshape(given[n].shape) for r in res_a]
    out["sgu_w"] = [r[16:16 + sgw_rows].reshape(sgu_w.shape) for r in res_a]
    out["b_gate"] = [r[0:2][None] for r in res_b]
    out["conv_w"] = [r[8:8 + CONV_WIDTH][None] for r in res_b]

    for names, ssem, rsem, srcs, lands in sent:
        srcs, lands = _copy_wait("scatter", ssem, rsem, srcs, lands, res_a[0], name=f"grads_{names[0]}_wait")
        for n, partials, landed in zip(names, srcs, lands):
            res = _adamw_shard(partials, landed, dev_id, shard_of(n), shard_of(n, "m_"), shard_of(n, "v_"),
                               name=f"adamw_{n}")
            out[n] = [(jnp.transpose(r) if n in transposed else r)[None] for r in res]

    order = ["norm_mix", "w_in", "b_gate", "conv_w", "conv_b", "conv_ln_g", "conv_ln_b", "w_conv_out", "sgu_ln_g",
             "sgu_ln_b", "sgu_w", "sgu_b", "w_sgu_out", "w_mix_out", "norm_xattn", "norm_mem", "w_q", "w_kv", "w_xo",
             "norm_ffn", "w_gu", "w_down", "norm_final"]
    loss = res_a[0][nrep, 0]
    return (loss, grad_x.reshape(x.shape), *[out[n][0] for n in order], *[out[n][1] for n in order],
            *[out[n][2] for n in order], *[out[n][3] for n in order])
```

```python
import functools

import jax
import jax.numpy as jnp
from jax import lax
from jax.experimental import pallas as pl
from jax.experimental.pallas import tpu as pltpu

F32 = jnp.float32
BF16 = jnp.bfloat16
RMS_EPS = 1e-6
LN_EPS = 1e-5
CONV_WIDTH = 31
CONV_HALO = 32
CONV_ROWS = 64
CONV_COLS = 256
LANES = 128
SGU_CHUNK = 128
SGU_GROUPS = 8
HEADS = 4
N_DEV = 8
ADAM_LR, ADAM_B1, ADAM_B2, ADAM_EPS, ADAM_WD, ADAM_STEP = 0.001, 0.9, 0.999, 1e-08, 0.01, 10
VMEM_LIMIT = 56 * 1024 * 1024
TOKEN_TILE = 256
ATTN_TILE = 1024
MESH_ID = pl.DeviceIdType.MESH

_GELU_K = 0.7978845608028654
_GELU_C = 0.044715


def _cparams(sem=None):
    return pltpu.CompilerParams(dimension_semantics=sem, vmem_limit_bytes=VMEM_LIMIT)


def _sigmoid(v):
    return 0.5 * jnp.tanh(0.5 * v) + 0.5


def _gelu(v):
    return 0.5 * v * (1.0 + jnp.tanh(_GELU_K * (v + _GELU_C * v * v * v)))


def _gelu_grad(v):
    th = jnp.tanh(_GELU_K * (v + _GELU_C * v * v * v))
    return 0.5 * (1.0 + th) + 0.5 * v * (1.0 - th * th) * _GELU_K * (1.0 + 3.0 * _GELU_C * v * v)


def _dot(a, b, dims):
    return lax.dot_general(a, b, (dims, ((), ())), preferred_element_type=F32)


_NN = ((1,), (0,))
_NT = ((1,), (1,))
_TN = ((0,), (0,))


def _matmul(a, b, *, mode, out_dtype, name, tm=512, tn=512, tk=512, chunk=None, residual=None, rms_gain=None,
            col_blocks=None, b_cols=None):
    if mode == "nn":
        (m, k), (_, n) = a.shape, b.shape
    elif mode == "nt":
        (m, k), (n, _) = a.shape, b.shape
    else:
        (k, m), (_, n) = a.shape, b.shape
    b_first = 0
    if b_cols is not None:
        assert mode == "tn"
        b_first, n = b_cols
    tm, tn, tk = min(tm, m), min(tn, n), min(tk, k)
    assert b_first % tn == 0
    b_first //= tn
    assert m % tm == 0 and n % tn == 0 and k % tk == 0, (name, a.shape, b.shape, tm, tn, tk)
    nk = k // tk
    dims = {"nn": _NN, "nt": _NT, "tn": _TN}[mode]
    chunk = tn if chunk is None else min(chunk, tn)
    assert tn % chunk == 0
    if rms_gain is not None:
        assert tn == n and chunk == n

    def body(*refs):
        refs = list(refs)
        a_ref, b_ref = refs[:2]
        pos = 2
        r_ref = g_ref = None
        if residual is not None:
            r_ref = refs[pos]
            pos += 1
        if rms_gain is not None:
            g_ref = refs[pos]
            pos += 1
        o_ref = refs[pos]
        pos += 1
        h_ref = None
        if rms_gain is not None:
            h_ref = refs[pos]
            pos += 1
        acc_ref = refs[pos] if nk > 1 else None
        av = a_ref[...].astype(BF16)
        for c0 in range(0, tn, chunk):
            cs = slice(c0, c0 + chunk)
            bv = (b_ref[cs, :] if mode == "nt" else b_ref[:, cs]).astype(BF16)
            part = _dot(av, bv, dims)

            def finish(res, cs=cs):
                if r_ref is not None:
                    res = res + r_ref[:, cs].astype(F32)
                o_ref[:, cs] = res.astype(out_dtype)
                if h_ref is not None:
                    r = lax.rsqrt(jnp.mean(res * res, axis=-1, keepdims=True) + RMS_EPS)
                    h_ref[...] = (res * r * g_ref[...]).astype(BF16)

            if nk == 1:
                finish(part)
            else:
                kk = pl.program_id(2)

                @pl.when(kk == 0)
                def _(part=part, cs=cs):
                    acc_ref[:, cs] = part

                @pl.when(kk > 0)
                def _(part=part, cs=cs):
                    acc_ref[:, cs] += part

                @pl.when(kk == nk - 1)
                def _(finish=finish, cs=cs):
                    finish(acc_ref[:, cs])

    resident = dict(pipeline_mode=pl.Buffered(1)) if (n == tn and nk == 1 and mode != "tn" and m > tm) else {}
    if mode == "nn":
        a_spec = pl.BlockSpec((tm, tk), lambda i, j, kk: (i, kk))
        b_spec = pl.BlockSpec((tk, tn), lambda i, j, kk: (kk, j), **resident)
    elif mode == "nt":
        a_spec = pl.BlockSpec((tm, tk), lambda i, j, kk: (i, kk))
        b_spec = pl.BlockSpec((tn, tk), lambda i, j, kk: (j, kk), **resident)
    else:
        a_spec = pl.BlockSpec((tk, tm), lambda i, j, kk: (kk, i))
        b_spec = pl.BlockSpec((tk, tn), lambda i, j, kk: (kk, j + b_first))
    o_spec = pl.BlockSpec((tm, tn), lambda i, j, kk: (i, j))
    in_specs, args = [a_spec, b_spec], [a, b]
    if residual is not None:
        in_specs.append(o_spec)
        args.append(residual)
    out_shape, out_specs = [jax.ShapeDtypeStruct((m, n), out_dtype)], [o_spec]
    if col_blocks is not None:
        assert residual is None and rms_gain is None and (n // col_blocks) % tn == 0
        per = n // col_blocks // tn
        out_shape = [jax.ShapeDtypeStruct((col_blocks, m, n // col_blocks), out_dtype)]
        out_specs = [pl.BlockSpec((None, tm, tn), lambda i, j, kk: (j // per, i, j % per))]
    if rms_gain is not None:
        in_specs.append(pl.BlockSpec((1, n), lambda i, j, kk: (0, 0)))
        args.append(rms_gain)
        out_shape.append(jax.ShapeDtypeStruct((m, n), BF16))
        out_specs.append(o_spec)
    res = pl.pallas_call(
        body, name=name, grid=(m // tm, n // tn, nk), in_specs=in_specs, out_specs=out_specs, out_shape=out_shape,
        scratch_shapes=[pltpu.VMEM((tm, tn), F32)] if nk > 1 else [],
        compiler_params=_cparams(("parallel", "parallel", "arbitrary")),
    )(*args)
    return res if rms_gain is not None else res[0]


def _row_call(name, t, tm, rows_in, residents, rows_out, accs, body):
    n_in, n_res, n_out, n_acc = len(rows_in), len(residents), len(rows_out), len(accs)
    steps = t // tm
    assert t % tm == 0
    narrow = [i for i, (_, dt) in enumerate(accs) if dt != F32]

    def kernel_body(*refs):
        in_refs, res_refs = refs[:n_in], refs[n_in:n_in + n_res]
        out_refs = refs[n_in + n_res:n_in + n_res + n_out]
        acc_out = list(refs[n_in + n_res + n_out:n_in + n_res + n_out + n_acc])
        scratch = refs[n_in + n_res + n_out + n_acc:]
        acc_refs = list(acc_out)
        for s_ref, i in zip(scratch, narrow):
            acc_refs[i] = s_ref
        if accs:
            @pl.when(pl.program_id(0) == 0)
            def _():
                for acc in acc_refs:
                    acc[...] = jnp.zeros_like(acc)
        body(in_refs, res_refs, out_refs, acc_refs)
        if narrow:
            @pl.when(pl.program_id(0) == steps - 1)
            def _():
                for i in narrow:
                    acc_out[i][...] = acc_refs[i][...].astype(acc_out[i].dtype)

    once = dict(pipeline_mode=pl.Buffered(1)) if steps > 1 else {}
    in_specs = [pl.BlockSpec((tm, cols), lambda i, cb=cb: (i, cb)) for _, cols, cb in rows_in]
    in_specs += [pl.BlockSpec(r.shape, lambda i, nd=r.ndim: (0,) * nd, **once) for r in residents]
    out_specs = [pl.BlockSpec((tm, cols), lambda i, cb=cb: (i, cb)) for _, cols, cb, _ in rows_out]
    out_specs += [pl.BlockSpec(shape, lambda i, nd=len(shape): (0,) * nd) for shape, _ in accs]
    out_shape = [jax.ShapeDtypeStruct((t, total), dt) for total, _, _, dt in rows_out]
    out_shape += [jax.ShapeDtypeStruct(shape, dt) for shape, dt in accs]
    return pl.pallas_call(
        kernel_body, name=name, grid=(steps,), in_specs=in_specs, out_specs=out_specs, out_shape=out_shape,
        scratch_shapes=[pltpu.VMEM(accs[i][0], F32) for i in narrow],
        compiler_params=_cparams(("arbitrary",) if accs else ("parallel",)),
    )(*[a for a, _, _ in rows_in], *residents)


def _rms_apply(xv, gain):
    return xv * lax.rsqrt(jnp.mean(xv * xv, axis=-1, keepdims=True) + RMS_EPS) * gain


def _rms_grad(dres, dh, xv, gain):
    r = lax.rsqrt(jnp.mean(xv * xv, axis=-1, keepdims=True) + RMS_EPS)
    xhat = xv * r
    dxh = dh * gain
    dx = dres + r * (dxh - xhat * jnp.mean(dxh * xhat, axis=-1, keepdims=True))
    return dx, jnp.sum(dh * xhat, axis=0, keepdims=True)


def _in_proj(xf, gain, w_in, *, name):
    t, d = xf.shape
    n = w_in.shape[1]
    chunk = n // 4

    def body(ins, res, outs, accs):
        (x_ref,), (g_ref, w_ref), (h_ref, p_ref) = ins, res, outs
        h = _rms_apply(x_ref[...], g_ref[...]).astype(BF16)
        h_ref[...] = h
        for c0 in range(0, n, chunk):
            p_ref[:, c0:c0 + chunk] = _dot(h, w_ref[:, c0:c0 + chunk], _NN).astype(BF16)

    return _row_call(name, t, min(512, t), [(xf, d, 0)], [gain, w_in], [(d, d, 0, BF16), (n, n, 0, BF16)], [], body)


def _in_proj_gather(xf, gain, w_shard, *, name):
    t, d = xf.shape
    cb = w_shard.shape[1]
    tm = min(1024, t)
    steps = t // tm
    mx, my, _ = _mesh_pos()
    order = jnp.stack([2 * mx + my, 2 * (1 - mx) + my, 2 * mx + (1 - my), 2 * (1 - mx) + (1 - my)]).astype(jnp.int32)

    def body(order_ref, x_ref, g_ref, ws_ref, h_ref, p_ref, wout_ref, w_ref, send_sems, recv_sems, own_sem):
        ps, i = pl.program_id(0), pl.program_id(1)
        x, y, c = _mesh_pos()
        me, sib = (x, y, c), (x, y, 1 - c)
        chips = [(1 - x, y), (x, 1 - y), (1 - x, 1 - y)]

        def copy(k, block, to, from_shard=False):
            return pltpu.make_async_remote_copy(
                src_ref=ws_ref if from_shard else w_ref.at[_dev_index(block)], dst_ref=w_ref.at[_dev_index(block)],
                send_sem=send_sems.at[k], recv_sem=recv_sems.at[k], device_id=to, device_id_type=MESH_ID)

        own = pltpu.make_async_copy(ws_ref, w_ref.at[_dev_index(me)], own_sem)
        first = [copy(0, me, sib, True)] + [copy(1 + j, me, (*chip, c), True) for j, chip in enumerate(chips)]
        passed = [copy(4 + j, (*chip, c), sib) for j, chip in enumerate(chips)]

        @pl.when(jnp.logical_and(ps == 0, i == 0))
        def _():
            own.start()
            for cp in first:
                cp.start()
            own.wait()
            copy(0, sib, me).wait_recv()

        for j, chip in enumerate(chips):
            @pl.when(jnp.logical_and(ps == j + 1, i == 0))
            def _(j=j, chip=chip):
                copy(1 + j, (*chip, c), me).wait_recv()
                passed[j].start()
                copy(4 + j, (*chip, 1 - c), me).wait_recv()

        h = _rms_apply(x_ref[...], g_ref[...]).astype(BF16)
        h_ref[...] = h
        chip_id = order_ref[ps]
        p_ref[:, 0:cb] = _dot(h, w_ref[2 * chip_id], _NN).astype(BF16)
        p_ref[:, cb:2 * cb] = _dot(h, w_ref[2 * chip_id + 1], _NN).astype(BF16)

        @pl.when(jnp.logical_and(ps == 3, i == steps - 1))
        def _():
            for cp in first + passed:
                cp.wait_send()
            keep = pltpu.make_async_copy(w_ref, wout_ref, own_sem)
            keep.start()
            keep.wait()

    gs = pltpu.PrefetchScalarGridSpec(
        num_scalar_prefetch=1, grid=(4, steps),
        in_specs=[pl.BlockSpec((tm, d), lambda ps, i, o: (i, 0)), pl.BlockSpec((1, d), lambda ps, i, o: (0, 0)),
                  pl.BlockSpec(memory_space=pl.ANY)],
        out_specs=[pl.BlockSpec((tm, d), lambda ps, i, o: (jnp.where(ps == 0, i, steps - 1), 0)),
                   pl.BlockSpec((tm, 2 * cb), lambda ps, i, o: (i, o[ps])), pl.BlockSpec(memory_space=pl.ANY)],
        scratch_shapes=[pltpu.VMEM((N_DEV, d, cb), BF16), pltpu.SemaphoreType.DMA((7,)), pltpu.SemaphoreType.DMA((7,)),
                        pltpu.SemaphoreType.DMA(())])
    return pl.pallas_call(
        body, name=name, grid_spec=gs,
        out_shape=[jax.ShapeDtypeStruct((t, d), BF16), jax.ShapeDtypeStruct((t, N_DEV * cb), BF16),
                   jax.ShapeDtypeStruct((N_DEV, d, cb), BF16)],
        compiler_params=_cparams(("arbitrary", "arbitrary")))(order, xf, gain, w_shard)


def _mix_out(p, y_a, y_b, b_gate, xf, w_mix, gain, w_q, *, name):
    t, d = xf.shape

    def body(ins, res, outs, accs):
        ga_ref, gb_ref, ya_ref, yb_ref, x_ref = ins
        bg_ref, wm_ref, g_ref, wq_ref = res
        m_ref, x1_ref, h_ref, q_ref = outs
        sa = _sigmoid(ga_ref[...].astype(F32) + bg_ref[0:1, :])
        sb = _sigmoid(gb_ref[...].astype(F32) + bg_ref[1:2, :])
        merged = (sa * ya_ref[...].astype(F32) + sb * yb_ref[...].astype(F32)).astype(BF16)
        m_ref[...] = merged
        x1 = x_ref[...] + _dot(merged, wm_ref[...], _NN)
        x1_ref[...] = x1
        h = _rms_apply(x1, g_ref[...]).astype(BF16)
        h_ref[...] = h
        q_ref[...] = _dot(h, wq_ref[...], _NN).astype(BF16)

    return _row_call(name, t, min(512, t), [(p, d, 4), (p, d, 5), (y_a, d, 0), (y_b, d, 0), (xf, d, 0)],
                     [b_gate, w_mix, gain, w_q], [(d, d, 0, BF16), (d, d, 0, F32), (d, d, 0, BF16), (d, d, 0, BF16)], [], body)


def _ffn_fwd(h3, x2, target, w_gu_t, w_down, gain, *, name):
    t, d = x2.shape
    f2 = w_gu_t.shape[0]
    f = f2 // 2
    half = f // 2

    def body(ins, res, outs, accs):
        h_ref, x2_ref, t_ref = ins
        wgu_ref, wd_ref, g_ref = res
        gu_ref, act_ref, dx_ref = outs
        loss_ref, dg_ref = accs
        h = h_ref[...]
        x3 = x2_ref[...]
        for c0 in (0, half):
            gt = _dot(h, wgu_ref[c0:c0 + half, :], _NT).astype(BF16)
            up = _dot(h, wgu_ref[f + c0:f + c0 + half, :], _NT).astype(BF16)
            gu_ref[:, c0:c0 + half] = gt
            gu_ref[:, f + c0:f + c0 + half] = up
            gtf = gt.astype(F32)
            act = (gtf * _sigmoid(gtf) * up.astype(F32)).astype(BF16)
            act_ref[:, c0:c0 + half] = act
            x3 = x3 + _dot(act, wd_ref[c0:c0 + half, :], _NN)
        g = g_ref[...]
        r = lax.rsqrt(jnp.mean(x3 * x3, axis=-1, keepdims=True) + RMS_EPS)
        xhat = x3 * r
        err = xhat * g - t_ref[...]
        loss_ref[...] += 0.5 * jnp.sum(jnp.mean(err * err, axis=-1, keepdims=True), axis=0, keepdims=True)
        dy = err * (1.0 / d)
        dg_ref[...] += jnp.sum(dy * xhat, axis=0, keepdims=True)
        dxh = dy * g
        dx_ref[...] = r * (dxh - xhat * jnp.mean(dxh * xhat, axis=-1, keepdims=True))

    return _row_call(name, t, min(256, t), [(h3, d, 0), (x2, d, 0), (target, d, 0)], [w_gu_t, w_down, gain],
                     [(f2, f2, 0, BF16), (f, f, 0, BF16), (d, d, 0, F32)], [((1, 1), F32), ((1, d), F32)], body)


def _ffn_bwd(dx3, gu, x2, w_down, w_gu_t, gain, w_xo, *, name):
    t, d = x2.shape
    f2 = w_gu_t.shape[0]
    f = f2 // 2
    half = f // 2

    def body(ins, res, outs, accs):
        dx3_ref, gu_ref, x2_ref = ins
        wd_ref, wgu_ref, g_ref, wxo_ref = res
        dgu_ref, dx2_ref, do_ref = outs
        (dg_ref,) = accs
        dx3v = dx3_ref[...]
        dxb = dx3v.astype(BF16)
        dh = jnp.zeros(dx3v.shape, F32)
        for c0 in (0, half):
            dact = _dot(dxb, wd_ref[c0:c0 + half, :], _NT)
            gt = gu_ref[:, c0:c0 + half].astype(F32)
            up = gu_ref[:, f + c0:f + c0 + half].astype(F32)
            sg = _sigmoid(gt)
            dgt = (dact * up * sg * (1.0 + gt * (1.0 - sg))).astype(BF16)
            dup = (dact * gt * sg).astype(BF16)
            dgu_ref[:, c0:c0 + half] = dgt
            dgu_ref[:, f + c0:f + c0 + half] = dup
            dh = dh + _dot(dgt, wgu_ref[c0:c0 + half, :], _NN) + _dot(dup, wgu_ref[f + c0:f + c0 + half, :], _NN)
        dx2, dg = _rms_grad(dx3v, dh, x2_ref[...], g_ref[...])
        dx2_ref[...] = dx2
        dg_ref[...] += dg
        do_ref[...] = _dot(dx2.astype(BF16), wxo_ref[...], _NT).astype(BF16)

    return _row_call(name, t, min(256, t), [(dx3, d, 0), (gu, f2, 0), (x2, d, 0)], [w_down, w_gu_t, gain, w_xo],
                     [(f2, f2, 0, BF16), (d, d, 0, F32), (d, d, 0, BF16)], [((1, d), F32)], body)


def _proj_rms_bwd(dy, dres, x, w, gain, *, name, h=None):
    t, d = x.shape
    k = dy.shape[1]

    def body(ins, res, outs, accs):
        dy_ref, dres_ref, x_ref = ins[:3]
        w_ref, g_ref = res
        if h is not None:
            accs[1][...] += _dot(ins[3][...], dy_ref[...], _TN)
        if w.ndim == 3:
            cb = w.shape[2]
            dh = _dot(dy_ref[:, 0:cb], w_ref[0], _NT)
            for j in range(1, w.shape[0]):
                dh = dh + _dot(dy_ref[:, j * cb:(j + 1) * cb], w_ref[j], _NT)
        else:
            dh = _dot(dy_ref[...], w_ref[...], _NT)
        dx, dg = _rms_grad(dres_ref[...], dh, x_ref[...], g_ref[...])
        outs[0][...] = dx
        accs[0][...] += dg

    rows_in = [(dy, k, 0), (dres, d, 0), (x, d, 0)] + ([(h, d, 0)] if h is not None else [])
    accs = [((1, d), F32)] + ([((d, k), BF16)] if h is not None else [])
    return _row_call(name, t, min(512, t), rows_in, [w, gain], [(d, d, 0, F32)], accs, body)


def _gates_bwd_fused(dx1, p, y_a, y_b, b_gate, w_mix, merged, h1, *, name):
    t, d = y_a.shape

    def body(ins, res, outs, accs):
        dx_ref, ga_ref, gb_ref, ya_ref, yb_ref, m_ref, h1_ref = ins
        bg_ref, wm_ref = res
        dp_ref, dya_ref, dyb_ref = outs
        dbg_ref, dwm_ref, dwin_ref = accs
        dxb = dx_ref[...].astype(BF16)
        dwm_ref[...] += _dot(m_ref[...], dxb, _TN)
        dm = _dot(dxb, wm_ref[...], _NT)
        sa = _sigmoid(ga_ref[...].astype(F32) + bg_ref[0:1, :])
        sb = _sigmoid(gb_ref[...].astype(F32) + bg_ref[1:2, :])
        dya_ref[...] = (dm * sa).astype(BF16)
        dyb_ref[...] = (dm * sb).astype(BF16)
        dga = dm * ya_ref[...].astype(F32) * sa * (1.0 - sa)
        dgb = dm * yb_ref[...].astype(F32) * sb * (1.0 - sb)
        dp_ref[:, 0:d] = dga.astype(BF16)
        dp_ref[:, d:2 * d] = dgb.astype(BF16)
        dbg_ref[0:1, :] += jnp.sum(dga, axis=0, keepdims=True)
        dbg_ref[1:2, :] += jnp.sum(dgb, axis=0, keepdims=True)
        dwin_ref[...] += _dot(h1_ref[...], dp_ref[...], _TN)

    return _row_call(name, t, min(256, t),
                     [(dx1, d, 0), (p, d, 4), (p, d, 5), (y_a, d, 0), (y_b, d, 0), (merged, d, 0), (h1, d, 0)],
                     [b_gate, w_mix], [(p.shape[1], 2 * d, 2, BF16), (d, d, 0, BF16), (d, d, 0, BF16)],
                     [((8, d), F32), ((d, d), BF16), ((d, 2 * d), BF16)], body)


def _conv_ln_bwd_fused(dy_a, c, a_act, w_conv_out, ln_g, ln_b, *, name):
    t, d = c.shape

    def body(ins, res, outs, accs):
        dy_ref, c_ref, act_ref = ins
        w_ref, lg_ref, lb_ref = res
        dlg_ref, dlb_ref, dw_ref = accs
        dw_ref[...] += _dot(act_ref[...], dy_ref[...], _TN)
        dact = _dot(dy_ref[...], w_ref[...], _NT)
        cv = c_ref[...].astype(F32)
        g = lg_ref[...]
        mu = jnp.mean(cv, axis=-1, keepdims=True)
        dv = cv - mu
        rstd = lax.rsqrt(jnp.mean(dv * dv, axis=-1, keepdims=True) + LN_EPS)
        chat = dv * rstd
        aln = chat * g + lb_ref[...]
        sg = _sigmoid(aln)
        daln = dact * (sg * (1.0 + aln * (1.0 - sg)))
        dlb_ref[...] += jnp.sum(daln, axis=0, keepdims=True)
        dlg_ref[...] += jnp.sum(daln * chat, axis=0, keepdims=True)
        dchat = daln * g
        dc = rstd * (dchat - jnp.mean(dchat, axis=-1, keepdims=True)
                     - chat * jnp.mean(dchat * chat, axis=-1, keepdims=True))
        outs[0][...] = dc.astype(BF16)

    return _row_call(name, t, min(512, t), [(dy_a, d, 0), (c, d, 0), (a_act, d, 0)], [w_conv_out, ln_g, ln_b],
                     [(d, d, 0, BF16)], [((1, d), F32), ((1, d), F32), ((d, d), BF16)], body)


def _row_spec(tt, cols, col_block=0):
    return pl.BlockSpec((tt, cols), lambda i: (i, col_block))


def _const_spec(shape):
    return pl.BlockSpec(shape, lambda *_: (0,) * len(shape))


def _rms_fwd(x, gain, *, name):
    t, d = x.shape
    tt = min(TOKEN_TILE, t)

    def body(x_ref, g_ref, h_ref):
        xv = x_ref[...]
        r = lax.rsqrt(jnp.mean(xv * xv, axis=-1, keepdims=True) + RMS_EPS)
        h_ref[...] = (xv * r * g_ref[...]).astype(BF16)

    return pl.pallas_call(
        body, name=name, grid=(t // tt,), in_specs=[_row_spec(tt, d), _const_spec((1, d))],
        out_specs=_row_spec(tt, d), out_shape=jax.ShapeDtypeStruct((t, d), BF16),
        compiler_params=_cparams(("parallel",)))(x, gain)


def _rms_bwd(dres, dh, x, gain, *, name, need_dx=True):
    t, d = x.shape
    tt = min(TOKEN_TILE, t)

    def body(*refs):
        if need_dx:
            dres_ref, dh_ref, x_ref, g_ref, dx_ref, dg_ref = refs
        else:
            dh_ref, x_ref, g_ref, dg_ref = refs

        @pl.when(pl.program_id(0) == 0)
        def _():
            dg_ref[...] = jnp.zeros_like(dg_ref)

        xv = x_ref[...]
        dhv = dh_ref[...].astype(F32)
        r = lax.rsqrt(jnp.mean(xv * xv, axis=-1, keepdims=True) + RMS_EPS)
        xhat = xv * r
        dg_ref[...] += jnp.sum(dhv * xhat, axis=0, keepdims=True)
        if need_dx:
            dxh = dhv * g_ref[...]
            dx_ref[...] = dres_ref[...] + r * (dxh - xhat * jnp.mean(dxh * xhat, axis=-1, keepdims=True))

    rs = _row_spec(tt, d)
    if need_dx:
        in_specs, args = [rs, rs, rs, _const_spec((1, d))], (dres, dh, x, gain)
        out_specs = [rs, _const_spec((1, d))]
        out_shape = [jax.ShapeDtypeStruct((t, d), F32), jax.ShapeDtypeStruct((1, d), F32)]
    else:
        in_specs, args = [rs, rs, _const_spec((1, d))], (dh, x, gain)
        out_specs = [_const_spec((1, d))]
        out_shape = [jax.ShapeDtypeStruct((1, d), F32)]
    res = pl.pallas_call(body, name=name, grid=(t // tt,), in_specs=in_specs, out_specs=out_specs, out_shape=out_shape,
                         compiler_params=_cparams(("arbitrary",)))(*args)
    return res if need_dx else res[0]


def _final_loss(x3, target, gain, *, name):
    t, d = x3.shape
    tt = min(TOKEN_TILE, t)

    def body(x_ref, t_ref, g_ref, loss_ref, dx_ref, dg_ref):
        @pl.when(pl.program_id(0) == 0)
        def _():
            loss_ref[...] = jnp.zeros_like(loss_ref)
            dg_ref[...] = jnp.zeros_like(dg_ref)

        xv = x_ref[...]
        g = g_ref[...]
        r = lax.rsqrt(jnp.mean(xv * xv, axis=-1, keepdims=True) + RMS_EPS)
        xhat = xv * r
        err = xhat * g - t_ref[...]
        loss_ref[...] += 0.5 * jnp.sum(jnp.mean(err * err, axis=-1, keepdims=True), axis=0, keepdims=True)
        dy = err * (1.0 / d)
        dg_ref[...] += jnp.sum(dy * xhat, axis=0, keepdims=True)
        dxh = dy * g
        dx_ref[...] = r * (dxh - xhat * jnp.mean(dxh * xhat, axis=-1, keepdims=True))

    rs = _row_spec(tt, d)
    return pl.pallas_call(
        body, name=name, grid=(t // tt,), in_specs=[rs, rs, _const_spec((1, d))],
        out_specs=[_const_spec((1, 1)), rs, _const_spec((1, d))],
        out_shape=[jax.ShapeDtypeStruct((1, 1), F32), jax.ShapeDtypeStruct((t, d), F32), jax.ShapeDtypeStruct((1, d), F32)],
        compiler_params=_cparams(("arbitrary",)))(x3, target, gain)


SUBLANES = 8
SHIFT_ROWS = 40


def _conv_apply(sbuf_ref, w_ref, out_ref, tt, offsets, bias_ref=None):
    d = out_ref.shape[1]
    for cc in range(d // LANES):
        cs = slice(cc * LANES, (cc + 1) * LANES)
        taps = [jnp.broadcast_to(w_ref[k:k + 1, cs], (SUBLANES, LANES)) for k in range(CONV_WIDTH)]
        bias = None if bias_ref is None else jnp.broadcast_to(bias_ref[:, cs], (SUBLANES, LANES))

        def row_body(r, carry, cs=cs, taps=taps, bias=bias):
            r0 = pl.multiple_of(r * CONV_ROWS, CONV_ROWS)
            for q in range(CONV_ROWS // SUBLANES):
                acc = _tap(sbuf_ref, r0 + q * SUBLANES, cs, offsets[0]) * taps[0]
                for k in range(1, CONV_WIDTH):
                    acc = acc + _tap(sbuf_ref, r0 + q * SUBLANES, cs, offsets[k]) * taps[k]
                if bias is not None:
                    acc = acc + bias
                out_ref[pl.ds(r0 + q * SUBLANES, SUBLANES), cs] = acc
            return carry

        lax.fori_loop(0, tt // CONV_ROWS, row_body, 0)


def _fill_shifts(sbuf_ref, rows):
    d = sbuf_ref.shape[2]
    assert rows % SHIFT_ROWS == 0

    def row_body(i, carry):
        r0 = pl.multiple_of(i * SHIFT_ROWS, SUBLANES)
        for cc in range(d // CONV_COLS):
            cs = slice(cc * CONV_COLS, (cc + 1) * CONV_COLS)
            win = sbuf_ref[0, pl.ds(r0, SHIFT_ROWS + SUBLANES), cs]
            for sh in range(1, SUBLANES):
                sbuf_ref[sh, pl.ds(r0, SHIFT_ROWS), cs] = win[sh:sh + SHIFT_ROWS, :]
        return carry

    lax.fori_loop(0, rows // SHIFT_ROWS, row_body, 0)


def _tap(sbuf_ref, r0, cs, offset):
    sh = offset % SUBLANES
    return sbuf_ref[sh, pl.ds(pl.multiple_of(r0 + (offset - sh), SUBLANES), SUBLANES), cs]


def _conv_specs(bl, s, tt, d, col_a, col_g):
    nj = s // tt
    per = tt // CONV_HALO
    main_a = pl.BlockSpec((tt, d), lambda b, j: (b * nj + j, col_a))
    main_g = pl.BlockSpec((tt, d), lambda b, j: (b * nj + j, col_g))
    prev = lambda b, j: jnp.maximum((b * nj + j) * per - 1, 0)
    halo_a = pl.BlockSpec((CONV_HALO, d), lambda b, j: (prev(b, j), col_a))
    halo_g = pl.BlockSpec((CONV_HALO, d), lambda b, j: (prev(b, j), col_g))
    return main_a, main_g, halo_a, halo_g


def _fill_glu(sbuf_ref, a_ref, g_ref, ha_ref, hg_ref, tt):
    first = pl.program_id(1) == 0
    ha = ha_ref[...].astype(F32)
    hg = hg_ref[...].astype(F32)
    sbuf_ref[0, pl.ds(0, CONV_HALO), :] = jnp.where(first, 0.0, ha * _sigmoid(hg))
    av = a_ref[...].astype(F32)
    gv = g_ref[...].astype(F32)
    sbuf_ref[0, pl.ds(CONV_HALO, tt), :] = av * _sigmoid(gv)
    _fill_shifts(sbuf_ref, tt + CONV_HALO - SUBLANES)


def _conv_fwd(p, conv_w, conv_b, ln_g, ln_b, *, bl, s, name):
    t = p.shape[0]
    d = conv_w.shape[1]
    tt = min(TOKEN_TILE, s)
    off = CONV_HALO - (CONV_WIDTH - 1)

    def body(a_ref, g_ref, ha_ref, hg_ref, w_ref, b_ref, lg_ref, lb_ref, c_ref, act_ref, sbuf_ref, cbuf_ref):
        _fill_glu(sbuf_ref, a_ref, g_ref, ha_ref, hg_ref, tt)

        _conv_apply(sbuf_ref, w_ref, cbuf_ref, tt, [off + k for k in range(CONV_WIDTH)], bias_ref=b_ref)
        cv = cbuf_ref[...]
        c_ref[...] = cv.astype(BF16)
        mu = jnp.mean(cv, axis=-1, keepdims=True)
        dv = cv - mu
        rstd = lax.rsqrt(jnp.mean(dv * dv, axis=-1, keepdims=True) + LN_EPS)
        aln = dv * rstd * lg_ref[...] + lb_ref[...]
        act_ref[...] = (aln * _sigmoid(aln)).astype(BF16)

    main_a, main_g, halo_a, halo_g = _conv_specs(bl, s, tt, d, 0, 1)
    out_spec = pl.BlockSpec((tt, d), lambda b, j: (b * (s // tt) + j, 0))
    return pl.pallas_call(
        body, name=name, grid=(bl, s // tt),
        in_specs=[main_a, main_g, halo_a, halo_g, _const_spec((CONV_HALO, d)), _const_spec((1, d)), _const_spec((1, d)),
                  _const_spec((1, d))],
        out_specs=[out_spec, out_spec],
        out_shape=[jax.ShapeDtypeStruct((t, d), BF16), jax.ShapeDtypeStruct((t, d), BF16)],
        scratch_shapes=[pltpu.VMEM((SUBLANES, tt + CONV_HALO, d), F32), pltpu.VMEM((tt, d), F32)],
        compiler_params=_cparams(("parallel", "parallel")))(p, p, p, p, conv_w, conv_b, ln_g, ln_b)


def _conv_ln_bwd(dact, c, ln_g, ln_b, *, name):
    t, d = c.shape
    tt = min(TOKEN_TILE, t)

    def body(da_ref, c_ref, lg_ref, lb_ref, dc_ref, dlg_ref, dlb_ref):
        @pl.when(pl.program_id(0) == 0)
        def _():
            dlg_ref[...] = jnp.zeros_like(dlg_ref)
            dlb_ref[...] = jnp.zeros_like(dlb_ref)

        cv = c_ref[...].astype(F32)
        g = lg_ref[...]
        mu = jnp.mean(cv, axis=-1, keepdims=True)
        dv = cv - mu
        rstd = lax.rsqrt(jnp.mean(dv * dv, axis=-1, keepdims=True) + LN_EPS)
        chat = dv * rstd
        aln = chat * g + lb_ref[...]
        sg = _sigmoid(aln)
        daln = da_ref[...].astype(F32) * (sg * (1.0 + aln * (1.0 - sg)))
        dlb_ref[...] += jnp.sum(daln, axis=0, keepdims=True)
        dlg_ref[...] += jnp.sum(daln * chat, axis=0, keepdims=True)
        dchat = daln * g
        dc = rstd * (dchat - jnp.mean(dchat, axis=-1, keepdims=True)
                     - chat * jnp.mean(dchat * chat, axis=-1, keepdims=True))
        dc_ref[...] = dc.astype(BF16)

    rs = _row_spec(tt, d)
    cs = _const_spec((1, d))
    return pl.pallas_call(
        body, name=name, grid=(t // tt,), in_specs=[rs, rs, cs, cs], out_specs=[rs, cs, cs],
        out_shape=[jax.ShapeDtypeStruct((t, d), BF16), jax.ShapeDtypeStruct((1, d), F32), jax.ShapeDtypeStruct((1, d), F32)],
        compiler_params=_cparams(("arbitrary",)))(dact, c, ln_g, ln_b)


def _conv_bwd(dp, dc, p, conv_w, w_in, dx1, xf, gain, *, bl, s, name):
    t = p.shape[0]
    d = conv_w.shape[1]
    tt = min(TOKEN_TILE, s)
    nj = s // tt
    per = tt // CONV_HALO
    off = CONV_HALO - (CONV_WIDTH - 1)
    last_blk = t // CONV_HALO - 1

    def body(dc_ref, dcn_ref, a_ref, g_ref, ha_ref, hg_ref, w_ref, dps_ref, dpg_ref, win_ref, dx1_ref, x_ref, gain_ref,
             dp_ref, dw_ref, db_ref, gx_ref, dgain_ref, gbuf_ref, dbuf_ref, dglu_ref, acc_ref):
        b, j = pl.program_id(0), pl.program_id(1)
        start = jnp.logical_and(b == 0, j == 0)
        end = jnp.logical_and(b == bl - 1, j == nj - 1)

        @pl.when(start)
        def _():
            acc_ref[...] = jnp.zeros_like(acc_ref)
            db_ref[...] = jnp.zeros_like(db_ref)
            dgain_ref[...] = jnp.zeros_like(dgain_ref)

        _fill_glu(gbuf_ref, a_ref, g_ref, ha_ref, hg_ref, tt)
        dcv = dc_ref[...].astype(F32)
        dbuf_ref[0, pl.ds(0, tt), :] = dcv
        dbuf_ref[0, pl.ds(tt, CONV_HALO), :] = jnp.where(j == nj - 1, 0.0, dcn_ref[...].astype(F32))
        _fill_shifts(dbuf_ref, tt + CONV_HALO - SUBLANES)
        db_ref[...] += jnp.sum(dcv, axis=0, keepdims=True)

        for cc in range(d // LANES):
            cs = slice(cc * LANES, (cc + 1) * LANES)

            def row_body(r, accs, cs=cs):
                r0 = pl.multiple_of(r * CONV_ROWS, CONV_ROWS)
                accs = list(accs)
                for q in range(CONV_ROWS // SUBLANES):
                    dcw = dbuf_ref[0, pl.ds(r0 + q * SUBLANES, SUBLANES), cs]
                    for k in range(CONV_WIDTH):
                        accs[k] = accs[k] + dcw * _tap(gbuf_ref, r0 + q * SUBLANES, cs, off + k)
                return tuple(accs)

            zero = jnp.zeros((SUBLANES, LANES), F32)
            accs = lax.fori_loop(0, tt // CONV_ROWS, row_body, (zero,) * CONV_WIDTH)
            for k in range(CONV_WIDTH):
                acc_ref[k, :, cs] += accs[k]

        _conv_apply(dbuf_ref, w_ref, dglu_ref, tt, [CONV_WIDTH - 1 - k for k in range(CONV_WIDTH)])
        dglu = dglu_ref[...]
        av = a_ref[...].astype(F32)
        sg = _sigmoid(g_ref[...].astype(F32))
        dp_ref[:, 0:d] = (dglu * sg).astype(BF16)
        dp_ref[:, d:2 * d] = (dglu * av * sg * (1.0 - sg)).astype(BF16)
        dh = (_dot(dp_ref[...], win_ref[:, 0:2 * d], _NT) + _dot(dps_ref[...], win_ref[:, 2 * d:4 * d], _NT)
              + _dot(dpg_ref[...], win_ref[:, 4 * d:6 * d], _NT))
        gx, dg = _rms_grad(dx1_ref[...], dh, x_ref[...], gain_ref[...])
        gx_ref[...] = gx
        dgain_ref[...] += dg

        @pl.when(end)
        def _():
            for k in range(CONV_WIDTH):
                dw_ref[k:k + 1, :] = jnp.sum(acc_ref[k], axis=0, keepdims=True)
            dw_ref[CONV_WIDTH:CONV_HALO, :] = jnp.zeros((CONV_HALO - CONV_WIDTH, d), F32)

    main_a, main_g, halo_a, halo_g = _conv_specs(bl, s, tt, d, 0, 1)
    dc_main = pl.BlockSpec((tt, d), lambda b, j: (b * nj + j, 0))
    dc_next = pl.BlockSpec((CONV_HALO, d), lambda b, j: (jnp.minimum((b * nj + j + 1) * per, last_blk), 0))
    wide = lambda col: pl.BlockSpec((tt, 2 * d), lambda b, j: (b * nj + j, col))
    return pl.pallas_call(
        body, name=name, grid=(bl, nj),
        in_specs=[dc_main, dc_next, main_a, main_g, halo_a, halo_g, _const_spec((CONV_HALO, d)), wide(1), wide(2),
                  pl.BlockSpec(w_in.shape, lambda b, j: (0, 0), pipeline_mode=pl.Buffered(1)), dc_main, dc_main,
                  _const_spec((1, d))],
        out_specs=[wide(0), _const_spec((CONV_HALO, d)), _const_spec((1, d)), dc_main, _const_spec((1, d))],
        out_shape=[jax.ShapeDtypeStruct((t, 2 * d), BF16), jax.ShapeDtypeStruct((CONV_HALO, d), F32),
                   jax.ShapeDtypeStruct((1, d), F32), jax.ShapeDtypeStruct((t, d), F32), jax.ShapeDtypeStruct((1, d), F32)],
        scratch_shapes=[pltpu.VMEM((SUBLANES, tt + CONV_HALO, d), F32), pltpu.VMEM((SUBLANES, tt + CONV_HALO, d), F32),
                        pltpu.VMEM((tt, d), F32), pltpu.VMEM((CONV_HALO, SUBLANES, d), F32)],
        compiler_params=_cparams(("arbitrary", "arbitrary")))(dc, dc, p, p, p, p, conv_w, dp, dp, w_in, dx1, xf, gain)


def _sgu_stats(bv):
    gv = _gelu(bv)
    mu = jnp.mean(gv, axis=-1, keepdims=True)
    dv = gv - mu
    rstd = lax.rsqrt(jnp.mean(dv * dv, axis=-1, keepdims=True) + LN_EPS)
    return dv * rstd, rstd


def _sgu_fwd(p, wm, bias, ln_g, ln_b, *, name):
    t = p.shape[0]
    d = ln_g.shape[1]
    tt = SGU_CHUNK
    gd = d // SGU_GROUPS

    def body(u_ref, v_ref, wm_ref, bias_ref, lg_ref, lb_ref, sg_ref, vn_ref):
        u = _gelu(u_ref[...].astype(F32))
        vhat, _ = _sgu_stats(v_ref[...].astype(F32))
        vb = (vhat * lg_ref[...] + lb_ref[...]).astype(BF16)
        vn_ref[...] = vb
        for g in range(SGU_GROUPS):
            gs = slice(g * gd, (g + 1) * gd)
            z = _dot(wm_ref[g], vb[:, gs], _NN) + bias_ref[g]
            sg_ref[:, gs] = (u[:, gs] * z).astype(BF16)

    rs = _row_spec(tt, d)
    return pl.pallas_call(
        body, name=name, grid=(t // tt,),
        in_specs=[_row_spec(tt, d, 2), _row_spec(tt, d, 3), _const_spec(wm.shape), _const_spec(bias.shape),
                  _const_spec((1, d)), _const_spec((1, d))],
        out_specs=[rs, rs], out_shape=[jax.ShapeDtypeStruct((t, d), BF16), jax.ShapeDtypeStruct((t, d), BF16)],
        compiler_params=_cparams(("parallel",)))(p, p, wm, bias, ln_g, ln_b)


def _sgu_bwd(dp, dy_b, w_out, p, vn, wm, wmt, bias, ln_g, *, name):
    t = p.shape[0]
    d = ln_g.shape[1]
    tt = SGU_CHUNK
    gd = d // SGU_GROUPS
    nsteps = t // tt

    def body(dp_in, dyb_ref, wout_ref, u_ref, v_ref, vn_ref, wm_ref, wmt_ref, bias_ref, lg_ref,
             dp_ref, dw_ref, dbs_ref, dlg_ref, dlb_ref, dz_acc):
        del dp_in
        i = pl.program_id(0)

        @pl.when(i == 0)
        def _():
            dw_ref[...] = jnp.zeros_like(dw_ref)
            dlg_ref[...] = jnp.zeros_like(dlg_ref)
            dlb_ref[...] = jnp.zeros_like(dlb_ref)
            dz_acc[...] = jnp.zeros_like(dz_acc)

        bu = u_ref[...].astype(F32)
        bv = v_ref[...].astype(F32)
        u = _gelu(bu)
        vhat, rstd = _sgu_stats(bv)
        vb = vn_ref[...]
        dsg = _dot(dyb_ref[...], wout_ref[...], _NT)
        row = lax.broadcasted_iota(jnp.int32, (tt, tt), 0)
        col = lax.broadcasted_iota(jnp.int32, (tt, tt), 1)
        causal = col <= row
        du_parts, dv_parts = [], []
        for g in range(SGU_GROUPS):
            gs = slice(g * gd, (g + 1) * gd)
            z = _dot(wm_ref[g], vb[:, gs], _NN) + bias_ref[g]
            du_parts.append(dsg[:, gs] * z)
            dz = dsg[:, gs] * u[:, gs]
            dz_acc[:, gs] += dz
            dzb = dz.astype(BF16)
            dw_ref[g] += jnp.where(causal, _dot(dzb, vb[:, gs], _NT), 0.0)
            dv_parts.append(_dot(wmt_ref[g], dzb, _NN))
        du = jnp.concatenate(du_parts, axis=1)
        dv = jnp.concatenate(dv_parts, axis=1)
        dp_ref[:, 0:d] = (du * _gelu_grad(bu)).astype(BF16)
        dlb_ref[...] += jnp.sum(dv, axis=0, keepdims=True)
        dlg_ref[...] += jnp.sum(dv * vhat, axis=0, keepdims=True)
        dvh = dv * lg_ref[...]
        dgv = rstd * (dvh - jnp.mean(dvh, axis=-1, keepdims=True) - vhat * jnp.mean(dvh * vhat, axis=-1, keepdims=True))
        dp_ref[:, d:2 * d] = (dgv * _gelu_grad(bv)).astype(BF16)

        @pl.when(i == nsteps - 1)
        def _():
            ones = jnp.ones((8, gd), F32)
            for g in range(SGU_GROUPS):
                gs = slice(g * gd, (g + 1) * gd)
                tot = lax.dot_general(ones, dz_acc[:, gs], (_NT, ((), ())), preferred_element_type=F32,
                                      precision=lax.Precision.HIGHEST)
                dbs_ref[g:g + 1, :] = tot[0:1, :]

    rs = _row_spec(tt, d)
    c1 = _const_spec((1, d))
    return pl.pallas_call(
        body, name=name, grid=(nsteps,),
        in_specs=[pl.BlockSpec(memory_space=pl.ANY), rs, _const_spec(w_out.shape), _row_spec(tt, d, 2), _row_spec(tt, d, 3),
                  rs, _const_spec(wm.shape), _const_spec(wmt.shape), _const_spec(bias.shape), c1],
        out_specs=[pl.BlockSpec((tt, 2 * d), lambda i: (i, 1)), _const_spec(wm.shape), _const_spec((SGU_GROUPS, tt)), c1, c1],
        out_shape=[jax.ShapeDtypeStruct(dp.shape, dp.dtype), jax.ShapeDtypeStruct(wm.shape, F32),
                   jax.ShapeDtypeStruct((SGU_GROUPS, tt), F32), jax.ShapeDtypeStruct((1, d), F32),
                   jax.ShapeDtypeStruct((1, d), F32)],
        scratch_shapes=[pltpu.VMEM((tt, d), F32)],
        input_output_aliases={0: 0},
        compiler_params=_cparams(("arbitrary",)))(dp, dy_b, w_out, p, p, vn, wm, wmt, bias, ln_g)


def _gates_fwd(p, ya, yb, b_gate, *, name):
    t, d = ya.shape
    tt = min(TOKEN_TILE, t)

    def body(ga_ref, gb_ref, ya_ref, yb_ref, bg_ref, o_ref):
        sa = _sigmoid(ga_ref[...].astype(F32) + bg_ref[0:1, :])
        sb = _sigmoid(gb_ref[...].astype(F32) + bg_ref[1:2, :])
        o_ref[...] = (sa * ya_ref[...].astype(F32) + sb * yb_ref[...].astype(F32)).astype(BF16)

    rs = _row_spec(tt, d)
    return pl.pallas_call(
        body, name=name, grid=(t // tt,),
        in_specs=[_row_spec(tt, d, 4), _row_spec(tt, d, 5), rs, rs, _const_spec(b_gate.shape)],
        out_specs=rs, out_shape=jax.ShapeDtypeStruct((t, d), BF16),
        compiler_params=_cparams(("parallel",)))(p, p, ya, yb, b_gate)


def _gates_bwd(dmerged, p, ya, yb, b_gate, *, name):
    t, d = ya.shape
    tt = min(TOKEN_TILE, t)

    def body(dm_ref, ga_ref, gb_ref, ya_ref, yb_ref, bg_ref, dp_ref, dya_ref, dyb_ref, dbg_ref):
        @pl.when(pl.program_id(0) == 0)
        def _():
            dbg_ref[...] = jnp.zeros_like(dbg_ref)

        dm = dm_ref[...].astype(F32)
        sa = _sigmoid(ga_ref[...].astype(F32) + bg_ref[0:1, :])
        sb = _sigmoid(gb_ref[...].astype(F32) + bg_ref[1:2, :])
        dya_ref[...] = (dm * sa).astype(BF16)
        dyb_ref[...] = (dm * sb).astype(BF16)
        dga = dm * ya_ref[...].astype(F32) * sa * (1.0 - sa)
        dgb = dm * yb_ref[...].astype(F32) * sb * (1.0 - sb)
        dp_ref[:, 0:d] = dga.astype(BF16)
        dp_ref[:, d:2 * d] = dgb.astype(BF16)
        dbg_ref[0:1, :] += jnp.sum(dga, axis=0, keepdims=True)
        dbg_ref[1:2, :] += jnp.sum(dgb, axis=0, keepdims=True)

    rs = _row_spec(tt, d)
    return pl.pallas_call(
        body, name=name, grid=(t // tt,),
        in_specs=[rs, _row_spec(tt, d, 4), _row_spec(tt, d, 5), rs, rs, _const_spec(b_gate.shape)],
        out_specs=[pl.BlockSpec((tt, 2 * d), lambda i: (i, 2)), rs, rs, _const_spec((8, d))],
        out_shape=[jax.ShapeDtypeStruct(p.shape, BF16), jax.ShapeDtypeStruct((t, d), BF16),
                   jax.ShapeDtypeStruct((t, d), BF16), jax.ShapeDtypeStruct((8, d), F32)],
        compiler_params=_cparams(("arbitrary",)))(dmerged, p, p, ya, yb, b_gate)


def _softmax_rows(s):
    e = jnp.exp(s - jnp.max(s, axis=-1, keepdims=True))
    return e / jnp.sum(e, axis=-1, keepdims=True)


def _attn_fwd(q, kv, x1, w_xo, gain, *, bl, s, name):
    t, d = q.shape
    mlen = kv.shape[0] // bl
    hd = d // HEADS
    tq = min(ATTN_TILE, s)
    nq = s // tq
    scale = hd ** -0.5

    def body(q_ref, kv_ref, x1_ref, w_ref, g_ref, o_ref, x2_ref, h_ref):
        for h in range(HEADS):
            hs = slice(h * hd, (h + 1) * hd)
            vs = slice(d + h * hd, d + (h + 1) * hd)
            pr = _softmax_rows(_dot(q_ref[:, hs], kv_ref[:, hs], _NT) * scale)
            o_ref[:, hs] = _dot(pr.astype(BF16), kv_ref[:, vs], _NN).astype(BF16)
        x2 = x1_ref[...] + _dot(o_ref[...], w_ref[...], _NN)
        x2_ref[...] = x2
        h_ref[...] = _rms_apply(x2, g_ref[...]).astype(BF16)

    qs = pl.BlockSpec((tq, d), lambda b, j: (b * nq + j, 0))
    return pl.pallas_call(
        body, name=name, grid=(bl, nq),
        in_specs=[qs, pl.BlockSpec((mlen, 2 * d), lambda b, j: (b, 0)), qs, _const_spec(w_xo.shape), _const_spec((1, d))],
        out_specs=[qs, qs, qs],
        out_shape=[jax.ShapeDtypeStruct((t, d), BF16), jax.ShapeDtypeStruct((t, d), F32), jax.ShapeDtypeStruct((t, d), BF16)],
        compiler_params=_cparams(("parallel", "parallel")))(q, kv, x1, w_xo, gain)


def _attn_bwd(q, kv, do, *, bl, s, name):
    t, d = q.shape
    mlen = kv.shape[0] // bl
    hd = d // HEADS
    tq = min(ATTN_TILE, s)
    nq = s // tq
    scale = hd ** -0.5

    def body(q_ref, kv_ref, do_ref, dq_ref, dkv_ref):
        @pl.when(pl.program_id(1) == 0)
        def _():
            dkv_ref[...] = jnp.zeros_like(dkv_ref)

        for h in range(HEADS):
            hs = slice(h * hd, (h + 1) * hd)
            vs = slice(d + h * hd, d + (h + 1) * hd)
            qh, kh, vh, doh = q_ref[:, hs], kv_ref[:, hs], kv_ref[:, vs], do_ref[:, hs]
            pr = _softmax_rows(_dot(qh, kh, _NT) * scale)
            dpr = _dot(doh, vh, _NT)
            dkv_ref[:, vs] += _dot(pr.astype(BF16), doh, _TN)
            ds = (pr * (dpr - jnp.sum(dpr * pr, axis=-1, keepdims=True)) * scale).astype(BF16)
            dq_ref[:, hs] = _dot(ds, kh, _NN).astype(BF16)
            dkv_ref[:, hs] += _dot(ds, qh, _TN)

    qs = pl.BlockSpec((tq, d), lambda b, j: (b * nq + j, 0))
    ks = pl.BlockSpec((mlen, 2 * d), lambda b, j: (b, 0))
    return pl.pallas_call(
        body, name=name, grid=(bl, nq), in_specs=[qs, ks, qs], out_specs=[qs, ks],
        out_shape=[jax.ShapeDtypeStruct((t, d), BF16), jax.ShapeDtypeStruct(kv.shape, F32)],
        compiler_params=_cparams(("parallel", "arbitrary")))(q, kv, do)


def _swiglu_fwd(gu, *, name):
    t, f2 = gu.shape
    f = f2 // 2
    tt = min(TOKEN_TILE, t)

    def body(gu_ref, o_ref):
        gt = gu_ref[:, 0:f].astype(F32)
        up = gu_ref[:, f:f2].astype(F32)
        o_ref[...] = (gt * _sigmoid(gt) * up).astype(BF16)

    return pl.pallas_call(
        body, name=name, grid=(t // tt,), in_specs=[_row_spec(tt, f2)], out_specs=_row_spec(tt, f),
        out_shape=jax.ShapeDtypeStruct((t, f), BF16), compiler_params=_cparams(("parallel",)))(gu)


def _swiglu_bwd(gu, dact, *, name):
    t, f2 = gu.shape
    f = f2 // 2
    tt = min(TOKEN_TILE, t)

    def body(gu_ref, da_ref, o_ref):
        gt = gu_ref[:, 0:f].astype(F32)
        up = gu_ref[:, f:f2].astype(F32)
        da = da_ref[...].astype(F32)
        sg = _sigmoid(gt)
        o_ref[:, 0:f] = (da * up * sg * (1.0 + gt * (1.0 - sg))).astype(BF16)
        o_ref[:, f:f2] = (da * gt * sg).astype(BF16)

    return pl.pallas_call(
        body, name=name, grid=(t // tt,), in_specs=[_row_spec(tt, f2), _row_spec(tt, f)], out_specs=_row_spec(tt, f2),
        out_shape=jax.ShapeDtypeStruct((t, f2), BF16), compiler_params=_cparams(("parallel",)))(gu, dact)


def _mesh_pos():
    return lax.axis_index("x"), lax.axis_index("y"), lax.axis_index("c")


def _all_gather(arrs, *, name):
    n = len(arrs)
    hbm = pl.BlockSpec(memory_space=pl.ANY)

    def body(*refs):
        ins, outs = refs[:n], refs[n:2 * n]
        send_sems, recv_sems, loc_sems = refs[2 * n:]
        x, y, c = _mesh_pos()
        me, sib = (x, y, c), (x, y, 1 - c)
        chips = [(1 - x, y), (x, 1 - y), (1 - x, 1 - y)]

        def idx(dev):
            return 4 * dev[0] + 2 * dev[1] + dev[2]

        def copy(w, k, block, to, from_input=False):
            return pltpu.make_async_remote_copy(
                src_ref=ins[w] if from_input else outs[w].at[idx(block)], dst_ref=outs[w].at[idx(block)],
                send_sem=send_sems.at[w, k], recv_sem=recv_sems.at[w, k], device_id=to, device_id_type=MESH_ID)

        own = [pltpu.make_async_copy(ins[w], outs[w].at[idx(me)], loc_sems.at[w]) for w in range(n)]
        for cp in own:
            cp.start()
        first = []
        for w in range(n):
            first.append(copy(w, 0, me, sib, True))
            first += [copy(w, 1 + j, me, (*chip, c), True) for j, chip in enumerate(chips)]
        for cp in first:
            cp.start()
        passed = []
        for j, chip in enumerate(chips):
            for w in range(n):
                copy(w, 1 + j, (*chip, c), me).wait_recv()
                fwd = copy(w, 4 + j, (*chip, c), sib)
                fwd.start()
                passed.append(fwd)
        for w in range(n):
            copy(w, 0, sib, me).wait_recv()
            for j, chip in enumerate(chips):
                copy(w, 4 + j, (*chip, 1 - c), me).wait_recv()
        for cp in first + passed:
            cp.wait_send()
        for cp in own:
            cp.wait()

    return pl.pallas_call(
        body, name=name, in_specs=[hbm] * n, out_specs=[hbm] * n,
        out_shape=[jax.ShapeDtypeStruct((N_DEV, *a.shape), a.dtype) for a in arrs],
        scratch_shapes=[pltpu.SemaphoreType.DMA((n, 7)), pltpu.SemaphoreType.DMA((n, 7)), pltpu.SemaphoreType.DMA((n,))],
    )(*arrs)


_HBM = pl.BlockSpec(memory_space=pltpu.HBM)
_SEM = pl.BlockSpec(memory_space=pltpu.SEMAPHORE)
_ANY = pl.BlockSpec(memory_space=pl.ANY)
_EFFECT = pltpu.SideEffectType.DATAFLOW_SIDE_EFFECTING
N_PEERS = N_DEV - 1


def _related(pos, r):
    x, y, c = pos
    return (1 - x if r & 4 else x, 1 - y if r & 2 else y, 1 - c if r & 1 else c)


def _dev_index(dev):
    return 4 * dev[0] + 2 * dev[1] + dev[2]


def _in_hbm(a):
    return pltpu.with_memory_space_constraint(a, pltpu.HBM)


def _split_copies(kind, srcs, lands, send_sems, recv_sems):
    pos = _mesh_pos()
    me = _dev_index(pos)
    out = []
    for w in range(len(srcs)):
        for r in range(1, N_DEV):
            peer = _related(pos, r)
            if kind == "gather":
                src, dst_here, dst_there = srcs[w], lands[w].at[_dev_index(peer)], lands[w].at[me]
            elif srcs[w].ndim == 2:
                cb = lands[w].shape[2]
                src = srcs[w].at[:, pl.ds(pl.multiple_of(_dev_index(peer) * cb, LANES), cb)]
                dst_here = dst_there = lands[w].at[r - 1]
            else:
                src, dst_here, dst_there = srcs[w].at[_dev_index(peer)], lands[w].at[r - 1], lands[w].at[r - 1]
            out.append((src, dst_here, dst_there, send_sems.at[w * N_PEERS + r - 1], recv_sems.at[w * N_PEERS + r - 1], peer))
    return out


def _copy_start(kind, srcs, land_shapes, *, name, after=None):
    n = len(srcs)
    n_after = 0 if after is None else 1

    def body(*refs):
        src_refs, land_refs = refs[:n], refs[n:2 * n]
        send_sems, recv_sems = refs[2 * n + n_after], refs[2 * n + n_after + 1]
        token = refs[-1]
        for src, _, dst, ssem, rsem, peer in _split_copies(kind, src_refs, land_refs, send_sems, recv_sems):
            pltpu.make_async_remote_copy(src_ref=src, dst_ref=dst, send_sem=ssem, recv_sem=rsem, device_id=peer,
                                         device_id_type=MESH_ID).start()
        token[...] = jnp.zeros_like(token)

    lands = [_in_hbm(lax.empty(shape, s.dtype)) for s, shape in zip(srcs, land_shapes)]
    res = pl.pallas_call(
        body, name=name,
        out_shape=(pltpu.SemaphoreType.DMA((n * N_PEERS,)), pltpu.SemaphoreType.DMA((n * N_PEERS,)),
                   *[pltpu.HBM(s.shape, s.dtype) for s in srcs], *[pltpu.HBM(l.shape, l.dtype) for l in lands],
                   jax.ShapeDtypeStruct((8, 128), F32)),
        in_specs=[_HBM] * (2 * n) + [_ANY] * n_after,
        out_specs=(_SEM, _SEM, *[_HBM] * (2 * n), pl.BlockSpec(memory_space=pltpu.VMEM)),
        input_output_aliases={i: 2 + i for i in range(2 * n)},
        compiler_params=pltpu.CompilerParams(has_side_effects=_EFFECT),
    )(*[_in_hbm(s) for s in srcs], *lands, *([] if after is None else [after]))
    return res[0], res[1], list(res[2:2 + n]), list(res[2 + n:2 + 2 * n]), res[-1]


def _copy_wait(kind, send_sems, recv_sems, srcs, lands, after, *, name):
    n = len(srcs)

    def body(*refs):
        src_refs, land_refs = refs[:n], refs[n:2 * n]
        ssems, rsems = refs[2 * n], refs[2 * n + 1]
        for src, dst, _, ssem, rsem, peer in _split_copies(kind, src_refs, land_refs, ssems, rsems):
            cp = pltpu.make_async_remote_copy(src_ref=src, dst_ref=dst, send_sem=ssem, recv_sem=rsem, device_id=peer,
                                              device_id_type=MESH_ID)
            cp.wait_send()
            cp.wait_recv()

    res = pl.pallas_call(
        body, name=name,
        out_shape=(*[pltpu.HBM(s.shape, s.dtype) for s in srcs], *[pltpu.HBM(l.shape, l.dtype) for l in lands]),
        in_specs=[_HBM] * (2 * n) + [_SEM, _SEM, _ANY], out_specs=tuple([_HBM] * (2 * n)),
        input_output_aliases={i: i for i in range(2 * n)},
        compiler_params=pltpu.CompilerParams(has_side_effects=_EFFECT),
    )(*srcs, *lands, send_sems, recv_sems, after)
    return list(res[:n]), list(res[n:])


def _row_tile(rows):
    return max(tr for tr in range(16, min(rows, 512) + 1, 16) if rows % tr == 0)


def _adamw_math(w, g, m, v):
    m2 = ADAM_B1 * m + (1.0 - ADAM_B1) * g
    v2 = ADAM_B2 * v + (1.0 - ADAM_B2) * (g * g)
    m_hat = m2 / (1.0 - ADAM_B1 ** ADAM_STEP)
    v_hat = v2 / (1.0 - ADAM_B2 ** ADAM_STEP)
    delta = -ADAM_LR * (m_hat / (jnp.sqrt(v_hat) + ADAM_EPS) + ADAM_WD * w)
    return delta, m2, v2


def _adamw_shard(partials, landed, dev, w, m, v, *, name):
    r, c = w.shape
    tr = _row_tile(r)

    def body(dev_ref, p_ref, l_ref, w_ref, m_ref, v_ref, g_out, d_out, m_out, v_out):
        del dev_ref
        g = p_ref[...].astype(F32)
        for k in range(N_PEERS):
            g = g + l_ref[k].astype(F32)
        delta, m2, v2 = _adamw_math(w_ref[...], g, m_ref[...], v_ref[...])
        g_out[...] = g
        d_out[...] = delta
        m_out[...] = m2
        v_out[...] = v2

    blk = pl.BlockSpec((tr, c), lambda i, dev_ref: (i, 0))
    if partials.ndim == 2:
        own = pl.BlockSpec((tr, c), lambda i, dev_ref: (i, dev_ref[0]))
    else:
        own = pl.BlockSpec((None, tr, c), lambda i, dev_ref: (dev_ref[0], i, 0))
    gs = pltpu.PrefetchScalarGridSpec(
        num_scalar_prefetch=1, grid=(r // tr,),
        in_specs=[own, pl.BlockSpec((N_PEERS, tr, c), lambda i, dev_ref: (0, i, 0)), blk, blk, blk],
        out_specs=[blk] * 4)
    return pl.pallas_call(
        body, name=name, grid_spec=gs, out_shape=[jax.ShapeDtypeStruct((r, c), F32)] * 4,
        compiler_params=_cparams(("parallel",)))(dev, partials, landed, w, m, v)


def _adamw_small(parts, dev, w, m, v, *, name, col_block):
    _, r, d = parts.shape
    cols = w.shape[1]

    def body(dev_ref, p_ref, w_ref, m_ref, v_ref, g_out, d_out, m_out, v_out):
        del dev_ref
        g = p_ref[0]
        for k in range(1, N_DEV):
            g = g + p_ref[k]
        delta, m2, v2 = _adamw_math(w_ref[...], g, m_ref[...], v_ref[...])
        g_out[...] = g
        d_out[...] = delta
        m_out[...] = m2
        v_out[...] = v2

    blk = pl.BlockSpec((r, cols), lambda i, dev_ref: (0, 0))
    pidx = (lambda i, dev_ref: (0, 0, dev_ref[0])) if col_block else (lambda i, dev_ref: (0, 0, 0))
    gs = pltpu.PrefetchScalarGridSpec(
        num_scalar_prefetch=1, grid=(1,),
        in_specs=[pl.BlockSpec((N_DEV, r, cols), pidx), blk, blk, blk], out_specs=[blk] * 4)
    return pl.pallas_call(
        body, name=name, grid_spec=gs, out_shape=[jax.ShapeDtypeStruct((r, cols), F32)] * 4,
        compiler_params=_cparams(("arbitrary",)))(dev, parts, w, m, v)


def _pad_rows(a, rows):
    return jnp.pad(a, ((0, rows - a.shape[0]), (0, 0)))


def _unblock_cols(g):
    return jnp.transpose(g, (1, 0, 2)).reshape(g.shape[1], N_DEV * g.shape[2])


def _block_cols(full):
    r, c8 = full.shape
    return jnp.transpose(full.reshape(r, N_DEV, c8 // N_DEV), (1, 0, 2))


def kernel(x, mem, norm_mix, w_in, b_gate, conv_w, conv_b, conv_ln_g, conv_ln_b, w_conv_out, sgu_ln_g, sgu_ln_b, sgu_w, sgu_b, w_sgu_out, w_mix_out, norm_xattn, norm_mem, w_q, w_kv, w_xo, norm_ffn, w_gu, w_down, norm_final, loss_target, m_norm_mix, m_w_in, m_b_gate, m_conv_w, m_conv_b, m_conv_ln_g, m_conv_ln_b, m_w_conv_out, m_sgu_ln_g, m_sgu_ln_b, m_sgu_w, m_sgu_b, m_w_sgu_out, m_w_mix_out, m_norm_xattn, m_norm_mem, m_w_q, m_w_kv, m_w_xo, m_norm_ffn, m_w_gu, m_w_down, m_norm_final, v_norm_mix, v_w_in, v_b_gate, v_conv_w, v_conv_b, v_conv_ln_g, v_conv_ln_b, v_w_conv_out, v_sgu_ln_g, v_sgu_ln_b, v_sgu_w, v_sgu_b, v_w_sgu_out, v_w_mix_out, v_norm_xattn, v_norm_mem, v_w_q, v_w_kv, v_w_xo, v_norm_ffn, v_w_gu, v_w_down, v_norm_final):
    given = dict(locals())
    bl, s, d = x.shape
    t = bl * s
    xf = x.reshape(t, d)
    tgt = loss_target.reshape(t, d)
    memf = mem.reshape(bl * mem.shape[1], d)
    cx, cy, cc = lax.axis_index("x"), lax.axis_index("y"), lax.axis_index("c")
    dev = 4 * cx + 2 * cy + cc
    dev_id = dev.astype(jnp.int32).reshape(1)
    col_sharded = ["w_in", "w_kv"]
    transposed = ["w_gu"]

    def shard_of(name, prefix=""):
        a = given[prefix + name][0]
        return jnp.transpose(a) if name in transposed else a

    def full_weight(name, blocks):
        return _unblock_cols(blocks) if name in col_sharded else blocks.reshape(N_DEV * blocks.shape[1], blocks.shape[2])

    g_bg, g_cw = _all_gather([_pad_rows(b_gate[0], 8), _pad_rows(conv_w[0], CONV_HALO)], name="gather_small_params")
    h1, p, w_in_blocks = _in_proj_gather(xf, norm_mix + g_bg[0, 7:8, 0:1], w_in[0].astype(BF16), name="in_proj")
    early = ["w_conv_out", "w_sgu_out", "w_mix_out", "w_q", "w_kv", "w_xo"]
    late = ["w_gu", "w_down"]
    shards = {n: shard_of(n).astype(BF16) for n in early + late}
    started = {}
    for grp, names in (("early", early), ("late", late)):
        srcs = [shards[n] for n in names]
        started[grp] = _copy_start("gather", srcs, [(N_DEV, *a.shape) for a in srcs], name=f"gather_{grp}_start", after=p)
    token = started["early"][4][0:1, 0:1] + started["late"][4][0:1, 0:1]
    wfull = {}
    bg_full = _unblock_cols(g_bg)
    cw_full = _unblock_cols(g_cw)

    def finish_gather(grp, names, after):
        ssem, rsem, srcs, lands, _ = started[grp]
        _, lands = _copy_wait("gather", ssem, rsem, srcs, lands, after, name=f"gather_{grp}_wait")
        for n, land in zip(names, lands):
            wfull[n] = full_weight(n, lax.dynamic_update_index_in_dim(land, shards[n], dev, 0))

    tri = jnp.tril(jnp.ones((SGU_CHUNK, SGU_CHUNK), bool))
    wm32 = jnp.where(tri[None], sgu_w[0], 0.0)
    wm = wm32.astype(BF16)
    wmt = jnp.transpose(wm32, (0, 2, 1)).astype(BF16)
    sgu_bias = jnp.broadcast_to(sgu_b[0][:, :, None], (SGU_GROUPS, SGU_CHUNK, d // SGU_GROUPS))

    c_conv, a_act = _conv_fwd(p, cw_full, conv_b + token, conv_ln_g, conv_ln_b, bl=bl, s=s, name="conv_fwd")
    sg, vn = _sgu_fwd(p, wm, sgu_bias, sgu_ln_g, sgu_ln_b + token, name="sgu_fwd")
    finish_gather("early", early, a_act[0:16, 0:128] + sg[0:16, 0:128])
    y_a = _matmul(a_act, wfull["w_conv_out"], mode="nn", out_dtype=BF16, name="mm_conv_out", tm=1024, tn=1024, tk=1024)
    y_b = _matmul(sg, wfull["w_sgu_out"], mode="nn", out_dtype=BF16, name="mm_sgu_out", tm=1024, tn=1024, tk=1024)
    merged, x1, h2, q = _mix_out(p, y_a, y_b, bg_full, xf, wfull["w_mix_out"], norm_xattn, wfull["w_q"], name="mix_out")
    mem_n = _rms_fwd(memf, norm_mem, name="rms_mem")
    kv = _matmul(mem_n, wfull["w_kv"], mode="nn", out_dtype=BF16, name="mm_kv", tm=1024, tn=1024, tk=1024)
    o, x2, h3 = _attn_fwd(q, kv, x1, wfull["w_xo"], norm_ffn, bl=bl, s=s, name="attn_fwd")
    finish_gather("late", late, h3)
    gu, act, dx3, loss_part, d_norm_final = _ffn_fwd(h3, x2, tgt, wfull["w_gu"], wfull["w_down"],
                                                     norm_final.reshape(1, d), name="ffn_fwd")

    grads = {}
    sent = []

    def send_grads(names, tag, after=None):
        blocks, land_shapes = [], []
        for n in names:
            g = grads[n]
            if g.ndim == 2 and n in col_sharded:
                land_shapes.append((N_PEERS, g.shape[0], g.shape[1] // N_DEV))
            else:
                if g.ndim == 2:
                    g = g.reshape(N_DEV, -1, g.shape[1])
                land_shapes.append((N_PEERS, *g.shape[1:]))
            blocks.append(g)
        ssem, rsem, srcs, lands, tok = _copy_start("scatter", blocks, land_shapes, name=f"grads_{tag}_start", after=after)
        sent.append((names, ssem, rsem, srcs, lands))
        return tok[0:1, 0:1]

    dgu, dx2, do, d_norm_ffn = _ffn_bwd(dx3, gu, x2, wfull["w_down"], wfull["w_gu"], norm_ffn, wfull["w_xo"], name="ffn_bwd")
    grads["w_down"] = _matmul(act, dx3, mode="tn", out_dtype=BF16, name="mm_dw_down", tm=1408, tn=1024, tk=1024)
    grads["w_gu"] = _matmul(dgu, h3, mode="tn", out_dtype=BF16, name="mm_dw_gu", tm=1408, tn=1024, tk=1024)
    tok = send_grads(["w_down", "w_gu"], "ffn")
    grads["w_xo"] = _matmul(o, dx2, mode="tn", out_dtype=BF16, name="mm_dw_xo", tm=1024, tn=1024, tk=1024)
    dq, dkv = _attn_bwd(q, kv, do, bl=bl, s=s, name="attn_bwd")
    grads["w_kv"] = _matmul(mem_n, dkv, mode="tn", out_dtype=BF16, name="mm_dw_kv", tm=1024, tn=256, tk=1024,
                            col_blocks=N_DEV)
    tok2 = send_grads(["w_xo", "w_kv"], "attn")
    dmem_n = _matmul(dkv, wfull["w_kv"], mode="nt", out_dtype=F32, name="mm_d_mem", tm=512, tn=1024, tk=2048)
    d_norm_mem = _rms_bwd(None, dmem_n, memf, norm_mem, name="rms_mem_bwd", need_dx=False)
    dx1, d_norm_xattn, dw_q = _proj_rms_bwd(dq, dx2, x1, wfull["w_q"], norm_xattn + (tok + tok2), name="q_rms_bwd", h=h2)
    dp, dy_a, dy_b, d_b_gate, dw_mix, dw_in_gates = _gates_bwd_fused(dx1, p, y_a, y_b, bg_full, wfull["w_mix_out"],
                                                                    merged, h1, name="gates_bwd")
    grads["w_q"] = dw_q.astype(BF16)
    grads["w_mix_out"] = dw_mix.astype(BF16)
    grads["w_sgu_out"] = _matmul(sg, dy_b, mode="tn", out_dtype=BF16, name="mm_dw_sgu", tm=1024, tn=1024, tk=1024)
    dc, d_conv_ln_g, d_conv_ln_b, dw_conv = _conv_ln_bwd_fused(dy_a, c_conv, a_act, wfull["w_conv_out"], conv_ln_g,
                                                               conv_ln_b, name="conv_ln_bwd")
    grads["w_conv_out"] = dw_conv.astype(BF16)
    tok = send_grads(["w_q", "w_mix_out", "w_sgu_out", "w_conv_out"], "mixer")
    dp, d_sgu_w, d_sgu_b, d_sgu_ln_g, d_sgu_ln_b = _sgu_bwd(dp, dy_b, wfull["w_sgu_out"], p, vn, wm, wmt, sgu_bias,
                                                             sgu_ln_g + tok, name="sgu_bwd")
    dw_in_sgu = _matmul(h1, dp, mode="tn", out_dtype=BF16, name="mm_dw_in_sgu", tm=1024, tn=1024, tk=2048,
                        b_cols=(2 * d, 2 * d))
    dp_conv, d_conv_w, d_conv_b, grad_x, d_norm_mix = _conv_bwd(dp, dc, p, cw_full, _unblock_cols(w_in_blocks), dx1, xf,
                                                                 norm_mix, bl=bl, s=s, name="conv_bwd")
    out = {}

    rep_names = ["norm_mix", "conv_b", "conv_ln_g", "conv_ln_b", "sgu_ln_g", "sgu_ln_b", "norm_xattn", "norm_mem",
                 "norm_ffn", "norm_final", "sgu_b"]
    rep_grads = [d_norm_mix, d_conv_b, d_conv_ln_g, d_conv_ln_b, d_sgu_ln_g, d_sgu_ln_b, d_norm_xattn, d_norm_mem,
                 d_norm_ffn, d_norm_final, d_sgu_b.reshape(1, d)]
    nrep = len(rep_names)
    pad = jnp.zeros((16 - nrep, d), F32)
    sgw_rows = SGU_GROUPS * SGU_CHUNK * SGU_CHUNK // d

    def pack_rep(vecs, sgw, extra=None):
        fill = pad if extra is None else jnp.concatenate([extra, pad[1:]], axis=0)
        return jnp.concatenate([v.reshape(1, d) for v in vecs] + [fill, sgw.reshape(sgw_rows, d)], axis=0)

    def pack_col(bg, cw):
        return jnp.concatenate([_pad_rows(bg, 8), _pad_rows(cw, CONV_HALO)], axis=0)

    small_a = pack_rep(rep_grads, d_sgu_w, extra=jnp.broadcast_to(loss_part, (1, d)))
    small_b = jnp.concatenate([d_b_gate, d_conv_w], axis=0)
    parts_a, parts_b = _all_gather([small_a, small_b], name="gather_small_grads")
    dw_in_conv = _matmul(h1, dp_conv, mode="tn", out_dtype=BF16, name="mm_dw_in_conv", tm=1024, tn=1024, tk=2048)
    grads["w_in"] = jnp.concatenate([dw_in_conv, dw_in_sgu, dw_in_gates], axis=1)
    send_grads(["w_in"], "in", after=parts_a)
    res_a = _adamw_small(parts_a, dev_id, pack_rep([given[n] for n in rep_names], sgu_w),
                         pack_rep([given["m_" + n] for n in rep_names], m_sgu_w),
                         pack_rep([given["v_" + n] for n in rep_names], v_sgu_w), name="adamw_small", col_block=False)
    res_b = _adamw_small(parts_b, dev_id, pack_col(b_gate[0], conv_w[0]), pack_col(m_b_gate[0], m_conv_w[0]),
                         pack_col(v_b_gate[0], v_conv_w[0]), name="adamw_small_cols", col_block=True)
    for i, n in enumerate(rep_names):
        out[n] = [r[i].reshape(given[n].shape) for r in res_a]
    out["sgu_w"] = [r[16:16 + sgw_rows].reshape(sgu_w.shape) for r in res_a]
    out["b_gate"] = [r[0:2][None] for r in res_b]
    out["conv_w"] = [r[8:8 + CONV_WIDTH][None] for r in res_b]

    done = res_a[0]
    for names, ssem, rsem, srcs, lands in sent:
        srcs, lands = _copy_wait("scatter", ssem, rsem, srcs, lands, done, name=f"grads_{names[0]}_wait")
        for n, partials, landed in zip(names, srcs, lands):
            res = _adamw_shard(partials, landed, dev_id, shard_of(n), shard_of(n, "m_"), shard_of(n, "v_"),
                               name=f"adamw_{n}")
            done = res[0]
            out[n] = [(jnp.transpose(r) if n in transposed else r)[None] for r in res]

    order = ["norm_mix", "w_in", "b_gate", "conv_w", "conv_b", "conv_ln_g", "conv_ln_b", "w_conv_out", "sgu_ln_g",
             "sgu_ln_b", "sgu_w", "sgu_b", "w_sgu_out", "w_mix_out", "norm_xattn", "norm_mem", "w_q", "w_kv", "w_xo",
             "norm_ffn", "w_gu", "w_down", "norm_final"]
    loss = res_a[0][nrep, 0]
    return (loss, grad_x.reshape(x.shape), *[out[n][0] for n in order], *[out[n][1] for n in order],
            *[out[n][2] for n in order], *[out[n][3] for n in order])
```

```python
import functools

import jax
import jax.numpy as jnp
from jax import lax
from jax.experimental import pallas as pl
from jax.experimental.pallas import tpu as pltpu

F32 = jnp.float32
BF16 = jnp.bfloat16
RMS_EPS = 1e-6
LN_EPS = 1e-5
CONV_WIDTH = 31
CONV_HALO = 32
CONV_ROWS = 64
CONV_COLS = 256
LANES = 128
SGU_CHUNK = 128
SGU_GROUPS = 8
HEADS = 4
N_DEV = 8
ADAM_LR, ADAM_B1, ADAM_B2, ADAM_EPS, ADAM_WD, ADAM_STEP = 0.001, 0.9, 0.999, 1e-08, 0.01, 10
VMEM_LIMIT = 56 * 1024 * 1024
TOKEN_TILE = 256
ATTN_TILE = 1024
MESH_ID = pl.DeviceIdType.MESH

_GELU_K = 0.7978845608028654
_GELU_C = 0.044715


def _cparams(sem=None):
    return pltpu.CompilerParams(dimension_semantics=sem, vmem_limit_bytes=VMEM_LIMIT)


def _sigmoid(v):
    return 0.5 * jnp.tanh(0.5 * v) + 0.5


def _gelu(v):
    return 0.5 * v * (1.0 + jnp.tanh(_GELU_K * (v + _GELU_C * v * v * v)))


def _gelu_grad(v):
    th = jnp.tanh(_GELU_K * (v + _GELU_C * v * v * v))
    return 0.5 * (1.0 + th) + 0.5 * v * (1.0 - th * th) * _GELU_K * (1.0 + 3.0 * _GELU_C * v * v)


def _dot(a, b, dims):
    return lax.dot_general(a, b, (dims, ((), ())), preferred_element_type=F32)


_NN = ((1,), (0,))
_NT = ((1,), (1,))
_TN = ((0,), (0,))


def _matmul(a, b, *, mode, out_dtype, name, tm=512, tn=512, tk=512, chunk=None, residual=None, rms_gain=None,
            col_blocks=None, b_cols=None):
    if mode == "nn":
        (m, k), (_, n) = a.shape, b.shape
    elif mode == "nt":
        (m, k), (n, _) = a.shape, b.shape
    else:
        (k, m), (_, n) = a.shape, b.shape
    b_first = 0
    if b_cols is not None:
        assert mode == "tn"
        b_first, n = b_cols
    tm, tn, tk = min(tm, m), min(tn, n), min(tk, k)
    assert b_first % tn == 0
    b_first //= tn
    assert m % tm == 0 and n % tn == 0 and k % tk == 0, (name, a.shape, b.shape, tm, tn, tk)
    nk = k // tk
    dims = {"nn": _NN, "nt": _NT, "tn": _TN}[mode]
    chunk = tn if chunk is None else min(chunk, tn)
    assert tn % chunk == 0
    if rms_gain is not None:
        assert tn == n and chunk == n

    def body(*refs):
        refs = list(refs)
        a_ref, b_ref = refs[:2]
        pos = 2
        r_ref = g_ref = None
        if residual is not None:
            r_ref = refs[pos]
            pos += 1
        if rms_gain is not None:
            g_ref = refs[pos]
            pos += 1
        o_ref = refs[pos]
        pos += 1
        h_ref = None
        if rms_gain is not None:
            h_ref = refs[pos]
            pos += 1
        acc_ref = refs[pos] if nk > 1 else None
        av = a_ref[...].astype(BF16)
        for c0 in range(0, tn, chunk):
            cs = slice(c0, c0 + chunk)
            bv = (b_ref[cs, :] if mode == "nt" else b_ref[:, cs]).astype(BF16)
            part = _dot(av, bv, dims)

            def finish(res, cs=cs):
                if r_ref is not None:
                    res = res + r_ref[:, cs].astype(F32)
                o_ref[:, cs] = res.astype(out_dtype)
                if h_ref is not None:
                    r = lax.rsqrt(jnp.mean(res * res, axis=-1, keepdims=True) + RMS_EPS)
                    h_ref[...] = (res * r * g_ref[...]).astype(BF16)

            if nk == 1:
                finish(part)
            else:
                kk = pl.program_id(2)

                @pl.when(kk == 0)
                def _(part=part, cs=cs):
                    acc_ref[:, cs] = part

                @pl.when(kk > 0)
                def _(part=part, cs=cs):
                    acc_ref[:, cs] += part

                @pl.when(kk == nk - 1)
                def _(finish=finish, cs=cs):
                    finish(acc_ref[:, cs])

    resident = dict(pipeline_mode=pl.Buffered(1)) if (n == tn and nk == 1 and mode != "tn" and m > tm) else {}
    if mode == "nn":
        a_spec = pl.BlockSpec((tm, tk), lambda i, j, kk: (i, kk))
        b_spec = pl.BlockSpec((tk, tn), lambda i, j, kk: (kk, j), **resident)
    elif mode == "nt":
        a_spec = pl.BlockSpec((tm, tk), lambda i, j, kk: (i, kk))
        b_spec = pl.BlockSpec((tn, tk), lambda i, j, kk: (j, kk), **resident)
    else:
        a_spec = pl.BlockSpec((tk, tm), lambda i, j, kk: (kk, i))
        b_spec = pl.BlockSpec((tk, tn), lambda i, j, kk: (kk, j + b_first))
    o_spec = pl.BlockSpec((tm, tn), lambda i, j, kk: (i, j))
    in_specs, args = [a_spec, b_spec], [a, b]
    if residual is not None:
        in_specs.append(o_spec)
        args.append(residual)
    out_shape, out_specs = [jax.ShapeDtypeStruct((m, n), out_dtype)], [o_spec]
    if col_blocks is not None:
        assert residual is None and rms_gain is None and (n // col_blocks) % tn == 0
        per = n // col_blocks // tn
        out_shape = [jax.ShapeDtypeStruct((col_blocks, m, n // col_blocks), out_dtype)]
        out_specs = [pl.BlockSpec((None, tm, tn), lambda i, j, kk: (j // per, i, j % per))]
    if rms_gain is not None:
        in_specs.append(pl.BlockSpec((1, n), lambda i, j, kk: (0, 0)))
        args.append(rms_gain)
        out_shape.append(jax.ShapeDtypeStruct((m, n), BF16))
        out_specs.append(o_spec)
    res = pl.pallas_call(
        body, name=name, grid=(m // tm, n // tn, nk), in_specs=in_specs, out_specs=out_specs, out_shape=out_shape,
        scratch_shapes=[pltpu.VMEM((tm, tn), F32)] if nk > 1 else [],
        compiler_params=_cparams(("parallel", "parallel", "arbitrary")),
    )(*args)
    return res if rms_gain is not None else res[0]


def _row_call(name, t, tm, rows_in, residents, rows_out, accs, body):
    n_in, n_res, n_out, n_acc = len(rows_in), len(residents), len(rows_out), len(accs)
    steps = t // tm
    assert t % tm == 0
    narrow = [i for i, (_, dt) in enumerate(accs) if dt != F32]

    def kernel_body(*refs):
        in_refs, res_refs = refs[:n_in], refs[n_in:n_in + n_res]
        out_refs = refs[n_in + n_res:n_in + n_res + n_out]
        acc_out = list(refs[n_in + n_res + n_out:n_in + n_res + n_out + n_acc])
        scratch = refs[n_in + n_res + n_out + n_acc:]
        acc_refs = list(acc_out)
        for s_ref, i in zip(scratch, narrow):
            acc_refs[i] = s_ref
        if accs:
            @pl.when(pl.program_id(0) == 0)
            def _():
                for acc in acc_refs:
                    acc[...] = jnp.zeros_like(acc)
        body(in_refs, res_refs, out_refs, acc_refs)
        if narrow:
            @pl.when(pl.program_id(0) == steps - 1)
            def _():
                for i in narrow:
                    acc_out[i][...] = acc_refs[i][...].astype(acc_out[i].dtype)

    once = dict(pipeline_mode=pl.Buffered(1)) if steps > 1 else {}
    in_specs = [pl.BlockSpec((tm, cols), lambda i, cb=cb: (i, cb)) for _, cols, cb in rows_in]
    in_specs += [pl.BlockSpec(r.shape, lambda i, nd=r.ndim: (0,) * nd, **once) for r in residents]
    out_specs = [pl.BlockSpec((tm, cols), lambda i, cb=cb: (i, cb)) for _, cols, cb, _ in rows_out]
    out_specs += [pl.BlockSpec(shape, lambda i, nd=len(shape): (0,) * nd) for shape, _ in accs]
    out_shape = [jax.ShapeDtypeStruct((t, total), dt) for total, _, _, dt in rows_out]
    out_shape += [jax.ShapeDtypeStruct(shape, dt) for shape, dt in accs]
    return pl.pallas_call(
        kernel_body, name=name, grid=(steps,), in_specs=in_specs, out_specs=out_specs, out_shape=out_shape,
        scratch_shapes=[pltpu.VMEM(accs[i][0], F32) for i in narrow],
        compiler_params=_cparams(("arbitrary",) if accs else ("parallel",)),
    )(*[a for a, _, _ in rows_in], *residents)


def _rms_apply(xv, gain):
    return xv * lax.rsqrt(jnp.mean(xv * xv, axis=-1, keepdims=True) + RMS_EPS) * gain


def _rms_grad(dres, dh, xv, gain):
    r = lax.rsqrt(jnp.mean(xv * xv, axis=-1, keepdims=True) + RMS_EPS)
    xhat = xv * r
    dxh = dh * gain
    dx = dres + r * (dxh - xhat * jnp.mean(dxh * xhat, axis=-1, keepdims=True))
    return dx, jnp.sum(dh * xhat, axis=0, keepdims=True)


def _in_proj(xf, gain, w_in, *, name):
    t, d = xf.shape
    n = w_in.shape[1]
    chunk = n // 4

    def body(ins, res, outs, accs):
        (x_ref,), (g_ref, w_ref), (h_ref, p_ref) = ins, res, outs
        h = _rms_apply(x_ref[...], g_ref[...]).astype(BF16)
        h_ref[...] = h
        for c0 in range(0, n, chunk):
            p_ref[:, c0:c0 + chunk] = _dot(h, w_ref[:, c0:c0 + chunk], _NN).astype(BF16)

    return _row_call(name, t, min(512, t), [(xf, d, 0)], [gain, w_in], [(d, d, 0, BF16), (n, n, 0, BF16)], [], body)


def _in_proj_gather(xf, gain, w_shard, *, name):
    t, d = xf.shape
    cb = w_shard.shape[1]
    tm = min(1024, t)
    steps = t // tm
    mx, my, _ = _mesh_pos()
    order = jnp.stack([2 * mx + my, 2 * (1 - mx) + my, 2 * mx + (1 - my), 2 * (1 - mx) + (1 - my)]).astype(jnp.int32)

    def body(order_ref, x_ref, g_ref, ws_ref, h_ref, p_ref, wout_ref, w_ref, send_sems, recv_sems, own_sem):
        ps, i = pl.program_id(0), pl.program_id(1)
        x, y, c = _mesh_pos()
        me, sib = (x, y, c), (x, y, 1 - c)
        chips = [(1 - x, y), (x, 1 - y), (1 - x, 1 - y)]

        def copy(k, block, to, from_shard=False):
            return pltpu.make_async_remote_copy(
                src_ref=ws_ref if from_shard else w_ref.at[_dev_index(block)], dst_ref=w_ref.at[_dev_index(block)],
                send_sem=send_sems.at[k], recv_sem=recv_sems.at[k], device_id=to, device_id_type=MESH_ID)

        own = pltpu.make_async_copy(ws_ref, w_ref.at[_dev_index(me)], own_sem)
        first = [copy(0, me, sib, True)] + [copy(1 + j, me, (*chip, c), True) for j, chip in enumerate(chips)]
        passed = [copy(4 + j, (*chip, c), sib) for j, chip in enumerate(chips)]

        @pl.when(jnp.logical_and(ps == 0, i == 0))
        def _():
            own.start()
            for cp in first:
                cp.start()
            own.wait()
            copy(0, sib, me).wait_recv()

        for j, chip in enumerate(chips):
            @pl.when(jnp.logical_and(ps == j + 1, i == 0))
            def _(j=j, chip=chip):
                copy(1 + j, (*chip, c), me).wait_recv()
                passed[j].start()
                copy(4 + j, (*chip, 1 - c), me).wait_recv()

        h = _rms_apply(x_ref[...], g_ref[...]).astype(BF16)
        h_ref[...] = h
        chip_id = order_ref[ps]
        p_ref[:, 0:cb] = _dot(h, w_ref[2 * chip_id], _NN).astype(BF16)
        p_ref[:, cb:2 * cb] = _dot(h, w_ref[2 * chip_id + 1], _NN).astype(BF16)

        @pl.when(jnp.logical_and(ps == 3, i == steps - 1))
        def _():
            for cp in first + passed:
                cp.wait_send()
            keep = pltpu.make_async_copy(w_ref, wout_ref, own_sem)
            keep.start()
            keep.wait()

    gs = pltpu.PrefetchScalarGridSpec(
        num_scalar_prefetch=1, grid=(4, steps),
        in_specs=[pl.BlockSpec((tm, d), lambda ps, i, o: (i, 0)), pl.BlockSpec((1, d), lambda ps, i, o: (0, 0)),
                  pl.BlockSpec(memory_space=pl.ANY)],
        out_specs=[pl.BlockSpec((tm, d), lambda ps, i, o: (jnp.where(ps == 0, i, steps - 1), 0)),
                   pl.BlockSpec((tm, 2 * cb), lambda ps, i, o: (i, o[ps])), pl.BlockSpec(memory_space=pl.ANY)],
        scratch_shapes=[pltpu.VMEM((N_DEV, d, cb), BF16), pltpu.SemaphoreType.DMA((7,)), pltpu.SemaphoreType.DMA((7,)),
                        pltpu.SemaphoreType.DMA(())])
    return pl.pallas_call(
        body, name=name, grid_spec=gs,
        out_shape=[jax.ShapeDtypeStruct((t, d), BF16), jax.ShapeDtypeStruct((t, N_DEV * cb), BF16),
                   jax.ShapeDtypeStruct((N_DEV, d, cb), BF16)],
        compiler_params=_cparams(("arbitrary", "arbitrary")))(order, xf, gain, w_shard)


def _mix_out(p, y_a, y_b, b_gate, xf, w_mix, gain, w_q, *, name):
    t, d = xf.shape

    def body(ins, res, outs, accs):
        ga_ref, gb_ref, ya_ref, yb_ref, x_ref = ins
        bg_ref, wm_ref, g_ref, wq_ref = res
        m_ref, x1_ref, h_ref, q_ref = outs
        sa = _sigmoid(ga_ref[...].astype(F32) + bg_ref[0:1, :])
        sb = _sigmoid(gb_ref[...].astype(F32) + bg_ref[1:2, :])
        merged = (sa * ya_ref[...].astype(F32) + sb * yb_ref[...].astype(F32)).astype(BF16)
        m_ref[...] = merged
        x1 = x_ref[...] + _dot(merged, wm_ref[...], _NN)
        x1_ref[...] = x1
        h = _rms_apply(x1, g_ref[...]).astype(BF16)
        h_ref[...] = h
        q_ref[...] = _dot(h, wq_ref[...], _NN).astype(BF16)

    return _row_call(name, t, min(512, t), [(p, d, 4), (p, d, 5), (y_a, d, 0), (y_b, d, 0), (xf, d, 0)],
                     [b_gate, w_mix, gain, w_q], [(d, d, 0, BF16), (d, d, 0, F32), (d, d, 0, BF16), (d, d, 0, BF16)], [], body)


def _ffn_fwd(h3, x2, target, w_gu_t, w_down, gain, *, name):
    t, d = x2.shape
    f2 = w_gu_t.shape[0]
    f = f2 // 2
    half = f // 2

    def body(ins, res, outs, accs):
        h_ref, x2_ref, t_ref = ins
        wgu_ref, wd_ref, g_ref = res
        gu_ref, act_ref, dx_ref = outs
        loss_ref, dg_ref = accs
        h = h_ref[...]
        x3 = x2_ref[...]
        for c0 in (0, half):
            gt = _dot(h, wgu_ref[c0:c0 + half, :], _NT).astype(BF16)
            up = _dot(h, wgu_ref[f + c0:f + c0 + half, :], _NT).astype(BF16)
            gu_ref[:, c0:c0 + half] = gt
            gu_ref[:, f + c0:f + c0 + half] = up
            gtf = gt.astype(F32)
            act = (gtf * _sigmoid(gtf) * up.astype(F32)).astype(BF16)
            act_ref[:, c0:c0 + half] = act
            x3 = x3 + _dot(act, wd_ref[c0:c0 + half, :], _NN)
        g = g_ref[...]
        r = lax.rsqrt(jnp.mean(x3 * x3, axis=-1, keepdims=True) + RMS_EPS)
        xhat = x3 * r
        err = xhat * g - t_ref[...]
        loss_ref[...] += 0.5 * jnp.sum(jnp.mean(err * err, axis=-1, keepdims=True), axis=0, keepdims=True)
        dy = err * (1.0 / d)
        dg_ref[...] += jnp.sum(dy * xhat, axis=0, keepdims=True)
        dxh = dy * g
        dx_ref[...] = r * (dxh - xhat * jnp.mean(dxh * xhat, axis=-1, keepdims=True))

    return _row_call(name, t, min(256, t), [(h3, d, 0), (x2, d, 0), (target, d, 0)], [w_gu_t, w_down, gain],
                     [(f2, f2, 0, BF16), (f, f, 0, BF16), (d, d, 0, F32)], [((1, 1), F32), ((1, d), F32)], body)


def _ffn_bwd(dx3, gu, x2, w_down, w_gu_t, gain, w_xo, *, name):
    t, d = x2.shape
    f2 = w_gu_t.shape[0]
    f = f2 // 2
    half = f // 2

    def body(ins, res, outs, accs):
        dx3_ref, gu_ref, x2_ref = ins
        wd_ref, wgu_ref, g_ref, wxo_ref = res
        dgu_ref, dx2_ref, do_ref = outs
        (dg_ref,) = accs
        dx3v = dx3_ref[...]
        dxb = dx3v.astype(BF16)
        dh = jnp.zeros(dx3v.shape, F32)
        for c0 in (0, half):
            dact = _dot(dxb, wd_ref[c0:c0 + half, :], _NT)
            gt = gu_ref[:, c0:c0 + half].astype(F32)
            up = gu_ref[:, f + c0:f + c0 + half].astype(F32)
            sg = _sigmoid(gt)
            dgt = (dact * up * sg * (1.0 + gt * (1.0 - sg))).astype(BF16)
            dup = (dact * gt * sg).astype(BF16)
            dgu_ref[:, c0:c0 + half] = dgt
            dgu_ref[:, f + c0:f + c0 + half] = dup
            dh = dh + _dot(dgt, wgu_ref[c0:c0 + half, :], _NN) + _dot(dup, wgu_ref[f + c0:f + c0 + half, :], _NN)
        dx2, dg = _rms_grad(dx3v, dh, x2_ref[...], g_ref[...])
        dx2_ref[...] = dx2
        dg_ref[...] += dg
        do_ref[...] = _dot(dx2.astype(BF16), wxo_ref[...], _NT).astype(BF16)

    return _row_call(name, t, min(256, t), [(dx3, d, 0), (gu, f2, 0), (x2, d, 0)], [w_down, w_gu_t, gain, w_xo],
                     [(f2, f2, 0, BF16), (d, d, 0, F32), (d, d, 0, BF16)], [((1, d), F32)], body)


def _proj_rms_bwd(dy, dres, x, w, gain, *, name, h=None):
    t, d = x.shape
    k = dy.shape[1]

    def body(ins, res, outs, accs):
        dy_ref, dres_ref, x_ref = ins[:3]
        w_ref, g_ref = res
        if h is not None:
            accs[1][...] += _dot(ins[3][...], dy_ref[...], _TN)
        if w.ndim == 3:
            cb = w.shape[2]
            dh = _dot(dy_ref[:, 0:cb], w_ref[0], _NT)
            for j in range(1, w.shape[0]):
                dh = dh + _dot(dy_ref[:, j * cb:(j + 1) * cb], w_ref[j], _NT)
        else:
            dh = _dot(dy_ref[...], w_ref[...], _NT)
        dx, dg = _rms_grad(dres_ref[...], dh, x_ref[...], g_ref[...])
        outs[0][...] = dx
        accs[0][...] += dg

    rows_in = [(dy, k, 0), (dres, d, 0), (x, d, 0)] + ([(h, d, 0)] if h is not None else [])
    accs = [((1, d), F32)] + ([((d, k), BF16)] if h is not None else [])
    return _row_call(name, t, min(512, t), rows_in, [w, gain], [(d, d, 0, F32)], accs, body)


def _gates_bwd_fused(dx1, p, y_a, y_b, b_gate, w_mix, merged, h1, *, name):
    t, d = y_a.shape

    def body(ins, res, outs, accs):
        dx_ref, ga_ref, gb_ref, ya_ref, yb_ref, m_ref, h1_ref = ins
        bg_ref, wm_ref = res
        dp_ref, dya_ref, dyb_ref = outs
        dbg_ref, dwm_ref, dwin_ref = accs
        dxb = dx_ref[...].astype(BF16)
        dwm_ref[...] += _dot(m_ref[...], dxb, _TN)
        dm = _dot(dxb, wm_ref[...], _NT)
        sa = _sigmoid(ga_ref[...].astype(F32) + bg_ref[0:1, :])
        sb = _sigmoid(gb_ref[...].astype(F32) + bg_ref[1:2, :])
        dya_ref[...] = (dm * sa).astype(BF16)
        dyb_ref[...] = (dm * sb).astype(BF16)
        dga = dm * ya_ref[...].astype(F32) * sa * (1.0 - sa)
        dgb = dm * yb_ref[...].astype(F32) * sb * (1.0 - sb)
        dp_ref[:, 0:d] = dga.astype(BF16)
        dp_ref[:, d:2 * d] = dgb.astype(BF16)
        dbg_ref[0:1, :] += jnp.sum(dga, axis=0, keepdims=True)
        dbg_ref[1:2, :] += jnp.sum(dgb, axis=0, keepdims=True)
        dwin_ref[...] += _dot(h1_ref[...], dp_ref[...], _TN)

    return _row_call(name, t, min(256, t),
                     [(dx1, d, 0), (p, d, 4), (p, d, 5), (y_a, d, 0), (y_b, d, 0), (merged, d, 0), (h1, d, 0)],
                     [b_gate, w_mix], [(p.shape[1], 2 * d, 2, BF16), (d, d, 0, BF16), (d, d, 0, BF16)],
                     [((8, d), F32), ((d, d), BF16), ((d, 2 * d), BF16)], body)


def _conv_ln_bwd_fused(dy_a, c, a_act, w_conv_out, ln_g, ln_b, *, name):
    t, d = c.shape

    def body(ins, res, outs, accs):
        dy_ref, c_ref, act_ref = ins
        w_ref, lg_ref, lb_ref = res
        dlg_ref, dlb_ref, dw_ref = accs
        dw_ref[...] += _dot(act_ref[...], dy_ref[...], _TN)
        dact = _dot(dy_ref[...], w_ref[...], _NT)
        cv = c_ref[...].astype(F32)
        g = lg_ref[...]
        mu = jnp.mean(cv, axis=-1, keepdims=True)
        dv = cv - mu
        rstd = lax.rsqrt(jnp.mean(dv * dv, axis=-1, keepdims=True) + LN_EPS)
        chat = dv * rstd
        aln = chat * g + lb_ref[...]
        sg = _sigmoid(aln)
        daln = dact * (sg * (1.0 + aln * (1.0 - sg)))
        dlb_ref[...] += jnp.sum(daln, axis=0, keepdims=True)
        dlg_ref[...] += jnp.sum(daln * chat, axis=0, keepdims=True)
        dchat = daln * g
        dc = rstd * (dchat - jnp.mean(dchat, axis=-1, keepdims=True)
                     - chat * jnp.mean(dchat * chat, axis=-1, keepdims=True))
        outs[0][...] = dc.astype(BF16)

    return _row_call(name, t, min(512, t), [(dy_a, d, 0), (c, d, 0), (a_act, d, 0)], [w_conv_out, ln_g, ln_b],
                     [(d, d, 0, BF16)], [((1, d), F32), ((1, d), F32), ((d, d), BF16)], body)


def _row_spec(tt, cols, col_block=0):
    return pl.BlockSpec((tt, cols), lambda i: (i, col_block))


def _const_spec(shape):
    return pl.BlockSpec(shape, lambda *_: (0,) * len(shape))


def _rms_fwd(x, gain, *, name):
    t, d = x.shape
    tt = min(TOKEN_TILE, t)

    def body(x_ref, g_ref, h_ref):
        xv = x_ref[...]
        r = lax.rsqrt(jnp.mean(xv * xv, axis=-1, keepdims=True) + RMS_EPS)
        h_ref[...] = (xv * r * g_ref[...]).astype(BF16)

    return pl.pallas_call(
        body, name=name, grid=(t // tt,), in_specs=[_row_spec(tt, d), _const_spec((1, d))],
        out_specs=_row_spec(tt, d), out_shape=jax.ShapeDtypeStruct((t, d), BF16),
        compiler_params=_cparams(("parallel",)))(x, gain)


def _rms_bwd(dres, dh, x, gain, *, name, need_dx=True):
    t, d = x.shape
    tt = min(TOKEN_TILE, t)

    def body(*refs):
        if need_dx:
            dres_ref, dh_ref, x_ref, g_ref, dx_ref, dg_ref = refs
        else:
            dh_ref, x_ref, g_ref, dg_ref = refs

        @pl.when(pl.program_id(0) == 0)
        def _():
            dg_ref[...] = jnp.zeros_like(dg_ref)

        xv = x_ref[...]
        dhv = dh_ref[...].astype(F32)
        r = lax.rsqrt(jnp.mean(xv * xv, axis=-1, keepdims=True) + RMS_EPS)
        xhat = xv * r
        dg_ref[...] += jnp.sum(dhv * xhat, axis=0, keepdims=True)
        if need_dx:
            dxh = dhv * g_ref[...]
            dx_ref[...] = dres_ref[...] + r * (dxh - xhat * jnp.mean(dxh * xhat, axis=-1, keepdims=True))

    rs = _row_spec(tt, d)
    if need_dx:
        in_specs, args = [rs, rs, rs, _const_spec((1, d))], (dres, dh, x, gain)
        out_specs = [rs, _const_spec((1, d))]
        out_shape = [jax.ShapeDtypeStruct((t, d), F32), jax.ShapeDtypeStruct((1, d), F32)]
    else:
        in_specs, args = [rs, rs, _const_spec((1, d))], (dh, x, gain)
        out_specs = [_const_spec((1, d))]
        out_shape = [jax.ShapeDtypeStruct((1, d), F32)]
    res = pl.pallas_call(body, name=name, grid=(t // tt,), in_specs=in_specs, out_specs=out_specs, out_shape=out_shape,
                         compiler_params=_cparams(("arbitrary",)))(*args)
    return res if need_dx else res[0]


def _final_loss(x3, target, gain, *, name):
    t, d = x3.shape
    tt = min(TOKEN_TILE, t)

    def body(x_ref, t_ref, g_ref, loss_ref, dx_ref, dg_ref):
        @pl.when(pl.program_id(0) == 0)
        def _():
            loss_ref[...] = jnp.zeros_like(loss_ref)
            dg_ref[...] = jnp.zeros_like(dg_ref)

        xv = x_ref[...]
        g = g_ref[...]
        r = lax.rsqrt(jnp.mean(xv * xv, axis=-1, keepdims=True) + RMS_EPS)
        xhat = xv * r
        err = xhat * g - t_ref[...]
        loss_ref[...] += 0.5 * jnp.sum(jnp.mean(err * err, axis=-1, keepdims=True), axis=0, keepdims=True)
        dy = err * (1.0 / d)
        dg_ref[...] += jnp.sum(dy * xhat, axis=0, keepdims=True)
        dxh = dy * g
        dx_ref[...] = r * (dxh - xhat * jnp.mean(dxh * xhat, axis=-1, keepdims=True))

    rs = _row_spec(tt, d)
    return pl.pallas_call(
        body, name=name, grid=(t // tt,), in_specs=[rs, rs, _const_spec((1, d))],
        out_specs=[_const_spec((1, 1)), rs, _const_spec((1, d))],
        out_shape=[jax.ShapeDtypeStruct((1, 1), F32), jax.ShapeDtypeStruct((t, d), F32), jax.ShapeDtypeStruct((1, d), F32)],
        compiler_params=_cparams(("arbitrary",)))(x3, target, gain)


SUBLANES = 8
SHIFT_ROWS = 40


def _conv_apply(sbuf_ref, w_ref, out_ref, tt, offsets, bias_ref=None):
    d = out_ref.shape[1]
    for cc in range(d // LANES):
        cs = slice(cc * LANES, (cc + 1) * LANES)
        taps = [jnp.broadcast_to(w_ref[k:k + 1, cs], (SUBLANES, LANES)) for k in range(CONV_WIDTH)]
        bias = None if bias_ref is None else jnp.broadcast_to(bias_ref[:, cs], (SUBLANES, LANES))

        def row_body(r, carry, cs=cs, taps=taps, bias=bias):
            r0 = pl.multiple_of(r * CONV_ROWS, CONV_ROWS)
            for q in range(CONV_ROWS // SUBLANES):
                acc = _tap(sbuf_ref, r0 + q * SUBLANES, cs, offsets[0]) * taps[0]
                for k in range(1, CONV_WIDTH):
                    acc = acc + _tap(sbuf_ref, r0 + q * SUBLANES, cs, offsets[k]) * taps[k]
                if bias is not None:
                    acc = acc + bias
                out_ref[pl.ds(r0 + q * SUBLANES, SUBLANES), cs] = acc
            return carry

        lax.fori_loop(0, tt // CONV_ROWS, row_body, 0)


def _fill_shifts(sbuf_ref, rows):
    d = sbuf_ref.shape[2]
    assert rows % SHIFT_ROWS == 0

    def row_body(i, carry):
        r0 = pl.multiple_of(i * SHIFT_ROWS, SUBLANES)
        for cc in range(d // CONV_COLS):
            cs = slice(cc * CONV_COLS, (cc + 1) * CONV_COLS)
            win = sbuf_ref[0, pl.ds(r0, SHIFT_ROWS + SUBLANES), cs]
            for sh in range(1, SUBLANES):
                sbuf_ref[sh, pl.ds(r0, SHIFT_ROWS), cs] = win[sh:sh + SHIFT_ROWS, :]
        return carry

    lax.fori_loop(0, rows // SHIFT_ROWS, row_body, 0)


def _tap(sbuf_ref, r0, cs, offset):
    sh = offset % SUBLANES
    return sbuf_ref[sh, pl.ds(pl.multiple_of(r0 + (offset - sh), SUBLANES), SUBLANES), cs]


def _conv_specs(bl, s, tt, d, col_a, col_g):
    nj = s // tt
    per = tt // CONV_HALO
    main_a = pl.BlockSpec((tt, d), lambda b, j: (b * nj + j, col_a))
    main_g = pl.BlockSpec((tt, d), lambda b, j: (b * nj + j, col_g))
    prev = lambda b, j: jnp.maximum((b * nj + j) * per - 1, 0)
    halo_a = pl.BlockSpec((CONV_HALO, d), lambda b, j: (prev(b, j), col_a))
    halo_g = pl.BlockSpec((CONV_HALO, d), lambda b, j: (prev(b, j), col_g))
    return main_a, main_g, halo_a, halo_g


def _fill_glu(sbuf_ref, a_ref, g_ref, ha_ref, hg_ref, tt):
    first = pl.program_id(1) == 0
    ha = ha_ref[...].astype(F32)
    hg = hg_ref[...].astype(F32)
    sbuf_ref[0, pl.ds(0, CONV_HALO), :] = jnp.where(first, 0.0, ha * _sigmoid(hg))
    av = a_ref[...].astype(F32)
    gv = g_ref[...].astype(F32)
    sbuf_ref[0, pl.ds(CONV_HALO, tt), :] = av * _sigmoid(gv)
    _fill_shifts(sbuf_ref, tt + CONV_HALO - SUBLANES)


def _conv_fwd(p, conv_w, conv_b, ln_g, ln_b, *, bl, s, name):
    t = p.shape[0]
    d = conv_w.shape[1]
    tt = min(TOKEN_TILE, s)
    off = CONV_HALO - (CONV_WIDTH - 1)

    def body(a_ref, g_ref, ha_ref, hg_ref, w_ref, b_ref, lg_ref, lb_ref, c_ref, act_ref, sbuf_ref, cbuf_ref):
        _fill_glu(sbuf_ref, a_ref, g_ref, ha_ref, hg_ref, tt)

        _conv_apply(sbuf_ref, w_ref, cbuf_ref, tt, [off + k for k in range(CONV_WIDTH)], bias_ref=b_ref)
        cv = cbuf_ref[...]
        c_ref[...] = cv.astype(BF16)
        mu = jnp.mean(cv, axis=-1, keepdims=True)
        dv = cv - mu
        rstd = lax.rsqrt(jnp.mean(dv * dv, axis=-1, keepdims=True) + LN_EPS)
        aln = dv * rstd * lg_ref[...] + lb_ref[...]
        act_ref[...] = (aln * _sigmoid(aln)).astype(BF16)

    main_a, main_g, halo_a, halo_g = _conv_specs(bl, s, tt, d, 0, 1)
    out_spec = pl.BlockSpec((tt, d), lambda b, j: (b * (s // tt) + j, 0))
    return pl.pallas_call(
        body, name=name, grid=(bl, s // tt),
        in_specs=[main_a, main_g, halo_a, halo_g, _const_spec((CONV_HALO, d)), _const_spec((1, d)), _const_spec((1, d)),
                  _const_spec((1, d))],
        out_specs=[out_spec, out_spec],
        out_shape=[jax.ShapeDtypeStruct((t, d), BF16), jax.ShapeDtypeStruct((t, d), BF16)],
        scratch_shapes=[pltpu.VMEM((SUBLANES, tt + CONV_HALO, d), F32), pltpu.VMEM((tt, d), F32)],
        compiler_params=_cparams(("parallel", "parallel")))(p, p, p, p, conv_w, conv_b, ln_g, ln_b)


def _conv_ln_bwd(dact, c, ln_g, ln_b, *, name):
    t, d = c.shape
    tt = min(TOKEN_TILE, t)

    def body(da_ref, c_ref, lg_ref, lb_ref, dc_ref, dlg_ref, dlb_ref):
        @pl.when(pl.program_id(0) == 0)
        def _():
            dlg_ref[...] = jnp.zeros_like(dlg_ref)
            dlb_ref[...] = jnp.zeros_like(dlb_ref)

        cv = c_ref[...].astype(F32)
        g = lg_ref[...]
        mu = jnp.mean(cv, axis=-1, keepdims=True)
        dv = cv - mu
        rstd = lax.rsqrt(jnp.mean(dv * dv, axis=-1, keepdims=True) + LN_EPS)
        chat = dv * rstd
        aln = chat * g + lb_ref[...]
        sg = _sigmoid(aln)
        daln = da_ref[...].astype(F32) * (sg * (1.0 + aln * (1.0 - sg)))
        dlb_ref[...] += jnp.sum(daln, axis=0, keepdims=True)
        dlg_ref[...] += jnp.sum(daln * chat, axis=0, keepdims=True)
        dchat = daln * g
        dc = rstd * (dchat - jnp.mean(dchat, axis=-1, keepdims=True)
                     - chat * jnp.mean(dchat * chat, axis=-1, keepdims=True))
        dc_ref[...] = dc.astype(BF16)

    rs = _row_spec(tt, d)
    cs = _const_spec((1, d))
    return pl.pallas_call(
        body, name=name, grid=(t // tt,), in_specs=[rs, rs, cs, cs], out_specs=[rs, cs, cs],
        out_shape=[jax.ShapeDtypeStruct((t, d), BF16), jax.ShapeDtypeStruct((1, d), F32), jax.ShapeDtypeStruct((1, d), F32)],
        compiler_params=_cparams(("arbitrary",)))(dact, c, ln_g, ln_b)


def _conv_bwd(dp, dc, p, conv_w, h1, *, bl, s, name):
    t = p.shape[0]
    d = conv_w.shape[1]
    tt = min(TOKEN_TILE, s)
    nj = s // tt
    per = tt // CONV_HALO
    off = CONV_HALO - (CONV_WIDTH - 1)
    last_blk = t // CONV_HALO - 1

    def body(dp_in, dc_ref, dcn_ref, a_ref, g_ref, ha_ref, hg_ref, w_ref, h1_ref, dp_ref, dw_ref, db_ref, dwin_ref,
             gbuf_ref, dbuf_ref, dglu_ref, acc_ref):
        del dp_in
        b, j = pl.program_id(0), pl.program_id(1)
        start = jnp.logical_and(b == 0, j == 0)
        end = jnp.logical_and(b == bl - 1, j == nj - 1)

        @pl.when(start)
        def _():
            acc_ref[...] = jnp.zeros_like(acc_ref)
            db_ref[...] = jnp.zeros_like(db_ref)
            dwin_ref[...] = jnp.zeros_like(dwin_ref)

        _fill_glu(gbuf_ref, a_ref, g_ref, ha_ref, hg_ref, tt)
        dcv = dc_ref[...].astype(F32)
        dbuf_ref[0, pl.ds(0, tt), :] = dcv
        dbuf_ref[0, pl.ds(tt, CONV_HALO), :] = jnp.where(j == nj - 1, 0.0, dcn_ref[...].astype(F32))
        _fill_shifts(dbuf_ref, tt + CONV_HALO - SUBLANES)
        db_ref[...] += jnp.sum(dcv, axis=0, keepdims=True)

        for cc in range(d // LANES):
            cs = slice(cc * LANES, (cc + 1) * LANES)

            def row_body(r, accs, cs=cs):
                r0 = pl.multiple_of(r * CONV_ROWS, CONV_ROWS)
                accs = list(accs)
                for q in range(CONV_ROWS // SUBLANES):
                    dcw = dbuf_ref[0, pl.ds(r0 + q * SUBLANES, SUBLANES), cs]
                    for k in range(CONV_WIDTH):
                        accs[k] = accs[k] + dcw * _tap(gbuf_ref, r0 + q * SUBLANES, cs, off + k)
                return tuple(accs)

            zero = jnp.zeros((SUBLANES, LANES), F32)
            accs = lax.fori_loop(0, tt // CONV_ROWS, row_body, (zero,) * CONV_WIDTH)
            for k in range(CONV_WIDTH):
                acc_ref[k, :, cs] += accs[k]

        _conv_apply(dbuf_ref, w_ref, dglu_ref, tt, [CONV_WIDTH - 1 - k for k in range(CONV_WIDTH)])
        dglu = dglu_ref[...]
        av = a_ref[...].astype(F32)
        sg = _sigmoid(g_ref[...].astype(F32))
        dp_ref[:, 0:d] = (dglu * sg).astype(BF16)
        dp_ref[:, d:2 * d] = (dglu * av * sg * (1.0 - sg)).astype(BF16)
        dwin_ref[...] += _dot(h1_ref[...], dp_ref[...], _TN)

        @pl.when(end)
        def _():
            for k in range(CONV_WIDTH):
                dw_ref[k:k + 1, :] = jnp.sum(acc_ref[k], axis=0, keepdims=True)
            dw_ref[CONV_WIDTH:CONV_HALO, :] = jnp.zeros((CONV_HALO - CONV_WIDTH, d), F32)

    main_a, main_g, halo_a, halo_g = _conv_specs(bl, s, tt, d, 0, 1)
    dc_main = pl.BlockSpec((tt, d), lambda b, j: (b * nj + j, 0))
    dc_next = pl.BlockSpec((CONV_HALO, d), lambda b, j: (jnp.minimum((b * nj + j + 1) * per, last_blk), 0))
    return pl.pallas_call(
        body, name=name, grid=(bl, nj),
        in_specs=[pl.BlockSpec(memory_space=pl.ANY), dc_main, dc_next, main_a, main_g, halo_a, halo_g,
                  _const_spec((CONV_HALO, d)), dc_main],
        out_specs=[pl.BlockSpec((tt, 2 * d), lambda b, j: (b * nj + j, 0)), _const_spec((CONV_HALO, d)), _const_spec((1, d)),
                   _const_spec((d, 2 * d))],
        out_shape=[jax.ShapeDtypeStruct(dp.shape, dp.dtype), jax.ShapeDtypeStruct((CONV_HALO, d), F32),
                   jax.ShapeDtypeStruct((1, d), F32), jax.ShapeDtypeStruct((d, 2 * d), F32)],
        scratch_shapes=[pltpu.VMEM((SUBLANES, tt + CONV_HALO, d), F32), pltpu.VMEM((SUBLANES, tt + CONV_HALO, d), F32),
                        pltpu.VMEM((tt, d), F32), pltpu.VMEM((CONV_HALO, SUBLANES, d), F32)],
        input_output_aliases={0: 0},
        compiler_params=_cparams(("arbitrary", "arbitrary")))(dp, dc, dc, p, p, p, p, conv_w, h1)


def _sgu_stats(bv):
    gv = _gelu(bv)
    mu = jnp.mean(gv, axis=-1, keepdims=True)
    dv = gv - mu
    rstd = lax.rsqrt(jnp.mean(dv * dv, axis=-1, keepdims=True) + LN_EPS)
    return dv * rstd, rstd


def _sgu_fwd(p, wm, bias, ln_g, ln_b, *, name):
    t = p.shape[0]
    d = ln_g.shape[1]
    tt = SGU_CHUNK
    gd = d // SGU_GROUPS

    def body(u_ref, v_ref, wm_ref, bias_ref, lg_ref, lb_ref, sg_ref, vn_ref):
        u = _gelu(u_ref[...].astype(F32))
        vhat, _ = _sgu_stats(v_ref[...].astype(F32))
        vb = (vhat * lg_ref[...] + lb_ref[...]).astype(BF16)
        vn_ref[...] = vb
        for g in range(SGU_GROUPS):
            gs = slice(g * gd, (g + 1) * gd)
            z = _dot(wm_ref[g], vb[:, gs], _NN) + bias_ref[g]
            sg_ref[:, gs] = (u[:, gs] * z).astype(BF16)

    rs = _row_spec(tt, d)
    return pl.pallas_call(
        body, name=name, grid=(t // tt,),
        in_specs=[_row_spec(tt, d, 2), _row_spec(tt, d, 3), _const_spec(wm.shape), _const_spec(bias.shape),
                  _const_spec((1, d)), _const_spec((1, d))],
        out_specs=[rs, rs], out_shape=[jax.ShapeDtypeStruct((t, d), BF16), jax.ShapeDtypeStruct((t, d), BF16)],
        compiler_params=_cparams(("parallel",)))(p, p, wm, bias, ln_g, ln_b)


def _sgu_bwd(dp, dy_b, w_out, p, vn, wm, wmt, bias, ln_g, *, name):
    t = p.shape[0]
    d = ln_g.shape[1]
    tt = SGU_CHUNK
    gd = d // SGU_GROUPS
    nsteps = t // tt

    def body(dp_in, dyb_ref, wout_ref, u_ref, v_ref, vn_ref, wm_ref, wmt_ref, bias_ref, lg_ref,
             dp_ref, dw_ref, dbs_ref, dlg_ref, dlb_ref, dz_acc):
        del dp_in
        i = pl.program_id(0)

        @pl.when(i == 0)
        def _():
            dw_ref[...] = jnp.zeros_like(dw_ref)
            dlg_ref[...] = jnp.zeros_like(dlg_ref)
            dlb_ref[...] = jnp.zeros_like(dlb_ref)
            dz_acc[...] = jnp.zeros_like(dz_acc)

        bu = u_ref[...].astype(F32)
        bv = v_ref[...].astype(F32)
        u = _gelu(bu)
        vhat, rstd = _sgu_stats(bv)
        vb = vn_ref[...]
        dsg = _dot(dyb_ref[...], wout_ref[...], _NT)
        row = lax.broadcasted_iota(jnp.int32, (tt, tt), 0)
        col = lax.broadcasted_iota(jnp.int32, (tt, tt), 1)
        causal = col <= row
        du_parts, dv_parts = [], []
        for g in range(SGU_GROUPS):
            gs = slice(g * gd, (g + 1) * gd)
            z = _dot(wm_ref[g], vb[:, gs], _NN) + bias_ref[g]
            du_parts.append(dsg[:, gs] * z)
            dz = dsg[:, gs] * u[:, gs]
            dz_acc[:, gs] += dz
            dzb = dz.astype(BF16)
            dw_ref[g] += jnp.where(causal, _dot(dzb, vb[:, gs], _NT), 0.0)
            dv_parts.append(_dot(wmt_ref[g], dzb, _NN))
        du = jnp.concatenate(du_parts, axis=1)
        dv = jnp.concatenate(dv_parts, axis=1)
        dp_ref[:, 0:d] = (du * _gelu_grad(bu)).astype(BF16)
        dlb_ref[...] += jnp.sum(dv, axis=0, keepdims=True)
        dlg_ref[...] += jnp.sum(dv * vhat, axis=0, keepdims=True)
        dvh = dv * lg_ref[...]
        dgv = rstd * (dvh - jnp.mean(dvh, axis=-1, keepdims=True) - vhat * jnp.mean(dvh * vhat, axis=-1, keepdims=True))
        dp_ref[:, d:2 * d] = (dgv * _gelu_grad(bv)).astype(BF16)

        @pl.when(i == nsteps - 1)
        def _():
            ones = jnp.ones((8, gd), F32)
            for g in range(SGU_GROUPS):
                gs = slice(g * gd, (g + 1) * gd)
                tot = lax.dot_general(ones, dz_acc[:, gs], (_NT, ((), ())), preferred_element_type=F32,
                                      precision=lax.Precision.HIGHEST)
                dbs_ref[g:g + 1, :] = tot[0:1, :]

    rs = _row_spec(tt, d)
    c1 = _const_spec((1, d))
    return pl.pallas_call(
        body, name=name, grid=(nsteps,),
        in_specs=[pl.BlockSpec(memory_space=pl.ANY), rs, _const_spec(w_out.shape), _row_spec(tt, d, 2), _row_spec(tt, d, 3),
                  rs, _const_spec(wm.shape), _const_spec(wmt.shape), _const_spec(bias.shape), c1],
        out_specs=[pl.BlockSpec((tt, 2 * d), lambda i: (i, 1)), _const_spec(wm.shape), _const_spec((SGU_GROUPS, tt)), c1, c1],
        out_shape=[jax.ShapeDtypeStruct(dp.shape, dp.dtype), jax.ShapeDtypeStruct(wm.shape, F32),
                   jax.ShapeDtypeStruct((SGU_GROUPS, tt), F32), jax.ShapeDtypeStruct((1, d), F32),
                   jax.ShapeDtypeStruct((1, d), F32)],
        scratch_shapes=[pltpu.VMEM((tt, d), F32)],
        input_output_aliases={0: 0},
        compiler_params=_cparams(("arbitrary",)))(dp, dy_b, w_out, p, p, vn, wm, wmt, bias, ln_g)


def _gates_fwd(p, ya, yb, b_gate, *, name):
    t, d = ya.shape
    tt = min(TOKEN_TILE, t)

    def body(ga_ref, gb_ref, ya_ref, yb_ref, bg_ref, o_ref):
        sa = _sigmoid(ga_ref[...].astype(F32) + bg_ref[0:1, :])
        sb = _sigmoid(gb_ref[...].astype(F32) + bg_ref[1:2, :])
        o_ref[...] = (sa * ya_ref[...].astype(F32) + sb * yb_ref[...].astype(F32)).astype(BF16)

    rs = _row_spec(tt, d)
    return pl.pallas_call(
        body, name=name, grid=(t // tt,),
        in_specs=[_row_spec(tt, d, 4), _row_spec(tt, d, 5), rs, rs, _const_spec(b_gate.shape)],
        out_specs=rs, out_shape=jax.ShapeDtypeStruct((t, d), BF16),
        compiler_params=_cparams(("parallel",)))(p, p, ya, yb, b_gate)


def _gates_bwd(dmerged, p, ya, yb, b_gate, *, name):
    t, d = ya.shape
    tt = min(TOKEN_TILE, t)

    def body(dm_ref, ga_ref, gb_ref, ya_ref, yb_ref, bg_ref, dp_ref, dya_ref, dyb_ref, dbg_ref):
        @pl.when(pl.program_id(0) == 0)
        def _():
            dbg_ref[...] = jnp.zeros_like(dbg_ref)

        dm = dm_ref[...].astype(F32)
        sa = _sigmoid(ga_ref[...].astype(F32) + bg_ref[0:1, :])
        sb = _sigmoid(gb_ref[...].astype(F32) + bg_ref[1:2, :])
        dya_ref[...] = (dm * sa).astype(BF16)
        dyb_ref[...] = (dm * sb).astype(BF16)
        dga = dm * ya_ref[...].astype(F32) * sa * (1.0 - sa)
        dgb = dm * yb_ref[...].astype(F32) * sb * (1.0 - sb)
        dp_ref[:, 0:d] = dga.astype(BF16)
        dp_ref[:, d:2 * d] = dgb.astype(BF16)
        dbg_ref[0:1, :] += jnp.sum(dga, axis=0, keepdims=True)
        dbg_ref[1:2, :] += jnp.sum(dgb, axis=0, keepdims=True)

    rs = _row_spec(tt, d)
    return pl.pallas_call(
        body, name=name, grid=(t // tt,),
        in_specs=[rs, _row_spec(tt, d, 4), _row_spec(tt, d, 5), rs, rs, _const_spec(b_gate.shape)],
        out_specs=[pl.BlockSpec((tt, 2 * d), lambda i: (i, 2)), rs, rs, _const_spec((8, d))],
        out_shape=[jax.ShapeDtypeStruct(p.shape, BF16), jax.ShapeDtypeStruct((t, d), BF16),
                   jax.ShapeDtypeStruct((t, d), BF16), jax.ShapeDtypeStruct((8, d), F32)],
        compiler_params=_cparams(("arbitrary",)))(dmerged, p, p, ya, yb, b_gate)


def _softmax_rows(s):
    e = jnp.exp(s - jnp.max(s, axis=-1, keepdims=True))
    return e / jnp.sum(e, axis=-1, keepdims=True)


def _attn_fwd(q, kv, x1, w_xo, gain, *, bl, s, name):
    t, d = q.shape
    mlen = kv.shape[0] // bl
    hd = d // HEADS
    tq = min(ATTN_TILE, s)
    nq = s // tq
    scale = hd ** -0.5

    def body(q_ref, kv_ref, x1_ref, w_ref, g_ref, o_ref, x2_ref, h_ref):
        for h in range(HEADS):
            hs = slice(h * hd, (h + 1) * hd)
            vs = slice(d + h * hd, d + (h + 1) * hd)
            pr = _softmax_rows(_dot(q_ref[:, hs], kv_ref[:, hs], _NT) * scale)
            o_ref[:, hs] = _dot(pr.astype(BF16), kv_ref[:, vs], _NN).astype(BF16)
        x2 = x1_ref[...] + _dot(o_ref[...], w_ref[...], _NN)
        x2_ref[...] = x2
        h_ref[...] = _rms_apply(x2, g_ref[...]).astype(BF16)

    qs = pl.BlockSpec((tq, d), lambda b, j: (b * nq + j, 0))
    return pl.pallas_call(
        body, name=name, grid=(bl, nq),
        in_specs=[qs, pl.BlockSpec((mlen, 2 * d), lambda b, j: (b, 0)), qs, _const_spec(w_xo.shape), _const_spec((1, d))],
        out_specs=[qs, qs, qs],
        out_shape=[jax.ShapeDtypeStruct((t, d), BF16), jax.ShapeDtypeStruct((t, d), F32), jax.ShapeDtypeStruct((t, d), BF16)],
        compiler_params=_cparams(("parallel", "parallel")))(q, kv, x1, w_xo, gain)


def _attn_bwd(q, kv, do, *, bl, s, name):
    t, d = q.shape
    mlen = kv.shape[0] // bl
    hd = d // HEADS
    tq = min(ATTN_TILE, s)
    nq = s // tq
    scale = hd ** -0.5

    def body(q_ref, kv_ref, do_ref, dq_ref, dkv_ref):
        @pl.when(pl.program_id(1) == 0)
        def _():
            dkv_ref[...] = jnp.zeros_like(dkv_ref)

        for h in range(HEADS):
            hs = slice(h * hd, (h + 1) * hd)
            vs = slice(d + h * hd, d + (h + 1) * hd)
            qh, kh, vh, doh = q_ref[:, hs], kv_ref[:, hs], kv_ref[:, vs], do_ref[:, hs]
            pr = _softmax_rows(_dot(qh, kh, _NT) * scale)
            dpr = _dot(doh, vh, _NT)
            dkv_ref[:, vs] += _dot(pr.astype(BF16), doh, _TN)
            ds = (pr * (dpr - jnp.sum(dpr * pr, axis=-1, keepdims=True)) * scale).astype(BF16)
            dq_ref[:, hs] = _dot(ds, kh, _NN).astype(BF16)
            dkv_ref[:, hs] += _dot(ds, qh, _TN)

    qs = pl.BlockSpec((tq, d), lambda b, j: (b * nq + j, 0))
    ks = pl.BlockSpec((mlen, 2 * d), lambda b, j: (b, 0))
    return pl.pallas_call(
        body, name=name, grid=(bl, nq), in_specs=[qs, ks, qs], out_specs=[qs, ks],
        out_shape=[jax.ShapeDtypeStruct((t, d), BF16), jax.ShapeDtypeStruct(kv.shape, F32)],
        compiler_params=_cparams(("parallel", "arbitrary")))(q, kv, do)


def _swiglu_fwd(gu, *, name):
    t, f2 = gu.shape
    f = f2 // 2
    tt = min(TOKEN_TILE, t)

    def body(gu_ref, o_ref):
        gt = gu_ref[:, 0:f].astype(F32)
        up = gu_ref[:, f:f2].astype(F32)
        o_ref[...] = (gt * _sigmoid(gt) * up).astype(BF16)

    return pl.pallas_call(
        body, name=name, grid=(t // tt,), in_specs=[_row_spec(tt, f2)], out_specs=_row_spec(tt, f),
        out_shape=jax.ShapeDtypeStruct((t, f), BF16), compiler_params=_cparams(("parallel",)))(gu)


def _swiglu_bwd(gu, dact, *, name):
    t, f2 = gu.shape
    f = f2 // 2
    tt = min(TOKEN_TILE, t)

    def body(gu_ref, da_ref, o_ref):
        gt = gu_ref[:, 0:f].astype(F32)
        up = gu_ref[:, f:f2].astype(F32)
        da = da_ref[...].astype(F32)
        sg = _sigmoid(gt)
        o_ref[:, 0:f] = (da * up * sg * (1.0 + gt * (1.0 - sg))).astype(BF16)
        o_ref[:, f:f2] = (da * gt * sg).astype(BF16)

    return pl.pallas_call(
        body, name=name, grid=(t // tt,), in_specs=[_row_spec(tt, f2), _row_spec(tt, f)], out_specs=_row_spec(tt, f2),
        out_shape=jax.ShapeDtypeStruct((t, f2), BF16), compiler_params=_cparams(("parallel",)))(gu, dact)


def _mesh_pos():
    return lax.axis_index("x"), lax.axis_index("y"), lax.axis_index("c")


def _all_gather(arrs, *, name):
    n = len(arrs)
    hbm = pl.BlockSpec(memory_space=pl.ANY)

    def body(*refs):
        ins, outs = refs[:n], refs[n:2 * n]
        send_sems, recv_sems, loc_sems = refs[2 * n:]
        x, y, c = _mesh_pos()
        me, sib = (x, y, c), (x, y, 1 - c)
        chips = [(1 - x, y), (x, 1 - y), (1 - x, 1 - y)]

        def idx(dev):
            return 4 * dev[0] + 2 * dev[1] + dev[2]

        def copy(w, k, block, to, from_input=False):
            return pltpu.make_async_remote_copy(
                src_ref=ins[w] if from_input else outs[w].at[idx(block)], dst_ref=outs[w].at[idx(block)],
                send_sem=send_sems.at[w, k], recv_sem=recv_sems.at[w, k], device_id=to, device_id_type=MESH_ID)

        own = [pltpu.make_async_copy(ins[w], outs[w].at[idx(me)], loc_sems.at[w]) for w in range(n)]
        for cp in own:
            cp.start()
        first = []
        for w in range(n):
            first.append(copy(w, 0, me, sib, True))
            first += [copy(w, 1 + j, me, (*chip, c), True) for j, chip in enumerate(chips)]
        for cp in first:
            cp.start()
        passed = []
        for j, chip in enumerate(chips):
            for w in range(n):
                copy(w, 1 + j, (*chip, c), me).wait_recv()
                fwd = copy(w, 4 + j, (*chip, c), sib)
                fwd.start()
                passed.append(fwd)
        for w in range(n):
            copy(w, 0, sib, me).wait_recv()
            for j, chip in enumerate(chips):
                copy(w, 4 + j, (*chip, 1 - c), me).wait_recv()
        for cp in first + passed:
            cp.wait_send()
        for cp in own:
            cp.wait()

    return pl.pallas_call(
        body, name=name, in_specs=[hbm] * n, out_specs=[hbm] * n,
        out_shape=[jax.ShapeDtypeStruct((N_DEV, *a.shape), a.dtype) for a in arrs],
        scratch_shapes=[pltpu.SemaphoreType.DMA((n, 7)), pltpu.SemaphoreType.DMA((n, 7)), pltpu.SemaphoreType.DMA((n,))],
    )(*arrs)


_HBM = pl.BlockSpec(memory_space=pltpu.HBM)
_SEM = pl.BlockSpec(memory_space=pltpu.SEMAPHORE)
_ANY = pl.BlockSpec(memory_space=pl.ANY)
_EFFECT = pltpu.SideEffectType.DATAFLOW_SIDE_EFFECTING
N_PEERS = N_DEV - 1


def _related(pos, r):
    x, y, c = pos
    return (1 - x if r & 4 else x, 1 - y if r & 2 else y, 1 - c if r & 1 else c)


def _dev_index(dev):
    return 4 * dev[0] + 2 * dev[1] + dev[2]


def _in_hbm(a):
    return pltpu.with_memory_space_constraint(a, pltpu.HBM)


def _split_copies(kind, srcs, lands, send_sems, recv_sems):
    pos = _mesh_pos()
    me = _dev_index(pos)
    out = []
    for w in range(len(srcs)):
        for r in range(1, N_DEV):
            peer = _related(pos, r)
            if kind == "gather":
                src, dst_here, dst_there = srcs[w], lands[w].at[_dev_index(peer)], lands[w].at[me]
            elif srcs[w].ndim == 2:
                cb = lands[w].shape[2]
                src = srcs[w].at[:, pl.ds(pl.multiple_of(_dev_index(peer) * cb, LANES), cb)]
                dst_here = dst_there = lands[w].at[r - 1]
            else:
                src, dst_here, dst_there = srcs[w].at[_dev_index(peer)], lands[w].at[r - 1], lands[w].at[r - 1]
            out.append((src, dst_here, dst_there, send_sems.at[w * N_PEERS + r - 1], recv_sems.at[w * N_PEERS + r - 1], peer))
    return out


def _copy_start(kind, srcs, land_shapes, *, name, after=None):
    n = len(srcs)
    n_after = 0 if after is None else 1

    def body(*refs):
        src_refs, land_refs = refs[:n], refs[n:2 * n]
        send_sems, recv_sems = refs[2 * n + n_after], refs[2 * n + n_after + 1]
        token = refs[-1]
        for src, _, dst, ssem, rsem, peer in _split_copies(kind, src_refs, land_refs, send_sems, recv_sems):
            pltpu.make_async_remote_copy(src_ref=src, dst_ref=dst, send_sem=ssem, recv_sem=rsem, device_id=peer,
                                         device_id_type=MESH_ID).start()
        token[...] = jnp.zeros_like(token)

    lands = [_in_hbm(lax.empty(shape, s.dtype)) for s, shape in zip(srcs, land_shapes)]
    res = pl.pallas_call(
        body, name=name,
        out_shape=(pltpu.SemaphoreType.DMA((n * N_PEERS,)), pltpu.SemaphoreType.DMA((n * N_PEERS,)),
                   *[pltpu.HBM(s.shape, s.dtype) for s in srcs], *[pltpu.HBM(l.shape, l.dtype) for l in lands],
                   jax.ShapeDtypeStruct((8, 128), F32)),
        in_specs=[_HBM] * (2 * n) + [_ANY] * n_after,
        out_specs=(_SEM, _SEM, *[_HBM] * (2 * n), pl.BlockSpec(memory_space=pltpu.VMEM)),
        input_output_aliases={i: 2 + i for i in range(2 * n)},
        compiler_params=pltpu.CompilerParams(has_side_effects=_EFFECT),
    )(*[_in_hbm(s) for s in srcs], *lands, *([] if after is None else [after]))
    return res[0], res[1], list(res[2:2 + n]), list(res[2 + n:2 + 2 * n]), res[-1]


def _copy_wait(kind, send_sems, recv_sems, srcs, lands, after, *, name):
    n = len(srcs)

    def body(*refs):
        src_refs, land_refs = refs[:n], refs[n:2 * n]
        ssems, rsems = refs[2 * n], refs[2 * n + 1]
        for src, dst, _, ssem, rsem, peer in _split_copies(kind, src_refs, land_refs, ssems, rsems):
            cp = pltpu.make_async_remote_copy(src_ref=src, dst_ref=dst, send_sem=ssem, recv_sem=rsem, device_id=peer,
                                              device_id_type=MESH_ID)
            cp.wait_send()
            cp.wait_recv()

    res = pl.pallas_call(
        body, name=name,
        out_shape=(*[pltpu.HBM(s.shape, s.dtype) for s in srcs], *[pltpu.HBM(l.shape, l.dtype) for l in lands]),
        in_specs=[_HBM] * (2 * n) + [_SEM, _SEM, _ANY], out_specs=tuple([_HBM] * (2 * n)),
        input_output_aliases={i: i for i in range(2 * n)},
        compiler_params=pltpu.CompilerParams(has_side_effects=_EFFECT),
    )(*srcs, *lands, send_sems, recv_sems, after)
    return list(res[:n]), list(res[n:])


def _row_tile(rows):
    return max(tr for tr in range(16, min(rows, 512) + 1, 16) if rows % tr == 0)


def _adamw_math(w, g, m, v):
    m2 = ADAM_B1 * m + (1.0 - ADAM_B1) * g
    v2 = ADAM_B2 * v + (1.0 - ADAM_B2) * (g * g)
    m_hat = m2 / (1.0 - ADAM_B1 ** ADAM_STEP)
    v_hat = v2 / (1.0 - ADAM_B2 ** ADAM_STEP)
    delta = -ADAM_LR * (m_hat / (jnp.sqrt(v_hat) + ADAM_EPS) + ADAM_WD * w)
    return delta, m2, v2


def _adamw_shard(partials, landed, dev, w, m, v, *, name):
    r, c = w.shape
    tr = _row_tile(r)

    def body(dev_ref, p_ref, l_ref, w_ref, m_ref, v_ref, g_out, d_out, m_out, v_out):
        del dev_ref
        g = p_ref[...].astype(F32)
        for k in range(N_PEERS):
            g = g + l_ref[k].astype(F32)
        delta, m2, v2 = _adamw_math(w_ref[...], g, m_ref[...], v_ref[...])
        g_out[...] = g
        d_out[...] = delta
        m_out[...] = m2
        v_out[...] = v2

    blk = pl.BlockSpec((tr, c), lambda i, dev_ref: (i, 0))
    if partials.ndim == 2:
        own = pl.BlockSpec((tr, c), lambda i, dev_ref: (i, dev_ref[0]))
    else:
        own = pl.BlockSpec((None, tr, c), lambda i, dev_ref: (dev_ref[0], i, 0))
    gs = pltpu.PrefetchScalarGridSpec(
        num_scalar_prefetch=1, grid=(r // tr,),
        in_specs=[own, pl.BlockSpec((N_PEERS, tr, c), lambda i, dev_ref: (0, i, 0)), blk, blk, blk],
        out_specs=[blk] * 4)
    return pl.pallas_call(
        body, name=name, grid_spec=gs, out_shape=[jax.ShapeDtypeStruct((r, c), F32)] * 4,
        compiler_params=_cparams(("parallel",)))(dev, partials, landed, w, m, v)


def _adamw_small(parts, dev, w, m, v, *, name, col_block):
    _, r, d = parts.shape
    cols = w.shape[1]

    def body(dev_ref, p_ref, w_ref, m_ref, v_ref, g_out, d_out, m_out, v_out):
        del dev_ref
        g = p_ref[0]
        for k in range(1, N_DEV):
            g = g + p_ref[k]
        delta, m2, v2 = _adamw_math(w_ref[...], g, m_ref[...], v_ref[...])
        g_out[...] = g
        d_out[...] = delta
        m_out[...] = m2
        v_out[...] = v2

    blk = pl.BlockSpec((r, cols), lambda i, dev_ref: (0, 0))
    pidx = (lambda i, dev_ref: (0, 0, dev_ref[0])) if col_block else (lambda i, dev_ref: (0, 0, 0))
    gs = pltpu.PrefetchScalarGridSpec(
        num_scalar_prefetch=1, grid=(1,),
        in_specs=[pl.BlockSpec((N_DEV, r, cols), pidx), blk, blk, blk], out_specs=[blk] * 4)
    return pl.pallas_call(
        body, name=name, grid_spec=gs, out_shape=[jax.ShapeDtypeStruct((r, cols), F32)] * 4,
        compiler_params=_cparams(("arbitrary",)))(dev, parts, w, m, v)


def _pad_rows(a, rows):
    return jnp.pad(a, ((0, rows - a.shape[0]), (0, 0)))


def _unblock_cols(g):
    return jnp.transpose(g, (1, 0, 2)).reshape(g.shape[1], N_DEV * g.shape[2])


def _block_cols(full):
    r, c8 = full.shape
    return jnp.transpose(full.reshape(r, N_DEV, c8 // N_DEV), (1, 0, 2))


def kernel(x, mem, norm_mix, w_in, b_gate, conv_w, conv_b, conv_ln_g, conv_ln_b, w_conv_out, sgu_ln_g, sgu_ln_b, sgu_w, sgu_b, w_sgu_out, w_mix_out, norm_xattn, norm_mem, w_q, w_kv, w_xo, norm_ffn, w_gu, w_down, norm_final, loss_target, m_norm_mix, m_w_in, m_b_gate, m_conv_w, m_conv_b, m_conv_ln_g, m_conv_ln_b, m_w_conv_out, m_sgu_ln_g, m_sgu_ln_b, m_sgu_w, m_sgu_b, m_w_sgu_out, m_w_mix_out, m_norm_xattn, m_norm_mem, m_w_q, m_w_kv, m_w_xo, m_norm_ffn, m_w_gu, m_w_down, m_norm_final, v_norm_mix, v_w_in, v_b_gate, v_conv_w, v_conv_b, v_conv_ln_g, v_conv_ln_b, v_w_conv_out, v_sgu_ln_g, v_sgu_ln_b, v_sgu_w, v_sgu_b, v_w_sgu_out, v_w_mix_out, v_norm_xattn, v_norm_mem, v_w_q, v_w_kv, v_w_xo, v_norm_ffn, v_w_gu, v_w_down, v_norm_final):
    given = dict(locals())
    bl, s, d = x.shape
    t = bl * s
    xf = x.reshape(t, d)
    tgt = loss_target.reshape(t, d)
    memf = mem.reshape(bl * mem.shape[1], d)
    cx, cy, cc = lax.axis_index("x"), lax.axis_index("y"), lax.axis_index("c")
    dev = 4 * cx + 2 * cy + cc
    dev_id = dev.astype(jnp.int32).reshape(1)
    col_sharded = ["w_in", "w_kv"]
    transposed = ["w_gu"]

    def shard_of(name, prefix=""):
        a = given[prefix + name][0]
        return jnp.transpose(a) if name in transposed else a

    def full_weight(name, blocks):
        return _unblock_cols(blocks) if name in col_sharded else blocks.reshape(N_DEV * blocks.shape[1], blocks.shape[2])

    g_bg, g_cw = _all_gather([_pad_rows(b_gate[0], 8), _pad_rows(conv_w[0], CONV_HALO)], name="gather_small_params")
    h1, p, w_in_blocks = _in_proj_gather(xf, norm_mix + g_bg[0, 7:8, 0:1], w_in[0].astype(BF16), name="in_proj")
    early = ["w_conv_out", "w_sgu_out", "w_mix_out", "w_q", "w_kv", "w_xo"]
    late = ["w_gu", "w_down"]
    shards = {n: shard_of(n).astype(BF16) for n in early + late}
    started = {}
    for grp, names in (("early", early), ("late", late)):
        srcs = [shards[n] for n in names]
        started[grp] = _copy_start("gather", srcs, [(N_DEV, *a.shape) for a in srcs], name=f"gather_{grp}_start", after=p)
    token = started["early"][4][0:1, 0:1] + started["late"][4][0:1, 0:1]
    wfull = {}
    bg_full = _unblock_cols(g_bg)
    cw_full = _unblock_cols(g_cw)

    def finish_gather(grp, names, after):
        ssem, rsem, srcs, lands, _ = started[grp]
        _, lands = _copy_wait("gather", ssem, rsem, srcs, lands, after, name=f"gather_{grp}_wait")
        for n, land in zip(names, lands):
            wfull[n] = full_weight(n, lax.dynamic_update_index_in_dim(land, shards[n], dev, 0))

    tri = jnp.tril(jnp.ones((SGU_CHUNK, SGU_CHUNK), bool))
    wm32 = jnp.where(tri[None], sgu_w[0], 0.0)
    wm = wm32.astype(BF16)
    wmt = jnp.transpose(wm32, (0, 2, 1)).astype(BF16)
    sgu_bias = jnp.broadcast_to(sgu_b[0][:, :, None], (SGU_GROUPS, SGU_CHUNK, d // SGU_GROUPS))

    c_conv, a_act = _conv_fwd(p, cw_full, conv_b + token, conv_ln_g, conv_ln_b, bl=bl, s=s, name="conv_fwd")
    sg, vn = _sgu_fwd(p, wm, sgu_bias, sgu_ln_g, sgu_ln_b + token, name="sgu_fwd")
    finish_gather("early", early, a_act[0:16, 0:128] + sg[0:16, 0:128])
    y_a = _matmul(a_act, wfull["w_conv_out"], mode="nn", out_dtype=BF16, name="mm_conv_out", tm=1024, tn=1024, tk=1024)
    y_b = _matmul(sg, wfull["w_sgu_out"], mode="nn", out_dtype=BF16, name="mm_sgu_out", tm=1024, tn=1024, tk=1024)
    merged, x1, h2, q = _mix_out(p, y_a, y_b, bg_full, xf, wfull["w_mix_out"], norm_xattn, wfull["w_q"], name="mix_out")
    mem_n = _rms_fwd(memf, norm_mem, name="rms_mem")
    kv = _matmul(mem_n, wfull["w_kv"], mode="nn", out_dtype=BF16, name="mm_kv", tm=1024, tn=1024, tk=1024)
    o, x2, h3 = _attn_fwd(q, kv, x1, wfull["w_xo"], norm_ffn, bl=bl, s=s, name="attn_fwd")
    finish_gather("late", late, h3)
    gu, act, dx3, loss_part, d_norm_final = _ffn_fwd(h3, x2, tgt, wfull["w_gu"], wfull["w_down"],
                                                     norm_final.reshape(1, d), name="ffn_fwd")

    grads = {}
    sent = []

    def send_grads(names, tag, after=None):
        blocks, land_shapes = [], []
        for n in names:
            g = grads[n]
            if g.ndim == 2 and n in col_sharded:
                land_shapes.append((N_PEERS, g.shape[0], g.shape[1] // N_DEV))
            else:
                if g.ndim == 2:
                    g = g.reshape(N_DEV, -1, g.shape[1])
                land_shapes.append((N_PEERS, *g.shape[1:]))
            blocks.append(g)
        ssem, rsem, srcs, lands, tok = _copy_start("scatter", blocks, land_shapes, name=f"grads_{tag}_start", after=after)
        sent.append((names, ssem, rsem, srcs, lands))
        return tok[0:1, 0:1]

    dgu, dx2, do, d_norm_ffn = _ffn_bwd(dx3, gu, x2, wfull["w_down"], wfull["w_gu"], norm_ffn, wfull["w_xo"], name="ffn_bwd")
    grads["w_down"] = _matmul(act, dx3, mode="tn", out_dtype=BF16, name="mm_dw_down", tm=1408, tn=1024, tk=1024)
    grads["w_gu"] = _matmul(dgu, h3, mode="tn", out_dtype=BF16, name="mm_dw_gu", tm=1408, tn=1024, tk=1024)
    tok = send_grads(["w_down", "w_gu"], "ffn")
    grads["w_xo"] = _matmul(o, dx2, mode="tn", out_dtype=BF16, name="mm_dw_xo", tm=1024, tn=1024, tk=1024)
    dq, dkv = _attn_bwd(q, kv, do, bl=bl, s=s, name="attn_bwd")
    grads["w_kv"] = _matmul(mem_n, dkv, mode="tn", out_dtype=BF16, name="mm_dw_kv", tm=1024, tn=256, tk=1024,
                            col_blocks=N_DEV)
    tok2 = send_grads(["w_xo", "w_kv"], "attn")
    dmem_n = _matmul(dkv, wfull["w_kv"], mode="nt", out_dtype=F32, name="mm_d_mem", tm=512, tn=1024, tk=2048)
    d_norm_mem = _rms_bwd(None, dmem_n, memf, norm_mem, name="rms_mem_bwd", need_dx=False)
    dx1, d_norm_xattn, dw_q = _proj_rms_bwd(dq, dx2, x1, wfull["w_q"], norm_xattn + (tok + tok2), name="q_rms_bwd", h=h2)
    dp, dy_a, dy_b, d_b_gate, dw_mix, dw_in_gates = _gates_bwd_fused(dx1, p, y_a, y_b, bg_full, wfull["w_mix_out"],
                                                                    merged, h1, name="gates_bwd")
    grads["w_q"] = dw_q.astype(BF16)
    grads["w_mix_out"] = dw_mix.astype(BF16)
    grads["w_sgu_out"] = _matmul(sg, dy_b, mode="tn", out_dtype=BF16, name="mm_dw_sgu", tm=1024, tn=1024, tk=1024)
    dc, d_conv_ln_g, d_conv_ln_b, dw_conv = _conv_ln_bwd_fused(dy_a, c_conv, a_act, wfull["w_conv_out"], conv_ln_g,
                                                               conv_ln_b, name="conv_ln_bwd")
    grads["w_conv_out"] = dw_conv.astype(BF16)
    tok = send_grads(["w_q", "w_mix_out", "w_sgu_out", "w_conv_out"], "mixer")
    dp, d_sgu_w, d_sgu_b, d_sgu_ln_g, d_sgu_ln_b = _sgu_bwd(dp, dy_b, wfull["w_sgu_out"], p, vn, wm, wmt, sgu_bias,
                                                             sgu_ln_g + tok, name="sgu_bwd")
    dw_in_sgu = _matmul(h1, dp, mode="tn", out_dtype=BF16, name="mm_dw_in_sgu", tm=1024, tn=1024, tk=2048,
                        b_cols=(2 * d, 2 * d))
    dp, d_conv_w, d_conv_b, dw_in_conv = _conv_bwd(dp, dc, p, cw_full, h1, bl=bl, s=s, name="conv_bwd")
    grads["w_in"] = jnp.concatenate([dw_in_conv.astype(BF16), dw_in_sgu, dw_in_gates], axis=1)
    tok = send_grads(["w_in"], "in")
    grad_x, d_norm_mix = _proj_rms_bwd(dp, dx1, xf, w_in_blocks, norm_mix + tok, name="in_proj_bwd")
    out = {}

    rep_names = ["norm_mix", "conv_b", "conv_ln_g", "conv_ln_b", "sgu_ln_g", "sgu_ln_b", "norm_xattn", "norm_mem",
                 "norm_ffn", "norm_final", "sgu_b"]
    rep_grads = [d_norm_mix, d_conv_b, d_conv_ln_g, d_conv_ln_b, d_sgu_ln_g, d_sgu_ln_b, d_norm_xattn, d_norm_mem,
                 d_norm_ffn, d_norm_final, d_sgu_b.reshape(1, d)]
    nrep = len(rep_names)
    pad = jnp.zeros((16 - nrep, d), F32)
    sgw_rows = SGU_GROUPS * SGU_CHUNK * SGU_CHUNK // d

    def pack_rep(vecs, sgw, extra=None):
        fill = pad if extra is None else jnp.concatenate([extra, pad[1:]], axis=0)
        return jnp.concatenate([v.reshape(1, d) for v in vecs] + [fill, sgw.reshape(sgw_rows, d)], axis=0)

    def pack_col(bg, cw):
        return jnp.concatenate([_pad_rows(bg, 8), _pad_rows(cw, CONV_HALO)], axis=0)

    small_a = pack_rep(rep_grads, d_sgu_w, extra=jnp.broadcast_to(loss_part, (1, d)))
    small_b = jnp.concatenate([d_b_gate, d_conv_w], axis=0)
    parts_a, parts_b = _all_gather([small_a, small_b], name="gather_small_grads")
    res_a = _adamw_small(parts_a, dev_id, pack_rep([given[n] for n in rep_names], sgu_w),
                         pack_rep([given["m_" + n] for n in rep_names], m_sgu_w),
                         pack_rep([given["v_" + n] for n in rep_names], v_sgu_w), name="adamw_small", col_block=False)
    res_b = _adamw_small(parts_b, dev_id, pack_col(b_gate[0], conv_w[0]), pack_col(m_b_gate[0], m_conv_w[0]),
                         pack_col(v_b_gate[0], v_conv_w[0]), name="adamw_small_cols", col_block=True)
    for i, n in enumerate(rep_names):
        out[n] = [r[i].reshape(given[n].shape) for r in res_a]
    out["sgu_w"] = [r[16:16 + sgw_rows].reshape(sgu_w.shape) for r in res_a]
    out["b_gate"] = [r[0:2][None] for r in res_b]
    out["conv_w"] = [r[8:8 + CONV_WIDTH][None] for r in res_b]

    done = res_a[0]
    for names, ssem, rsem, srcs, lands in sent:
        srcs, lands = _copy_wait("scatter", ssem, rsem, srcs, lands, done, name=f"grads_{names[0]}_wait")
        for n, partials, landed in zip(names, srcs, lands):
            res = _adamw_shard(partials, landed, dev_id, shard_of(n), shard_of(n, "m_"), shard_of(n, "v_"),
                               name=f"adamw_{n}")
            done = res[0]
            out[n] = [(jnp.transpose(r) if n in transposed else r)[None] for r in res]

    order = ["norm_mix", "w_in", "b_gate", "conv_w", "conv_b", "conv_ln_g", "conv_ln_b", "w_conv_out", "sgu_ln_g",
             "sgu_ln_b", "sgu_w", "sgu_b", "w_sgu_out", "w_mix_out", "norm_xattn", "norm_mem", "w_q", "w_kv", "w_xo",
             "norm_ffn", "w_gu", "w_down", "norm_final"]
    loss = res_a[0][nrep, 0]
    return (loss, grad_x.reshape(x.shape), *[out[n][0] for n in order], *[out[n][1] for n in order],
            *[out[n][2] for n in order], *[out[n][3] for n in order])
```

```python
import functools

import jax
import jax.numpy as jnp
from jax import lax
from jax.experimental import pallas as pl
from jax.experimental.pallas import tpu as pltpu

F32 = jnp.float32
BF16 = jnp.bfloat16
RMS_EPS = 1e-6
LN_EPS = 1e-5
CONV_WIDTH = 31
CONV_HALO = 32
CONV_ROWS = 64
CONV_COLS = 256
LANES = 128
SGU_CHUNK = 128
SGU_GROUPS = 8
SGU_TILE = 512
HEADS = 4
N_DEV = 8
ADAM_LR, ADAM_B1, ADAM_B2, ADAM_EPS, ADAM_WD, ADAM_STEP = 0.001, 0.9, 0.999, 1e-08, 0.01, 10
VMEM_LIMIT = 56 * 1024 * 1024
TOKEN_TILE = 256
ATTN_TILE = 1024
MESH_ID = pl.DeviceIdType.MESH

_GELU_K = 0.7978845608028654
_GELU_C = 0.044715


def _cparams(sem=None):
    return pltpu.CompilerParams(dimension_semantics=sem, vmem_limit_bytes=VMEM_LIMIT)


def _sigmoid(v):
    return 0.5 * jnp.tanh(0.5 * v) + 0.5


def _gelu(v):
    return 0.5 * v * (1.0 + jnp.tanh(_GELU_K * (v + _GELU_C * v * v * v)))


def _gelu_grad(v):
    th = jnp.tanh(_GELU_K * (v + _GELU_C * v * v * v))
    return 0.5 * (1.0 + th) + 0.5 * v * (1.0 - th * th) * _GELU_K * (1.0 + 3.0 * _GELU_C * v * v)


def _dot(a, b, dims):
    return lax.dot_general(a, b, (dims, ((), ())), preferred_element_type=F32)


_NN = ((1,), (0,))
_NT = ((1,), (1,))
_TN = ((0,), (0,))


def _matmul(a, b, *, mode, out_dtype, name, tm=512, tn=512, tk=512, chunk=None, residual=None, rms_gain=None,
            col_blocks=None, b_cols=None):
    if mode == "nn":
        (m, k), (_, n) = a.shape, b.shape
    elif mode == "nt":
        (m, k), (n, _) = a.shape, b.shape
    else:
        (k, m), (_, n) = a.shape, b.shape
    b_first = 0
    if b_cols is not None:
        assert mode == "tn"
        b_first, n = b_cols
    tm, tn, tk = min(tm, m), min(tn, n), min(tk, k)
    assert b_first % tn == 0
    b_first //= tn
    assert m % tm == 0 and n % tn == 0 and k % tk == 0, (name, a.shape, b.shape, tm, tn, tk)
    nk = k // tk
    dims = {"nn": _NN, "nt": _NT, "tn": _TN}[mode]
    chunk = tn if chunk is None else min(chunk, tn)
    assert tn % chunk == 0
    if rms_gain is not None:
        assert tn == n and chunk == n

    def body(*refs):
        refs = list(refs)
        a_ref, b_ref = refs[:2]
        pos = 2
        r_ref = g_ref = None
        if residual is not None:
            r_ref = refs[pos]
            pos += 1
        if rms_gain is not None:
            g_ref = refs[pos]
            pos += 1
        o_ref = refs[pos]
        pos += 1
        h_ref = None
        if rms_gain is not None:
            h_ref = refs[pos]
            pos += 1
        acc_ref = refs[pos] if nk > 1 else None
        av = a_ref[...].astype(BF16)
        for c0 in range(0, tn, chunk):
            cs = slice(c0, c0 + chunk)
            bv = (b_ref[cs, :] if mode == "nt" else b_ref[:, cs]).astype(BF16)
            part = _dot(av, bv, dims)

            def finish(res, cs=cs):
                if r_ref is not None:
                    res = res + r_ref[:, cs].astype(F32)
                o_ref[:, cs] = res.astype(out_dtype)
                if h_ref is not None:
                    r = lax.rsqrt(jnp.mean(res * res, axis=-1, keepdims=True) + RMS_EPS)
                    h_ref[...] = (res * r * g_ref[...]).astype(BF16)

            if nk == 1:
                finish(part)
            else:
                kk = pl.program_id(2)

                @pl.when(kk == 0)
                def _(part=part, cs=cs):
                    acc_ref[:, cs] = part

                @pl.when(kk > 0)
                def _(part=part, cs=cs):
                    acc_ref[:, cs] += part

                @pl.when(kk == nk - 1)
                def _(finish=finish, cs=cs):
                    finish(acc_ref[:, cs])

    resident = dict(pipeline_mode=pl.Buffered(1)) if (n == tn and nk == 1 and mode != "tn" and m > tm) else {}
    if mode == "nn":
        a_spec = pl.BlockSpec((tm, tk), lambda i, j, kk: (i, kk))
        b_spec = pl.BlockSpec((tk, tn), lambda i, j, kk: (kk, j), **resident)
    elif mode == "nt":
        a_spec = pl.BlockSpec((tm, tk), lambda i, j, kk: (i, kk))
        b_spec = pl.BlockSpec((tn, tk), lambda i, j, kk: (j, kk), **resident)
    else:
        a_spec = pl.BlockSpec((tk, tm), lambda i, j, kk: (kk, i))
        b_spec = pl.BlockSpec((tk, tn), lambda i, j, kk: (kk, j + b_first))
    o_spec = pl.BlockSpec((tm, tn), lambda i, j, kk: (i, j))
    in_specs, args = [a_spec, b_spec], [a, b]
    if residual is not None:
        in_specs.append(o_spec)
        args.append(residual)
    out_shape, out_specs = [jax.ShapeDtypeStruct((m, n), out_dtype)], [o_spec]
    if col_blocks is not None:
        assert residual is None and rms_gain is None and (n // col_blocks) % tn == 0
        per = n // col_blocks // tn
        out_shape = [jax.ShapeDtypeStruct((col_blocks, m, n // col_blocks), out_dtype)]
        out_specs = [pl.BlockSpec((None, tm, tn), lambda i, j, kk: (j // per, i, j % per))]
    if rms_gain is not None:
        in_specs.append(pl.BlockSpec((1, n), lambda i, j, kk: (0, 0)))
        args.append(rms_gain)
        out_shape.append(jax.ShapeDtypeStruct((m, n), BF16))
        out_specs.append(o_spec)
    res = pl.pallas_call(
        body, name=name, grid=(m // tm, n // tn, nk), in_specs=in_specs, out_specs=out_specs, out_shape=out_shape,
        scratch_shapes=[pltpu.VMEM((tm, tn), F32)] if nk > 1 else [],
        compiler_params=_cparams(("parallel", "parallel", "arbitrary")),
    )(*args)
    return res if rms_gain is not None else res[0]


def _row_call(name, t, tm, rows_in, residents, rows_out, accs, body):
    n_in, n_res, n_out, n_acc = len(rows_in), len(residents), len(rows_out), len(accs)
    steps = t // tm
    assert t % tm == 0
    narrow = [i for i, (_, dt) in enumerate(accs) if dt != F32]

    def kernel_body(*refs):
        in_refs, res_refs = refs[:n_in], refs[n_in:n_in + n_res]
        out_refs = refs[n_in + n_res:n_in + n_res + n_out]
        acc_out = list(refs[n_in + n_res + n_out:n_in + n_res + n_out + n_acc])
        scratch = refs[n_in + n_res + n_out + n_acc:]
        acc_refs = list(acc_out)
        for s_ref, i in zip(scratch, narrow):
            acc_refs[i] = s_ref
        if accs:
            @pl.when(pl.program_id(0) == 0)
            def _():
                for acc in acc_refs:
                    acc[...] = jnp.zeros_like(acc)
        body(in_refs, res_refs, out_refs, acc_refs)
        if narrow:
            @pl.when(pl.program_id(0) == steps - 1)
            def _():
                for i in narrow:
                    acc_out[i][...] = acc_refs[i][...].astype(acc_out[i].dtype)

    once = dict(pipeline_mode=pl.Buffered(1)) if steps > 1 else {}
    in_specs = [pl.BlockSpec((tm, cols), lambda i, cb=cb: (i, cb)) for _, cols, cb in rows_in]
    in_specs += [pl.BlockSpec(r.shape, lambda i, nd=r.ndim: (0,) * nd, **once) for r in residents]
    out_specs = [pl.BlockSpec((tm, cols), lambda i, cb=cb: (i, cb)) for _, cols, cb, _ in rows_out]
    out_specs += [pl.BlockSpec(shape, lambda i, nd=len(shape): (0,) * nd) for shape, _ in accs]
    out_shape = [jax.ShapeDtypeStruct((t, total), dt) for total, _, _, dt in rows_out]
    out_shape += [jax.ShapeDtypeStruct(shape, dt) for shape, dt in accs]
    return pl.pallas_call(
        kernel_body, name=name, grid=(steps,), in_specs=in_specs, out_specs=out_specs, out_shape=out_shape,
        scratch_shapes=[pltpu.VMEM(accs[i][0], F32) for i in narrow],
        compiler_params=_cparams(("arbitrary",) if accs else ("parallel",)),
    )(*[a for a, _, _ in rows_in], *residents)


def _rms_apply(xv, gain):
    return xv * lax.rsqrt(jnp.mean(xv * xv, axis=-1, keepdims=True) + RMS_EPS) * gain


def _rms_grad(dres, dh, xv, gain):
    r = lax.rsqrt(jnp.mean(xv * xv, axis=-1, keepdims=True) + RMS_EPS)
    xhat = xv * r
    dxh = dh * gain
    dx = dres + r * (dxh - xhat * jnp.mean(dxh * xhat, axis=-1, keepdims=True))
    return dx, jnp.sum(dh * xhat, axis=0, keepdims=True)


def _in_proj_gather(xf, gain, w_shard, *, name):
    t, d = xf.shape
    cb = w_shard.shape[1]
    tm = min(1024, t)
    steps = t // tm
    mx, my, _ = _mesh_pos()
    order = jnp.stack([2 * mx + my, 2 * (1 - mx) + my, 2 * mx + (1 - my), 2 * (1 - mx) + (1 - my)]).astype(jnp.int32)

    def body(order_ref, x_ref, g_ref, ws_ref, h_ref, p_ref, wout_ref, w_ref, send_sems, recv_sems, own_sem):
        ps, i = pl.program_id(0), pl.program_id(1)
        x, y, c = _mesh_pos()
        me, sib = (x, y, c), (x, y, 1 - c)
        chips = [(1 - x, y), (x, 1 - y), (1 - x, 1 - y)]

        def copy(k, block, to, from_shard=False):
            return pltpu.make_async_remote_copy(
                src_ref=ws_ref if from_shard else w_ref.at[_dev_index(block)], dst_ref=w_ref.at[_dev_index(block)],
                send_sem=send_sems.at[k], recv_sem=recv_sems.at[k], device_id=to, device_id_type=MESH_ID)

        own = pltpu.make_async_copy(ws_ref, w_ref.at[_dev_index(me)], own_sem)
        first = [copy(0, me, sib, True)] + [copy(1 + j, me, (*chip, c), True) for j, chip in enumerate(chips)]
        passed = [copy(4 + j, (*chip, c), sib) for j, chip in enumerate(chips)]

        @pl.when(jnp.logical_and(ps == 0, i == 0))
        def _():
            own.start()
            for cp in first:
                cp.start()
            own.wait()
            copy(0, sib, me).wait_recv()

        for j, chip in enumerate(chips):
            @pl.when(jnp.logical_and(ps == j + 1, i == 0))
            def _(j=j, chip=chip):
                copy(1 + j, (*chip, c), me).wait_recv()
                passed[j].start()
                copy(4 + j, (*chip, 1 - c), me).wait_recv()

        h = _rms_apply(x_ref[...], g_ref[...]).astype(BF16)
        h_ref[...] = h
        chip_id = order_ref[ps]
        p_ref[:, 0:cb] = _dot(h, w_ref[2 * chip_id], _NN).astype(BF16)
        p_ref[:, cb:2 * cb] = _dot(h, w_ref[2 * chip_id + 1], _NN).astype(BF16)

        @pl.when(jnp.logical_and(ps == 3, i == steps - 1))
        def _():
            for cp in first + passed:
                cp.wait_send()
            keep = pltpu.make_async_copy(w_ref, wout_ref, own_sem)
            keep.start()
            keep.wait()

    gs = pltpu.PrefetchScalarGridSpec(
        num_scalar_prefetch=1, grid=(4, steps),
        in_specs=[pl.BlockSpec((tm, d), lambda ps, i, o: (i, 0)), pl.BlockSpec((1, d), lambda ps, i, o: (0, 0)),
                  pl.BlockSpec(memory_space=pl.ANY)],
        out_specs=[pl.BlockSpec((tm, d), lambda ps, i, o: (jnp.where(ps == 0, i, steps - 1), 0)),
                   pl.BlockSpec((tm, 2 * cb), lambda ps, i, o: (i, o[ps])), pl.BlockSpec(memory_space=pl.ANY)],
        scratch_shapes=[pltpu.VMEM((N_DEV, d, cb), BF16), pltpu.SemaphoreType.DMA((7,)), pltpu.SemaphoreType.DMA((7,)),
                        pltpu.SemaphoreType.DMA(())])
    return pl.pallas_call(
        body, name=name, grid_spec=gs,
        out_shape=[jax.ShapeDtypeStruct((t, d), BF16), jax.ShapeDtypeStruct((t, N_DEV * cb), BF16),
                   jax.ShapeDtypeStruct((N_DEV, d, cb), BF16)],
        compiler_params=_cparams(("arbitrary", "arbitrary")))(order, xf, gain, w_shard)


def _mix_out(p, y_a, y_b, b_gate, xf, w_mix, gain, w_q, *, name):
    t, d = xf.shape

    def body(ins, res, outs, accs):
        ga_ref, gb_ref, ya_ref, yb_ref, x_ref = ins
        bg_ref, wm_ref, g_ref, wq_ref = res
        m_ref, x1_ref, h_ref, q_ref = outs
        sa = _sigmoid(ga_ref[...].astype(F32) + bg_ref[0:1, :])
        sb = _sigmoid(gb_ref[...].astype(F32) + bg_ref[1:2, :])
        merged = (sa * ya_ref[...].astype(F32) + sb * yb_ref[...].astype(F32)).astype(BF16)
        m_ref[...] = merged
        x1 = x_ref[...] + _dot(merged, wm_ref[...], _NN)
        x1_ref[...] = x1
        h = _rms_apply(x1, g_ref[...]).astype(BF16)
        h_ref[...] = h
        q_ref[...] = _dot(h, wq_ref[...], _NN).astype(BF16)

    return _row_call(name, t, min(512, t), [(p, d, 4), (p, d, 5), (y_a, d, 0), (y_b, d, 0), (xf, d, 0)],
                     [b_gate, w_mix, gain, w_q], [(d, d, 0, BF16), (d, d, 0, F32), (d, d, 0, BF16), (d, d, 0, BF16)], [], body)


def _ffn_fwd(h3, x2, target, w_gu_t, w_down, gain, *, name):
    t, d = x2.shape
    f2 = w_gu_t.shape[0]
    f = f2 // 2
    half = f // 2

    def body(ins, res, outs, accs):
        h_ref, x2_ref, t_ref = ins
        wgu_ref, wd_ref, g_ref = res
        gu_ref, act_ref, dx_ref = outs
        loss_ref, dg_ref = accs
        h = h_ref[...]
        x3 = x2_ref[...]
        for c0 in (0, half):
            gt = _dot(h, wgu_ref[c0:c0 + half, :], _NT).astype(BF16)
            up = _dot(h, wgu_ref[f + c0:f + c0 + half, :], _NT).astype(BF16)
            gu_ref[:, c0:c0 + half] = gt
            gu_ref[:, f + c0:f + c0 + half] = up
            gtf = gt.astype(F32)
            act = (gtf * _sigmoid(gtf) * up.astype(F32)).astype(BF16)
            act_ref[:, c0:c0 + half] = act
            x3 = x3 + _dot(act, wd_ref[c0:c0 + half, :], _NN)
        g = g_ref[...]
        r = lax.rsqrt(jnp.mean(x3 * x3, axis=-1, keepdims=True) + RMS_EPS)
        xhat = x3 * r
        err = xhat * g - t_ref[...]
        loss_ref[...] += 0.5 * jnp.sum(jnp.mean(err * err, axis=-1, keepdims=True), axis=0, keepdims=True)
        dy = err * (1.0 / d)
        dg_ref[...] += jnp.sum(dy * xhat, axis=0, keepdims=True)
        dxh = dy * g
        dx_ref[...] = r * (dxh - xhat * jnp.mean(dxh * xhat, axis=-1, keepdims=True))

    return _row_call(name, t, min(256, t), [(h3, d, 0), (x2, d, 0), (target, d, 0)], [w_gu_t, w_down, gain],
                     [(f2, f2, 0, BF16), (f, f, 0, BF16), (d, d, 0, F32)], [((1, 1), F32), ((1, d), F32)], body)


def _ffn_bwd(dx3, gu, x2, w_down, w_gu_t, gain, w_xo, *, name):
    t, d = x2.shape
    f2 = w_gu_t.shape[0]
    f = f2 // 2
    half = f // 2

    def body(ins, res, outs, accs):
        dx3_ref, gu_ref, x2_ref = ins
        wd_ref, wgu_ref, g_ref, wxo_ref = res
        dgu_ref, dx2_ref, do_ref = outs
        (dg_ref,) = accs
        dx3v = dx3_ref[...]
        dxb = dx3v.astype(BF16)
        dh = jnp.zeros(dx3v.shape, F32)
        for c0 in (0, half):
            dact = _dot(dxb, wd_ref[c0:c0 + half, :], _NT)
            gt = gu_ref[:, c0:c0 + half].astype(F32)
            up = gu_ref[:, f + c0:f + c0 + half].astype(F32)
            sg = _sigmoid(gt)
            dgt = (dact * up * sg * (1.0 + gt * (1.0 - sg))).astype(BF16)
            dup = (dact * gt * sg).astype(BF16)
            dgu_ref[:, c0:c0 + half] = dgt
            dgu_ref[:, f + c0:f + c0 + half] = dup
            dh = dh + _dot(dgt, wgu_ref[c0:c0 + half, :], _NN) + _dot(dup, wgu_ref[f + c0:f + c0 + half, :], _NN)
        dx2, dg = _rms_grad(dx3v, dh, x2_ref[...], g_ref[...])
        dx2_ref[...] = dx2
        dg_ref[...] += dg
        do_ref[...] = _dot(dx2.astype(BF16), wxo_ref[...], _NT).astype(BF16)

    return _row_call(name, t, min(256, t), [(dx3, d, 0), (gu, f2, 0), (x2, d, 0)], [w_down, w_gu_t, gain, w_xo],
                     [(f2, f2, 0, BF16), (d, d, 0, F32), (d, d, 0, BF16)], [((1, d), F32)], body)


def _proj_rms_bwd(dy, dres, x, w, gain, *, name, h=None):
    t, d = x.shape
    k = dy.shape[1]

    def body(ins, res, outs, accs):
        dy_ref, dres_ref, x_ref = ins[:3]
        w_ref, g_ref = res
        if h is not None:
            accs[1][...] += _dot(ins[3][...], dy_ref[...], _TN)
        if w.ndim == 3:
            cb = w.shape[2]
            dh = _dot(dy_ref[:, 0:cb], w_ref[0], _NT)
            for j in range(1, w.shape[0]):
                dh = dh + _dot(dy_ref[:, j * cb:(j + 1) * cb], w_ref[j], _NT)
        else:
            dh = _dot(dy_ref[...], w_ref[...], _NT)
        dx, dg = _rms_grad(dres_ref[...], dh, x_ref[...], g_ref[...])
        outs[0][...] = dx
        accs[0][...] += dg

    rows_in = [(dy, k, 0), (dres, d, 0), (x, d, 0)] + ([(h, d, 0)] if h is not None else [])
    accs = [((1, d), F32)] + ([((d, k), BF16)] if h is not None else [])
    return _row_call(name, t, min(512, t), rows_in, [w, gain], [(d, d, 0, F32)], accs, body)


def _gates_bwd_fused(dx1, p, y_a, y_b, b_gate, w_mix, merged, h1, *, name):
    t, d = y_a.shape

    def body(ins, res, outs, accs):
        dx_ref, ga_ref, gb_ref, ya_ref, yb_ref, m_ref, h1_ref = ins
        bg_ref, wm_ref = res
        dp_ref, dya_ref, dyb_ref = outs
        dbg_ref, dwm_ref, dwin_ref = accs
        dxb = dx_ref[...].astype(BF16)
        dwm_ref[...] += _dot(m_ref[...], dxb, _TN)
        dm = _dot(dxb, wm_ref[...], _NT)
        sa = _sigmoid(ga_ref[...].astype(F32) + bg_ref[0:1, :])
        sb = _sigmoid(gb_ref[...].astype(F32) + bg_ref[1:2, :])
        dya_ref[...] = (dm * sa).astype(BF16)
        dyb_ref[...] = (dm * sb).astype(BF16)
        dga = dm * ya_ref[...].astype(F32) * sa * (1.0 - sa)
        dgb = dm * yb_ref[...].astype(F32) * sb * (1.0 - sb)
        dp_ref[:, 0:d] = dga.astype(BF16)
        dp_ref[:, d:2 * d] = dgb.astype(BF16)
        dbg_ref[0:1, :] += jnp.sum(dga, axis=0, keepdims=True)
        dbg_ref[1:2, :] += jnp.sum(dgb, axis=0, keepdims=True)
        dwin_ref[...] += _dot(h1_ref[...], dp_ref[...], _TN)

    return _row_call(name, t, min(256, t),
                     [(dx1, d, 0), (p, d, 4), (p, d, 5), (y_a, d, 0), (y_b, d, 0), (merged, d, 0), (h1, d, 0)],
                     [b_gate, w_mix], [(p.shape[1], 2 * d, 2, BF16), (d, d, 0, BF16), (d, d, 0, BF16)],
                     [((8, d), F32), ((d, d), BF16), ((d, 2 * d), BF16)], body)


def _conv_ln_bwd_fused(dy_a, c, a_act, w_conv_out, ln_g, ln_b, *, name):
    t, d = c.shape

    def body(ins, res, outs, accs):
        dy_ref, c_ref, act_ref = ins
        w_ref, lg_ref, lb_ref = res
        dlg_ref, dlb_ref, dw_ref = accs
        dw_ref[...] += _dot(act_ref[...], dy_ref[...], _TN)
        dact = _dot(dy_ref[...], w_ref[...], _NT)
        cv = c_ref[...].astype(F32)
        g = lg_ref[...]
        mu = jnp.mean(cv, axis=-1, keepdims=True)
        dv = cv - mu
        rstd = lax.rsqrt(jnp.mean(dv * dv, axis=-1, keepdims=True) + LN_EPS)
        chat = dv * rstd
        aln = chat * g + lb_ref[...]
        sg = _sigmoid(aln)
        daln = dact * (sg * (1.0 + aln * (1.0 - sg)))
        dlb_ref[...] += jnp.sum(daln, axis=0, keepdims=True)
        dlg_ref[...] += jnp.sum(daln * chat, axis=0, keepdims=True)
        dchat = daln * g
        dc = rstd * (dchat - jnp.mean(dchat, axis=-1, keepdims=True)
                     - chat * jnp.mean(dchat * chat, axis=-1, keepdims=True))
        outs[0][...] = dc.astype(BF16)

    return _row_call(name, t, min(512, t), [(dy_a, d, 0), (c, d, 0), (a_act, d, 0)], [w_conv_out, ln_g, ln_b],
                     [(d, d, 0, BF16)], [((1, d), F32), ((1, d), F32), ((d, d), BF16)], body)


def _row_spec(tt, cols, col_block=0):
    return pl.BlockSpec((tt, cols), lambda i: (i, col_block))


def _const_spec(shape):
    return pl.BlockSpec(shape, lambda *_: (0,) * len(shape))


def _rms_fwd(x, gain, *, name):
    t, d = x.shape
    tt = min(TOKEN_TILE, t)

    def body(x_ref, g_ref, h_ref):
        xv = x_ref[...]
        r = lax.rsqrt(jnp.mean(xv * xv, axis=-1, keepdims=True) + RMS_EPS)
        h_ref[...] = (xv * r * g_ref[...]).astype(BF16)

    return pl.pallas_call(
        body, name=name, grid=(t // tt,), in_specs=[_row_spec(tt, d), _const_spec((1, d))],
        out_specs=_row_spec(tt, d), out_shape=jax.ShapeDtypeStruct((t, d), BF16),
        compiler_params=_cparams(("parallel",)))(x, gain)


def _rms_bwd(dres, dh, x, gain, *, name, need_dx=True):
    t, d = x.shape
    tt = min(TOKEN_TILE, t)

    def body(*refs):
        if need_dx:
            dres_ref, dh_ref, x_ref, g_ref, dx_ref, dg_ref = refs
        else:
            dh_ref, x_ref, g_ref, dg_ref = refs

        @pl.when(pl.program_id(0) == 0)
        def _():
            dg_ref[...] = jnp.zeros_like(dg_ref)

        xv = x_ref[...]
        dhv = dh_ref[...].astype(F32)
        r = lax.rsqrt(jnp.mean(xv * xv, axis=-1, keepdims=True) + RMS_EPS)
        xhat = xv * r
        dg_ref[...] += jnp.sum(dhv * xhat, axis=0, keepdims=True)
        if need_dx:
            dxh = dhv * g_ref[...]
            dx_ref[...] = dres_ref[...] + r * (dxh - xhat * jnp.mean(dxh * xhat, axis=-1, keepdims=True))

    rs = _row_spec(tt, d)
    if need_dx:
        in_specs, args = [rs, rs, rs, _const_spec((1, d))], (dres, dh, x, gain)
        out_specs = [rs, _const_spec((1, d))]
        out_shape = [jax.ShapeDtypeStruct((t, d), F32), jax.ShapeDtypeStruct((1, d), F32)]
    else:
        in_specs, args = [rs, rs, _const_spec((1, d))], (dh, x, gain)
        out_specs = [_const_spec((1, d))]
        out_shape = [jax.ShapeDtypeStruct((1, d), F32)]
    res = pl.pallas_call(body, name=name, grid=(t // tt,), in_specs=in_specs, out_specs=out_specs, out_shape=out_shape,
                         compiler_params=_cparams(("arbitrary",)))(*args)
    return res if need_dx else res[0]


SUBLANES = 8
SHIFT_ROWS = 40


def _conv_apply(sbuf_ref, w_ref, out_ref, tt, offsets, bias_ref=None):
    d = out_ref.shape[1]
    for cc in range(d // LANES):
        cs = slice(cc * LANES, (cc + 1) * LANES)
        taps = [jnp.broadcast_to(w_ref[k:k + 1, cs], (SUBLANES, LANES)) for k in range(CONV_WIDTH)]
        bias = None if bias_ref is None else jnp.broadcast_to(bias_ref[:, cs], (SUBLANES, LANES))

        def row_body(r, carry, cs=cs, taps=taps, bias=bias):
            r0 = pl.multiple_of(r * CONV_ROWS, CONV_ROWS)
            for q in range(CONV_ROWS // SUBLANES):
                acc = _tap(sbuf_ref, r0 + q * SUBLANES, cs, offsets[0]) * taps[0]
                for k in range(1, CONV_WIDTH):
                    acc = acc + _tap(sbuf_ref, r0 + q * SUBLANES, cs, offsets[k]) * taps[k]
                if bias is not None:
                    acc = acc + bias
                out_ref[pl.ds(r0 + q * SUBLANES, SUBLANES), cs] = acc
            return carry

        lax.fori_loop(0, tt // CONV_ROWS, row_body, 0)


def _fill_shifts(sbuf_ref, rows):
    d = sbuf_ref.shape[2]
    assert rows % SHIFT_ROWS == 0

    def row_body(i, carry):
        r0 = pl.multiple_of(i * SHIFT_ROWS, SUBLANES)
        for cc in range(d // CONV_COLS):
            cs = slice(cc * CONV_COLS, (cc + 1) * CONV_COLS)
            win = sbuf_ref[0, pl.ds(r0, SHIFT_ROWS + SUBLANES), cs]
            for sh in range(1, SUBLANES):
                sbuf_ref[sh, pl.ds(r0, SHIFT_ROWS), cs] = win[sh:sh + SHIFT_ROWS, :]
        return carry

    lax.fori_loop(0, rows // SHIFT_ROWS, row_body, 0)


def _tap(sbuf_ref, r0, cs, offset):
    sh = offset % SUBLANES
    return sbuf_ref[sh, pl.ds(pl.multiple_of(r0 + (offset - sh), SUBLANES), SUBLANES), cs]


def _conv_specs(bl, s, tt, d, col_a, col_g):
    nj = s // tt
    per = tt // CONV_HALO
    main_a = pl.BlockSpec((tt, d), lambda b, j: (b * nj + j, col_a))
    main_g = pl.BlockSpec((tt, d), lambda b, j: (b * nj + j, col_g))
    prev = lambda b, j: jnp.maximum((b * nj + j) * per - 1, 0)
    halo_a = pl.BlockSpec((CONV_HALO, d), lambda b, j: (prev(b, j), col_a))
    halo_g = pl.BlockSpec((CONV_HALO, d), lambda b, j: (prev(b, j), col_g))
    return main_a, main_g, halo_a, halo_g


def _fill_glu(sbuf_ref, a_ref, g_ref, ha_ref, hg_ref, tt):
    first = pl.program_id(1) == 0
    ha = ha_ref[...].astype(F32)
    hg = hg_ref[...].astype(F32)
    sbuf_ref[0, pl.ds(0, CONV_HALO), :] = jnp.where(first, 0.0, ha * _sigmoid(hg))
    av = a_ref[...].astype(F32)
    gv = g_ref[...].astype(F32)
    sbuf_ref[0, pl.ds(CONV_HALO, tt), :] = av * _sigmoid(gv)
    _fill_shifts(sbuf_ref, tt + CONV_HALO - SUBLANES)


def _conv_fwd(p, conv_w, conv_b, ln_g, ln_b, *, bl, s, name):
    t = p.shape[0]
    d = conv_w.shape[1]
    tt = min(TOKEN_TILE, s)
    off = CONV_HALO - (CONV_WIDTH - 1)

    def body(a_ref, g_ref, ha_ref, hg_ref, w_ref, b_ref, lg_ref, lb_ref, c_ref, act_ref, sbuf_ref, cbuf_ref):
        _fill_glu(sbuf_ref, a_ref, g_ref, ha_ref, hg_ref, tt)

        _conv_apply(sbuf_ref, w_ref, cbuf_ref, tt, [off + k for k in range(CONV_WIDTH)], bias_ref=b_ref)
        cv = cbuf_ref[...]
        c_ref[...] = cv.astype(BF16)
        mu = jnp.mean(cv, axis=-1, keepdims=True)
        dv = cv - mu
        rstd = lax.rsqrt(jnp.mean(dv * dv, axis=-1, keepdims=True) + LN_EPS)
        aln = dv * rstd * lg_ref[...] + lb_ref[...]
        act_ref[...] = (aln * _sigmoid(aln)).astype(BF16)

    main_a, main_g, halo_a, halo_g = _conv_specs(bl, s, tt, d, 0, 1)
    out_spec = pl.BlockSpec((tt, d), lambda b, j: (b * (s // tt) + j, 0))
    return pl.pallas_call(
        body, name=name, grid=(bl, s // tt),
        in_specs=[main_a, main_g, halo_a, halo_g, _const_spec((CONV_HALO, d)), _const_spec((1, d)), _const_spec((1, d)),
                  _const_spec((1, d))],
        out_specs=[out_spec, out_spec],
        out_shape=[jax.ShapeDtypeStruct((t, d), BF16), jax.ShapeDtypeStruct((t, d), BF16)],
        scratch_shapes=[pltpu.VMEM((SUBLANES, tt + CONV_HALO, d), F32), pltpu.VMEM((tt, d), F32)],
        compiler_params=_cparams(("parallel", "parallel")))(p, p, p, p, conv_w, conv_b, ln_g, ln_b)


def _conv_bwd(dp, dc, p, conv_w, h1, *, bl, s, name):
    t = p.shape[0]
    d = conv_w.shape[1]
    tt = min(TOKEN_TILE, s)
    nj = s // tt
    per = tt // CONV_HALO
    off = CONV_HALO - (CONV_WIDTH - 1)
    last_blk = t // CONV_HALO - 1

    def body(dp_in, dc_ref, dcn_ref, a_ref, g_ref, ha_ref, hg_ref, w_ref, h1_ref, dp_ref, dw_ref, db_ref, dwin_ref,
             gbuf_ref, dbuf_ref, dglu_ref, acc_ref):
        del dp_in
        b, j = pl.program_id(0), pl.program_id(1)
        start = jnp.logical_and(b == 0, j == 0)
        end = jnp.logical_and(b == bl - 1, j == nj - 1)

        @pl.when(start)
        def _():
            acc_ref[...] = jnp.zeros_like(acc_ref)
            db_ref[...] = jnp.zeros_like(db_ref)
            dwin_ref[...] = jnp.zeros_like(dwin_ref)

        _fill_glu(gbuf_ref, a_ref, g_ref, ha_ref, hg_ref, tt)
        dcv = dc_ref[...].astype(F32)
        dbuf_ref[0, pl.ds(0, tt), :] = dcv
        dbuf_ref[0, pl.ds(tt, CONV_HALO), :] = jnp.where(j == nj - 1, 0.0, dcn_ref[...].astype(F32))
        _fill_shifts(dbuf_ref, tt + CONV_HALO - SUBLANES)
        db_ref[...] += jnp.sum(dcv, axis=0, keepdims=True)

        for cc in range(d // LANES):
            cs = slice(cc * LANES, (cc + 1) * LANES)

            def row_body(r, accs, cs=cs):
                r0 = pl.multiple_of(r * CONV_ROWS, CONV_ROWS)
                accs = list(accs)
                for q in range(CONV_ROWS // SUBLANES):
                    dcw = dbuf_ref[0, pl.ds(r0 + q * SUBLANES, SUBLANES), cs]
                    for k in range(CONV_WIDTH):
                        accs[k] = accs[k] + dcw * _tap(gbuf_ref, r0 + q * SUBLANES, cs, off + k)
                return tuple(accs)

            zero = jnp.zeros((SUBLANES, LANES), F32)
            accs = lax.fori_loop(0, tt // CONV_ROWS, row_body, (zero,) * CONV_WIDTH)
            for k in range(CONV_WIDTH):
                acc_ref[k, :, cs] += accs[k]

        _conv_apply(dbuf_ref, w_ref, dglu_ref, tt, [CONV_WIDTH - 1 - k for k in range(CONV_WIDTH)])
        dglu = dglu_ref[...]
        av = a_ref[...].astype(F32)
        sg = _sigmoid(g_ref[...].astype(F32))
        dp_ref[:, 0:d] = (dglu * sg).astype(BF16)
        dp_ref[:, d:2 * d] = (dglu * av * sg * (1.0 - sg)).astype(BF16)
        dwin_ref[...] += _dot(h1_ref[...], dp_ref[...], _TN)

        @pl.when(end)
        def _():
            for k in range(CONV_WIDTH):
                dw_ref[k:k + 1, :] = jnp.sum(acc_ref[k], axis=0, keepdims=True)
            dw_ref[CONV_WIDTH:CONV_HALO, :] = jnp.zeros((CONV_HALO - CONV_WIDTH, d), F32)

    main_a, main_g, halo_a, halo_g = _conv_specs(bl, s, tt, d, 0, 1)
    dc_main = pl.BlockSpec((tt, d), lambda b, j: (b * nj + j, 0))
    dc_next = pl.BlockSpec((CONV_HALO, d), lambda b, j: (jnp.minimum((b * nj + j + 1) * per, last_blk), 0))
    return pl.pallas_call(
        body, name=name, grid=(bl, nj),
        in_specs=[pl.BlockSpec(memory_space=pl.ANY), dc_main, dc_next, main_a, main_g, halo_a, halo_g,
                  _const_spec((CONV_HALO, d)), dc_main],
        out_specs=[pl.BlockSpec((tt, 2 * d), lambda b, j: (b * nj + j, 0)), _const_spec((CONV_HALO, d)), _const_spec((1, d)),
                   _const_spec((d, 2 * d))],
        out_shape=[jax.ShapeDtypeStruct(dp.shape, dp.dtype), jax.ShapeDtypeStruct((CONV_HALO, d), F32),
                   jax.ShapeDtypeStruct((1, d), F32), jax.ShapeDtypeStruct((d, 2 * d), F32)],
        scratch_shapes=[pltpu.VMEM((SUBLANES, tt + CONV_HALO, d), F32), pltpu.VMEM((SUBLANES, tt + CONV_HALO, d), F32),
                        pltpu.VMEM((tt, d), F32), pltpu.VMEM((CONV_HALO, SUBLANES, d), F32)],
        input_output_aliases={0: 0},
        compiler_params=_cparams(("arbitrary", "arbitrary")))(dp, dc, dc, p, p, p, p, conv_w, h1)


def _sgu_stats(bv):
    gv = _gelu(bv)
    mu = jnp.mean(gv, axis=-1, keepdims=True)
    dv = gv - mu
    rstd = lax.rsqrt(jnp.mean(dv * dv, axis=-1, keepdims=True) + LN_EPS)
    return dv * rstd, rstd


def _sgu_fwd(p, wm, bias, ln_g, ln_b, *, name):
    t = p.shape[0]
    d = ln_g.shape[1]
    tt = SGU_TILE
    gd = d // SGU_GROUPS

    def body(u_ref, v_ref, wm_ref, bias_ref, lg_ref, lb_ref, sg_ref, vn_ref):
        u = _gelu(u_ref[...].astype(F32))
        vhat, _ = _sgu_stats(v_ref[...].astype(F32))
        vb = (vhat * lg_ref[...] + lb_ref[...]).astype(BF16)
        vn_ref[...] = vb
        for ci in range(tt // SGU_CHUNK):
            rows = slice(ci * SGU_CHUNK, (ci + 1) * SGU_CHUNK)
            for g in range(SGU_GROUPS):
                gs = slice(g * gd, (g + 1) * gd)
                z = _dot(wm_ref[g], vb[rows, gs], _NN) + bias_ref[g]
                sg_ref[rows, gs] = (u[rows, gs] * z).astype(BF16)

    rs = _row_spec(tt, d)
    return pl.pallas_call(
        body, name=name, grid=(t // tt,),
        in_specs=[_row_spec(tt, d, 2), _row_spec(tt, d, 3), _const_spec(wm.shape), _const_spec(bias.shape),
                  _const_spec((1, d)), _const_spec((1, d))],
        out_specs=[rs, rs], out_shape=[jax.ShapeDtypeStruct((t, d), BF16), jax.ShapeDtypeStruct((t, d), BF16)],
        compiler_params=_cparams(("parallel",)))(p, p, wm, bias, ln_g, ln_b)


def _sgu_bwd(dp, dy_b, w_out, p, vn, wm, wmt, bias, ln_g, *, name):
    t = p.shape[0]
    d = ln_g.shape[1]
    tt = SGU_TILE
    ck = SGU_CHUNK
    gd = d // SGU_GROUPS
    nsteps = t // tt

    def body(dp_in, dyb_ref, wout_ref, u_ref, v_ref, vn_ref, wm_ref, wmt_ref, bias_ref, lg_ref,
             dp_ref, dw_ref, dbs_ref, dlg_ref, dlb_ref, dz_acc):
        del dp_in
        i = pl.program_id(0)

        @pl.when(i == 0)
        def _():
            dw_ref[...] = jnp.zeros_like(dw_ref)
            dlg_ref[...] = jnp.zeros_like(dlg_ref)
            dlb_ref[...] = jnp.zeros_like(dlb_ref)
            dz_acc[...] = jnp.zeros_like(dz_acc)

        bu = u_ref[...].astype(F32)
        bv = v_ref[...].astype(F32)
        u = _gelu(bu)
        vhat, rstd = _sgu_stats(bv)
        vb = vn_ref[...]
        dsg = _dot(dyb_ref[...], wout_ref[...], _NT)
        row = lax.broadcasted_iota(jnp.int32, (ck, ck), 0)
        col = lax.broadcasted_iota(jnp.int32, (ck, ck), 1)
        causal = col <= row
        du_rows, dv_rows = [], []
        for ci in range(tt // ck):
            rows = slice(ci * ck, (ci + 1) * ck)
            du_parts, dv_parts = [], []
            for g in range(SGU_GROUPS):
                gs = slice(g * gd, (g + 1) * gd)
                z = _dot(wm_ref[g], vb[rows, gs], _NN) + bias_ref[g]
                du_parts.append(dsg[rows, gs] * z)
                dz = dsg[rows, gs] * u[rows, gs]
                dz_acc[:, gs] += dz
                dzb = dz.astype(BF16)
                dw_ref[g] += jnp.where(causal, _dot(dzb, vb[rows, gs], _NT), 0.0)
                dv_parts.append(_dot(wmt_ref[g], dzb, _NN))
            du_rows.append(jnp.concatenate(du_parts, axis=1))
            dv_rows.append(jnp.concatenate(dv_parts, axis=1))
        du = jnp.concatenate(du_rows, axis=0)
        dv = jnp.concatenate(dv_rows, axis=0)
        dp_ref[:, 0:d] = (du * _gelu_grad(bu)).astype(BF16)
        dlb_ref[...] += jnp.sum(dv, axis=0, keepdims=True)
        dlg_ref[...] += jnp.sum(dv * vhat, axis=0, keepdims=True)
        dvh = dv * lg_ref[...]
        dgv = rstd * (dvh - jnp.mean(dvh, axis=-1, keepdims=True) - vhat * jnp.mean(dvh * vhat, axis=-1, keepdims=True))
        dp_ref[:, d:2 * d] = (dgv * _gelu_grad(bv)).astype(BF16)

        @pl.when(i == nsteps - 1)
        def _():
            ones = jnp.ones((8, gd), F32)
            for g in range(SGU_GROUPS):
                gs = slice(g * gd, (g + 1) * gd)
                tot = lax.dot_general(ones, dz_acc[:, gs], (_NT, ((), ())), preferred_element_type=F32,
                                      precision=lax.Precision.HIGHEST)
                dbs_ref[g:g + 1, :] = tot[0:1, :]

    rs = _row_spec(tt, d)
    c1 = _const_spec((1, d))
    return pl.pallas_call(
        body, name=name, grid=(nsteps,),
        in_specs=[pl.BlockSpec(memory_space=pl.ANY), rs, _const_spec(w_out.shape), _row_spec(tt, d, 2), _row_spec(tt, d, 3),
                  rs, _const_spec(wm.shape), _const_spec(wmt.shape), _const_spec(bias.shape), c1],
        out_specs=[pl.BlockSpec((tt, 2 * d), lambda i: (i, 1)), _const_spec(wm.shape), _const_spec((SGU_GROUPS, ck)), c1, c1],
        out_shape=[jax.ShapeDtypeStruct(dp.shape, dp.dtype), jax.ShapeDtypeStruct(wm.shape, F32),
                   jax.ShapeDtypeStruct((SGU_GROUPS, ck), F32), jax.ShapeDtypeStruct((1, d), F32),
                   jax.ShapeDtypeStruct((1, d), F32)],
        scratch_shapes=[pltpu.VMEM((ck, d), F32)],
        input_output_aliases={0: 0},
        compiler_params=_cparams(("arbitrary",)))(dp, dy_b, w_out, p, p, vn, wm, wmt, bias, ln_g)


def _softmax_rows(s):
    e = jnp.exp(s - jnp.max(s, axis=-1, keepdims=True))
    return e / jnp.sum(e, axis=-1, keepdims=True)


def _attn_fwd(q, kv, x1, w_xo, gain, *, bl, s, name):
    t, d = q.shape
    mlen = kv.shape[0] // bl
    hd = d // HEADS
    tq = min(ATTN_TILE, s)
    nq = s // tq
    scale = hd ** -0.5

    def body(q_ref, kv_ref, x1_ref, w_ref, g_ref, o_ref, x2_ref, h_ref):
        for h in range(HEADS):
            hs = slice(h * hd, (h + 1) * hd)
            vs = slice(d + h * hd, d + (h + 1) * hd)
            pr = _softmax_rows(_dot(q_ref[:, hs], kv_ref[:, hs], _NT) * scale)
            o_ref[:, hs] = _dot(pr.astype(BF16), kv_ref[:, vs], _NN).astype(BF16)
        x2 = x1_ref[...] + _dot(o_ref[...], w_ref[...], _NN)
        x2_ref[...] = x2
        h_ref[...] = _rms_apply(x2, g_ref[...]).astype(BF16)

    qs = pl.BlockSpec((tq, d), lambda b, j: (b * nq + j, 0))
    return pl.pallas_call(
        body, name=name, grid=(bl, nq),
        in_specs=[qs, pl.BlockSpec((mlen, 2 * d), lambda b, j: (b, 0)), qs, _const_spec(w_xo.shape), _const_spec((1, d))],
        out_specs=[qs, qs, qs],
        out_shape=[jax.ShapeDtypeStruct((t, d), BF16), jax.ShapeDtypeStruct((t, d), F32), jax.ShapeDtypeStruct((t, d), BF16)],
        compiler_params=_cparams(("parallel", "parallel")))(q, kv, x1, w_xo, gain)


def _attn_bwd(q, kv, do, *, bl, s, name):
    t, d = q.shape
    mlen = kv.shape[0] // bl
    hd = d // HEADS
    tq = min(ATTN_TILE, s)
    nq = s // tq
    scale = hd ** -0.5

    def body(q_ref, kv_ref, do_ref, dq_ref, dkv_ref):
        @pl.when(pl.program_id(1) == 0)
        def _():
            dkv_ref[...] = jnp.zeros_like(dkv_ref)

        for h in range(HEADS):
            hs = slice(h * hd, (h + 1) * hd)
            vs = slice(d + h * hd, d + (h + 1) * hd)
            qh, kh, vh, doh = q_ref[:, hs], kv_ref[:, hs], kv_ref[:, vs], do_ref[:, hs]
            pr = _softmax_rows(_dot(qh, kh, _NT) * scale)
            dpr = _dot(doh, vh, _NT)
            dkv_ref[:, vs] += _dot(pr.astype(BF16), doh, _TN)
            ds = (pr * (dpr - jnp.sum(dpr * pr, axis=-1, keepdims=True)) * scale).astype(BF16)
            dq_ref[:, hs] = _dot(ds, kh, _NN).astype(BF16)
            dkv_ref[:, hs] += _dot(ds, qh, _TN)

    qs = pl.BlockSpec((tq, d), lambda b, j: (b * nq + j, 0))
    ks = pl.BlockSpec((mlen, 2 * d), lambda b, j: (b, 0))
    return pl.pallas_call(
        body, name=name, grid=(bl, nq), in_specs=[qs, ks, qs], out_specs=[qs, ks],
        out_shape=[jax.ShapeDtypeStruct((t, d), BF16), jax.ShapeDtypeStruct(kv.shape, F32)],
        compiler_params=_cparams(("parallel", "arbitrary")))(q, kv, do)


def _mesh_pos():
    return lax.axis_index("x"), lax.axis_index("y"), lax.axis_index("c")


def _all_gather(arrs, *, name):
    n = len(arrs)
    hbm = pl.BlockSpec(memory_space=pl.ANY)

    def body(*refs):
        ins, outs = refs[:n], refs[n:2 * n]
        send_sems, recv_sems, loc_sems = refs[2 * n:]
        x, y, c = _mesh_pos()
        me, sib = (x, y, c), (x, y, 1 - c)
        chips = [(1 - x, y), (x, 1 - y), (1 - x, 1 - y)]

        def idx(dev):
            return 4 * dev[0] + 2 * dev[1] + dev[2]

        def copy(w, k, block, to, from_input=False):
            return pltpu.make_async_remote_copy(
                src_ref=ins[w] if from_input else outs[w].at[idx(block)], dst_ref=outs[w].at[idx(block)],
                send_sem=send_sems.at[w, k], recv_sem=recv_sems.at[w, k], device_id=to, device_id_type=MESH_ID)

        own = [pltpu.make_async_copy(ins[w], outs[w].at[idx(me)], loc_sems.at[w]) for w in range(n)]
        for cp in own:
            cp.start()
        first = []
        for w in range(n):
            first.append(copy(w, 0, me, sib, True))
            first += [copy(w, 1 + j, me, (*chip, c), True) for j, chip in enumerate(chips)]
        for cp in first:
            cp.start()
        passed = []
        for j, chip in enumerate(chips):
            for w in range(n):
                copy(w, 1 + j, (*chip, c), me).wait_recv()
                fwd = copy(w, 4 + j, (*chip, c), sib)
                fwd.start()
                passed.append(fwd)
        for w in range(n):
            copy(w, 0, sib, me).wait_recv()
            for j, chip in enumerate(chips):
                copy(w, 4 + j, (*chip, 1 - c), me).wait_recv()
        for cp in first + passed:
            cp.wait_send()
        for cp in own:
            cp.wait()

    return pl.pallas_call(
        body, name=name, in_specs=[hbm] * n, out_specs=[hbm] * n,
        out_shape=[jax.ShapeDtypeStruct((N_DEV, *a.shape), a.dtype) for a in arrs],
        scratch_shapes=[pltpu.SemaphoreType.DMA((n, 7)), pltpu.SemaphoreType.DMA((n, 7)), pltpu.SemaphoreType.DMA((n,))],
    )(*arrs)


_HBM = pl.BlockSpec(memory_space=pltpu.HBM)
_SEM = pl.BlockSpec(memory_space=pltpu.SEMAPHORE)
_ANY = pl.BlockSpec(memory_space=pl.ANY)
_EFFECT = pltpu.SideEffectType.DATAFLOW_SIDE_EFFECTING
N_PEERS = N_DEV - 1


def _related(pos, r):
    x, y, c = pos
    return (1 - x if r & 4 else x, 1 - y if r & 2 else y, 1 - c if r & 1 else c)


def _dev_index(dev):
    return 4 * dev[0] + 2 * dev[1] + dev[2]


def _in_hbm(a):
    return pltpu.with_memory_space_constraint(a, pltpu.HBM)


def _split_copies(kind, srcs, lands, send_sems, recv_sems):
    pos = _mesh_pos()
    me = _dev_index(pos)
    out = []
    for w in range(len(srcs)):
        for r in range(1, N_DEV):
            peer = _related(pos, r)
            if kind == "gather":
                src, dst_here, dst_there = srcs[w], lands[w].at[_dev_index(peer)], lands[w].at[me]
            elif srcs[w].ndim == 2:
                cb = lands[w].shape[2]
                src = srcs[w].at[:, pl.ds(pl.multiple_of(_dev_index(peer) * cb, LANES), cb)]
                dst_here = dst_there = lands[w].at[r - 1]
            else:
                src, dst_here, dst_there = srcs[w].at[_dev_index(peer)], lands[w].at[r - 1], lands[w].at[r - 1]
            out.append((src, dst_here, dst_there, send_sems.at[w * N_PEERS + r - 1], recv_sems.at[w * N_PEERS + r - 1], peer))
    return out


def _copy_start(kind, srcs, land_shapes, *, name, after=None):
    n = len(srcs)
    n_after = 0 if after is None else 1

    def body(*refs):
        src_refs, land_refs = refs[:n], refs[n:2 * n]
        send_sems, recv_sems = refs[2 * n + n_after], refs[2 * n + n_after + 1]
        token = refs[-1]
        for src, _, dst, ssem, rsem, peer in _split_copies(kind, src_refs, land_refs, send_sems, recv_sems):
            pltpu.make_async_remote_copy(src_ref=src, dst_ref=dst, send_sem=ssem, recv_sem=rsem, device_id=peer,
                                         device_id_type=MESH_ID).start()
        token[...] = jnp.zeros_like(token)

    lands = [_in_hbm(lax.empty(shape, s.dtype)) for s, shape in zip(srcs, land_shapes)]
    res = pl.pallas_call(
        body, name=name,
        out_shape=(pltpu.SemaphoreType.DMA((n * N_PEERS,)), pltpu.SemaphoreType.DMA((n * N_PEERS,)),
                   *[pltpu.HBM(s.shape, s.dtype) for s in srcs], *[pltpu.HBM(l.shape, l.dtype) for l in lands],
                   jax.ShapeDtypeStruct((8, 128), F32)),
        in_specs=[_HBM] * (2 * n) + [_ANY] * n_after,
        out_specs=(_SEM, _SEM, *[_HBM] * (2 * n), pl.BlockSpec(memory_space=pltpu.VMEM)),
        input_output_aliases={i: 2 + i for i in range(2 * n)},
        compiler_params=pltpu.CompilerParams(has_side_effects=_EFFECT),
    )(*[_in_hbm(s) for s in srcs], *lands, *([] if after is None else [after]))
    return res[0], res[1], list(res[2:2 + n]), list(res[2 + n:2 + 2 * n]), res[-1]


def _copy_wait(kind, send_sems, recv_sems, srcs, lands, after, *, name):
    n = len(srcs)

    def body(*refs):
        src_refs, land_refs = refs[:n], refs[n:2 * n]
        ssems, rsems = refs[2 * n], refs[2 * n + 1]
        for src, dst, _, ssem, rsem, peer in _split_copies(kind, src_refs, land_refs, ssems, rsems):
            cp = pltpu.make_async_remote_copy(src_ref=src, dst_ref=dst, send_sem=ssem, recv_sem=rsem, device_id=peer,
                                              device_id_type=MESH_ID)
            cp.wait_send()
            cp.wait_recv()

    res = pl.pallas_call(
        body, name=name,
        out_shape=(*[pltpu.HBM(s.shape, s.dtype) for s in srcs], *[pltpu.HBM(l.shape, l.dtype) for l in lands]),
        in_specs=[_HBM] * (2 * n) + [_SEM, _SEM, _ANY], out_specs=tuple([_HBM] * (2 * n)),
        input_output_aliases={i: i for i in range(2 * n)},
        compiler_params=pltpu.CompilerParams(has_side_effects=_EFFECT),
    )(*srcs, *lands, send_sems, recv_sems, after)
    return list(res[:n]), list(res[n:])


def _row_tile(rows):
    return max(tr for tr in range(16, min(rows, 512) + 1, 16) if rows % tr == 0)


def _adamw_math(w, g, m, v):
    m2 = ADAM_B1 * m + (1.0 - ADAM_B1) * g
    v2 = ADAM_B2 * v + (1.0 - ADAM_B2) * (g * g)
    m_hat = m2 / (1.0 - ADAM_B1 ** ADAM_STEP)
    v_hat = v2 / (1.0 - ADAM_B2 ** ADAM_STEP)
    delta = -ADAM_LR * (m_hat / (jnp.sqrt(v_hat) + ADAM_EPS) + ADAM_WD * w)
    return delta, m2, v2


def _adamw_shard(partials, landed, dev, w, m, v, *, name):
    r, c = w.shape
    tr = _row_tile(r)

    def body(dev_ref, p_ref, l_ref, w_ref, m_ref, v_ref, g_out, d_out, m_out, v_out):
        del dev_ref
        g = p_ref[...].astype(F32)
        for k in range(N_PEERS):
            g = g + l_ref[k].astype(F32)
        delta, m2, v2 = _adamw_math(w_ref[...], g, m_ref[...], v_ref[...])
        g_out[...] = g
        d_out[...] = delta
        m_out[...] = m2
        v_out[...] = v2

    blk = pl.BlockSpec((tr, c), lambda i, dev_ref: (i, 0))
    if partials.ndim == 2:
        own = pl.BlockSpec((tr, c), lambda i, dev_ref: (i, dev_ref[0]))
    else:
        own = pl.BlockSpec((None, tr, c), lambda i, dev_ref: (dev_ref[0], i, 0))
    gs = pltpu.PrefetchScalarGridSpec(
        num_scalar_prefetch=1, grid=(r // tr,),
        in_specs=[own, pl.BlockSpec((N_PEERS, tr, c), lambda i, dev_ref: (0, i, 0)), blk, blk, blk],
        out_specs=[blk] * 4)
    return pl.pallas_call(
        body, name=name, grid_spec=gs, out_shape=[jax.ShapeDtypeStruct((r, c), F32)] * 4,
        compiler_params=_cparams(("parallel",)))(dev, partials, landed, w, m, v)


def _adamw_small(parts, dev, w, m, v, *, name, col_block):
    _, r, d = parts.shape
    cols = w.shape[1]

    def body(dev_ref, p_ref, w_ref, m_ref, v_ref, g_out, d_out, m_out, v_out):
        del dev_ref
        g = p_ref[0]
        for k in range(1, N_DEV):
            g = g + p_ref[k]
        delta, m2, v2 = _adamw_math(w_ref[...], g, m_ref[...], v_ref[...])
        g_out[...] = g
        d_out[...] = delta
        m_out[...] = m2
        v_out[...] = v2

    blk = pl.BlockSpec((r, cols), lambda i, dev_ref: (0, 0))
    pidx = (lambda i, dev_ref: (0, 0, dev_ref[0])) if col_block else (lambda i, dev_ref: (0, 0, 0))
    gs = pltpu.PrefetchScalarGridSpec(
        num_scalar_prefetch=1, grid=(1,),
        in_specs=[pl.BlockSpec((N_DEV, r, cols), pidx), blk, blk, blk], out_specs=[blk] * 4)
    return pl.pallas_call(
        body, name=name, grid_spec=gs, out_shape=[jax.ShapeDtypeStruct((r, cols), F32)] * 4,
        compiler_params=_cparams(("arbitrary",)))(dev, parts, w, m, v)


def _pad_rows(a, rows):
    return jnp.pad(a, ((0, rows - a.shape[0]), (0, 0)))


def _unblock_cols(g):
    return jnp.transpose(g, (1, 0, 2)).reshape(g.shape[1], N_DEV * g.shape[2])


def kernel(x, mem, norm_mix, w_in, b_gate, conv_w, conv_b, conv_ln_g, conv_ln_b, w_conv_out, sgu_ln_g, sgu_ln_b, sgu_w, sgu_b, w_sgu_out, w_mix_out, norm_xattn, norm_mem, w_q, w_kv, w_xo, norm_ffn, w_gu, w_down, norm_final, loss_target, m_norm_mix, m_w_in, m_b_gate, m_conv_w, m_conv_b, m_conv_ln_g, m_conv_ln_b, m_w_conv_out, m_sgu_ln_g, m_sgu_ln_b, m_sgu_w, m_sgu_b, m_w_sgu_out, m_w_mix_out, m_norm_xattn, m_norm_mem, m_w_q, m_w_kv, m_w_xo, m_norm_ffn, m_w_gu, m_w_down, m_norm_final, v_norm_mix, v_w_in, v_b_gate, v_conv_w, v_conv_b, v_conv_ln_g, v_conv_ln_b, v_w_conv_out, v_sgu_ln_g, v_sgu_ln_b, v_sgu_w, v_sgu_b, v_w_sgu_out, v_w_mix_out, v_norm_xattn, v_norm_mem, v_w_q, v_w_kv, v_w_xo, v_norm_ffn, v_w_gu, v_w_down, v_norm_final):
    given = dict(locals())
    bl, s, d = x.shape
    t = bl * s
    xf = x.reshape(t, d)
    tgt = loss_target.reshape(t, d)
    memf = mem.reshape(bl * mem.shape[1], d)
    cx, cy, cc = lax.axis_index("x"), lax.axis_index("y"), lax.axis_index("c")
    dev = 4 * cx + 2 * cy + cc
    dev_id = dev.astype(jnp.int32).reshape(1)
    col_sharded = ["w_in", "w_kv"]
    transposed = ["w_gu"]

    def shard_of(name, prefix=""):
        a = given[prefix + name][0]
        return jnp.transpose(a) if name in transposed else a

    def full_weight(name, blocks):
        return _unblock_cols(blocks) if name in col_sharded else blocks.reshape(N_DEV * blocks.shape[1], blocks.shape[2])

    g_bg, g_cw = _all_gather([_pad_rows(b_gate[0], 8), _pad_rows(conv_w[0], CONV_HALO)], name="gather_small_params")
    h1, p, w_in_blocks = _in_proj_gather(xf, norm_mix + g_bg[0, 7:8, 0:1], w_in[0].astype(BF16), name="in_proj")
    early = ["w_conv_out", "w_sgu_out", "w_mix_out", "w_q", "w_kv", "w_xo"]
    late = ["w_gu", "w_down"]
    shards = {n: shard_of(n).astype(BF16) for n in early + late}
    started = {}
    for grp, names in (("early", early), ("late", late)):
        srcs = [shards[n] for n in names]
        started[grp] = _copy_start("gather", srcs, [(N_DEV, *a.shape) for a in srcs], name=f"gather_{grp}_start", after=p)
    token = started["early"][4][0:1, 0:1] + started["late"][4][0:1, 0:1]
    wfull = {}
    bg_full = _unblock_cols(g_bg)
    cw_full = _unblock_cols(g_cw)

    def finish_gather(grp, names, after):
        ssem, rsem, srcs, lands, _ = started[grp]
        _, lands = _copy_wait("gather", ssem, rsem, srcs, lands, after, name=f"gather_{grp}_wait")
        for n, land in zip(names, lands):
            wfull[n] = full_weight(n, lax.dynamic_update_index_in_dim(land, shards[n], dev, 0))

    tri = jnp.tril(jnp.ones((SGU_CHUNK, SGU_CHUNK), bool))
    wm32 = jnp.where(tri[None], sgu_w[0], 0.0)
    wm = wm32.astype(BF16)
    wmt = jnp.transpose(wm32, (0, 2, 1)).astype(BF16)
    sgu_bias = jnp.broadcast_to(sgu_b[0][:, :, None], (SGU_GROUPS, SGU_CHUNK, d // SGU_GROUPS))

    c_conv, a_act = _conv_fwd(p, cw_full, conv_b + token, conv_ln_g, conv_ln_b, bl=bl, s=s, name="conv_fwd")
    sg, vn = _sgu_fwd(p, wm, sgu_bias, sgu_ln_g, sgu_ln_b + token, name="sgu_fwd")
    finish_gather("early", early, a_act[0:16, 0:128] + sg[0:16, 0:128])
    y_a = _matmul(a_act, wfull["w_conv_out"], mode="nn", out_dtype=BF16, name="mm_conv_out", tm=1024, tn=1024, tk=1024)
    y_b = _matmul(sg, wfull["w_sgu_out"], mode="nn", out_dtype=BF16, name="mm_sgu_out", tm=1024, tn=1024, tk=1024)
    merged, x1, h2, q = _mix_out(p, y_a, y_b, bg_full, xf, wfull["w_mix_out"], norm_xattn, wfull["w_q"], name="mix_out")
    mem_n = _rms_fwd(memf, norm_mem, name="rms_mem")
    kv = _matmul(mem_n, wfull["w_kv"], mode="nn", out_dtype=BF16, name="mm_kv", tm=1024, tn=1024, tk=1024)
    o, x2, h3 = _attn_fwd(q, kv, x1, wfull["w_xo"], norm_ffn, bl=bl, s=s, name="attn_fwd")
    finish_gather("late", late, h3)
    gu, act, dx3, loss_part, d_norm_final = _ffn_fwd(h3, x2, tgt, wfull["w_gu"], wfull["w_down"],
                                                     norm_final.reshape(1, d), name="ffn_fwd")

    grads = {}
    sent = []

    def send_grads(names, tag, after=None):
        blocks, land_shapes = [], []
        for n in names:
            g = grads[n]
            if g.ndim == 2 and n in col_sharded:
                land_shapes.append((N_PEERS, g.shape[0], g.shape[1] // N_DEV))
            else:
                if g.ndim == 2:
                    g = g.reshape(N_DEV, -1, g.shape[1])
                land_shapes.append((N_PEERS, *g.shape[1:]))
            blocks.append(g)
        ssem, rsem, srcs, lands, tok = _copy_start("scatter", blocks, land_shapes, name=f"grads_{tag}_start", after=after)
        sent.append((names, ssem, rsem, srcs, lands))
        return tok[0:1, 0:1]

    dgu, dx2, do, d_norm_ffn = _ffn_bwd(dx3, gu, x2, wfull["w_down"], wfull["w_gu"], norm_ffn, wfull["w_xo"], name="ffn_bwd")
    grads["w_down"] = _matmul(act, dx3, mode="tn", out_dtype=BF16, name="mm_dw_down", tm=1408, tn=1024, tk=2048)
    grads["w_gu"] = _matmul(dgu, h3, mode="tn", out_dtype=BF16, name="mm_dw_gu", tm=1408, tn=1024, tk=2048)
    tok = send_grads(["w_down", "w_gu"], "ffn")
    grads["w_xo"] = _matmul(o, dx2, mode="tn", out_dtype=BF16, name="mm_dw_xo", tm=1024, tn=1024, tk=2048)
    dq, dkv = _attn_bwd(q, kv, do, bl=bl, s=s, name="attn_bwd")
    grads["w_kv"] = _matmul(mem_n, dkv, mode="tn", out_dtype=BF16, name="mm_dw_kv", tm=1024, tn=256, tk=1024,
                            col_blocks=N_DEV)
    tok2 = send_grads(["w_xo", "w_kv"], "attn")
    dmem_n = _matmul(dkv, wfull["w_kv"], mode="nt", out_dtype=F32, name="mm_d_mem", tm=512, tn=1024, tk=2048)
    d_norm_mem = _rms_bwd(None, dmem_n, memf, norm_mem, name="rms_mem_bwd", need_dx=False)
    dx1, d_norm_xattn, dw_q = _proj_rms_bwd(dq, dx2, x1, wfull["w_q"], norm_xattn + (tok + tok2), name="q_rms_bwd", h=h2)
    dp, dy_a, dy_b, d_b_gate, dw_mix, dw_in_gates = _gates_bwd_fused(dx1, p, y_a, y_b, bg_full, wfull["w_mix_out"],
                                                                    merged, h1, name="gates_bwd")
    grads["w_q"] = dw_q.astype(BF16)
    grads["w_mix_out"] = dw_mix.astype(BF16)
    grads["w_sgu_out"] = _matmul(sg, dy_b, mode="tn", out_dtype=BF16, name="mm_dw_sgu", tm=1024, tn=1024, tk=2048)
    dc, d_conv_ln_g, d_conv_ln_b, dw_conv = _conv_ln_bwd_fused(dy_a, c_conv, a_act, wfull["w_conv_out"], conv_ln_g,
                                                               conv_ln_b, name="conv_ln_bwd")
    grads["w_conv_out"] = dw_conv.astype(BF16)
    tok = send_grads(["w_q", "w_mix_out", "w_sgu_out", "w_conv_out"], "mixer")
    dp, d_sgu_w, d_sgu_b, d_sgu_ln_g, d_sgu_ln_b = _sgu_bwd(dp, dy_b, wfull["w_sgu_out"], p, vn, wm, wmt, sgu_bias,
                                                             sgu_ln_g + tok, name="sgu_bwd")
    dw_in_sgu = _matmul(h1, dp, mode="tn", out_dtype=BF16, name="mm_dw_in_sgu", tm=1024, tn=1024, tk=2048,
                        b_cols=(2 * d, 2 * d))
    dp, d_conv_w, d_conv_b, dw_in_conv = _conv_bwd(dp, dc, p, cw_full, h1, bl=bl, s=s, name="conv_bwd")
    grads["w_in"] = jnp.concatenate([dw_in_conv.astype(BF16), dw_in_sgu, dw_in_gates], axis=1)
    tok = send_grads(["w_in"], "in")
    grad_x, d_norm_mix = _proj_rms_bwd(dp, dx1, xf, w_in_blocks, norm_mix + tok, name="in_proj_bwd")
    out = {}

    rep_names = ["norm_mix", "conv_b", "conv_ln_g", "conv_ln_b", "sgu_ln_g", "sgu_ln_b", "norm_xattn", "norm_mem",
                 "norm_ffn", "norm_final", "sgu_b"]
    rep_grads = [d_norm_mix, d_conv_b, d_conv_ln_g, d_conv_ln_b, d_sgu_ln_g, d_sgu_ln_b, d_norm_xattn, d_norm_mem,
                 d_norm_ffn, d_norm_final, d_sgu_b.reshape(1, d)]
    nrep = len(rep_names)
    pad = jnp.zeros((16 - nrep, d), F32)
    sgw_rows = SGU_GROUPS * SGU_CHUNK * SGU_CHUNK // d

    def pack_rep(vecs, sgw, extra=None):
        fill = pad if extra is None else jnp.concatenate([extra, pad[1:]], axis=0)
        return jnp.concatenate([v.reshape(1, d) for v in vecs] + [fill, sgw.reshape(sgw_rows, d)], axis=0)

    def pack_col(bg, cw):
        return jnp.concatenate([_pad_rows(bg, 8), _pad_rows(cw, CONV_HALO)], axis=0)

    small_a = pack_rep(rep_grads, d_sgu_w, extra=jnp.broadcast_to(loss_part, (1, d)))
    small_b = jnp.concatenate([d_b_gate, d_conv_w], axis=0)
    parts_a, parts_b = _all_gather([small_a, small_b], name="gather_small_grads")
    res_a = _adamw_small(parts_a, dev_id, pack_rep([given[n] for n in rep_names], sgu_w),
                         pack_rep([given["m_" + n] for n in rep_names], m_sgu_w),
                         pack_rep([given["v_" + n] for n in rep_names], v_sgu_w), name="adamw_small", col_block=False)
    res_b = _adamw_small(parts_b, dev_id, pack_col(b_gate[0], conv_w[0]), pack_col(m_b_gate[0], m_conv_w[0]),
                         pack_col(v_b_gate[0], v_conv_w[0]), name="adamw_small_cols", col_block=True)
    for i, n in enumerate(rep_names):
        out[n] = [r[i].reshape(given[n].shape) for r in res_a]
    out["sgu_w"] = [r[16:16 + sgw_rows].reshape(sgu_w.shape) for r in res_a]
    out["b_gate"] = [r[0:2][None] for r in res_b]
    out["conv_w"] = [r[8:8 + CONV_WIDTH][None] for r in res_b]

    done = res_a[0]
    for names, ssem, rsem, srcs, lands in sent:
        srcs, lands = _copy_wait("scatter", ssem, rsem, srcs, lands, done, name=f"grads_{names[0]}_wait")
        for n, partials, landed in zip(names, srcs, lands):
            res = _adamw_shard(partials, landed, dev_id, shard_of(n), shard_of(n, "m_"), shard_of(n, "v_"),
                               name=f"adamw_{n}")
            done = res[0]
            out[n] = [(jnp.transpose(r) if n in transposed else r)[None] for r in res]

    order = ["norm_mix", "w_in", "b_gate", "conv_w", "conv_b", "conv_ln_g", "conv_ln_b", "w_conv_out", "sgu_ln_g",
             "sgu_ln_b", "sgu_w", "sgu_b", "w_sgu_out", "w_mix_out", "norm_xattn", "norm_mem", "w_q", "w_kv", "w_xo",
             "norm_ffn", "w_gu", "w_down", "norm_final"]
    loss = res_a[0][nrep, 0]
    return (loss, grad_x.reshape(x.shape), *[out[n][0] for n in order], *[out[n][1] for n in order],
            *[out[n][2] for n in order], *[out[n][3] for n in order])
```

```python
import functools

import jax
import jax.numpy as jnp
from jax import lax
from jax.experimental import pallas as pl
from jax.experimental.pallas import tpu as pltpu

F32 = jnp.float32
BF16 = jnp.bfloat16
RMS_EPS = 1e-6
LN_EPS = 1e-5
CONV_WIDTH = 31
CONV_HALO = 32
CONV_ROWS = 128
CONV_COLS = 256
LANES = 128
SGU_CHUNK = 128
SGU_GROUPS = 8
SGU_TILE = 512
HEADS = 4
N_DEV = 8
ADAM_LR, ADAM_B1, ADAM_B2, ADAM_EPS, ADAM_WD, ADAM_STEP = 0.001, 0.9, 0.999, 1e-08, 0.01, 10
VMEM_LIMIT = 56 * 1024 * 1024
TOKEN_TILE = 256
ATTN_TILE = 1024
MESH_ID = pl.DeviceIdType.MESH

_GELU_K = 0.7978845608028654
_GELU_C = 0.044715


def _cparams(sem=None):
    return pltpu.CompilerParams(dimension_semantics=sem, vmem_limit_bytes=VMEM_LIMIT)


def _sigmoid(v):
    return 0.5 * jnp.tanh(0.5 * v) + 0.5


def _gelu(v):
    return 0.5 * v * (1.0 + jnp.tanh(_GELU_K * (v + _GELU_C * v * v * v)))


def _gelu_grad(v):
    th = jnp.tanh(_GELU_K * (v + _GELU_C * v * v * v))
    return 0.5 * (1.0 + th) + 0.5 * v * (1.0 - th * th) * _GELU_K * (1.0 + 3.0 * _GELU_C * v * v)


def _dot(a, b, dims):
    return lax.dot_general(a, b, (dims, ((), ())), preferred_element_type=F32)


_NN = ((1,), (0,))
_NT = ((1,), (1,))
_TN = ((0,), (0,))


def _matmul(a, b, *, mode, out_dtype, name, tm=512, tn=512, tk=512, chunk=None, residual=None, rms_gain=None,
            col_blocks=None, b_cols=None):
    if mode == "nn":
        (m, k), (_, n) = a.shape, b.shape
    elif mode == "nt":
        (m, k), (n, _) = a.shape, b.shape
    else:
        (k, m), (_, n) = a.shape, b.shape
    b_first = 0
    if b_cols is not None:
        assert mode == "tn"
        b_first, n = b_cols
    tm, tn, tk = min(tm, m), min(tn, n), min(tk, k)
    assert b_first % tn == 0
    b_first //= tn
    assert m % tm == 0 and n % tn == 0 and k % tk == 0, (name, a.shape, b.shape, tm, tn, tk)
    nk = k // tk
    dims = {"nn": _NN, "nt": _NT, "tn": _TN}[mode]
    chunk = tn if chunk is None else min(chunk, tn)
    assert tn % chunk == 0
    if rms_gain is not None:
        assert tn == n and chunk == n

    def body(*refs):
        refs = list(refs)
        a_ref, b_ref = refs[:2]
        pos = 2
        r_ref = g_ref = None
        if residual is not None:
            r_ref = refs[pos]
            pos += 1
        if rms_gain is not None:
            g_ref = refs[pos]
            pos += 1
        o_ref = refs[pos]
        pos += 1
        h_ref = None
        if rms_gain is not None:
            h_ref = refs[pos]
            pos += 1
        acc_ref = refs[pos] if nk > 1 else None
        av = a_ref[...].astype(BF16)
        for c0 in range(0, tn, chunk):
            cs = slice(c0, c0 + chunk)
            bv = (b_ref[cs, :] if mode == "nt" else b_ref[:, cs]).astype(BF16)
            part = _dot(av, bv, dims)

            def finish(res, cs=cs):
                if r_ref is not None:
                    res = res + r_ref[:, cs].astype(F32)
                o_ref[:, cs] = res.astype(out_dtype)
                if h_ref is not None:
                    r = lax.rsqrt(jnp.mean(res * res, axis=-1, keepdims=True) + RMS_EPS)
                    h_ref[...] = (res * r * g_ref[...]).astype(BF16)

            if nk == 1:
                finish(part)
            else:
                kk = pl.program_id(2)

                @pl.when(kk == 0)
                def _(part=part, cs=cs):
                    acc_ref[:, cs] = part

                @pl.when(kk > 0)
                def _(part=part, cs=cs):
                    acc_ref[:, cs] += part

                @pl.when(kk == nk - 1)
                def _(finish=finish, cs=cs):
                    finish(acc_ref[:, cs])

    resident = dict(pipeline_mode=pl.Buffered(1)) if (n == tn and nk == 1 and mode != "tn" and m > tm) else {}
    if mode == "nn":
        a_spec = pl.BlockSpec((tm, tk), lambda i, j, kk: (i, kk))
        b_spec = pl.BlockSpec((tk, tn), lambda i, j, kk: (kk, j), **resident)
    elif mode == "nt":
        a_spec = pl.BlockSpec((tm, tk), lambda i, j, kk: (i, kk))
        b_spec = pl.BlockSpec((tn, tk), lambda i, j, kk: (j, kk), **resident)
    else:
        a_spec = pl.BlockSpec((tk, tm), lambda i, j, kk: (kk, i))
        b_spec = pl.BlockSpec((tk, tn), lambda i, j, kk: (kk, j + b_first))
    o_spec = pl.BlockSpec((tm, tn), lambda i, j, kk: (i, j))
    in_specs, args = [a_spec, b_spec], [a, b]
    if residual is not None:
        in_specs.append(o_spec)
        args.append(residual)
    out_shape, out_specs = [jax.ShapeDtypeStruct((m, n), out_dtype)], [o_spec]
    if col_blocks is not None:
        assert residual is None and rms_gain is None and (n // col_blocks) % tn == 0
        per = n // col_blocks // tn
        out_shape = [jax.ShapeDtypeStruct((col_blocks, m, n // col_blocks), out_dtype)]
        out_specs = [pl.BlockSpec((None, tm, tn), lambda i, j, kk: (j // per, i, j % per))]
    if rms_gain is not None:
        in_specs.append(pl.BlockSpec((1, n), lambda i, j, kk: (0, 0)))
        args.append(rms_gain)
        out_shape.append(jax.ShapeDtypeStruct((m, n), BF16))
        out_specs.append(o_spec)
    res = pl.pallas_call(
        body, name=name, grid=(m // tm, n // tn, nk), in_specs=in_specs, out_specs=out_specs, out_shape=out_shape,
        scratch_shapes=[pltpu.VMEM((tm, tn), F32)] if nk > 1 else [],
        compiler_params=_cparams(("parallel", "parallel", "arbitrary")),
    )(*args)
    return res if rms_gain is not None else res[0]


def _row_call(name, t, tm, rows_in, residents, rows_out, accs, body):
    n_in, n_res, n_out, n_acc = len(rows_in), len(residents), len(rows_out), len(accs)
    steps = t // tm
    assert t % tm == 0
    narrow = [i for i, (_, dt) in enumerate(accs) if dt != F32]

    def kernel_body(*refs):
        in_refs, res_refs = refs[:n_in], refs[n_in:n_in + n_res]
        out_refs = refs[n_in + n_res:n_in + n_res + n_out]
        acc_out = list(refs[n_in + n_res + n_out:n_in + n_res + n_out + n_acc])
        scratch = refs[n_in + n_res + n_out + n_acc:]
        acc_refs = list(acc_out)
        for s_ref, i in zip(scratch, narrow):
            acc_refs[i] = s_ref
        if accs:
            @pl.when(pl.program_id(0) == 0)
            def _():
                for acc in acc_refs:
                    acc[...] = jnp.zeros_like(acc)
        body(in_refs, res_refs, out_refs, acc_refs)
        if narrow:
            @pl.when(pl.program_id(0) == steps - 1)
            def _():
                for i in narrow:
                    acc_out[i][...] = acc_refs[i][...].astype(acc_out[i].dtype)

    once = dict(pipeline_mode=pl.Buffered(1)) if steps > 1 else {}
    in_specs = [pl.BlockSpec((tm, cols), lambda i, cb=cb: (i, cb)) for _, cols, cb in rows_in]
    in_specs += [pl.BlockSpec(r.shape, lambda i, nd=r.ndim: (0,) * nd, **once) for r in residents]
    out_specs = [pl.BlockSpec((tm, cols), lambda i, cb=cb: (i, cb)) for _, cols, cb, _ in rows_out]
    out_specs += [pl.BlockSpec(shape, lambda i, nd=len(shape): (0,) * nd) for shape, _ in accs]
    out_shape = [jax.ShapeDtypeStruct((t, total), dt) for total, _, _, dt in rows_out]
    out_shape += [jax.ShapeDtypeStruct(shape, dt) for shape, dt in accs]
    return pl.pallas_call(
        kernel_body, name=name, grid=(steps,), in_specs=in_specs, out_specs=out_specs, out_shape=out_shape,
        scratch_shapes=[pltpu.VMEM(accs[i][0], F32) for i in narrow],
        compiler_params=_cparams(("arbitrary",) if accs else ("parallel",)),
    )(*[a for a, _, _ in rows_in], *residents)


def _rms_apply(xv, gain):
    return xv * lax.rsqrt(jnp.mean(xv * xv, axis=-1, keepdims=True) + RMS_EPS) * gain


def _rms_grad(dres, dh, xv, gain):
    r = lax.rsqrt(jnp.mean(xv * xv, axis=-1, keepdims=True) + RMS_EPS)
    xhat = xv * r
    dxh = dh * gain
    dx = dres + r * (dxh - xhat * jnp.mean(dxh * xhat, axis=-1, keepdims=True))
    return dx, jnp.sum(dh * xhat, axis=0, keepdims=True)


def _in_proj_gather(xf, gain, w_shard, *, name):
    t, d = xf.shape
    cb = w_shard.shape[1]
    tm = min(1024, t)
    steps = t // tm
    mx, my, _ = _mesh_pos()
    order = jnp.stack([2 * mx + my, 2 * (1 - mx) + my, 2 * mx + (1 - my), 2 * (1 - mx) + (1 - my)]).astype(jnp.int32)

    def body(order_ref, x_ref, g_ref, ws_ref, h_ref, p_ref, wout_ref, w_ref, send_sems, recv_sems, own_sem):
        ps, i = pl.program_id(0), pl.program_id(1)
        x, y, c = _mesh_pos()
        me, sib = (x, y, c), (x, y, 1 - c)
        chips = [(1 - x, y), (x, 1 - y), (1 - x, 1 - y)]

        def copy(k, block, to, from_shard=False):
            return pltpu.make_async_remote_copy(
                src_ref=ws_ref if from_shard else w_ref.at[_dev_index(block)], dst_ref=w_ref.at[_dev_index(block)],
                send_sem=send_sems.at[k], recv_sem=recv_sems.at[k], device_id=to, device_id_type=MESH_ID)

        own = pltpu.make_async_copy(ws_ref, w_ref.at[_dev_index(me)], own_sem)
        first = [copy(0, me, sib, True)] + [copy(1 + j, me, (*chip, c), True) for j, chip in enumerate(chips)]
        passed = [copy(4 + j, (*chip, c), sib) for j, chip in enumerate(chips)]

        @pl.when(jnp.logical_and(ps == 0, i == 0))
        def _():
            own.start()
            for cp in first:
                cp.start()
            own.wait()
            copy(0, sib, me).wait_recv()

        for j, chip in enumerate(chips):
            @pl.when(jnp.logical_and(ps == j + 1, i == 0))
            def _(j=j, chip=chip):
                copy(1 + j, (*chip, c), me).wait_recv()
                passed[j].start()
                copy(4 + j, (*chip, 1 - c), me).wait_recv()

        h = _rms_apply(x_ref[...], g_ref[...]).astype(BF16)
        h_ref[...] = h
        chip_id = order_ref[ps]
        p_ref[:, 0:cb] = _dot(h, w_ref[2 * chip_id], _NN).astype(BF16)
        p_ref[:, cb:2 * cb] = _dot(h, w_ref[2 * chip_id + 1], _NN).astype(BF16)

        @pl.when(jnp.logical_and(ps == 3, i == steps - 1))
        def _():
            for cp in first + passed:
                cp.wait_send()
            keep = pltpu.make_async_copy(w_ref, wout_ref, own_sem)
            keep.start()
            keep.wait()

    gs = pltpu.PrefetchScalarGridSpec(
        num_scalar_prefetch=1, grid=(4, steps),
        in_specs=[pl.BlockSpec((tm, d), lambda ps, i, o: (i, 0)), pl.BlockSpec((1, d), lambda ps, i, o: (0, 0)),
                  pl.BlockSpec(memory_space=pl.ANY)],
        out_specs=[pl.BlockSpec((tm, d), lambda ps, i, o: (jnp.where(ps == 0, i, steps - 1), 0)),
                   pl.BlockSpec((tm, 2 * cb), lambda ps, i, o: (i, o[ps])), pl.BlockSpec(memory_space=pl.ANY)],
        scratch_shapes=[pltpu.VMEM((N_DEV, d, cb), BF16), pltpu.SemaphoreType.DMA((7,)), pltpu.SemaphoreType.DMA((7,)),
                        pltpu.SemaphoreType.DMA(())])
    return pl.pallas_call(
        body, name=name, grid_spec=gs,
        out_shape=[jax.ShapeDtypeStruct((t, d), BF16), jax.ShapeDtypeStruct((t, N_DEV * cb), BF16),
                   jax.ShapeDtypeStruct((N_DEV, d, cb), BF16)],
        compiler_params=_cparams(("arbitrary", "arbitrary")))(order, xf, gain, w_shard)


def _mix_out(p, y_a, y_b, b_gate, xf, w_mix, gain, w_q, *, name):
    t, d = xf.shape

    def body(ins, res, outs, accs):
        ga_ref, gb_ref, ya_ref, yb_ref, x_ref = ins
        bg_ref, wm_ref, g_ref, wq_ref = res
        m_ref, x1_ref, h_ref, q_ref = outs
        sa = _sigmoid(ga_ref[...].astype(F32) + bg_ref[0:1, :])
        sb = _sigmoid(gb_ref[...].astype(F32) + bg_ref[1:2, :])
        merged = (sa * ya_ref[...].astype(F32) + sb * yb_ref[...].astype(F32)).astype(BF16)
        m_ref[...] = merged
        x1 = x_ref[...] + _dot(merged, wm_ref[...], _NN)
        x1_ref[...] = x1
        h = _rms_apply(x1, g_ref[...]).astype(BF16)
        h_ref[...] = h
        q_ref[...] = _dot(h, wq_ref[...], _NN).astype(BF16)

    return _row_call(name, t, min(512, t), [(p, d, 4), (p, d, 5), (y_a, d, 0), (y_b, d, 0), (xf, d, 0)],
                     [b_gate, w_mix, gain, w_q], [(d, d, 0, BF16), (d, d, 0, F32), (d, d, 0, BF16), (d, d, 0, BF16)], [], body)


def _ffn_fwd(h3, x2, target, w_gu_t, w_down, gain, *, name):
    t, d = x2.shape
    f2 = w_gu_t.shape[0]
    f = f2 // 2
    half = f // 2

    def body(ins, res, outs, accs):
        h_ref, x2_ref, t_ref = ins
        wgu_ref, wd_ref, g_ref = res
        gu_ref, act_ref, dx_ref = outs
        loss_ref, dg_ref = accs
        h = h_ref[...]
        x3 = x2_ref[...]
        for c0 in (0, half):
            gt = _dot(h, wgu_ref[c0:c0 + half, :], _NT).astype(BF16)
            up = _dot(h, wgu_ref[f + c0:f + c0 + half, :], _NT).astype(BF16)
            gu_ref[:, c0:c0 + half] = gt
            gu_ref[:, f + c0:f + c0 + half] = up
            gtf = gt.astype(F32)
            act = (gtf * _sigmoid(gtf) * up.astype(F32)).astype(BF16)
            act_ref[:, c0:c0 + half] = act
            x3 = x3 + _dot(act, wd_ref[c0:c0 + half, :], _NN)
        g = g_ref[...]
        r = lax.rsqrt(jnp.mean(x3 * x3, axis=-1, keepdims=True) + RMS_EPS)
        xhat = x3 * r
        err = xhat * g - t_ref[...]
        loss_ref[...] += 0.5 * jnp.sum(jnp.mean(err * err, axis=-1, keepdims=True), axis=0, keepdims=True)
        dy = err * (1.0 / d)
        dg_ref[...] += jnp.sum(dy * xhat, axis=0, keepdims=True)
        dxh = dy * g
        dx_ref[...] = r * (dxh - xhat * jnp.mean(dxh * xhat, axis=-1, keepdims=True))

    return _row_call(name, t, min(256, t), [(h3, d, 0), (x2, d, 0), (target, d, 0)], [w_gu_t, w_down, gain],
                     [(f2, f2, 0, BF16), (f, f, 0, BF16), (d, d, 0, F32)], [((1, 1), F32), ((1, d), F32)], body)


def _ffn_bwd(dx3, gu, x2, w_down, w_gu_t, gain, w_xo, *, name):
    t, d = x2.shape
    f2 = w_gu_t.shape[0]
    f = f2 // 2
    half = f // 2

    def body(ins, res, outs, accs):
        dx3_ref, gu_ref, x2_ref = ins
        wd_ref, wgu_ref, g_ref, wxo_ref = res
        dgu_ref, dx2_ref, do_ref = outs
        (dg_ref,) = accs
        dx3v = dx3_ref[...]
        dxb = dx3v.astype(BF16)
        dh = jnp.zeros(dx3v.shape, F32)
        for c0 in (0, half):
            dact = _dot(dxb, wd_ref[c0:c0 + half, :], _NT)
            gt = gu_ref[:, c0:c0 + half].astype(F32)
            up = gu_ref[:, f + c0:f + c0 + half].astype(F32)
            sg = _sigmoid(gt)
            dgt = (dact * up * sg * (1.0 + gt * (1.0 - sg))).astype(BF16)
            dup = (dact * gt * sg).astype(BF16)
            dgu_ref[:, c0:c0 + half] = dgt
            dgu_ref[:, f + c0:f + c0 + half] = dup
            dh = dh + _dot(dgt, wgu_ref[c0:c0 + half, :], _NN) + _dot(dup, wgu_ref[f + c0:f + c0 + half, :], _NN)
        dx2, dg = _rms_grad(dx3v, dh, x2_ref[...], g_ref[...])
        dx2_ref[...] = dx2
        dg_ref[...] += dg
        do_ref[...] = _dot(dx2.astype(BF16), wxo_ref[...], _NT).astype(BF16)

    return _row_call(name, t, min(256, t), [(dx3, d, 0), (gu, f2, 0), (x2, d, 0)], [w_down, w_gu_t, gain, w_xo],
                     [(f2, f2, 0, BF16), (d, d, 0, F32), (d, d, 0, BF16)], [((1, d), F32)], body)


def _proj_rms_bwd(dy, dres, x, w, gain, *, name, h=None):
    t, d = x.shape
    k = dy.shape[1]

    def body(ins, res, outs, accs):
        dy_ref, dres_ref, x_ref = ins[:3]
        w_ref, g_ref = res
        if h is not None:
            accs[1][...] += _dot(ins[3][...], dy_ref[...], _TN)
        if w.ndim == 3:
            cb = w.shape[2]
            dh = _dot(dy_ref[:, 0:cb], w_ref[0], _NT)
            for j in range(1, w.shape[0]):
                dh = dh + _dot(dy_ref[:, j * cb:(j + 1) * cb], w_ref[j], _NT)
        else:
            dh = _dot(dy_ref[...], w_ref[...], _NT)
        dx, dg = _rms_grad(dres_ref[...], dh, x_ref[...], g_ref[...])
        outs[0][...] = dx
        accs[0][...] += dg

    rows_in = [(dy, k, 0), (dres, d, 0), (x, d, 0)] + ([(h, d, 0)] if h is not None else [])
    accs = [((1, d), F32)] + ([((d, k), BF16)] if h is not None else [])
    return _row_call(name, t, min(512, t), rows_in, [w, gain], [(d, d, 0, F32)], accs, body)


def _gates_bwd_fused(dx1, p, y_a, y_b, b_gate, w_mix, merged, h1, *, name):
    t, d = y_a.shape

    def body(ins, res, outs, accs):
        dx_ref, ga_ref, gb_ref, ya_ref, yb_ref, m_ref, h1_ref = ins
        bg_ref, wm_ref = res
        dp_ref, dya_ref, dyb_ref = outs
        dbg_ref, dwm_ref, dwin_ref = accs
        dxb = dx_ref[...].astype(BF16)
        dwm_ref[...] += _dot(m_ref[...], dxb, _TN)
        dm = _dot(dxb, wm_ref[...], _NT)
        sa = _sigmoid(ga_ref[...].astype(F32) + bg_ref[0:1, :])
        sb = _sigmoid(gb_ref[...].astype(F32) + bg_ref[1:2, :])
        dya_ref[...] = (dm * sa).astype(BF16)
        dyb_ref[...] = (dm * sb).astype(BF16)
        dga = dm * ya_ref[...].astype(F32) * sa * (1.0 - sa)
        dgb = dm * yb_ref[...].astype(F32) * sb * (1.0 - sb)
        dp_ref[:, 0:d] = dga.astype(BF16)
        dp_ref[:, d:2 * d] = dgb.astype(BF16)
        dbg_ref[0:1, :] += jnp.sum(dga, axis=0, keepdims=True)
        dbg_ref[1:2, :] += jnp.sum(dgb, axis=0, keepdims=True)
        dwin_ref[...] += _dot(h1_ref[...], dp_ref[...], _TN)

    return _row_call(name, t, min(256, t),
                     [(dx1, d, 0), (p, d, 4), (p, d, 5), (y_a, d, 0), (y_b, d, 0), (merged, d, 0), (h1, d, 0)],
                     [b_gate, w_mix], [(p.shape[1], 2 * d, 2, BF16), (d, d, 0, BF16), (d, d, 0, BF16)],
                     [((8, d), F32), ((d, d), BF16), ((d, 2 * d), BF16)], body)


def _conv_ln_bwd_fused(dy_a, c, a_act, w_conv_out, ln_g, ln_b, *, name):
    t, d = c.shape

    def body(ins, res, outs, accs):
        dy_ref, c_ref, act_ref = ins
        w_ref, lg_ref, lb_ref = res
        dlg_ref, dlb_ref, dw_ref = accs
        dw_ref[...] += _dot(act_ref[...], dy_ref[...], _TN)
        dact = _dot(dy_ref[...], w_ref[...], _NT)
        cv = c_ref[...].astype(F32)
        g = lg_ref[...]
        mu = jnp.mean(cv, axis=-1, keepdims=True)
        dv = cv - mu
        rstd = lax.rsqrt(jnp.mean(dv * dv, axis=-1, keepdims=True) + LN_EPS)
        chat = dv * rstd
        aln = chat * g + lb_ref[...]
        sg = _sigmoid(aln)
        daln = dact * (sg * (1.0 + aln * (1.0 - sg)))
        dlb_ref[...] += jnp.sum(daln, axis=0, keepdims=True)
        dlg_ref[...] += jnp.sum(daln * chat, axis=0, keepdims=True)
        dchat = daln * g
        dc = rstd * (dchat - jnp.mean(dchat, axis=-1, keepdims=True)
                     - chat * jnp.mean(dchat * chat, axis=-1, keepdims=True))
        outs[0][...] = dc.astype(BF16)

    return _row_call(name, t, min(512, t), [(dy_a, d, 0), (c, d, 0), (a_act, d, 0)], [w_conv_out, ln_g, ln_b],
                     [(d, d, 0, BF16)], [((1, d), F32), ((1, d), F32), ((d, d), BF16)], body)


def _row_spec(tt, cols, col_block=0):
    return pl.BlockSpec((tt, cols), lambda i: (i, col_block))


def _const_spec(shape):
    return pl.BlockSpec(shape, lambda *_: (0,) * len(shape))


def _rms_fwd(x, gain, *, name):
    t, d = x.shape
    tt = min(TOKEN_TILE, t)

    def body(x_ref, g_ref, h_ref):
        xv = x_ref[...]
        r = lax.rsqrt(jnp.mean(xv * xv, axis=-1, keepdims=True) + RMS_EPS)
        h_ref[...] = (xv * r * g_ref[...]).astype(BF16)

    return pl.pallas_call(
        body, name=name, grid=(t // tt,), in_specs=[_row_spec(tt, d), _const_spec((1, d))],
        out_specs=_row_spec(tt, d), out_shape=jax.ShapeDtypeStruct((t, d), BF16),
        compiler_params=_cparams(("parallel",)))(x, gain)


def _rms_bwd(dres, dh, x, gain, *, name, need_dx=True):
    t, d = x.shape
    tt = min(TOKEN_TILE, t)

    def body(*refs):
        if need_dx:
            dres_ref, dh_ref, x_ref, g_ref, dx_ref, dg_ref = refs
        else:
            dh_ref, x_ref, g_ref, dg_ref = refs

        @pl.when(pl.program_id(0) == 0)
        def _():
            dg_ref[...] = jnp.zeros_like(dg_ref)

        xv = x_ref[...]
        dhv = dh_ref[...].astype(F32)
        r = lax.rsqrt(jnp.mean(xv * xv, axis=-1, keepdims=True) + RMS_EPS)
        xhat = xv * r
        dg_ref[...] += jnp.sum(dhv * xhat, axis=0, keepdims=True)
        if need_dx:
            dxh = dhv * g_ref[...]
            dx_ref[...] = dres_ref[...] + r * (dxh - xhat * jnp.mean(dxh * xhat, axis=-1, keepdims=True))

    rs = _row_spec(tt, d)
    if need_dx:
        in_specs, args = [rs, rs, rs, _const_spec((1, d))], (dres, dh, x, gain)
        out_specs = [rs, _const_spec((1, d))]
        out_shape = [jax.ShapeDtypeStruct((t, d), F32), jax.ShapeDtypeStruct((1, d), F32)]
    else:
        in_specs, args = [rs, rs, _const_spec((1, d))], (dh, x, gain)
        out_specs = [_const_spec((1, d))]
        out_shape = [jax.ShapeDtypeStruct((1, d), F32)]
    res = pl.pallas_call(body, name=name, grid=(t // tt,), in_specs=in_specs, out_specs=out_specs, out_shape=out_shape,
                         compiler_params=_cparams(("arbitrary",)))(*args)
    return res if need_dx else res[0]


SUBLANES = 8
SHIFT_ROWS = 40


def _conv_apply(sbuf_ref, w_ref, out_ref, tt, offsets, bias_ref=None):
    d = out_ref.shape[1]
    for cc in range(d // LANES):
        cs = slice(cc * LANES, (cc + 1) * LANES)
        taps = [jnp.broadcast_to(w_ref[k:k + 1, cs], (SUBLANES, LANES)) for k in range(CONV_WIDTH)]
        bias = None if bias_ref is None else jnp.broadcast_to(bias_ref[:, cs], (SUBLANES, LANES))

        def row_body(r, carry, cs=cs, taps=taps, bias=bias):
            r0 = pl.multiple_of(r * CONV_ROWS, CONV_ROWS)
            for q in range(CONV_ROWS // SUBLANES):
                acc = _tap(sbuf_ref, r0 + q * SUBLANES, cs, offsets[0]) * taps[0]
                for k in range(1, CONV_WIDTH):
                    acc = acc + _tap(sbuf_ref, r0 + q * SUBLANES, cs, offsets[k]) * taps[k]
                if bias is not None:
                    acc = acc + bias
                out_ref[pl.ds(r0 + q * SUBLANES, SUBLANES), cs] = acc
            return carry

        lax.fori_loop(0, tt // CONV_ROWS, row_body, 0)


def _fill_shifts(sbuf_ref, rows):
    d = sbuf_ref.shape[2]
    assert rows % SHIFT_ROWS == 0

    def row_body(i, carry):
        r0 = pl.multiple_of(i * SHIFT_ROWS, SUBLANES)
        for cc in range(d // CONV_COLS):
            cs = slice(cc * CONV_COLS, (cc + 1) * CONV_COLS)
            win = sbuf_ref[0, pl.ds(r0, SHIFT_ROWS + SUBLANES), cs]
            for sh in range(1, SUBLANES):
                sbuf_ref[sh, pl.ds(r0, SHIFT_ROWS), cs] = win[sh:sh + SHIFT_ROWS, :]
        return carry

    lax.fori_loop(0, rows // SHIFT_ROWS, row_body, 0)


def _tap(sbuf_ref, r0, cs, offset):
    sh = offset % SUBLANES
    return sbuf_ref[sh, pl.ds(pl.multiple_of(r0 + (offset - sh), SUBLANES), SUBLANES), cs]


def _conv_specs(bl, s, tt, d, col_a, col_g):
    nj = s // tt
    per = tt // CONV_HALO
    main_a = pl.BlockSpec((tt, d), lambda b, j: (b * nj + j, col_a))
    main_g = pl.BlockSpec((tt, d), lambda b, j: (b * nj + j, col_g))
    prev = lambda b, j: jnp.maximum((b * nj + j) * per - 1, 0)
    halo_a = pl.BlockSpec((CONV_HALO, d), lambda b, j: (prev(b, j), col_a))
    halo_g = pl.BlockSpec((CONV_HALO, d), lambda b, j: (prev(b, j), col_g))
    return main_a, main_g, halo_a, halo_g


def _fill_glu(sbuf_ref, a_ref, g_ref, ha_ref, hg_ref, tt):
    first = pl.program_id(1) == 0
    ha = ha_ref[...].astype(F32)
    hg = hg_ref[...].astype(F32)
    sbuf_ref[0, pl.ds(0, CONV_HALO), :] = jnp.where(first, 0.0, ha * _sigmoid(hg))
    av = a_ref[...].astype(F32)
    gv = g_ref[...].astype(F32)
    sbuf_ref[0, pl.ds(CONV_HALO, tt), :] = av * _sigmoid(gv)
    _fill_shifts(sbuf_ref, tt + CONV_HALO - SUBLANES)


def _conv_fwd(p, conv_w, conv_b, ln_g, ln_b, *, bl, s, name):
    t = p.shape[0]
    d = conv_w.shape[1]
    tt = min(TOKEN_TILE, s)
    off = CONV_HALO - (CONV_WIDTH - 1)

    def body(a_ref, g_ref, ha_ref, hg_ref, w_ref, b_ref, lg_ref, lb_ref, c_ref, act_ref, sbuf_ref, cbuf_ref):
        _fill_glu(sbuf_ref, a_ref, g_ref, ha_ref, hg_ref, tt)

        _conv_apply(sbuf_ref, w_ref, cbuf_ref, tt, [off + k for k in range(CONV_WIDTH)], bias_ref=b_ref)
        cv = cbuf_ref[...]
        c_ref[...] = cv.astype(BF16)
        mu = jnp.mean(cv, axis=-1, keepdims=True)
        dv = cv - mu
        rstd = lax.rsqrt(jnp.mean(dv * dv, axis=-1, keepdims=True) + LN_EPS)
        aln = dv * rstd * lg_ref[...] + lb_ref[...]
        act_ref[...] = (aln * _sigmoid(aln)).astype(BF16)

    main_a, main_g, halo_a, halo_g = _conv_specs(bl, s, tt, d, 0, 1)
    out_spec = pl.BlockSpec((tt, d), lambda b, j: (b * (s // tt) + j, 0))
    return pl.pallas_call(
        body, name=name, grid=(bl, s // tt),
        in_specs=[main_a, main_g, halo_a, halo_g, _const_spec((CONV_HALO, d)), _const_spec((1, d)), _const_spec((1, d)),
                  _const_spec((1, d))],
        out_specs=[out_spec, out_spec],
        out_shape=[jax.ShapeDtypeStruct((t, d), BF16), jax.ShapeDtypeStruct((t, d), BF16)],
        scratch_shapes=[pltpu.VMEM((SUBLANES, tt + CONV_HALO, d), F32), pltpu.VMEM((tt, d), F32)],
        compiler_params=_cparams(("parallel", "parallel")))(p, p, p, p, conv_w, conv_b, ln_g, ln_b)


def _conv_bwd(dp, dc, p, conv_w, h1, *, bl, s, name):
    t = p.shape[0]
    d = conv_w.shape[1]
    tt = min(TOKEN_TILE, s)
    nj = s // tt
    per = tt // CONV_HALO
    off = CONV_HALO - (CONV_WIDTH - 1)
    last_blk = t // CONV_HALO - 1

    def body(dp_in, dc_ref, dcn_ref, a_ref, g_ref, ha_ref, hg_ref, w_ref, h1_ref, dp_ref, dw_ref, db_ref, dwin_ref,
             gbuf_ref, dbuf_ref, dglu_ref, acc_ref):
        del dp_in
        b, j = pl.program_id(0), pl.program_id(1)
        start = jnp.logical_and(b == 0, j == 0)
        end = jnp.logical_and(b == bl - 1, j == nj - 1)

        @pl.when(start)
        def _():
            acc_ref[...] = jnp.zeros_like(acc_ref)
            db_ref[...] = jnp.zeros_like(db_ref)
            dwin_ref[...] = jnp.zeros_like(dwin_ref)

        _fill_glu(gbuf_ref, a_ref, g_ref, ha_ref, hg_ref, tt)
        dcv = dc_ref[...].astype(F32)
        dbuf_ref[0, pl.ds(0, tt), :] = dcv
        dbuf_ref[0, pl.ds(tt, CONV_HALO), :] = jnp.where(j == nj - 1, 0.0, dcn_ref[...].astype(F32))
        _fill_shifts(dbuf_ref, tt + CONV_HALO - SUBLANES)
        db_ref[...] += jnp.sum(dcv, axis=0, keepdims=True)

        for cc in range(d // LANES):
            cs = slice(cc * LANES, (cc + 1) * LANES)

            def row_body(r, accs, cs=cs):
                r0 = pl.multiple_of(r * CONV_ROWS, CONV_ROWS)
                accs = list(accs)
                for q in range(CONV_ROWS // SUBLANES):
                    dcw = dbuf_ref[0, pl.ds(r0 + q * SUBLANES, SUBLANES), cs]
                    for k in range(CONV_WIDTH):
                        accs[k] = accs[k] + dcw * _tap(gbuf_ref, r0 + q * SUBLANES, cs, off + k)
                return tuple(accs)

            zero = jnp.zeros((SUBLANES, LANES), F32)
            accs = lax.fori_loop(0, tt // CONV_ROWS, row_body, (zero,) * CONV_WIDTH)
            for k in range(CONV_WIDTH):
                acc_ref[k, :, cs] += accs[k]

        _conv_apply(dbuf_ref, w_ref, dglu_ref, tt, [CONV_WIDTH - 1 - k for k in range(CONV_WIDTH)])
        dglu = dglu_ref[...]
        av = a_ref[...].astype(F32)
        sg = _sigmoid(g_ref[...].astype(F32))
        dp_ref[:, 0:d] = (dglu * sg).astype(BF16)
        dp_ref[:, d:2 * d] = (dglu * av * sg * (1.0 - sg)).astype(BF16)
        dwin_ref[...] += _dot(h1_ref[...], dp_ref[...], _TN)

        @pl.when(end)
        def _():
            for k in range(CONV_WIDTH):
                dw_ref[k:k + 1, :] = jnp.sum(acc_ref[k], axis=0, keepdims=True)
            dw_ref[CONV_WIDTH:CONV_HALO, :] = jnp.zeros((CONV_HALO - CONV_WIDTH, d), F32)

    main_a, main_g, halo_a, halo_g = _conv_specs(bl, s, tt, d, 0, 1)
    dc_main = pl.BlockSpec((tt, d), lambda b, j: (b * nj + j, 0))
    dc_next = pl.BlockSpec((CONV_HALO, d), lambda b, j: (jnp.minimum((b * nj + j + 1) * per, last_blk), 0))
    return pl.pallas_call(
        body, name=name, grid=(bl, nj),
        in_specs=[pl.BlockSpec(memory_space=pl.ANY), dc_main, dc_next, main_a, main_g, halo_a, halo_g,
                  _const_spec((CONV_HALO, d)), dc_main],
        out_specs=[pl.BlockSpec((tt, 2 * d), lambda b, j: (b * nj + j, 0)), _const_spec((CONV_HALO, d)), _const_spec((1, d)),
                   _const_spec((d, 2 * d))],
        out_shape=[jax.ShapeDtypeStruct(dp.shape, dp.dtype), jax.ShapeDtypeStruct((CONV_HALO, d), F32),
                   jax.ShapeDtypeStruct((1, d), F32), jax.ShapeDtypeStruct((d, 2 * d), F32)],
        scratch_shapes=[pltpu.VMEM((SUBLANES, tt + CONV_HALO, d), F32), pltpu.VMEM((SUBLANES, tt + CONV_HALO, d), F32),
                        pltpu.VMEM((tt, d), F32), pltpu.VMEM((CONV_HALO, SUBLANES, d), F32)],
        input_output_aliases={0: 0},
        compiler_params=_cparams(("arbitrary", "arbitrary")))(dp, dc, dc, p, p, p, p, conv_w, h1)


def _sgu_stats(bv):
    gv = _gelu(bv)
    mu = jnp.mean(gv, axis=-1, keepdims=True)
    dv = gv - mu
    rstd = lax.rsqrt(jnp.mean(dv * dv, axis=-1, keepdims=True) + LN_EPS)
    return dv * rstd, rstd


def _sgu_fwd(p, wm, bias, ln_g, ln_b, *, name):
    t = p.shape[0]
    d = ln_g.shape[1]
    tt = SGU_TILE
    gd = d // SGU_GROUPS

    def body(u_ref, v_ref, wm_ref, bias_ref, lg_ref, lb_ref, sg_ref, vn_ref):
        u = _gelu(u_ref[...].astype(F32))
        vhat, _ = _sgu_stats(v_ref[...].astype(F32))
        vb = (vhat * lg_ref[...] + lb_ref[...]).astype(BF16)
        vn_ref[...] = vb
        for ci in range(tt // SGU_CHUNK):
            rows = slice(ci * SGU_CHUNK, (ci + 1) * SGU_CHUNK)
            for g in range(SGU_GROUPS):
                gs = slice(g * gd, (g + 1) * gd)
                z = _dot(wm_ref[g], vb[rows, gs], _NN) + bias_ref[g]
                sg_ref[rows, gs] = (u[rows, gs] * z).astype(BF16)

    rs = _row_spec(tt, d)
    return pl.pallas_call(
        body, name=name, grid=(t // tt,),
        in_specs=[_row_spec(tt, d, 2), _row_spec(tt, d, 3), _const_spec(wm.shape), _const_spec(bias.shape),
                  _const_spec((1, d)), _const_spec((1, d))],
        out_specs=[rs, rs], out_shape=[jax.ShapeDtypeStruct((t, d), BF16), jax.ShapeDtypeStruct((t, d), BF16)],
        compiler_params=_cparams(("parallel",)))(p, p, wm, bias, ln_g, ln_b)


def _sgu_bwd(dp, dy_b, w_out, p, vn, wm, wmt, bias, ln_g, *, name):
    t = p.shape[0]
    d = ln_g.shape[1]
    tt = SGU_TILE
    ck = SGU_CHUNK
    gd = d // SGU_GROUPS
    nsteps = t // tt

    def body(dp_in, dyb_ref, wout_ref, u_ref, v_ref, vn_ref, wm_ref, wmt_ref, bias_ref, lg_ref,
             dp_ref, dw_ref, dbs_ref, dlg_ref, dlb_ref, dz_acc):
        del dp_in
        i = pl.program_id(0)

        @pl.when(i == 0)
        def _():
            dw_ref[...] = jnp.zeros_like(dw_ref)
            dlg_ref[...] = jnp.zeros_like(dlg_ref)
            dlb_ref[...] = jnp.zeros_like(dlb_ref)
            dz_acc[...] = jnp.zeros_like(dz_acc)

        bu = u_ref[...].astype(F32)
        bv = v_ref[...].astype(F32)
        u = _gelu(bu)
        vhat, rstd = _sgu_stats(bv)
        vb = vn_ref[...]
        dsg = _dot(dyb_ref[...], wout_ref[...], _NT)
        row = lax.broadcasted_iota(jnp.int32, (ck, ck), 0)
        col = lax.broadcasted_iota(jnp.int32, (ck, ck), 1)
        causal = col <= row
        du_rows, dv_rows = [], []
        for ci in range(tt // ck):
            rows = slice(ci * ck, (ci + 1) * ck)
            du_parts, dv_parts = [], []
            for g in range(SGU_GROUPS):
                gs = slice(g * gd, (g + 1) * gd)
                z = _dot(wm_ref[g], vb[rows, gs], _NN) + bias_ref[g]
                du_parts.append(dsg[rows, gs] * z)
                dz = dsg[rows, gs] * u[rows, gs]
                dz_acc[:, gs] += dz
                dzb = dz.astype(BF16)
                dw_ref[g] += jnp.where(causal, _dot(dzb, vb[rows, gs], _NT), 0.0)
                dv_parts.append(_dot(wmt_ref[g], dzb, _NN))
            du_rows.append(jnp.concatenate(du_parts, axis=1))
            dv_rows.append(jnp.concatenate(dv_parts, axis=1))
        du = jnp.concatenate(du_rows, axis=0)
        dv = jnp.concatenate(dv_rows, axis=0)
        dp_ref[:, 0:d] = (du * _gelu_grad(bu)).astype(BF16)
        dlb_ref[...] += jnp.sum(dv, axis=0, keepdims=True)
        dlg_ref[...] += jnp.sum(dv * vhat, axis=0, keepdims=True)
        dvh = dv * lg_ref[...]
        dgv = rstd * (dvh - jnp.mean(dvh, axis=-1, keepdims=True) - vhat * jnp.mean(dvh * vhat, axis=-1, keepdims=True))
        dp_ref[:, d:2 * d] = (dgv * _gelu_grad(bv)).astype(BF16)

        @pl.when(i == nsteps - 1)
        def _():
            ones = jnp.ones((8, gd), F32)
            for g in range(SGU_GROUPS):
                gs = slice(g * gd, (g + 1) * gd)
                tot = lax.dot_general(ones, dz_acc[:, gs], (_NT, ((), ())), preferred_element_type=F32,
                                      precision=lax.Precision.HIGHEST)
                dbs_ref[g:g + 1, :] = tot[0:1, :]

    rs = _row_spec(tt, d)
    c1 = _const_spec((1, d))
    return pl.pallas_call(
        body, name=name, grid=(nsteps,),
        in_specs=[pl.BlockSpec(memory_space=pl.ANY), rs, _const_spec(w_out.shape), _row_spec(tt, d, 2), _row_spec(tt, d, 3),
                  rs, _const_spec(wm.shape), _const_spec(wmt.shape), _const_spec(bias.shape), c1],
        out_specs=[pl.BlockSpec((tt, 2 * d), lambda i: (i, 1)), _const_spec(wm.shape), _const_spec((SGU_GROUPS, ck)), c1, c1],
        out_shape=[jax.ShapeDtypeStruct(dp.shape, dp.dtype), jax.ShapeDtypeStruct(wm.shape, F32),
                   jax.ShapeDtypeStruct((SGU_GROUPS, ck), F32), jax.ShapeDtypeStruct((1, d), F32),
                   jax.ShapeDtypeStruct((1, d), F32)],
        scratch_shapes=[pltpu.VMEM((ck, d), F32)],
        input_output_aliases={0: 0},
        compiler_params=_cparams(("arbitrary",)))(dp, dy_b, w_out, p, p, vn, wm, wmt, bias, ln_g)


def _softmax_rows(s):
    e = jnp.exp(s - jnp.max(s, axis=-1, keepdims=True))
    return e / jnp.sum(e, axis=-1, keepdims=True)


def _attn_fwd(q, kv, x1, w_xo, gain, *, bl, s, name):
    t, d = q.shape
    mlen = kv.shape[0] // bl
    hd = d // HEADS
    tq = min(ATTN_TILE, s)
    nq = s // tq
    scale = hd ** -0.5

    def body(q_ref, kv_ref, x1_ref, w_ref, g_ref, o_ref, x2_ref, h_ref):
        for h in range(HEADS):
            hs = slice(h * hd, (h + 1) * hd)
            vs = slice(d + h * hd, d + (h + 1) * hd)
            pr = _softmax_rows(_dot(q_ref[:, hs], kv_ref[:, hs], _NT) * scale)
            o_ref[:, hs] = _dot(pr.astype(BF16), kv_ref[:, vs], _NN).astype(BF16)
        x2 = x1_ref[...] + _dot(o_ref[...], w_ref[...], _NN)
        x2_ref[...] = x2
        h_ref[...] = _rms_apply(x2, g_ref[...]).astype(BF16)

    qs = pl.BlockSpec((tq, d), lambda b, j: (b * nq + j, 0))
    return pl.pallas_call(
        body, name=name, grid=(bl, nq),
        in_specs=[qs, pl.BlockSpec((mlen, 2 * d), lambda b, j: (b, 0)), qs, _const_spec(w_xo.shape), _const_spec((1, d))],
        out_specs=[qs, qs, qs],
        out_shape=[jax.ShapeDtypeStruct((t, d), BF16), jax.ShapeDtypeStruct((t, d), F32), jax.ShapeDtypeStruct((t, d), BF16)],
        compiler_params=_cparams(("parallel", "parallel")))(q, kv, x1, w_xo, gain)


def _attn_bwd(q, kv, do, *, bl, s, name):
    t, d = q.shape
    mlen = kv.shape[0] // bl
    hd = d // HEADS
    tq = min(ATTN_TILE, s)
    nq = s // tq
    scale = hd ** -0.5

    def body(q_ref, kv_ref, do_ref, dq_ref, dkv_ref):
        @pl.when(pl.program_id(1) == 0)
        def _():
            dkv_ref[...] = jnp.zeros_like(dkv_ref)

        for h in range(HEADS):
            hs = slice(h * hd, (h + 1) * hd)
            vs = slice(d + h * hd, d + (h + 1) * hd)
            qh, kh, vh, doh = q_ref[:, hs], kv_ref[:, hs], kv_ref[:, vs], do_ref[:, hs]
            pr = _softmax_rows(_dot(qh, kh, _NT) * scale)
            dpr = _dot(doh, vh, _NT)
            dkv_ref[:, vs] += _dot(pr.astype(BF16), doh, _TN)
            ds = (pr * (dpr - jnp.sum(dpr * pr, axis=-1, keepdims=True)) * scale).astype(BF16)
            dq_ref[:, hs] = _dot(ds, kh, _NN).astype(BF16)
            dkv_ref[:, hs] += _dot(ds, qh, _TN)

    qs = pl.BlockSpec((tq, d), lambda b, j: (b * nq + j, 0))
    ks = pl.BlockSpec((mlen, 2 * d), lambda b, j: (b, 0))
    return pl.pallas_call(
        body, name=name, grid=(bl, nq), in_specs=[qs, ks, qs], out_specs=[qs, ks],
        out_shape=[jax.ShapeDtypeStruct((t, d), BF16), jax.ShapeDtypeStruct(kv.shape, F32)],
        compiler_params=_cparams(("parallel", "arbitrary")))(q, kv, do)


def _mesh_pos():
    return lax.axis_index("x"), lax.axis_index("y"), lax.axis_index("c")


def _all_gather(arrs, *, name):
    n = len(arrs)
    hbm = pl.BlockSpec(memory_space=pl.ANY)

    def body(*refs):
        ins, outs = refs[:n], refs[n:2 * n]
        send_sems, recv_sems, loc_sems = refs[2 * n:]
        x, y, c = _mesh_pos()
        me, sib = (x, y, c), (x, y, 1 - c)
        chips = [(1 - x, y), (x, 1 - y), (1 - x, 1 - y)]

        def idx(dev):
            return 4 * dev[0] + 2 * dev[1] + dev[2]

        def copy(w, k, block, to, from_input=False):
            return pltpu.make_async_remote_copy(
                src_ref=ins[w] if from_input else outs[w].at[idx(block)], dst_ref=outs[w].at[idx(block)],
                send_sem=send_sems.at[w, k], recv_sem=recv_sems.at[w, k], device_id=to, device_id_type=MESH_ID)

        own = [pltpu.make_async_copy(ins[w], outs[w].at[idx(me)], loc_sems.at[w]) for w in range(n)]
        for cp in own:
            cp.start()
        first = []
        for w in range(n):
            first.append(copy(w, 0, me, sib, True))
            first += [copy(w, 1 + j, me, (*chip, c), True) for j, chip in enumerate(chips)]
        for cp in first:
            cp.start()
        passed = []
        for j, chip in enumerate(chips):
            for w in range(n):
                copy(w, 1 + j, (*chip, c), me).wait_recv()
                fwd = copy(w, 4 + j, (*chip, c), sib)
                fwd.start()
                passed.append(fwd)
        for w in range(n):
            copy(w, 0, sib, me).wait_recv()
            for j, chip in enumerate(chips):
                copy(w, 4 + j, (*chip, 1 - c), me).wait_recv()
        for cp in first + passed:
            cp.wait_send()
        for cp in own:
            cp.wait()

    return pl.pallas_call(
        body, name=name, in_specs=[hbm] * n, out_specs=[hbm] * n,
        out_shape=[jax.ShapeDtypeStruct((N_DEV, *a.shape), a.dtype) for a in arrs],
        scratch_shapes=[pltpu.SemaphoreType.DMA((n, 7)), pltpu.SemaphoreType.DMA((n, 7)), pltpu.SemaphoreType.DMA((n,))],
    )(*arrs)


_HBM = pl.BlockSpec(memory_space=pltpu.HBM)
_SEM = pl.BlockSpec(memory_space=pltpu.SEMAPHORE)
_ANY = pl.BlockSpec(memory_space=pl.ANY)
_EFFECT = pltpu.SideEffectType.DATAFLOW_SIDE_EFFECTING
N_PEERS = N_DEV - 1


def _related(pos, r):
    x, y, c = pos
    return (1 - x if r & 4 else x, 1 - y if r & 2 else y, 1 - c if r & 1 else c)


def _dev_index(dev):
    return 4 * dev[0] + 2 * dev[1] + dev[2]


def _in_hbm(a):
    return pltpu.with_memory_space_constraint(a, pltpu.HBM)


def _split_copies(kind, srcs, lands, send_sems, recv_sems):
    pos = _mesh_pos()
    me = _dev_index(pos)
    out = []
    for w in range(len(srcs)):
        for r in range(1, N_DEV):
            peer = _related(pos, r)
            if kind == "gather":
                src, dst_here, dst_there = srcs[w], lands[w].at[_dev_index(peer)], lands[w].at[me]
            elif srcs[w].ndim == 2:
                cb = lands[w].shape[2]
                src = srcs[w].at[:, pl.ds(pl.multiple_of(_dev_index(peer) * cb, LANES), cb)]
                dst_here = dst_there = lands[w].at[r - 1]
            else:
                src, dst_here, dst_there = srcs[w].at[_dev_index(peer)], lands[w].at[r - 1], lands[w].at[r - 1]
            out.append((src, dst_here, dst_there, send_sems.at[w * N_PEERS + r - 1], recv_sems.at[w * N_PEERS + r - 1], peer))
    return out


def _copy_start(kind, srcs, land_shapes, *, name, after=None):
    n = len(srcs)
    n_after = 0 if after is None else 1

    def body(*refs):
        src_refs, land_refs = refs[:n], refs[n:2 * n]
        send_sems, recv_sems = refs[2 * n + n_after], refs[2 * n + n_after + 1]
        token = refs[-1]
        for src, _, dst, ssem, rsem, peer in _split_copies(kind, src_refs, land_refs, send_sems, recv_sems):
            pltpu.make_async_remote_copy(src_ref=src, dst_ref=dst, send_sem=ssem, recv_sem=rsem, device_id=peer,
                                         device_id_type=MESH_ID).start()
        token[...] = jnp.zeros_like(token)

    lands = [_in_hbm(lax.empty(shape, s.dtype)) for s, shape in zip(srcs, land_shapes)]
    res = pl.pallas_call(
        body, name=name,
        out_shape=(pltpu.SemaphoreType.DMA((n * N_PEERS,)), pltpu.SemaphoreType.DMA((n * N_PEERS,)),
                   *[pltpu.HBM(s.shape, s.dtype) for s in srcs], *[pltpu.HBM(l.shape, l.dtype) for l in lands],
                   jax.ShapeDtypeStruct((8, 128), F32)),
        in_specs=[_HBM] * (2 * n) + [_ANY] * n_after,
        out_specs=(_SEM, _SEM, *[_HBM] * (2 * n), pl.BlockSpec(memory_space=pltpu.VMEM)),
        input_output_aliases={i: 2 + i for i in range(2 * n)},
        compiler_params=pltpu.CompilerParams(has_side_effects=_EFFECT),
    )(*[_in_hbm(s) for s in srcs], *lands, *([] if after is None else [after]))
    return res[0], res[1], list(res[2:2 + n]), list(res[2 + n:2 + 2 * n]), res[-1]


def _copy_wait(kind, send_sems, recv_sems, srcs, lands, after, *, name):
    n = len(srcs)

    def body(*refs):
        src_refs, land_refs = refs[:n], refs[n:2 * n]
        ssems, rsems = refs[2 * n], refs[2 * n + 1]
        for src, dst, _, ssem, rsem, peer in _split_copies(kind, src_refs, land_refs, ssems, rsems):
            cp = pltpu.make_async_remote_copy(src_ref=src, dst_ref=dst, send_sem=ssem, recv_sem=rsem, device_id=peer,
                                              device_id_type=MESH_ID)
            cp.wait_send()
            cp.wait_recv()

    res = pl.pallas_call(
        body, name=name,
        out_shape=(*[pltpu.HBM(s.shape, s.dtype) for s in srcs], *[pltpu.HBM(l.shape, l.dtype) for l in lands]),
        in_specs=[_HBM] * (2 * n) + [_SEM, _SEM, _ANY], out_specs=tuple([_HBM] * (2 * n)),
        input_output_aliases={i: i for i in range(2 * n)},
        compiler_params=pltpu.CompilerParams(has_side_effects=_EFFECT),
    )(*srcs, *lands, send_sems, recv_sems, after)
    return list(res[:n]), list(res[n:])


def _row_tile(rows):
    return max(tr for tr in range(16, min(rows, 512) + 1, 16) if rows % tr == 0)


def _adamw_math(w, g, m, v):
    m2 = ADAM_B1 * m + (1.0 - ADAM_B1) * g
    v2 = ADAM_B2 * v + (1.0 - ADAM_B2) * (g * g)
    m_hat = m2 / (1.0 - ADAM_B1 ** ADAM_STEP)
    v_hat = v2 / (1.0 - ADAM_B2 ** ADAM_STEP)
    delta = -ADAM_LR * (m_hat / (jnp.sqrt(v_hat) + ADAM_EPS) + ADAM_WD * w)
    return delta, m2, v2


def _adamw_shard(partials, landed, dev, w, m, v, *, name):
    r, c = w.shape
    tr = _row_tile(r)

    def body(dev_ref, p_ref, l_ref, w_ref, m_ref, v_ref, g_out, d_out, m_out, v_out):
        del dev_ref
        g = p_ref[...].astype(F32)
        for k in range(N_PEERS):
            g = g + l_ref[k].astype(F32)
        delta, m2, v2 = _adamw_math(w_ref[...], g, m_ref[...], v_ref[...])
        g_out[...] = g
        d_out[...] = delta
        m_out[...] = m2
        v_out[...] = v2

    blk = pl.BlockSpec((tr, c), lambda i, dev_ref: (i, 0))
    if partials.ndim == 2:
        own = pl.BlockSpec((tr, c), lambda i, dev_ref: (i, dev_ref[0]))
    else:
        own = pl.BlockSpec((None, tr, c), lambda i, dev_ref: (dev_ref[0], i, 0))
    gs = pltpu.PrefetchScalarGridSpec(
        num_scalar_prefetch=1, grid=(r // tr,),
        in_specs=[own, pl.BlockSpec((N_PEERS, tr, c), lambda i, dev_ref: (0, i, 0)), blk, blk, blk],
        out_specs=[blk] * 4)
    return pl.pallas_call(
        body, name=name, grid_spec=gs, out_shape=[jax.ShapeDtypeStruct((r, c), F32)] * 4,
        compiler_params=_cparams(("parallel",)))(dev, partials, landed, w, m, v)


def _adamw_small(parts, dev, w, m, v, *, name, col_block):
    _, r, d = parts.shape
    cols = w.shape[1]

    def body(dev_ref, p_ref, w_ref, m_ref, v_ref, g_out, d_out, m_out, v_out):
        del dev_ref
        g = p_ref[0]
        for k in range(1, N_DEV):
            g = g + p_ref[k]
        delta, m2, v2 = _adamw_math(w_ref[...], g, m_ref[...], v_ref[...])
        g_out[...] = g
        d_out[...] = delta
        m_out[...] = m2
        v_out[...] = v2

    blk = pl.BlockSpec((r, cols), lambda i, dev_ref: (0, 0))
    pidx = (lambda i, dev_ref: (0, 0, dev_ref[0])) if col_block else (lambda i, dev_ref: (0, 0, 0))
    gs = pltpu.PrefetchScalarGridSpec(
        num_scalar_prefetch=1, grid=(1,),
        in_specs=[pl.BlockSpec((N_DEV, r, cols), pidx), blk, blk, blk], out_specs=[blk] * 4)
    return pl.pallas_call(
        body, name=name, grid_spec=gs, out_shape=[jax.ShapeDtypeStruct((r, cols), F32)] * 4,
        compiler_params=_cparams(("arbitrary",)))(dev, parts, w, m, v)


def _pad_rows(a, rows):
    return jnp.pad(a, ((0, rows - a.shape[0]), (0, 0)))


def _unblock_cols(g):
    return jnp.transpose(g, (1, 0, 2)).reshape(g.shape[1], N_DEV * g.shape[2])


def kernel(x, mem, norm_mix, w_in, b_gate, conv_w, conv_b, conv_ln_g, conv_ln_b, w_conv_out, sgu_ln_g, sgu_ln_b, sgu_w, sgu_b, w_sgu_out, w_mix_out, norm_xattn, norm_mem, w_q, w_kv, w_xo, norm_ffn, w_gu, w_down, norm_final, loss_target, m_norm_mix, m_w_in, m_b_gate, m_conv_w, m_conv_b, m_conv_ln_g, m_conv_ln_b, m_w_conv_out, m_sgu_ln_g, m_sgu_ln_b, m_sgu_w, m_sgu_b, m_w_sgu_out, m_w_mix_out, m_norm_xattn, m_norm_mem, m_w_q, m_w_kv, m_w_xo, m_norm_ffn, m_w_gu, m_w_down, m_norm_final, v_norm_mix, v_w_in, v_b_gate, v_conv_w, v_conv_b, v_conv_ln_g, v_conv_ln_b, v_w_conv_out, v_sgu_ln_g, v_sgu_ln_b, v_sgu_w, v_sgu_b, v_w_sgu_out, v_w_mix_out, v_norm_xattn, v_norm_mem, v_w_q, v_w_kv, v_w_xo, v_norm_ffn, v_w_gu, v_w_down, v_norm_final):
    given = dict(locals())
    bl, s, d = x.shape
    t = bl * s
    xf = x.reshape(t, d)
    tgt = loss_target.reshape(t, d)
    memf = mem.reshape(bl * mem.shape[1], d)
    cx, cy, cc = lax.axis_index("x"), lax.axis_index("y"), lax.axis_index("c")
    dev = 4 * cx + 2 * cy + cc
    dev_id = dev.astype(jnp.int32).reshape(1)
    col_sharded = ["w_in", "w_kv"]
    transposed = ["w_gu"]

    def shard_of(name, prefix=""):
        a = given[prefix + name][0]
        return jnp.transpose(a) if name in transposed else a

    def full_weight(name, blocks):
        return _unblock_cols(blocks) if name in col_sharded else blocks.reshape(N_DEV * blocks.shape[1], blocks.shape[2])

    g_bg, g_cw = _all_gather([_pad_rows(b_gate[0], 8), _pad_rows(conv_w[0], CONV_HALO)], name="gather_small_params")
    h1, p, w_in_blocks = _in_proj_gather(xf, norm_mix + g_bg[0, 7:8, 0:1], w_in[0].astype(BF16), name="in_proj")
    early = ["w_conv_out", "w_sgu_out", "w_mix_out", "w_q", "w_kv", "w_xo"]
    late = ["w_gu", "w_down"]
    shards = {n: shard_of(n).astype(BF16) for n in early + late}
    started = {}
    for grp, names in (("early", early), ("late", late)):
        srcs = [shards[n] for n in names]
        started[grp] = _copy_start("gather", srcs, [(N_DEV, *a.shape) for a in srcs], name=f"gather_{grp}_start", after=p)
    token = started["early"][4][0:1, 0:1] + started["late"][4][0:1, 0:1]
    wfull = {}
    bg_full = _unblock_cols(g_bg)
    cw_full = _unblock_cols(g_cw)

    def finish_gather(grp, names, after):
        ssem, rsem, srcs, lands, _ = started[grp]
        _, lands = _copy_wait("gather", ssem, rsem, srcs, lands, after, name=f"gather_{grp}_wait")
        for n, land in zip(names, lands):
            wfull[n] = full_weight(n, lax.dynamic_update_index_in_dim(land, shards[n], dev, 0))

    tri = jnp.tril(jnp.ones((SGU_CHUNK, SGU_CHUNK), bool))
    wm32 = jnp.where(tri[None], sgu_w[0], 0.0)
    wm = wm32.astype(BF16)
    wmt = jnp.transpose(wm32, (0, 2, 1)).astype(BF16)
    sgu_bias = jnp.broadcast_to(sgu_b[0][:, :, None], (SGU_GROUPS, SGU_CHUNK, d // SGU_GROUPS))

    c_conv, a_act = _conv_fwd(p, cw_full, conv_b + token, conv_ln_g, conv_ln_b, bl=bl, s=s, name="conv_fwd")
    sg, vn = _sgu_fwd(p, wm, sgu_bias, sgu_ln_g, sgu_ln_b + token, name="sgu_fwd")
    finish_gather("early", early, a_act[0:16, 0:128] + sg[0:16, 0:128])
    y_a = _matmul(a_act, wfull["w_conv_out"], mode="nn", out_dtype=BF16, name="mm_conv_out", tm=1024, tn=1024, tk=1024)
    y_b = _matmul(sg, wfull["w_sgu_out"], mode="nn", out_dtype=BF16, name="mm_sgu_out", tm=1024, tn=1024, tk=1024)
    merged, x1, h2, q = _mix_out(p, y_a, y_b, bg_full, xf, wfull["w_mix_out"], norm_xattn, wfull["w_q"], name="mix_out")
    mem_n = _rms_fwd(memf, norm_mem, name="rms_mem")
    kv = _matmul(mem_n, wfull["w_kv"], mode="nn", out_dtype=BF16, name="mm_kv", tm=1024, tn=1024, tk=1024)
    o, x2, h3 = _attn_fwd(q, kv, x1, wfull["w_xo"], norm_ffn, bl=bl, s=s, name="attn_fwd")
    finish_gather("late", late, h3)
    gu, act, dx3, loss_part, d_norm_final = _ffn_fwd(h3, x2, tgt, wfull["w_gu"], wfull["w_down"],
                                                     norm_final.reshape(1, d), name="ffn_fwd")

    grads = {}
    sent = []

    def send_grads(names, tag, after=None):
        blocks, land_shapes = [], []
        for n in names:
            g = grads[n]
            if g.ndim == 2 and n in col_sharded:
                land_shapes.append((N_PEERS, g.shape[0], g.shape[1] // N_DEV))
            else:
                if g.ndim == 2:
                    g = g.reshape(N_DEV, -1, g.shape[1])
                land_shapes.append((N_PEERS, *g.shape[1:]))
            blocks.append(g)
        ssem, rsem, srcs, lands, tok = _copy_start("scatter", blocks, land_shapes, name=f"grads_{tag}_start", after=after)
        sent.append((names, ssem, rsem, srcs, lands))
        return tok[0:1, 0:1]

    dgu, dx2, do, d_norm_ffn = _ffn_bwd(dx3, gu, x2, wfull["w_down"], wfull["w_gu"], norm_ffn, wfull["w_xo"], name="ffn_bwd")
    grads["w_down"] = _matmul(act, dx3, mode="tn", out_dtype=BF16, name="mm_dw_down", tm=1408, tn=1024, tk=2048)
    grads["w_gu"] = _matmul(dgu, h3, mode="tn", out_dtype=BF16, name="mm_dw_gu", tm=1408, tn=1024, tk=2048)
    tok = send_grads(["w_down", "w_gu"], "ffn")
    grads["w_xo"] = _matmul(o, dx2, mode="tn", out_dtype=BF16, name="mm_dw_xo", tm=1024, tn=1024, tk=2048)
    dq, dkv = _attn_bwd(q, kv, do, bl=bl, s=s, name="attn_bwd")
    grads["w_kv"] = _matmul(mem_n, dkv, mode="tn", out_dtype=BF16, name="mm_dw_kv", tm=1024, tn=256, tk=1024,
                            col_blocks=N_DEV)
    tok2 = send_grads(["w_xo", "w_kv"], "attn")
    dmem_n = _matmul(dkv, wfull["w_kv"], mode="nt", out_dtype=F32, name="mm_d_mem", tm=512, tn=1024, tk=2048)
    d_norm_mem = _rms_bwd(None, dmem_n, memf, norm_mem, name="rms_mem_bwd", need_dx=False)
    dx1, d_norm_xattn, dw_q = _proj_rms_bwd(dq, dx2, x1, wfull["w_q"], norm_xattn + (tok + tok2), name="q_rms_bwd", h=h2)
    dp, dy_a, dy_b, d_b_gate, dw_mix, dw_in_gates = _gates_bwd_fused(dx1, p, y_a, y_b, bg_full, wfull["w_mix_out"],
                                                                    merged, h1, name="gates_bwd")
    grads["w_q"] = dw_q.astype(BF16)
    grads["w_mix_out"] = dw_mix.astype(BF16)
    grads["w_sgu_out"] = _matmul(sg, dy_b, mode="tn", out_dtype=BF16, name="mm_dw_sgu", tm=1024, tn=1024, tk=2048)
    dc, d_conv_ln_g, d_conv_ln_b, dw_conv = _conv_ln_bwd_fused(dy_a, c_conv, a_act, wfull["w_conv_out"], conv_ln_g,
                                                               conv_ln_b, name="conv_ln_bwd")
    grads["w_conv_out"] = dw_conv.astype(BF16)
    tok = send_grads(["w_q", "w_mix_out", "w_sgu_out", "w_conv_out"], "mixer")
    dp, d_sgu_w, d_sgu_b, d_sgu_ln_g, d_sgu_ln_b = _sgu_bwd(dp, dy_b, wfull["w_sgu_out"], p, vn, wm, wmt, sgu_bias,
                                                             sgu_ln_g + tok, name="sgu_bwd")
    dw_in_sgu = _matmul(h1, dp, mode="tn", out_dtype=BF16, name="mm_dw_in_sgu", tm=1024, tn=1024, tk=2048,
                        b_cols=(2 * d, 2 * d))
    dp, d_conv_w, d_conv_b, dw_in_conv = _conv_bwd(dp, dc, p, cw_full, h1, bl=bl, s=s, name="conv_bwd")
    grads["w_in"] = jnp.concatenate([dw_in_conv.astype(BF16), dw_in_sgu, dw_in_gates], axis=1)
    tok = send_grads(["w_in"], "in")
    grad_x, d_norm_mix = _proj_rms_bwd(dp, dx1, xf, w_in_blocks, norm_mix + tok, name="in_proj_bwd")
    out = {}

    rep_names = ["norm_mix", "conv_b", "conv_ln_g", "conv_ln_b", "sgu_ln_g", "sgu_ln_b", "norm_xattn", "norm_mem",
                 "norm_ffn", "norm_final", "sgu_b"]
    rep_grads = [d_norm_mix, d_conv_b, d_conv_ln_g, d_conv_ln_b, d_sgu_ln_g, d_sgu_ln_b, d_norm_xattn, d_norm_mem,
                 d_norm_ffn, d_norm_final, d_sgu_b.reshape(1, d)]
    nrep = len(rep_names)
    pad = jnp.zeros((16 - nrep, d), F32)
    sgw_rows = SGU_GROUPS * SGU_CHUNK * SGU_CHUNK // d

    def pack_rep(vecs, sgw, extra=None):
        fill = pad if extra is None else jnp.concatenate([extra, pad[1:]], axis=0)
        return jnp.concatenate([v.reshape(1, d) for v in vecs] + [fill, sgw.reshape(sgw_rows, d)], axis=0)

    def pack_col(bg, cw):
        return jnp.concatenate([_pad_rows(bg, 8), _pad_rows(cw, CONV_HALO)], axis=0)

    small_a = pack_rep(rep_grads, d_sgu_w, extra=jnp.broadcast_to(loss_part, (1, d)))
    small_b = jnp.concatenate([d_b_gate, d_conv_w], axis=0)
    parts_a, parts_b = _all_gather([small_a, small_b], name="gather_small_grads")
    res_a = _adamw_small(parts_a, dev_id, pack_rep([given[n] for n in rep_names], sgu_w),
                         pack_rep([given["m_" + n] for n in rep_names], m_sgu_w),
                         pack_rep([given["v_" + n] for n in rep_names], v_sgu_w), name="adamw_small", col_block=False)
    res_b = _adamw_small(parts_b, dev_id, pack_col(b_gate[0], conv_w[0]), pack_col(m_b_gate[0], m_conv_w[0]),
                         pack_col(v_b_gate[0], v_conv_w[0]), name="adamw_small_cols", col_block=True)
    for i, n in enumerate(rep_names):
        out[n] = [r[i].reshape(given[n].shape) for r in res_a]
    out["sgu_w"] = [r[16:16 + sgw_rows].reshape(sgu_w.shape) for r in res_a]
    out["b_gate"] = [r[0:2][None] for r in res_b]
    out["conv_w"] = [r[8:8 + CONV_WIDTH][None] for r in res_b]

    done = res_a[0]
    for names, ssem, rsem, srcs, lands in sent:
        srcs, lands = _copy_wait("scatter", ssem, rsem, srcs, lands, done, name=f"grads_{names[0]}_wait")
        for n, partials, landed in zip(names, srcs, lands):
            res = _adamw_shard(partials, landed, dev_id, shard_of(n), shard_of(n, "m_"), shard_of(n, "v_"),
                               name=f"adamw_{n}")
            done = res[0]
            out[n] = [(jnp.transpose(r) if n in transposed else r)[None] for r in res]

    order = ["norm_mix", "w_in", "b_gate", "conv_w", "conv_b", "conv_ln_g", "conv_ln_b", "w_conv_out", "sgu_ln_g",
             "sgu_ln_b", "sgu_w", "sgu_b", "w_sgu_out", "w_mix_out", "norm_xattn", "norm_mem", "w_q", "w_kv", "w_xo",
             "norm_ffn", "w_gu", "w_down", "norm_final"]
    loss = res_a[0][nrep, 0]
    return (loss, grad_x.reshape(x.shape), *[out[n][0] for n in order], *[out[n][1] for n in order],
            *[out[n][2] for n in order], *[out[n][3] for n in order])
```

```python
import functools

import jax
import jax.numpy as jnp
from jax import lax
from jax.experimental import pallas as pl
from jax.experimental.pallas import tpu as pltpu

F32 = jnp.float32
BF16 = jnp.bfloat16
RMS_EPS = 1e-6
LN_EPS = 1e-5
CONV_WIDTH = 31
CONV_HALO = 32
CONV_ROWS = 128
CONV_COLS = 256
LANES = 128
SGU_CHUNK = 128
SGU_GROUPS = 8
SGU_TILE = 512
HEADS = 4
N_DEV = 8
ADAM_LR, ADAM_B1, ADAM_B2, ADAM_EPS, ADAM_WD, ADAM_STEP = 0.001, 0.9, 0.999, 1e-08, 0.01, 10
VMEM_LIMIT = 56 * 1024 * 1024
TOKEN_TILE = 256
ATTN_TILE = 1024
MESH_ID = pl.DeviceIdType.MESH

_GELU_K = 0.7978845608028654
_GELU_C = 0.044715


def _cparams(sem=None):
    return pltpu.CompilerParams(dimension_semantics=sem, vmem_limit_bytes=VMEM_LIMIT)


def _sigmoid(v):
    return 0.5 * jnp.tanh(0.5 * v) + 0.5


def _gelu(v):
    return 0.5 * v * (1.0 + jnp.tanh(_GELU_K * (v + _GELU_C * v * v * v)))


def _gelu_grad(v):
    th = jnp.tanh(_GELU_K * (v + _GELU_C * v * v * v))
    return 0.5 * (1.0 + th) + 0.5 * v * (1.0 - th * th) * _GELU_K * (1.0 + 3.0 * _GELU_C * v * v)


def _dot(a, b, dims):
    return lax.dot_general(a, b, (dims, ((), ())), preferred_element_type=F32)


_NN = ((1,), (0,))
_NT = ((1,), (1,))
_TN = ((0,), (0,))


def _matmul(a, b, *, mode, out_dtype, name, tm=512, tn=512, tk=512, chunk=None, residual=None, rms_gain=None,
            col_blocks=None, b_cols=None):
    if mode == "nn":
        (m, k), (_, n) = a.shape, b.shape
    elif mode == "nt":
        (m, k), (n, _) = a.shape, b.shape
    else:
        (k, m), (_, n) = a.shape, b.shape
    b_first = 0
    if b_cols is not None:
        assert mode == "tn"
        b_first, n = b_cols
    tm, tn, tk = min(tm, m), min(tn, n), min(tk, k)
    assert b_first % tn == 0
    b_first //= tn
    assert m % tm == 0 and n % tn == 0 and k % tk == 0, (name, a.shape, b.shape, tm, tn, tk)
    nk = k // tk
    dims = {"nn": _NN, "nt": _NT, "tn": _TN}[mode]
    chunk = tn if chunk is None else min(chunk, tn)
    assert tn % chunk == 0
    if rms_gain is not None:
        assert tn == n and chunk == n

    def body(*refs):
        refs = list(refs)
        a_ref, b_ref = refs[:2]
        pos = 2
        r_ref = g_ref = None
        if residual is not None:
            r_ref = refs[pos]
            pos += 1
        if rms_gain is not None:
            g_ref = refs[pos]
            pos += 1
        o_ref = refs[pos]
        pos += 1
        h_ref = None
        if rms_gain is not None:
            h_ref = refs[pos]
            pos += 1
        acc_ref = refs[pos] if nk > 1 else None
        av = a_ref[...].astype(BF16)
        for c0 in range(0, tn, chunk):
            cs = slice(c0, c0 + chunk)
            bv = (b_ref[cs, :] if mode == "nt" else b_ref[:, cs]).astype(BF16)
            part = _dot(av, bv, dims)

            def finish(res, cs=cs):
                if r_ref is not None:
                    res = res + r_ref[:, cs].astype(F32)
                o_ref[:, cs] = res.astype(out_dtype)
                if h_ref is not None:
                    r = lax.rsqrt(jnp.mean(res * res, axis=-1, keepdims=True) + RMS_EPS)
                    h_ref[...] = (res * r * g_ref[...]).astype(BF16)

            if nk == 1:
                finish(part)
            else:
                kk = pl.program_id(2)

                @pl.when(kk == 0)
                def _(part=part, cs=cs):
                    acc_ref[:, cs] = part

                @pl.when(kk > 0)
                def _(part=part, cs=cs):
                    acc_ref[:, cs] += part

                @pl.when(kk == nk - 1)
                def _(finish=finish, cs=cs):
                    finish(acc_ref[:, cs])

    resident = dict(pipeline_mode=pl.Buffered(1)) if (n == tn and nk == 1 and mode != "tn" and m > tm) else {}
    if mode == "nn":
        a_spec = pl.BlockSpec((tm, tk), lambda i, j, kk: (i, kk))
        b_spec = pl.BlockSpec((tk, tn), lambda i, j, kk: (kk, j), **resident)
    elif mode == "nt":
        a_spec = pl.BlockSpec((tm, tk), lambda i, j, kk: (i, kk))
        b_spec = pl.BlockSpec((tn, tk), lambda i, j, kk: (j, kk), **resident)
    else:
        a_spec = pl.BlockSpec((tk, tm), lambda i, j, kk: (kk, i))
        b_spec = pl.BlockSpec((tk, tn), lambda i, j, kk: (kk, j + b_first))
    o_spec = pl.BlockSpec((tm, tn), lambda i, j, kk: (i, j))
    in_specs, args = [a_spec, b_spec], [a, b]
    if residual is not None:
        in_specs.append(o_spec)
        args.append(residual)
    out_shape, out_specs = [jax.ShapeDtypeStruct((m, n), out_dtype)], [o_spec]
    if col_blocks is not None:
        assert residual is None and rms_gain is None and (n // col_blocks) % tn == 0
        per = n // col_blocks // tn
        out_shape = [jax.ShapeDtypeStruct((col_blocks, m, n // col_blocks), out_dtype)]
        out_specs = [pl.BlockSpec((None, tm, tn), lambda i, j, kk: (j // per, i, j % per))]
    if rms_gain is not None:
        in_specs.append(pl.BlockSpec((1, n), lambda i, j, kk: (0, 0)))
        args.append(rms_gain)
        out_shape.append(jax.ShapeDtypeStruct((m, n), BF16))
        out_specs.append(o_spec)
    res = pl.pallas_call(
        body, name=name, grid=(m // tm, n // tn, nk), in_specs=in_specs, out_specs=out_specs, out_shape=out_shape,
        scratch_shapes=[pltpu.VMEM((tm, tn), F32)] if nk > 1 else [],
        compiler_params=_cparams(("parallel", "parallel", "arbitrary")),
    )(*args)
    return res if rms_gain is not None else res[0]


def _row_call(name, t, tm, rows_in, residents, rows_out, accs, body):
    n_in, n_res, n_out, n_acc = len(rows_in), len(residents), len(rows_out), len(accs)
    steps = t // tm
    assert t % tm == 0
    narrow = [i for i, (_, dt) in enumerate(accs) if dt != F32]

    def kernel_body(*refs):
        in_refs, res_refs = refs[:n_in], refs[n_in:n_in + n_res]
        out_refs = refs[n_in + n_res:n_in + n_res + n_out]
        acc_out = list(refs[n_in + n_res + n_out:n_in + n_res + n_out + n_acc])
        scratch = refs[n_in + n_res + n_out + n_acc:]
        acc_refs = list(acc_out)
        for s_ref, i in zip(scratch, narrow):
            acc_refs[i] = s_ref
        if accs:
            @pl.when(pl.program_id(0) == 0)
            def _():
                for acc in acc_refs:
                    acc[...] = jnp.zeros_like(acc)
        body(in_refs, res_refs, out_refs, acc_refs)
        if narrow:
            @pl.when(pl.program_id(0) == steps - 1)
            def _():
                for i in narrow:
                    acc_out[i][...] = acc_refs[i][...].astype(acc_out[i].dtype)

    once = dict(pipeline_mode=pl.Buffered(1)) if steps > 1 else {}
    in_specs = [pl.BlockSpec((tm, cols), lambda i, cb=cb: (i, cb)) for _, cols, cb in rows_in]
    in_specs += [pl.BlockSpec(r.shape, lambda i, nd=r.ndim: (0,) * nd, **once) for r in residents]
    out_specs = [pl.BlockSpec((tm, cols), lambda i, cb=cb: (i, cb)) for _, cols, cb, _ in rows_out]
    out_specs += [pl.BlockSpec(shape, lambda i, nd=len(shape): (0,) * nd) for shape, _ in accs]
    out_shape = [jax.ShapeDtypeStruct((t, total), dt) for total, _, _, dt in rows_out]
    out_shape += [jax.ShapeDtypeStruct(shape, dt) for shape, dt in accs]
    return pl.pallas_call(
        kernel_body, name=name, grid=(steps,), in_specs=in_specs, out_specs=out_specs, out_shape=out_shape,
        scratch_shapes=[pltpu.VMEM(accs[i][0], F32) for i in narrow],
        compiler_params=_cparams(("arbitrary",) if accs else ("parallel",)),
    )(*[a for a, _, _ in rows_in], *residents)


def _rms_apply(xv, gain):
    return xv * lax.rsqrt(jnp.mean(xv * xv, axis=-1, keepdims=True) + RMS_EPS) * gain


def _rms_grad(dres, dh, xv, gain):
    r = lax.rsqrt(jnp.mean(xv * xv, axis=-1, keepdims=True) + RMS_EPS)
    xhat = xv * r
    dxh = dh * gain
    dx = dres + r * (dxh - xhat * jnp.mean(dxh * xhat, axis=-1, keepdims=True))
    return dx, jnp.sum(dh * xhat, axis=0, keepdims=True)


def _in_proj_gather(xf, gain, w_shard, *, name):
    t, d = xf.shape
    cb = w_shard.shape[1]
    tm = min(1024, t)
    steps = t // tm
    mx, my, _ = _mesh_pos()
    order = jnp.stack([2 * mx + my, 2 * (1 - mx) + my, 2 * mx + (1 - my), 2 * (1 - mx) + (1 - my)]).astype(jnp.int32)

    def body(order_ref, x_ref, g_ref, ws_ref, h_ref, p_ref, wout_ref, w_ref, send_sems, recv_sems, own_sem):
        ps, i = pl.program_id(0), pl.program_id(1)
        x, y, c = _mesh_pos()
        me, sib = (x, y, c), (x, y, 1 - c)
        chips = [(1 - x, y), (x, 1 - y), (1 - x, 1 - y)]

        def copy(k, block, to, from_shard=False):
            return pltpu.make_async_remote_copy(
                src_ref=ws_ref if from_shard else w_ref.at[_dev_index(block)], dst_ref=w_ref.at[_dev_index(block)],
                send_sem=send_sems.at[k], recv_sem=recv_sems.at[k], device_id=to, device_id_type=MESH_ID)

        own = pltpu.make_async_copy(ws_ref, w_ref.at[_dev_index(me)], own_sem)
        first = [copy(0, me, sib, True)] + [copy(1 + j, me, (*chip, c), True) for j, chip in enumerate(chips)]
        passed = [copy(4 + j, (*chip, c), sib) for j, chip in enumerate(chips)]

        @pl.when(jnp.logical_and(ps == 0, i == 0))
        def _():
            own.start()
            for cp in first:
                cp.start()
            own.wait()
            copy(0, sib, me).wait_recv()

        for j, chip in enumerate(chips):
            @pl.when(jnp.logical_and(ps == j + 1, i == 0))
            def _(j=j, chip=chip):
                copy(1 + j, (*chip, c), me).wait_recv()
                passed[j].start()
                copy(4 + j, (*chip, 1 - c), me).wait_recv()

        h = _rms_apply(x_ref[...], g_ref[...]).astype(BF16)
        h_ref[...] = h
        chip_id = order_ref[ps]
        p_ref[:, 0:cb] = _dot(h, w_ref[2 * chip_id], _NN).astype(BF16)
        p_ref[:, cb:2 * cb] = _dot(h, w_ref[2 * chip_id + 1], _NN).astype(BF16)

        @pl.when(jnp.logical_and(ps == 3, i == steps - 1))
        def _():
            for cp in first + passed:
                cp.wait_send()
            keep = pltpu.make_async_copy(w_ref, wout_ref, own_sem)
            keep.start()
            keep.wait()

    gs = pltpu.PrefetchScalarGridSpec(
        num_scalar_prefetch=1, grid=(4, steps),
        in_specs=[pl.BlockSpec((tm, d), lambda ps, i, o: (i, 0)), pl.BlockSpec((1, d), lambda ps, i, o: (0, 0)),
                  pl.BlockSpec(memory_space=pl.ANY)],
        out_specs=[pl.BlockSpec((tm, d), lambda ps, i, o: (jnp.where(ps == 0, i, steps - 1), 0)),
                   pl.BlockSpec((tm, 2 * cb), lambda ps, i, o: (i, o[ps])), pl.BlockSpec(memory_space=pl.ANY)],
        scratch_shapes=[pltpu.VMEM((N_DEV, d, cb), BF16), pltpu.SemaphoreType.DMA((7,)), pltpu.SemaphoreType.DMA((7,)),
                        pltpu.SemaphoreType.DMA(())])
    return pl.pallas_call(
        body, name=name, grid_spec=gs,
        out_shape=[jax.ShapeDtypeStruct((t, d), BF16), jax.ShapeDtypeStruct((t, N_DEV * cb), BF16),
                   jax.ShapeDtypeStruct((N_DEV, d, cb), BF16)],
        compiler_params=_cparams(("arbitrary", "arbitrary")))(order, xf, gain, w_shard)


def _mix_out(p, y_a, y_b, b_gate, xf, w_mix, gain, w_q, *, name):
    t, d = xf.shape

    def body(ins, res, outs, accs):
        ga_ref, gb_ref, ya_ref, yb_ref, x_ref = ins
        bg_ref, wm_ref, g_ref, wq_ref = res
        m_ref, x1_ref, h_ref, q_ref = outs
        sa = _sigmoid(ga_ref[...].astype(F32) + bg_ref[0:1, :])
        sb = _sigmoid(gb_ref[...].astype(F32) + bg_ref[1:2, :])
        merged = (sa * ya_ref[...].astype(F32) + sb * yb_ref[...].astype(F32)).astype(BF16)
        m_ref[...] = merged
        x1 = x_ref[...] + _dot(merged, wm_ref[...], _NN)
        x1_ref[...] = x1
        h = _rms_apply(x1, g_ref[...]).astype(BF16)
        h_ref[...] = h
        q_ref[...] = _dot(h, wq_ref[...], _NN).astype(BF16)

    return _row_call(name, t, min(512, t), [(p, d, 4), (p, d, 5), (y_a, d, 0), (y_b, d, 0), (xf, d, 0)],
                     [b_gate, w_mix, gain, w_q], [(d, d, 0, BF16), (d, d, 0, F32), (d, d, 0, BF16), (d, d, 0, BF16)], [], body)


def _ffn_fwd(h3, x2, target, w_gu_t, w_down, gain, *, name):
    t, d = x2.shape
    f2 = w_gu_t.shape[0]
    f = f2 // 2
    half = f // 2

    def body(ins, res, outs, accs):
        h_ref, x2_ref, t_ref = ins
        wgu_ref, wd_ref, g_ref = res
        gu_ref, act_ref, dx_ref = outs
        loss_ref, dg_ref = accs
        h = h_ref[...]
        x3 = x2_ref[...]
        for c0 in (0, half):
            gt = _dot(h, wgu_ref[c0:c0 + half, :], _NT).astype(BF16)
            up = _dot(h, wgu_ref[f + c0:f + c0 + half, :], _NT).astype(BF16)
            gu_ref[:, c0:c0 + half] = gt
            gu_ref[:, f + c0:f + c0 + half] = up
            gtf = gt.astype(F32)
            act = (gtf * _sigmoid(gtf) * up.astype(F32)).astype(BF16)
            act_ref[:, c0:c0 + half] = act
            x3 = x3 + _dot(act, wd_ref[c0:c0 + half, :], _NN)
        g = g_ref[...]
        r = lax.rsqrt(jnp.mean(x3 * x3, axis=-1, keepdims=True) + RMS_EPS)
        xhat = x3 * r
        err = xhat * g - t_ref[...]
        loss_ref[...] += 0.5 * jnp.sum(jnp.mean(err * err, axis=-1, keepdims=True), axis=0, keepdims=True)
        dy = err * (1.0 / d)
        dg_ref[...] += jnp.sum(dy * xhat, axis=0, keepdims=True)
        dxh = dy * g
        dx_ref[...] = r * (dxh - xhat * jnp.mean(dxh * xhat, axis=-1, keepdims=True))

    return _row_call(name, t, min(256, t), [(h3, d, 0), (x2, d, 0), (target, d, 0)], [w_gu_t, w_down, gain],
                     [(f2, f2, 0, BF16), (f, f, 0, BF16), (d, d, 0, F32)], [((1, 1), F32), ((1, d), F32)], body)


def _ffn_bwd(dx3, gu, x2, w_down, w_gu_t, gain, w_xo, *, name):
    t, d = x2.shape
    f2 = w_gu_t.shape[0]
    f = f2 // 2
    half = f // 2

    def body(ins, res, outs, accs):
        dx3_ref, gu_ref, x2_ref = ins
        wd_ref, wgu_ref, g_ref, wxo_ref = res
        dgu_ref, dx2_ref, do_ref = outs
        (dg_ref,) = accs
        dx3v = dx3_ref[...]
        dxb = dx3v.astype(BF16)
        dh = jnp.zeros(dx3v.shape, F32)
        for c0 in (0, half):
            dact = _dot(dxb, wd_ref[c0:c0 + half, :], _NT)
            gt = gu_ref[:, c0:c0 + half].astype(F32)
            up = gu_ref[:, f + c0:f + c0 + half].astype(F32)
            sg = _sigmoid(gt)
            dgt = (dact * up * sg * (1.0 + gt * (1.0 - sg))).astype(BF16)
            dup = (dact * gt * sg).astype(BF16)
            dgu_ref[:, c0:c0 + half] = dgt
            dgu_ref[:, f + c0:f + c0 + half] = dup
            dh = dh + _dot(dgt, wgu_ref[c0:c0 + half, :], _NN) + _dot(dup, wgu_ref[f + c0:f + c0 + half, :], _NN)
        dx2, dg = _rms_grad(dx3v, dh, x2_ref[...], g_ref[...])
        dx2_ref[...] = dx2
        dg_ref[...] += dg
        do_ref[...] = _dot(dx2.astype(BF16), wxo_ref[...], _NT).astype(BF16)

    return _row_call(name, t, min(256, t), [(dx3, d, 0), (gu, f2, 0), (x2, d, 0)], [w_down, w_gu_t, gain, w_xo],
                     [(f2, f2, 0, BF16), (d, d, 0, F32), (d, d, 0, BF16)], [((1, d), F32)], body)


def _proj_rms_bwd(dy, dres, x, w, gain, *, name, h=None):
    t, d = x.shape
    k = dy.shape[1]

    def body(ins, res, outs, accs):
        dy_ref, dres_ref, x_ref = ins[:3]
        w_ref, g_ref = res
        if h is not None:
            accs[1][...] += _dot(ins[3][...], dy_ref[...], _TN)
        if w.ndim == 3:
            cb = w.shape[2]
            dh = _dot(dy_ref[:, 0:cb], w_ref[0], _NT)
            for j in range(1, w.shape[0]):
                dh = dh + _dot(dy_ref[:, j * cb:(j + 1) * cb], w_ref[j], _NT)
        else:
            dh = _dot(dy_ref[...], w_ref[...], _NT)
        dx, dg = _rms_grad(dres_ref[...], dh, x_ref[...], g_ref[...])
        outs[0][...] = dx
        accs[0][...] += dg

    rows_in = [(dy, k, 0), (dres, d, 0), (x, d, 0)] + ([(h, d, 0)] if h is not None else [])
    accs = [((1, d), F32)] + ([((d, k), BF16)] if h is not None else [])
    return _row_call(name, t, min(512, t), rows_in, [w, gain], [(d, d, 0, F32)], accs, body)


def _gates_bwd_fused(dx1, p, y_a, y_b, b_gate, w_mix, merged, h1, *, name):
    t, d = y_a.shape

    def body(ins, res, outs, accs):
        dx_ref, ga_ref, gb_ref, ya_ref, yb_ref, m_ref, h1_ref = ins
        bg_ref, wm_ref = res
        dp_ref, dya_ref, dyb_ref = outs
        dbg_ref, dwm_ref, dwin_ref = accs
        dxb = dx_ref[...].astype(BF16)
        dwm_ref[...] += _dot(m_ref[...], dxb, _TN)
        dm = _dot(dxb, wm_ref[...], _NT)
        sa = _sigmoid(ga_ref[...].astype(F32) + bg_ref[0:1, :])
        sb = _sigmoid(gb_ref[...].astype(F32) + bg_ref[1:2, :])
        dya_ref[...] = (dm * sa).astype(BF16)
        dyb_ref[...] = (dm * sb).astype(BF16)
        dga = dm * ya_ref[...].astype(F32) * sa * (1.0 - sa)
        dgb = dm * yb_ref[...].astype(F32) * sb * (1.0 - sb)
        dp_ref[:, 0:d] = dga.astype(BF16)
        dp_ref[:, d:2 * d] = dgb.astype(BF16)
        dbg_ref[0:1, :] += jnp.sum(dga, axis=0, keepdims=True)
        dbg_ref[1:2, :] += jnp.sum(dgb, axis=0, keepdims=True)
        dwin_ref[...] += _dot(h1_ref[...], dp_ref[...], _TN)

    return _row_call(name, t, min(256, t),
                     [(dx1, d, 0), (p, d, 4), (p, d, 5), (y_a, d, 0), (y_b, d, 0), (merged, d, 0), (h1, d, 0)],
                     [b_gate, w_mix], [(p.shape[1], 2 * d, 2, BF16), (d, d, 0, BF16), (d, d, 0, BF16)],
                     [((8, d), F32), ((d, d), BF16), ((d, 2 * d), BF16)], body)


def _conv_ln_bwd_fused(dy_a, c, a_act, w_conv_out, ln_g, ln_b, *, name):
    t, d = c.shape

    def body(ins, res, outs, accs):
        dy_ref, c_ref, act_ref = ins
        w_ref, lg_ref, lb_ref = res
        dlg_ref, dlb_ref, dw_ref = accs
        dw_ref[...] += _dot(act_ref[...], dy_ref[...], _TN)
        dact = _dot(dy_ref[...], w_ref[...], _NT)
        cv = c_ref[...].astype(F32)
        g = lg_ref[...]
        mu = jnp.mean(cv, axis=-1, keepdims=True)
        dv = cv - mu
        rstd = lax.rsqrt(jnp.mean(dv * dv, axis=-1, keepdims=True) + LN_EPS)
        chat = dv * rstd
        aln = chat * g + lb_ref[...]
        sg = _sigmoid(aln)
        daln = dact * (sg * (1.0 + aln * (1.0 - sg)))
        dlb_ref[...] += jnp.sum(daln, axis=0, keepdims=True)
        dlg_ref[...] += jnp.sum(daln * chat, axis=0, keepdims=True)
        dchat = daln * g
        dc = rstd * (dchat - jnp.mean(dchat, axis=-1, keepdims=True)
                     - chat * jnp.mean(dchat * chat, axis=-1, keepdims=True))
        outs[0][...] = dc.astype(BF16)

    return _row_call(name, t, min(512, t), [(dy_a, d, 0), (c, d, 0), (a_act, d, 0)], [w_conv_out, ln_g, ln_b],
                     [(d, d, 0, BF16)], [((1, d), F32), ((1, d), F32), ((d, d), BF16)], body)


def _row_spec(tt, cols, col_block=0):
    return pl.BlockSpec((tt, cols), lambda i: (i, col_block))


def _const_spec(shape):
    return pl.BlockSpec(shape, lambda *_: (0,) * len(shape))


def _rms_fwd(x, gain, *, name):
    t, d = x.shape
    tt = min(TOKEN_TILE, t)

    def body(x_ref, g_ref, h_ref):
        xv = x_ref[...]
        r = lax.rsqrt(jnp.mean(xv * xv, axis=-1, keepdims=True) + RMS_EPS)
        h_ref[...] = (xv * r * g_ref[...]).astype(BF16)

    return pl.pallas_call(
        body, name=name, grid=(t // tt,), in_specs=[_row_spec(tt, d), _const_spec((1, d))],
        out_specs=_row_spec(tt, d), out_shape=jax.ShapeDtypeStruct((t, d), BF16),
        compiler_params=_cparams(("parallel",)))(x, gain)


def _rms_bwd(dres, dh, x, gain, *, name, need_dx=True):
    t, d = x.shape
    tt = min(TOKEN_TILE, t)

    def body(*refs):
        if need_dx:
            dres_ref, dh_ref, x_ref, g_ref, dx_ref, dg_ref = refs
        else:
            dh_ref, x_ref, g_ref, dg_ref = refs

        @pl.when(pl.program_id(0) == 0)
        def _():
            dg_ref[...] = jnp.zeros_like(dg_ref)

        xv = x_ref[...]
        dhv = dh_ref[...].astype(F32)
        r = lax.rsqrt(jnp.mean(xv * xv, axis=-1, keepdims=True) + RMS_EPS)
        xhat = xv * r
        dg_ref[...] += jnp.sum(dhv * xhat, axis=0, keepdims=True)
        if need_dx:
            dxh = dhv * g_ref[...]
            dx_ref[...] = dres_ref[...] + r * (dxh - xhat * jnp.mean(dxh * xhat, axis=-1, keepdims=True))

    rs = _row_spec(tt, d)
    if need_dx:
        in_specs, args = [rs, rs, rs, _const_spec((1, d))], (dres, dh, x, gain)
        out_specs = [rs, _const_spec((1, d))]
        out_shape = [jax.ShapeDtypeStruct((t, d), F32), jax.ShapeDtypeStruct((1, d), F32)]
    else:
        in_specs, args = [rs, rs, _const_spec((1, d))], (dh, x, gain)
        out_specs = [_const_spec((1, d))]
        out_shape = [jax.ShapeDtypeStruct((1, d), F32)]
    res = pl.pallas_call(body, name=name, grid=(t // tt,), in_specs=in_specs, out_specs=out_specs, out_shape=out_shape,
                         compiler_params=_cparams(("arbitrary",)))(*args)
    return res if need_dx else res[0]


SUBLANES = 8
SHIFT_ROWS = 40


def _conv_apply(sbuf_ref, w_ref, out_ref, tt, offsets, bias_ref=None):
    d = out_ref.shape[1]
    for cc in range(d // LANES):
        cs = slice(cc * LANES, (cc + 1) * LANES)
        taps = [jnp.broadcast_to(w_ref[k:k + 1, cs], (SUBLANES, LANES)) for k in range(CONV_WIDTH)]
        bias = None if bias_ref is None else jnp.broadcast_to(bias_ref[:, cs], (SUBLANES, LANES))

        def row_body(r, carry, cs=cs, taps=taps, bias=bias):
            r0 = pl.multiple_of(r * CONV_ROWS, CONV_ROWS)
            for q in range(CONV_ROWS // SUBLANES):
                acc = _tap(sbuf_ref, r0 + q * SUBLANES, cs, offsets[0]) * taps[0]
                for k in range(1, CONV_WIDTH):
                    acc = acc + _tap(sbuf_ref, r0 + q * SUBLANES, cs, offsets[k]) * taps[k]
                if bias is not None:
                    acc = acc + bias
                out_ref[pl.ds(r0 + q * SUBLANES, SUBLANES), cs] = acc
            return carry

        lax.fori_loop(0, tt // CONV_ROWS, row_body, 0)


def _fill_shifts(sbuf_ref, rows):
    d = sbuf_ref.shape[2]
    assert rows % SHIFT_ROWS == 0

    def row_body(i, carry):
        r0 = pl.multiple_of(i * SHIFT_ROWS, SUBLANES)
        for cc in range(d // CONV_COLS):
            cs = slice(cc * CONV_COLS, (cc + 1) * CONV_COLS)
            win = sbuf_ref[0, pl.ds(r0, SHIFT_ROWS + SUBLANES), cs]
            for sh in range(1, SUBLANES):
                sbuf_ref[sh, pl.ds(r0, SHIFT_ROWS), cs] = win[sh:sh + SHIFT_ROWS, :]
        return carry

    lax.fori_loop(0, rows // SHIFT_ROWS, row_body, 0)


def _tap(sbuf_ref, r0, cs, offset):
    sh = offset % SUBLANES
    return sbuf_ref[sh, pl.ds(pl.multiple_of(r0 + (offset - sh), SUBLANES), SUBLANES), cs]


def _conv_specs(bl, s, tt, d, col_a, col_g):
    nj = s // tt
    per = tt // CONV_HALO
    main_a = pl.BlockSpec((tt, d), lambda b, j: (b * nj + j, col_a))
    main_g = pl.BlockSpec((tt, d), lambda b, j: (b * nj + j, col_g))
    prev = lambda b, j: jnp.maximum((b * nj + j) * per - 1, 0)
    halo_a = pl.BlockSpec((CONV_HALO, d), lambda b, j: (prev(b, j), col_a))
    halo_g = pl.BlockSpec((CONV_HALO, d), lambda b, j: (prev(b, j), col_g))
    return main_a, main_g, halo_a, halo_g


def _fill_glu(sbuf_ref, a_ref, g_ref, ha_ref, hg_ref, tt):
    first = pl.program_id(1) == 0
    ha = ha_ref[...].astype(F32)
    hg = hg_ref[...].astype(F32)
    sbuf_ref[0, pl.ds(0, CONV_HALO), :] = jnp.where(first, 0.0, ha * _sigmoid(hg))
    av = a_ref[...].astype(F32)
    gv = g_ref[...].astype(F32)
    sbuf_ref[0, pl.ds(CONV_HALO, tt), :] = av * _sigmoid(gv)
    _fill_shifts(sbuf_ref, tt + CONV_HALO - SUBLANES)


def _conv_fwd(p, conv_w, conv_b, ln_g, ln_b, *, bl, s, name):
    t = p.shape[0]
    d = conv_w.shape[1]
    tt = min(TOKEN_TILE, s)
    off = CONV_HALO - (CONV_WIDTH - 1)

    def body(a_ref, g_ref, ha_ref, hg_ref, w_ref, b_ref, lg_ref, lb_ref, c_ref, act_ref, sbuf_ref, cbuf_ref):
        _fill_glu(sbuf_ref, a_ref, g_ref, ha_ref, hg_ref, tt)

        _conv_apply(sbuf_ref, w_ref, cbuf_ref, tt, [off + k for k in range(CONV_WIDTH)], bias_ref=b_ref)
        cv = cbuf_ref[...]
        c_ref[...] = cv.astype(BF16)
        mu = jnp.mean(cv, axis=-1, keepdims=True)
        dv = cv - mu
        rstd = lax.rsqrt(jnp.mean(dv * dv, axis=-1, keepdims=True) + LN_EPS)
        aln = dv * rstd * lg_ref[...] + lb_ref[...]
        act_ref[...] = (aln * _sigmoid(aln)).astype(BF16)

    main_a, main_g, halo_a, halo_g = _conv_specs(bl, s, tt, d, 0, 1)
    out_spec = pl.BlockSpec((tt, d), lambda b, j: (b * (s // tt) + j, 0))
    return pl.pallas_call(
        body, name=name, grid=(bl, s // tt),
        in_specs=[main_a, main_g, halo_a, halo_g, _const_spec((CONV_HALO, d)), _const_spec((1, d)), _const_spec((1, d)),
                  _const_spec((1, d))],
        out_specs=[out_spec, out_spec],
        out_shape=[jax.ShapeDtypeStruct((t, d), BF16), jax.ShapeDtypeStruct((t, d), BF16)],
        scratch_shapes=[pltpu.VMEM((SUBLANES, tt + CONV_HALO, d), F32), pltpu.VMEM((tt, d), F32)],
        compiler_params=_cparams(("parallel", "parallel")))(p, p, p, p, conv_w, conv_b, ln_g, ln_b)


def _conv_bwd(dp, dc, p, conv_w, h1, *, bl, s, name):
    t = p.shape[0]
    d = conv_w.shape[1]
    tt = min(TOKEN_TILE, s)
    nj = s // tt
    per = tt // CONV_HALO
    off = CONV_HALO - (CONV_WIDTH - 1)
    last_blk = t // CONV_HALO - 1

    def body(dp_in, dc_ref, dcn_ref, a_ref, g_ref, ha_ref, hg_ref, w_ref, h1_ref, dp_ref, dw_ref, db_ref, dwin_ref,
             gbuf_ref, dbuf_ref, dglu_ref, acc_ref):
        del dp_in
        b, j = pl.program_id(0), pl.program_id(1)
        start = jnp.logical_and(b == 0, j == 0)
        end = jnp.logical_and(b == bl - 1, j == nj - 1)

        @pl.when(start)
        def _():
            acc_ref[...] = jnp.zeros_like(acc_ref)
            db_ref[...] = jnp.zeros_like(db_ref)
            dwin_ref[...] = jnp.zeros_like(dwin_ref)

        _fill_glu(gbuf_ref, a_ref, g_ref, ha_ref, hg_ref, tt)
        dcv = dc_ref[...].astype(F32)
        dbuf_ref[0, pl.ds(0, tt), :] = dcv
        dbuf_ref[0, pl.ds(tt, CONV_HALO), :] = jnp.where(j == nj - 1, 0.0, dcn_ref[...].astype(F32))
        _fill_shifts(dbuf_ref, tt + CONV_HALO - SUBLANES)
        db_ref[...] += jnp.sum(dcv, axis=0, keepdims=True)

        for cc in range(d // LANES):
            cs = slice(cc * LANES, (cc + 1) * LANES)

            def row_body(r, accs, cs=cs):
                r0 = pl.multiple_of(r * CONV_ROWS, CONV_ROWS)
                accs = list(accs)
                for q in range(CONV_ROWS // SUBLANES):
                    dcw = dbuf_ref[0, pl.ds(r0 + q * SUBLANES, SUBLANES), cs]
                    for k in range(CONV_WIDTH):
                        accs[k] = accs[k] + dcw * _tap(gbuf_ref, r0 + q * SUBLANES, cs, off + k)
                return tuple(accs)

            zero = jnp.zeros((SUBLANES, LANES), F32)
            accs = lax.fori_loop(0, tt // CONV_ROWS, row_body, (zero,) * CONV_WIDTH)
            for k in range(CONV_WIDTH):
                acc_ref[k, :, cs] += accs[k]

        _conv_apply(dbuf_ref, w_ref, dglu_ref, tt, [CONV_WIDTH - 1 - k for k in range(CONV_WIDTH)])
        dglu = dglu_ref[...]
        av = a_ref[...].astype(F32)
        sg = _sigmoid(g_ref[...].astype(F32))
        dp_ref[:, 0:d] = (dglu * sg).astype(BF16)
        dp_ref[:, d:2 * d] = (dglu * av * sg * (1.0 - sg)).astype(BF16)
        dwin_ref[...] += _dot(h1_ref[...], dp_ref[...], _TN)

        @pl.when(end)
        def _():
            for k in range(CONV_WIDTH):
                dw_ref[k:k + 1, :] = jnp.sum(acc_ref[k], axis=0, keepdims=True)
            dw_ref[CONV_WIDTH:CONV_HALO, :] = jnp.zeros((CONV_HALO - CONV_WIDTH, d), F32)

    main_a, main_g, halo_a, halo_g = _conv_specs(bl, s, tt, d, 0, 1)
    dc_main = pl.BlockSpec((tt, d), lambda b, j: (b * nj + j, 0))
    dc_next = pl.BlockSpec((CONV_HALO, d), lambda b, j: (jnp.minimum((b * nj + j + 1) * per, last_blk), 0))
    return pl.pallas_call(
        body, name=name, grid=(bl, nj),
        in_specs=[pl.BlockSpec(memory_space=pl.ANY), dc_main, dc_next, main_a, main_g, halo_a, halo_g,
                  _const_spec((CONV_HALO, d)), dc_main],
        out_specs=[pl.BlockSpec((tt, 2 * d), lambda b, j: (b * nj + j, 0)), _const_spec((CONV_HALO, d)), _const_spec((1, d)),
                   _const_spec((d, 2 * d))],
        out_shape=[jax.ShapeDtypeStruct(dp.shape, dp.dtype), jax.ShapeDtypeStruct((CONV_HALO, d), F32),
                   jax.ShapeDtypeStruct((1, d), F32), jax.ShapeDtypeStruct((d, 2 * d), F32)],
        scratch_shapes=[pltpu.VMEM((SUBLANES, tt + CONV_HALO, d), F32), pltpu.VMEM((SUBLANES, tt + CONV_HALO, d), F32),
                        pltpu.VMEM((tt, d), F32), pltpu.VMEM((CONV_HALO, SUBLANES, d), F32)],
        input_output_aliases={0: 0},
        compiler_params=_cparams(("arbitrary", "arbitrary")))(dp, dc, dc, p, p, p, p, conv_w, h1)


def _sgu_stats(bv):
    gv = _gelu(bv)
    mu = jnp.mean(gv, axis=-1, keepdims=True)
    dv = gv - mu
    rstd = lax.rsqrt(jnp.mean(dv * dv, axis=-1, keepdims=True) + LN_EPS)
    return dv * rstd, rstd


def _sgu_fwd(p, wm, bias, ln_g, ln_b, *, name):
    t = p.shape[0]
    d = ln_g.shape[1]
    tt = SGU_TILE
    gd = d // SGU_GROUPS

    def body(u_ref, v_ref, wm_ref, bias_ref, lg_ref, lb_ref, sg_ref, vn_ref):
        u = _gelu(u_ref[...].astype(F32))
        vhat, _ = _sgu_stats(v_ref[...].astype(F32))
        vb = (vhat * lg_ref[...] + lb_ref[...]).astype(BF16)
        vn_ref[...] = vb
        for ci in range(tt // SGU_CHUNK):
            rows = slice(ci * SGU_CHUNK, (ci + 1) * SGU_CHUNK)
            for g in range(SGU_GROUPS):
                gs = slice(g * gd, (g + 1) * gd)
                z = _dot(wm_ref[g], vb[rows, gs], _NN) + bias_ref[g]
                sg_ref[rows, gs] = (u[rows, gs] * z).astype(BF16)

    rs = _row_spec(tt, d)
    return pl.pallas_call(
        body, name=name, grid=(t // tt,),
        in_specs=[_row_spec(tt, d, 2), _row_spec(tt, d, 3), _const_spec(wm.shape), _const_spec(bias.shape),
                  _const_spec((1, d)), _const_spec((1, d))],
        out_specs=[rs, rs], out_shape=[jax.ShapeDtypeStruct((t, d), BF16), jax.ShapeDtypeStruct((t, d), BF16)],
        compiler_params=_cparams(("parallel",)))(p, p, wm, bias, ln_g, ln_b)


def _sgu_bwd(dp, dy_b, w_out, p, vn, wm, wmt, bias, ln_g, *, name):
    t = p.shape[0]
    d = ln_g.shape[1]
    tt = SGU_TILE
    ck = SGU_CHUNK
    gd = d // SGU_GROUPS
    nsteps = t // tt

    def body(dp_in, dyb_ref, wout_ref, u_ref, v_ref, vn_ref, wm_ref, wmt_ref, bias_ref, lg_ref,
             dp_ref, dw_ref, dbs_ref, dlg_ref, dlb_ref, dz_acc):
        del dp_in
        i = pl.program_id(0)

        @pl.when(i == 0)
        def _():
            dw_ref[...] = jnp.zeros_like(dw_ref)
            dlg_ref[...] = jnp.zeros_like(dlg_ref)
            dlb_ref[...] = jnp.zeros_like(dlb_ref)
            dz_acc[...] = jnp.zeros_like(dz_acc)

        bu = u_ref[...].astype(F32)
        bv = v_ref[...].astype(F32)
        u = _gelu(bu)
        vhat, rstd = _sgu_stats(bv)
        vb = vn_ref[...]
        dsg = _dot(dyb_ref[...], wout_ref[...], _NT)
        row = lax.broadcasted_iota(jnp.int32, (ck, ck), 0)
        col = lax.broadcasted_iota(jnp.int32, (ck, ck), 1)
        causal = col <= row
        du_rows, dv_rows = [], []
        for ci in range(tt // ck):
            rows = slice(ci * ck, (ci + 1) * ck)
            du_parts, dv_parts = [], []
            for g in range(SGU_GROUPS):
                gs = slice(g * gd, (g + 1) * gd)
                z = _dot(wm_ref[g], vb[rows, gs], _NN) + bias_ref[g]
                du_parts.append(dsg[rows, gs] * z)
                dz = dsg[rows, gs] * u[rows, gs]
                dz_acc[:, gs] += dz
                dzb = dz.astype(BF16)
                dw_ref[g] += jnp.where(causal, _dot(dzb, vb[rows, gs], _NT), 0.0)
                dv_parts.append(_dot(wmt_ref[g], dzb, _NN))
            du_rows.append(jnp.concatenate(du_parts, axis=1))
            dv_rows.append(jnp.concatenate(dv_parts, axis=1))
        du = jnp.concatenate(du_rows, axis=0)
        dv = jnp.concatenate(dv_rows, axis=0)
        dp_ref[:, 0:d] = (du * _gelu_grad(bu)).astype(BF16)
        dlb_ref[...] += jnp.sum(dv, axis=0, keepdims=True)
        dlg_ref[...] += jnp.sum(dv * vhat, axis=0, keepdims=True)
        dvh = dv * lg_ref[...]
        dgv = rstd * (dvh - jnp.mean(dvh, axis=-1, keepdims=True) - vhat * jnp.mean(dvh * vhat, axis=-1, keepdims=True))
        dp_ref[:, d:2 * d] = (dgv * _gelu_grad(bv)).astype(BF16)

        @pl.when(i == nsteps - 1)
        def _():
            ones = jnp.ones((8, gd), F32)
            for g in range(SGU_GROUPS):
                gs = slice(g * gd, (g + 1) * gd)
                tot = lax.dot_general(ones, dz_acc[:, gs], (_NT, ((), ())), preferred_element_type=F32,
                                      precision=lax.Precision.HIGHEST)
                dbs_ref[g:g + 1, :] = tot[0:1, :]

    rs = _row_spec(tt, d)
    c1 = _const_spec((1, d))
    return pl.pallas_call(
        body, name=name, grid=(nsteps,),
        in_specs=[pl.BlockSpec(memory_space=pl.ANY), rs, _const_spec(w_out.shape), _row_spec(tt, d, 2), _row_spec(tt, d, 3),
                  rs, _const_spec(wm.shape), _const_spec(wmt.shape), _const_spec(bias.shape), c1],
        out_specs=[pl.BlockSpec((tt, 2 * d), lambda i: (i, 1)), _const_spec(wm.shape), _const_spec((SGU_GROUPS, ck)), c1, c1],
        out_shape=[jax.ShapeDtypeStruct(dp.shape, dp.dtype), jax.ShapeDtypeStruct(wm.shape, F32),
                   jax.ShapeDtypeStruct((SGU_GROUPS, ck), F32), jax.ShapeDtypeStruct((1, d), F32),
                   jax.ShapeDtypeStruct((1, d), F32)],
        scratch_shapes=[pltpu.VMEM((ck, d), F32)],
        input_output_aliases={0: 0},
        compiler_params=_cparams(("arbitrary",)))(dp, dy_b, w_out, p, p, vn, wm, wmt, bias, ln_g)


def _softmax_rows(s):
    e = jnp.exp(s - jnp.max(s, axis=-1, keepdims=True))
    return e / jnp.sum(e, axis=-1, keepdims=True)


def _attn_fwd(q, kv, x1, w_xo, gain, *, bl, s, name):
    t, d = q.shape
    mlen = kv.shape[0] // bl
    hd = d // HEADS
    tq = min(ATTN_TILE, s)
    nq = s // tq
    scale = hd ** -0.5

    def body(q_ref, kv_ref, x1_ref, w_ref, g_ref, o_ref, x2_ref, h_ref):
        for h in range(HEADS):
            hs = slice(h * hd, (h + 1) * hd)
            vs = slice(d + h * hd, d + (h + 1) * hd)
            pr = _softmax_rows(_dot(q_ref[:, hs], kv_ref[:, hs], _NT) * scale)
            o_ref[:, hs] = _dot(pr.astype(BF16), kv_ref[:, vs], _NN).astype(BF16)
        x2 = x1_ref[...] + _dot(o_ref[...], w_ref[...], _NN)
        x2_ref[...] = x2
        h_ref[...] = _rms_apply(x2, g_ref[...]).astype(BF16)

    qs = pl.BlockSpec((tq, d), lambda b, j: (b * nq + j, 0))
    return pl.pallas_call(
        body, name=name, grid=(bl, nq),
        in_specs=[qs, pl.BlockSpec((mlen, 2 * d), lambda b, j: (b, 0)), qs, _const_spec(w_xo.shape), _const_spec((1, d))],
        out_specs=[qs, qs, qs],
        out_shape=[jax.ShapeDtypeStruct((t, d), BF16), jax.ShapeDtypeStruct((t, d), F32), jax.ShapeDtypeStruct((t, d), BF16)],
        compiler_params=_cparams(("parallel", "parallel")))(q, kv, x1, w_xo, gain)


def _attn_bwd(q, kv, do, *, bl, s, name):
    t, d = q.shape
    mlen = kv.shape[0] // bl
    hd = d // HEADS
    tq = min(ATTN_TILE, s)
    nq = s // tq
    scale = hd ** -0.5

    def body(q_ref, kv_ref, do_ref, dq_ref, dkv_ref):
        @pl.when(pl.program_id(1) == 0)
        def _():
            dkv_ref[...] = jnp.zeros_like(dkv_ref)

        for h in range(HEADS):
            hs = slice(h * hd, (h + 1) * hd)
            vs = slice(d + h * hd, d + (h + 1) * hd)
            qh, kh, vh, doh = q_ref[:, hs], kv_ref[:, hs], kv_ref[:, vs], do_ref[:, hs]
            pr = _softmax_rows(_dot(qh, kh, _NT) * scale)
            dpr = _dot(doh, vh, _NT)
            dkv_ref[:, vs] += _dot(pr.astype(BF16), doh, _TN)
            ds = (pr * (dpr - jnp.sum(dpr * pr, axis=-1, keepdims=True)) * scale).astype(BF16)
            dq_ref[:, hs] = _dot(ds, kh, _NN).astype(BF16)
            dkv_ref[:, hs] += _dot(ds, qh, _TN)

    qs = pl.BlockSpec((tq, d), lambda b, j: (b * nq + j, 0))
    ks = pl.BlockSpec((mlen, 2 * d), lambda b, j: (b, 0))
    return pl.pallas_call(
        body, name=name, grid=(bl, nq), in_specs=[qs, ks, qs], out_specs=[qs, ks],
        out_shape=[jax.ShapeDtypeStruct((t, d), BF16), jax.ShapeDtypeStruct(kv.shape, F32)],
        compiler_params=_cparams(("parallel", "arbitrary")))(q, kv, do)


def _mesh_pos():
    return lax.axis_index("x"), lax.axis_index("y"), lax.axis_index("c")


def _all_gather(arrs, *, name):
    n = len(arrs)
    hbm = pl.BlockSpec(memory_space=pl.ANY)

    def body(*refs):
        ins, outs = refs[:n], refs[n:2 * n]
        send_sems, recv_sems, loc_sems = refs[2 * n:]
        x, y, c = _mesh_pos()
        me, sib = (x, y, c), (x, y, 1 - c)
        chips = [(1 - x, y), (x, 1 - y), (1 - x, 1 - y)]

        def idx(dev):
            return 4 * dev[0] + 2 * dev[1] + dev[2]

        def copy(w, k, block, to, from_input=False):
            return pltpu.make_async_remote_copy(
                src_ref=ins[w] if from_input else outs[w].at[idx(block)], dst_ref=outs[w].at[idx(block)],
                send_sem=send_sems.at[w, k], recv_sem=recv_sems.at[w, k], device_id=to, device_id_type=MESH_ID)

        own = [pltpu.make_async_copy(ins[w], outs[w].at[idx(me)], loc_sems.at[w]) for w in range(n)]
        for cp in own:
            cp.start()
        first = []
        for w in range(n):
            first.append(copy(w, 0, me, sib, True))
            first += [copy(w, 1 + j, me, (*chip, c), True) for j, chip in enumerate(chips)]
        for cp in first:
            cp.start()
        passed = []
        for j, chip in enumerate(chips):
            for w in range(n):
                copy(w, 1 + j, (*chip, c), me).wait_recv()
                fwd = copy(w, 4 + j, (*chip, c), sib)
                fwd.start()
                passed.append(fwd)
        for w in range(n):
            copy(w, 0, sib, me).wait_recv()
            for j, chip in enumerate(chips):
                copy(w, 4 + j, (*chip, 1 - c), me).wait_recv()
        for cp in first + passed:
            cp.wait_send()
        for cp in own:
            cp.wait()

    return pl.pallas_call(
        body, name=name, in_specs=[hbm] * n, out_specs=[hbm] * n,
        out_shape=[jax.ShapeDtypeStruct((N_DEV, *a.shape), a.dtype) for a in arrs],
        scratch_shapes=[pltpu.SemaphoreType.DMA((n, 7)), pltpu.SemaphoreType.DMA((n, 7)), pltpu.SemaphoreType.DMA((n,))],
    )(*arrs)


_HBM = pl.BlockSpec(memory_space=pltpu.HBM)
_SEM = pl.BlockSpec(memory_space=pltpu.SEMAPHORE)
_ANY = pl.BlockSpec(memory_space=pl.ANY)
_EFFECT = pltpu.SideEffectType.DATAFLOW_SIDE_EFFECTING
N_PEERS = N_DEV - 1


def _related(pos, r):
    x, y, c = pos
    return (1 - x if r & 4 else x, 1 - y if r & 2 else y, 1 - c if r & 1 else c)


def _dev_index(dev):
    return 4 * dev[0] + 2 * dev[1] + dev[2]


def _in_hbm(a):
    return pltpu.with_memory_space_constraint(a, pltpu.HBM)


def _split_copies(kind, srcs, lands, send_sems, recv_sems):
    pos = _mesh_pos()
    me = _dev_index(pos)
    out = []
    for w in range(len(srcs)):
        for r in range(1, N_DEV):
            peer = _related(pos, r)
            if kind == "gather":
                src, dst_here, dst_there = srcs[w], lands[w].at[_dev_index(peer)], lands[w].at[me]
            elif srcs[w].ndim == 2:
                cb = lands[w].shape[2]
                src = srcs[w].at[:, pl.ds(pl.multiple_of(_dev_index(peer) * cb, LANES), cb)]
                dst_here = dst_there = lands[w].at[r - 1]
            else:
                src, dst_here, dst_there = srcs[w].at[_dev_index(peer)], lands[w].at[r - 1], lands[w].at[r - 1]
            out.append((src, dst_here, dst_there, send_sems.at[w * N_PEERS + r - 1], recv_sems.at[w * N_PEERS + r - 1], peer))
    return out


def _copy_start(kind, srcs, land_shapes, *, name, after=None):
    n = len(srcs)
    n_after = 0 if after is None else 1

    def body(*refs):
        src_refs, land_refs = refs[:n], refs[n:2 * n]
        send_sems, recv_sems = refs[2 * n + n_after], refs[2 * n + n_after + 1]
        token = refs[-1]
        for src, _, dst, ssem, rsem, peer in _split_copies(kind, src_refs, land_refs, send_sems, recv_sems):
            pltpu.make_async_remote_copy(src_ref=src, dst_ref=dst, send_sem=ssem, recv_sem=rsem, device_id=peer,
                                         device_id_type=MESH_ID).start()
        token[...] = jnp.zeros_like(token)

    lands = [_in_hbm(lax.empty(shape, s.dtype)) for s, shape in zip(srcs, land_shapes)]
    res = pl.pallas_call(
        body, name=name,
        out_shape=(pltpu.SemaphoreType.DMA((n * N_PEERS,)), pltpu.SemaphoreType.DMA((n * N_PEERS,)),
                   *[pltpu.HBM(s.shape, s.dtype) for s in srcs], *[pltpu.HBM(l.shape, l.dtype) for l in lands],
                   jax.ShapeDtypeStruct((8, 128), F32)),
        in_specs=[_HBM] * (2 * n) + [_ANY] * n_after,
        out_specs=(_SEM, _SEM, *[_HBM] * (2 * n), pl.BlockSpec(memory_space=pltpu.VMEM)),
        input_output_aliases={i: 2 + i for i in range(2 * n)},
        compiler_params=pltpu.CompilerParams(has_side_effects=_EFFECT),
    )(*[_in_hbm(s) for s in srcs], *lands, *([] if after is None else [after]))
    return res[0], res[1], list(res[2:2 + n]), list(res[2 + n:2 + 2 * n]), res[-1]


def _copy_wait(kind, send_sems, recv_sems, srcs, lands, after, *, name):
    n = len(srcs)

    def body(*refs):
        src_refs, land_refs = refs[:n], refs[n:2 * n]
        ssems, rsems = refs[2 * n], refs[2 * n + 1]
        for src, dst, _, ssem, rsem, peer in _split_copies(kind, src_refs, land_refs, ssems, rsems):
            cp = pltpu.make_async_remote_copy(src_ref=src, dst_ref=dst, send_sem=ssem, recv_sem=rsem, device_id=peer,
                                              device_id_type=MESH_ID)
            cp.wait_send()
            cp.wait_recv()

    res = pl.pallas_call(
        body, name=name,
        out_shape=(*[pltpu.HBM(s.shape, s.dtype) for s in srcs], *[pltpu.HBM(l.shape, l.dtype) for l in lands]),
        in_specs=[_HBM] * (2 * n) + [_SEM, _SEM, _ANY], out_specs=tuple([_HBM] * (2 * n)),
        input_output_aliases={i: i for i in range(2 * n)},
        compiler_params=pltpu.CompilerParams(has_side_effects=_EFFECT),
    )(*srcs, *lands, send_sems, recv_sems, after)
    return list(res[:n]), list(res[n:])


def _row_tile(rows):
    return max(tr for tr in range(16, min(rows, 512) + 1, 16) if rows % tr == 0)


def _adamw_math(w, g, m, v):
    m2 = ADAM_B1 * m + (1.0 - ADAM_B1) * g
    v2 = ADAM_B2 * v + (1.0 - ADAM_B2) * (g * g)
    m_hat = m2 / (1.0 - ADAM_B1 ** ADAM_STEP)
    v_hat = v2 / (1.0 - ADAM_B2 ** ADAM_STEP)
    delta = -ADAM_LR * (m_hat / (jnp.sqrt(v_hat) + ADAM_EPS) + ADAM_WD * w)
    return delta, m2, v2


def _adamw_shard(partials, landed, dev, w, m, v, *, name):
    r, c = w.shape
    tr = _row_tile(r)

    def body(dev_ref, p_ref, l_ref, w_ref, m_ref, v_ref, g_out, d_out, m_out, v_out):
        del dev_ref
        g = p_ref[...].astype(F32)
        for k in range(N_PEERS):
            g = g + l_ref[k].astype(F32)
        delta, m2, v2 = _adamw_math(w_ref[...], g, m_ref[...], v_ref[...])
        g_out[...] = g
        d_out[...] = delta
        m_out[...] = m2
        v_out[...] = v2

    blk = pl.BlockSpec((tr, c), lambda i, dev_ref: (i, 0))
    if partials.ndim == 2:
        own = pl.BlockSpec((tr, c), lambda i, dev_ref: (i, dev_ref[0]))
    else:
        own = pl.BlockSpec((None, tr, c), lambda i, dev_ref: (dev_ref[0], i, 0))
    gs = pltpu.PrefetchScalarGridSpec(
        num_scalar_prefetch=1, grid=(r // tr,),
        in_specs=[own, pl.BlockSpec((N_PEERS, tr, c), lambda i, dev_ref: (0, i, 0)), blk, blk, blk],
        out_specs=[blk] * 4)
    return pl.pallas_call(
        body, name=name, grid_spec=gs, out_shape=[jax.ShapeDtypeStruct((r, c), F32)] * 4,
        compiler_params=_cparams(("parallel",)))(dev, partials, landed, w, m, v)


def _sum_devices(p_ref, *idx):
    g = p_ref[(0, *idx)]
    for k in range(1, N_DEV):
        g = g + p_ref[(k, *idx)]
    return g


def _adamw_replicated(parts, states, loss_row, *, name):
    n_parts, n_par = len(parts), len(states)
    n_vec = n_par - (n_parts - 1)

    def body(*refs):
        part_refs, st = refs[:n_parts], refs[n_parts:n_parts + 3 * n_par]
        outs = refs[n_parts + 3 * n_par:]
        outs[0][...] = _sum_devices(part_refs[0], slice(loss_row, loss_row + 1), slice(0, 1))
        for i in range(n_par):
            g = _sum_devices(part_refs[0], slice(i, i + 1)) if i < n_vec else _sum_devices(part_refs[1 + i - n_vec])
            delta, m2, v2 = _adamw_math(st[3 * i][...], g, st[3 * i + 1][...], st[3 * i + 2][...])
            for o, val in zip(outs[1 + 4 * i:5 + 4 * i], (g, delta, m2, v2)):
                o[...] = val

    flat = [a for wmv in states for a in wmv]
    return pl.pallas_call(
        body, name=name,
        out_shape=[jax.ShapeDtypeStruct((1, 1), F32)] + [jax.ShapeDtypeStruct(w.shape, F32) for w, _, _ in states for _ in range(4)],
        compiler_params=pltpu.CompilerParams(vmem_limit_bytes=VMEM_LIMIT))(*parts, *flat)


def _adamw_column_shards(parts, dev, states, row0s, *, name):
    _, rows, _ = parts.shape
    c = states[0][0].shape[1]

    def body(dev_ref, p_ref, *refs):
        del dev_ref
        st, outs = refs[:3 * len(states)], refs[3 * len(states):]
        for j, r0 in enumerate(row0s):
            w_ref = st[3 * j]
            g = _sum_devices(p_ref, slice(r0, r0 + w_ref.shape[0]))
            delta, m2, v2 = _adamw_math(w_ref[...], g, st[3 * j + 1][...], st[3 * j + 2][...])
            for o, val in zip(outs[4 * j:4 * j + 4], (g, delta, m2, v2)):
                o[...] = val

    whole = lambda a: pl.BlockSpec(a.shape, lambda i, dev_ref: (0, 0))
    flat = [a for wmv in states for a in wmv]
    outs = [w for w, _, _ in states for _ in range(4)]
    gs = pltpu.PrefetchScalarGridSpec(
        num_scalar_prefetch=1, grid=(1,),
        in_specs=[pl.BlockSpec((N_DEV, rows, c), lambda i, dev_ref: (0, 0, dev_ref[0]))] + [whole(a) for a in flat],
        out_specs=[whole(a) for a in outs])
    return pl.pallas_call(
        body, name=name, grid_spec=gs, out_shape=[jax.ShapeDtypeStruct(a.shape, F32) for a in outs],
        compiler_params=_cparams(("arbitrary",)))(dev, parts, *flat)


def _pad_rows(a, rows):
    return jnp.pad(a, ((0, rows - a.shape[0]), (0, 0)))


def _unblock_cols(g):
    return jnp.transpose(g, (1, 0, 2)).reshape(g.shape[1], N_DEV * g.shape[2])


def kernel(x, mem, norm_mix, w_in, b_gate, conv_w, conv_b, conv_ln_g, conv_ln_b, w_conv_out, sgu_ln_g, sgu_ln_b, sgu_w, sgu_b, w_sgu_out, w_mix_out, norm_xattn, norm_mem, w_q, w_kv, w_xo, norm_ffn, w_gu, w_down, norm_final, loss_target, m_norm_mix, m_w_in, m_b_gate, m_conv_w, m_conv_b, m_conv_ln_g, m_conv_ln_b, m_w_conv_out, m_sgu_ln_g, m_sgu_ln_b, m_sgu_w, m_sgu_b, m_w_sgu_out, m_w_mix_out, m_norm_xattn, m_norm_mem, m_w_q, m_w_kv, m_w_xo, m_norm_ffn, m_w_gu, m_w_down, m_norm_final, v_norm_mix, v_w_in, v_b_gate, v_conv_w, v_conv_b, v_conv_ln_g, v_conv_ln_b, v_w_conv_out, v_sgu_ln_g, v_sgu_ln_b, v_sgu_w, v_sgu_b, v_w_sgu_out, v_w_mix_out, v_norm_xattn, v_norm_mem, v_w_q, v_w_kv, v_w_xo, v_norm_ffn, v_w_gu, v_w_down, v_norm_final):
    given = dict(locals())
    bl, s, d = x.shape
    t = bl * s
    xf = x.reshape(t, d)
    tgt = loss_target.reshape(t, d)
    memf = mem.reshape(bl * mem.shape[1], d)
    cx, cy, cc = lax.axis_index("x"), lax.axis_index("y"), lax.axis_index("c")
    dev = 4 * cx + 2 * cy + cc
    dev_id = dev.astype(jnp.int32).reshape(1)
    col_sharded = ["w_in", "w_kv"]
    transposed = ["w_gu"]

    def shard_of(name, prefix=""):
        a = given[prefix + name][0]
        return jnp.transpose(a) if name in transposed else a

    def full_weight(name, blocks):
        return _unblock_cols(blocks) if name in col_sharded else blocks.reshape(N_DEV * blocks.shape[1], blocks.shape[2])

    g_bg, g_cw = _all_gather([_pad_rows(b_gate[0], 8), _pad_rows(conv_w[0], CONV_HALO)], name="gather_small_params")
    h1, p, w_in_blocks = _in_proj_gather(xf, norm_mix + g_bg[0, 7:8, 0:1], w_in[0].astype(BF16), name="in_proj")
    early = ["w_conv_out", "w_sgu_out", "w_mix_out", "w_q", "w_kv", "w_xo"]
    late = ["w_gu", "w_down"]
    shards = {n: shard_of(n).astype(BF16) for n in early + late}
    started = {}
    for grp, names in (("early", early), ("late", late)):
        srcs = [shards[n] for n in names]
        started[grp] = _copy_start("gather", srcs, [(N_DEV, *a.shape) for a in srcs], name=f"gather_{grp}_start", after=p)
    token = started["early"][4][0:1, 0:1] + started["late"][4][0:1, 0:1]
    wfull = {}
    bg_full = _unblock_cols(g_bg)
    cw_full = _unblock_cols(g_cw)

    def finish_gather(grp, names, after):
        ssem, rsem, srcs, lands, _ = started[grp]
        _, lands = _copy_wait("gather", ssem, rsem, srcs, lands, after, name=f"gather_{grp}_wait")
        for n, land in zip(names, lands):
            wfull[n] = full_weight(n, lax.dynamic_update_index_in_dim(land, shards[n], dev, 0))

    tri = jnp.tril(jnp.ones((SGU_CHUNK, SGU_CHUNK), bool))
    wm32 = jnp.where(tri[None], sgu_w[0], 0.0)
    wm = wm32.astype(BF16)
    wmt = jnp.transpose(wm32, (0, 2, 1)).astype(BF16)
    sgu_bias = jnp.broadcast_to(sgu_b[0][:, :, None], (SGU_GROUPS, SGU_CHUNK, d // SGU_GROUPS))

    c_conv, a_act = _conv_fwd(p, cw_full, conv_b + token, conv_ln_g, conv_ln_b, bl=bl, s=s, name="conv_fwd")
    sg, vn = _sgu_fwd(p, wm, sgu_bias, sgu_ln_g, sgu_ln_b + token, name="sgu_fwd")
    finish_gather("early", early, a_act[0:16, 0:128] + sg[0:16, 0:128])
    y_a = _matmul(a_act, wfull["w_conv_out"], mode="nn", out_dtype=BF16, name="mm_conv_out", tm=1024, tn=1024, tk=1024)
    y_b = _matmul(sg, wfull["w_sgu_out"], mode="nn", out_dtype=BF16, name="mm_sgu_out", tm=1024, tn=1024, tk=1024)
    merged, x1, h2, q = _mix_out(p, y_a, y_b, bg_full, xf, wfull["w_mix_out"], norm_xattn, wfull["w_q"], name="mix_out")
    mem_n = _rms_fwd(memf, norm_mem, name="rms_mem")
    kv = _matmul(mem_n, wfull["w_kv"], mode="nn", out_dtype=BF16, name="mm_kv", tm=1024, tn=1024, tk=1024)
    o, x2, h3 = _attn_fwd(q, kv, x1, wfull["w_xo"], norm_ffn, bl=bl, s=s, name="attn_fwd")
    finish_gather("late", late, h3)
    gu, act, dx3, loss_part, d_norm_final = _ffn_fwd(h3, x2, tgt, wfull["w_gu"], wfull["w_down"],
                                                     norm_final.reshape(1, d), name="ffn_fwd")

    grads = {}
    sent = []

    def send_grads(names, tag, after=None):
        blocks, land_shapes = [], []
        for n in names:
            g = grads[n]
            if g.ndim == 2 and n in col_sharded:
                land_shapes.append((N_PEERS, g.shape[0], g.shape[1] // N_DEV))
            else:
                if g.ndim == 2:
                    g = g.reshape(N_DEV, -1, g.shape[1])
                land_shapes.append((N_PEERS, *g.shape[1:]))
            blocks.append(g)
        ssem, rsem, srcs, lands, tok = _copy_start("scatter", blocks, land_shapes, name=f"grads_{tag}_start", after=after)
        sent.append((names, ssem, rsem, srcs, lands))
        return tok[0:1, 0:1]

    dgu, dx2, do, d_norm_ffn = _ffn_bwd(dx3, gu, x2, wfull["w_down"], wfull["w_gu"], norm_ffn, wfull["w_xo"], name="ffn_bwd")
    grads["w_down"] = _matmul(act, dx3, mode="tn", out_dtype=BF16, name="mm_dw_down", tm=1408, tn=1024, tk=2048)
    grads["w_gu"] = _matmul(dgu, h3, mode="tn", out_dtype=BF16, name="mm_dw_gu", tm=1408, tn=1024, tk=2048)
    tok = send_grads(["w_down", "w_gu"], "ffn")
    grads["w_xo"] = _matmul(o, dx2, mode="tn", out_dtype=BF16, name="mm_dw_xo", tm=1024, tn=1024, tk=2048)
    dq, dkv = _attn_bwd(q, kv, do, bl=bl, s=s, name="attn_bwd")
    grads["w_kv"] = _matmul(mem_n, dkv, mode="tn", out_dtype=BF16, name="mm_dw_kv", tm=1024, tn=256, tk=1024,
                            col_blocks=N_DEV)
    tok2 = send_grads(["w_xo", "w_kv"], "attn")
    dmem_n = _matmul(dkv, wfull["w_kv"], mode="nt", out_dtype=F32, name="mm_d_mem", tm=512, tn=1024, tk=2048)
    d_norm_mem = _rms_bwd(None, dmem_n, memf, norm_mem, name="rms_mem_bwd", need_dx=False)
    dx1, d_norm_xattn, dw_q = _proj_rms_bwd(dq, dx2, x1, wfull["w_q"], norm_xattn + (tok + tok2), name="q_rms_bwd", h=h2)
    dp, dy_a, dy_b, d_b_gate, dw_mix, dw_in_gates = _gates_bwd_fused(dx1, p, y_a, y_b, bg_full, wfull["w_mix_out"],
                                                                    merged, h1, name="gates_bwd")
    grads["w_q"] = dw_q.astype(BF16)
    grads["w_mix_out"] = dw_mix.astype(BF16)
    grads["w_sgu_out"] = _matmul(sg, dy_b, mode="tn", out_dtype=BF16, name="mm_dw_sgu", tm=1024, tn=1024, tk=2048)
    dc, d_conv_ln_g, d_conv_ln_b, dw_conv = _conv_ln_bwd_fused(dy_a, c_conv, a_act, wfull["w_conv_out"], conv_ln_g,
                                                               conv_ln_b, name="conv_ln_bwd")
    grads["w_conv_out"] = dw_conv.astype(BF16)
    tok = send_grads(["w_q", "w_mix_out", "w_sgu_out", "w_conv_out"], "mixer")
    dp, d_sgu_w, d_sgu_b, d_sgu_ln_g, d_sgu_ln_b = _sgu_bwd(dp, dy_b, wfull["w_sgu_out"], p, vn, wm, wmt, sgu_bias,
                                                             sgu_ln_g + tok, name="sgu_bwd")
    dw_in_sgu = _matmul(h1, dp, mode="tn", out_dtype=BF16, name="mm_dw_in_sgu", tm=1024, tn=1024, tk=2048,
                        b_cols=(2 * d, 2 * d))
    dp, d_conv_w, d_conv_b, dw_in_conv = _conv_bwd(dp, dc, p, cw_full, h1, bl=bl, s=s, name="conv_bwd")
    grads["w_in"] = jnp.concatenate([dw_in_conv.astype(BF16), dw_in_sgu, dw_in_gates], axis=1)
    tok = send_grads(["w_in"], "in")
    grad_x, d_norm_mix = _proj_rms_bwd(dp, dx1, xf, w_in_blocks, norm_mix + tok, name="in_proj_bwd")
    out = {}

    vec_names = ["norm_mix", "conv_b", "conv_ln_g", "conv_ln_b", "sgu_ln_g", "sgu_ln_b", "norm_xattn", "norm_mem",
                 "norm_ffn", "norm_final"]
    vec_grads = [d_norm_mix, d_conv_b, d_conv_ln_g, d_conv_ln_b, d_sgu_ln_g, d_sgu_ln_b, d_norm_xattn, d_norm_mem,
                 d_norm_ffn, d_norm_final]
    n_vec = len(vec_names)
    small_vec = jnp.concatenate([g.reshape(1, d) for g in vec_grads]
                                + [jnp.broadcast_to(loss_part, (1, d)), jnp.zeros((16 - n_vec - 1, d), F32)], axis=0)
    small_cols = jnp.concatenate([d_b_gate, d_conv_w], axis=0)
    parts_vec, parts_sb, parts_sw, parts_cols = _all_gather([small_vec, d_sgu_b, d_sgu_w, small_cols],
                                                            name="gather_small_grads")
    rep_names = vec_names + ["sgu_b", "sgu_w"]
    rep_shapes = [(1, d)] * n_vec + [d_sgu_b.shape, d_sgu_w.shape]
    states = [tuple(given[pre + n].reshape(shape) for pre in ("", "m_", "v_")) for n, shape in zip(rep_names, rep_shapes)]
    res_rep = _adamw_replicated([parts_vec, parts_sb, parts_sw], states, n_vec, name="adamw_small")
    for i, n in enumerate(rep_names):
        out[n] = [r.reshape(given[n].shape) for r in res_rep[1 + 4 * i:5 + 4 * i]]
    res_cols = _adamw_column_shards(parts_cols, dev_id, [(b_gate[0], m_b_gate[0], v_b_gate[0]),
                                                        (conv_w[0], m_conv_w[0], v_conv_w[0])], (0, 8),
                                    name="adamw_small_cols")
    out["b_gate"] = [r[None] for r in res_cols[0:4]]
    out["conv_w"] = [r[None] for r in res_cols[4:8]]

    done = res_rep[1]
    for names, ssem, rsem, srcs, lands in sent:
        srcs, lands = _copy_wait("scatter", ssem, rsem, srcs, lands, done, name=f"grads_{names[0]}_wait")
        for n, partials, landed in zip(names, srcs, lands):
            res = _adamw_shard(partials, landed, dev_id, shard_of(n), shard_of(n, "m_"), shard_of(n, "v_"),
                               name=f"adamw_{n}")
            done = res[0]
            out[n] = [(jnp.transpose(r) if n in transposed else r)[None] for r in res]

    order = ["norm_mix", "w_in", "b_gate", "conv_w", "conv_b", "conv_ln_g", "conv_ln_b", "w_conv_out", "sgu_ln_g",
             "sgu_ln_b", "sgu_w", "sgu_b", "w_sgu_out", "w_mix_out", "norm_xattn", "norm_mem", "w_q", "w_kv", "w_xo",
             "norm_ffn", "w_gu", "w_down", "norm_final"]
    loss = res_rep[0][0, 0]
    return (loss, grad_x.reshape(x.shape), *[out[n][0] for n in order], *[out[n][1] for n in order],
            *[out[n][2] for n in order], *[out[n][3] for n in order])
```

```python
import functools

import jax
import jax.numpy as jnp
from jax import lax
from jax.experimental import pallas as pl
from jax.experimental.pallas import tpu as pltpu

F32 = jnp.float32
BF16 = jnp.bfloat16
RMS_EPS = 1e-6
LN_EPS = 1e-5
CONV_WIDTH = 31
CONV_HALO = 32
CONV_ROWS = 128
CONV_COLS = 256
LANES = 128
SGU_CHUNK = 128
SGU_GROUPS = 8
SGU_TILE = 512
HEADS = 4
N_DEV = 8
ADAM_LR, ADAM_B1, ADAM_B2, ADAM_EPS, ADAM_WD, ADAM_STEP = 0.001, 0.9, 0.999, 1e-08, 0.01, 10
VMEM_LIMIT = 56 * 1024 * 1024
TOKEN_TILE = 256
ATTN_TILE = 1024
MESH_ID = pl.DeviceIdType.MESH

_GELU_K = 0.7978845608028654
_GELU_C = 0.044715


def _cparams(sem=None):
    return pltpu.CompilerParams(dimension_semantics=sem, vmem_limit_bytes=VMEM_LIMIT)


def _sigmoid(v):
    return 0.5 * jnp.tanh(0.5 * v) + 0.5


def _gelu(v):
    return 0.5 * v * (1.0 + jnp.tanh(_GELU_K * (v + _GELU_C * v * v * v)))


def _gelu_grad(v):
    th = jnp.tanh(_GELU_K * (v + _GELU_C * v * v * v))
    return 0.5 * (1.0 + th) + 0.5 * v * (1.0 - th * th) * _GELU_K * (1.0 + 3.0 * _GELU_C * v * v)


def _dot(a, b, dims):
    return lax.dot_general(a, b, (dims, ((), ())), preferred_element_type=F32)


_NN = ((1,), (0,))
_NT = ((1,), (1,))
_TN = ((0,), (0,))


def _matmul(a, b, *, mode, out_dtype, name, tm=512, tn=512, tk=512, chunk=None, residual=None, rms_gain=None,
            col_blocks=None, b_cols=None, out_into=None):
    if mode == "nn":
        (m, k), (_, n) = a.shape, b.shape
    elif mode == "nt":
        (m, k), (n, _) = a.shape, b.shape
    else:
        (k, m), (_, n) = a.shape, b.shape
    b_first = 0
    if b_cols is not None:
        assert mode == "tn"
        b_first, n = b_cols
    tm, tn, tk = min(tm, m), min(tn, n), min(tk, k)
    assert b_first % tn == 0
    b_first //= tn
    assert m % tm == 0 and n % tn == 0 and k % tk == 0, (name, a.shape, b.shape, tm, tn, tk)
    nk = k // tk
    dims = {"nn": _NN, "nt": _NT, "tn": _TN}[mode]
    chunk = tn if chunk is None else min(chunk, tn)
    assert tn % chunk == 0
    if rms_gain is not None:
        assert tn == n and chunk == n

    def body(*refs):
        refs = list(refs)
        a_ref, b_ref = refs[:2]
        pos = 2
        r_ref = g_ref = None
        if residual is not None:
            r_ref = refs[pos]
            pos += 1
        if rms_gain is not None:
            g_ref = refs[pos]
            pos += 1
        if out_into is not None:
            pos += 1
        o_ref = refs[pos]
        pos += 1
        h_ref = None
        if rms_gain is not None:
            h_ref = refs[pos]
            pos += 1
        acc_ref = refs[pos] if nk > 1 else None
        av = a_ref[...].astype(BF16)
        for c0 in range(0, tn, chunk):
            cs = slice(c0, c0 + chunk)
            bv = (b_ref[cs, :] if mode == "nt" else b_ref[:, cs]).astype(BF16)
            part = _dot(av, bv, dims)

            def finish(res, cs=cs):
                if r_ref is not None:
                    res = res + r_ref[:, cs].astype(F32)
                o_ref[:, cs] = res.astype(out_dtype)
                if h_ref is not None:
                    r = lax.rsqrt(jnp.mean(res * res, axis=-1, keepdims=True) + RMS_EPS)
                    h_ref[...] = (res * r * g_ref[...]).astype(BF16)

            if nk == 1:
                finish(part)
            else:
                kk = pl.program_id(2)

                @pl.when(kk == 0)
                def _(part=part, cs=cs):
                    acc_ref[:, cs] = part

                @pl.when(kk > 0)
                def _(part=part, cs=cs):
                    acc_ref[:, cs] += part

                @pl.when(kk == nk - 1)
                def _(finish=finish, cs=cs):
                    finish(acc_ref[:, cs])

    resident = dict(pipeline_mode=pl.Buffered(1)) if (n == tn and nk == 1 and mode != "tn" and m > tm) else {}
    if mode == "nn":
        a_spec = pl.BlockSpec((tm, tk), lambda i, j, kk: (i, kk))
        b_spec = pl.BlockSpec((tk, tn), lambda i, j, kk: (kk, j), **resident)
    elif mode == "nt":
        a_spec = pl.BlockSpec((tm, tk), lambda i, j, kk: (i, kk))
        b_spec = pl.BlockSpec((tn, tk), lambda i, j, kk: (j, kk), **resident)
    else:
        a_spec = pl.BlockSpec((tk, tm), lambda i, j, kk: (kk, i))
        b_spec = pl.BlockSpec((tk, tn), lambda i, j, kk: (kk, j + b_first))
    o_spec = pl.BlockSpec((tm, tn), lambda i, j, kk: (i, j))
    in_specs, args = [a_spec, b_spec], [a, b]
    if residual is not None:
        in_specs.append(o_spec)
        args.append(residual)
    out_shape, out_specs = [jax.ShapeDtypeStruct((m, n), out_dtype)], [o_spec]
    if col_blocks is not None:
        assert residual is None and rms_gain is None and (n // col_blocks) % tn == 0
        per = n // col_blocks // tn
        out_shape = [jax.ShapeDtypeStruct((col_blocks, m, n // col_blocks), out_dtype)]
        out_specs = [pl.BlockSpec((None, tm, tn), lambda i, j, kk: (j // per, i, j % per))]
    if rms_gain is not None:
        in_specs.append(pl.BlockSpec((1, n), lambda i, j, kk: (0, 0)))
        args.append(rms_gain)
        out_shape.append(jax.ShapeDtypeStruct((m, n), BF16))
        out_specs.append(o_spec)
    aliases = {}
    if out_into is not None:
        target, first = out_into
        assert col_blocks is None and rms_gain is None and first % tn == 0 and target.dtype == out_dtype
        in_specs.append(pl.BlockSpec(memory_space=pl.ANY))
        args.append(target)
        aliases = {len(args) - 1: 0}
        out_shape = [jax.ShapeDtypeStruct(target.shape, target.dtype)]
        out_specs = [pl.BlockSpec((tm, tn), lambda i, j, kk: (i, j + first // tn))]
    res = pl.pallas_call(
        body, name=name, grid=(m // tm, n // tn, nk), in_specs=in_specs, out_specs=out_specs, out_shape=out_shape,
        scratch_shapes=[pltpu.VMEM((tm, tn), F32)] if nk > 1 else [], input_output_aliases=aliases,
        compiler_params=_cparams(("parallel", "parallel", "arbitrary")),
    )(*args)
    return res if rms_gain is not None else res[0]


def _row_call(name, t, tm, rows_in, residents, rows_out, accs, body):
    n_in, n_res, n_out, n_acc = len(rows_in), len(residents), len(rows_out), len(accs)
    steps = t // tm
    assert t % tm == 0
    narrow = [i for i, a in enumerate(accs) if a[1] != F32]

    def kernel_body(*refs):
        in_refs, res_refs = refs[:n_in], refs[n_in:n_in + n_res]
        out_refs = refs[n_in + n_res:n_in + n_res + n_out]
        acc_out = list(refs[n_in + n_res + n_out:n_in + n_res + n_out + n_acc])
        scratch = refs[n_in + n_res + n_out + n_acc:]
        acc_refs = list(acc_out)
        for s_ref, i in zip(scratch, narrow):
            acc_refs[i] = s_ref
        if accs:
            @pl.when(pl.program_id(0) == 0)
            def _():
                for acc in acc_refs:
                    acc[...] = jnp.zeros_like(acc)
        body(in_refs, res_refs, out_refs, acc_refs)
        if narrow:
            @pl.when(pl.program_id(0) == steps - 1)
            def _():
                for i in narrow:
                    acc_out[i][...] = acc_refs[i][...].astype(acc_out[i].dtype)

    once = dict(pipeline_mode=pl.Buffered(1)) if steps > 1 else {}
    in_specs = [pl.BlockSpec((tm, cols), lambda i, cb=cb: (i, cb)) for _, cols, cb in rows_in]
    in_specs += [pl.BlockSpec(r.shape, lambda i, nd=r.ndim: (0,) * nd, **once) for r in residents]
    out_specs = [pl.BlockSpec((tm, cols), lambda i, cb=cb: (i, cb)) for _, cols, cb, _ in rows_out]
    out_specs += [pl.BlockSpec(a[0], lambda i, nd=len(a[0]), cb=(a[3] if len(a) == 4 else 0): (0,) * (nd - 1) + (cb,))
                  for a in accs]
    out_shape = [jax.ShapeDtypeStruct((t, total), dt) for total, _, _, dt in rows_out]
    out_shape += [jax.ShapeDtypeStruct((a[0][0], a[2]) if len(a) == 4 else a[0], a[1]) for a in accs]
    return pl.pallas_call(
        kernel_body, name=name, grid=(steps,), in_specs=in_specs, out_specs=out_specs, out_shape=out_shape,
        scratch_shapes=[pltpu.VMEM(accs[i][0], F32) for i in narrow],
        compiler_params=_cparams(("arbitrary",) if accs else ("parallel",)),
    )(*[a for a, _, _ in rows_in], *residents)


def _rms_apply(xv, gain):
    return xv * lax.rsqrt(jnp.mean(xv * xv, axis=-1, keepdims=True) + RMS_EPS) * gain


def _rms_grad(dres, dh, xv, gain):
    r = lax.rsqrt(jnp.mean(xv * xv, axis=-1, keepdims=True) + RMS_EPS)
    xhat = xv * r
    dxh = dh * gain
    dx = dres + r * (dxh - xhat * jnp.mean(dxh * xhat, axis=-1, keepdims=True))
    return dx, jnp.sum(dh * xhat, axis=0, keepdims=True)


def _in_proj_gather(xf, gain, w_shard, *, name):
    t, d = xf.shape
    cb = w_shard.shape[1]
    tm = min(1024, t)
    steps = t // tm
    mx, my, _ = _mesh_pos()
    order = jnp.stack([2 * mx + my, 2 * (1 - mx) + my, 2 * mx + (1 - my), 2 * (1 - mx) + (1 - my)]).astype(jnp.int32)

    def body(order_ref, x_ref, g_ref, ws_ref, h_ref, p_ref, wout_ref, w_ref, send_sems, recv_sems, own_sem):
        ps, i = pl.program_id(0), pl.program_id(1)
        x, y, c = _mesh_pos()
        me, sib = (x, y, c), (x, y, 1 - c)
        chips = [(1 - x, y), (x, 1 - y), (1 - x, 1 - y)]

        def copy(k, block, to, from_shard=False):
            return pltpu.make_async_remote_copy(
                src_ref=ws_ref if from_shard else w_ref.at[_dev_index(block)], dst_ref=w_ref.at[_dev_index(block)],
                send_sem=send_sems.at[k], recv_sem=recv_sems.at[k], device_id=to, device_id_type=MESH_ID)

        own = pltpu.make_async_copy(ws_ref, w_ref.at[_dev_index(me)], own_sem)
        first = [copy(0, me, sib, True)] + [copy(1 + j, me, (*chip, c), True) for j, chip in enumerate(chips)]
        passed = [copy(4 + j, (*chip, c), sib) for j, chip in enumerate(chips)]

        @pl.when(jnp.logical_and(ps == 0, i == 0))
        def _():
            own.start()
            for cp in first:
                cp.start()
            own.wait()
            copy(0, sib, me).wait_recv()

        for j, chip in enumerate(chips):
            @pl.when(jnp.logical_and(ps == j + 1, i == 0))
            def _(j=j, chip=chip):
                copy(1 + j, (*chip, c), me).wait_recv()
                passed[j].start()
                copy(4 + j, (*chip, 1 - c), me).wait_recv()

        h = _rms_apply(x_ref[...], g_ref[...]).astype(BF16)
        h_ref[...] = h
        chip_id = order_ref[ps]
        p_ref[:, 0:cb] = _dot(h, w_ref[2 * chip_id], _NN).astype(BF16)
        p_ref[:, cb:2 * cb] = _dot(h, w_ref[2 * chip_id + 1], _NN).astype(BF16)

        @pl.when(jnp.logical_and(ps == 3, i == steps - 1))
        def _():
            for cp in first + passed:
                cp.wait_send()
            keep = pltpu.make_async_copy(w_ref, wout_ref, own_sem)
            keep.start()
            keep.wait()

    gs = pltpu.PrefetchScalarGridSpec(
        num_scalar_prefetch=1, grid=(4, steps),
        in_specs=[pl.BlockSpec((tm, d), lambda ps, i, o: (i, 0)), pl.BlockSpec((1, d), lambda ps, i, o: (0, 0)),
                  pl.BlockSpec(memory_space=pl.ANY)],
        out_specs=[pl.BlockSpec((tm, d), lambda ps, i, o: (jnp.where(ps == 0, i, steps - 1), 0)),
                   pl.BlockSpec((tm, 2 * cb), lambda ps, i, o: (i, o[ps])), pl.BlockSpec(memory_space=pl.ANY)],
        scratch_shapes=[pltpu.VMEM((N_DEV, d, cb), BF16), pltpu.SemaphoreType.DMA((7,)), pltpu.SemaphoreType.DMA((7,)),
                        pltpu.SemaphoreType.DMA(())])
    return pl.pallas_call(
        body, name=name, grid_spec=gs,
        out_shape=[jax.ShapeDtypeStruct((t, d), BF16), jax.ShapeDtypeStruct((t, N_DEV * cb), BF16),
                   jax.ShapeDtypeStruct((N_DEV, d, cb), BF16)],
        compiler_params=_cparams(("arbitrary", "arbitrary")))(order, xf, gain, w_shard)


def _mix_out(p, y_a, y_b, b_gate, xf, w_mix, gain, w_q, *, name):
    t, d = xf.shape

    def body(ins, res, outs, accs):
        ga_ref, gb_ref, ya_ref, yb_ref, x_ref = ins
        bg_ref, wm_ref, g_ref, wq_ref = res
        m_ref, x1_ref, h_ref, q_ref = outs
        sa = _sigmoid(ga_ref[...].astype(F32) + bg_ref[0:1, :])
        sb = _sigmoid(gb_ref[...].astype(F32) + bg_ref[1:2, :])
        merged = (sa * ya_ref[...].astype(F32) + sb * yb_ref[...].astype(F32)).astype(BF16)
        m_ref[...] = merged
        x1 = x_ref[...] + _dot(merged, wm_ref[...], _NN)
        x1_ref[...] = x1
        h = _rms_apply(x1, g_ref[...]).astype(BF16)
        h_ref[...] = h
        q_ref[...] = _dot(h, wq_ref[...], _NN).astype(BF16)

    return _row_call(name, t, min(512, t), [(p, d, 4), (p, d, 5), (y_a, d, 0), (y_b, d, 0), (xf, d, 0)],
                     [b_gate, w_mix, gain, w_q], [(d, d, 0, BF16), (d, d, 0, F32), (d, d, 0, BF16), (d, d, 0, BF16)], [], body)


def _ffn_fwd(h3, x2, target, w_gu_t, w_down, gain, *, name):
    t, d = x2.shape
    f2 = w_gu_t.shape[0]
    f = f2 // 2
    half = f // 2

    def body(ins, res, outs, accs):
        h_ref, x2_ref, t_ref = ins
        wgu_ref, wd_ref, g_ref = res
        gu_ref, act_ref, dx_ref = outs
        loss_ref, dg_ref = accs
        h = h_ref[...]
        x3 = x2_ref[...]
        for c0 in (0, half):
            gt = _dot(h, wgu_ref[c0:c0 + half, :], _NT).astype(BF16)
            up = _dot(h, wgu_ref[f + c0:f + c0 + half, :], _NT).astype(BF16)
            gu_ref[:, c0:c0 + half] = gt
            gu_ref[:, f + c0:f + c0 + half] = up
            gtf = gt.astype(F32)
            act = (gtf * _sigmoid(gtf) * up.astype(F32)).astype(BF16)
            act_ref[:, c0:c0 + half] = act
            x3 = x3 + _dot(act, wd_ref[c0:c0 + half, :], _NN)
        g = g_ref[...]
        r = lax.rsqrt(jnp.mean(x3 * x3, axis=-1, keepdims=True) + RMS_EPS)
        xhat = x3 * r
        err = xhat * g - t_ref[...]
        loss_ref[...] += 0.5 * jnp.sum(jnp.mean(err * err, axis=-1, keepdims=True), axis=0, keepdims=True)
        dy = err * (1.0 / d)
        dg_ref[...] += jnp.sum(dy * xhat, axis=0, keepdims=True)
        dxh = dy * g
        dx_ref[...] = r * (dxh - xhat * jnp.mean(dxh * xhat, axis=-1, keepdims=True))

    return _row_call(name, t, min(256, t), [(h3, d, 0), (x2, d, 0), (target, d, 0)], [w_gu_t, w_down, gain],
                     [(f2, f2, 0, BF16), (f, f, 0, BF16), (d, d, 0, F32)], [((1, 1), F32), ((1, d), F32)], body)


def _ffn_bwd(dx3, gu, x2, w_down, w_gu_t, gain, w_xo, *, name):
    t, d = x2.shape
    f2 = w_gu_t.shape[0]
    f = f2 // 2
    half = f // 2

    def body(ins, res, outs, accs):
        dx3_ref, gu_ref, x2_ref = ins
        wd_ref, wgu_ref, g_ref, wxo_ref = res
        dgu_ref, dx2_ref, do_ref = outs
        (dg_ref,) = accs
        dx3v = dx3_ref[...]
        dxb = dx3v.astype(BF16)
        dh = jnp.zeros(dx3v.shape, F32)
        for c0 in (0, half):
            dact = _dot(dxb, wd_ref[c0:c0 + half, :], _NT)
            gt = gu_ref[:, c0:c0 + half].astype(F32)
            up = gu_ref[:, f + c0:f + c0 + half].astype(F32)
            sg = _sigmoid(gt)
            dgt = (dact * up * sg * (1.0 + gt * (1.0 - sg))).astype(BF16)
            dup = (dact * gt * sg).astype(BF16)
            dgu_ref[:, c0:c0 + half] = dgt
            dgu_ref[:, f + c0:f + c0 + half] = dup
            dh = dh + _dot(dgt, wgu_ref[c0:c0 + half, :], _NN) + _dot(dup, wgu_ref[f + c0:f + c0 + half, :], _NN)
        dx2, dg = _rms_grad(dx3v, dh, x2_ref[...], g_ref[...])
        dx2_ref[...] = dx2
        dg_ref[...] += dg
        do_ref[...] = _dot(dx2.astype(BF16), wxo_ref[...], _NT).astype(BF16)

    return _row_call(name, t, min(256, t), [(dx3, d, 0), (gu, f2, 0), (x2, d, 0)], [w_down, w_gu_t, gain, w_xo],
                     [(f2, f2, 0, BF16), (d, d, 0, F32), (d, d, 0, BF16)], [((1, d), F32)], body)


def _proj_rms_bwd(dy, dres, x, w, gain, *, name, h=None):
    t, d = x.shape
    k = dy.shape[1]

    def body(ins, res, outs, accs):
        dy_ref, dres_ref, x_ref = ins[:3]
        w_ref, g_ref = res
        if h is not None:
            accs[1][...] += _dot(ins[3][...], dy_ref[...], _TN)
        if w.ndim == 3:
            cb = w.shape[2]
            dh = _dot(dy_ref[:, 0:cb], w_ref[0], _NT)
            for j in range(1, w.shape[0]):
                dh = dh + _dot(dy_ref[:, j * cb:(j + 1) * cb], w_ref[j], _NT)
        else:
            dh = _dot(dy_ref[...], w_ref[...], _NT)
        dx, dg = _rms_grad(dres_ref[...], dh, x_ref[...], g_ref[...])
        outs[0][...] = dx
        accs[0][...] += dg

    rows_in = [(dy, k, 0), (dres, d, 0), (x, d, 0)] + ([(h, d, 0)] if h is not None else [])
    accs = [((1, d), F32)] + ([((d, k), BF16)] if h is not None else [])
    tm = 1024 if w.ndim == 2 else 512
    return _row_call(name, t, min(tm, t), rows_in, [w, gain], [(d, d, 0, F32)], accs, body)


def _gates_bwd_fused(dx1, p, y_a, y_b, b_gate, w_mix, merged, h1, *, name):
    t, d = y_a.shape

    def body(ins, res, outs, accs):
        dx_ref, ga_ref, gb_ref, ya_ref, yb_ref, m_ref, h1_ref = ins
        bg_ref, wm_ref = res
        dp_ref, dya_ref, dyb_ref = outs
        dbg_ref, dwm_ref, dwin_ref = accs
        dxb = dx_ref[...].astype(BF16)
        dwm_ref[...] += _dot(m_ref[...], dxb, _TN)
        dm = _dot(dxb, wm_ref[...], _NT)
        sa = _sigmoid(ga_ref[...].astype(F32) + bg_ref[0:1, :])
        sb = _sigmoid(gb_ref[...].astype(F32) + bg_ref[1:2, :])
        dya_ref[...] = (dm * sa).astype(BF16)
        dyb_ref[...] = (dm * sb).astype(BF16)
        dga = dm * ya_ref[...].astype(F32) * sa * (1.0 - sa)
        dgb = dm * yb_ref[...].astype(F32) * sb * (1.0 - sb)
        dp_ref[:, 0:d] = dga.astype(BF16)
        dp_ref[:, d:2 * d] = dgb.astype(BF16)
        dbg_ref[0:1, :] += jnp.sum(dga, axis=0, keepdims=True)
        dbg_ref[1:2, :] += jnp.sum(dgb, axis=0, keepdims=True)
        dwin_ref[...] += _dot(h1_ref[...], dp_ref[...], _TN)

    return _row_call(name, t, min(256, t),
                     [(dx1, d, 0), (p, d, 4), (p, d, 5), (y_a, d, 0), (y_b, d, 0), (merged, d, 0), (h1, d, 0)],
                     [b_gate, w_mix], [(p.shape[1], 2 * d, 2, BF16), (d, d, 0, BF16), (d, d, 0, BF16)],
                     [((8, d), F32), ((d, d), BF16), ((d, 2 * d), BF16, p.shape[1], 2)], body)


def _conv_ln_bwd_fused(dy_a, c, a_act, w_conv_out, ln_g, ln_b, *, name):
    t, d = c.shape

    def body(ins, res, outs, accs):
        dy_ref, c_ref, act_ref = ins
        w_ref, lg_ref, lb_ref = res
        dlg_ref, dlb_ref, dw_ref = accs
        dw_ref[...] += _dot(act_ref[...], dy_ref[...], _TN)
        dact = _dot(dy_ref[...], w_ref[...], _NT)
        cv = c_ref[...].astype(F32)
        g = lg_ref[...]
        mu = jnp.mean(cv, axis=-1, keepdims=True)
        dv = cv - mu
        rstd = lax.rsqrt(jnp.mean(dv * dv, axis=-1, keepdims=True) + LN_EPS)
        chat = dv * rstd
        aln = chat * g + lb_ref[...]
        sg = _sigmoid(aln)
        daln = dact * (sg * (1.0 + aln * (1.0 - sg)))
        dlb_ref[...] += jnp.sum(daln, axis=0, keepdims=True)
        dlg_ref[...] += jnp.sum(daln * chat, axis=0, keepdims=True)
        dchat = daln * g
        dc = rstd * (dchat - jnp.mean(dchat, axis=-1, keepdims=True)
                     - chat * jnp.mean(dchat * chat, axis=-1, keepdims=True))
        outs[0][...] = dc.astype(BF16)

    return _row_call(name, t, min(1024, t), [(dy_a, d, 0), (c, d, 0), (a_act, d, 0)], [w_conv_out, ln_g, ln_b],
                     [(d, d, 0, BF16)], [((1, d), F32), ((1, d), F32), ((d, d), BF16)], body)


def _row_spec(tt, cols, col_block=0):
    return pl.BlockSpec((tt, cols), lambda i: (i, col_block))


def _const_spec(shape):
    return pl.BlockSpec(shape, lambda *_: (0,) * len(shape))


def _rms_fwd(x, gain, *, name):
    t, d = x.shape
    tt = min(TOKEN_TILE, t)

    def body(x_ref, g_ref, h_ref):
        xv = x_ref[...]
        r = lax.rsqrt(jnp.mean(xv * xv, axis=-1, keepdims=True) + RMS_EPS)
        h_ref[...] = (xv * r * g_ref[...]).astype(BF16)

    return pl.pallas_call(
        body, name=name, grid=(t // tt,), in_specs=[_row_spec(tt, d), _const_spec((1, d))],
        out_specs=_row_spec(tt, d), out_shape=jax.ShapeDtypeStruct((t, d), BF16),
        compiler_params=_cparams(("parallel",)))(x, gain)


def _rms_bwd(dres, dh, x, gain, *, name, need_dx=True):
    t, d = x.shape
    tt = min(TOKEN_TILE, t)

    def body(*refs):
        if need_dx:
            dres_ref, dh_ref, x_ref, g_ref, dx_ref, dg_ref = refs
        else:
            dh_ref, x_ref, g_ref, dg_ref = refs

        @pl.when(pl.program_id(0) == 0)
        def _():
            dg_ref[...] = jnp.zeros_like(dg_ref)

        xv = x_ref[...]
        dhv = dh_ref[...].astype(F32)
        r = lax.rsqrt(jnp.mean(xv * xv, axis=-1, keepdims=True) + RMS_EPS)
        xhat = xv * r
        dg_ref[...] += jnp.sum(dhv * xhat, axis=0, keepdims=True)
        if need_dx:
            dxh = dhv * g_ref[...]
            dx_ref[...] = dres_ref[...] + r * (dxh - xhat * jnp.mean(dxh * xhat, axis=-1, keepdims=True))

    rs = _row_spec(tt, d)
    if need_dx:
        in_specs, args = [rs, rs, rs, _const_spec((1, d))], (dres, dh, x, gain)
        out_specs = [rs, _const_spec((1, d))]
        out_shape = [jax.ShapeDtypeStruct((t, d), F32), jax.ShapeDtypeStruct((1, d), F32)]
    else:
        in_specs, args = [rs, rs, _const_spec((1, d))], (dh, x, gain)
        out_specs = [_const_spec((1, d))]
        out_shape = [jax.ShapeDtypeStruct((1, d), F32)]
    res = pl.pallas_call(body, name=name, grid=(t // tt,), in_specs=in_specs, out_specs=out_specs, out_shape=out_shape,
                         compiler_params=_cparams(("arbitrary",)))(*args)
    return res if need_dx else res[0]


SUBLANES = 8
SHIFT_ROWS = 40


def _conv_apply(sbuf_ref, w_ref, out_ref, tt, offsets, bias_ref=None):
    d = out_ref.shape[1]
    for cc in range(d // LANES):
        cs = slice(cc * LANES, (cc + 1) * LANES)
        taps = [jnp.broadcast_to(w_ref[k:k + 1, cs], (SUBLANES, LANES)) for k in range(CONV_WIDTH)]
        bias = None if bias_ref is None else jnp.broadcast_to(bias_ref[:, cs], (SUBLANES, LANES))

        def row_body(r, carry, cs=cs, taps=taps, bias=bias):
            r0 = pl.multiple_of(r * CONV_ROWS, CONV_ROWS)
            for q in range(CONV_ROWS // SUBLANES):
                acc = _tap(sbuf_ref, r0 + q * SUBLANES, cs, offsets[0]) * taps[0]
                for k in range(1, CONV_WIDTH):
                    acc = acc + _tap(sbuf_ref, r0 + q * SUBLANES, cs, offsets[k]) * taps[k]
                if bias is not None:
                    acc = acc + bias
                out_ref[pl.ds(r0 + q * SUBLANES, SUBLANES), cs] = acc
            return carry

        lax.fori_loop(0, tt // CONV_ROWS, row_body, 0)


def _fill_shifts(sbuf_ref, rows):
    d = sbuf_ref.shape[2]
    assert rows % SHIFT_ROWS == 0

    def row_body(i, carry):
        r0 = pl.multiple_of(i * SHIFT_ROWS, SUBLANES)
        for cc in range(d // CONV_COLS):
            cs = slice(cc * CONV_COLS, (cc + 1) * CONV_COLS)
            win = sbuf_ref[0, pl.ds(r0, SHIFT_ROWS + SUBLANES), cs]
            for sh in range(1, SUBLANES):
                sbuf_ref[sh, pl.ds(r0, SHIFT_ROWS), cs] = win[sh:sh + SHIFT_ROWS, :]
        return carry

    lax.fori_loop(0, rows // SHIFT_ROWS, row_body, 0)


def _tap(sbuf_ref, r0, cs, offset):
    sh = offset % SUBLANES
    return sbuf_ref[sh, pl.ds(pl.multiple_of(r0 + (offset - sh), SUBLANES), SUBLANES), cs]


def _conv_specs(bl, s, tt, d, col_a, col_g):
    nj = s // tt
    per = tt // CONV_HALO
    main_a = pl.BlockSpec((tt, d), lambda b, j: (b * nj + j, col_a))
    main_g = pl.BlockSpec((tt, d), lambda b, j: (b * nj + j, col_g))
    prev = lambda b, j: jnp.maximum((b * nj + j) * per - 1, 0)
    halo_a = pl.BlockSpec((CONV_HALO, d), lambda b, j: (prev(b, j), col_a))
    halo_g = pl.BlockSpec((CONV_HALO, d), lambda b, j: (prev(b, j), col_g))
    return main_a, main_g, halo_a, halo_g


def _fill_glu(sbuf_ref, a_ref, g_ref, ha_ref, hg_ref, tt):
    first = pl.program_id(1) == 0
    ha = ha_ref[...].astype(F32)
    hg = hg_ref[...].astype(F32)
    sbuf_ref[0, pl.ds(0, CONV_HALO), :] = jnp.where(first, 0.0, ha * _sigmoid(hg))
    av = a_ref[...].astype(F32)
    gv = g_ref[...].astype(F32)
    sbuf_ref[0, pl.ds(CONV_HALO, tt), :] = av * _sigmoid(gv)
    _fill_shifts(sbuf_ref, tt + CONV_HALO - SUBLANES)


def _conv_fwd(p, conv_w, conv_b, ln_g, ln_b, *, bl, s, name):
    t = p.shape[0]
    d = conv_w.shape[1]
    tt = min(TOKEN_TILE, s)
    off = CONV_HALO - (CONV_WIDTH - 1)

    def body(a_ref, g_ref, ha_ref, hg_ref, w_ref, b_ref, lg_ref, lb_ref, c_ref, act_ref, sbuf_ref, cbuf_ref):
        _fill_glu(sbuf_ref, a_ref, g_ref, ha_ref, hg_ref, tt)

        _conv_apply(sbuf_ref, w_ref, cbuf_ref, tt, [off + k for k in range(CONV_WIDTH)], bias_ref=b_ref)
        cv = cbuf_ref[...]
        c_ref[...] = cv.astype(BF16)
        mu = jnp.mean(cv, axis=-1, keepdims=True)
        dv = cv - mu
        rstd = lax.rsqrt(jnp.mean(dv * dv, axis=-1, keepdims=True) + LN_EPS)
        aln = dv * rstd * lg_ref[...] + lb_ref[...]
        act_ref[...] = (aln * _sigmoid(aln)).astype(BF16)

    main_a, main_g, halo_a, halo_g = _conv_specs(bl, s, tt, d, 0, 1)
    out_spec = pl.BlockSpec((tt, d), lambda b, j: (b * (s // tt) + j, 0))
    return pl.pallas_call(
        body, name=name, grid=(bl, s // tt),
        in_specs=[main_a, main_g, halo_a, halo_g, _const_spec((CONV_HALO, d)), _const_spec((1, d)), _const_spec((1, d)),
                  _const_spec((1, d))],
        out_specs=[out_spec, out_spec],
        out_shape=[jax.ShapeDtypeStruct((t, d), BF16), jax.ShapeDtypeStruct((t, d), BF16)],
        scratch_shapes=[pltpu.VMEM((SUBLANES, tt + CONV_HALO, d), F32), pltpu.VMEM((tt, d), F32)],
        compiler_params=_cparams(("parallel", "parallel")))(p, p, p, p, conv_w, conv_b, ln_g, ln_b)


def _conv_bwd(dp, dc, p, conv_w, h1, dw_in, *, bl, s, name):
    t = p.shape[0]
    d = conv_w.shape[1]
    tt = min(TOKEN_TILE, s)
    nj = s // tt
    per = tt // CONV_HALO
    off = CONV_HALO - (CONV_WIDTH - 1)
    last_blk = t // CONV_HALO - 1

    def body(dp_in, dc_ref, dcn_ref, a_ref, g_ref, ha_ref, hg_ref, w_ref, h1_ref, dwin_in, dp_ref, dw_ref, db_ref,
             dwin_out, gbuf_ref, dbuf_ref, dglu_ref, acc_ref, dwin_ref):
        del dp_in, dwin_in
        b, j = pl.program_id(0), pl.program_id(1)
        start = jnp.logical_and(b == 0, j == 0)
        end = jnp.logical_and(b == bl - 1, j == nj - 1)

        @pl.when(start)
        def _():
            acc_ref[...] = jnp.zeros_like(acc_ref)
            db_ref[...] = jnp.zeros_like(db_ref)
            dwin_ref[...] = jnp.zeros_like(dwin_ref)

        _fill_glu(gbuf_ref, a_ref, g_ref, ha_ref, hg_ref, tt)
        dcv = dc_ref[...].astype(F32)
        dbuf_ref[0, pl.ds(0, tt), :] = dcv
        dbuf_ref[0, pl.ds(tt, CONV_HALO), :] = jnp.where(j == nj - 1, 0.0, dcn_ref[...].astype(F32))
        _fill_shifts(dbuf_ref, tt + CONV_HALO - SUBLANES)
        db_ref[...] += jnp.sum(dcv, axis=0, keepdims=True)

        for cc in range(d // LANES):
            cs = slice(cc * LANES, (cc + 1) * LANES)

            def row_body(r, accs, cs=cs):
                r0 = pl.multiple_of(r * CONV_ROWS, CONV_ROWS)
                accs = list(accs)
                for q in range(CONV_ROWS // SUBLANES):
                    dcw = dbuf_ref[0, pl.ds(r0 + q * SUBLANES, SUBLANES), cs]
                    for k in range(CONV_WIDTH):
                        accs[k] = accs[k] + dcw * _tap(gbuf_ref, r0 + q * SUBLANES, cs, off + k)
                return tuple(accs)

            zero = jnp.zeros((SUBLANES, LANES), F32)
            accs = lax.fori_loop(0, tt // CONV_ROWS, row_body, (zero,) * CONV_WIDTH)
            for k in range(CONV_WIDTH):
                acc_ref[k, :, cs] += accs[k]

        _conv_apply(dbuf_ref, w_ref, dglu_ref, tt, [CONV_WIDTH - 1 - k for k in range(CONV_WIDTH)])
        dglu = dglu_ref[...]
        av = a_ref[...].astype(F32)
        sg = _sigmoid(g_ref[...].astype(F32))
        dp_ref[:, 0:d] = (dglu * sg).astype(BF16)
        dp_ref[:, d:2 * d] = (dglu * av * sg * (1.0 - sg)).astype(BF16)
        dwin_ref[...] += _dot(h1_ref[...], dp_ref[...], _TN)

        @pl.when(end)
        def _():
            for k in range(CONV_WIDTH):
                dw_ref[k:k + 1, :] = jnp.sum(acc_ref[k], axis=0, keepdims=True)
            dw_ref[CONV_WIDTH:CONV_HALO, :] = jnp.zeros((CONV_HALO - CONV_WIDTH, d), F32)
            dwin_out[...] = dwin_ref[...].astype(dwin_out.dtype)

    main_a, main_g, halo_a, halo_g = _conv_specs(bl, s, tt, d, 0, 1)
    dc_main = pl.BlockSpec((tt, d), lambda b, j: (b * nj + j, 0))
    dc_next = pl.BlockSpec((CONV_HALO, d), lambda b, j: (jnp.minimum((b * nj + j + 1) * per, last_blk), 0))
    hbm = pl.BlockSpec(memory_space=pl.ANY)
    return pl.pallas_call(
        body, name=name, grid=(bl, nj),
        in_specs=[hbm, dc_main, dc_next, main_a, main_g, halo_a, halo_g, _const_spec((CONV_HALO, d)), dc_main, hbm],
        out_specs=[pl.BlockSpec((tt, 2 * d), lambda b, j: (b * nj + j, 0)), _const_spec((CONV_HALO, d)), _const_spec((1, d)),
                   _const_spec((d, 2 * d))],
        out_shape=[jax.ShapeDtypeStruct(dp.shape, dp.dtype), jax.ShapeDtypeStruct((CONV_HALO, d), F32),
                   jax.ShapeDtypeStruct((1, d), F32), jax.ShapeDtypeStruct(dw_in.shape, dw_in.dtype)],
        scratch_shapes=[pltpu.VMEM((SUBLANES, tt + CONV_HALO, d), F32), pltpu.VMEM((SUBLANES, tt + CONV_HALO, d), F32),
                        pltpu.VMEM((tt, d), F32), pltpu.VMEM((CONV_HALO, SUBLANES, d), F32), pltpu.VMEM((d, 2 * d), F32)],
        input_output_aliases={0: 0, 9: 3},
        compiler_params=_cparams(("arbitrary", "arbitrary")))(dp, dc, dc, p, p, p, p, conv_w, h1, dw_in)


def _sgu_stats(bv):
    gv = _gelu(bv)
    mu = jnp.mean(gv, axis=-1, keepdims=True)
    dv = gv - mu
    rstd = lax.rsqrt(jnp.mean(dv * dv, axis=-1, keepdims=True) + LN_EPS)
    return dv * rstd, rstd


def _sgu_fwd(p, wm, bias, ln_g, ln_b, *, name):
    t = p.shape[0]
    d = ln_g.shape[1]
    tt = SGU_TILE
    gd = d // SGU_GROUPS

    def body(u_ref, v_ref, wm_ref, bias_ref, lg_ref, lb_ref, sg_ref, vn_ref):
        u = _gelu(u_ref[...].astype(F32))
        vhat, _ = _sgu_stats(v_ref[...].astype(F32))
        vb = (vhat * lg_ref[...] + lb_ref[...]).astype(BF16)
        vn_ref[...] = vb
        for ci in range(tt // SGU_CHUNK):
            rows = slice(ci * SGU_CHUNK, (ci + 1) * SGU_CHUNK)
            for g in range(SGU_GROUPS):
                gs = slice(g * gd, (g + 1) * gd)
                z = _dot(wm_ref[g], vb[rows, gs], _NN) + bias_ref[g]
                sg_ref[rows, gs] = (u[rows, gs] * z).astype(BF16)

    rs = _row_spec(tt, d)
    return pl.pallas_call(
        body, name=name, grid=(t // tt,),
        in_specs=[_row_spec(tt, d, 2), _row_spec(tt, d, 3), _const_spec(wm.shape), _const_spec(bias.shape),
                  _const_spec((1, d)), _const_spec((1, d))],
        out_specs=[rs, rs], out_shape=[jax.ShapeDtypeStruct((t, d), BF16), jax.ShapeDtypeStruct((t, d), BF16)],
        compiler_params=_cparams(("parallel",)))(p, p, wm, bias, ln_g, ln_b)


def _sgu_bwd(dp, dy_b, w_out, p, vn, wm, wmt, bias, ln_g, *, name):
    t = p.shape[0]
    d = ln_g.shape[1]
    tt = SGU_TILE
    ck = SGU_CHUNK
    gd = d // SGU_GROUPS
    nsteps = t // tt

    def body(dp_in, dyb_ref, wout_ref, u_ref, v_ref, vn_ref, wm_ref, wmt_ref, bias_ref, lg_ref,
             dp_ref, dw_ref, dbs_ref, dlg_ref, dlb_ref, dz_acc):
        del dp_in
        i = pl.program_id(0)

        @pl.when(i == 0)
        def _():
            dw_ref[...] = jnp.zeros_like(dw_ref)
            dlg_ref[...] = jnp.zeros_like(dlg_ref)
            dlb_ref[...] = jnp.zeros_like(dlb_ref)
            dz_acc[...] = jnp.zeros_like(dz_acc)

        bu = u_ref[...].astype(F32)
        bv = v_ref[...].astype(F32)
        u = _gelu(bu)
        vhat, rstd = _sgu_stats(bv)
        vb = vn_ref[...]
        dsg = _dot(dyb_ref[...], wout_ref[...], _NT)
        row = lax.broadcasted_iota(jnp.int32, (ck, ck), 0)
        col = lax.broadcasted_iota(jnp.int32, (ck, ck), 1)
        causal = col <= row
        du_rows, dv_rows = [], []
        for ci in range(tt // ck):
            rows = slice(ci * ck, (ci + 1) * ck)
            du_parts, dv_parts = [], []
            for g in range(SGU_GROUPS):
                gs = slice(g * gd, (g + 1) * gd)
                z = _dot(wm_ref[g], vb[rows, gs], _NN) + bias_ref[g]
                du_parts.append(dsg[rows, gs] * z)
                dz = dsg[rows, gs] * u[rows, gs]
                dz_acc[:, gs] += dz
                dzb = dz.astype(BF16)
                dw_ref[g] += jnp.where(causal, _dot(dzb, vb[rows, gs], _NT), 0.0)
                dv_parts.append(_dot(wmt_ref[g], dzb, _NN))
            du_rows.append(jnp.concatenate(du_parts, axis=1))
            dv_rows.append(jnp.concatenate(dv_parts, axis=1))
        du = jnp.concatenate(du_rows, axis=0)
        dv = jnp.concatenate(dv_rows, axis=0)
        dp_ref[:, 0:d] = (du * _gelu_grad(bu)).astype(BF16)
        dlb_ref[...] += jnp.sum(dv, axis=0, keepdims=True)
        dlg_ref[...] += jnp.sum(dv * vhat, axis=0, keepdims=True)
        dvh = dv * lg_ref[...]
        dgv = rstd * (dvh - jnp.mean(dvh, axis=-1, keepdims=True) - vhat * jnp.mean(dvh * vhat, axis=-1, keepdims=True))
        dp_ref[:, d:2 * d] = (dgv * _gelu_grad(bv)).astype(BF16)

        @pl.when(i == nsteps - 1)
        def _():
            ones = jnp.ones((8, gd), F32)
            for g in range(SGU_GROUPS):
                gs = slice(g * gd, (g + 1) * gd)
                tot = lax.dot_general(ones, dz_acc[:, gs], (_NT, ((), ())), preferred_element_type=F32,
                                      precision=lax.Precision.HIGHEST)
                dbs_ref[g:g + 1, :] = tot[0:1, :]

    rs = _row_spec(tt, d)
    c1 = _const_spec((1, d))
    return pl.pallas_call(
        body, name=name, grid=(nsteps,),
        in_specs=[pl.BlockSpec(memory_space=pl.ANY), rs, _const_spec(w_out.shape), _row_spec(tt, d, 2), _row_spec(tt, d, 3),
                  rs, _const_spec(wm.shape), _const_spec(wmt.shape), _const_spec(bias.shape), c1],
        out_specs=[pl.BlockSpec((tt, 2 * d), lambda i: (i, 1)), _const_spec(wm.shape), _const_spec((SGU_GROUPS, ck)), c1, c1],
        out_shape=[jax.ShapeDtypeStruct(dp.shape, dp.dtype), jax.ShapeDtypeStruct(wm.shape, F32),
                   jax.ShapeDtypeStruct((SGU_GROUPS, ck), F32), jax.ShapeDtypeStruct((1, d), F32),
                   jax.ShapeDtypeStruct((1, d), F32)],
        scratch_shapes=[pltpu.VMEM((ck, d), F32)],
        input_output_aliases={0: 0},
        compiler_params=_cparams(("arbitrary",)))(dp, dy_b, w_out, p, p, vn, wm, wmt, bias, ln_g)


def _softmax_rows(s):
    e = jnp.exp(s - jnp.max(s, axis=-1, keepdims=True))
    return e / jnp.sum(e, axis=-1, keepdims=True)


def _attn_fwd(q, kv, x1, w_xo, gain, *, bl, s, name):
    t, d = q.shape
    mlen = kv.shape[0] // bl
    hd = d // HEADS
    tq = min(ATTN_TILE, s)
    nq = s // tq
    scale = hd ** -0.5

    def body(q_ref, kv_ref, x1_ref, w_ref, g_ref, o_ref, x2_ref, h_ref):
        for h in range(HEADS):
            hs = slice(h * hd, (h + 1) * hd)
            vs = slice(d + h * hd, d + (h + 1) * hd)
            pr = _softmax_rows(_dot(q_ref[:, hs], kv_ref[:, hs], _NT) * scale)
            o_ref[:, hs] = _dot(pr.astype(BF16), kv_ref[:, vs], _NN).astype(BF16)
        x2 = x1_ref[...] + _dot(o_ref[...], w_ref[...], _NN)
        x2_ref[...] = x2
        h_ref[...] = _rms_apply(x2, g_ref[...]).astype(BF16)

    qs = pl.BlockSpec((tq, d), lambda b, j: (b * nq + j, 0))
    return pl.pallas_call(
        body, name=name, grid=(bl, nq),
        in_specs=[qs, pl.BlockSpec((mlen, 2 * d), lambda b, j: (b, 0)), qs, _const_spec(w_xo.shape), _const_spec((1, d))],
        out_specs=[qs, qs, qs],
        out_shape=[jax.ShapeDtypeStruct((t, d), BF16), jax.ShapeDtypeStruct((t, d), F32), jax.ShapeDtypeStruct((t, d), BF16)],
        compiler_params=_cparams(("parallel", "parallel")))(q, kv, x1, w_xo, gain)


def _attn_bwd(q, kv, do, *, bl, s, name):
    t, d = q.shape
    mlen = kv.shape[0] // bl
    hd = d // HEADS
    tq = min(ATTN_TILE, s)
    nq = s // tq
    scale = hd ** -0.5

    def body(q_ref, kv_ref, do_ref, dq_ref, dkv_ref):
        @pl.when(pl.program_id(1) == 0)
        def _():
            dkv_ref[...] = jnp.zeros_like(dkv_ref)

        for h in range(HEADS):
            hs = slice(h * hd, (h + 1) * hd)
            vs = slice(d + h * hd, d + (h + 1) * hd)
            qh, kh, vh, doh = q_ref[:, hs], kv_ref[:, hs], kv_ref[:, vs], do_ref[:, hs]
            pr = _softmax_rows(_dot(qh, kh, _NT) * scale)
            dpr = _dot(doh, vh, _NT)
            dkv_ref[:, vs] += _dot(pr.astype(BF16), doh, _TN)
            ds = (pr * (dpr - jnp.sum(dpr * pr, axis=-1, keepdims=True)) * scale).astype(BF16)
            dq_ref[:, hs] = _dot(ds, kh, _NN).astype(BF16)
            dkv_ref[:, hs] += _dot(ds, qh, _TN)

    qs = pl.BlockSpec((tq, d), lambda b, j: (b * nq + j, 0))
    ks = pl.BlockSpec((mlen, 2 * d), lambda b, j: (b, 0))
    return pl.pallas_call(
        body, name=name, grid=(bl, nq), in_specs=[qs, ks, qs], out_specs=[qs, ks],
        out_shape=[jax.ShapeDtypeStruct((t, d), BF16), jax.ShapeDtypeStruct(kv.shape, F32)],
        compiler_params=_cparams(("parallel", "arbitrary")))(q, kv, do)


def _mesh_pos():
    return lax.axis_index("x"), lax.axis_index("y"), lax.axis_index("c")


def _all_gather(arrs, *, name):
    n = len(arrs)
    hbm = pl.BlockSpec(memory_space=pl.ANY)

    def body(*refs):
        ins, outs = refs[:n], refs[n:2 * n]
        send_sems, recv_sems, loc_sems = refs[2 * n:]
        x, y, c = _mesh_pos()
        me, sib = (x, y, c), (x, y, 1 - c)
        chips = [(1 - x, y), (x, 1 - y), (1 - x, 1 - y)]

        def idx(dev):
            return 4 * dev[0] + 2 * dev[1] + dev[2]

        def copy(w, k, block, to, from_input=False):
            return pltpu.make_async_remote_copy(
                src_ref=ins[w] if from_input else outs[w].at[idx(block)], dst_ref=outs[w].at[idx(block)],
                send_sem=send_sems.at[w, k], recv_sem=recv_sems.at[w, k], device_id=to, device_id_type=MESH_ID)

        own = [pltpu.make_async_copy(ins[w], outs[w].at[idx(me)], loc_sems.at[w]) for w in range(n)]
        for cp in own:
            cp.start()
        first = []
        for w in range(n):
            first.append(copy(w, 0, me, sib, True))
            first += [copy(w, 1 + j, me, (*chip, c), True) for j, chip in enumerate(chips)]
        for cp in first:
            cp.start()
        passed = []
        for j, chip in enumerate(chips):
            for w in range(n):
                copy(w, 1 + j, (*chip, c), me).wait_recv()
                fwd = copy(w, 4 + j, (*chip, c), sib)
                fwd.start()
                passed.append(fwd)
        for w in range(n):
            copy(w, 0, sib, me).wait_recv()
            for j, chip in enumerate(chips):
                copy(w, 4 + j, (*chip, 1 - c), me).wait_recv()
        for cp in first + passed:
            cp.wait_send()
        for cp in own:
            cp.wait()

    return pl.pallas_call(
        body, name=name, in_specs=[hbm] * n, out_specs=[hbm] * n,
        out_shape=[jax.ShapeDtypeStruct((N_DEV, *a.shape), a.dtype) for a in arrs],
        scratch_shapes=[pltpu.SemaphoreType.DMA((n, 7)), pltpu.SemaphoreType.DMA((n, 7)), pltpu.SemaphoreType.DMA((n,))],
    )(*arrs)


_HBM = pl.BlockSpec(memory_space=pltpu.HBM)
_SEM = pl.BlockSpec(memory_space=pltpu.SEMAPHORE)
_ANY = pl.BlockSpec(memory_space=pl.ANY)
_EFFECT = pltpu.SideEffectType.DATAFLOW_SIDE_EFFECTING
N_PEERS = N_DEV - 1


def _related(pos, r):
    x, y, c = pos
    return (1 - x if r & 4 else x, 1 - y if r & 2 else y, 1 - c if r & 1 else c)


def _dev_index(dev):
    return 4 * dev[0] + 2 * dev[1] + dev[2]


def _in_hbm(a):
    return pltpu.with_memory_space_constraint(a, pltpu.HBM)


def _split_copies(kind, srcs, lands, send_sems, recv_sems):
    pos = _mesh_pos()
    me = _dev_index(pos)
    out = []
    for w in range(len(srcs)):
        for r in range(1, N_DEV):
            peer = _related(pos, r)
            if kind == "gather":
                src, dst_here, dst_there = srcs[w], lands[w].at[_dev_index(peer)], lands[w].at[me]
            elif srcs[w].ndim == 2:
                cb = lands[w].shape[2]
                src = srcs[w].at[:, pl.ds(pl.multiple_of(_dev_index(peer) * cb, LANES), cb)]
                dst_here = dst_there = lands[w].at[r - 1]
            else:
                src, dst_here, dst_there = srcs[w].at[_dev_index(peer)], lands[w].at[r - 1], lands[w].at[r - 1]
            out.append((src, dst_here, dst_there, send_sems.at[w * N_PEERS + r - 1], recv_sems.at[w * N_PEERS + r - 1], peer))
    return out


def _copy_start(kind, srcs, land_shapes, *, name, after=None):
    n = len(srcs)
    n_after = 0 if after is None else 1

    def body(*refs):
        src_refs, land_refs = refs[:n], refs[n:2 * n]
        send_sems, recv_sems = refs[2 * n + n_after], refs[2 * n + n_after + 1]
        token = refs[-1]
        for src, _, dst, ssem, rsem, peer in _split_copies(kind, src_refs, land_refs, send_sems, recv_sems):
            pltpu.make_async_remote_copy(src_ref=src, dst_ref=dst, send_sem=ssem, recv_sem=rsem, device_id=peer,
                                         device_id_type=MESH_ID).start()
        token[...] = jnp.zeros_like(token)

    lands = [_in_hbm(lax.empty(shape, s.dtype)) for s, shape in zip(srcs, land_shapes)]
    res = pl.pallas_call(
        body, name=name,
        out_shape=(pltpu.SemaphoreType.DMA((n * N_PEERS,)), pltpu.SemaphoreType.DMA((n * N_PEERS,)),
                   *[pltpu.HBM(s.shape, s.dtype) for s in srcs], *[pltpu.HBM(l.shape, l.dtype) for l in lands],
                   jax.ShapeDtypeStruct((8, 128), F32)),
        in_specs=[_HBM] * (2 * n) + [_ANY] * n_after,
        out_specs=(_SEM, _SEM, *[_HBM] * (2 * n), pl.BlockSpec(memory_space=pltpu.VMEM)),
        input_output_aliases={i: 2 + i for i in range(2 * n)},
        compiler_params=pltpu.CompilerParams(has_side_effects=_EFFECT),
    )(*[_in_hbm(s) for s in srcs], *lands, *([] if after is None else [after]))
    return res[0], res[1], list(res[2:2 + n]), list(res[2 + n:2 + 2 * n]), res[-1]


def _copy_wait(kind, send_sems, recv_sems, srcs, lands, after, *, name):
    n = len(srcs)

    def body(*refs):
        src_refs, land_refs = refs[:n], refs[n:2 * n]
        ssems, rsems = refs[2 * n], refs[2 * n + 1]
        for src, dst, _, ssem, rsem, peer in _split_copies(kind, src_refs, land_refs, ssems, rsems):
            cp = pltpu.make_async_remote_copy(src_ref=src, dst_ref=dst, send_sem=ssem, recv_sem=rsem, device_id=peer,
                                              device_id_type=MESH_ID)
            cp.wait_send()
            cp.wait_recv()

    res = pl.pallas_call(
        body, name=name,
        out_shape=(*[pltpu.HBM(s.shape, s.dtype) for s in srcs], *[pltpu.HBM(l.shape, l.dtype) for l in lands]),
        in_specs=[_HBM] * (2 * n) + [_SEM, _SEM, _ANY], out_specs=tuple([_HBM] * (2 * n)),
        input_output_aliases={i: i for i in range(2 * n)},
        compiler_params=pltpu.CompilerParams(has_side_effects=_EFFECT),
    )(*srcs, *lands, send_sems, recv_sems, after)
    return list(res[:n]), list(res[n:])


def _row_tile(rows):
    return max(tr for tr in range(16, min(rows, 512) + 1, 16) if rows % tr == 0)


def _adamw_math(w, g, m, v):
    m2 = ADAM_B1 * m + (1.0 - ADAM_B1) * g
    v2 = ADAM_B2 * v + (1.0 - ADAM_B2) * (g * g)
    m_hat = m2 / (1.0 - ADAM_B1 ** ADAM_STEP)
    v_hat = v2 / (1.0 - ADAM_B2 ** ADAM_STEP)
    delta = -ADAM_LR * (m_hat / (jnp.sqrt(v_hat) + ADAM_EPS) + ADAM_WD * w)
    return delta, m2, v2


def _adamw_shard(partials, landed, dev, w, m, v, *, name):
    r, c = w.shape
    tr = _row_tile(r)

    def body(dev_ref, p_ref, l_ref, w_ref, m_ref, v_ref, g_out, d_out, m_out, v_out):
        del dev_ref
        g = p_ref[...].astype(F32)
        for k in range(N_PEERS):
            g = g + l_ref[k].astype(F32)
        delta, m2, v2 = _adamw_math(w_ref[...], g, m_ref[...], v_ref[...])
        g_out[...] = g
        d_out[...] = delta
        m_out[...] = m2
        v_out[...] = v2

    blk = pl.BlockSpec((tr, c), lambda i, dev_ref: (i, 0))
    if partials.ndim == 2:
        own = pl.BlockSpec((tr, c), lambda i, dev_ref: (i, dev_ref[0]))
    else:
        own = pl.BlockSpec((None, tr, c), lambda i, dev_ref: (dev_ref[0], i, 0))
    gs = pltpu.PrefetchScalarGridSpec(
        num_scalar_prefetch=1, grid=(r // tr,),
        in_specs=[own, pl.BlockSpec((N_PEERS, tr, c), lambda i, dev_ref: (0, i, 0)), blk, blk, blk],
        out_specs=[blk] * 4)
    return pl.pallas_call(
        body, name=name, grid_spec=gs, out_shape=[jax.ShapeDtypeStruct((r, c), F32)] * 4,
        compiler_params=_cparams(("parallel",)))(dev, partials, landed, w, m, v)


def _sum_devices(p_ref, *idx):
    g = p_ref[(0, *idx)]
    for k in range(1, N_DEV):
        g = g + p_ref[(k, *idx)]
    return g


def _adamw_replicated(parts, states, loss_row, *, name):
    n_parts, n_par = len(parts), len(states)
    n_vec = n_par - (n_parts - 1)

    def body(*refs):
        part_refs, st = refs[:n_parts], refs[n_parts:n_parts + 3 * n_par]
        outs = refs[n_parts + 3 * n_par:]
        outs[0][...] = _sum_devices(part_refs[0], slice(loss_row, loss_row + 1), slice(0, 1))
        for i in range(n_par):
            g = _sum_devices(part_refs[0], slice(i, i + 1)) if i < n_vec else _sum_devices(part_refs[1 + i - n_vec])
            delta, m2, v2 = _adamw_math(st[3 * i][...], g, st[3 * i + 1][...], st[3 * i + 2][...])
            for o, val in zip(outs[1 + 4 * i:5 + 4 * i], (g, delta, m2, v2)):
                o[...] = val

    flat = [a for wmv in states for a in wmv]
    return pl.pallas_call(
        body, name=name,
        out_shape=[jax.ShapeDtypeStruct((1, 1), F32)] + [jax.ShapeDtypeStruct(w.shape, F32) for w, _, _ in states for _ in range(4)],
        compiler_params=pltpu.CompilerParams(vmem_limit_bytes=VMEM_LIMIT))(*parts, *flat)


def _adamw_column_shards(parts, dev, states, row0s, *, name):
    _, rows, _ = parts.shape
    c = states[0][0].shape[1]

    def body(dev_ref, p_ref, *refs):
        del dev_ref
        st, outs = refs[:3 * len(states)], refs[3 * len(states):]
        for j, r0 in enumerate(row0s):
            w_ref = st[3 * j]
            g = _sum_devices(p_ref, slice(r0, r0 + w_ref.shape[0]))
            delta, m2, v2 = _adamw_math(w_ref[...], g, st[3 * j + 1][...], st[3 * j + 2][...])
            for o, val in zip(outs[4 * j:4 * j + 4], (g, delta, m2, v2)):
                o[...] = val

    whole = lambda a: pl.BlockSpec(a.shape, lambda i, dev_ref: (0, 0))
    flat = [a for wmv in states for a in wmv]
    outs = [w for w, _, _ in states for _ in range(4)]
    gs = pltpu.PrefetchScalarGridSpec(
        num_scalar_prefetch=1, grid=(1,),
        in_specs=[pl.BlockSpec((N_DEV, rows, c), lambda i, dev_ref: (0, 0, dev_ref[0]))] + [whole(a) for a in flat],
        out_specs=[whole(a) for a in outs])
    return pl.pallas_call(
        body, name=name, grid_spec=gs, out_shape=[jax.ShapeDtypeStruct(a.shape, F32) for a in outs],
        compiler_params=_cparams(("arbitrary",)))(dev, parts, *flat)


def _pad_rows(a, rows):
    return jnp.pad(a, ((0, rows - a.shape[0]), (0, 0)))


def _unblock_cols(g):
    return jnp.transpose(g, (1, 0, 2)).reshape(g.shape[1], N_DEV * g.shape[2])


def kernel(x, mem, norm_mix, w_in, b_gate, conv_w, conv_b, conv_ln_g, conv_ln_b, w_conv_out, sgu_ln_g, sgu_ln_b, sgu_w, sgu_b, w_sgu_out, w_mix_out, norm_xattn, norm_mem, w_q, w_kv, w_xo, norm_ffn, w_gu, w_down, norm_final, loss_target, m_norm_mix, m_w_in, m_b_gate, m_conv_w, m_conv_b, m_conv_ln_g, m_conv_ln_b, m_w_conv_out, m_sgu_ln_g, m_sgu_ln_b, m_sgu_w, m_sgu_b, m_w_sgu_out, m_w_mix_out, m_norm_xattn, m_norm_mem, m_w_q, m_w_kv, m_w_xo, m_norm_ffn, m_w_gu, m_w_down, m_norm_final, v_norm_mix, v_w_in, v_b_gate, v_conv_w, v_conv_b, v_conv_ln_g, v_conv_ln_b, v_w_conv_out, v_sgu_ln_g, v_sgu_ln_b, v_sgu_w, v_sgu_b, v_w_sgu_out, v_w_mix_out, v_norm_xattn, v_norm_mem, v_w_q, v_w_kv, v_w_xo, v_norm_ffn, v_w_gu, v_w_down, v_norm_final):
    given = dict(locals())
    bl, s, d = x.shape
    t = bl * s
    xf = x.reshape(t, d)
    tgt = loss_target.reshape(t, d)
    memf = mem.reshape(bl * mem.shape[1], d)
    cx, cy, cc = lax.axis_index("x"), lax.axis_index("y"), lax.axis_index("c")
    dev = 4 * cx + 2 * cy + cc
    dev_id = dev.astype(jnp.int32).reshape(1)
    col_sharded = ["w_in", "w_kv"]
    transposed = ["w_gu"]

    def shard_of(name, prefix=""):
        a = given[prefix + name][0]
        return jnp.transpose(a) if name in transposed else a

    def full_weight(name, blocks):
        return _unblock_cols(blocks) if name in col_sharded else blocks.reshape(N_DEV * blocks.shape[1], blocks.shape[2])

    g_bg, g_cw = _all_gather([_pad_rows(b_gate[0], 8), _pad_rows(conv_w[0], CONV_HALO)], name="gather_small_params")
    h1, p, w_in_blocks = _in_proj_gather(xf, norm_mix + g_bg[0, 7:8, 0:1], w_in[0].astype(BF16), name="in_proj")
    early = ["w_conv_out", "w_sgu_out", "w_mix_out", "w_q", "w_kv", "w_xo"]
    late = ["w_gu", "w_down"]
    shards = {n: shard_of(n).astype(BF16) for n in early + late}
    started = {}
    for grp, names in (("early", early), ("late", late)):
        srcs = [shards[n] for n in names]
        started[grp] = _copy_start("gather", srcs, [(N_DEV, *a.shape) for a in srcs], name=f"gather_{grp}_start", after=p)
    token = started["early"][4][0:1, 0:1] + started["late"][4][0:1, 0:1]
    wfull = {}
    bg_full = _unblock_cols(g_bg)
    cw_full = _unblock_cols(g_cw)

    def finish_gather(grp, names, after):
        ssem, rsem, srcs, lands, _ = started[grp]
        _, lands = _copy_wait("gather", ssem, rsem, srcs, lands, after, name=f"gather_{grp}_wait")
        for n, land in zip(names, lands):
            wfull[n] = full_weight(n, lax.dynamic_update_index_in_dim(land, shards[n], dev, 0))

    tri = jnp.tril(jnp.ones((SGU_CHUNK, SGU_CHUNK), bool))
    wm32 = jnp.where(tri[None], sgu_w[0], 0.0)
    wm = wm32.astype(BF16)
    wmt = jnp.transpose(wm32, (0, 2, 1)).astype(BF16)
    sgu_bias = jnp.broadcast_to(sgu_b[0][:, :, None], (SGU_GROUPS, SGU_CHUNK, d // SGU_GROUPS))

    c_conv, a_act = _conv_fwd(p, cw_full, conv_b + token, conv_ln_g, conv_ln_b, bl=bl, s=s, name="conv_fwd")
    sg, vn = _sgu_fwd(p, wm, sgu_bias, sgu_ln_g, sgu_ln_b + token, name="sgu_fwd")
    finish_gather("early", early, a_act[0:16, 0:128] + sg[0:16, 0:128])
    y_a = _matmul(a_act, wfull["w_conv_out"], mode="nn", out_dtype=BF16, name="mm_conv_out", tm=1024, tn=1024, tk=1024)
    y_b = _matmul(sg, wfull["w_sgu_out"], mode="nn", out_dtype=BF16, name="mm_sgu_out", tm=1024, tn=1024, tk=1024)
    merged, x1, h2, q = _mix_out(p, y_a, y_b, bg_full, xf, wfull["w_mix_out"], norm_xattn, wfull["w_q"], name="mix_out")
    mem_n = _rms_fwd(memf, norm_mem, name="rms_mem")
    kv = _matmul(mem_n, wfull["w_kv"], mode="nn", out_dtype=BF16, name="mm_kv", tm=1024, tn=1024, tk=1024)
    o, x2, h3 = _attn_fwd(q, kv, x1, wfull["w_xo"], norm_ffn, bl=bl, s=s, name="attn_fwd")
    finish_gather("late", late, h3)
    gu, act, dx3, loss_part, d_norm_final = _ffn_fwd(h3, x2, tgt, wfull["w_gu"], wfull["w_down"],
                                                     norm_final.reshape(1, d), name="ffn_fwd")

    grads = {}
    sent = []

    def send_grads(names, tag, after=None):
        blocks, land_shapes = [], []
        for n in names:
            g = grads[n]
            if g.ndim == 2 and n in col_sharded:
                land_shapes.append((N_PEERS, g.shape[0], g.shape[1] // N_DEV))
            else:
                if g.ndim == 2:
                    g = g.reshape(N_DEV, -1, g.shape[1])
                land_shapes.append((N_PEERS, *g.shape[1:]))
            blocks.append(g)
        ssem, rsem, srcs, lands, tok = _copy_start("scatter", blocks, land_shapes, name=f"grads_{tag}_start", after=after)
        sent.append((names, ssem, rsem, srcs, lands))
        return tok[0:1, 0:1]

    dgu, dx2, do, d_norm_ffn = _ffn_bwd(dx3, gu, x2, wfull["w_down"], wfull["w_gu"], norm_ffn, wfull["w_xo"], name="ffn_bwd")
    grads["w_down"] = _matmul(act, dx3, mode="tn", out_dtype=BF16, name="mm_dw_down", tm=1408, tn=1024, tk=2048)
    grads["w_gu"] = _matmul(dgu, h3, mode="tn", out_dtype=BF16, name="mm_dw_gu", tm=1408, tn=1024, tk=2048)
    tok = send_grads(["w_down", "w_gu"], "ffn")
    grads["w_xo"] = _matmul(o, dx2, mode="tn", out_dtype=BF16, name="mm_dw_xo", tm=1024, tn=1024, tk=2048)
    dq, dkv = _attn_bwd(q, kv, do, bl=bl, s=s, name="attn_bwd")
    grads["w_kv"] = _matmul(mem_n, dkv, mode="tn", out_dtype=BF16, name="mm_dw_kv", tm=1024, tn=256, tk=1024,
                            col_blocks=N_DEV)
    tok2 = send_grads(["w_xo", "w_kv"], "attn")
    dmem_n = _matmul(dkv, wfull["w_kv"], mode="nt", out_dtype=F32, name="mm_d_mem", tm=512, tn=1024, tk=2048)
    d_norm_mem = _rms_bwd(None, dmem_n, memf, norm_mem, name="rms_mem_bwd", need_dx=False)
    dx1, d_norm_xattn, dw_q = _proj_rms_bwd(dq, dx2, x1, wfull["w_q"], norm_xattn + (tok + tok2), name="q_rms_bwd", h=h2)
    dp, dy_a, dy_b, d_b_gate, dw_mix, dw_in_gates = _gates_bwd_fused(dx1, p, y_a, y_b, bg_full, wfull["w_mix_out"],
                                                                    merged, h1, name="gates_bwd")
    grads["w_q"] = dw_q.astype(BF16)
    grads["w_mix_out"] = dw_mix.astype(BF16)
    grads["w_sgu_out"] = _matmul(sg, dy_b, mode="tn", out_dtype=BF16, name="mm_dw_sgu", tm=1024, tn=1024, tk=2048)
    dc, d_conv_ln_g, d_conv_ln_b, dw_conv = _conv_ln_bwd_fused(dy_a, c_conv, a_act, wfull["w_conv_out"], conv_ln_g,
                                                               conv_ln_b, name="conv_ln_bwd")
    grads["w_conv_out"] = dw_conv.astype(BF16)
    tok = send_grads(["w_q", "w_mix_out", "w_sgu_out", "w_conv_out"], "mixer")
    dp, d_sgu_w, d_sgu_b, d_sgu_ln_g, d_sgu_ln_b = _sgu_bwd(dp, dy_b, wfull["w_sgu_out"], p, vn, wm, wmt, sgu_bias,
                                                             sgu_ln_g + tok, name="sgu_bwd")
    dw_in = _matmul(h1, dp, mode="tn", out_dtype=BF16, name="mm_dw_in_sgu", tm=1024, tn=1024, tk=2048,
                    b_cols=(2 * d, 2 * d), out_into=(dw_in_gates, 2 * d))
    dp, d_conv_w, d_conv_b, dw_in = _conv_bwd(dp, dc, p, cw_full, h1, dw_in, bl=bl, s=s, name="conv_bwd")
    grads["w_in"] = dw_in
    tok = send_grads(["w_in"], "in")
    grad_x, d_norm_mix = _proj_rms_bwd(dp, dx1, xf, w_in_blocks, norm_mix + tok, name="in_proj_bwd")
    out = {}

    vec_names = ["norm_mix", "conv_b", "conv_ln_g", "conv_ln_b", "sgu_ln_g", "sgu_ln_b", "norm_xattn", "norm_mem",
                 "norm_ffn", "norm_final"]
    vec_grads = [d_norm_mix, d_conv_b, d_conv_ln_g, d_conv_ln_b, d_sgu_ln_g, d_sgu_ln_b, d_norm_xattn, d_norm_mem,
                 d_norm_ffn, d_norm_final]
    n_vec = len(vec_names)
    small_vec = jnp.concatenate([g.reshape(1, d) for g in vec_grads]
                                + [jnp.broadcast_to(loss_part, (1, d)), jnp.zeros((16 - n_vec - 1, d), F32)], axis=0)
    small_cols = jnp.concatenate([d_b_gate, d_conv_w], axis=0)
    parts_vec, parts_sb, parts_sw, parts_cols = _all_gather([small_vec, d_sgu_b, d_sgu_w, small_cols],
                                                            name="gather_small_grads")
    rep_names = vec_names + ["sgu_b", "sgu_w"]
    rep_shapes = [(1, d)] * n_vec + [d_sgu_b.shape, d_sgu_w.shape]
    states = [tuple(given[pre + n].reshape(shape) for pre in ("", "m_", "v_")) for n, shape in zip(rep_names, rep_shapes)]
    res_rep = _adamw_replicated([parts_vec, parts_sb, parts_sw], states, n_vec, name="adamw_small")
    for i, n in enumerate(rep_names):
        out[n] = [r.reshape(given[n].shape) for r in res_rep[1 + 4 * i:5 + 4 * i]]
    res_cols = _adamw_column_shards(parts_cols, dev_id, [(b_gate[0], m_b_gate[0], v_b_gate[0]),
                                                        (conv_w[0], m_conv_w[0], v_conv_w[0])], (0, 8),
                                    name="adamw_small_cols")
    out["b_gate"] = [r[None] for r in res_cols[0:4]]
    out["conv_w"] = [r[None] for r in res_cols[4:8]]

    done = res_rep[1]
    for names, ssem, rsem, srcs, lands in sent:
        srcs, lands = _copy_wait("scatter", ssem, rsem, srcs, lands, done, name=f"grads_{names[0]}_wait")
        for n, partials, landed in zip(names, srcs, lands):
            res = _adamw_shard(partials, landed, dev_id, shard_of(n), shard_of(n, "m_"), shard_of(n, "v_"),
                               name=f"adamw_{n}")
            done = res[0]
            out[n] = [(jnp.transpose(r) if n in transposed else r)[None] for r in res]

    order = ["norm_mix", "w_in", "b_gate", "conv_w", "conv_b", "conv_ln_g", "conv_ln_b", "w_conv_out", "sgu_ln_g",
             "sgu_ln_b", "sgu_w", "sgu_b", "w_sgu_out", "w_mix_out", "norm_xattn", "norm_mem", "w_q", "w_kv", "w_xo",
             "norm_ffn", "w_gu", "w_down", "norm_final"]
    loss = res_rep[0][0, 0]
    return (loss, grad_x.reshape(x.shape), *[out[n][0] for n in order], *[out[n][1] for n in order],
            *[out[n][2] for n in order], *[out[n][3] for n in order])
```

```python
import functools

import jax
import jax.numpy as jnp
from jax import lax
from jax.experimental import pallas as pl
from jax.experimental.pallas import tpu as pltpu

F32 = jnp.float32
BF16 = jnp.bfloat16
RMS_EPS = 1e-6
LN_EPS = 1e-5
CONV_WIDTH = 31
CONV_HALO = 32
CONV_ROWS = 128
CONV_COLS = 256
LANES = 128
SGU_CHUNK = 128
SGU_GROUPS = 8
SGU_TILE = 512
HEADS = 4
N_DEV = 8
ADAM_LR, ADAM_B1, ADAM_B2, ADAM_EPS, ADAM_WD, ADAM_STEP = 0.001, 0.9, 0.999, 1e-08, 0.01, 10
VMEM_LIMIT = 56 * 1024 * 1024
TOKEN_TILE = 256
ATTN_TILE = 1024
MESH_ID = pl.DeviceIdType.MESH

_GELU_K = 0.7978845608028654
_GELU_C = 0.044715


def _cparams(sem=None):
    return pltpu.CompilerParams(dimension_semantics=sem, vmem_limit_bytes=VMEM_LIMIT)


def _sigmoid(v):
    return 0.5 * jnp.tanh(0.5 * v) + 0.5


def _gelu(v):
    return 0.5 * v * (1.0 + jnp.tanh(_GELU_K * (v + _GELU_C * v * v * v)))


def _gelu_grad(v):
    th = jnp.tanh(_GELU_K * (v + _GELU_C * v * v * v))
    return 0.5 * (1.0 + th) + 0.5 * v * (1.0 - th * th) * _GELU_K * (1.0 + 3.0 * _GELU_C * v * v)


def _dot(a, b, dims):
    return lax.dot_general(a, b, (dims, ((), ())), preferred_element_type=F32)


_NN = ((1,), (0,))
_NT = ((1,), (1,))
_TN = ((0,), (0,))


def _matmul(a, b, *, mode, out_dtype, name, tm=512, tn=512, tk=512, chunk=None, residual=None, rms_gain=None,
            col_blocks=None, b_cols=None, out_into=None):
    if mode == "nn":
        (m, k), (_, n) = a.shape, b.shape
    elif mode == "nt":
        (m, k), (n, _) = a.shape, b.shape
    else:
        (k, m), (_, n) = a.shape, b.shape
    b_first = 0
    if b_cols is not None:
        assert mode == "tn"
        b_first, n = b_cols
    tm, tn, tk = min(tm, m), min(tn, n), min(tk, k)
    assert b_first % tn == 0
    b_first //= tn
    assert m % tm == 0 and n % tn == 0 and k % tk == 0, (name, a.shape, b.shape, tm, tn, tk)
    nk = k // tk
    dims = {"nn": _NN, "nt": _NT, "tn": _TN}[mode]
    chunk = tn if chunk is None else min(chunk, tn)
    assert tn % chunk == 0
    if rms_gain is not None:
        assert tn == n and chunk == n

    def body(*refs):
        refs = list(refs)
        a_ref, b_ref = refs[:2]
        pos = 2
        r_ref = g_ref = None
        if residual is not None:
            r_ref = refs[pos]
            pos += 1
        if rms_gain is not None:
            g_ref = refs[pos]
            pos += 1
        if out_into is not None:
            pos += 1
        o_ref = refs[pos]
        pos += 1
        h_ref = None
        if rms_gain is not None:
            h_ref = refs[pos]
            pos += 1
        acc_ref = refs[pos] if nk > 1 else None
        av = a_ref[...].astype(BF16)
        for c0 in range(0, tn, chunk):
            cs = slice(c0, c0 + chunk)
            bv = (b_ref[cs, :] if mode == "nt" else b_ref[:, cs]).astype(BF16)
            part = _dot(av, bv, dims)

            def finish(res, cs=cs):
                if r_ref is not None:
                    res = res + r_ref[:, cs].astype(F32)
                o_ref[:, cs] = res.astype(out_dtype)
                if h_ref is not None:
                    r = lax.rsqrt(jnp.mean(res * res, axis=-1, keepdims=True) + RMS_EPS)
                    h_ref[...] = (res * r * g_ref[...]).astype(BF16)

            if nk == 1:
                finish(part)
            else:
                kk = pl.program_id(2)

                @pl.when(kk == 0)
                def _(part=part, cs=cs):
                    acc_ref[:, cs] = part

                @pl.when(kk > 0)
                def _(part=part, cs=cs):
                    acc_ref[:, cs] += part

                @pl.when(kk == nk - 1)
                def _(finish=finish, cs=cs):
                    finish(acc_ref[:, cs])

    resident = dict(pipeline_mode=pl.Buffered(1)) if (n == tn and nk == 1 and mode != "tn" and m > tm) else {}
    if mode == "nn":
        a_spec = pl.BlockSpec((tm, tk), lambda i, j, kk: (i, kk))
        b_spec = pl.BlockSpec((tk, tn), lambda i, j, kk: (kk, j), **resident)
    elif mode == "nt":
        a_spec = pl.BlockSpec((tm, tk), lambda i, j, kk: (i, kk))
        b_spec = pl.BlockSpec((tn, tk), lambda i, j, kk: (j, kk), **resident)
    else:
        a_spec = pl.BlockSpec((tk, tm), lambda i, j, kk: (kk, i))
        b_spec = pl.BlockSpec((tk, tn), lambda i, j, kk: (kk, j + b_first))
    o_spec = pl.BlockSpec((tm, tn), lambda i, j, kk: (i, j))
    in_specs, args = [a_spec, b_spec], [a, b]
    if residual is not None:
        in_specs.append(o_spec)
        args.append(residual)
    out_shape, out_specs = [jax.ShapeDtypeStruct((m, n), out_dtype)], [o_spec]
    if col_blocks is not None:
        assert residual is None and rms_gain is None and (n // col_blocks) % tn == 0
        per = n // col_blocks // tn
        out_shape = [jax.ShapeDtypeStruct((col_blocks, m, n // col_blocks), out_dtype)]
        out_specs = [pl.BlockSpec((None, tm, tn), lambda i, j, kk: (j // per, i, j % per))]
    if rms_gain is not None:
        in_specs.append(pl.BlockSpec((1, n), lambda i, j, kk: (0, 0)))
        args.append(rms_gain)
        out_shape.append(jax.ShapeDtypeStruct((m, n), BF16))
        out_specs.append(o_spec)
    aliases = {}
    if out_into is not None:
        target, first = out_into
        assert col_blocks is None and rms_gain is None and first % tn == 0 and target.dtype == out_dtype
        in_specs.append(pl.BlockSpec(memory_space=pl.ANY))
        args.append(target)
        aliases = {len(args) - 1: 0}
        out_shape = [jax.ShapeDtypeStruct(target.shape, target.dtype)]
        out_specs = [pl.BlockSpec((tm, tn), lambda i, j, kk: (i, j + first // tn))]
    res = pl.pallas_call(
        body, name=name, grid=(m // tm, n // tn, nk), in_specs=in_specs, out_specs=out_specs, out_shape=out_shape,
        scratch_shapes=[pltpu.VMEM((tm, tn), F32)] if nk > 1 else [], input_output_aliases=aliases,
        compiler_params=_cparams(("parallel", "parallel", "arbitrary")),
    )(*args)
    return res if rms_gain is not None else res[0]


def _row_call(name, t, tm, rows_in, residents, rows_out, accs, body):
    n_in, n_res, n_out, n_acc = len(rows_in), len(residents), len(rows_out), len(accs)
    steps = t // tm
    assert t % tm == 0
    narrow = [i for i, a in enumerate(accs) if a[1] != F32]

    def kernel_body(*refs):
        in_refs, res_refs = refs[:n_in], refs[n_in:n_in + n_res]
        out_refs = refs[n_in + n_res:n_in + n_res + n_out]
        acc_out = list(refs[n_in + n_res + n_out:n_in + n_res + n_out + n_acc])
        scratch = refs[n_in + n_res + n_out + n_acc:]
        acc_refs = list(acc_out)
        for s_ref, i in zip(scratch, narrow):
            acc_refs[i] = s_ref
        if accs:
            @pl.when(pl.program_id(0) == 0)
            def _():
                for acc in acc_refs:
                    acc[...] = jnp.zeros_like(acc)
        body(in_refs, res_refs, out_refs, acc_refs)
        if narrow:
            @pl.when(pl.program_id(0) == steps - 1)
            def _():
                for i in narrow:
                    acc_out[i][...] = acc_refs[i][...].astype(acc_out[i].dtype)

    once = dict(pipeline_mode=pl.Buffered(1)) if steps > 1 else {}
    in_specs = [pl.BlockSpec((tm, cols), lambda i, cb=cb: (i, cb)) for _, cols, cb in rows_in]
    in_specs += [pl.BlockSpec(r.shape, lambda i, nd=r.ndim: (0,) * nd, **once) for r in residents]
    out_specs = [pl.BlockSpec((tm, cols), lambda i, cb=cb: (i, cb)) for _, cols, cb, _ in rows_out]
    out_specs += [pl.BlockSpec(a[0], lambda i, nd=len(a[0]), cb=(a[3] if len(a) == 4 else 0): (0,) * (nd - 1) + (cb,))
                  for a in accs]
    out_shape = [jax.ShapeDtypeStruct((t, total), dt) for total, _, _, dt in rows_out]
    out_shape += [jax.ShapeDtypeStruct((a[0][0], a[2]) if len(a) == 4 else a[0], a[1]) for a in accs]
    return pl.pallas_call(
        kernel_body, name=name, grid=(steps,), in_specs=in_specs, out_specs=out_specs, out_shape=out_shape,
        scratch_shapes=[pltpu.VMEM(accs[i][0], F32) for i in narrow],
        compiler_params=_cparams(("arbitrary",) if accs else ("parallel",)),
    )(*[a for a, _, _ in rows_in], *residents)


def _rms_apply(xv, gain):
    return xv * lax.rsqrt(jnp.mean(xv * xv, axis=-1, keepdims=True) + RMS_EPS) * gain


def _rms_grad(dres, dh, xv, gain):
    r = lax.rsqrt(jnp.mean(xv * xv, axis=-1, keepdims=True) + RMS_EPS)
    xhat = xv * r
    dxh = dh * gain
    dx = dres + r * (dxh - xhat * jnp.mean(dxh * xhat, axis=-1, keepdims=True))
    return dx, jnp.sum(dh * xhat, axis=0, keepdims=True)


def _in_proj_gather(xf, gain, w_shard, *, name):
    t, d = xf.shape
    cb = w_shard.shape[1]
    tm = min(1024, t)
    steps = t // tm
    mx, my, _ = _mesh_pos()
    order = jnp.stack([2 * mx + my, 2 * (1 - mx) + my, 2 * mx + (1 - my), 2 * (1 - mx) + (1 - my)]).astype(jnp.int32)

    def body(order_ref, x_ref, g_ref, ws_ref, h_ref, p_ref, wout_ref, w_ref, send_sems, recv_sems, own_sem):
        ps, i = pl.program_id(0), pl.program_id(1)
        x, y, c = _mesh_pos()
        me, sib = (x, y, c), (x, y, 1 - c)
        chips = [(1 - x, y), (x, 1 - y), (1 - x, 1 - y)]

        def copy(k, block, to, from_shard=False):
            return pltpu.make_async_remote_copy(
                src_ref=ws_ref if from_shard else w_ref.at[_dev_index(block)], dst_ref=w_ref.at[_dev_index(block)],
                send_sem=send_sems.at[k], recv_sem=recv_sems.at[k], device_id=to, device_id_type=MESH_ID)

        own = pltpu.make_async_copy(ws_ref, w_ref.at[_dev_index(me)], own_sem)
        first = [copy(0, me, sib, True)] + [copy(1 + j, me, (*chip, c), True) for j, chip in enumerate(chips)]
        passed = [copy(4 + j, (*chip, c), sib) for j, chip in enumerate(chips)]

        @pl.when(jnp.logical_and(ps == 0, i == 0))
        def _():
            own.start()
            for cp in first:
                cp.start()
            own.wait()
            copy(0, sib, me).wait_recv()

        for j, chip in enumerate(chips):
            @pl.when(jnp.logical_and(ps == j + 1, i == 0))
            def _(j=j, chip=chip):
                copy(1 + j, (*chip, c), me).wait_recv()
                passed[j].start()
                copy(4 + j, (*chip, 1 - c), me).wait_recv()

        h = _rms_apply(x_ref[...], g_ref[...]).astype(BF16)
        h_ref[...] = h
        chip_id = order_ref[ps]
        p_ref[:, 0:cb] = _dot(h, w_ref[2 * chip_id], _NN).astype(BF16)
        p_ref[:, cb:2 * cb] = _dot(h, w_ref[2 * chip_id + 1], _NN).astype(BF16)

        @pl.when(jnp.logical_and(ps == 3, i == steps - 1))
        def _():
            for cp in first + passed:
                cp.wait_send()
            keep = pltpu.make_async_copy(w_ref, wout_ref, own_sem)
            keep.start()
            keep.wait()

    gs = pltpu.PrefetchScalarGridSpec(
        num_scalar_prefetch=1, grid=(4, steps),
        in_specs=[pl.BlockSpec((tm, d), lambda ps, i, o: (i, 0)), pl.BlockSpec((1, d), lambda ps, i, o: (0, 0)),
                  pl.BlockSpec(memory_space=pl.ANY)],
        out_specs=[pl.BlockSpec((tm, d), lambda ps, i, o: (jnp.where(ps == 0, i, steps - 1), 0)),
                   pl.BlockSpec((tm, 2 * cb), lambda ps, i, o: (i, o[ps])), pl.BlockSpec(memory_space=pl.ANY)],
        scratch_shapes=[pltpu.VMEM((N_DEV, d, cb), BF16), pltpu.SemaphoreType.DMA((7,)), pltpu.SemaphoreType.DMA((7,)),
                        pltpu.SemaphoreType.DMA(())])
    return pl.pallas_call(
        body, name=name, grid_spec=gs,
        out_shape=[jax.ShapeDtypeStruct((t, d), BF16), jax.ShapeDtypeStruct((t, N_DEV * cb), BF16),
                   jax.ShapeDtypeStruct((N_DEV, d, cb), BF16)],
        compiler_params=_cparams(("arbitrary", "arbitrary")))(order, xf, gain, w_shard)


def _mix_out(p, y_a, y_b, b_gate, xf, w_mix, gain, w_q, *, name):
    t, d = xf.shape

    def body(ins, res, outs, accs):
        ga_ref, gb_ref, ya_ref, yb_ref, x_ref = ins
        bg_ref, wm_ref, g_ref, wq_ref = res
        m_ref, x1_ref, h_ref, q_ref = outs
        sa = _sigmoid(ga_ref[...].astype(F32) + bg_ref[0:1, :])
        sb = _sigmoid(gb_ref[...].astype(F32) + bg_ref[1:2, :])
        merged = (sa * ya_ref[...].astype(F32) + sb * yb_ref[...].astype(F32)).astype(BF16)
        m_ref[...] = merged
        x1 = x_ref[...] + _dot(merged, wm_ref[...], _NN)
        x1_ref[...] = x1
        h = _rms_apply(x1, g_ref[...]).astype(BF16)
        h_ref[...] = h
        q_ref[...] = _dot(h, wq_ref[...], _NN).astype(BF16)

    return _row_call(name, t, min(512, t), [(p, d, 4), (p, d, 5), (y_a, d, 0), (y_b, d, 0), (xf, d, 0)],
                     [b_gate, w_mix, gain, w_q], [(d, d, 0, BF16), (d, d, 0, F32), (d, d, 0, BF16), (d, d, 0, BF16)], [], body)


def _ffn_fwd(h3, x2, target, w_gu_t, w_down, gain, *, name):
    t, d = x2.shape
    f2 = w_gu_t.shape[0]
    f = f2 // 2
    half = f // 2

    def body(ins, res, outs, accs):
        h_ref, x2_ref, t_ref = ins
        wgu_ref, wd_ref, g_ref = res
        gu_ref, act_ref, dx_ref = outs
        loss_ref, dg_ref = accs
        h = h_ref[...]
        x3 = x2_ref[...]
        for c0 in (0, half):
            gt = _dot(h, wgu_ref[c0:c0 + half, :], _NT).astype(BF16)
            up = _dot(h, wgu_ref[f + c0:f + c0 + half, :], _NT).astype(BF16)
            gu_ref[:, c0:c0 + half] = gt
            gu_ref[:, f + c0:f + c0 + half] = up
            gtf = gt.astype(F32)
            act = (gtf * _sigmoid(gtf) * up.astype(F32)).astype(BF16)
            act_ref[:, c0:c0 + half] = act
            x3 = x3 + _dot(act, wd_ref[c0:c0 + half, :], _NN)
        g = g_ref[...]
        r = lax.rsqrt(jnp.mean(x3 * x3, axis=-1, keepdims=True) + RMS_EPS)
        xhat = x3 * r
        err = xhat * g - t_ref[...]
        loss_ref[...] += 0.5 * jnp.sum(jnp.mean(err * err, axis=-1, keepdims=True), axis=0, keepdims=True)
        dy = err * (1.0 / d)
        dg_ref[...] += jnp.sum(dy * xhat, axis=0, keepdims=True)
        dxh = dy * g
        dx_ref[...] = r * (dxh - xhat * jnp.mean(dxh * xhat, axis=-1, keepdims=True))

    return _row_call(name, t, min(256, t), [(h3, d, 0), (x2, d, 0), (target, d, 0)], [w_gu_t, w_down, gain],
                     [(f2, f2, 0, BF16), (f, f, 0, BF16), (d, d, 0, F32)], [((1, 1), F32), ((1, d), F32)], body)


def _ffn_bwd(dx3, gu, x2, w_down, w_gu_t, gain, w_xo, *, name):
    t, d = x2.shape
    f2 = w_gu_t.shape[0]
    f = f2 // 2
    half = f // 2

    def body(ins, res, outs, accs):
        dx3_ref, gu_ref, x2_ref = ins
        wd_ref, wgu_ref, g_ref, wxo_ref = res
        dgu_ref, dx2_ref, do_ref = outs
        (dg_ref,) = accs
        dx3v = dx3_ref[...]
        dxb = dx3v.astype(BF16)
        dh = jnp.zeros(dx3v.shape, F32)
        for c0 in (0, half):
            dact = _dot(dxb, wd_ref[c0:c0 + half, :], _NT)
            gt = gu_ref[:, c0:c0 + half].astype(F32)
            up = gu_ref[:, f + c0:f + c0 + half].astype(F32)
            sg = _sigmoid(gt)
            dgt = (dact * up * sg * (1.0 + gt * (1.0 - sg))).astype(BF16)
            dup = (dact * gt * sg).astype(BF16)
            dgu_ref[:, c0:c0 + half] = dgt
            dgu_ref[:, f + c0:f + c0 + half] = dup
            dh = dh + _dot(dgt, wgu_ref[c0:c0 + half, :], _NN) + _dot(dup, wgu_ref[f + c0:f + c0 + half, :], _NN)
        dx2, dg = _rms_grad(dx3v, dh, x2_ref[...], g_ref[...])
        dx2_ref[...] = dx2
        dg_ref[...] += dg
        do_ref[...] = _dot(dx2.astype(BF16), wxo_ref[...], _NT).astype(BF16)

    return _row_call(name, t, min(256, t), [(dx3, d, 0), (gu, f2, 0), (x2, d, 0)], [w_down, w_gu_t, gain, w_xo],
                     [(f2, f2, 0, BF16), (d, d, 0, F32), (d, d, 0, BF16)], [((1, d), F32)], body)


def _proj_rms_bwd(dy, dres, x, w, gain, *, name, h=None):
    t, d = x.shape
    k = dy.shape[1]

    def body(ins, res, outs, accs):
        dy_ref, dres_ref, x_ref = ins[:3]
        w_ref, g_ref = res
        if h is not None:
            accs[1][...] += _dot(ins[3][...], dy_ref[...], _TN)
        if w.ndim == 3:
            cb = w.shape[2]
            dh = _dot(dy_ref[:, 0:cb], w_ref[0], _NT)
            for j in range(1, w.shape[0]):
                dh = dh + _dot(dy_ref[:, j * cb:(j + 1) * cb], w_ref[j], _NT)
        else:
            dh = _dot(dy_ref[...], w_ref[...], _NT)
        dx, dg = _rms_grad(dres_ref[...], dh, x_ref[...], g_ref[...])
        outs[0][...] = dx
        accs[0][...] += dg

    rows_in = [(dy, k, 0), (dres, d, 0), (x, d, 0)] + ([(h, d, 0)] if h is not None else [])
    accs = [((1, d), F32)] + ([((d, k), BF16)] if h is not None else [])
    tm = 1024 if w.ndim == 2 else 512
    return _row_call(name, t, min(tm, t), rows_in, [w, gain], [(d, d, 0, F32)], accs, body)


def _gates_bwd_fused(dx1, p, y_a, y_b, b_gate, w_mix, merged, h1, *, name):
    t, d = y_a.shape

    def body(ins, res, outs, accs):
        dx_ref, ga_ref, gb_ref, ya_ref, yb_ref, m_ref, h1_ref = ins
        bg_ref, wm_ref = res
        dp_ref, dya_ref, dyb_ref = outs
        dbg_ref, dwm_ref, dwin_ref = accs
        dxb = dx_ref[...].astype(BF16)
        dwm_ref[...] += _dot(m_ref[...], dxb, _TN)
        dm = _dot(dxb, wm_ref[...], _NT)
        sa = _sigmoid(ga_ref[...].astype(F32) + bg_ref[0:1, :])
        sb = _sigmoid(gb_ref[...].astype(F32) + bg_ref[1:2, :])
        dya_ref[...] = (dm * sa).astype(BF16)
        dyb_ref[...] = (dm * sb).astype(BF16)
        dga = dm * ya_ref[...].astype(F32) * sa * (1.0 - sa)
        dgb = dm * yb_ref[...].astype(F32) * sb * (1.0 - sb)
        dp_ref[:, 0:d] = dga.astype(BF16)
        dp_ref[:, d:2 * d] = dgb.astype(BF16)
        dbg_ref[0:1, :] += jnp.sum(dga, axis=0, keepdims=True)
        dbg_ref[1:2, :] += jnp.sum(dgb, axis=0, keepdims=True)
        dwin_ref[...] += _dot(h1_ref[...], dp_ref[...], _TN)

    return _row_call(name, t, min(256, t),
                     [(dx1, d, 0), (p, d, 4), (p, d, 5), (y_a, d, 0), (y_b, d, 0), (merged, d, 0), (h1, d, 0)],
                     [b_gate, w_mix], [(p.shape[1], 2 * d, 2, BF16), (d, d, 0, BF16), (d, d, 0, BF16)],
                     [((8, d), F32), ((d, d), BF16), ((d, 2 * d), BF16, p.shape[1], 2)], body)


def _conv_ln_bwd_fused(dy_a, c, a_act, w_conv_out, ln_g, ln_b, *, name):
    t, d = c.shape

    def body(ins, res, outs, accs):
        dy_ref, c_ref, act_ref = ins
        w_ref, lg_ref, lb_ref = res
        dlg_ref, dlb_ref, dw_ref = accs
        dw_ref[...] += _dot(act_ref[...], dy_ref[...], _TN)
        dact = _dot(dy_ref[...], w_ref[...], _NT)
        cv = c_ref[...].astype(F32)
        g = lg_ref[...]
        mu = jnp.mean(cv, axis=-1, keepdims=True)
        dv = cv - mu
        rstd = lax.rsqrt(jnp.mean(dv * dv, axis=-1, keepdims=True) + LN_EPS)
        chat = dv * rstd
        aln = chat * g + lb_ref[...]
        sg = _sigmoid(aln)
        daln = dact * (sg * (1.0 + aln * (1.0 - sg)))
        dlb_ref[...] += jnp.sum(daln, axis=0, keepdims=True)
        dlg_ref[...] += jnp.sum(daln * chat, axis=0, keepdims=True)
        dchat = daln * g
        dc = rstd * (dchat - jnp.mean(dchat, axis=-1, keepdims=True)
                     - chat * jnp.mean(dchat * chat, axis=-1, keepdims=True))
        outs[0][...] = dc.astype(BF16)

    return _row_call(name, t, min(1024, t), [(dy_a, d, 0), (c, d, 0), (a_act, d, 0)], [w_conv_out, ln_g, ln_b],
                     [(d, d, 0, BF16)], [((1, d), F32), ((1, d), F32), ((d, d), BF16)], body)


def _row_spec(tt, cols, col_block=0):
    return pl.BlockSpec((tt, cols), lambda i: (i, col_block))


def _const_spec(shape):
    return pl.BlockSpec(shape, lambda *_: (0,) * len(shape))


def _rms_fwd(x, gain, *, name):
    t, d = x.shape
    tt = min(TOKEN_TILE, t)

    def body(x_ref, g_ref, h_ref):
        xv = x_ref[...]
        r = lax.rsqrt(jnp.mean(xv * xv, axis=-1, keepdims=True) + RMS_EPS)
        h_ref[...] = (xv * r * g_ref[...]).astype(BF16)

    return pl.pallas_call(
        body, name=name, grid=(t // tt,), in_specs=[_row_spec(tt, d), _const_spec((1, d))],
        out_specs=_row_spec(tt, d), out_shape=jax.ShapeDtypeStruct((t, d), BF16),
        compiler_params=_cparams(("parallel",)))(x, gain)


def _rms_bwd(dres, dh, x, gain, *, name, need_dx=True):
    t, d = x.shape
    tt = min(TOKEN_TILE, t)

    def body(*refs):
        if need_dx:
            dres_ref, dh_ref, x_ref, g_ref, dx_ref, dg_ref = refs
        else:
            dh_ref, x_ref, g_ref, dg_ref = refs

        @pl.when(pl.program_id(0) == 0)
        def _():
            dg_ref[...] = jnp.zeros_like(dg_ref)

        xv = x_ref[...]
        dhv = dh_ref[...].astype(F32)
        r = lax.rsqrt(jnp.mean(xv * xv, axis=-1, keepdims=True) + RMS_EPS)
        xhat = xv * r
        dg_ref[...] += jnp.sum(dhv * xhat, axis=0, keepdims=True)
        if need_dx:
            dxh = dhv * g_ref[...]
            dx_ref[...] = dres_ref[...] + r * (dxh - xhat * jnp.mean(dxh * xhat, axis=-1, keepdims=True))

    rs = _row_spec(tt, d)
    if need_dx:
        in_specs, args = [rs, rs, rs, _const_spec((1, d))], (dres, dh, x, gain)
        out_specs = [rs, _const_spec((1, d))]
        out_shape = [jax.ShapeDtypeStruct((t, d), F32), jax.ShapeDtypeStruct((1, d), F32)]
    else:
        in_specs, args = [rs, rs, _const_spec((1, d))], (dh, x, gain)
        out_specs = [_const_spec((1, d))]
        out_shape = [jax.ShapeDtypeStruct((1, d), F32)]
    res = pl.pallas_call(body, name=name, grid=(t // tt,), in_specs=in_specs, out_specs=out_specs, out_shape=out_shape,
                         compiler_params=_cparams(("arbitrary",)))(*args)
    return res if need_dx else res[0]


SUBLANES = 8
SHIFT_ROWS = 40


def _conv_apply(sbuf_ref, w_ref, out_ref, tt, offsets, bias_ref=None):
    d = out_ref.shape[1]
    for cc in range(d // LANES):
        cs = slice(cc * LANES, (cc + 1) * LANES)
        taps = [jnp.broadcast_to(w_ref[k:k + 1, cs], (SUBLANES, LANES)) for k in range(CONV_WIDTH)]
        bias = None if bias_ref is None else jnp.broadcast_to(bias_ref[:, cs], (SUBLANES, LANES))

        def row_body(r, carry, cs=cs, taps=taps, bias=bias):
            r0 = pl.multiple_of(r * CONV_ROWS, CONV_ROWS)
            for q in range(CONV_ROWS // SUBLANES):
                acc = _tap(sbuf_ref, r0 + q * SUBLANES, cs, offsets[0]) * taps[0]
                for k in range(1, CONV_WIDTH):
                    acc = acc + _tap(sbuf_ref, r0 + q * SUBLANES, cs, offsets[k]) * taps[k]
                if bias is not None:
                    acc = acc + bias
                out_ref[pl.ds(r0 + q * SUBLANES, SUBLANES), cs] = acc
            return carry

        lax.fori_loop(0, tt // CONV_ROWS, row_body, 0)


def _fill_shifts(sbuf_ref, rows):
    d = sbuf_ref.shape[2]
    assert rows % SHIFT_ROWS == 0

    def row_body(i, carry):
        r0 = pl.multiple_of(i * SHIFT_ROWS, SUBLANES)
        for cc in range(d // CONV_COLS):
            cs = slice(cc * CONV_COLS, (cc + 1) * CONV_COLS)
            win = sbuf_ref[0, pl.ds(r0, SHIFT_ROWS + SUBLANES), cs]
            for sh in range(1, SUBLANES):
                sbuf_ref[sh, pl.ds(r0, SHIFT_ROWS), cs] = win[sh:sh + SHIFT_ROWS, :]
        return carry

    lax.fori_loop(0, rows // SHIFT_ROWS, row_body, 0)


def _tap(sbuf_ref, r0, cs, offset):
    sh = offset % SUBLANES
    return sbuf_ref[sh, pl.ds(pl.multiple_of(r0 + (offset - sh), SUBLANES), SUBLANES), cs]


def _conv_specs(bl, s, tt, d, col_a, col_g):
    nj = s // tt
    per = tt // CONV_HALO
    main_a = pl.BlockSpec((tt, d), lambda b, j: (b * nj + j, col_a))
    main_g = pl.BlockSpec((tt, d), lambda b, j: (b * nj + j, col_g))
    prev = lambda b, j: jnp.maximum((b * nj + j) * per - 1, 0)
    halo_a = pl.BlockSpec((CONV_HALO, d), lambda b, j: (prev(b, j), col_a))
    halo_g = pl.BlockSpec((CONV_HALO, d), lambda b, j: (prev(b, j), col_g))
    return main_a, main_g, halo_a, halo_g


def _fill_glu(sbuf_ref, a_ref, g_ref, ha_ref, hg_ref, tt):
    first = pl.program_id(1) == 0
    ha = ha_ref[...].astype(F32)
    hg = hg_ref[...].astype(F32)
    sbuf_ref[0, pl.ds(0, CONV_HALO), :] = jnp.where(first, 0.0, ha * _sigmoid(hg))
    av = a_ref[...].astype(F32)
    gv = g_ref[...].astype(F32)
    sbuf_ref[0, pl.ds(CONV_HALO, tt), :] = av * _sigmoid(gv)
    _fill_shifts(sbuf_ref, tt + CONV_HALO - SUBLANES)


def _conv_fwd(p, conv_w, conv_b, ln_g, ln_b, *, bl, s, name):
    t = p.shape[0]
    d = conv_w.shape[1]
    tt = min(TOKEN_TILE, s)
    off = CONV_HALO - (CONV_WIDTH - 1)

    def body(a_ref, g_ref, ha_ref, hg_ref, w_ref, b_ref, lg_ref, lb_ref, c_ref, act_ref, sbuf_ref, cbuf_ref):
        _fill_glu(sbuf_ref, a_ref, g_ref, ha_ref, hg_ref, tt)

        _conv_apply(sbuf_ref, w_ref, cbuf_ref, tt, [off + k for k in range(CONV_WIDTH)], bias_ref=b_ref)
        cv = cbuf_ref[...]
        c_ref[...] = cv.astype(BF16)
        mu = jnp.mean(cv, axis=-1, keepdims=True)
        dv = cv - mu
        rstd = lax.rsqrt(jnp.mean(dv * dv, axis=-1, keepdims=True) + LN_EPS)
        aln = dv * rstd * lg_ref[...] + lb_ref[...]
        act_ref[...] = (aln * _sigmoid(aln)).astype(BF16)

    main_a, main_g, halo_a, halo_g = _conv_specs(bl, s, tt, d, 0, 1)
    out_spec = pl.BlockSpec((tt, d), lambda b, j: (b * (s // tt) + j, 0))
    return pl.pallas_call(
        body, name=name, grid=(bl, s // tt),
        in_specs=[main_a, main_g, halo_a, halo_g, _const_spec((CONV_HALO, d)), _const_spec((1, d)), _const_spec((1, d)),
                  _const_spec((1, d))],
        out_specs=[out_spec, out_spec],
        out_shape=[jax.ShapeDtypeStruct((t, d), BF16), jax.ShapeDtypeStruct((t, d), BF16)],
        scratch_shapes=[pltpu.VMEM((SUBLANES, tt + CONV_HALO, d), F32), pltpu.VMEM((tt, d), F32)],
        compiler_params=_cparams(("parallel", "parallel")))(p, p, p, p, conv_w, conv_b, ln_g, ln_b)


def _conv_bwd(dp, dc, p, conv_w, h1, dw_in, *, bl, s, name):
    t = p.shape[0]
    d = conv_w.shape[1]
    tt = min(TOKEN_TILE, s)
    nj = s // tt
    per = tt // CONV_HALO
    off = CONV_HALO - (CONV_WIDTH - 1)
    last_blk = t // CONV_HALO - 1

    def body(dp_in, dc_ref, dcn_ref, a_ref, g_ref, ha_ref, hg_ref, w_ref, h1_ref, dwin_in, dp_ref, dw_ref, db_ref,
             dwin_out, gbuf_ref, dbuf_ref, dglu_ref, acc_ref, dwin_ref):
        del dp_in, dwin_in
        b, j = pl.program_id(0), pl.program_id(1)
        start = jnp.logical_and(b == 0, j == 0)
        end = jnp.logical_and(b == bl - 1, j == nj - 1)

        @pl.when(start)
        def _():
            acc_ref[...] = jnp.zeros_like(acc_ref)
            db_ref[...] = jnp.zeros_like(db_ref)
            dwin_ref[...] = jnp.zeros_like(dwin_ref)

        _fill_glu(gbuf_ref, a_ref, g_ref, ha_ref, hg_ref, tt)
        dcv = dc_ref[...].astype(F32)
        dbuf_ref[0, pl.ds(0, tt), :] = dcv
        dbuf_ref[0, pl.ds(tt, CONV_HALO), :] = jnp.where(j == nj - 1, 0.0, dcn_ref[...].astype(F32))
        _fill_shifts(dbuf_ref, tt + CONV_HALO - SUBLANES)
        db_ref[...] += jnp.sum(dcv, axis=0, keepdims=True)

        for cc in range(d // LANES):
            cs = slice(cc * LANES, (cc + 1) * LANES)

            def row_body(r, accs, cs=cs):
                r0 = pl.multiple_of(r * CONV_ROWS, CONV_ROWS)
                accs = list(accs)
                for q in range(CONV_ROWS // SUBLANES):
                    dcw = dbuf_ref[0, pl.ds(r0 + q * SUBLANES, SUBLANES), cs]
                    for k in range(CONV_WIDTH):
                        accs[k] = accs[k] + dcw * _tap(gbuf_ref, r0 + q * SUBLANES, cs, off + k)
                return tuple(accs)

            zero = jnp.zeros((SUBLANES, LANES), F32)
            accs = lax.fori_loop(0, tt // CONV_ROWS, row_body, (zero,) * CONV_WIDTH)
            for k in range(CONV_WIDTH):
                acc_ref[k, :, cs] += accs[k]

        _conv_apply(dbuf_ref, w_ref, dglu_ref, tt, [CONV_WIDTH - 1 - k for k in range(CONV_WIDTH)])
        dglu = dglu_ref[...]
        av = a_ref[...].astype(F32)
        sg = _sigmoid(g_ref[...].astype(F32))
        dp_ref[:, 0:d] = (dglu * sg).astype(BF16)
        dp_ref[:, d:2 * d] = (dglu * av * sg * (1.0 - sg)).astype(BF16)
        dwin_ref[...] += _dot(h1_ref[...], dp_ref[...], _TN)

        @pl.when(end)
        def _():
            for k in range(CONV_WIDTH):
                dw_ref[k:k + 1, :] = jnp.sum(acc_ref[k], axis=0, keepdims=True)
            dw_ref[CONV_WIDTH:CONV_HALO, :] = jnp.zeros((CONV_HALO - CONV_WIDTH, d), F32)
            dwin_out[...] = dwin_ref[...].astype(dwin_out.dtype)

    main_a, main_g, halo_a, halo_g = _conv_specs(bl, s, tt, d, 0, 1)
    dc_main = pl.BlockSpec((tt, d), lambda b, j: (b * nj + j, 0))
    dc_next = pl.BlockSpec((CONV_HALO, d), lambda b, j: (jnp.minimum((b * nj + j + 1) * per, last_blk), 0))
    hbm = pl.BlockSpec(memory_space=pl.ANY)
    return pl.pallas_call(
        body, name=name, grid=(bl, nj),
        in_specs=[hbm, dc_main, dc_next, main_a, main_g, halo_a, halo_g, _const_spec((CONV_HALO, d)), dc_main, hbm],
        out_specs=[pl.BlockSpec((tt, 2 * d), lambda b, j: (b * nj + j, 0)), _const_spec((CONV_HALO, d)), _const_spec((1, d)),
                   _const_spec((d, 2 * d))],
        out_shape=[jax.ShapeDtypeStruct(dp.shape, dp.dtype), jax.ShapeDtypeStruct((CONV_HALO, d), F32),
                   jax.ShapeDtypeStruct((1, d), F32), jax.ShapeDtypeStruct(dw_in.shape, dw_in.dtype)],
        scratch_shapes=[pltpu.VMEM((SUBLANES, tt + CONV_HALO, d), F32), pltpu.VMEM((SUBLANES, tt + CONV_HALO, d), F32),
                        pltpu.VMEM((tt, d), F32), pltpu.VMEM((CONV_HALO, SUBLANES, d), F32), pltpu.VMEM((d, 2 * d), F32)],
        input_output_aliases={0: 0, 9: 3},
        compiler_params=_cparams(("arbitrary", "arbitrary")))(dp, dc, dc, p, p, p, p, conv_w, h1, dw_in)


def _sgu_stats(bv):
    gv = _gelu(bv)
    mu = jnp.mean(gv, axis=-1, keepdims=True)
    dv = gv - mu
    rstd = lax.rsqrt(jnp.mean(dv * dv, axis=-1, keepdims=True) + LN_EPS)
    return dv * rstd, rstd


def _sgu_fwd(p, wm, bias, ln_g, ln_b, *, name):
    t = p.shape[0]
    d = ln_g.shape[1]
    tt = SGU_TILE
    gd = d // SGU_GROUPS

    def body(u_ref, v_ref, wm_ref, bias_ref, lg_ref, lb_ref, sg_ref, vn_ref):
        u = _gelu(u_ref[...].astype(F32))
        vhat, _ = _sgu_stats(v_ref[...].astype(F32))
        vb = (vhat * lg_ref[...] + lb_ref[...]).astype(BF16)
        vn_ref[...] = vb
        for ci in range(tt // SGU_CHUNK):
            rows = slice(ci * SGU_CHUNK, (ci + 1) * SGU_CHUNK)
            for g in range(SGU_GROUPS):
                gs = slice(g * gd, (g + 1) * gd)
                z = _dot(wm_ref[g], vb[rows, gs], _NN) + bias_ref[g]
                sg_ref[rows, gs] = (u[rows, gs] * z).astype(BF16)

    rs = _row_spec(tt, d)
    return pl.pallas_call(
        body, name=name, grid=(t // tt,),
        in_specs=[_row_spec(tt, d, 2), _row_spec(tt, d, 3), _const_spec(wm.shape), _const_spec(bias.shape),
                  _const_spec((1, d)), _const_spec((1, d))],
        out_specs=[rs, rs], out_shape=[jax.ShapeDtypeStruct((t, d), BF16), jax.ShapeDtypeStruct((t, d), BF16)],
        compiler_params=_cparams(("parallel",)))(p, p, wm, bias, ln_g, ln_b)


def _sgu_bwd(dp, dy_b, w_out, p, vn, wm, wmt, bias, ln_g, *, name):
    t = p.shape[0]
    d = ln_g.shape[1]
    tt = SGU_TILE
    ck = SGU_CHUNK
    gd = d // SGU_GROUPS
    nsteps = t // tt

    def body(dp_in, dyb_ref, wout_ref, u_ref, v_ref, vn_ref, wm_ref, wmt_ref, bias_ref, lg_ref,
             dp_ref, dw_ref, dbs_ref, dlg_ref, dlb_ref, dz_acc):
        del dp_in
        i = pl.program_id(0)

        @pl.when(i == 0)
        def _():
            dw_ref[...] = jnp.zeros_like(dw_ref)
            dlg_ref[...] = jnp.zeros_like(dlg_ref)
            dlb_ref[...] = jnp.zeros_like(dlb_ref)
            dz_acc[...] = jnp.zeros_like(dz_acc)

        bu = u_ref[...].astype(F32)
        bv = v_ref[...].astype(F32)
        u = _gelu(bu)
        vhat, rstd = _sgu_stats(bv)
        vb = vn_ref[...]
        dsg = _dot(dyb_ref[...], wout_ref[...], _NT)
        row = lax.broadcasted_iota(jnp.int32, (ck, ck), 0)
        col = lax.broadcasted_iota(jnp.int32, (ck, ck), 1)
        causal = col <= row
        du_rows, dv_rows = [], []
        for ci in range(tt // ck):
            rows = slice(ci * ck, (ci + 1) * ck)
            du_parts, dv_parts = [], []
            for g in range(SGU_GROUPS):
                gs = slice(g * gd, (g + 1) * gd)
                z = _dot(wm_ref[g], vb[rows, gs], _NN) + bias_ref[g]
                du_parts.append(dsg[rows, gs] * z)
                dz = dsg[rows, gs] * u[rows, gs]
                dz_acc[:, gs] += dz
                dzb = dz.astype(BF16)
                dw_ref[g] += jnp.where(causal, _dot(dzb, vb[rows, gs], _NT), 0.0)
                dv_parts.append(_dot(wmt_ref[g], dzb, _NN))
            du_rows.append(jnp.concatenate(du_parts, axis=1))
            dv_rows.append(jnp.concatenate(dv_parts, axis=1))
        du = jnp.concatenate(du_rows, axis=0)
        dv = jnp.concatenate(dv_rows, axis=0)
        dp_ref[:, 0:d] = (du * _gelu_grad(bu)).astype(BF16)
        dlb_ref[...] += jnp.sum(dv, axis=0, keepdims=True)
        dlg_ref[...] += jnp.sum(dv * vhat, axis=0, keepdims=True)
        dvh = dv * lg_ref[...]
        dgv = rstd * (dvh - jnp.mean(dvh, axis=-1, keepdims=True) - vhat * jnp.mean(dvh * vhat, axis=-1, keepdims=True))
        dp_ref[:, d:2 * d] = (dgv * _gelu_grad(bv)).astype(BF16)

        @pl.when(i == nsteps - 1)
        def _():
            ones = jnp.ones((8, gd), F32)
            for g in range(SGU_GROUPS):
                gs = slice(g * gd, (g + 1) * gd)
                tot = lax.dot_general(ones, dz_acc[:, gs], (_NT, ((), ())), preferred_element_type=F32,
                                      precision=lax.Precision.HIGHEST)
                dbs_ref[g:g + 1, :] = tot[0:1, :]

    rs = _row_spec(tt, d)
    c1 = _const_spec((1, d))
    return pl.pallas_call(
        body, name=name, grid=(nsteps,),
        in_specs=[pl.BlockSpec(memory_space=pl.ANY), rs, _const_spec(w_out.shape), _row_spec(tt, d, 2), _row_spec(tt, d, 3),
                  rs, _const_spec(wm.shape), _const_spec(wmt.shape), _const_spec(bias.shape), c1],
        out_specs=[pl.BlockSpec((tt, 2 * d), lambda i: (i, 1)), _const_spec(wm.shape), _const_spec((SGU_GROUPS, ck)), c1, c1],
        out_shape=[jax.ShapeDtypeStruct(dp.shape, dp.dtype), jax.ShapeDtypeStruct(wm.shape, F32),
                   jax.ShapeDtypeStruct((SGU_GROUPS, ck), F32), jax.ShapeDtypeStruct((1, d), F32),
                   jax.ShapeDtypeStruct((1, d), F32)],
        scratch_shapes=[pltpu.VMEM((ck, d), F32)],
        input_output_aliases={0: 0},
        compiler_params=_cparams(("arbitrary",)))(dp, dy_b, w_out, p, p, vn, wm, wmt, bias, ln_g)


def _softmax_rows(s):
    e = jnp.exp(s - jnp.max(s, axis=-1, keepdims=True))
    return e / jnp.sum(e, axis=-1, keepdims=True)


def _attn_fwd(q, kv, x1, w_xo, gain, *, bl, s, name):
    t, d = q.shape
    mlen = kv.shape[0] // bl
    hd = d // HEADS
    tq = min(ATTN_TILE, s)
    nq = s // tq
    scale = hd ** -0.5

    def body(q_ref, kv_ref, x1_ref, w_ref, g_ref, o_ref, x2_ref, h_ref):
        for h in range(HEADS):
            hs = slice(h * hd, (h + 1) * hd)
            vs = slice(d + h * hd, d + (h + 1) * hd)
            pr = _softmax_rows(_dot(q_ref[:, hs], kv_ref[:, hs], _NT) * scale)
            o_ref[:, hs] = _dot(pr.astype(BF16), kv_ref[:, vs], _NN).astype(BF16)
        x2 = x1_ref[...] + _dot(o_ref[...], w_ref[...], _NN)
        x2_ref[...] = x2
        h_ref[...] = _rms_apply(x2, g_ref[...]).astype(BF16)

    qs = pl.BlockSpec((tq, d), lambda b, j: (b * nq + j, 0))
    return pl.pallas_call(
        body, name=name, grid=(bl, nq),
        in_specs=[qs, pl.BlockSpec((mlen, 2 * d), lambda b, j: (b, 0)), qs, _const_spec(w_xo.shape), _const_spec((1, d))],
        out_specs=[qs, qs, qs],
        out_shape=[jax.ShapeDtypeStruct((t, d), BF16), jax.ShapeDtypeStruct((t, d), F32), jax.ShapeDtypeStruct((t, d), BF16)],
        compiler_params=_cparams(("parallel", "parallel")))(q, kv, x1, w_xo, gain)


def _attn_bwd(q, kv, do, *, bl, s, name):
    t, d = q.shape
    mlen = kv.shape[0] // bl
    hd = d // HEADS
    tq = min(ATTN_TILE, s)
    nq = s // tq
    scale = hd ** -0.5

    def body(q_ref, kv_ref, do_ref, dq_ref, dkv_ref):
        @pl.when(pl.program_id(1) == 0)
        def _():
            dkv_ref[...] = jnp.zeros_like(dkv_ref)

        for h in range(HEADS):
            hs = slice(h * hd, (h + 1) * hd)
            vs = slice(d + h * hd, d + (h + 1) * hd)
            qh, kh, vh, doh = q_ref[:, hs], kv_ref[:, hs], kv_ref[:, vs], do_ref[:, hs]
            pr = _softmax_rows(_dot(qh, kh, _NT) * scale)
            dpr = _dot(doh, vh, _NT)
            dkv_ref[:, vs] += _dot(pr.astype(BF16), doh, _TN)
            ds = (pr * (dpr - jnp.sum(dpr * pr, axis=-1, keepdims=True)) * scale).astype(BF16)
            dq_ref[:, hs] = _dot(ds, kh, _NN).astype(BF16)
            dkv_ref[:, hs] += _dot(ds, qh, _TN)

    qs = pl.BlockSpec((tq, d), lambda b, j: (b * nq + j, 0))
    ks = pl.BlockSpec((mlen, 2 * d), lambda b, j: (b, 0))
    return pl.pallas_call(
        body, name=name, grid=(bl, nq), in_specs=[qs, ks, qs], out_specs=[qs, ks],
        out_shape=[jax.ShapeDtypeStruct((t, d), BF16), jax.ShapeDtypeStruct(kv.shape, F32)],
        compiler_params=_cparams(("parallel", "arbitrary")))(q, kv, do)


def _mesh_pos():
    return lax.axis_index("x"), lax.axis_index("y"), lax.axis_index("c")


def _all_gather(arrs, *, name):
    n = len(arrs)
    hbm = pl.BlockSpec(memory_space=pl.ANY)

    def body(*refs):
        ins, outs = refs[:n], refs[n:2 * n]
        send_sems, recv_sems, loc_sems = refs[2 * n:]
        x, y, c = _mesh_pos()
        me, sib = (x, y, c), (x, y, 1 - c)
        chips = [(1 - x, y), (x, 1 - y), (1 - x, 1 - y)]

        def idx(dev):
            return 4 * dev[0] + 2 * dev[1] + dev[2]

        def copy(w, k, block, to, from_input=False):
            return pltpu.make_async_remote_copy(
                src_ref=ins[w] if from_input else outs[w].at[idx(block)], dst_ref=outs[w].at[idx(block)],
                send_sem=send_sems.at[w, k], recv_sem=recv_sems.at[w, k], device_id=to, device_id_type=MESH_ID)

        own = [pltpu.make_async_copy(ins[w], outs[w].at[idx(me)], loc_sems.at[w]) for w in range(n)]
        for cp in own:
            cp.start()
        first = []
        for w in range(n):
            first.append(copy(w, 0, me, sib, True))
            first += [copy(w, 1 + j, me, (*chip, c), True) for j, chip in enumerate(chips)]
        for cp in first:
            cp.start()
        passed = []
        for j, chip in enumerate(chips):
            for w in range(n):
                copy(w, 1 + j, (*chip, c), me).wait_recv()
                fwd = copy(w, 4 + j, (*chip, c), sib)
                fwd.start()
                passed.append(fwd)
        for w in range(n):
            copy(w, 0, sib, me).wait_recv()
            for j, chip in enumerate(chips):
                copy(w, 4 + j, (*chip, 1 - c), me).wait_recv()
        for cp in first + passed:
            cp.wait_send()
        for cp in own:
            cp.wait()

    return pl.pallas_call(
        body, name=name, in_specs=[hbm] * n, out_specs=[hbm] * n,
        out_shape=[jax.ShapeDtypeStruct((N_DEV, *a.shape), a.dtype) for a in arrs],
        scratch_shapes=[pltpu.SemaphoreType.DMA((n, 7)), pltpu.SemaphoreType.DMA((n, 7)), pltpu.SemaphoreType.DMA((n,))],
    )(*arrs)


_HBM = pl.BlockSpec(memory_space=pltpu.HBM)
_SEM = pl.BlockSpec(memory_space=pltpu.SEMAPHORE)
_ANY = pl.BlockSpec(memory_space=pl.ANY)
_EFFECT = pltpu.SideEffectType.DATAFLOW_SIDE_EFFECTING
N_PEERS = N_DEV - 1


def _related(pos, r):
    x, y, c = pos
    return (1 - x if r & 4 else x, 1 - y if r & 2 else y, 1 - c if r & 1 else c)


def _dev_index(dev):
    return 4 * dev[0] + 2 * dev[1] + dev[2]


def _in_hbm(a):
    return pltpu.with_memory_space_constraint(a, pltpu.HBM)


def _split_copies(kind, srcs, lands, send_sems, recv_sems):
    pos = _mesh_pos()
    me = _dev_index(pos)
    out = []
    for w in range(len(srcs)):
        for r in range(1, N_DEV):
            peer = _related(pos, r)
            if kind == "gather":
                src, dst_here, dst_there = srcs[w], lands[w].at[_dev_index(peer)], lands[w].at[me]
            elif srcs[w].ndim == 2:
                cb = lands[w].shape[2]
                src = srcs[w].at[:, pl.ds(pl.multiple_of(_dev_index(peer) * cb, LANES), cb)]
                dst_here = dst_there = lands[w].at[r - 1]
            else:
                src, dst_here, dst_there = srcs[w].at[_dev_index(peer)], lands[w].at[r - 1], lands[w].at[r - 1]
            out.append((src, dst_here, dst_there, send_sems.at[w * N_PEERS + r - 1], recv_sems.at[w * N_PEERS + r - 1], peer))
    return out


def _copy_start(kind, srcs, land_shapes, *, name, after=None):
    n = len(srcs)
    n_after = 0 if after is None else 1

    def body(*refs):
        src_refs, land_refs = refs[:n], refs[n:2 * n]
        send_sems, recv_sems = refs[2 * n + n_after], refs[2 * n + n_after + 1]
        token = refs[-1]
        for src, _, dst, ssem, rsem, peer in _split_copies(kind, src_refs, land_refs, send_sems, recv_sems):
            pltpu.make_async_remote_copy(src_ref=src, dst_ref=dst, send_sem=ssem, recv_sem=rsem, device_id=peer,
                                         device_id_type=MESH_ID).start()
        token[...] = jnp.zeros_like(token)

    lands = [_in_hbm(lax.empty(shape, s.dtype)) for s, shape in zip(srcs, land_shapes)]
    res = pl.pallas_call(
        body, name=name,
        out_shape=(pltpu.SemaphoreType.DMA((n * N_PEERS,)), pltpu.SemaphoreType.DMA((n * N_PEERS,)),
                   *[pltpu.HBM(s.shape, s.dtype) for s in srcs], *[pltpu.HBM(l.shape, l.dtype) for l in lands],
                   jax.ShapeDtypeStruct((8, 128), F32)),
        in_specs=[_HBM] * (2 * n) + [_ANY] * n_after,
        out_specs=(_SEM, _SEM, *[_HBM] * (2 * n), pl.BlockSpec(memory_space=pltpu.VMEM)),
        input_output_aliases={i: 2 + i for i in range(2 * n)},
        compiler_params=pltpu.CompilerParams(has_side_effects=_EFFECT),
    )(*[_in_hbm(s) for s in srcs], *lands, *([] if after is None else [after]))
    return res[0], res[1], list(res[2:2 + n]), list(res[2 + n:2 + 2 * n]), res[-1]


def _copy_wait(kind, send_sems, recv_sems, srcs, lands, after, *, name):
    n = len(srcs)

    def body(*refs):
        src_refs, land_refs = refs[:n], refs[n:2 * n]
        ssems, rsems = refs[2 * n], refs[2 * n + 1]
        for src, dst, _, ssem, rsem, peer in _split_copies(kind, src_refs, land_refs, ssems, rsems):
            cp = pltpu.make_async_remote_copy(src_ref=src, dst_ref=dst, send_sem=ssem, recv_sem=rsem, device_id=peer,
                                              device_id_type=MESH_ID)
            cp.wait_send()
            cp.wait_recv()

    res = pl.pallas_call(
        body, name=name,
        out_shape=(*[pltpu.HBM(s.shape, s.dtype) for s in srcs], *[pltpu.HBM(l.shape, l.dtype) for l in lands]),
        in_specs=[_HBM] * (2 * n) + [_SEM, _SEM, _ANY], out_specs=tuple([_HBM] * (2 * n)),
        input_output_aliases={i: i for i in range(2 * n)},
        compiler_params=pltpu.CompilerParams(has_side_effects=_EFFECT),
    )(*srcs, *lands, send_sems, recv_sems, after)
    return list(res[:n]), list(res[n:])


def _row_tile(rows):
    return max(tr for tr in range(16, min(rows, 512) + 1, 16) if rows % tr == 0)


def _adamw_math(w, g, m, v):
    m2 = ADAM_B1 * m + (1.0 - ADAM_B1) * g
    v2 = ADAM_B2 * v + (1.0 - ADAM_B2) * (g * g)
    m_hat = m2 / (1.0 - ADAM_B1 ** ADAM_STEP)
    v_hat = v2 / (1.0 - ADAM_B2 ** ADAM_STEP)
    delta = -ADAM_LR * (m_hat / (jnp.sqrt(v_hat) + ADAM_EPS) + ADAM_WD * w)
    return delta, m2, v2


def _adamw_shard(partials, landed, dev, w, m, v, *, name):
    r, c = w.shape
    tr = _row_tile(r)

    def body(dev_ref, p_ref, l_ref, w_ref, m_ref, v_ref, g_out, d_out, m_out, v_out):
        del dev_ref
        g = p_ref[...].astype(F32)
        for k in range(N_PEERS):
            g = g + l_ref[k].astype(F32)
        delta, m2, v2 = _adamw_math(w_ref[...], g, m_ref[...], v_ref[...])
        g_out[...] = g
        d_out[...] = delta
        m_out[...] = m2
        v_out[...] = v2

    blk = pl.BlockSpec((tr, c), lambda i, dev_ref: (i, 0))
    if partials.ndim == 2:
        own = pl.BlockSpec((tr, c), lambda i, dev_ref: (i, dev_ref[0]))
    else:
        own = pl.BlockSpec((None, tr, c), lambda i, dev_ref: (dev_ref[0], i, 0))
    gs = pltpu.PrefetchScalarGridSpec(
        num_scalar_prefetch=1, grid=(r // tr,),
        in_specs=[own, pl.BlockSpec((N_PEERS, tr, c), lambda i, dev_ref: (0, i, 0)), blk, blk, blk],
        out_specs=[blk] * 4)
    return pl.pallas_call(
        body, name=name, grid_spec=gs, out_shape=[jax.ShapeDtypeStruct((r, c), F32)] * 4,
        compiler_params=_cparams(("parallel",)))(dev, partials, landed, w, m, v)


def _sum_devices(p_ref, *idx):
    g = p_ref[(0, *idx)]
    for k in range(1, N_DEV):
        g = g + p_ref[(k, *idx)]
    return g


def _adamw_replicated(parts, states, loss_row, *, name):
    n_parts, n_par = len(parts), len(states)
    n_vec = n_par - (n_parts - 1)

    def body(*refs):
        part_refs, st = refs[:n_parts], refs[n_parts:n_parts + 3 * n_par]
        outs = refs[n_parts + 3 * n_par:]
        outs[0][...] = _sum_devices(part_refs[0], slice(loss_row, loss_row + 1), slice(0, 1))
        for i in range(n_par):
            g = _sum_devices(part_refs[0], slice(i, i + 1)) if i < n_vec else _sum_devices(part_refs[1 + i - n_vec])
            delta, m2, v2 = _adamw_math(st[3 * i][...], g, st[3 * i + 1][...], st[3 * i + 2][...])
            for o, val in zip(outs[1 + 4 * i:5 + 4 * i], (g, delta, m2, v2)):
                o[...] = val

    flat = [a for wmv in states for a in wmv]
    return pl.pallas_call(
        body, name=name,
        out_shape=[jax.ShapeDtypeStruct((1, 1), F32)] + [jax.ShapeDtypeStruct(w.shape, F32) for w, _, _ in states for _ in range(4)],
        compiler_params=pltpu.CompilerParams(vmem_limit_bytes=VMEM_LIMIT))(*parts, *flat)


def _adamw_column_shards(parts, dev, states, row0s, *, name):
    _, rows, _ = parts.shape
    c = states[0][0].shape[1]

    def body(dev_ref, p_ref, *refs):
        del dev_ref
        st, outs = refs[:3 * len(states)], refs[3 * len(states):]
        for j, r0 in enumerate(row0s):
            w_ref = st[3 * j]
            g = _sum_devices(p_ref, slice(r0, r0 + w_ref.shape[0]))
            delta, m2, v2 = _adamw_math(w_ref[...], g, st[3 * j + 1][...], st[3 * j + 2][...])
            for o, val in zip(outs[4 * j:4 * j + 4], (g, delta, m2, v2)):
                o[...] = val

    whole = lambda a: pl.BlockSpec(a.shape, lambda i, dev_ref: (0, 0))
    flat = [a for wmv in states for a in wmv]
    outs = [w for w, _, _ in states for _ in range(4)]
    gs = pltpu.PrefetchScalarGridSpec(
        num_scalar_prefetch=1, grid=(1,),
        in_specs=[pl.BlockSpec((N_DEV, rows, c), lambda i, dev_ref: (0, 0, dev_ref[0]))] + [whole(a) for a in flat],
        out_specs=[whole(a) for a in outs])
    return pl.pallas_call(
        body, name=name, grid_spec=gs, out_shape=[jax.ShapeDtypeStruct(a.shape, F32) for a in outs],
        compiler_params=_cparams(("arbitrary",)))(dev, parts, *flat)


def _pad_rows(a, rows):
    return jnp.pad(a, ((0, rows - a.shape[0]), (0, 0)))


def _unblock_cols(g):
    return jnp.transpose(g, (1, 0, 2)).reshape(g.shape[1], N_DEV * g.shape[2])


def kernel(x, mem, norm_mix, w_in, b_gate, conv_w, conv_b, conv_ln_g, conv_ln_b, w_conv_out, sgu_ln_g, sgu_ln_b, sgu_w, sgu_b, w_sgu_out, w_mix_out, norm_xattn, norm_mem, w_q, w_kv, w_xo, norm_ffn, w_gu, w_down, norm_final, loss_target, m_norm_mix, m_w_in, m_b_gate, m_conv_w, m_conv_b, m_conv_ln_g, m_conv_ln_b, m_w_conv_out, m_sgu_ln_g, m_sgu_ln_b, m_sgu_w, m_sgu_b, m_w_sgu_out, m_w_mix_out, m_norm_xattn, m_norm_mem, m_w_q, m_w_kv, m_w_xo, m_norm_ffn, m_w_gu, m_w_down, m_norm_final, v_norm_mix, v_w_in, v_b_gate, v_conv_w, v_conv_b, v_conv_ln_g, v_conv_ln_b, v_w_conv_out, v_sgu_ln_g, v_sgu_ln_b, v_sgu_w, v_sgu_b, v_w_sgu_out, v_w_mix_out, v_norm_xattn, v_norm_mem, v_w_q, v_w_kv, v_w_xo, v_norm_ffn, v_w_gu, v_w_down, v_norm_final):
    given = dict(locals())
    bl, s, d = x.shape
    t = bl * s
    xf = x.reshape(t, d)
    tgt = loss_target.reshape(t, d)
    memf = mem.reshape(bl * mem.shape[1], d)
    cx, cy, cc = lax.axis_index("x"), lax.axis_index("y"), lax.axis_index("c")
    dev = 4 * cx + 2 * cy + cc
    dev_id = dev.astype(jnp.int32).reshape(1)
    col_sharded = ["w_in", "w_kv"]
    transposed = ["w_gu"]

    def shard_of(name, prefix=""):
        a = given[prefix + name][0]
        return jnp.transpose(a) if name in transposed else a

    def full_weight(name, blocks):
        return _unblock_cols(blocks) if name in col_sharded else blocks.reshape(N_DEV * blocks.shape[1], blocks.shape[2])

    g_bg, g_cw = _all_gather([_pad_rows(b_gate[0], 8), _pad_rows(conv_w[0], CONV_HALO)], name="gather_small_params")
    h1, p, w_in_blocks = _in_proj_gather(xf, norm_mix + g_bg[0, 7:8, 0:1], w_in[0].astype(BF16), name="in_proj")
    early = ["w_conv_out", "w_sgu_out", "w_mix_out", "w_q", "w_kv", "w_xo"]
    late = ["w_gu", "w_down"]
    shards = {n: shard_of(n).astype(BF16) for n in early + late}
    started = {}
    for grp, names in (("early", early), ("late", late)):
        srcs = [shards[n] for n in names]
        started[grp] = _copy_start("gather", srcs, [(N_DEV, *a.shape) for a in srcs], name=f"gather_{grp}_start", after=p)
    token = started["early"][4][0:1, 0:1] + started["late"][4][0:1, 0:1]
    wfull = {}
    bg_full = _unblock_cols(g_bg)
    cw_full = _unblock_cols(g_cw)

    def finish_gather(grp, names, after):
        ssem, rsem, srcs, lands, _ = started[grp]
        _, lands = _copy_wait("gather", ssem, rsem, srcs, lands, after, name=f"gather_{grp}_wait")
        for n, land in zip(names, lands):
            wfull[n] = full_weight(n, lax.dynamic_update_index_in_dim(land, shards[n], dev, 0))

    tri = jnp.tril(jnp.ones((SGU_CHUNK, SGU_CHUNK), bool))
    wm32 = jnp.where(tri[None], sgu_w[0], 0.0)
    wm = wm32.astype(BF16)
    wmt = jnp.transpose(wm32, (0, 2, 1)).astype(BF16)
    sgu_bias = jnp.broadcast_to(sgu_b[0][:, :, None], (SGU_GROUPS, SGU_CHUNK, d // SGU_GROUPS))

    c_conv, a_act = _conv_fwd(p, cw_full, conv_b + token, conv_ln_g, conv_ln_b, bl=bl, s=s, name="conv_fwd")
    sg, vn = _sgu_fwd(p, wm, sgu_bias, sgu_ln_g, sgu_ln_b + token, name="sgu_fwd")
    finish_gather("early", early, a_act[0:16, 0:128] + sg[0:16, 0:128])
    y_a = _matmul(a_act, wfull["w_conv_out"], mode="nn", out_dtype=BF16, name="mm_conv_out", tm=1024, tn=1024, tk=1024)
    y_b = _matmul(sg, wfull["w_sgu_out"], mode="nn", out_dtype=BF16, name="mm_sgu_out", tm=1024, tn=1024, tk=1024)
    merged, x1, h2, q = _mix_out(p, y_a, y_b, bg_full, xf, wfull["w_mix_out"], norm_xattn, wfull["w_q"], name="mix_out")
    mem_n = _rms_fwd(memf, norm_mem, name="rms_mem")
    kv = _matmul(mem_n, wfull["w_kv"], mode="nn", out_dtype=BF16, name="mm_kv", tm=1024, tn=1024, tk=1024)
    o, x2, h3 = _attn_fwd(q, kv, x1, wfull["w_xo"], norm_ffn, bl=bl, s=s, name="attn_fwd")
    finish_gather("late", late, h3)
    gu, act, dx3, loss_part, d_norm_final = _ffn_fwd(h3, x2, tgt, wfull["w_gu"], wfull["w_down"],
                                                     norm_final.reshape(1, d), name="ffn_fwd")

    grads = {}
    sent = []

    def send_grads(names, tag, after=None):
        blocks, land_shapes = [], []
        for n in names:
            g = grads[n]
            if g.ndim == 2 and n in col_sharded:
                land_shapes.append((N_PEERS, g.shape[0], g.shape[1] // N_DEV))
            else:
                if g.ndim == 2:
                    g = g.reshape(N_DEV, -1, g.shape[1])
                land_shapes.append((N_PEERS, *g.shape[1:]))
            blocks.append(g)
        ssem, rsem, srcs, lands, tok = _copy_start("scatter", blocks, land_shapes, name=f"grads_{tag}_start", after=after)
        sent.append((names, ssem, rsem, srcs, lands))
        return tok[0:1, 0:1]

    dgu, dx2, do, d_norm_ffn = _ffn_bwd(dx3, gu, x2, wfull["w_down"], wfull["w_gu"], norm_ffn, wfull["w_xo"], name="ffn_bwd")
    grads["w_down"] = _matmul(act, dx3, mode="tn", out_dtype=BF16, name="mm_dw_down", tm=1408, tn=1024, tk=2048)
    grads["w_gu"] = _matmul(dgu, h3, mode="tn", out_dtype=BF16, name="mm_dw_gu", tm=1408, tn=1024, tk=2048)
    tok = send_grads(["w_down", "w_gu"], "ffn")
    grads["w_xo"] = _matmul(o, dx2, mode="tn", out_dtype=BF16, name="mm_dw_xo", tm=1024, tn=1024, tk=2048)
    dq, dkv = _attn_bwd(q, kv, do, bl=bl, s=s, name="attn_bwd")
    grads["w_kv"] = _matmul(mem_n, dkv, mode="tn", out_dtype=BF16, name="mm_dw_kv", tm=1024, tn=256, tk=1024,
                            col_blocks=N_DEV)
    tok2 = send_grads(["w_xo", "w_kv"], "attn")
    dmem_n = _matmul(dkv, wfull["w_kv"], mode="nt", out_dtype=F32, name="mm_d_mem", tm=512, tn=1024, tk=2048)
    d_norm_mem = _rms_bwd(None, dmem_n, memf, norm_mem, name="rms_mem_bwd", need_dx=False)
    dx1, d_norm_xattn, dw_q = _proj_rms_bwd(dq, dx2, x1, wfull["w_q"], norm_xattn + (tok + tok2), name="q_rms_bwd", h=h2)
    dp, dy_a, dy_b, d_b_gate, dw_mix, dw_in_gates = _gates_bwd_fused(dx1, p, y_a, y_b, bg_full, wfull["w_mix_out"],
                                                                    merged, h1, name="gates_bwd")
    grads["w_q"] = dw_q.astype(BF16)
    grads["w_mix_out"] = dw_mix.astype(BF16)
    grads["w_sgu_out"] = _matmul(sg, dy_b, mode="tn", out_dtype=BF16, name="mm_dw_sgu", tm=1024, tn=1024, tk=2048)
    dc, d_conv_ln_g, d_conv_ln_b, dw_conv = _conv_ln_bwd_fused(dy_a, c_conv, a_act, wfull["w_conv_out"], conv_ln_g,
                                                               conv_ln_b, name="conv_ln_bwd")
    grads["w_conv_out"] = dw_conv.astype(BF16)
    tok = send_grads(["w_q", "w_mix_out", "w_sgu_out", "w_conv_out"], "mixer")
    dp, d_sgu_w, d_sgu_b, d_sgu_ln_g, d_sgu_ln_b = _sgu_bwd(dp, dy_b, wfull["w_sgu_out"], p, vn, wm, wmt, sgu_bias,
                                                             sgu_ln_g + tok, name="sgu_bwd")
    sgw_ssem, sgw_rsem, sgw_src, sgw_land, tok = _copy_start("gather", [d_sgu_w], [(N_DEV, *d_sgu_w.shape)],
                                                             name="gather_sgu_w_start")
    cw_full = cw_full + tok[0:1, 0:1]
    dw_in = _matmul(h1, dp, mode="tn", out_dtype=BF16, name="mm_dw_in_sgu", tm=1024, tn=1024, tk=2048,
                    b_cols=(2 * d, 2 * d), out_into=(dw_in_gates, 2 * d))
    dp, d_conv_w, d_conv_b, dw_in = _conv_bwd(dp, dc, p, cw_full, h1, dw_in, bl=bl, s=s, name="conv_bwd")
    grads["w_in"] = dw_in
    tok = send_grads(["w_in"], "in")
    grad_x, d_norm_mix = _proj_rms_bwd(dp, dx1, xf, w_in_blocks, norm_mix + tok, name="in_proj_bwd")
    out = {}

    vec_names = ["norm_mix", "conv_b", "conv_ln_g", "conv_ln_b", "sgu_ln_g", "sgu_ln_b", "norm_xattn", "norm_mem",
                 "norm_ffn", "norm_final"]
    vec_grads = [d_norm_mix, d_conv_b, d_conv_ln_g, d_conv_ln_b, d_sgu_ln_g, d_sgu_ln_b, d_norm_xattn, d_norm_mem,
                 d_norm_ffn, d_norm_final]
    n_vec = len(vec_names)
    small_vec = jnp.concatenate([g.reshape(1, d) for g in vec_grads]
                                + [jnp.broadcast_to(loss_part, (1, d)), jnp.zeros((16 - n_vec - 1, d), F32)], axis=0)
    small_cols = jnp.concatenate([d_b_gate, d_conv_w], axis=0)
    parts_vec, parts_sb, parts_cols = _all_gather([small_vec, d_sgu_b, small_cols], name="gather_small_grads")
    _, sgw_land = _copy_wait("gather", sgw_ssem, sgw_rsem, sgw_src, sgw_land, parts_vec, name="gather_sgu_w_wait")
    parts_sw = lax.dynamic_update_index_in_dim(sgw_land[0], d_sgu_w, dev, 0)
    rep_names = vec_names + ["sgu_b", "sgu_w"]
    rep_shapes = [(1, d)] * n_vec + [d_sgu_b.shape, d_sgu_w.shape]
    states = [tuple(given[pre + n].reshape(shape) for pre in ("", "m_", "v_")) for n, shape in zip(rep_names, rep_shapes)]
    res_rep = _adamw_replicated([parts_vec, parts_sb, parts_sw], states, n_vec, name="adamw_small")
    for i, n in enumerate(rep_names):
        out[n] = [r.reshape(given[n].shape) for r in res_rep[1 + 4 * i:5 + 4 * i]]
    res_cols = _adamw_column_shards(parts_cols, dev_id, [(b_gate[0], m_b_gate[0], v_b_gate[0]),
                                                        (conv_w[0], m_conv_w[0], v_conv_w[0])], (0, 8),
                                    name="adamw_small_cols")
    out["b_gate"] = [r[None] for r in res_cols[0:4]]
    out["conv_w"] = [r[None] for r in res_cols[4:8]]

    done = res_rep[1]
    for names, ssem, rsem, srcs, lands in sent:
        srcs, lands = _copy_wait("scatter", ssem, rsem, srcs, lands, done, name=f"grads_{names[0]}_wait")
        for n, partials, landed in zip(names, srcs, lands):
            res = _adamw_shard(partials, landed, dev_id, shard_of(n), shard_of(n, "m_"), shard_of(n, "v_"),
                               name=f"adamw_{n}")
            done = res[0]
            out[n] = [(jnp.transpose(r) if n in transposed else r)[None] for r in res]

    order = ["norm_mix", "w_in", "b_gate", "conv_w", "conv_b", "conv_ln_g", "conv_ln_b", "w_conv_out", "sgu_ln_g",
             "sgu_ln_b", "sgu_w", "sgu_b", "w_sgu_out", "w_mix_out", "norm_xattn", "norm_mem", "w_q", "w_kv", "w_xo",
             "norm_ffn", "w_gu", "w_down", "norm_final"]
    loss = res_rep[0][0, 0]
    return (loss, grad_x.reshape(x.shape), *[out[n][0] for n in order], *[out[n][1] for n in order],
            *[out[n][2] for n in order], *[out[n][3] for n in order])
```

```python
import jax
import jax.numpy as jnp
from jax import lax
from jax.experimental import pallas as pl
from jax.experimental.pallas import tpu as pltpu

F32 = jnp.float32
BF16 = jnp.bfloat16
RMS_EPS = 1e-6
LN_EPS = 1e-5
CONV_WIDTH = 31
CONV_HALO = 32
CONV_ROWS = 128
CONV_COLS = 256
LANES = 128
SGU_CHUNK = 128
SGU_GROUPS = 8
SGU_TILE = 512
HEADS = 4
N_DEV = 8
ADAM_LR, ADAM_B1, ADAM_B2, ADAM_EPS, ADAM_WD, ADAM_STEP = 0.001, 0.9, 0.999, 1e-08, 0.01, 10
VMEM_LIMIT = 56 * 1024 * 1024
TOKEN_TILE = 256
ATTN_TILE = 1024
MESH_ID = pl.DeviceIdType.MESH

_GELU_K = 0.7978845608028654
_GELU_C = 0.044715


def _cparams(sem=None):
    return pltpu.CompilerParams(dimension_semantics=sem, vmem_limit_bytes=VMEM_LIMIT)


def _sigmoid(v):
    return 0.5 * jnp.tanh(0.5 * v) + 0.5


def _gelu(v):
    return 0.5 * v * (1.0 + jnp.tanh(_GELU_K * (v + _GELU_C * v * v * v)))


def _gelu_grad(v):
    th = jnp.tanh(_GELU_K * (v + _GELU_C * v * v * v))
    return 0.5 * (1.0 + th) + 0.5 * v * (1.0 - th * th) * _GELU_K * (1.0 + 3.0 * _GELU_C * v * v)


def _dot(a, b, dims):
    return lax.dot_general(a, b, (dims, ((), ())), preferred_element_type=F32)


_NN = ((1,), (0,))
_NT = ((1,), (1,))
_TN = ((0,), (0,))


def _matmul(a, b, *, mode, out_dtype, name, tm=512, tn=512, tk=512, col_blocks=None, b_cols=None, out_into=None):
    if mode == "nn":
        (m, k), (_, n) = a.shape, b.shape
    elif mode == "nt":
        (m, k), (n, _) = a.shape, b.shape
    else:
        (k, m), (_, n) = a.shape, b.shape
    b_first = 0
    if b_cols is not None:
        assert mode == "tn"
        b_first, n = b_cols
    tm, tn, tk = min(tm, m), min(tn, n), min(tk, k)
    assert b_first % tn == 0
    b_first //= tn
    assert m % tm == 0 and n % tn == 0 and k % tk == 0, (name, a.shape, b.shape, tm, tn, tk)
    nk = k // tk
    dims = {"nn": _NN, "nt": _NT, "tn": _TN}[mode]

    def body(*refs):
        a_ref, b_ref = refs[:2]
        o_ref = refs[3] if out_into is not None else refs[2]
        part = _dot(a_ref[...].astype(BF16), b_ref[...].astype(BF16), dims)
        if nk == 1:
            o_ref[...] = part.astype(out_dtype)
        else:
            acc_ref = refs[-1]
            kk = pl.program_id(2)

            @pl.when(kk == 0)
            def _():
                acc_ref[...] = part

            @pl.when(kk > 0)
            def _():
                acc_ref[...] += part

            @pl.when(kk == nk - 1)
            def _():
                o_ref[...] = acc_ref[...].astype(out_dtype)

    resident = dict(pipeline_mode=pl.Buffered(1)) if (n == tn and nk == 1 and mode != "tn" and m > tm) else {}
    if mode == "nn":
        a_spec = pl.BlockSpec((tm, tk), lambda i, j, kk: (i, kk))
        b_spec = pl.BlockSpec((tk, tn), lambda i, j, kk: (kk, j), **resident)
    elif mode == "nt":
        a_spec = pl.BlockSpec((tm, tk), lambda i, j, kk: (i, kk))
        b_spec = pl.BlockSpec((tn, tk), lambda i, j, kk: (j, kk), **resident)
    else:
        a_spec = pl.BlockSpec((tk, tm), lambda i, j, kk: (kk, i))
        b_spec = pl.BlockSpec((tk, tn), lambda i, j, kk: (kk, j + b_first))
    in_specs, args = [a_spec, b_spec], [a, b]
    out_shape = [jax.ShapeDtypeStruct((m, n), out_dtype)]
    out_specs = [pl.BlockSpec((tm, tn), lambda i, j, kk: (i, j))]
    if col_blocks is not None:
        assert (n // col_blocks) % tn == 0
        per = n // col_blocks // tn
        out_shape = [jax.ShapeDtypeStruct((col_blocks, m, n // col_blocks), out_dtype)]
        out_specs = [pl.BlockSpec((None, tm, tn), lambda i, j, kk: (j // per, i, j % per))]
    aliases = {}
    if out_into is not None:
        target, first = out_into
        assert col_blocks is None and first % tn == 0 and target.dtype == out_dtype
        in_specs.append(pl.BlockSpec(memory_space=pl.ANY))
        args.append(target)
        aliases = {len(args) - 1: 0}
        out_shape = [jax.ShapeDtypeStruct(target.shape, target.dtype)]
        out_specs = [pl.BlockSpec((tm, tn), lambda i, j, kk: (i, j + first // tn))]
    res = pl.pallas_call(
        body, name=name, grid=(m // tm, n // tn, nk), in_specs=in_specs, out_specs=out_specs, out_shape=out_shape,
        scratch_shapes=[pltpu.VMEM((tm, tn), F32)] if nk > 1 else [], input_output_aliases=aliases,
        compiler_params=_cparams(("parallel", "parallel", "arbitrary")),
    )(*args)
    return res[0]


def _row_call(name, t, tm, rows_in, residents, rows_out, accs, body):
    n_in, n_res, n_out, n_acc = len(rows_in), len(residents), len(rows_out), len(accs)
    steps = t // tm
    assert t % tm == 0
    narrow = [i for i, a in enumerate(accs) if a[1] != F32]

    def kernel_body(*refs):
        in_refs, res_refs = refs[:n_in], refs[n_in:n_in + n_res]
        out_refs = refs[n_in + n_res:n_in + n_res + n_out]
        acc_out = list(refs[n_in + n_res + n_out:n_in + n_res + n_out + n_acc])
        scratch = refs[n_in + n_res + n_out + n_acc:]
        acc_refs = list(acc_out)
        for s_ref, i in zip(scratch, narrow):
            acc_refs[i] = s_ref
        if accs:
            @pl.when(pl.program_id(0) == 0)
            def _():
                for acc in acc_refs:
                    acc[...] = jnp.zeros_like(acc)
        body(in_refs, res_refs, out_refs, acc_refs)
        if narrow:
            @pl.when(pl.program_id(0) == steps - 1)
            def _():
                for i in narrow:
                    acc_out[i][...] = acc_refs[i][...].astype(acc_out[i].dtype)

    once = dict(pipeline_mode=pl.Buffered(1)) if steps > 1 else {}
    in_specs = [pl.BlockSpec((tm, cols), lambda i, cb=cb: (i, cb)) for _, cols, cb in rows_in]
    in_specs += [pl.BlockSpec(r.shape, lambda i, nd=r.ndim: (0,) * nd, **once) for r in residents]
    out_specs = [pl.BlockSpec((tm, cols), lambda i, cb=cb: (i, cb)) for _, cols, cb, _ in rows_out]
    out_specs += [pl.BlockSpec(a[0], lambda i, nd=len(a[0]), cb=(a[3] if len(a) == 4 else 0): (0,) * (nd - 1) + (cb,))
                  for a in accs]
    out_shape = [jax.ShapeDtypeStruct((t, total), dt) for total, _, _, dt in rows_out]
    out_shape += [jax.ShapeDtypeStruct((a[0][0], a[2]) if len(a) == 4 else a[0], a[1]) for a in accs]
    return pl.pallas_call(
        kernel_body, name=name, grid=(steps,), in_specs=in_specs, out_specs=out_specs, out_shape=out_shape,
        scratch_shapes=[pltpu.VMEM(accs[i][0], F32) for i in narrow],
        compiler_params=_cparams(("arbitrary",) if accs else ("parallel",)),
    )(*[a for a, _, _ in rows_in], *residents)


def _rms_apply(xv, gain):
    return xv * lax.rsqrt(jnp.mean(xv * xv, axis=-1, keepdims=True) + RMS_EPS) * gain


def _rms_grad(dres, dh, xv, gain):
    r = lax.rsqrt(jnp.mean(xv * xv, axis=-1, keepdims=True) + RMS_EPS)
    xhat = xv * r
    dxh = dh * gain
    dx = dres + r * (dxh - xhat * jnp.mean(dxh * xhat, axis=-1, keepdims=True))
    return dx, jnp.sum(dh * xhat, axis=0, keepdims=True)


def _in_proj_gather(xf, gain, w_shard, *, name):
    t, d = xf.shape
    cb = w_shard.shape[1]
    tm = min(1024, t)
    steps = t // tm
    mx, my, _ = _mesh_pos()
    order = jnp.stack([2 * mx + my, 2 * (1 - mx) + my, 2 * mx + (1 - my), 2 * (1 - mx) + (1 - my)]).astype(jnp.int32)

    def body(order_ref, x_ref, g_ref, ws_ref, h_ref, p_ref, wout_ref, w_ref, send_sems, recv_sems, own_sem):
        ps, i = pl.program_id(0), pl.program_id(1)
        x, y, c = _mesh_pos()
        me, sib = (x, y, c), (x, y, 1 - c)
        chips = [(1 - x, y), (x, 1 - y), (1 - x, 1 - y)]

        def copy(k, block, to, from_shard=False):
            return pltpu.make_async_remote_copy(
                src_ref=ws_ref if from_shard else w_ref.at[_dev_index(block)], dst_ref=w_ref.at[_dev_index(block)],
                send_sem=send_sems.at[k], recv_sem=recv_sems.at[k], device_id=to, device_id_type=MESH_ID)

        own = pltpu.make_async_copy(ws_ref, w_ref.at[_dev_index(me)], own_sem)
        first = [copy(0, me, sib, True)] + [copy(1 + j, me, (*chip, c), True) for j, chip in enumerate(chips)]
        passed = [copy(4 + j, (*chip, c), sib) for j, chip in enumerate(chips)]

        @pl.when(jnp.logical_and(ps == 0, i == 0))
        def _():
            own.start()
            for cp in first:
                cp.start()
            own.wait()
            copy(0, sib, me).wait_recv()

        for j, chip in enumerate(chips):
            @pl.when(jnp.logical_and(ps == j + 1, i == 0))
            def _(j=j, chip=chip):
                copy(1 + j, (*chip, c), me).wait_recv()
                passed[j].start()
                copy(4 + j, (*chip, 1 - c), me).wait_recv()

        h = _rms_apply(x_ref[...], g_ref[...]).astype(BF16)
        h_ref[...] = h
        chip_id = order_ref[ps]
        p_ref[:, 0:cb] = _dot(h, w_ref[2 * chip_id], _NN).astype(BF16)
        p_ref[:, cb:2 * cb] = _dot(h, w_ref[2 * chip_id + 1], _NN).astype(BF16)

        @pl.when(jnp.logical_and(ps == 3, i == steps - 1))
        def _():
            for cp in first + passed:
                cp.wait_send()
            keep = pltpu.make_async_copy(w_ref, wout_ref, own_sem)
            keep.start()
            keep.wait()

    gs = pltpu.PrefetchScalarGridSpec(
        num_scalar_prefetch=1, grid=(4, steps),
        in_specs=[pl.BlockSpec((tm, d), lambda ps, i, o: (i, 0)), pl.BlockSpec((1, d), lambda ps, i, o: (0, 0)),
                  pl.BlockSpec(memory_space=pl.ANY)],
        out_specs=[pl.BlockSpec((tm, d), lambda ps, i, o: (jnp.where(ps == 0, i, steps - 1), 0)),
                   pl.BlockSpec((tm, 2 * cb), lambda ps, i, o: (i, o[ps])), pl.BlockSpec(memory_space=pl.ANY)],
        scratch_shapes=[pltpu.VMEM((N_DEV, d, cb), BF16), pltpu.SemaphoreType.DMA((7,)), pltpu.SemaphoreType.DMA((7,)),
                        pltpu.SemaphoreType.DMA(())])
    return pl.pallas_call(
        body, name=name, grid_spec=gs,
        out_shape=[jax.ShapeDtypeStruct((t, d), BF16), jax.ShapeDtypeStruct((t, N_DEV * cb), BF16),
                   jax.ShapeDtypeStruct((N_DEV, d, cb), BF16)],
        compiler_params=_cparams(("arbitrary", "arbitrary")))(order, xf, gain, w_shard)


def _mix_out(p, y_a, y_b, b_gate, xf, w_mix, gain, w_q, *, name):
    t, d = xf.shape

    def body(ins, res, outs, accs):
        ga_ref, gb_ref, ya_ref, yb_ref, x_ref = ins
        bg_ref, wm_ref, g_ref, wq_ref = res
        m_ref, x1_ref, h_ref, q_ref = outs
        sa = _sigmoid(ga_ref[...].astype(F32) + bg_ref[0:1, :])
        sb = _sigmoid(gb_ref[...].astype(F32) + bg_ref[1:2, :])
        merged = (sa * ya_ref[...].astype(F32) + sb * yb_ref[...].astype(F32)).astype(BF16)
        m_ref[...] = merged
        x1 = x_ref[...] + _dot(merged, wm_ref[...], _NN)
        x1_ref[...] = x1
        h = _rms_apply(x1, g_ref[...]).astype(BF16)
        h_ref[...] = h
        q_ref[...] = _dot(h, wq_ref[...], _NN).astype(BF16)

    return _row_call(name, t, min(512, t), [(p, d, 4), (p, d, 5), (y_a, d, 0), (y_b, d, 0), (xf, d, 0)],
                     [b_gate, w_mix, gain, w_q], [(d, d, 0, BF16), (d, d, 0, F32), (d, d, 0, BF16), (d, d, 0, BF16)], [], body)


def _ffn_fwd(h3, x2, target, w_gu_t, w_down, gain, *, name):
    t, d = x2.shape
    f2 = w_gu_t.shape[0]
    f = f2 // 2
    half = f // 2

    def body(ins, res, outs, accs):
        h_ref, x2_ref, t_ref = ins
        wgu_ref, wd_ref, g_ref = res
        gu_ref, act_ref, dx_ref = outs
        loss_ref, dg_ref = accs
        h = h_ref[...]
        x3 = x2_ref[...]
        for c0 in (0, half):
            gt = _dot(h, wgu_ref[c0:c0 + half, :], _NT).astype(BF16)
            up = _dot(h, wgu_ref[f + c0:f + c0 + half, :], _NT).astype(BF16)
            gu_ref[:, c0:c0 + half] = gt
            gu_ref[:, f + c0:f + c0 + half] = up
            gtf = gt.astype(F32)
            act = (gtf * _sigmoid(gtf) * up.astype(F32)).astype(BF16)
            act_ref[:, c0:c0 + half] = act
            x3 = x3 + _dot(act, wd_ref[c0:c0 + half, :], _NN)
        g = g_ref[...]
        r = lax.rsqrt(jnp.mean(x3 * x3, axis=-1, keepdims=True) + RMS_EPS)
        xhat = x3 * r
        err = xhat * g - t_ref[...]
        loss_ref[...] += 0.5 * jnp.sum(jnp.mean(err * err, axis=-1, keepdims=True), axis=0, keepdims=True)
        dy = err * (1.0 / d)
        dg_ref[...] += jnp.sum(dy * xhat, axis=0, keepdims=True)
        dxh = dy * g
        dx_ref[...] = r * (dxh - xhat * jnp.mean(dxh * xhat, axis=-1, keepdims=True))

    return _row_call(name, t, min(256, t), [(h3, d, 0), (x2, d, 0), (target, d, 0)], [w_gu_t, w_down, gain],
                     [(f2, f2, 0, BF16), (f, f, 0, BF16), (d, d, 0, F32)], [((1, 1), F32), ((1, d), F32)], body)


def _ffn_bwd(dx3, gu, x2, w_down, w_gu_t, gain, w_xo, *, name):
    t, d = x2.shape
    f2 = w_gu_t.shape[0]
    f = f2 // 2
    half = f // 2

    def body(ins, res, outs, accs):
        dx3_ref, gu_ref, x2_ref = ins
        wd_ref, wgu_ref, g_ref, wxo_ref = res
        dgu_ref, dx2_ref, do_ref = outs
        (dg_ref,) = accs
        dx3v = dx3_ref[...]
        dxb = dx3v.astype(BF16)
        dh = jnp.zeros(dx3v.shape, F32)
        for c0 in (0, half):
            dact = _dot(dxb, wd_ref[c0:c0 + half, :], _NT)
            gt = gu_ref[:, c0:c0 + half].astype(F32)
            up = gu_ref[:, f + c0:f + c0 + half].astype(F32)
            sg = _sigmoid(gt)
            dgt = (dact * up * sg * (1.0 + gt * (1.0 - sg))).astype(BF16)
            dup = (dact * gt * sg).astype(BF16)
            dgu_ref[:, c0:c0 + half] = dgt
            dgu_ref[:, f + c0:f + c0 + half] = dup
            dh = dh + _dot(dgt, wgu_ref[c0:c0 + half, :], _NN) + _dot(dup, wgu_ref[f + c0:f + c0 + half, :], _NN)
        dx2, dg = _rms_grad(dx3v, dh, x2_ref[...], g_ref[...])
        dx2_ref[...] = dx2
        dg_ref[...] += dg
        do_ref[...] = _dot(dx2.astype(BF16), wxo_ref[...], _NT).astype(BF16)

    return _row_call(name, t, min(256, t), [(dx3, d, 0), (gu, f2, 0), (x2, d, 0)], [w_down, w_gu_t, gain, w_xo],
                     [(f2, f2, 0, BF16), (d, d, 0, F32), (d, d, 0, BF16)], [((1, d), F32)], body)


def _proj_rms_bwd(dy, dres, x, w, gain, *, name, h=None):
    t, d = x.shape
    k = dy.shape[1]

    def body(ins, res, outs, accs):
        dy_ref, dres_ref, x_ref = ins[:3]
        w_ref, g_ref = res
        if h is not None:
            accs[1][...] += _dot(ins[3][...], dy_ref[...], _TN)
        if w.ndim == 3:
            cb = w.shape[2]
            dh = _dot(dy_ref[:, 0:cb], w_ref[0], _NT)
            for j in range(1, w.shape[0]):
                dh = dh + _dot(dy_ref[:, j * cb:(j + 1) * cb], w_ref[j], _NT)
        else:
            dh = _dot(dy_ref[...], w_ref[...], _NT)
        dx, dg = _rms_grad(dres_ref[...], dh, x_ref[...], g_ref[...])
        outs[0][...] = dx
        accs[0][...] += dg

    rows_in = [(dy, k, 0), (dres, d, 0), (x, d, 0)] + ([(h, d, 0)] if h is not None else [])
    accs = [((1, d), F32)] + ([((d, k), BF16)] if h is not None else [])
    tm = 1024 if w.ndim == 2 else 512
    return _row_call(name, t, min(tm, t), rows_in, [w, gain], [(d, d, 0, F32)], accs, body)


def _gates_bwd_fused(dx1, p, y_a, y_b, b_gate, w_mix, merged, h1, *, name):
    t, d = y_a.shape

    def body(ins, res, outs, accs):
        dx_ref, ga_ref, gb_ref, ya_ref, yb_ref, m_ref, h1_ref = ins
        bg_ref, wm_ref = res
        dp_ref, dya_ref, dyb_ref = outs
        dbg_ref, dwm_ref, dwin_ref = accs
        dxb = dx_ref[...].astype(BF16)
        dwm_ref[...] += _dot(m_ref[...], dxb, _TN)
        dm = _dot(dxb, wm_ref[...], _NT)
        sa = _sigmoid(ga_ref[...].astype(F32) + bg_ref[0:1, :])
        sb = _sigmoid(gb_ref[...].astype(F32) + bg_ref[1:2, :])
        dya_ref[...] = (dm * sa).astype(BF16)
        dyb_ref[...] = (dm * sb).astype(BF16)
        dga = dm * ya_ref[...].astype(F32) * sa * (1.0 - sa)
        dgb = dm * yb_ref[...].astype(F32) * sb * (1.0 - sb)
        dp_ref[:, 0:d] = dga.astype(BF16)
        dp_ref[:, d:2 * d] = dgb.astype(BF16)
        dbg_ref[0:1, :] += jnp.sum(dga, axis=0, keepdims=True)
        dbg_ref[1:2, :] += jnp.sum(dgb, axis=0, keepdims=True)
        dwin_ref[...] += _dot(h1_ref[...], dp_ref[...], _TN)

    return _row_call(name, t, min(256, t),
                     [(dx1, d, 0), (p, d, 4), (p, d, 5), (y_a, d, 0), (y_b, d, 0), (merged, d, 0), (h1, d, 0)],
                     [b_gate, w_mix], [(p.shape[1], 2 * d, 2, BF16), (d, d, 0, BF16), (d, d, 0, BF16)],
                     [((8, d), F32), ((d, d), BF16), ((d, 2 * d), BF16, p.shape[1], 2)], body)


def _conv_ln_bwd_fused(dy_a, c, a_act, w_conv_out, ln_g, ln_b, *, name):
    t, d = c.shape

    def body(ins, res, outs, accs):
        dy_ref, c_ref, act_ref = ins
        w_ref, lg_ref, lb_ref = res
        dlg_ref, dlb_ref, dw_ref = accs
        dw_ref[...] += _dot(act_ref[...], dy_ref[...], _TN)
        dact = _dot(dy_ref[...], w_ref[...], _NT)
        cv = c_ref[...].astype(F32)
        g = lg_ref[...]
        mu = jnp.mean(cv, axis=-1, keepdims=True)
        dv = cv - mu
        rstd = lax.rsqrt(jnp.mean(dv * dv, axis=-1, keepdims=True) + LN_EPS)
        chat = dv * rstd
        aln = chat * g + lb_ref[...]
        sg = _sigmoid(aln)
        daln = dact * (sg * (1.0 + aln * (1.0 - sg)))
        dlb_ref[...] += jnp.sum(daln, axis=0, keepdims=True)
        dlg_ref[...] += jnp.sum(daln * chat, axis=0, keepdims=True)
        dchat = daln * g
        dc = rstd * (dchat - jnp.mean(dchat, axis=-1, keepdims=True)
                     - chat * jnp.mean(dchat * chat, axis=-1, keepdims=True))
        outs[0][...] = dc.astype(BF16)

    return _row_call(name, t, min(1024, t), [(dy_a, d, 0), (c, d, 0), (a_act, d, 0)], [w_conv_out, ln_g, ln_b],
                     [(d, d, 0, BF16)], [((1, d), F32), ((1, d), F32), ((d, d), BF16)], body)


def _row_spec(tt, cols, col_block=0):
    return pl.BlockSpec((tt, cols), lambda i: (i, col_block))


def _const_spec(shape):
    return pl.BlockSpec(shape, lambda *_: (0,) * len(shape))


def _rms_fwd(x, gain, *, name):
    t, d = x.shape
    tt = min(TOKEN_TILE, t)

    def body(x_ref, g_ref, h_ref):
        xv = x_ref[...]
        r = lax.rsqrt(jnp.mean(xv * xv, axis=-1, keepdims=True) + RMS_EPS)
        h_ref[...] = (xv * r * g_ref[...]).astype(BF16)

    return pl.pallas_call(
        body, name=name, grid=(t // tt,), in_specs=[_row_spec(tt, d), _const_spec((1, d))],
        out_specs=_row_spec(tt, d), out_shape=jax.ShapeDtypeStruct((t, d), BF16),
        compiler_params=_cparams(("parallel",)))(x, gain)


def _rms_gain_grad(dh, x, *, name):
    t, d = x.shape
    tt = min(TOKEN_TILE, t)

    def body(dh_ref, x_ref, dg_ref):
        @pl.when(pl.program_id(0) == 0)
        def _():
            dg_ref[...] = jnp.zeros_like(dg_ref)

        xv = x_ref[...]
        xhat = xv * lax.rsqrt(jnp.mean(xv * xv, axis=-1, keepdims=True) + RMS_EPS)
        dg_ref[...] += jnp.sum(dh_ref[...].astype(F32) * xhat, axis=0, keepdims=True)

    rs = _row_spec(tt, d)
    return pl.pallas_call(
        body, name=name, grid=(t // tt,), in_specs=[rs, rs], out_specs=_const_spec((1, d)),
        out_shape=jax.ShapeDtypeStruct((1, d), F32), compiler_params=_cparams(("arbitrary",)))(dh, x)


SUBLANES = 8
SHIFT_ROWS = 40


def _conv_apply(sbuf_ref, w_ref, out_ref, tt, offsets, bias_ref=None):
    d = out_ref.shape[1]
    for cc in range(d // LANES):
        cs = slice(cc * LANES, (cc + 1) * LANES)
        taps = [jnp.broadcast_to(w_ref[k:k + 1, cs], (SUBLANES, LANES)) for k in range(CONV_WIDTH)]
        bias = None if bias_ref is None else jnp.broadcast_to(bias_ref[:, cs], (SUBLANES, LANES))

        def row_body(r, carry, cs=cs, taps=taps, bias=bias):
            r0 = pl.multiple_of(r * CONV_ROWS, CONV_ROWS)
            for q in range(CONV_ROWS // SUBLANES):
                acc = _tap(sbuf_ref, r0 + q * SUBLANES, cs, offsets[0]) * taps[0]
                for k in range(1, CONV_WIDTH):
                    acc = acc + _tap(sbuf_ref, r0 + q * SUBLANES, cs, offsets[k]) * taps[k]
                if bias is not None:
                    acc = acc + bias
                out_ref[pl.ds(r0 + q * SUBLANES, SUBLANES), cs] = acc
            return carry

        lax.fori_loop(0, tt // CONV_ROWS, row_body, 0)


def _fill_shifts(sbuf_ref, rows):
    d = sbuf_ref.shape[2]
    assert rows % SHIFT_ROWS == 0

    def row_body(i, carry):
        r0 = pl.multiple_of(i * SHIFT_ROWS, SUBLANES)
        for cc in range(d // CONV_COLS):
            cs = slice(cc * CONV_COLS, (cc + 1) * CONV_COLS)
            win = sbuf_ref[0, pl.ds(r0, SHIFT_ROWS + SUBLANES), cs]
            for sh in range(1, SUBLANES):
                sbuf_ref[sh, pl.ds(r0, SHIFT_ROWS), cs] = win[sh:sh + SHIFT_ROWS, :]
        return carry

    lax.fori_loop(0, rows // SHIFT_ROWS, row_body, 0)


def _tap(sbuf_ref, r0, cs, offset):
    sh = offset % SUBLANES
    return sbuf_ref[sh, pl.ds(pl.multiple_of(r0 + (offset - sh), SUBLANES), SUBLANES), cs]


def _conv_specs(bl, s, tt, d, col_a, col_g):
    nj = s // tt
    per = tt // CONV_HALO
    main_a = pl.BlockSpec((tt, d), lambda b, j: (b * nj + j, col_a))
    main_g = pl.BlockSpec((tt, d), lambda b, j: (b * nj + j, col_g))
    prev = lambda b, j: jnp.maximum((b * nj + j) * per - 1, 0)
    halo_a = pl.BlockSpec((CONV_HALO, d), lambda b, j: (prev(b, j), col_a))
    halo_g = pl.BlockSpec((CONV_HALO, d), lambda b, j: (prev(b, j), col_g))
    return main_a, main_g, halo_a, halo_g


def _fill_glu(sbuf_ref, a_ref, g_ref, ha_ref, hg_ref, tt):
    first = pl.program_id(1) == 0
    ha = ha_ref[...].astype(F32)
    hg = hg_ref[...].astype(F32)
    sbuf_ref[0, pl.ds(0, CONV_HALO), :] = jnp.where(first, 0.0, ha * _sigmoid(hg))
    av = a_ref[...].astype(F32)
    gv = g_ref[...].astype(F32)
    sbuf_ref[0, pl.ds(CONV_HALO, tt), :] = av * _sigmoid(gv)
    _fill_shifts(sbuf_ref, tt + CONV_HALO - SUBLANES)


def _conv_fwd(p, conv_w, conv_b, ln_g, ln_b, *, bl, s, name):
    t = p.shape[0]
    d = conv_w.shape[1]
    tt = min(TOKEN_TILE, s)
    off = CONV_HALO - (CONV_WIDTH - 1)

    def body(a_ref, g_ref, ha_ref, hg_ref, w_ref, b_ref, lg_ref, lb_ref, c_ref, act_ref, sbuf_ref, cbuf_ref):
        _fill_glu(sbuf_ref, a_ref, g_ref, ha_ref, hg_ref, tt)

        _conv_apply(sbuf_ref, w_ref, cbuf_ref, tt, [off + k for k in range(CONV_WIDTH)], bias_ref=b_ref)
        cv = cbuf_ref[...]
        c_ref[...] = cv.astype(BF16)
        mu = jnp.mean(cv, axis=-1, keepdims=True)
        dv = cv - mu
        rstd = lax.rsqrt(jnp.mean(dv * dv, axis=-1, keepdims=True) + LN_EPS)
        aln = dv * rstd * lg_ref[...] + lb_ref[...]
        act_ref[...] = (aln * _sigmoid(aln)).astype(BF16)

    main_a, main_g, halo_a, halo_g = _conv_specs(bl, s, tt, d, 0, 1)
    out_spec = pl.BlockSpec((tt, d), lambda b, j: (b * (s // tt) + j, 0))
    return pl.pallas_call(
        body, name=name, grid=(bl, s // tt),
        in_specs=[main_a, main_g, halo_a, halo_g, _const_spec((CONV_HALO, d)), _const_spec((1, d)), _const_spec((1, d)),
                  _const_spec((1, d))],
        out_specs=[out_spec, out_spec],
        out_shape=[jax.ShapeDtypeStruct((t, d), BF16), jax.ShapeDtypeStruct((t, d), BF16)],
        scratch_shapes=[pltpu.VMEM((SUBLANES, tt + CONV_HALO, d), F32), pltpu.VMEM((tt, d), F32)],
        compiler_params=_cparams(("parallel", "parallel")))(p, p, p, p, conv_w, conv_b, ln_g, ln_b)


def _conv_bwd(dp, dc, p, conv_w, h1, dw_in, *, bl, s, name):
    t = p.shape[0]
    d = conv_w.shape[1]
    tt = min(TOKEN_TILE, s)
    nj = s // tt
    per = tt // CONV_HALO
    off = CONV_HALO - (CONV_WIDTH - 1)
    last_blk = t // CONV_HALO - 1

    def body(dp_in, dc_ref, dcn_ref, a_ref, g_ref, ha_ref, hg_ref, w_ref, h1_ref, dwin_in, dp_ref, dw_ref, db_ref,
             dwin_out, gbuf_ref, dbuf_ref, dglu_ref, acc_ref, dwin_ref):
        del dp_in, dwin_in
        b, j = pl.program_id(0), pl.program_id(1)
        start = jnp.logical_and(b == 0, j == 0)
        end = jnp.logical_and(b == bl - 1, j == nj - 1)

        @pl.when(start)
        def _():
            acc_ref[...] = jnp.zeros_like(acc_ref)
            db_ref[...] = jnp.zeros_like(db_ref)
            dwin_ref[...] = jnp.zeros_like(dwin_ref)

        _fill_glu(gbuf_ref, a_ref, g_ref, ha_ref, hg_ref, tt)
        dcv = dc_ref[...].astype(F32)
        dbuf_ref[0, pl.ds(0, tt), :] = dcv
        dbuf_ref[0, pl.ds(tt, CONV_HALO), :] = jnp.where(j == nj - 1, 0.0, dcn_ref[...].astype(F32))
        _fill_shifts(dbuf_ref, tt + CONV_HALO - SUBLANES)
        db_ref[...] += jnp.sum(dcv, axis=0, keepdims=True)

        for cc in range(d // LANES):
            cs = slice(cc * LANES, (cc + 1) * LANES)

            def row_body(r, accs, cs=cs):
                r0 = pl.multiple_of(r * CONV_ROWS, CONV_ROWS)
                accs = list(accs)
                for q in range(CONV_ROWS // SUBLANES):
                    dcw = dbuf_ref[0, pl.ds(r0 + q * SUBLANES, SUBLANES), cs]
                    for k in range(CONV_WIDTH):
                        accs[k] = accs[k] + dcw * _tap(gbuf_ref, r0 + q * SUBLANES, cs, off + k)
                return tuple(accs)

            zero = jnp.zeros((SUBLANES, LANES), F32)
            accs = lax.fori_loop(0, tt // CONV_ROWS, row_body, (zero,) * CONV_WIDTH)
            for k in range(CONV_WIDTH):
                acc_ref[k, :, cs] += accs[k]

        _conv_apply(dbuf_ref, w_ref, dglu_ref, tt, [CONV_WIDTH - 1 - k for k in range(CONV_WIDTH)])
        dglu = dglu_ref[...]
        av = a_ref[...].astype(F32)
        sg = _sigmoid(g_ref[...].astype(F32))
        dp_ref[:, 0:d] = (dglu * sg).astype(BF16)
        dp_ref[:, d:2 * d] = (dglu * av * sg * (1.0 - sg)).astype(BF16)
        dwin_ref[...] += _dot(h1_ref[...], dp_ref[...], _TN)

        @pl.when(end)
        def _():
            for k in range(CONV_WIDTH):
                dw_ref[k:k + 1, :] = jnp.sum(acc_ref[k], axis=0, keepdims=True)
            dw_ref[CONV_WIDTH:CONV_HALO, :] = jnp.zeros((CONV_HALO - CONV_WIDTH, d), F32)
            dwin_out[...] = dwin_ref[...].astype(dwin_out.dtype)

    main_a, main_g, halo_a, halo_g = _conv_specs(bl, s, tt, d, 0, 1)
    dc_main = pl.BlockSpec((tt, d), lambda b, j: (b * nj + j, 0))
    dc_next = pl.BlockSpec((CONV_HALO, d), lambda b, j: (jnp.minimum((b * nj + j + 1) * per, last_blk), 0))
    hbm = pl.BlockSpec(memory_space=pl.ANY)
    return pl.pallas_call(
        body, name=name, grid=(bl, nj),
        in_specs=[hbm, dc_main, dc_next, main_a, main_g, halo_a, halo_g, _const_spec((CONV_HALO, d)), dc_main, hbm],
        out_specs=[pl.BlockSpec((tt, 2 * d), lambda b, j: (b * nj + j, 0)), _const_spec((CONV_HALO, d)), _const_spec((1, d)),
                   _const_spec((d, 2 * d))],
        out_shape=[jax.ShapeDtypeStruct(dp.shape, dp.dtype), jax.ShapeDtypeStruct((CONV_HALO, d), F32),
                   jax.ShapeDtypeStruct((1, d), F32), jax.ShapeDtypeStruct(dw_in.shape, dw_in.dtype)],
        scratch_shapes=[pltpu.VMEM((SUBLANES, tt + CONV_HALO, d), F32), pltpu.VMEM((SUBLANES, tt + CONV_HALO, d), F32),
                        pltpu.VMEM((tt, d), F32), pltpu.VMEM((CONV_HALO, SUBLANES, d), F32), pltpu.VMEM((d, 2 * d), F32)],
        input_output_aliases={0: 0, 9: 3},
        compiler_params=_cparams(("arbitrary", "arbitrary")))(dp, dc, dc, p, p, p, p, conv_w, h1, dw_in)


def _sgu_stats(bv):
    gv = _gelu(bv)
    mu = jnp.mean(gv, axis=-1, keepdims=True)
    dv = gv - mu
    rstd = lax.rsqrt(jnp.mean(dv * dv, axis=-1, keepdims=True) + LN_EPS)
    return dv * rstd, rstd


def _sgu_fwd(p, wm, bias, ln_g, ln_b, *, name):
    t = p.shape[0]
    d = ln_g.shape[1]
    tt = SGU_TILE
    gd = d // SGU_GROUPS

    def body(u_ref, v_ref, wm_ref, bias_ref, lg_ref, lb_ref, sg_ref, vn_ref):
        u = _gelu(u_ref[...].astype(F32))
        vhat, _ = _sgu_stats(v_ref[...].astype(F32))
        vb = (vhat * lg_ref[...] + lb_ref[...]).astype(BF16)
        vn_ref[...] = vb
        for ci in range(tt // SGU_CHUNK):
            rows = slice(ci * SGU_CHUNK, (ci + 1) * SGU_CHUNK)
            for g in range(SGU_GROUPS):
                gs = slice(g * gd, (g + 1) * gd)
                z = _dot(wm_ref[g], vb[rows, gs], _NN) + bias_ref[g]
                sg_ref[rows, gs] = (u[rows, gs] * z).astype(BF16)

    rs = _row_spec(tt, d)
    return pl.pallas_call(
        body, name=name, grid=(t // tt,),
        in_specs=[_row_spec(tt, d, 2), _row_spec(tt, d, 3), _const_spec(wm.shape), _const_spec(bias.shape),
                  _const_spec((1, d)), _const_spec((1, d))],
        out_specs=[rs, rs], out_shape=[jax.ShapeDtypeStruct((t, d), BF16), jax.ShapeDtypeStruct((t, d), BF16)],
        compiler_params=_cparams(("parallel",)))(p, p, wm, bias, ln_g, ln_b)


def _sgu_bwd(dp, dy_b, w_out, p, vn, wm, wmt, bias, ln_g, *, name):
    t = p.shape[0]
    d = ln_g.shape[1]
    tt = SGU_TILE
    ck = SGU_CHUNK
    gd = d // SGU_GROUPS
    nsteps = t // tt

    def body(dp_in, dyb_ref, wout_ref, u_ref, v_ref, vn_ref, wm_ref, wmt_ref, bias_ref, lg_ref,
             dp_ref, dw_ref, dbs_ref, dlg_ref, dlb_ref, dz_acc):
        del dp_in
        i = pl.program_id(0)

        @pl.when(i == 0)
        def _():
            dw_ref[...] = jnp.zeros_like(dw_ref)
            dlg_ref[...] = jnp.zeros_like(dlg_ref)
            dlb_ref[...] = jnp.zeros_like(dlb_ref)
            dz_acc[...] = jnp.zeros_like(dz_acc)

        bu = u_ref[...].astype(F32)
        bv = v_ref[...].astype(F32)
        u = _gelu(bu)
        vhat, rstd = _sgu_stats(bv)
        vb = vn_ref[...]
        dsg = _dot(dyb_ref[...], wout_ref[...], _NT)
        row = lax.broadcasted_iota(jnp.int32, (ck, ck), 0)
        col = lax.broadcasted_iota(jnp.int32, (ck, ck), 1)
        causal = col <= row
        du_rows, dv_rows = [], []
        for ci in range(tt // ck):
            rows = slice(ci * ck, (ci + 1) * ck)
            du_parts, dv_parts = [], []
            for g in range(SGU_GROUPS):
                gs = slice(g * gd, (g + 1) * gd)
                z = _dot(wm_ref[g], vb[rows, gs], _NN) + bias_ref[g]
                du_parts.append(dsg[rows, gs] * z)
                dz = dsg[rows, gs] * u[rows, gs]
                dz_acc[:, gs] += dz
                dzb = dz.astype(BF16)
                dw_ref[g] += jnp.where(causal, _dot(dzb, vb[rows, gs], _NT), 0.0)
                dv_parts.append(_dot(wmt_ref[g], dzb, _NN))
            du_rows.append(jnp.concatenate(du_parts, axis=1))
            dv_rows.append(jnp.concatenate(dv_parts, axis=1))
        du = jnp.concatenate(du_rows, axis=0)
        dv = jnp.concatenate(dv_rows, axis=0)
        dp_ref[:, 0:d] = (du * _gelu_grad(bu)).astype(BF16)
        dlb_ref[...] += jnp.sum(dv, axis=0, keepdims=True)
        dlg_ref[...] += jnp.sum(dv * vhat, axis=0, keepdims=True)
        dvh = dv * lg_ref[...]
        dgv = rstd * (dvh - jnp.mean(dvh, axis=-1, keepdims=True) - vhat * jnp.mean(dvh * vhat, axis=-1, keepdims=True))
        dp_ref[:, d:2 * d] = (dgv * _gelu_grad(bv)).astype(BF16)

        @pl.when(i == nsteps - 1)
        def _():
            ones = jnp.ones((8, gd), F32)
            for g in range(SGU_GROUPS):
                gs = slice(g * gd, (g + 1) * gd)
                tot = lax.dot_general(ones, dz_acc[:, gs], (_NT, ((), ())), preferred_element_type=F32,
                                      precision=lax.Precision.HIGHEST)
                dbs_ref[g:g + 1, :] = tot[0:1, :]

    rs = _row_spec(tt, d)
    c1 = _const_spec((1, d))
    return pl.pallas_call(
        body, name=name, grid=(nsteps,),
        in_specs=[pl.BlockSpec(memory_space=pl.ANY), rs, _const_spec(w_out.shape), _row_spec(tt, d, 2), _row_spec(tt, d, 3),
                  rs, _const_spec(wm.shape), _const_spec(wmt.shape), _const_spec(bias.shape), c1],
        out_specs=[pl.BlockSpec((tt, 2 * d), lambda i: (i, 1)), _const_spec(wm.shape), _const_spec((SGU_GROUPS, ck)), c1, c1],
        out_shape=[jax.ShapeDtypeStruct(dp.shape, dp.dtype), jax.ShapeDtypeStruct(wm.shape, F32),
                   jax.ShapeDtypeStruct((SGU_GROUPS, ck), F32), jax.ShapeDtypeStruct((1, d), F32),
                   jax.ShapeDtypeStruct((1, d), F32)],
        scratch_shapes=[pltpu.VMEM((ck, d), F32)],
        input_output_aliases={0: 0},
        compiler_params=_cparams(("arbitrary",)))(dp, dy_b, w_out, p, p, vn, wm, wmt, bias, ln_g)


def _softmax_rows(s):
    e = jnp.exp(s - jnp.max(s, axis=-1, keepdims=True))
    return e / jnp.sum(e, axis=-1, keepdims=True)


def _attn_fwd(q, kv, x1, w_xo, gain, *, bl, s, name):
    t, d = q.shape
    mlen = kv.shape[0] // bl
    hd = d // HEADS
    tq = min(ATTN_TILE, s)
    nq = s // tq
    scale = hd ** -0.5

    def body(q_ref, kv_ref, x1_ref, w_ref, g_ref, o_ref, x2_ref, h_ref):
        for h in range(HEADS):
            hs = slice(h * hd, (h + 1) * hd)
            vs = slice(d + h * hd, d + (h + 1) * hd)
            pr = _softmax_rows(_dot(q_ref[:, hs], kv_ref[:, hs], _NT) * scale)
            o_ref[:, hs] = _dot(pr.astype(BF16), kv_ref[:, vs], _NN).astype(BF16)
        x2 = x1_ref[...] + _dot(o_ref[...], w_ref[...], _NN)
        x2_ref[...] = x2
        h_ref[...] = _rms_apply(x2, g_ref[...]).astype(BF16)

    qs = pl.BlockSpec((tq, d), lambda b, j: (b * nq + j, 0))
    return pl.pallas_call(
        body, name=name, grid=(bl, nq),
        in_specs=[qs, pl.BlockSpec((mlen, 2 * d), lambda b, j: (b, 0)), qs, _const_spec(w_xo.shape), _const_spec((1, d))],
        out_specs=[qs, qs, qs],
        out_shape=[jax.ShapeDtypeStruct((t, d), BF16), jax.ShapeDtypeStruct((t, d), F32), jax.ShapeDtypeStruct((t, d), BF16)],
        compiler_params=_cparams(("parallel", "parallel")))(q, kv, x1, w_xo, gain)


def _attn_bwd(q, kv, do, *, bl, s, name):
    t, d = q.shape
    mlen = kv.shape[0] // bl
    hd = d // HEADS
    tq = min(ATTN_TILE, s)
    nq = s // tq
    scale = hd ** -0.5

    def body(q_ref, kv_ref, do_ref, dq_ref, dkv_ref):
        @pl.when(pl.program_id(1) == 0)
        def _():
            dkv_ref[...] = jnp.zeros_like(dkv_ref)

        for h in range(HEADS):
            hs = slice(h * hd, (h + 1) * hd)
            vs = slice(d + h * hd, d + (h + 1) * hd)
            qh, kh, vh, doh = q_ref[:, hs], kv_ref[:, hs], kv_ref[:, vs], do_ref[:, hs]
            pr = _softmax_rows(_dot(qh, kh, _NT) * scale)
            dpr = _dot(doh, vh, _NT)
            dkv_ref[:, vs] += _dot(pr.astype(BF16), doh, _TN)
            ds = (pr * (dpr - jnp.sum(dpr * pr, axis=-1, keepdims=True)) * scale).astype(BF16)
            dq_ref[:, hs] = _dot(ds, kh, _NN).astype(BF16)
            dkv_ref[:, hs] += _dot(ds, qh, _TN)

    qs = pl.BlockSpec((tq, d), lambda b, j: (b * nq + j, 0))
    ks = pl.BlockSpec((mlen, 2 * d), lambda b, j: (b, 0))
    return pl.pallas_call(
        body, name=name, grid=(bl, nq), in_specs=[qs, ks, qs], out_specs=[qs, ks],
        out_shape=[jax.ShapeDtypeStruct((t, d), BF16), jax.ShapeDtypeStruct(kv.shape, F32)],
        compiler_params=_cparams(("parallel", "arbitrary")))(q, kv, do)


def _mesh_pos():
    return lax.axis_index("x"), lax.axis_index("y"), lax.axis_index("c")


def _all_gather(arrs, *, name):
    n = len(arrs)
    hbm = pl.BlockSpec(memory_space=pl.ANY)

    def body(*refs):
        ins, outs = refs[:n], refs[n:2 * n]
        send_sems, recv_sems, loc_sems = refs[2 * n:]
        x, y, c = _mesh_pos()
        me, sib = (x, y, c), (x, y, 1 - c)
        chips = [(1 - x, y), (x, 1 - y), (1 - x, 1 - y)]

        def idx(dev):
            return 4 * dev[0] + 2 * dev[1] + dev[2]

        def copy(w, k, block, to, from_input=False):
            return pltpu.make_async_remote_copy(
                src_ref=ins[w] if from_input else outs[w].at[idx(block)], dst_ref=outs[w].at[idx(block)],
                send_sem=send_sems.at[w, k], recv_sem=recv_sems.at[w, k], device_id=to, device_id_type=MESH_ID)

        own = [pltpu.make_async_copy(ins[w], outs[w].at[idx(me)], loc_sems.at[w]) for w in range(n)]
        for cp in own:
            cp.start()
        first = []
        for w in range(n):
            first.append(copy(w, 0, me, sib, True))
            first += [copy(w, 1 + j, me, (*chip, c), True) for j, chip in enumerate(chips)]
        for cp in first:
            cp.start()
        passed = []
        for j, chip in enumerate(chips):
            for w in range(n):
                copy(w, 1 + j, (*chip, c), me).wait_recv()
                fwd = copy(w, 4 + j, (*chip, c), sib)
                fwd.start()
                passed.append(fwd)
        for w in range(n):
            copy(w, 0, sib, me).wait_recv()
            for j, chip in enumerate(chips):
                copy(w, 4 + j, (*chip, 1 - c), me).wait_recv()
        for cp in first + passed:
            cp.wait_send()
        for cp in own:
            cp.wait()

    return pl.pallas_call(
        body, name=name, in_specs=[hbm] * n, out_specs=[hbm] * n,
        out_shape=[jax.ShapeDtypeStruct((N_DEV, *a.shape), a.dtype) for a in arrs],
        scratch_shapes=[pltpu.SemaphoreType.DMA((n, 7)), pltpu.SemaphoreType.DMA((n, 7)), pltpu.SemaphoreType.DMA((n,))],
    )(*arrs)


_HBM = pl.BlockSpec(memory_space=pltpu.HBM)
_SEM = pl.BlockSpec(memory_space=pltpu.SEMAPHORE)
_ANY = pl.BlockSpec(memory_space=pl.ANY)
_EFFECT = pltpu.SideEffectType.DATAFLOW_SIDE_EFFECTING
N_PEERS = N_DEV - 1


def _related(pos, r):
    x, y, c = pos
    return (1 - x if r & 4 else x, 1 - y if r & 2 else y, 1 - c if r & 1 else c)


def _dev_index(dev):
    return 4 * dev[0] + 2 * dev[1] + dev[2]


def _in_hbm(a):
    return pltpu.with_memory_space_constraint(a, pltpu.HBM)


def _split_copies(kind, srcs, lands, send_sems, recv_sems):
    pos = _mesh_pos()
    me = _dev_index(pos)
    out = []
    for w in range(len(srcs)):
        for r in range(1, N_DEV):
            peer = _related(pos, r)
            if kind == "gather":
                src, dst_here, dst_there = srcs[w], lands[w].at[_dev_index(peer)], lands[w].at[me]
            elif srcs[w].ndim == 2:
                cb = lands[w].shape[2]
                src = srcs[w].at[:, pl.ds(pl.multiple_of(_dev_index(peer) * cb, LANES), cb)]
                dst_here = dst_there = lands[w].at[r - 1]
            else:
                src, dst_here, dst_there = srcs[w].at[_dev_index(peer)], lands[w].at[r - 1], lands[w].at[r - 1]
            out.append((src, dst_here, dst_there, send_sems.at[w * N_PEERS + r - 1], recv_sems.at[w * N_PEERS + r - 1], peer))
    return out


def _copy_start(kind, srcs, land_shapes, *, name, after=None):
    n = len(srcs)
    n_after = 0 if after is None else 1

    def body(*refs):
        src_refs, land_refs = refs[:n], refs[n:2 * n]
        send_sems, recv_sems = refs[2 * n + n_after], refs[2 * n + n_after + 1]
        token = refs[-1]
        for src, _, dst, ssem, rsem, peer in _split_copies(kind, src_refs, land_refs, send_sems, recv_sems):
            pltpu.make_async_remote_copy(src_ref=src, dst_ref=dst, send_sem=ssem, recv_sem=rsem, device_id=peer,
                                         device_id_type=MESH_ID).start()
        token[...] = jnp.zeros_like(token)

    lands = [_in_hbm(lax.empty(shape, s.dtype)) for s, shape in zip(srcs, land_shapes)]
    res = pl.pallas_call(
        body, name=name,
        out_shape=(pltpu.SemaphoreType.DMA((n * N_PEERS,)), pltpu.SemaphoreType.DMA((n * N_PEERS,)),
                   *[pltpu.HBM(s.shape, s.dtype) for s in srcs], *[pltpu.HBM(l.shape, l.dtype) for l in lands],
                   jax.ShapeDtypeStruct((8, 128), F32)),
        in_specs=[_HBM] * (2 * n) + [_ANY] * n_after,
        out_specs=(_SEM, _SEM, *[_HBM] * (2 * n), pl.BlockSpec(memory_space=pltpu.VMEM)),
        input_output_aliases={i: 2 + i for i in range(2 * n)},
        compiler_params=pltpu.CompilerParams(has_side_effects=_EFFECT),
    )(*[_in_hbm(s) for s in srcs], *lands, *([] if after is None else [after]))
    return res[0], res[1], list(res[2:2 + n]), list(res[2 + n:2 + 2 * n]), res[-1]


def _copy_wait(kind, send_sems, recv_sems, srcs, lands, after, *, name):
    n = len(srcs)

    def body(*refs):
        src_refs, land_refs = refs[:n], refs[n:2 * n]
        ssems, rsems = refs[2 * n], refs[2 * n + 1]
        for src, dst, _, ssem, rsem, peer in _split_copies(kind, src_refs, land_refs, ssems, rsems):
            cp = pltpu.make_async_remote_copy(src_ref=src, dst_ref=dst, send_sem=ssem, recv_sem=rsem, device_id=peer,
                                              device_id_type=MESH_ID)
            cp.wait_send()
            cp.wait_recv()

    res = pl.pallas_call(
        body, name=name,
        out_shape=(*[pltpu.HBM(s.shape, s.dtype) for s in srcs], *[pltpu.HBM(l.shape, l.dtype) for l in lands]),
        in_specs=[_HBM] * (2 * n) + [_SEM, _SEM, _ANY], out_specs=tuple([_HBM] * (2 * n)),
        input_output_aliases={i: i for i in range(2 * n)},
        compiler_params=pltpu.CompilerParams(has_side_effects=_EFFECT),
    )(*srcs, *lands, send_sems, recv_sems, after)
    return list(res[:n]), list(res[n:])


def _row_tile(rows):
    return max(tr for tr in range(16, min(rows, 512) + 1, 16) if rows % tr == 0)


def _adamw_math(w, g, m, v):
    m2 = ADAM_B1 * m + (1.0 - ADAM_B1) * g
    v2 = ADAM_B2 * v + (1.0 - ADAM_B2) * (g * g)
    m_hat = m2 / (1.0 - ADAM_B1 ** ADAM_STEP)
    v_hat = v2 / (1.0 - ADAM_B2 ** ADAM_STEP)
    delta = -ADAM_LR * (m_hat / (jnp.sqrt(v_hat) + ADAM_EPS) + ADAM_WD * w)
    return delta, m2, v2


def _adamw_shard(partials, landed, dev, w, m, v, *, name):
    r, c = w.shape
    tr = _row_tile(r)

    def body(dev_ref, p_ref, l_ref, w_ref, m_ref, v_ref, g_out, d_out, m_out, v_out):
        del dev_ref
        g = p_ref[...].astype(F32)
        for k in range(N_PEERS):
            g = g + l_ref[k].astype(F32)
        delta, m2, v2 = _adamw_math(w_ref[...], g, m_ref[...], v_ref[...])
        g_out[...] = g
        d_out[...] = delta
        m_out[...] = m2
        v_out[...] = v2

    blk = pl.BlockSpec((tr, c), lambda i, dev_ref: (i, 0))
    if partials.ndim == 2:
        own = pl.BlockSpec((tr, c), lambda i, dev_ref: (i, dev_ref[0]))
    else:
        own = pl.BlockSpec((None, tr, c), lambda i, dev_ref: (dev_ref[0], i, 0))
    gs = pltpu.PrefetchScalarGridSpec(
        num_scalar_prefetch=1, grid=(r // tr,),
        in_specs=[own, pl.BlockSpec((N_PEERS, tr, c), lambda i, dev_ref: (0, i, 0)), blk, blk, blk],
        out_specs=[blk] * 4)
    return pl.pallas_call(
        body, name=name, grid_spec=gs, out_shape=[jax.ShapeDtypeStruct((r, c), F32)] * 4,
        compiler_params=_cparams(("parallel",)))(dev, partials, landed, w, m, v)


def _sum_devices(p_ref, *idx):
    g = p_ref[(0, *idx)]
    for k in range(1, N_DEV):
        g = g + p_ref[(k, *idx)]
    return g


def _adamw_replicated(parts, states, loss_row, *, name):
    n_parts, n_par = len(parts), len(states)
    n_vec = n_par - (n_parts - 1)

    def body(*refs):
        part_refs, st = refs[:n_parts], refs[n_parts:n_parts + 3 * n_par]
        outs = refs[n_parts + 3 * n_par:]
        outs[0][...] = _sum_devices(part_refs[0], slice(loss_row, loss_row + 1), slice(0, 1))
        for i in range(n_par):
            g = _sum_devices(part_refs[0], slice(i, i + 1)) if i < n_vec else _sum_devices(part_refs[1 + i - n_vec])
            delta, m2, v2 = _adamw_math(st[3 * i][...], g, st[3 * i + 1][...], st[3 * i + 2][...])
            for o, val in zip(outs[1 + 4 * i:5 + 4 * i], (g, delta, m2, v2)):
                o[...] = val

    flat = [a for wmv in states for a in wmv]
    return pl.pallas_call(
        body, name=name,
        out_shape=[jax.ShapeDtypeStruct((1, 1), F32)] + [jax.ShapeDtypeStruct(w.shape, F32) for w, _, _ in states for _ in range(4)],
        compiler_params=pltpu.CompilerParams(vmem_limit_bytes=VMEM_LIMIT))(*parts, *flat)


def _adamw_column_shards(parts, dev, states, row0s, *, name):
    _, rows, _ = parts.shape
    c = states[0][0].shape[1]

    def body(dev_ref, p_ref, *refs):
        del dev_ref
        st, outs = refs[:3 * len(states)], refs[3 * len(states):]
        for j, r0 in enumerate(row0s):
            w_ref = st[3 * j]
            g = _sum_devices(p_ref, slice(r0, r0 + w_ref.shape[0]))
            delta, m2, v2 = _adamw_math(w_ref[...], g, st[3 * j + 1][...], st[3 * j + 2][...])
            for o, val in zip(outs[4 * j:4 * j + 4], (g, delta, m2, v2)):
                o[...] = val

    whole = lambda a: pl.BlockSpec(a.shape, lambda i, dev_ref: (0, 0))
    flat = [a for wmv in states for a in wmv]
    outs = [w for w, _, _ in states for _ in range(4)]
    gs = pltpu.PrefetchScalarGridSpec(
        num_scalar_prefetch=1, grid=(1,),
        in_specs=[pl.BlockSpec((N_DEV, rows, c), lambda i, dev_ref: (0, 0, dev_ref[0]))] + [whole(a) for a in flat],
        out_specs=[whole(a) for a in outs])
    return pl.pallas_call(
        body, name=name, grid_spec=gs, out_shape=[jax.ShapeDtypeStruct(a.shape, F32) for a in outs],
        compiler_params=_cparams(("arbitrary",)))(dev, parts, *flat)


def _pad_rows(a, rows):
    return jnp.pad(a, ((0, rows - a.shape[0]), (0, 0)))


def _unblock_cols(g):
    return jnp.transpose(g, (1, 0, 2)).reshape(g.shape[1], N_DEV * g.shape[2])


def kernel(x, mem, norm_mix, w_in, b_gate, conv_w, conv_b, conv_ln_g, conv_ln_b, w_conv_out, sgu_ln_g, sgu_ln_b, sgu_w, sgu_b, w_sgu_out, w_mix_out, norm_xattn, norm_mem, w_q, w_kv, w_xo, norm_ffn, w_gu, w_down, norm_final, loss_target, m_norm_mix, m_w_in, m_b_gate, m_conv_w, m_conv_b, m_conv_ln_g, m_conv_ln_b, m_w_conv_out, m_sgu_ln_g, m_sgu_ln_b, m_sgu_w, m_sgu_b, m_w_sgu_out, m_w_mix_out, m_norm_xattn, m_norm_mem, m_w_q, m_w_kv, m_w_xo, m_norm_ffn, m_w_gu, m_w_down, m_norm_final, v_norm_mix, v_w_in, v_b_gate, v_conv_w, v_conv_b, v_conv_ln_g, v_conv_ln_b, v_w_conv_out, v_sgu_ln_g, v_sgu_ln_b, v_sgu_w, v_sgu_b, v_w_sgu_out, v_w_mix_out, v_norm_xattn, v_norm_mem, v_w_q, v_w_kv, v_w_xo, v_norm_ffn, v_w_gu, v_w_down, v_norm_final):
    given = dict(locals())
    bl, s, d = x.shape
    t = bl * s
    xf = x.reshape(t, d)
    tgt = loss_target.reshape(t, d)
    memf = mem.reshape(bl * mem.shape[1], d)
    cx, cy, cc = lax.axis_index("x"), lax.axis_index("y"), lax.axis_index("c")
    dev = 4 * cx + 2 * cy + cc
    dev_id = dev.astype(jnp.int32).reshape(1)
    col_sharded = ["w_in", "w_kv"]
    transposed = ["w_gu"]

    def shard_of(name, prefix=""):
        a = given[prefix + name][0]
        return jnp.transpose(a) if name in transposed else a

    def full_weight(name, blocks):
        return _unblock_cols(blocks) if name in col_sharded else blocks.reshape(N_DEV * blocks.shape[1], blocks.shape[2])

    g_bg, g_cw = _all_gather([_pad_rows(b_gate[0], 8), _pad_rows(conv_w[0], CONV_HALO)], name="gather_small_params")
    h1, p, w_in_blocks = _in_proj_gather(xf, norm_mix + g_bg[0, 7:8, 0:1], w_in[0].astype(BF16), name="in_proj")
    early = ["w_conv_out", "w_sgu_out", "w_mix_out", "w_q", "w_kv", "w_xo"]
    late = ["w_gu", "w_down"]
    shards = {n: shard_of(n).astype(BF16) for n in early + late}
    started = {}
    for grp, names in (("early", early), ("late", late)):
        srcs = [shards[n] for n in names]
        started[grp] = _copy_start("gather", srcs, [(N_DEV, *a.shape) for a in srcs], name=f"gather_{grp}_start", after=p)
    token = started["early"][4][0:1, 0:1] + started["late"][4][0:1, 0:1]
    wfull = {}
    bg_full = _unblock_cols(g_bg)
    cw_full = _unblock_cols(g_cw)

    def finish_gather(grp, names, after):
        ssem, rsem, srcs, lands, _ = started[grp]
        _, lands = _copy_wait("gather", ssem, rsem, srcs, lands, after, name=f"gather_{grp}_wait")
        for n, land in zip(names, lands):
            wfull[n] = full_weight(n, lax.dynamic_update_index_in_dim(land, shards[n], dev, 0))

    tri = jnp.tril(jnp.ones((SGU_CHUNK, SGU_CHUNK), bool))
    wm32 = jnp.where(tri[None], sgu_w[0], 0.0)
    wm = wm32.astype(BF16)
    wmt = jnp.transpose(wm32, (0, 2, 1)).astype(BF16)
    sgu_bias = jnp.broadcast_to(sgu_b[0][:, :, None], (SGU_GROUPS, SGU_CHUNK, d // SGU_GROUPS))

    c_conv, a_act = _conv_fwd(p, cw_full, conv_b + token, conv_ln_g, conv_ln_b, bl=bl, s=s, name="conv_fwd")
    sg, vn = _sgu_fwd(p, wm, sgu_bias, sgu_ln_g, sgu_ln_b + token, name="sgu_fwd")
    finish_gather("early", early, a_act[0:16, 0:128] + sg[0:16, 0:128])
    y_a = _matmul(a_act, wfull["w_conv_out"], mode="nn", out_dtype=BF16, name="mm_conv_out", tm=1024, tn=1024, tk=1024)
    y_b = _matmul(sg, wfull["w_sgu_out"], mode="nn", out_dtype=BF16, name="mm_sgu_out", tm=1024, tn=1024, tk=1024)
    merged, x1, h2, q = _mix_out(p, y_a, y_b, bg_full, xf, wfull["w_mix_out"], norm_xattn, wfull["w_q"], name="mix_out")
    mem_n = _rms_fwd(memf, norm_mem, name="rms_mem")
    kv = _matmul(mem_n, wfull["w_kv"], mode="nn", out_dtype=BF16, name="mm_kv", tm=1024, tn=1024, tk=1024)
    o, x2, h3 = _attn_fwd(q, kv, x1, wfull["w_xo"], norm_ffn, bl=bl, s=s, name="attn_fwd")
    finish_gather("late", late, h3)
    gu, act, dx3, loss_part, d_norm_final = _ffn_fwd(h3, x2, tgt, wfull["w_gu"], wfull["w_down"],
                                                     norm_final.reshape(1, d), name="ffn_fwd")

    grads = {}
    sent = []

    def send_grads(names, tag, after=None):
        blocks, land_shapes = [], []
        for n in names:
            g = grads[n]
            if g.ndim == 2 and n in col_sharded:
                land_shapes.append((N_PEERS, g.shape[0], g.shape[1] // N_DEV))
            else:
                if g.ndim == 2:
                    g = g.reshape(N_DEV, -1, g.shape[1])
                land_shapes.append((N_PEERS, *g.shape[1:]))
            blocks.append(g)
        ssem, rsem, srcs, lands, tok = _copy_start("scatter", blocks, land_shapes, name=f"grads_{tag}_start", after=after)
        sent.append((names, ssem, rsem, srcs, lands))
        return tok[0:1, 0:1]

    dgu, dx2, do, d_norm_ffn = _ffn_bwd(dx3, gu, x2, wfull["w_down"], wfull["w_gu"], norm_ffn, wfull["w_xo"], name="ffn_bwd")
    grads["w_down"] = _matmul(act, dx3, mode="tn", out_dtype=BF16, name="mm_dw_down", tm=1408, tn=1024, tk=2048)
    grads["w_gu"] = _matmul(dgu, h3, mode="tn", out_dtype=BF16, name="mm_dw_gu", tm=1408, tn=1024, tk=2048)
    tok = send_grads(["w_down", "w_gu"], "ffn")
    grads["w_xo"] = _matmul(o, dx2, mode="tn", out_dtype=BF16, name="mm_dw_xo", tm=1024, tn=1024, tk=2048)
    dq, dkv = _attn_bwd(q, kv, do, bl=bl, s=s, name="attn_bwd")
    grads["w_kv"] = _matmul(mem_n, dkv, mode="tn", out_dtype=BF16, name="mm_dw_kv", tm=1024, tn=256, tk=1024,
                            col_blocks=N_DEV)
    tok2 = send_grads(["w_xo", "w_kv"], "attn")
    dmem_n = _matmul(dkv, wfull["w_kv"], mode="nt", out_dtype=F32, name="mm_d_mem", tm=512, tn=1024, tk=2048)
    d_norm_mem = _rms_gain_grad(dmem_n, memf, name="rms_mem_bwd")
    dx1, d_norm_xattn, dw_q = _proj_rms_bwd(dq, dx2, x1, wfull["w_q"], norm_xattn + (tok + tok2), name="q_rms_bwd", h=h2)
    dp, dy_a, dy_b, d_b_gate, dw_mix, dw_in_gates = _gates_bwd_fused(dx1, p, y_a, y_b, bg_full, wfull["w_mix_out"],
                                                                    merged, h1, name="gates_bwd")
    grads["w_q"] = dw_q.astype(BF16)
    grads["w_mix_out"] = dw_mix.astype(BF16)
    grads["w_sgu_out"] = _matmul(sg, dy_b, mode="tn", out_dtype=BF16, name="mm_dw_sgu", tm=1024, tn=1024, tk=2048)
    dc, d_conv_ln_g, d_conv_ln_b, dw_conv = _conv_ln_bwd_fused(dy_a, c_conv, a_act, wfull["w_conv_out"], conv_ln_g,
                                                               conv_ln_b, name="conv_ln_bwd")
    grads["w_conv_out"] = dw_conv.astype(BF16)
    tok = send_grads(["w_q", "w_mix_out", "w_sgu_out", "w_conv_out"], "mixer")
    dp, d_sgu_w, d_sgu_b, d_sgu_ln_g, d_sgu_ln_b = _sgu_bwd(dp, dy_b, wfull["w_sgu_out"], p, vn, wm, wmt, sgu_bias,
                                                             sgu_ln_g + tok, name="sgu_bwd")
    sgw_ssem, sgw_rsem, sgw_src, sgw_land, tok = _copy_start("gather", [d_sgu_w], [(N_DEV, *d_sgu_w.shape)],
                                                             name="gather_sgu_w_start")
    cw_full = cw_full + tok[0:1, 0:1]
    dw_in = _matmul(h1, dp, mode="tn", out_dtype=BF16, name="mm_dw_in_sgu", tm=1024, tn=1024, tk=2048,
                    b_cols=(2 * d, 2 * d), out_into=(dw_in_gates, 2 * d))
    dp, d_conv_w, d_conv_b, dw_in = _conv_bwd(dp, dc, p, cw_full, h1, dw_in, bl=bl, s=s, name="conv_bwd")
    grads["w_in"] = dw_in
    tok = send_grads(["w_in"], "in")
    grad_x, d_norm_mix = _proj_rms_bwd(dp, dx1, xf, w_in_blocks, norm_mix + tok, name="in_proj_bwd")
    out = {}

    vec_names = ["norm_mix", "conv_b", "conv_ln_g", "conv_ln_b", "sgu_ln_g", "sgu_ln_b", "norm_xattn", "norm_mem",
                 "norm_ffn", "norm_final"]
    vec_grads = [d_norm_mix, d_conv_b, d_conv_ln_g, d_conv_ln_b, d_sgu_ln_g, d_sgu_ln_b, d_norm_xattn, d_norm_mem,
                 d_norm_ffn, d_norm_final]
    n_vec = len(vec_names)
    small_vec = jnp.concatenate([g.reshape(1, d) for g in vec_grads]
                                + [jnp.broadcast_to(loss_part, (1, d)), jnp.zeros((16 - n_vec - 1, d), F32)], axis=0)
    small_cols = jnp.concatenate([d_b_gate, d_conv_w], axis=0)
    parts_vec, parts_sb, parts_cols = _all_gather([small_vec, d_sgu_b, small_cols], name="gather_small_grads")
    _, sgw_land = _copy_wait("gather", sgw_ssem, sgw_rsem, sgw_src, sgw_land, parts_vec, name="gather_sgu_w_wait")
    parts_sw = lax.dynamic_update_index_in_dim(sgw_land[0], d_sgu_w, dev, 0)
    rep_names = vec_names + ["sgu_b", "sgu_w"]
    rep_shapes = [(1, d)] * n_vec + [d_sgu_b.shape, d_sgu_w.shape]
    states = [tuple(given[pre + n].reshape(shape) for pre in ("", "m_", "v_")) for n, shape in zip(rep_names, rep_shapes)]
    res_rep = _adamw_replicated([parts_vec, parts_sb, parts_sw], states, n_vec, name="adamw_small")
    for i, n in enumerate(rep_names):
        out[n] = [r.reshape(given[n].shape) for r in res_rep[1 + 4 * i:5 + 4 * i]]
    res_cols = _adamw_column_shards(parts_cols, dev_id, [(b_gate[0], m_b_gate[0], v_b_gate[0]),
                                                        (conv_w[0], m_conv_w[0], v_conv_w[0])], (0, 8),
                                    name="adamw_small_cols")
    out["b_gate"] = [r[None] for r in res_cols[0:4]]
    out["conv_w"] = [r[None] for r in res_cols[4:8]]

    done = res_rep[1]
    for names, ssem, rsem, srcs, lands in sent:
        srcs, lands = _copy_wait("scatter", ssem, rsem, srcs, lands, done, name=f"grads_{names[0]}_wait")
        for n, partials, landed in zip(names, srcs, lands):
            res = _adamw_shard(partials, landed, dev_id, shard_of(n), shard_of(n, "m_"), shard_of(n, "v_"),
                               name=f"adamw_{n}")
            done = res[0]
            out[n] = [(jnp.transpose(r) if n in transposed else r)[None] for r in res]

    order = ["norm_mix", "w_in", "b_gate", "conv_w", "conv_b", "conv_ln_g", "conv_ln_b", "w_conv_out", "sgu_ln_g",
             "sgu_ln_b", "sgu_w", "sgu_b", "w_sgu_out", "w_mix_out", "norm_xattn", "norm_mem", "w_q", "w_kv", "w_xo",
             "norm_ffn", "w_gu", "w_down", "norm_final"]
    loss = res_rep[0][0, 0]
    return (loss, grad_x.reshape(x.shape), *[out[n][0] for n in order], *[out[n][1] for n in order],
            *[out[n][2] for n in order], *[out[n][3] for n in order])
```

```python
import jax
import jax.numpy as jnp
from jax import lax
from jax.experimental import pallas as pl
from jax.experimental.pallas import tpu as pltpu

F32 = jnp.float32
BF16 = jnp.bfloat16
RMS_EPS = 1e-6
LN_EPS = 1e-5
CONV_WIDTH = 31
CONV_HALO = 32
CONV_ROWS = 128
CONV_COLS = 256
LANES = 128
SGU_CHUNK = 128
SGU_GROUPS = 8
SGU_TILE = 1024
HEADS = 4
N_DEV = 8
ADAM_LR, ADAM_B1, ADAM_B2, ADAM_EPS, ADAM_WD, ADAM_STEP = 0.001, 0.9, 0.999, 1e-08, 0.01, 10
VMEM_LIMIT = 56 * 1024 * 1024
TOKEN_TILE = 256
ATTN_TILE = 1024
MESH_ID = pl.DeviceIdType.MESH

_GELU_K = 0.7978845608028654
_GELU_C = 0.044715


def _cparams(sem=None):
    return pltpu.CompilerParams(dimension_semantics=sem, vmem_limit_bytes=VMEM_LIMIT)


def _sigmoid(v):
    return 0.5 * jnp.tanh(0.5 * v) + 0.5


def _gelu(v):
    return 0.5 * v * (1.0 + jnp.tanh(_GELU_K * (v + _GELU_C * v * v * v)))


def _gelu_grad(v):
    th = jnp.tanh(_GELU_K * (v + _GELU_C * v * v * v))
    return 0.5 * (1.0 + th) + 0.5 * v * (1.0 - th * th) * _GELU_K * (1.0 + 3.0 * _GELU_C * v * v)


def _dot(a, b, dims):
    return lax.dot_general(a, b, (dims, ((), ())), preferred_element_type=F32)


_NN = ((1,), (0,))
_NT = ((1,), (1,))
_TN = ((0,), (0,))


def _matmul(a, b, *, mode, out_dtype, name, tm=512, tn=512, tk=512, col_blocks=None, b_cols=None, out_into=None):
    if mode == "nn":
        (m, k), (_, n) = a.shape, b.shape
    elif mode == "nt":
        (m, k), (n, _) = a.shape, b.shape
    else:
        (k, m), (_, n) = a.shape, b.shape
    b_first = 0
    if b_cols is not None:
        assert mode == "tn"
        b_first, n = b_cols
    tm, tn, tk = min(tm, m), min(tn, n), min(tk, k)
    assert b_first % tn == 0
    b_first //= tn
    assert m % tm == 0 and n % tn == 0 and k % tk == 0, (name, a.shape, b.shape, tm, tn, tk)
    nk = k // tk
    dims = {"nn": _NN, "nt": _NT, "tn": _TN}[mode]

    def body(*refs):
        a_ref, b_ref = refs[:2]
        o_ref = refs[3] if out_into is not None else refs[2]
        part = _dot(a_ref[...].astype(BF16), b_ref[...].astype(BF16), dims)
        if nk == 1:
            o_ref[...] = part.astype(out_dtype)
        else:
            acc_ref = refs[-1]
            kk = pl.program_id(2)

            @pl.when(kk == 0)
            def _():
                acc_ref[...] = part

            @pl.when(kk > 0)
            def _():
                acc_ref[...] += part

            @pl.when(kk == nk - 1)
            def _():
                o_ref[...] = acc_ref[...].astype(out_dtype)

    resident = dict(pipeline_mode=pl.Buffered(1)) if (n == tn and nk == 1 and mode != "tn" and m > tm) else {}
    if mode == "nn":
        a_spec = pl.BlockSpec((tm, tk), lambda i, j, kk: (i, kk))
        b_spec = pl.BlockSpec((tk, tn), lambda i, j, kk: (kk, j), **resident)
    elif mode == "nt":
        a_spec = pl.BlockSpec((tm, tk), lambda i, j, kk: (i, kk))
        b_spec = pl.BlockSpec((tn, tk), lambda i, j, kk: (j, kk), **resident)
    else:
        a_spec = pl.BlockSpec((tk, tm), lambda i, j, kk: (kk, i))
        b_spec = pl.BlockSpec((tk, tn), lambda i, j, kk: (kk, j + b_first))
    in_specs, args = [a_spec, b_spec], [a, b]
    out_shape = [jax.ShapeDtypeStruct((m, n), out_dtype)]
    out_specs = [pl.BlockSpec((tm, tn), lambda i, j, kk: (i, j))]
    if col_blocks is not None:
        assert (n // col_blocks) % tn == 0
        per = n // col_blocks // tn
        out_shape = [jax.ShapeDtypeStruct((col_blocks, m, n // col_blocks), out_dtype)]
        out_specs = [pl.BlockSpec((None, tm, tn), lambda i, j, kk: (j // per, i, j % per))]
    aliases = {}
    if out_into is not None:
        target, first = out_into
        assert col_blocks is None and first % tn == 0 and target.dtype == out_dtype
        in_specs.append(pl.BlockSpec(memory_space=pl.ANY))
        args.append(target)
        aliases = {len(args) - 1: 0}
        out_shape = [jax.ShapeDtypeStruct(target.shape, target.dtype)]
        out_specs = [pl.BlockSpec((tm, tn), lambda i, j, kk: (i, j + first // tn))]
    res = pl.pallas_call(
        body, name=name, grid=(m // tm, n // tn, nk), in_specs=in_specs, out_specs=out_specs, out_shape=out_shape,
        scratch_shapes=[pltpu.VMEM((tm, tn), F32)] if nk > 1 else [], input_output_aliases=aliases,
        compiler_params=_cparams(("parallel", "parallel", "arbitrary")),
    )(*args)
    return res[0]


def _row_call(name, t, tm, rows_in, residents, rows_out, accs, body):
    n_in, n_res, n_out, n_acc = len(rows_in), len(residents), len(rows_out), len(accs)
    steps = t // tm
    assert t % tm == 0
    narrow = [i for i, a in enumerate(accs) if a[1] != F32]

    def kernel_body(*refs):
        in_refs, res_refs = refs[:n_in], refs[n_in:n_in + n_res]
        out_refs = refs[n_in + n_res:n_in + n_res + n_out]
        acc_out = list(refs[n_in + n_res + n_out:n_in + n_res + n_out + n_acc])
        scratch = refs[n_in + n_res + n_out + n_acc:]
        acc_refs = list(acc_out)
        for s_ref, i in zip(scratch, narrow):
            acc_refs[i] = s_ref
        if accs:
            @pl.when(pl.program_id(0) == 0)
            def _():
                for acc in acc_refs:
                    acc[...] = jnp.zeros_like(acc)
        body(in_refs, res_refs, out_refs, acc_refs)
        if narrow:
            @pl.when(pl.program_id(0) == steps - 1)
            def _():
                for i in narrow:
                    acc_out[i][...] = acc_refs[i][...].astype(acc_out[i].dtype)

    once = dict(pipeline_mode=pl.Buffered(1)) if steps > 1 else {}
    in_specs = [pl.BlockSpec((tm, cols), lambda i, cb=cb: (i, cb)) for _, cols, cb in rows_in]
    in_specs += [pl.BlockSpec(r.shape, lambda i, nd=r.ndim: (0,) * nd, **once) for r in residents]
    out_specs = [pl.BlockSpec((tm, cols), lambda i, cb=cb: (i, cb)) for _, cols, cb, _ in rows_out]
    out_specs += [pl.BlockSpec(a[0], lambda i, nd=len(a[0]), cb=(a[3] if len(a) == 4 else 0): (0,) * (nd - 1) + (cb,))
                  for a in accs]
    out_shape = [jax.ShapeDtypeStruct((t, total), dt) for total, _, _, dt in rows_out]
    out_shape += [jax.ShapeDtypeStruct((a[0][0], a[2]) if len(a) == 4 else a[0], a[1]) for a in accs]
    return pl.pallas_call(
        kernel_body, name=name, grid=(steps,), in_specs=in_specs, out_specs=out_specs, out_shape=out_shape,
        scratch_shapes=[pltpu.VMEM(accs[i][0], F32) for i in narrow],
        compiler_params=_cparams(("arbitrary",) if accs else ("parallel",)),
    )(*[a for a, _, _ in rows_in], *residents)


def _rms_apply(xv, gain):
    return xv * lax.rsqrt(jnp.mean(xv * xv, axis=-1, keepdims=True) + RMS_EPS) * gain


def _rms_grad(dres, dh, xv, gain):
    r = lax.rsqrt(jnp.mean(xv * xv, axis=-1, keepdims=True) + RMS_EPS)
    xhat = xv * r
    dxh = dh * gain
    dx = dres + r * (dxh - xhat * jnp.mean(dxh * xhat, axis=-1, keepdims=True))
    return dx, jnp.sum(dh * xhat, axis=0, keepdims=True)


def _in_proj_gather(xf, gain, w_shard, *, name):
    t, d = xf.shape
    cb = w_shard.shape[1]
    tm = min(1024, t)
    steps = t // tm
    mx, my, _ = _mesh_pos()
    order = jnp.stack([2 * mx + my, 2 * (1 - mx) + my, 2 * mx + (1 - my), 2 * (1 - mx) + (1 - my)]).astype(jnp.int32)

    def body(order_ref, x_ref, g_ref, ws_ref, h_ref, p_ref, wout_ref, w_ref, send_sems, recv_sems, own_sem):
        ps, i = pl.program_id(0), pl.program_id(1)
        x, y, c = _mesh_pos()
        me, sib = (x, y, c), (x, y, 1 - c)
        chips = [(1 - x, y), (x, 1 - y), (1 - x, 1 - y)]

        def copy(k, block, to, from_shard=False):
            return pltpu.make_async_remote_copy(
                src_ref=ws_ref if from_shard else w_ref.at[_dev_index(block)], dst_ref=w_ref.at[_dev_index(block)],
                send_sem=send_sems.at[k], recv_sem=recv_sems.at[k], device_id=to, device_id_type=MESH_ID)

        own = pltpu.make_async_copy(ws_ref, w_ref.at[_dev_index(me)], own_sem)
        first = [copy(0, me, sib, True)] + [copy(1 + j, me, (*chip, c), True) for j, chip in enumerate(chips)]
        passed = [copy(4 + j, (*chip, c), sib) for j, chip in enumerate(chips)]

        @pl.when(jnp.logical_and(ps == 0, i == 0))
        def _():
            own.start()
            for cp in first:
                cp.start()
            own.wait()
            copy(0, sib, me).wait_recv()

        for j, chip in enumerate(chips):
            @pl.when(jnp.logical_and(ps == j + 1, i == 0))
            def _(j=j, chip=chip):
                copy(1 + j, (*chip, c), me).wait_recv()
                passed[j].start()
                copy(4 + j, (*chip, 1 - c), me).wait_recv()

        h = _rms_apply(x_ref[...], g_ref[...]).astype(BF16)
        h_ref[...] = h
        chip_id = order_ref[ps]
        p_ref[:, 0:cb] = _dot(h, w_ref[2 * chip_id], _NN).astype(BF16)
        p_ref[:, cb:2 * cb] = _dot(h, w_ref[2 * chip_id + 1], _NN).astype(BF16)

        @pl.when(jnp.logical_and(ps == 3, i == steps - 1))
        def _():
            for cp in first + passed:
                cp.wait_send()
            keep = pltpu.make_async_copy(w_ref, wout_ref, own_sem)
            keep.start()
            keep.wait()

    gs = pltpu.PrefetchScalarGridSpec(
        num_scalar_prefetch=1, grid=(4, steps),
        in_specs=[pl.BlockSpec((tm, d), lambda ps, i, o: (i, 0)), pl.BlockSpec((1, d), lambda ps, i, o: (0, 0)),
                  pl.BlockSpec(memory_space=pl.ANY)],
        out_specs=[pl.BlockSpec((tm, d), lambda ps, i, o: (jnp.where(ps == 0, i, steps - 1), 0)),
                   pl.BlockSpec((tm, 2 * cb), lambda ps, i, o: (i, o[ps])), pl.BlockSpec(memory_space=pl.ANY)],
        scratch_shapes=[pltpu.VMEM((N_DEV, d, cb), BF16), pltpu.SemaphoreType.DMA((7,)), pltpu.SemaphoreType.DMA((7,)),
                        pltpu.SemaphoreType.DMA(())])
    return pl.pallas_call(
        body, name=name, grid_spec=gs,
        out_shape=[jax.ShapeDtypeStruct((t, d), BF16), jax.ShapeDtypeStruct((t, N_DEV * cb), BF16),
                   jax.ShapeDtypeStruct((N_DEV, d, cb), BF16)],
        compiler_params=_cparams(("arbitrary", "arbitrary")))(order, xf, gain, w_shard)


def _mix_out(p, y_a, y_b, b_gate, xf, w_mix, gain, w_q, *, name):
    t, d = xf.shape

    def body(ins, res, outs, accs):
        ga_ref, gb_ref, ya_ref, yb_ref, x_ref = ins
        bg_ref, wm_ref, g_ref, wq_ref = res
        m_ref, x1_ref, h_ref, q_ref = outs
        sa = _sigmoid(ga_ref[...].astype(F32) + bg_ref[0:1, :])
        sb = _sigmoid(gb_ref[...].astype(F32) + bg_ref[1:2, :])
        merged = (sa * ya_ref[...].astype(F32) + sb * yb_ref[...].astype(F32)).astype(BF16)
        m_ref[...] = merged
        x1 = x_ref[...] + _dot(merged, wm_ref[...], _NN)
        x1_ref[...] = x1
        h = _rms_apply(x1, g_ref[...]).astype(BF16)
        h_ref[...] = h
        q_ref[...] = _dot(h, wq_ref[...], _NN).astype(BF16)

    return _row_call(name, t, min(512, t), [(p, d, 4), (p, d, 5), (y_a, d, 0), (y_b, d, 0), (xf, d, 0)],
                     [b_gate, w_mix, gain, w_q], [(d, d, 0, BF16), (d, d, 0, F32), (d, d, 0, BF16), (d, d, 0, BF16)], [], body)


def _ffn_fwd(h3, x2, target, w_gu_t, w_down, gain, *, name):
    t, d = x2.shape
    f2 = w_gu_t.shape[0]
    f = f2 // 2
    half = f // 2

    def body(ins, res, outs, accs):
        h_ref, x2_ref, t_ref = ins
        wgu_ref, wd_ref, g_ref = res
        gu_ref, act_ref, dx_ref = outs
        loss_ref, dg_ref = accs
        h = h_ref[...]
        x3 = x2_ref[...]
        for c0 in (0, half):
            gt = _dot(h, wgu_ref[c0:c0 + half, :], _NT).astype(BF16)
            up = _dot(h, wgu_ref[f + c0:f + c0 + half, :], _NT).astype(BF16)
            gu_ref[:, c0:c0 + half] = gt
            gu_ref[:, f + c0:f + c0 + half] = up
            gtf = gt.astype(F32)
            act = (gtf * _sigmoid(gtf) * up.astype(F32)).astype(BF16)
            act_ref[:, c0:c0 + half] = act
            x3 = x3 + _dot(act, wd_ref[c0:c0 + half, :], _NN)
        g = g_ref[...]
        r = lax.rsqrt(jnp.mean(x3 * x3, axis=-1, keepdims=True) + RMS_EPS)
        xhat = x3 * r
        err = xhat * g - t_ref[...]
        loss_ref[...] += 0.5 * jnp.sum(jnp.mean(err * err, axis=-1, keepdims=True), axis=0, keepdims=True)
        dy = err * (1.0 / d)
        dg_ref[...] += jnp.sum(dy * xhat, axis=0, keepdims=True)
        dxh = dy * g
        dx_ref[...] = r * (dxh - xhat * jnp.mean(dxh * xhat, axis=-1, keepdims=True))

    return _row_call(name, t, min(256, t), [(h3, d, 0), (x2, d, 0), (target, d, 0)], [w_gu_t, w_down, gain],
                     [(f2, f2, 0, BF16), (f, f, 0, BF16), (d, d, 0, F32)], [((1, 1), F32), ((1, d), F32)], body)


def _ffn_bwd(dx3, gu, x2, w_down, w_gu_t, gain, w_xo, *, name):
    t, d = x2.shape
    f2 = w_gu_t.shape[0]
    f = f2 // 2
    half = f // 2

    def body(ins, res, outs, accs):
        dx3_ref, gu_ref, x2_ref = ins
        wd_ref, wgu_ref, g_ref, wxo_ref = res
        dgu_ref, dx2_ref, do_ref = outs
        (dg_ref,) = accs
        dx3v = dx3_ref[...]
        dxb = dx3v.astype(BF16)
        dh = jnp.zeros(dx3v.shape, F32)
        for c0 in (0, half):
            dact = _dot(dxb, wd_ref[c0:c0 + half, :], _NT)
            gt = gu_ref[:, c0:c0 + half].astype(F32)
            up = gu_ref[:, f + c0:f + c0 + half].astype(F32)
            sg = _sigmoid(gt)
            dgt = (dact * up * sg * (1.0 + gt * (1.0 - sg))).astype(BF16)
            dup = (dact * gt * sg).astype(BF16)
            dgu_ref[:, c0:c0 + half] = dgt
            dgu_ref[:, f + c0:f + c0 + half] = dup
            dh = dh + _dot(dgt, wgu_ref[c0:c0 + half, :], _NN) + _dot(dup, wgu_ref[f + c0:f + c0 + half, :], _NN)
        dx2, dg = _rms_grad(dx3v, dh, x2_ref[...], g_ref[...])
        dx2_ref[...] = dx2
        dg_ref[...] += dg
        do_ref[...] = _dot(dx2.astype(BF16), wxo_ref[...], _NT).astype(BF16)

    return _row_call(name, t, min(256, t), [(dx3, d, 0), (gu, f2, 0), (x2, d, 0)], [w_down, w_gu_t, gain, w_xo],
                     [(f2, f2, 0, BF16), (d, d, 0, F32), (d, d, 0, BF16)], [((1, d), F32)], body)


def _proj_rms_bwd(dy, dres, x, w, gain, *, name, h=None):
    t, d = x.shape
    k = dy.shape[1]

    def body(ins, res, outs, accs):
        dy_ref, dres_ref, x_ref = ins[:3]
        w_ref, g_ref = res
        if h is not None:
            accs[1][...] += _dot(ins[3][...], dy_ref[...], _TN)
        if w.ndim == 3:
            cb = w.shape[2]
            dh = _dot(dy_ref[:, 0:cb], w_ref[0], _NT)
            for j in range(1, w.shape[0]):
                dh = dh + _dot(dy_ref[:, j * cb:(j + 1) * cb], w_ref[j], _NT)
        else:
            dh = _dot(dy_ref[...], w_ref[...], _NT)
        dx, dg = _rms_grad(dres_ref[...], dh, x_ref[...], g_ref[...])
        outs[0][...] = dx
        accs[0][...] += dg

    rows_in = [(dy, k, 0), (dres, d, 0), (x, d, 0)] + ([(h, d, 0)] if h is not None else [])
    accs = [((1, d), F32)] + ([((d, k), BF16)] if h is not None else [])
    tm = 1024 if w.ndim == 2 else 512
    return _row_call(name, t, min(tm, t), rows_in, [w, gain], [(d, d, 0, F32)], accs, body)


def _gates_bwd_fused(dx1, p, y_a, y_b, b_gate, w_mix, merged, h1, *, name):
    t, d = y_a.shape

    def body(ins, res, outs, accs):
        dx_ref, ga_ref, gb_ref, ya_ref, yb_ref, m_ref, h1_ref = ins
        bg_ref, wm_ref = res
        dp_ref, dya_ref, dyb_ref = outs
        dbg_ref, dwm_ref, dwin_ref = accs
        dxb = dx_ref[...].astype(BF16)
        dwm_ref[...] += _dot(m_ref[...], dxb, _TN)
        dm = _dot(dxb, wm_ref[...], _NT)
        sa = _sigmoid(ga_ref[...].astype(F32) + bg_ref[0:1, :])
        sb = _sigmoid(gb_ref[...].astype(F32) + bg_ref[1:2, :])
        dya_ref[...] = (dm * sa).astype(BF16)
        dyb_ref[...] = (dm * sb).astype(BF16)
        dga = dm * ya_ref[...].astype(F32) * sa * (1.0 - sa)
        dgb = dm * yb_ref[...].astype(F32) * sb * (1.0 - sb)
        dp_ref[:, 0:d] = dga.astype(BF16)
        dp_ref[:, d:2 * d] = dgb.astype(BF16)
        dbg_ref[0:1, :] += jnp.sum(dga, axis=0, keepdims=True)
        dbg_ref[1:2, :] += jnp.sum(dgb, axis=0, keepdims=True)
        dwin_ref[...] += _dot(h1_ref[...], dp_ref[...], _TN)

    return _row_call(name, t, min(256, t),
                     [(dx1, d, 0), (p, d, 4), (p, d, 5), (y_a, d, 0), (y_b, d, 0), (merged, d, 0), (h1, d, 0)],
                     [b_gate, w_mix], [(p.shape[1], 2 * d, 2, BF16), (d, d, 0, BF16), (d, d, 0, BF16)],
                     [((8, d), F32), ((d, d), BF16), ((d, 2 * d), BF16, p.shape[1], 2)], body)


def _conv_ln_bwd_fused(dy_a, c, a_act, w_conv_out, ln_g, ln_b, *, name):
    t, d = c.shape

    def body(ins, res, outs, accs):
        dy_ref, c_ref, act_ref = ins
        w_ref, lg_ref, lb_ref = res
        dlg_ref, dlb_ref, dw_ref = accs
        dw_ref[...] += _dot(act_ref[...], dy_ref[...], _TN)
        dact = _dot(dy_ref[...], w_ref[...], _NT)
        cv = c_ref[...].astype(F32)
        g = lg_ref[...]
        mu = jnp.mean(cv, axis=-1, keepdims=True)
        dv = cv - mu
        rstd = lax.rsqrt(jnp.mean(dv * dv, axis=-1, keepdims=True) + LN_EPS)
        chat = dv * rstd
        aln = chat * g + lb_ref[...]
        sg = _sigmoid(aln)
        daln = dact * (sg * (1.0 + aln * (1.0 - sg)))
        dlb_ref[...] += jnp.sum(daln, axis=0, keepdims=True)
        dlg_ref[...] += jnp.sum(daln * chat, axis=0, keepdims=True)
        dchat = daln * g
        dc = rstd * (dchat - jnp.mean(dchat, axis=-1, keepdims=True)
                     - chat * jnp.mean(dchat * chat, axis=-1, keepdims=True))
        outs[0][...] = dc.astype(BF16)

    return _row_call(name, t, min(1024, t), [(dy_a, d, 0), (c, d, 0), (a_act, d, 0)], [w_conv_out, ln_g, ln_b],
                     [(d, d, 0, BF16)], [((1, d), F32), ((1, d), F32), ((d, d), BF16)], body)


def _row_spec(tt, cols, col_block=0):
    return pl.BlockSpec((tt, cols), lambda i: (i, col_block))


def _const_spec(shape):
    return pl.BlockSpec(shape, lambda *_: (0,) * len(shape))


def _rms_fwd(x, gain, *, name):
    t, d = x.shape
    tt = min(TOKEN_TILE, t)

    def body(x_ref, g_ref, h_ref):
        xv = x_ref[...]
        r = lax.rsqrt(jnp.mean(xv * xv, axis=-1, keepdims=True) + RMS_EPS)
        h_ref[...] = (xv * r * g_ref[...]).astype(BF16)

    return pl.pallas_call(
        body, name=name, grid=(t // tt,), in_specs=[_row_spec(tt, d), _const_spec((1, d))],
        out_specs=_row_spec(tt, d), out_shape=jax.ShapeDtypeStruct((t, d), BF16),
        compiler_params=_cparams(("parallel",)))(x, gain)


def _rms_gain_grad(dh, x, *, name):
    t, d = x.shape
    tt = min(TOKEN_TILE, t)

    def body(dh_ref, x_ref, dg_ref):
        @pl.when(pl.program_id(0) == 0)
        def _():
            dg_ref[...] = jnp.zeros_like(dg_ref)

        xv = x_ref[...]
        xhat = xv * lax.rsqrt(jnp.mean(xv * xv, axis=-1, keepdims=True) + RMS_EPS)
        dg_ref[...] += jnp.sum(dh_ref[...].astype(F32) * xhat, axis=0, keepdims=True)

    rs = _row_spec(tt, d)
    return pl.pallas_call(
        body, name=name, grid=(t // tt,), in_specs=[rs, rs], out_specs=_const_spec((1, d)),
        out_shape=jax.ShapeDtypeStruct((1, d), F32), compiler_params=_cparams(("arbitrary",)))(dh, x)


SUBLANES = 8
SHIFT_ROWS = 40


def _conv_apply(sbuf_ref, w_ref, out_ref, tt, offsets, bias_ref=None):
    d = out_ref.shape[1]
    for cc in range(d // LANES):
        cs = slice(cc * LANES, (cc + 1) * LANES)
        taps = [jnp.broadcast_to(w_ref[k:k + 1, cs], (SUBLANES, LANES)) for k in range(CONV_WIDTH)]
        bias = None if bias_ref is None else jnp.broadcast_to(bias_ref[:, cs], (SUBLANES, LANES))

        def row_body(r, carry, cs=cs, taps=taps, bias=bias):
            r0 = pl.multiple_of(r * CONV_ROWS, CONV_ROWS)
            for q in range(CONV_ROWS // SUBLANES):
                acc = _tap(sbuf_ref, r0 + q * SUBLANES, cs, offsets[0]) * taps[0]
                for k in range(1, CONV_WIDTH):
                    acc = acc + _tap(sbuf_ref, r0 + q * SUBLANES, cs, offsets[k]) * taps[k]
                if bias is not None:
                    acc = acc + bias
                out_ref[pl.ds(r0 + q * SUBLANES, SUBLANES), cs] = acc
            return carry

        lax.fori_loop(0, tt // CONV_ROWS, row_body, 0)


def _fill_shifts(sbuf_ref, rows):
    d = sbuf_ref.shape[2]
    assert rows % SHIFT_ROWS == 0

    def row_body(i, carry):
        r0 = pl.multiple_of(i * SHIFT_ROWS, SUBLANES)
        for cc in range(d // CONV_COLS):
            cs = slice(cc * CONV_COLS, (cc + 1) * CONV_COLS)
            win = sbuf_ref[0, pl.ds(r0, SHIFT_ROWS + SUBLANES), cs]
            for sh in range(1, SUBLANES):
                sbuf_ref[sh, pl.ds(r0, SHIFT_ROWS), cs] = win[sh:sh + SHIFT_ROWS, :]
        return carry

    lax.fori_loop(0, rows // SHIFT_ROWS, row_body, 0)


def _tap(sbuf_ref, r0, cs, offset):
    sh = offset % SUBLANES
    return sbuf_ref[sh, pl.ds(pl.multiple_of(r0 + (offset - sh), SUBLANES), SUBLANES), cs]


def _conv_specs(bl, s, tt, d, col_a, col_g):
    nj = s // tt
    per = tt // CONV_HALO
    main_a = pl.BlockSpec((tt, d), lambda b, j: (b * nj + j, col_a))
    main_g = pl.BlockSpec((tt, d), lambda b, j: (b * nj + j, col_g))
    prev = lambda b, j: jnp.maximum((b * nj + j) * per - 1, 0)
    halo_a = pl.BlockSpec((CONV_HALO, d), lambda b, j: (prev(b, j), col_a))
    halo_g = pl.BlockSpec((CONV_HALO, d), lambda b, j: (prev(b, j), col_g))
    return main_a, main_g, halo_a, halo_g


def _fill_glu(sbuf_ref, a_ref, g_ref, ha_ref, hg_ref, tt):
    first = pl.program_id(1) == 0
    ha = ha_ref[...].astype(F32)
    hg = hg_ref[...].astype(F32)
    sbuf_ref[0, pl.ds(0, CONV_HALO), :] = jnp.where(first, 0.0, ha * _sigmoid(hg))
    av = a_ref[...].astype(F32)
    gv = g_ref[...].astype(F32)
    sbuf_ref[0, pl.ds(CONV_HALO, tt), :] = av * _sigmoid(gv)
    _fill_shifts(sbuf_ref, tt + CONV_HALO - SUBLANES)


def _conv_fwd(p, conv_w, conv_b, ln_g, ln_b, *, bl, s, name):
    t = p.shape[0]
    d = conv_w.shape[1]
    tt = min(TOKEN_TILE, s)
    off = CONV_HALO - (CONV_WIDTH - 1)

    def body(a_ref, g_ref, ha_ref, hg_ref, w_ref, b_ref, lg_ref, lb_ref, c_ref, act_ref, sbuf_ref, cbuf_ref):
        _fill_glu(sbuf_ref, a_ref, g_ref, ha_ref, hg_ref, tt)

        _conv_apply(sbuf_ref, w_ref, cbuf_ref, tt, [off + k for k in range(CONV_WIDTH)], bias_ref=b_ref)
        cv = cbuf_ref[...]
        c_ref[...] = cv.astype(BF16)
        mu = jnp.mean(cv, axis=-1, keepdims=True)
        dv = cv - mu
        rstd = lax.rsqrt(jnp.mean(dv * dv, axis=-1, keepdims=True) + LN_EPS)
        aln = dv * rstd * lg_ref[...] + lb_ref[...]
        act_ref[...] = (aln * _sigmoid(aln)).astype(BF16)

    main_a, main_g, halo_a, halo_g = _conv_specs(bl, s, tt, d, 0, 1)
    out_spec = pl.BlockSpec((tt, d), lambda b, j: (b * (s // tt) + j, 0))
    return pl.pallas_call(
        body, name=name, grid=(bl, s // tt),
        in_specs=[main_a, main_g, halo_a, halo_g, _const_spec((CONV_HALO, d)), _const_spec((1, d)), _const_spec((1, d)),
                  _const_spec((1, d))],
        out_specs=[out_spec, out_spec],
        out_shape=[jax.ShapeDtypeStruct((t, d), BF16), jax.ShapeDtypeStruct((t, d), BF16)],
        scratch_shapes=[pltpu.VMEM((SUBLANES, tt + CONV_HALO, d), F32), pltpu.VMEM((tt, d), F32)],
        compiler_params=_cparams(("parallel", "parallel")))(p, p, p, p, conv_w, conv_b, ln_g, ln_b)


def _conv_bwd(dp, dc, p, conv_w, h1, dw_in, *, bl, s, name):
    t = p.shape[0]
    d = conv_w.shape[1]
    tt = min(TOKEN_TILE, s)
    nj = s // tt
    per = tt // CONV_HALO
    off = CONV_HALO - (CONV_WIDTH - 1)
    last_blk = t // CONV_HALO - 1

    def body(dp_in, dc_ref, dcn_ref, a_ref, g_ref, ha_ref, hg_ref, w_ref, h1_ref, dwin_in, dp_ref, dw_ref, db_ref,
             dwin_out, gbuf_ref, dbuf_ref, dglu_ref, acc_ref, dwin_ref, hprev_ref, dpprev_ref):
        del dp_in, dwin_in
        b, j = pl.program_id(0), pl.program_id(1)
        start = jnp.logical_and(b == 0, j == 0)
        end = jnp.logical_and(b == bl - 1, j == nj - 1)

        @pl.when(start)
        def _():
            acc_ref[...] = jnp.zeros_like(acc_ref)
            db_ref[...] = jnp.zeros_like(db_ref)
            dwin_ref[...] = jnp.zeros_like(dwin_ref)

        _fill_glu(gbuf_ref, a_ref, g_ref, ha_ref, hg_ref, tt)
        dcv = dc_ref[...].astype(F32)
        dbuf_ref[0, pl.ds(0, tt), :] = dcv
        dbuf_ref[0, pl.ds(tt, CONV_HALO), :] = jnp.where(j == nj - 1, 0.0, dcn_ref[...].astype(F32))
        _fill_shifts(dbuf_ref, tt + CONV_HALO - SUBLANES)
        db_ref[...] += jnp.sum(dcv, axis=0, keepdims=True)

        for cc in range(d // LANES):
            cs = slice(cc * LANES, (cc + 1) * LANES)

            def row_body(r, accs, cs=cs):
                r0 = pl.multiple_of(r * CONV_ROWS, CONV_ROWS)
                accs = list(accs)
                for q in range(CONV_ROWS // SUBLANES):
                    dcw = dbuf_ref[0, pl.ds(r0 + q * SUBLANES, SUBLANES), cs]
                    for k in range(CONV_WIDTH):
                        accs[k] = accs[k] + dcw * _tap(gbuf_ref, r0 + q * SUBLANES, cs, off + k)
                return tuple(accs)

            zero = jnp.zeros((SUBLANES, LANES), F32)
            accs = lax.fori_loop(0, tt // CONV_ROWS, row_body, (zero,) * CONV_WIDTH)
            for k in range(CONV_WIDTH):
                acc_ref[k, :, cs] += accs[k]

        _conv_apply(dbuf_ref, w_ref, dglu_ref, tt, [CONV_WIDTH - 1 - k for k in range(CONV_WIDTH)])
        dglu = dglu_ref[...]
        av = a_ref[...].astype(F32)
        sg = _sigmoid(g_ref[...].astype(F32))
        dp_ref[:, 0:d] = (dglu * sg).astype(BF16)
        dp_ref[:, d:2 * d] = (dglu * av * sg * (1.0 - sg)).astype(BF16)
        odd = (b * nj + j) % 2 == 1

        @pl.when(jnp.logical_not(odd))
        def _():
            hprev_ref[...] = h1_ref[...]
            dpprev_ref[...] = dp_ref[...]

        @pl.when(odd)
        def _():
            dwin_ref[...] += _dot(jnp.concatenate([hprev_ref[...], h1_ref[...]], axis=0),
                                  jnp.concatenate([dpprev_ref[...], dp_ref[...]], axis=0), _TN)

        @pl.when(end)
        def _():
            for k in range(CONV_WIDTH):
                dw_ref[k:k + 1, :] = jnp.sum(acc_ref[k], axis=0, keepdims=True)
            dw_ref[CONV_WIDTH:CONV_HALO, :] = jnp.zeros((CONV_HALO - CONV_WIDTH, d), F32)
            if (bl * nj) % 2 == 1:
                dwin_ref[...] += _dot(hprev_ref[...], dpprev_ref[...], _TN)
            dwin_out[...] = dwin_ref[...].astype(dwin_out.dtype)

    main_a, main_g, halo_a, halo_g = _conv_specs(bl, s, tt, d, 0, 1)
    dc_main = pl.BlockSpec((tt, d), lambda b, j: (b * nj + j, 0))
    dc_next = pl.BlockSpec((CONV_HALO, d), lambda b, j: (jnp.minimum((b * nj + j + 1) * per, last_blk), 0))
    hbm = pl.BlockSpec(memory_space=pl.ANY)
    return pl.pallas_call(
        body, name=name, grid=(bl, nj),
        in_specs=[hbm, dc_main, dc_next, main_a, main_g, halo_a, halo_g, _const_spec((CONV_HALO, d)), dc_main, hbm],
        out_specs=[pl.BlockSpec((tt, 2 * d), lambda b, j: (b * nj + j, 0)), _const_spec((CONV_HALO, d)), _const_spec((1, d)),
                   _const_spec((d, 2 * d))],
        out_shape=[jax.ShapeDtypeStruct(dp.shape, dp.dtype), jax.ShapeDtypeStruct((CONV_HALO, d), F32),
                   jax.ShapeDtypeStruct((1, d), F32), jax.ShapeDtypeStruct(dw_in.shape, dw_in.dtype)],
        scratch_shapes=[pltpu.VMEM((SUBLANES, tt + CONV_HALO, d), F32), pltpu.VMEM((SUBLANES, tt + CONV_HALO, d), F32),
                        pltpu.VMEM((tt, d), F32), pltpu.VMEM((CONV_HALO, SUBLANES, d), F32), pltpu.VMEM((d, 2 * d), F32),
                        pltpu.VMEM((tt, d), BF16), pltpu.VMEM((tt, 2 * d), BF16)],
        input_output_aliases={0: 0, 9: 3},
        compiler_params=_cparams(("arbitrary", "arbitrary")))(dp, dc, dc, p, p, p, p, conv_w, h1, dw_in)


def _sgu_stats(bv):
    gv = _gelu(bv)
    mu = jnp.mean(gv, axis=-1, keepdims=True)
    dv = gv - mu
    rstd = lax.rsqrt(jnp.mean(dv * dv, axis=-1, keepdims=True) + LN_EPS)
    return dv * rstd, rstd


def _sgu_fwd(p, wm, bias, ln_g, ln_b, *, name):
    t = p.shape[0]
    d = ln_g.shape[1]
    tt = SGU_TILE
    gd = d // SGU_GROUPS

    def body(u_ref, v_ref, wm_ref, bias_ref, lg_ref, lb_ref, sg_ref, vn_ref):
        u = _gelu(u_ref[...].astype(F32))
        vhat, _ = _sgu_stats(v_ref[...].astype(F32))
        vb = (vhat * lg_ref[...] + lb_ref[...]).astype(BF16)
        vn_ref[...] = vb
        for ci in range(tt // SGU_CHUNK):
            rows = slice(ci * SGU_CHUNK, (ci + 1) * SGU_CHUNK)
            for g in range(SGU_GROUPS):
                gs = slice(g * gd, (g + 1) * gd)
                z = _dot(wm_ref[g], vb[rows, gs], _NN) + bias_ref[g]
                sg_ref[rows, gs] = (u[rows, gs] * z).astype(BF16)

    rs = _row_spec(tt, d)
    return pl.pallas_call(
        body, name=name, grid=(t // tt,),
        in_specs=[_row_spec(tt, d, 2), _row_spec(tt, d, 3), _const_spec(wm.shape), _const_spec(bias.shape),
                  _const_spec((1, d)), _const_spec((1, d))],
        out_specs=[rs, rs], out_shape=[jax.ShapeDtypeStruct((t, d), BF16), jax.ShapeDtypeStruct((t, d), BF16)],
        compiler_params=_cparams(("parallel",)))(p, p, wm, bias, ln_g, ln_b)


def _sgu_bwd(dp, dy_b, w_out, p, vn, wm, wmt, bias, ln_g, *, name):
    t = p.shape[0]
    d = ln_g.shape[1]
    tt = SGU_TILE
    ck = SGU_CHUNK
    gd = d // SGU_GROUPS
    nsteps = t // tt

    def body(dp_in, dyb_ref, wout_ref, u_ref, v_ref, vn_ref, wm_ref, wmt_ref, bias_ref, lg_ref,
             dp_ref, dw_ref, dbs_ref, dlg_ref, dlb_ref, dz_acc):
        del dp_in
        i = pl.program_id(0)

        @pl.when(i == 0)
        def _():
            dw_ref[...] = jnp.zeros_like(dw_ref)
            dlg_ref[...] = jnp.zeros_like(dlg_ref)
            dlb_ref[...] = jnp.zeros_like(dlb_ref)
            dz_acc[...] = jnp.zeros_like(dz_acc)

        bu = u_ref[...].astype(F32)
        bv = v_ref[...].astype(F32)
        u = _gelu(bu)
        vhat, rstd = _sgu_stats(bv)
        vb = vn_ref[...]
        dsg = _dot(dyb_ref[...], wout_ref[...], _NT)
        row = lax.broadcasted_iota(jnp.int32, (ck, ck), 0)
        col = lax.broadcasted_iota(jnp.int32, (ck, ck), 1)
        causal = col <= row
        du_rows, dv_rows = [], []
        for ci in range(tt // ck):
            rows = slice(ci * ck, (ci + 1) * ck)
            du_parts, dv_parts = [], []
            for g in range(SGU_GROUPS):
                gs = slice(g * gd, (g + 1) * gd)
                z = _dot(wm_ref[g], vb[rows, gs], _NN) + bias_ref[g]
                du_parts.append(dsg[rows, gs] * z)
                dz = dsg[rows, gs] * u[rows, gs]
                dz_acc[:, gs] += dz
                dzb = dz.astype(BF16)
                dw_ref[g] += jnp.where(causal, _dot(dzb, vb[rows, gs], _NT), 0.0)
                dv_parts.append(_dot(wmt_ref[g], dzb, _NN))
            du_rows.append(jnp.concatenate(du_parts, axis=1))
            dv_rows.append(jnp.concatenate(dv_parts, axis=1))
        du = jnp.concatenate(du_rows, axis=0)
        dv = jnp.concatenate(dv_rows, axis=0)
        dp_ref[:, 0:d] = (du * _gelu_grad(bu)).astype(BF16)
        dlb_ref[...] += jnp.sum(dv, axis=0, keepdims=True)
        dlg_ref[...] += jnp.sum(dv * vhat, axis=0, keepdims=True)
        dvh = dv * lg_ref[...]
        dgv = rstd * (dvh - jnp.mean(dvh, axis=-1, keepdims=True) - vhat * jnp.mean(dvh * vhat, axis=-1, keepdims=True))
        dp_ref[:, d:2 * d] = (dgv * _gelu_grad(bv)).astype(BF16)

        @pl.when(i == nsteps - 1)
        def _():
            ones = jnp.ones((8, gd), F32)
            for g in range(SGU_GROUPS):
                gs = slice(g * gd, (g + 1) * gd)
                tot = lax.dot_general(ones, dz_acc[:, gs], (_NT, ((), ())), preferred_element_type=F32,
                                      precision=lax.Precision.HIGHEST)
                dbs_ref[g:g + 1, :] = tot[0:1, :]

    rs = _row_spec(tt, d)
    c1 = _const_spec((1, d))
    return pl.pallas_call(
        body, name=name, grid=(nsteps,),
        in_specs=[pl.BlockSpec(memory_space=pl.ANY), rs, _const_spec(w_out.shape), _row_spec(tt, d, 2), _row_spec(tt, d, 3),
                  rs, _const_spec(wm.shape), _const_spec(wmt.shape), _const_spec(bias.shape), c1],
        out_specs=[pl.BlockSpec((tt, 2 * d), lambda i: (i, 1)), _const_spec(wm.shape), _const_spec((SGU_GROUPS, ck)), c1, c1],
        out_shape=[jax.ShapeDtypeStruct(dp.shape, dp.dtype), jax.ShapeDtypeStruct(wm.shape, F32),
                   jax.ShapeDtypeStruct((SGU_GROUPS, ck), F32), jax.ShapeDtypeStruct((1, d), F32),
                   jax.ShapeDtypeStruct((1, d), F32)],
        scratch_shapes=[pltpu.VMEM((ck, d), F32)],
        input_output_aliases={0: 0},
        compiler_params=_cparams(("arbitrary",)))(dp, dy_b, w_out, p, p, vn, wm, wmt, bias, ln_g)


def _softmax_rows(s):
    e = jnp.exp(s - jnp.max(s, axis=-1, keepdims=True))
    return e / jnp.sum(e, axis=-1, keepdims=True)


def _attn_fwd(q, kv, x1, w_xo, gain, *, bl, s, name):
    t, d = q.shape
    mlen = kv.shape[0] // bl
    hd = d // HEADS
    tq = min(ATTN_TILE, s)
    nq = s // tq
    scale = hd ** -0.5

    def body(q_ref, kv_ref, x1_ref, w_ref, g_ref, o_ref, x2_ref, h_ref):
        for h in range(HEADS):
            hs = slice(h * hd, (h + 1) * hd)
            vs = slice(d + h * hd, d + (h + 1) * hd)
            pr = _softmax_rows(_dot(q_ref[:, hs], kv_ref[:, hs], _NT) * scale)
            o_ref[:, hs] = _dot(pr.astype(BF16), kv_ref[:, vs], _NN).astype(BF16)
        x2 = x1_ref[...] + _dot(o_ref[...], w_ref[...], _NN)
        x2_ref[...] = x2
        h_ref[...] = _rms_apply(x2, g_ref[...]).astype(BF16)

    qs = pl.BlockSpec((tq, d), lambda b, j: (b * nq + j, 0))
    return pl.pallas_call(
        body, name=name, grid=(bl, nq),
        in_specs=[qs, pl.BlockSpec((mlen, 2 * d), lambda b, j: (b, 0)), qs, _const_spec(w_xo.shape), _const_spec((1, d))],
        out_specs=[qs, qs, qs],
        out_shape=[jax.ShapeDtypeStruct((t, d), BF16), jax.ShapeDtypeStruct((t, d), F32), jax.ShapeDtypeStruct((t, d), BF16)],
        compiler_params=_cparams(("parallel", "parallel")))(q, kv, x1, w_xo, gain)


def _attn_bwd(q, kv, do, *, bl, s, name):
    t, d = q.shape
    mlen = kv.shape[0] // bl
    hd = d // HEADS
    tq = min(ATTN_TILE, s)
    nq = s // tq
    scale = hd ** -0.5

    def body(q_ref, kv_ref, do_ref, dq_ref, dkv_ref):
        @pl.when(pl.program_id(1) == 0)
        def _():
            dkv_ref[...] = jnp.zeros_like(dkv_ref)

        for h in range(HEADS):
            hs = slice(h * hd, (h + 1) * hd)
            vs = slice(d + h * hd, d + (h + 1) * hd)
            qh, kh, vh, doh = q_ref[:, hs], kv_ref[:, hs], kv_ref[:, vs], do_ref[:, hs]
            pr = _softmax_rows(_dot(qh, kh, _NT) * scale)
            dpr = _dot(doh, vh, _NT)
            dkv_ref[:, vs] += _dot(pr.astype(BF16), doh, _TN)
            ds = (pr * (dpr - jnp.sum(dpr * pr, axis=-1, keepdims=True)) * scale).astype(BF16)
            dq_ref[:, hs] = _dot(ds, kh, _NN).astype(BF16)
            dkv_ref[:, hs] += _dot(ds, qh, _TN)

    qs = pl.BlockSpec((tq, d), lambda b, j: (b * nq + j, 0))
    ks = pl.BlockSpec((mlen, 2 * d), lambda b, j: (b, 0))
    return pl.pallas_call(
        body, name=name, grid=(bl, nq), in_specs=[qs, ks, qs], out_specs=[qs, ks],
        out_shape=[jax.ShapeDtypeStruct((t, d), BF16), jax.ShapeDtypeStruct(kv.shape, F32)],
        compiler_params=_cparams(("parallel", "arbitrary")))(q, kv, do)


def _mesh_pos():
    return lax.axis_index("x"), lax.axis_index("y"), lax.axis_index("c")


def _all_gather(arrs, *, name):
    n = len(arrs)
    hbm = pl.BlockSpec(memory_space=pl.ANY)

    def body(*refs):
        ins, outs = refs[:n], refs[n:2 * n]
        send_sems, recv_sems, loc_sems = refs[2 * n:]
        x, y, c = _mesh_pos()
        me, sib = (x, y, c), (x, y, 1 - c)
        chips = [(1 - x, y), (x, 1 - y), (1 - x, 1 - y)]

        def idx(dev):
            return 4 * dev[0] + 2 * dev[1] + dev[2]

        def copy(w, k, block, to, from_input=False):
            return pltpu.make_async_remote_copy(
                src_ref=ins[w] if from_input else outs[w].at[idx(block)], dst_ref=outs[w].at[idx(block)],
                send_sem=send_sems.at[w, k], recv_sem=recv_sems.at[w, k], device_id=to, device_id_type=MESH_ID)

        own = [pltpu.make_async_copy(ins[w], outs[w].at[idx(me)], loc_sems.at[w]) for w in range(n)]
        for cp in own:
            cp.start()
        first = []
        for w in range(n):
            first.append(copy(w, 0, me, sib, True))
            first += [copy(w, 1 + j, me, (*chip, c), True) for j, chip in enumerate(chips)]
        for cp in first:
            cp.start()
        passed = []
        for j, chip in enumerate(chips):
            for w in range(n):
                copy(w, 1 + j, (*chip, c), me).wait_recv()
                fwd = copy(w, 4 + j, (*chip, c), sib)
                fwd.start()
                passed.append(fwd)
        for w in range(n):
            copy(w, 0, sib, me).wait_recv()
            for j, chip in enumerate(chips):
                copy(w, 4 + j, (*chip, 1 - c), me).wait_recv()
        for cp in first + passed:
            cp.wait_send()
        for cp in own:
            cp.wait()

    return pl.pallas_call(
        body, name=name, in_specs=[hbm] * n, out_specs=[hbm] * n,
        out_shape=[jax.ShapeDtypeStruct((N_DEV, *a.shape), a.dtype) for a in arrs],
        scratch_shapes=[pltpu.SemaphoreType.DMA((n, 7)), pltpu.SemaphoreType.DMA((n, 7)), pltpu.SemaphoreType.DMA((n,))],
    )(*arrs)


_HBM = pl.BlockSpec(memory_space=pltpu.HBM)
_SEM = pl.BlockSpec(memory_space=pltpu.SEMAPHORE)
_ANY = pl.BlockSpec(memory_space=pl.ANY)
_EFFECT = pltpu.SideEffectType.DATAFLOW_SIDE_EFFECTING
N_PEERS = N_DEV - 1


def _related(pos, r):
    x, y, c = pos
    return (1 - x if r & 4 else x, 1 - y if r & 2 else y, 1 - c if r & 1 else c)


def _dev_index(dev):
    return 4 * dev[0] + 2 * dev[1] + dev[2]


def _in_hbm(a):
    return pltpu.with_memory_space_constraint(a, pltpu.HBM)


def _split_copies(kind, srcs, lands, send_sems, recv_sems):
    pos = _mesh_pos()
    me = _dev_index(pos)
    out = []
    for w in range(len(srcs)):
        for r in range(1, N_DEV):
            peer = _related(pos, r)
            if kind == "gather":
                src, dst_here, dst_there = srcs[w], lands[w].at[_dev_index(peer)], lands[w].at[me]
            elif srcs[w].ndim == 2:
                cb = lands[w].shape[2]
                src = srcs[w].at[:, pl.ds(pl.multiple_of(_dev_index(peer) * cb, LANES), cb)]
                dst_here = dst_there = lands[w].at[r - 1]
            else:
                src, dst_here, dst_there = srcs[w].at[_dev_index(peer)], lands[w].at[r - 1], lands[w].at[r - 1]
            out.append((src, dst_here, dst_there, send_sems.at[w * N_PEERS + r - 1], recv_sems.at[w * N_PEERS + r - 1], peer))
    return out


def _copy_start(kind, srcs, land_shapes, *, name, after=None):
    n = len(srcs)
    n_after = 0 if after is None else 1

    def body(*refs):
        src_refs, land_refs = refs[:n], refs[n:2 * n]
        send_sems, recv_sems = refs[2 * n + n_after], refs[2 * n + n_after + 1]
        token = refs[-1]
        for src, _, dst, ssem, rsem, peer in _split_copies(kind, src_refs, land_refs, send_sems, recv_sems):
            pltpu.make_async_remote_copy(src_ref=src, dst_ref=dst, send_sem=ssem, recv_sem=rsem, device_id=peer,
                                         device_id_type=MESH_ID).start()
        token[...] = jnp.zeros_like(token)

    lands = [_in_hbm(lax.empty(shape, s.dtype)) for s, shape in zip(srcs, land_shapes)]
    res = pl.pallas_call(
        body, name=name,
        out_shape=(pltpu.SemaphoreType.DMA((n * N_PEERS,)), pltpu.SemaphoreType.DMA((n * N_PEERS,)),
                   *[pltpu.HBM(s.shape, s.dtype) for s in srcs], *[pltpu.HBM(l.shape, l.dtype) for l in lands],
                   jax.ShapeDtypeStruct((8, 128), F32)),
        in_specs=[_HBM] * (2 * n) + [_ANY] * n_after,
        out_specs=(_SEM, _SEM, *[_HBM] * (2 * n), pl.BlockSpec(memory_space=pltpu.VMEM)),
        input_output_aliases={i: 2 + i for i in range(2 * n)},
        compiler_params=pltpu.CompilerParams(has_side_effects=_EFFECT),
    )(*[_in_hbm(s) for s in srcs], *lands, *([] if after is None else [after]))
    return res[0], res[1], list(res[2:2 + n]), list(res[2 + n:2 + 2 * n]), res[-1]


def _copy_wait(kind, send_sems, recv_sems, srcs, lands, after, *, name):
    n = len(srcs)

    def body(*refs):
        src_refs, land_refs = refs[:n], refs[n:2 * n]
        ssems, rsems = refs[2 * n], refs[2 * n + 1]
        for src, dst, _, ssem, rsem, peer in _split_copies(kind, src_refs, land_refs, ssems, rsems):
            cp = pltpu.make_async_remote_copy(src_ref=src, dst_ref=dst, send_sem=ssem, recv_sem=rsem, device_id=peer,
                                              device_id_type=MESH_ID)
            cp.wait_send()
            cp.wait_recv()

    res = pl.pallas_call(
        body, name=name,
        out_shape=(*[pltpu.HBM(s.shape, s.dtype) for s in srcs], *[pltpu.HBM(l.shape, l.dtype) for l in lands]),
        in_specs=[_HBM] * (2 * n) + [_SEM, _SEM, _ANY], out_specs=tuple([_HBM] * (2 * n)),
        input_output_aliases={i: i for i in range(2 * n)},
        compiler_params=pltpu.CompilerParams(has_side_effects=_EFFECT),
    )(*srcs, *lands, send_sems, recv_sems, after)
    return list(res[:n]), list(res[n:])


def _row_tile(rows):
    return max(tr for tr in range(16, min(rows, 512) + 1, 16) if rows % tr == 0)


def _adamw_math(w, g, m, v):
    m2 = ADAM_B1 * m + (1.0 - ADAM_B1) * g
    v2 = ADAM_B2 * v + (1.0 - ADAM_B2) * (g * g)
    m_hat = m2 / (1.0 - ADAM_B1 ** ADAM_STEP)
    v_hat = v2 / (1.0 - ADAM_B2 ** ADAM_STEP)
    delta = -ADAM_LR * (m_hat / (jnp.sqrt(v_hat) + ADAM_EPS) + ADAM_WD * w)
    return delta, m2, v2


def _adamw_shard(partials, landed, dev, w, m, v, *, name):
    r, c = w.shape
    tr = _row_tile(r)

    def body(dev_ref, p_ref, l_ref, w_ref, m_ref, v_ref, g_out, d_out, m_out, v_out):
        del dev_ref
        g = p_ref[...].astype(F32)
        for k in range(N_PEERS):
            g = g + l_ref[k].astype(F32)
        delta, m2, v2 = _adamw_math(w_ref[...], g, m_ref[...], v_ref[...])
        g_out[...] = g
        d_out[...] = delta
        m_out[...] = m2
        v_out[...] = v2

    blk = pl.BlockSpec((tr, c), lambda i, dev_ref: (i, 0))
    if partials.ndim == 2:
        own = pl.BlockSpec((tr, c), lambda i, dev_ref: (i, dev_ref[0]))
    else:
        own = pl.BlockSpec((None, tr, c), lambda i, dev_ref: (dev_ref[0], i, 0))
    gs = pltpu.PrefetchScalarGridSpec(
        num_scalar_prefetch=1, grid=(r // tr,),
        in_specs=[own, pl.BlockSpec((N_PEERS, tr, c), lambda i, dev_ref: (0, i, 0)), blk, blk, blk],
        out_specs=[blk] * 4)
    return pl.pallas_call(
        body, name=name, grid_spec=gs, out_shape=[jax.ShapeDtypeStruct((r, c), F32)] * 4,
        compiler_params=_cparams(("parallel",)))(dev, partials, landed, w, m, v)


def _sum_devices(p_ref, *idx):
    g = p_ref[(0, *idx)]
    for k in range(1, N_DEV):
        g = g + p_ref[(k, *idx)]
    return g


def _adamw_replicated(parts, states, loss_row, *, name):
    n_parts, n_par = len(parts), len(states)
    n_vec = n_par - (n_parts - 1)

    def body(*refs):
        part_refs, st = refs[:n_parts], refs[n_parts:n_parts + 3 * n_par]
        outs = refs[n_parts + 3 * n_par:]
        outs[0][...] = _sum_devices(part_refs[0], slice(loss_row, loss_row + 1), slice(0, 1))
        for i in range(n_par):
            g = _sum_devices(part_refs[0], slice(i, i + 1)) if i < n_vec else _sum_devices(part_refs[1 + i - n_vec])
            delta, m2, v2 = _adamw_math(st[3 * i][...], g, st[3 * i + 1][...], st[3 * i + 2][...])
            for o, val in zip(outs[1 + 4 * i:5 + 4 * i], (g, delta, m2, v2)):
                o[...] = val

    flat = [a for wmv in states for a in wmv]
    return pl.pallas_call(
        body, name=name,
        out_shape=[jax.ShapeDtypeStruct((1, 1), F32)] + [jax.ShapeDtypeStruct(w.shape, F32) for w, _, _ in states for _ in range(4)],
        compiler_params=pltpu.CompilerParams(vmem_limit_bytes=VMEM_LIMIT))(*parts, *flat)


def _adamw_column_shards(parts, dev, states, row0s, *, name):
    _, rows, _ = parts.shape
    c = states[0][0].shape[1]

    def body(dev_ref, p_ref, *refs):
        del dev_ref
        st, outs = refs[:3 * len(states)], refs[3 * len(states):]
        for j, r0 in enumerate(row0s):
            w_ref = st[3 * j]
            g = _sum_devices(p_ref, slice(r0, r0 + w_ref.shape[0]))
            delta, m2, v2 = _adamw_math(w_ref[...], g, st[3 * j + 1][...], st[3 * j + 2][...])
            for o, val in zip(outs[4 * j:4 * j + 4], (g, delta, m2, v2)):
                o[...] = val

    whole = lambda a: pl.BlockSpec(a.shape, lambda i, dev_ref: (0, 0))
    flat = [a for wmv in states for a in wmv]
    outs = [w for w, _, _ in states for _ in range(4)]
    gs = pltpu.PrefetchScalarGridSpec(
        num_scalar_prefetch=1, grid=(1,),
        in_specs=[pl.BlockSpec((N_DEV, rows, c), lambda i, dev_ref: (0, 0, dev_ref[0]))] + [whole(a) for a in flat],
        out_specs=[whole(a) for a in outs])
    return pl.pallas_call(
        body, name=name, grid_spec=gs, out_shape=[jax.ShapeDtypeStruct(a.shape, F32) for a in outs],
        compiler_params=_cparams(("arbitrary",)))(dev, parts, *flat)


def _pad_rows(a, rows):
    return jnp.pad(a, ((0, rows - a.shape[0]), (0, 0)))


def _unblock_cols(g):
    return jnp.transpose(g, (1, 0, 2)).reshape(g.shape[1], N_DEV * g.shape[2])


def kernel(x, mem, norm_mix, w_in, b_gate, conv_w, conv_b, conv_ln_g, conv_ln_b, w_conv_out, sgu_ln_g, sgu_ln_b, sgu_w, sgu_b, w_sgu_out, w_mix_out, norm_xattn, norm_mem, w_q, w_kv, w_xo, norm_ffn, w_gu, w_down, norm_final, loss_target, m_norm_mix, m_w_in, m_b_gate, m_conv_w, m_conv_b, m_conv_ln_g, m_conv_ln_b, m_w_conv_out, m_sgu_ln_g, m_sgu_ln_b, m_sgu_w, m_sgu_b, m_w_sgu_out, m_w_mix_out, m_norm_xattn, m_norm_mem, m_w_q, m_w_kv, m_w_xo, m_norm_ffn, m_w_gu, m_w_down, m_norm_final, v_norm_mix, v_w_in, v_b_gate, v_conv_w, v_conv_b, v_conv_ln_g, v_conv_ln_b, v_w_conv_out, v_sgu_ln_g, v_sgu_ln_b, v_sgu_w, v_sgu_b, v_w_sgu_out, v_w_mix_out, v_norm_xattn, v_norm_mem, v_w_q, v_w_kv, v_w_xo, v_norm_ffn, v_w_gu, v_w_down, v_norm_final):
    given = dict(locals())
    bl, s, d = x.shape
    t = bl * s
    xf = x.reshape(t, d)
    tgt = loss_target.reshape(t, d)
    memf = mem.reshape(bl * mem.shape[1], d)
    cx, cy, cc = lax.axis_index("x"), lax.axis_index("y"), lax.axis_index("c")
    dev = 4 * cx + 2 * cy + cc
    dev_id = dev.astype(jnp.int32).reshape(1)
    col_sharded = ["w_in", "w_kv"]
    transposed = ["w_gu"]

    def shard_of(name, prefix=""):
        a = given[prefix + name][0]
        return jnp.transpose(a) if name in transposed else a

    def full_weight(name, blocks):
        return _unblock_cols(blocks) if name in col_sharded else blocks.reshape(N_DEV * blocks.shape[1], blocks.shape[2])

    g_bg, g_cw = _all_gather([_pad_rows(b_gate[0], 8), _pad_rows(conv_w[0], CONV_HALO)], name="gather_small_params")
    h1, p, w_in_blocks = _in_proj_gather(xf, norm_mix + g_bg[0, 7:8, 0:1], w_in[0].astype(BF16), name="in_proj")
    early = ["w_conv_out", "w_sgu_out", "w_mix_out", "w_q", "w_kv", "w_xo"]
    late = ["w_gu", "w_down"]
    shards = {n: shard_of(n).astype(BF16) for n in early + late}
    started = {}
    for grp, names in (("early", early), ("late", late)):
        srcs = [shards[n] for n in names]
        started[grp] = _copy_start("gather", srcs, [(N_DEV, *a.shape) for a in srcs], name=f"gather_{grp}_start", after=p)
    token = started["early"][4][0:1, 0:1] + started["late"][4][0:1, 0:1]
    wfull = {}
    bg_full = _unblock_cols(g_bg)
    cw_full = _unblock_cols(g_cw)

    def finish_gather(grp, names, after):
        ssem, rsem, srcs, lands, _ = started[grp]
        _, lands = _copy_wait("gather", ssem, rsem, srcs, lands, after, name=f"gather_{grp}_wait")
        for n, land in zip(names, lands):
            wfull[n] = full_weight(n, lax.dynamic_update_index_in_dim(land, shards[n], dev, 0))

    tri = jnp.tril(jnp.ones((SGU_CHUNK, SGU_CHUNK), bool))
    wm32 = jnp.where(tri[None], sgu_w[0], 0.0)
    wm = wm32.astype(BF16)
    wmt = jnp.transpose(wm32, (0, 2, 1)).astype(BF16)
    sgu_bias = jnp.broadcast_to(sgu_b[0][:, :, None], (SGU_GROUPS, SGU_CHUNK, d // SGU_GROUPS))

    c_conv, a_act = _conv_fwd(p, cw_full, conv_b + token, conv_ln_g, conv_ln_b, bl=bl, s=s, name="conv_fwd")
    sg, vn = _sgu_fwd(p, wm, sgu_bias, sgu_ln_g, sgu_ln_b + token, name="sgu_fwd")
    finish_gather("early", early, a_act[0:16, 0:128] + sg[0:16, 0:128])
    y_a = _matmul(a_act, wfull["w_conv_out"], mode="nn", out_dtype=BF16, name="mm_conv_out", tm=1024, tn=1024, tk=1024)
    y_b = _matmul(sg, wfull["w_sgu_out"], mode="nn", out_dtype=BF16, name="mm_sgu_out", tm=1024, tn=1024, tk=1024)
    merged, x1, h2, q = _mix_out(p, y_a, y_b, bg_full, xf, wfull["w_mix_out"], norm_xattn, wfull["w_q"], name="mix_out")
    mem_n = _rms_fwd(memf, norm_mem, name="rms_mem")
    kv = _matmul(mem_n, wfull["w_kv"], mode="nn", out_dtype=BF16, name="mm_kv", tm=1024, tn=1024, tk=1024)
    o, x2, h3 = _attn_fwd(q, kv, x1, wfull["w_xo"], norm_ffn, bl=bl, s=s, name="attn_fwd")
    finish_gather("late", late, h3)
    gu, act, dx3, loss_part, d_norm_final = _ffn_fwd(h3, x2, tgt, wfull["w_gu"], wfull["w_down"],
                                                     norm_final.reshape(1, d), name="ffn_fwd")

    grads = {}
    sent = []

    def send_grads(names, tag, after=None):
        blocks, land_shapes = [], []
        for n in names:
            g = grads[n]
            if g.ndim == 2 and n in col_sharded:
                land_shapes.append((N_PEERS, g.shape[0], g.shape[1] // N_DEV))
            else:
                if g.ndim == 2:
                    g = g.reshape(N_DEV, -1, g.shape[1])
                land_shapes.append((N_PEERS, *g.shape[1:]))
            blocks.append(g)
        ssem, rsem, srcs, lands, tok = _copy_start("scatter", blocks, land_shapes, name=f"grads_{tag}_start", after=after)
        sent.append((names, ssem, rsem, srcs, lands))
        return tok[0:1, 0:1]

    dgu, dx2, do, d_norm_ffn = _ffn_bwd(dx3, gu, x2, wfull["w_down"], wfull["w_gu"], norm_ffn, wfull["w_xo"], name="ffn_bwd")
    grads["w_down"] = _matmul(act, dx3, mode="tn", out_dtype=BF16, name="mm_dw_down", tm=1408, tn=1024, tk=2048)
    grads["w_gu"] = _matmul(dgu, h3, mode="tn", out_dtype=BF16, name="mm_dw_gu", tm=1408, tn=1024, tk=2048)
    tok = send_grads(["w_down", "w_gu"], "ffn")
    grads["w_xo"] = _matmul(o, dx2, mode="tn", out_dtype=BF16, name="mm_dw_xo", tm=1024, tn=1024, tk=2048)
    dq, dkv = _attn_bwd(q, kv, do, bl=bl, s=s, name="attn_bwd")
    grads["w_kv"] = _matmul(mem_n, dkv, mode="tn", out_dtype=BF16, name="mm_dw_kv", tm=1024, tn=256, tk=1024,
                            col_blocks=N_DEV)
    tok2 = send_grads(["w_xo", "w_kv"], "attn")
    dmem_n = _matmul(dkv, wfull["w_kv"], mode="nt", out_dtype=F32, name="mm_d_mem", tm=512, tn=1024, tk=2048)
    d_norm_mem = _rms_gain_grad(dmem_n, memf, name="rms_mem_bwd")
    dx1, d_norm_xattn, dw_q = _proj_rms_bwd(dq, dx2, x1, wfull["w_q"], norm_xattn + (tok + tok2), name="q_rms_bwd", h=h2)
    dp, dy_a, dy_b, d_b_gate, dw_mix, dw_in_gates = _gates_bwd_fused(dx1, p, y_a, y_b, bg_full, wfull["w_mix_out"],
                                                                    merged, h1, name="gates_bwd")
    grads["w_q"] = dw_q.astype(BF16)
    grads["w_mix_out"] = dw_mix.astype(BF16)
    grads["w_sgu_out"] = _matmul(sg, dy_b, mode="tn", out_dtype=BF16, name="mm_dw_sgu", tm=1024, tn=1024, tk=2048)
    dc, d_conv_ln_g, d_conv_ln_b, dw_conv = _conv_ln_bwd_fused(dy_a, c_conv, a_act, wfull["w_conv_out"], conv_ln_g,
                                                               conv_ln_b, name="conv_ln_bwd")
    grads["w_conv_out"] = dw_conv.astype(BF16)
    tok = send_grads(["w_q", "w_mix_out", "w_sgu_out", "w_conv_out"], "mixer")
    dp, d_sgu_w, d_sgu_b, d_sgu_ln_g, d_sgu_ln_b = _sgu_bwd(dp, dy_b, wfull["w_sgu_out"], p, vn, wm, wmt, sgu_bias,
                                                             sgu_ln_g + tok, name="sgu_bwd")
    sgw_ssem, sgw_rsem, sgw_src, sgw_land, tok = _copy_start("gather", [d_sgu_w], [(N_DEV, *d_sgu_w.shape)],
                                                             name="gather_sgu_w_start")
    cw_full = cw_full + tok[0:1, 0:1]
    dw_in = _matmul(h1, dp, mode="tn", out_dtype=BF16, name="mm_dw_in_sgu", tm=1024, tn=1024, tk=2048,
                    b_cols=(2 * d, 2 * d), out_into=(dw_in_gates, 2 * d))
    dp, d_conv_w, d_conv_b, dw_in = _conv_bwd(dp, dc, p, cw_full, h1, dw_in, bl=bl, s=s, name="conv_bwd")
    grads["w_in"] = dw_in
    tok = send_grads(["w_in"], "in")
    grad_x, d_norm_mix = _proj_rms_bwd(dp, dx1, xf, w_in_blocks, norm_mix + tok, name="in_proj_bwd")
    out = {}

    vec_names = ["norm_mix", "conv_b", "conv_ln_g", "conv_ln_b", "sgu_ln_g", "sgu_ln_b", "norm_xattn", "norm_mem",
                 "norm_ffn", "norm_final"]
    vec_grads = [d_norm_mix, d_conv_b, d_conv_ln_g, d_conv_ln_b, d_sgu_ln_g, d_sgu_ln_b, d_norm_xattn, d_norm_mem,
                 d_norm_ffn, d_norm_final]
    n_vec = len(vec_names)
    small_vec = jnp.concatenate([g.reshape(1, d) for g in vec_grads]
                                + [jnp.broadcast_to(loss_part, (1, d)), jnp.zeros((16 - n_vec - 1, d), F32)], axis=0)
    small_cols = jnp.concatenate([d_b_gate, d_conv_w], axis=0)
    parts_vec, parts_sb, parts_cols = _all_gather([small_vec, d_sgu_b, small_cols], name="gather_small_grads")
    _, sgw_land = _copy_wait("gather", sgw_ssem, sgw_rsem, sgw_src, sgw_land, parts_vec, name="gather_sgu_w_wait")
    parts_sw = lax.dynamic_update_index_in_dim(sgw_land[0], d_sgu_w, dev, 0)
    rep_names = vec_names + ["sgu_b", "sgu_w"]
    rep_shapes = [(1, d)] * n_vec + [d_sgu_b.shape, d_sgu_w.shape]
    states = [tuple(given[pre + n].reshape(shape) for pre in ("", "m_", "v_")) for n, shape in zip(rep_names, rep_shapes)]
    res_rep = _adamw_replicated([parts_vec, parts_sb, parts_sw], states, n_vec, name="adamw_small")
    for i, n in enumerate(rep_names):
        out[n] = [r.reshape(given[n].shape) for r in res_rep[1 + 4 * i:5 + 4 * i]]
    res_cols = _adamw_column_shards(parts_cols, dev_id, [(b_gate[0], m_b_gate[0], v_b_gate[0]),
                                                        (conv_w[0], m_conv_w[0], v_conv_w[0])], (0, 8),
                                    name="adamw_small_cols")
    out["b_gate"] = [r[None] for r in res_cols[0:4]]
    out["conv_w"] = [r[None] for r in res_cols[4:8]]

    done = res_rep[1]
    for names, ssem, rsem, srcs, lands in sent:
        srcs, lands = _copy_wait("scatter", ssem, rsem, srcs, lands, done, name=f"grads_{names[0]}_wait")
        for n, partials, landed in zip(names, srcs, lands):
            res = _adamw_shard(partials, landed, dev_id, shard_of(n), shard_of(n, "m_"), shard_of(n, "v_"),
                               name=f"adamw_{n}")
            done = res[0]
            out[n] = [(jnp.transpose(r) if n in transposed else r)[None] for r in res]

    order = ["norm_mix", "w_in", "b_gate", "conv_w", "conv_b", "conv_ln_g", "conv_ln_b", "w_conv_out", "sgu_ln_g",
             "sgu_ln_b", "sgu_w", "sgu_b", "w_sgu_out", "w_mix_out", "norm_xattn", "norm_mem", "w_q", "w_kv", "w_xo",
             "norm_ffn", "w_gu", "w_down", "norm_final"]
    loss = res_rep[0][0, 0]
    return (loss, grad_x.reshape(x.shape), *[out[n][0] for n in order], *[out[n][1] for n in order],
            *[out[n][2] for n in order], *[out[n][3] for n in order])
```

```python
import jax
import jax.numpy as jnp
from jax import lax
from jax.experimental import pallas as pl
from jax.experimental.pallas import tpu as pltpu

F32 = jnp.float32
BF16 = jnp.bfloat16
RMS_EPS = 1e-6
LN_EPS = 1e-5
CONV_WIDTH = 31
CONV_HALO = 32
CONV_ROWS = 128
CONV_COLS = 256
LANES = 128
SGU_CHUNK = 128
SGU_GROUPS = 8
SGU_TILE = 512
HEADS = 4
N_DEV = 8
ADAM_LR, ADAM_B1, ADAM_B2, ADAM_EPS, ADAM_WD, ADAM_STEP = 0.001, 0.9, 0.999, 1e-08, 0.01, 10
VMEM_LIMIT = 56 * 1024 * 1024
TOKEN_TILE = 256
ATTN_TILE = 1024
MESH_ID = pl.DeviceIdType.MESH

_GELU_K = 0.7978845608028654
_GELU_C = 0.044715


def _cparams(sem=None):
    return pltpu.CompilerParams(dimension_semantics=sem, vmem_limit_bytes=VMEM_LIMIT)


def _sigmoid(v):
    return 0.5 * jnp.tanh(0.5 * v) + 0.5


def _gelu(v):
    return 0.5 * v * (1.0 + jnp.tanh(_GELU_K * (v + _GELU_C * v * v * v)))


def _gelu_grad(v):
    th = jnp.tanh(_GELU_K * (v + _GELU_C * v * v * v))
    return 0.5 * (1.0 + th) + 0.5 * v * (1.0 - th * th) * _GELU_K * (1.0 + 3.0 * _GELU_C * v * v)


def _dot(a, b, dims):
    return lax.dot_general(a, b, (dims, ((), ())), preferred_element_type=F32)


_NN = ((1,), (0,))
_NT = ((1,), (1,))
_TN = ((0,), (0,))


def _matmul(a, b, *, mode, out_dtype, name, tm=512, tn=512, tk=512, col_blocks=None, b_cols=None, out_into=None):
    if mode == "nn":
        (m, k), (_, n) = a.shape, b.shape
    elif mode == "nt":
        (m, k), (n, _) = a.shape, b.shape
    else:
        (k, m), (_, n) = a.shape, b.shape
    b_first = 0
    if b_cols is not None:
        assert mode == "tn"
        b_first, n = b_cols
    tm, tn, tk = min(tm, m), min(tn, n), min(tk, k)
    assert b_first % tn == 0
    b_first //= tn
    assert m % tm == 0 and n % tn == 0 and k % tk == 0, (name, a.shape, b.shape, tm, tn, tk)
    nk = k // tk
    dims = {"nn": _NN, "nt": _NT, "tn": _TN}[mode]

    def body(*refs):
        a_ref, b_ref = refs[:2]
        o_ref = refs[3] if out_into is not None else refs[2]
        part = _dot(a_ref[...].astype(BF16), b_ref[...].astype(BF16), dims)
        if nk == 1:
            o_ref[...] = part.astype(out_dtype)
        else:
            acc_ref = refs[-1]
            kk = pl.program_id(2)

            @pl.when(kk == 0)
            def _():
                acc_ref[...] = part

            @pl.when(kk > 0)
            def _():
                acc_ref[...] += part

            @pl.when(kk == nk - 1)
            def _():
                o_ref[...] = acc_ref[...].astype(out_dtype)

    resident = dict(pipeline_mode=pl.Buffered(1)) if (n == tn and nk == 1 and mode != "tn" and m > tm) else {}
    if mode == "nn":
        a_spec = pl.BlockSpec((tm, tk), lambda i, j, kk: (i, kk))
        b_spec = pl.BlockSpec((tk, tn), lambda i, j, kk: (kk, j), **resident)
    elif mode == "nt":
        a_spec = pl.BlockSpec((tm, tk), lambda i, j, kk: (i, kk))
        b_spec = pl.BlockSpec((tn, tk), lambda i, j, kk: (j, kk), **resident)
    else:
        a_spec = pl.BlockSpec((tk, tm), lambda i, j, kk: (kk, i))
        b_spec = pl.BlockSpec((tk, tn), lambda i, j, kk: (kk, j + b_first))
    in_specs, args = [a_spec, b_spec], [a, b]
    out_shape = [jax.ShapeDtypeStruct((m, n), out_dtype)]
    out_specs = [pl.BlockSpec((tm, tn), lambda i, j, kk: (i, j))]
    if col_blocks is not None:
        assert (n // col_blocks) % tn == 0
        per = n // col_blocks // tn
        out_shape = [jax.ShapeDtypeStruct((col_blocks, m, n // col_blocks), out_dtype)]
        out_specs = [pl.BlockSpec((None, tm, tn), lambda i, j, kk: (j // per, i, j % per))]
    aliases = {}
    if out_into is not None:
        target, first = out_into
        assert col_blocks is None and first % tn == 0 and target.dtype == out_dtype
        in_specs.append(pl.BlockSpec(memory_space=pl.ANY))
        args.append(target)
        aliases = {len(args) - 1: 0}
        out_shape = [jax.ShapeDtypeStruct(target.shape, target.dtype)]
        out_specs = [pl.BlockSpec((tm, tn), lambda i, j, kk: (i, j + first // tn))]
    res = pl.pallas_call(
        body, name=name, grid=(m // tm, n // tn, nk), in_specs=in_specs, out_specs=out_specs, out_shape=out_shape,
        scratch_shapes=[pltpu.VMEM((tm, tn), F32)] if nk > 1 else [], input_output_aliases=aliases,
        compiler_params=_cparams(("parallel", "parallel", "arbitrary")),
    )(*args)
    return res[0]


def _row_call(name, t, tm, rows_in, residents, rows_out, accs, body):
    n_in, n_res, n_out, n_acc = len(rows_in), len(residents), len(rows_out), len(accs)
    steps = t // tm
    assert t % tm == 0
    narrow = [i for i, a in enumerate(accs) if a[1] != F32]

    def kernel_body(*refs):
        in_refs, res_refs = refs[:n_in], refs[n_in:n_in + n_res]
        out_refs = refs[n_in + n_res:n_in + n_res + n_out]
        acc_out = list(refs[n_in + n_res + n_out:n_in + n_res + n_out + n_acc])
        scratch = refs[n_in + n_res + n_out + n_acc:]
        acc_refs = list(acc_out)
        for s_ref, i in zip(scratch, narrow):
            acc_refs[i] = s_ref
        if accs:
            @pl.when(pl.program_id(0) == 0)
            def _():
                for acc in acc_refs:
                    acc[...] = jnp.zeros_like(acc)
        body(in_refs, res_refs, out_refs, acc_refs)
        if narrow:
            @pl.when(pl.program_id(0) == steps - 1)
            def _():
                for i in narrow:
                    acc_out[i][...] = acc_refs[i][...].astype(acc_out[i].dtype)

    once = dict(pipeline_mode=pl.Buffered(1)) if steps > 1 else {}
    in_specs = [pl.BlockSpec((tm, cols), lambda i, cb=cb: (i, cb)) for _, cols, cb in rows_in]
    in_specs += [pl.BlockSpec(r.shape, lambda i, nd=r.ndim: (0,) * nd, **once) for r in residents]
    out_specs = [pl.BlockSpec((tm, cols), lambda i, cb=cb: (i, cb)) for _, cols, cb, _ in rows_out]
    out_specs += [pl.BlockSpec(a[0], lambda i, nd=len(a[0]), cb=(a[3] if len(a) == 4 else 0): (0,) * (nd - 1) + (cb,))
                  for a in accs]
    out_shape = [jax.ShapeDtypeStruct((t, total), dt) for total, _, _, dt in rows_out]
    out_shape += [jax.ShapeDtypeStruct((a[0][0], a[2]) if len(a) == 4 else a[0], a[1]) for a in accs]
    return pl.pallas_call(
        kernel_body, name=name, grid=(steps,), in_specs=in_specs, out_specs=out_specs, out_shape=out_shape,
        scratch_shapes=[pltpu.VMEM(accs[i][0], F32) for i in narrow],
        compiler_params=_cparams(("arbitrary",) if accs else ("parallel",)),
    )(*[a for a, _, _ in rows_in], *residents)


def _rms_apply(xv, gain):
    return xv * lax.rsqrt(jnp.mean(xv * xv, axis=-1, keepdims=True) + RMS_EPS) * gain


def _rms_grad(dres, dh, xv, gain):
    r = lax.rsqrt(jnp.mean(xv * xv, axis=-1, keepdims=True) + RMS_EPS)
    xhat = xv * r
    dxh = dh * gain
    dx = dres + r * (dxh - xhat * jnp.mean(dxh * xhat, axis=-1, keepdims=True))
    return dx, jnp.sum(dh * xhat, axis=0, keepdims=True)


def _in_proj_gather(xf, gain, w_shard, *, name):
    t, d = xf.shape
    cb = w_shard.shape[1]
    tm = min(1024, t)
    steps = t // tm
    mx, my, _ = _mesh_pos()
    order = jnp.stack([2 * mx + my, 2 * (1 - mx) + my, 2 * mx + (1 - my), 2 * (1 - mx) + (1 - my)]).astype(jnp.int32)

    def body(order_ref, x_ref, g_ref, ws_ref, h_ref, p_ref, wout_ref, w_ref, send_sems, recv_sems, own_sem):
        ps, i = pl.program_id(0), pl.program_id(1)
        x, y, c = _mesh_pos()
        me, sib = (x, y, c), (x, y, 1 - c)
        chips = [(1 - x, y), (x, 1 - y), (1 - x, 1 - y)]

        def copy(k, block, to, from_shard=False):
            return pltpu.make_async_remote_copy(
                src_ref=ws_ref if from_shard else w_ref.at[_dev_index(block)], dst_ref=w_ref.at[_dev_index(block)],
                send_sem=send_sems.at[k], recv_sem=recv_sems.at[k], device_id=to, device_id_type=MESH_ID)

        own = pltpu.make_async_copy(ws_ref, w_ref.at[_dev_index(me)], own_sem)
        first = [copy(0, me, sib, True)] + [copy(1 + j, me, (*chip, c), True) for j, chip in enumerate(chips)]
        passed = [copy(4 + j, (*chip, c), sib) for j, chip in enumerate(chips)]

        @pl.when(jnp.logical_and(ps == 0, i == 0))
        def _():
            own.start()
            for cp in first:
                cp.start()
            own.wait()
            copy(0, sib, me).wait_recv()

        for j, chip in enumerate(chips):
            @pl.when(jnp.logical_and(ps == j + 1, i == 0))
            def _(j=j, chip=chip):
                copy(1 + j, (*chip, c), me).wait_recv()
                passed[j].start()
                copy(4 + j, (*chip, 1 - c), me).wait_recv()

        h = _rms_apply(x_ref[...], g_ref[...]).astype(BF16)
        h_ref[...] = h
        chip_id = order_ref[ps]
        p_ref[:, 0:cb] = _dot(h, w_ref[2 * chip_id], _NN).astype(BF16)
        p_ref[:, cb:2 * cb] = _dot(h, w_ref[2 * chip_id + 1], _NN).astype(BF16)

        @pl.when(jnp.logical_and(ps == 3, i == steps - 1))
        def _():
            for cp in first + passed:
                cp.wait_send()
            keep = pltpu.make_async_copy(w_ref, wout_ref, own_sem)
            keep.start()
            keep.wait()

    gs = pltpu.PrefetchScalarGridSpec(
        num_scalar_prefetch=1, grid=(4, steps),
        in_specs=[pl.BlockSpec((tm, d), lambda ps, i, o: (i, 0)), pl.BlockSpec((1, d), lambda ps, i, o: (0, 0)),
                  pl.BlockSpec(memory_space=pl.ANY)],
        out_specs=[pl.BlockSpec((tm, d), lambda ps, i, o: (jnp.where(ps == 0, i, steps - 1), 0)),
                   pl.BlockSpec((tm, 2 * cb), lambda ps, i, o: (i, o[ps])), pl.BlockSpec(memory_space=pl.ANY)],
        scratch_shapes=[pltpu.VMEM((N_DEV, d, cb), BF16), pltpu.SemaphoreType.DMA((7,)), pltpu.SemaphoreType.DMA((7,)),
                        pltpu.SemaphoreType.DMA(())])
    return pl.pallas_call(
        body, name=name, grid_spec=gs,
        out_shape=[jax.ShapeDtypeStruct((t, d), BF16), jax.ShapeDtypeStruct((t, N_DEV * cb), BF16),
                   jax.ShapeDtypeStruct((N_DEV, d, cb), BF16)],
        compiler_params=_cparams(("arbitrary", "arbitrary")))(order, xf, gain, w_shard)


def _mix_out(p, a_act, sg, w_conv_out, w_sgu_out, b_gate, xf, w_mix, gain, w_q, *, name):
    t, d = xf.shape

    def body(ins, res, outs, accs):
        ga_ref, gb_ref, act_ref, sg_ref, x_ref = ins
        bg_ref, wm_ref, g_ref, wq_ref, wa_ref, wb_ref = res
        ya_ref, yb_ref, m_ref, x1_ref, h_ref, q_ref = outs
        y_a = _dot(act_ref[...], wa_ref[...], _NN).astype(BF16)
        y_b = _dot(sg_ref[...], wb_ref[...], _NN).astype(BF16)
        ya_ref[...] = y_a
        yb_ref[...] = y_b
        sa = _sigmoid(ga_ref[...].astype(F32) + bg_ref[0:1, :])
        sb = _sigmoid(gb_ref[...].astype(F32) + bg_ref[1:2, :])
        merged = (sa * y_a.astype(F32) + sb * y_b.astype(F32)).astype(BF16)
        m_ref[...] = merged
        x1 = x_ref[...] + _dot(merged, wm_ref[...], _NN)
        x1_ref[...] = x1
        h = _rms_apply(x1, g_ref[...]).astype(BF16)
        h_ref[...] = h
        q_ref[...] = _dot(h, wq_ref[...], _NN).astype(BF16)

    bf = (d, d, 0, BF16)
    return _row_call(name, t, min(512, t), [(p, d, 4), (p, d, 5), (a_act, d, 0), (sg, d, 0), (xf, d, 0)],
                     [b_gate, w_mix, gain, w_q, w_conv_out, w_sgu_out], [bf, bf, bf, (d, d, 0, F32), bf, bf], [], body)


def _ffn_fwd(h3, x2, target, w_gu_t, w_down, gain, *, name):
    t, d = x2.shape
    f2 = w_gu_t.shape[0]
    f = f2 // 2
    half = f // 2

    def body(ins, res, outs, accs):
        h_ref, x2_ref, t_ref = ins
        wgu_ref, wd_ref, g_ref = res
        gu_ref, act_ref, dx_ref = outs
        loss_ref, dg_ref = accs
        h = h_ref[...]
        x3 = x2_ref[...]
        for c0 in (0, half):
            gt = _dot(h, wgu_ref[c0:c0 + half, :], _NT).astype(BF16)
            up = _dot(h, wgu_ref[f + c0:f + c0 + half, :], _NT).astype(BF16)
            gu_ref[:, c0:c0 + half] = gt
            gu_ref[:, f + c0:f + c0 + half] = up
            gtf = gt.astype(F32)
            act = (gtf * _sigmoid(gtf) * up.astype(F32)).astype(BF16)
            act_ref[:, c0:c0 + half] = act
            x3 = x3 + _dot(act, wd_ref[c0:c0 + half, :], _NN)
        g = g_ref[...]
        r = lax.rsqrt(jnp.mean(x3 * x3, axis=-1, keepdims=True) + RMS_EPS)
        xhat = x3 * r
        err = xhat * g - t_ref[...]
        loss_ref[...] += 0.5 * jnp.sum(jnp.mean(err * err, axis=-1, keepdims=True), axis=0, keepdims=True)
        dy = err * (1.0 / d)
        dg_ref[...] += jnp.sum(dy * xhat, axis=0, keepdims=True)
        dxh = dy * g
        dx_ref[...] = r * (dxh - xhat * jnp.mean(dxh * xhat, axis=-1, keepdims=True))

    return _row_call(name, t, min(256, t), [(h3, d, 0), (x2, d, 0), (target, d, 0)], [w_gu_t, w_down, gain],
                     [(f2, f2, 0, BF16), (f, f, 0, BF16), (d, d, 0, F32)], [((1, 1), F32), ((1, d), F32)], body)


def _ffn_bwd(dx3, gu, x2, w_down, w_gu_t, gain, w_xo, *, name):
    t, d = x2.shape
    f2 = w_gu_t.shape[0]
    f = f2 // 2
    half = f // 2

    def body(ins, res, outs, accs):
        dx3_ref, gu_ref, x2_ref = ins
        wd_ref, wgu_ref, g_ref, wxo_ref = res
        dgu_ref, dx2_ref, do_ref = outs
        (dg_ref,) = accs
        dx3v = dx3_ref[...]
        dxb = dx3v.astype(BF16)
        dh = jnp.zeros(dx3v.shape, F32)
        for c0 in (0, half):
            dact = _dot(dxb, wd_ref[c0:c0 + half, :], _NT)
            gt = gu_ref[:, c0:c0 + half].astype(F32)
            up = gu_ref[:, f + c0:f + c0 + half].astype(F32)
            sg = _sigmoid(gt)
            dgt = (dact * up * sg * (1.0 + gt * (1.0 - sg))).astype(BF16)
            dup = (dact * gt * sg).astype(BF16)
            dgu_ref[:, c0:c0 + half] = dgt
            dgu_ref[:, f + c0:f + c0 + half] = dup
            dh = dh + _dot(dgt, wgu_ref[c0:c0 + half, :], _NN) + _dot(dup, wgu_ref[f + c0:f + c0 + half, :], _NN)
        dx2, dg = _rms_grad(dx3v, dh, x2_ref[...], g_ref[...])
        dx2_ref[...] = dx2
        dg_ref[...] += dg
        do_ref[...] = _dot(dx2.astype(BF16), wxo_ref[...], _NT).astype(BF16)

    return _row_call(name, t, min(256, t), [(dx3, d, 0), (gu, f2, 0), (x2, d, 0)], [w_down, w_gu_t, gain, w_xo],
                     [(f2, f2, 0, BF16), (d, d, 0, F32), (d, d, 0, BF16)], [((1, d), F32)], body)


def _proj_rms_bwd(dy, dres, x, w, gain, *, name, h=None):
    t, d = x.shape
    k = dy.shape[1]

    def body(ins, res, outs, accs):
        dy_ref, dres_ref, x_ref = ins[:3]
        w_ref, g_ref = res
        if h is not None:
            accs[1][...] += _dot(ins[3][...], dy_ref[...], _TN)
        if w.ndim == 3:
            cb = w.shape[2]
            dh = _dot(dy_ref[:, 0:cb], w_ref[0], _NT)
            for j in range(1, w.shape[0]):
                dh = dh + _dot(dy_ref[:, j * cb:(j + 1) * cb], w_ref[j], _NT)
        else:
            dh = _dot(dy_ref[...], w_ref[...], _NT)
        dx, dg = _rms_grad(dres_ref[...], dh, x_ref[...], g_ref[...])
        outs[0][...] = dx
        accs[0][...] += dg

    rows_in = [(dy, k, 0), (dres, d, 0), (x, d, 0)] + ([(h, d, 0)] if h is not None else [])
    accs = [((1, d), F32)] + ([((d, k), BF16)] if h is not None else [])
    tm = 1024 if w.ndim == 2 else 512
    return _row_call(name, t, min(tm, t), rows_in, [w, gain], [(d, d, 0, F32)], accs, body)


def _gates_bwd_fused(dx1, p, y_a, y_b, b_gate, w_mix, merged, h1, *, name):
    t, d = y_a.shape

    def body(ins, res, outs, accs):
        dx_ref, ga_ref, gb_ref, ya_ref, yb_ref, m_ref, h1_ref = ins
        bg_ref, wm_ref = res
        dp_ref, dya_ref, dyb_ref = outs
        dbg_ref, dwm_ref, dwin_ref = accs
        dxb = dx_ref[...].astype(BF16)
        dwm_ref[...] += _dot(m_ref[...], dxb, _TN)
        dm = _dot(dxb, wm_ref[...], _NT)
        sa = _sigmoid(ga_ref[...].astype(F32) + bg_ref[0:1, :])
        sb = _sigmoid(gb_ref[...].astype(F32) + bg_ref[1:2, :])
        dya_ref[...] = (dm * sa).astype(BF16)
        dyb_ref[...] = (dm * sb).astype(BF16)
        dga = dm * ya_ref[...].astype(F32) * sa * (1.0 - sa)
        dgb = dm * yb_ref[...].astype(F32) * sb * (1.0 - sb)
        dp_ref[:, 0:d] = dga.astype(BF16)
        dp_ref[:, d:2 * d] = dgb.astype(BF16)
        dbg_ref[0:1, :] += jnp.sum(dga, axis=0, keepdims=True)
        dbg_ref[1:2, :] += jnp.sum(dgb, axis=0, keepdims=True)
        dwin_ref[...] += _dot(h1_ref[...], dp_ref[...], _TN)

    return _row_call(name, t, min(256, t),
                     [(dx1, d, 0), (p, d, 4), (p, d, 5), (y_a, d, 0), (y_b, d, 0), (merged, d, 0), (h1, d, 0)],
                     [b_gate, w_mix], [(p.shape[1], 2 * d, 2, BF16), (d, d, 0, BF16), (d, d, 0, BF16)],
                     [((8, d), F32), ((d, d), BF16), ((d, 2 * d), BF16, p.shape[1], 2)], body)


def _conv_ln_bwd_fused(dy_a, c, a_act, w_conv_out, ln_g, ln_b, *, name):
    t, d = c.shape

    def body(ins, res, outs, accs):
        dy_ref, c_ref, act_ref = ins
        w_ref, lg_ref, lb_ref = res
        dlg_ref, dlb_ref, dw_ref = accs
        dw_ref[...] += _dot(act_ref[...], dy_ref[...], _TN)
        dact = _dot(dy_ref[...], w_ref[...], _NT)
        cv = c_ref[...].astype(F32)
        g = lg_ref[...]
        mu = jnp.mean(cv, axis=-1, keepdims=True)
        dv = cv - mu
        rstd = lax.rsqrt(jnp.mean(dv * dv, axis=-1, keepdims=True) + LN_EPS)
        chat = dv * rstd
        aln = chat * g + lb_ref[...]
        sg = _sigmoid(aln)
        daln = dact * (sg * (1.0 + aln * (1.0 - sg)))
        dlb_ref[...] += jnp.sum(daln, axis=0, keepdims=True)
        dlg_ref[...] += jnp.sum(daln * chat, axis=0, keepdims=True)
        dchat = daln * g
        dc = rstd * (dchat - jnp.mean(dchat, axis=-1, keepdims=True)
                     - chat * jnp.mean(dchat * chat, axis=-1, keepdims=True))
        outs[0][...] = dc.astype(BF16)

    return _row_call(name, t, min(1024, t), [(dy_a, d, 0), (c, d, 0), (a_act, d, 0)], [w_conv_out, ln_g, ln_b],
                     [(d, d, 0, BF16)], [((1, d), F32), ((1, d), F32), ((d, d), BF16)], body)


def _row_spec(tt, cols, col_block=0):
    return pl.BlockSpec((tt, cols), lambda i: (i, col_block))


def _const_spec(shape):
    return pl.BlockSpec(shape, lambda *_: (0,) * len(shape))


def _rms_fwd(x, gain, *, name):
    t, d = x.shape
    tt = min(TOKEN_TILE, t)

    def body(x_ref, g_ref, h_ref):
        xv = x_ref[...]
        r = lax.rsqrt(jnp.mean(xv * xv, axis=-1, keepdims=True) + RMS_EPS)
        h_ref[...] = (xv * r * g_ref[...]).astype(BF16)

    return pl.pallas_call(
        body, name=name, grid=(t // tt,), in_specs=[_row_spec(tt, d), _const_spec((1, d))],
        out_specs=_row_spec(tt, d), out_shape=jax.ShapeDtypeStruct((t, d), BF16),
        compiler_params=_cparams(("parallel",)))(x, gain)


def _rms_gain_grad(dh, x, *, name):
    t, d = x.shape
    tt = min(TOKEN_TILE, t)

    def body(dh_ref, x_ref, dg_ref):
        @pl.when(pl.program_id(0) == 0)
        def _():
            dg_ref[...] = jnp.zeros_like(dg_ref)

        xv = x_ref[...]
        xhat = xv * lax.rsqrt(jnp.mean(xv * xv, axis=-1, keepdims=True) + RMS_EPS)
        dg_ref[...] += jnp.sum(dh_ref[...].astype(F32) * xhat, axis=0, keepdims=True)

    rs = _row_spec(tt, d)
    return pl.pallas_call(
        body, name=name, grid=(t // tt,), in_specs=[rs, rs], out_specs=_const_spec((1, d)),
        out_shape=jax.ShapeDtypeStruct((1, d), F32), compiler_params=_cparams(("arbitrary",)))(dh, x)


SUBLANES = 8
SHIFT_ROWS = 40


def _conv_apply(sbuf_ref, w_ref, out_ref, tt, offsets, bias_ref=None):
    d = out_ref.shape[1]
    for cc in range(d // LANES):
        cs = slice(cc * LANES, (cc + 1) * LANES)
        taps = [jnp.broadcast_to(w_ref[k:k + 1, cs], (SUBLANES, LANES)) for k in range(CONV_WIDTH)]
        bias = None if bias_ref is None else jnp.broadcast_to(bias_ref[:, cs], (SUBLANES, LANES))

        def row_body(r, carry, cs=cs, taps=taps, bias=bias):
            r0 = pl.multiple_of(r * CONV_ROWS, CONV_ROWS)
            for q in range(CONV_ROWS // SUBLANES):
                acc = _tap(sbuf_ref, r0 + q * SUBLANES, cs, offsets[0]) * taps[0]
                for k in range(1, CONV_WIDTH):
                    acc = acc + _tap(sbuf_ref, r0 + q * SUBLANES, cs, offsets[k]) * taps[k]
                if bias is not None:
                    acc = acc + bias
                out_ref[pl.ds(r0 + q * SUBLANES, SUBLANES), cs] = acc
            return carry

        lax.fori_loop(0, tt // CONV_ROWS, row_body, 0)


def _fill_shifts(sbuf_ref, rows):
    d = sbuf_ref.shape[2]
    assert rows % SHIFT_ROWS == 0

    def row_body(i, carry):
        r0 = pl.multiple_of(i * SHIFT_ROWS, SUBLANES)
        for cc in range(d // CONV_COLS):
            cs = slice(cc * CONV_COLS, (cc + 1) * CONV_COLS)
            win = sbuf_ref[0, pl.ds(r0, SHIFT_ROWS + SUBLANES), cs]
            for sh in range(1, SUBLANES):
                sbuf_ref[sh, pl.ds(r0, SHIFT_ROWS), cs] = win[sh:sh + SHIFT_ROWS, :]
        return carry

    lax.fori_loop(0, rows // SHIFT_ROWS, row_body, 0)


def _tap(sbuf_ref, r0, cs, offset):
    sh = offset % SUBLANES
    return sbuf_ref[sh, pl.ds(pl.multiple_of(r0 + (offset - sh), SUBLANES), SUBLANES), cs]


def _conv_specs(bl, s, tt, d, col_a, col_g):
    nj = s // tt
    per = tt // CONV_HALO
    main_a = pl.BlockSpec((tt, d), lambda b, j: (b * nj + j, col_a))
    main_g = pl.BlockSpec((tt, d), lambda b, j: (b * nj + j, col_g))
    prev = lambda b, j: jnp.maximum((b * nj + j) * per - 1, 0)
    halo_a = pl.BlockSpec((CONV_HALO, d), lambda b, j: (prev(b, j), col_a))
    halo_g = pl.BlockSpec((CONV_HALO, d), lambda b, j: (prev(b, j), col_g))
    return main_a, main_g, halo_a, halo_g


def _fill_glu(sbuf_ref, a_ref, g_ref, ha_ref, hg_ref, tt):
    first = pl.program_id(1) == 0
    ha = ha_ref[...].astype(F32)
    hg = hg_ref[...].astype(F32)
    sbuf_ref[0, pl.ds(0, CONV_HALO), :] = jnp.where(first, 0.0, ha * _sigmoid(hg))
    av = a_ref[...].astype(F32)
    gv = g_ref[...].astype(F32)
    sbuf_ref[0, pl.ds(CONV_HALO, tt), :] = av * _sigmoid(gv)
    _fill_shifts(sbuf_ref, tt + CONV_HALO - SUBLANES)


def _conv_fwd(p, conv_w, conv_b, ln_g, ln_b, *, bl, s, name):
    t = p.shape[0]
    d = conv_w.shape[1]
    tt = min(TOKEN_TILE, s)
    off = CONV_HALO - (CONV_WIDTH - 1)

    def body(a_ref, g_ref, ha_ref, hg_ref, w_ref, b_ref, lg_ref, lb_ref, c_ref, act_ref, sbuf_ref, cbuf_ref):
        _fill_glu(sbuf_ref, a_ref, g_ref, ha_ref, hg_ref, tt)

        _conv_apply(sbuf_ref, w_ref, cbuf_ref, tt, [off + k for k in range(CONV_WIDTH)], bias_ref=b_ref)
        cv = cbuf_ref[...]
        c_ref[...] = cv.astype(BF16)
        mu = jnp.mean(cv, axis=-1, keepdims=True)
        dv = cv - mu
        rstd = lax.rsqrt(jnp.mean(dv * dv, axis=-1, keepdims=True) + LN_EPS)
        aln = dv * rstd * lg_ref[...] + lb_ref[...]
        act_ref[...] = (aln * _sigmoid(aln)).astype(BF16)

    main_a, main_g, halo_a, halo_g = _conv_specs(bl, s, tt, d, 0, 1)
    out_spec = pl.BlockSpec((tt, d), lambda b, j: (b * (s // tt) + j, 0))
    return pl.pallas_call(
        body, name=name, grid=(bl, s // tt),
        in_specs=[main_a, main_g, halo_a, halo_g, _const_spec((CONV_HALO, d)), _const_spec((1, d)), _const_spec((1, d)),
                  _const_spec((1, d))],
        out_specs=[out_spec, out_spec],
        out_shape=[jax.ShapeDtypeStruct((t, d), BF16), jax.ShapeDtypeStruct((t, d), BF16)],
        scratch_shapes=[pltpu.VMEM((SUBLANES, tt + CONV_HALO, d), F32), pltpu.VMEM((tt, d), F32)],
        compiler_params=_cparams(("parallel", "parallel")))(p, p, p, p, conv_w, conv_b, ln_g, ln_b)


def _conv_bwd(dp, dc, p, conv_w, h1, dw_in, *, bl, s, name):
    t = p.shape[0]
    d = conv_w.shape[1]
    tt = min(TOKEN_TILE, s)
    nj = s // tt
    per = tt // CONV_HALO
    off = CONV_HALO - (CONV_WIDTH - 1)
    last_blk = t // CONV_HALO - 1

    def body(dp_in, dc_ref, dcn_ref, a_ref, g_ref, ha_ref, hg_ref, w_ref, h1_ref, dwin_in, dp_ref, dw_ref, db_ref,
             dwin_out, gbuf_ref, dbuf_ref, dglu_ref, acc_ref, dwin_ref):
        del dp_in, dwin_in
        b, j = pl.program_id(0), pl.program_id(1)
        start = jnp.logical_and(b == 0, j == 0)
        end = jnp.logical_and(b == bl - 1, j == nj - 1)

        @pl.when(start)
        def _():
            acc_ref[...] = jnp.zeros_like(acc_ref)
            db_ref[...] = jnp.zeros_like(db_ref)
            dwin_ref[...] = jnp.zeros_like(dwin_ref)

        _fill_glu(gbuf_ref, a_ref, g_ref, ha_ref, hg_ref, tt)
        dcv = dc_ref[...].astype(F32)
        dbuf_ref[0, pl.ds(0, tt), :] = dcv
        dbuf_ref[0, pl.ds(tt, CONV_HALO), :] = jnp.where(j == nj - 1, 0.0, dcn_ref[...].astype(F32))
        _fill_shifts(dbuf_ref, tt + CONV_HALO - SUBLANES)
        db_ref[...] += jnp.sum(dcv, axis=0, keepdims=True)

        for cc in range(d // LANES):
            cs = slice(cc * LANES, (cc + 1) * LANES)

            def row_body(r, accs, cs=cs):
                r0 = pl.multiple_of(r * CONV_ROWS, CONV_ROWS)
                accs = list(accs)
                for q in range(CONV_ROWS // SUBLANES):
                    dcw = dbuf_ref[0, pl.ds(r0 + q * SUBLANES, SUBLANES), cs]
                    for k in range(CONV_WIDTH):
                        accs[k] = accs[k] + dcw * _tap(gbuf_ref, r0 + q * SUBLANES, cs, off + k)
                return tuple(accs)

            zero = jnp.zeros((SUBLANES, LANES), F32)
            accs = lax.fori_loop(0, tt // CONV_ROWS, row_body, (zero,) * CONV_WIDTH)
            for k in range(CONV_WIDTH):
                acc_ref[k, :, cs] += accs[k]

        _conv_apply(dbuf_ref, w_ref, dglu_ref, tt, [CONV_WIDTH - 1 - k for k in range(CONV_WIDTH)])
        dglu = dglu_ref[...]
        av = a_ref[...].astype(F32)
        sg = _sigmoid(g_ref[...].astype(F32))
        dp_ref[:, 0:d] = (dglu * sg).astype(BF16)
        dp_ref[:, d:2 * d] = (dglu * av * sg * (1.0 - sg)).astype(BF16)
        dwin_ref[...] += _dot(h1_ref[...], dp_ref[...], _TN)

        @pl.when(end)
        def _():
            for k in range(CONV_WIDTH):
                dw_ref[k:k + 1, :] = jnp.sum(acc_ref[k], axis=0, keepdims=True)
            dw_ref[CONV_WIDTH:CONV_HALO, :] = jnp.zeros((CONV_HALO - CONV_WIDTH, d), F32)
            dwin_out[...] = dwin_ref[...].astype(dwin_out.dtype)

    main_a, main_g, halo_a, halo_g = _conv_specs(bl, s, tt, d, 0, 1)
    dc_main = pl.BlockSpec((tt, d), lambda b, j: (b * nj + j, 0))
    dc_next = pl.BlockSpec((CONV_HALO, d), lambda b, j: (jnp.minimum((b * nj + j + 1) * per, last_blk), 0))
    hbm = pl.BlockSpec(memory_space=pl.ANY)
    return pl.pallas_call(
        body, name=name, grid=(bl, nj),
        in_specs=[hbm, dc_main, dc_next, main_a, main_g, halo_a, halo_g, _const_spec((CONV_HALO, d)), dc_main, hbm],
        out_specs=[pl.BlockSpec((tt, 2 * d), lambda b, j: (b * nj + j, 0)), _const_spec((CONV_HALO, d)), _const_spec((1, d)),
                   _const_spec((d, 2 * d))],
        out_shape=[jax.ShapeDtypeStruct(dp.shape, dp.dtype), jax.ShapeDtypeStruct((CONV_HALO, d), F32),
                   jax.ShapeDtypeStruct((1, d), F32), jax.ShapeDtypeStruct(dw_in.shape, dw_in.dtype)],
        scratch_shapes=[pltpu.VMEM((SUBLANES, tt + CONV_HALO, d), F32), pltpu.VMEM((SUBLANES, tt + CONV_HALO, d), F32),
                        pltpu.VMEM((tt, d), F32), pltpu.VMEM((CONV_HALO, SUBLANES, d), F32), pltpu.VMEM((d, 2 * d), F32)],
        input_output_aliases={0: 0, 9: 3},
        compiler_params=_cparams(("arbitrary", "arbitrary")))(dp, dc, dc, p, p, p, p, conv_w, h1, dw_in)


def _sgu_stats(bv):
    gv = _gelu(bv)
    mu = jnp.mean(gv, axis=-1, keepdims=True)
    dv = gv - mu
    rstd = lax.rsqrt(jnp.mean(dv * dv, axis=-1, keepdims=True) + LN_EPS)
    return dv * rstd, rstd


def _sgu_fwd(p, wm, bias, ln_g, ln_b, *, name):
    t = p.shape[0]
    d = ln_g.shape[1]
    tt = SGU_TILE
    gd = d // SGU_GROUPS

    def body(u_ref, v_ref, wm_ref, bias_ref, lg_ref, lb_ref, sg_ref, vn_ref):
        u = _gelu(u_ref[...].astype(F32))
        vhat, _ = _sgu_stats(v_ref[...].astype(F32))
        vb = (vhat * lg_ref[...] + lb_ref[...]).astype(BF16)
        vn_ref[...] = vb
        for ci in range(tt // SGU_CHUNK):
            rows = slice(ci * SGU_CHUNK, (ci + 1) * SGU_CHUNK)
            for g in range(SGU_GROUPS):
                gs = slice(g * gd, (g + 1) * gd)
                z = _dot(wm_ref[g], vb[rows, gs], _NN) + bias_ref[g]
                sg_ref[rows, gs] = (u[rows, gs] * z).astype(BF16)

    rs = _row_spec(tt, d)
    return pl.pallas_call(
        body, name=name, grid=(t // tt,),
        in_specs=[_row_spec(tt, d, 2), _row_spec(tt, d, 3), _const_spec(wm.shape), _const_spec(bias.shape),
                  _const_spec((1, d)), _const_spec((1, d))],
        out_specs=[rs, rs], out_shape=[jax.ShapeDtypeStruct((t, d), BF16), jax.ShapeDtypeStruct((t, d), BF16)],
        compiler_params=_cparams(("parallel",)))(p, p, wm, bias, ln_g, ln_b)


def _sgu_bwd(dp, dy_b, w_out, p, vn, wm, wmt, bias, ln_g, *, name):
    t = p.shape[0]
    d = ln_g.shape[1]
    tt = SGU_TILE
    ck = SGU_CHUNK
    gd = d // SGU_GROUPS
    nsteps = t // tt

    def body(dp_in, dyb_ref, wout_ref, u_ref, v_ref, vn_ref, wm_ref, wmt_ref, bias_ref, lg_ref,
             dp_ref, dw_ref, dbs_ref, dlg_ref, dlb_ref, dz_acc):
        del dp_in
        i = pl.program_id(0)

        @pl.when(i == 0)
        def _():
            dw_ref[...] = jnp.zeros_like(dw_ref)
            dlg_ref[...] = jnp.zeros_like(dlg_ref)
            dlb_ref[...] = jnp.zeros_like(dlb_ref)
            dz_acc[...] = jnp.zeros_like(dz_acc)

        bu = u_ref[...].astype(F32)
        bv = v_ref[...].astype(F32)
        u = _gelu(bu)
        vhat, rstd = _sgu_stats(bv)
        vb = vn_ref[...]
        dsg = _dot(dyb_ref[...], wout_ref[...], _NT)
        row = lax.broadcasted_iota(jnp.int32, (ck, ck), 0)
        col = lax.broadcasted_iota(jnp.int32, (ck, ck), 1)
        causal = col <= row
        du_rows, dv_rows = [], []
        for ci in range(tt // ck):
            rows = slice(ci * ck, (ci + 1) * ck)
            du_parts, dv_parts = [], []
            for g in range(SGU_GROUPS):
                gs = slice(g * gd, (g + 1) * gd)
                z = _dot(wm_ref[g], vb[rows, gs], _NN) + bias_ref[g]
                du_parts.append(dsg[rows, gs] * z)
                dz = dsg[rows, gs] * u[rows, gs]
                dz_acc[:, gs] += dz
                dzb = dz.astype(BF16)
                dw_ref[g] += jnp.where(causal, _dot(dzb, vb[rows, gs], _NT), 0.0)
                dv_parts.append(_dot(wmt_ref[g], dzb, _NN))
            du_rows.append(jnp.concatenate(du_parts, axis=1))
            dv_rows.append(jnp.concatenate(dv_parts, axis=1))
        du = jnp.concatenate(du_rows, axis=0)
        dv = jnp.concatenate(dv_rows, axis=0)
        dp_ref[:, 0:d] = (du * _gelu_grad(bu)).astype(BF16)
        dlb_ref[...] += jnp.sum(dv, axis=0, keepdims=True)
        dlg_ref[...] += jnp.sum(dv * vhat, axis=0, keepdims=True)
        dvh = dv * lg_ref[...]
        dgv = rstd * (dvh - jnp.mean(dvh, axis=-1, keepdims=True) - vhat * jnp.mean(dvh * vhat, axis=-1, keepdims=True))
        dp_ref[:, d:2 * d] = (dgv * _gelu_grad(bv)).astype(BF16)

        @pl.when(i == nsteps - 1)
        def _():
            ones = jnp.ones((8, gd), F32)
            for g in range(SGU_GROUPS):
                gs = slice(g * gd, (g + 1) * gd)
                tot = lax.dot_general(ones, dz_acc[:, gs], (_NT, ((), ())), preferred_element_type=F32,
                                      precision=lax.Precision.HIGHEST)
                dbs_ref[g:g + 1, :] = tot[0:1, :]

    rs = _row_spec(tt, d)
    c1 = _const_spec((1, d))
    return pl.pallas_call(
        body, name=name, grid=(nsteps,),
        in_specs=[pl.BlockSpec(memory_space=pl.ANY), rs, _const_spec(w_out.shape), _row_spec(tt, d, 2), _row_spec(tt, d, 3),
                  rs, _const_spec(wm.shape), _const_spec(wmt.shape), _const_spec(bias.shape), c1],
        out_specs=[pl.BlockSpec((tt, 2 * d), lambda i: (i, 1)), _const_spec(wm.shape), _const_spec((SGU_GROUPS, ck)), c1, c1],
        out_shape=[jax.ShapeDtypeStruct(dp.shape, dp.dtype), jax.ShapeDtypeStruct(wm.shape, F32),
                   jax.ShapeDtypeStruct((SGU_GROUPS, ck), F32), jax.ShapeDtypeStruct((1, d), F32),
                   jax.ShapeDtypeStruct((1, d), F32)],
        scratch_shapes=[pltpu.VMEM((ck, d), F32)],
        input_output_aliases={0: 0},
        compiler_params=_cparams(("arbitrary",)))(dp, dy_b, w_out, p, p, vn, wm, wmt, bias, ln_g)


def _softmax_rows(s):
    e = jnp.exp(s - jnp.max(s, axis=-1, keepdims=True))
    return e / jnp.sum(e, axis=-1, keepdims=True)


def _attn_fwd(q, kv, x1, w_xo, gain, *, bl, s, name):
    t, d = q.shape
    mlen = kv.shape[0] // bl
    hd = d // HEADS
    tq = min(ATTN_TILE, s)
    nq = s // tq
    scale = hd ** -0.5

    def body(q_ref, kv_ref, x1_ref, w_ref, g_ref, o_ref, x2_ref, h_ref):
        for h in range(HEADS):
            hs = slice(h * hd, (h + 1) * hd)
            vs = slice(d + h * hd, d + (h + 1) * hd)
            pr = _softmax_rows(_dot(q_ref[:, hs], kv_ref[:, hs], _NT) * scale)
            o_ref[:, hs] = _dot(pr.astype(BF16), kv_ref[:, vs], _NN).astype(BF16)
        x2 = x1_ref[...] + _dot(o_ref[...], w_ref[...], _NN)
        x2_ref[...] = x2
        h_ref[...] = _rms_apply(x2, g_ref[...]).astype(BF16)

    qs = pl.BlockSpec((tq, d), lambda b, j: (b * nq + j, 0))
    return pl.pallas_call(
        body, name=name, grid=(bl, nq),
        in_specs=[qs, pl.BlockSpec((mlen, 2 * d), lambda b, j: (b, 0)), qs, _const_spec(w_xo.shape), _const_spec((1, d))],
        out_specs=[qs, qs, qs],
        out_shape=[jax.ShapeDtypeStruct((t, d), BF16), jax.ShapeDtypeStruct((t, d), F32), jax.ShapeDtypeStruct((t, d), BF16)],
        compiler_params=_cparams(("parallel", "parallel")))(q, kv, x1, w_xo, gain)


def _attn_bwd(q, kv, do, *, bl, s, name):
    t, d = q.shape
    mlen = kv.shape[0] // bl
    hd = d // HEADS
    tq = min(ATTN_TILE, s)
    nq = s // tq
    scale = hd ** -0.5

    def body(q_ref, kv_ref, do_ref, dq_ref, dkv_ref):
        @pl.when(pl.program_id(1) == 0)
        def _():
            dkv_ref[...] = jnp.zeros_like(dkv_ref)

        for h in range(HEADS):
            hs = slice(h * hd, (h + 1) * hd)
            vs = slice(d + h * hd, d + (h + 1) * hd)
            qh, kh, vh, doh = q_ref[:, hs], kv_ref[:, hs], kv_ref[:, vs], do_ref[:, hs]
            pr = _softmax_rows(_dot(qh, kh, _NT) * scale)
            dpr = _dot(doh, vh, _NT)
            dkv_ref[:, vs] += _dot(pr.astype(BF16), doh, _TN)
            ds = (pr * (dpr - jnp.sum(dpr * pr, axis=-1, keepdims=True)) * scale).astype(BF16)
            dq_ref[:, hs] = _dot(ds, kh, _NN).astype(BF16)
            dkv_ref[:, hs] += _dot(ds, qh, _TN)

    qs = pl.BlockSpec((tq, d), lambda b, j: (b * nq + j, 0))
    ks = pl.BlockSpec((mlen, 2 * d), lambda b, j: (b, 0))
    return pl.pallas_call(
        body, name=name, grid=(bl, nq), in_specs=[qs, ks, qs], out_specs=[qs, ks],
        out_shape=[jax.ShapeDtypeStruct((t, d), BF16), jax.ShapeDtypeStruct(kv.shape, F32)],
        compiler_params=_cparams(("parallel", "arbitrary")))(q, kv, do)


def _mesh_pos():
    return lax.axis_index("x"), lax.axis_index("y"), lax.axis_index("c")


def _all_gather(arrs, *, name):
    n = len(arrs)
    hbm = pl.BlockSpec(memory_space=pl.ANY)

    def body(*refs):
        ins, outs = refs[:n], refs[n:2 * n]
        send_sems, recv_sems, loc_sems = refs[2 * n:]
        x, y, c = _mesh_pos()
        me, sib = (x, y, c), (x, y, 1 - c)
        chips = [(1 - x, y), (x, 1 - y), (1 - x, 1 - y)]

        def idx(dev):
            return 4 * dev[0] + 2 * dev[1] + dev[2]

        def copy(w, k, block, to, from_input=False):
            return pltpu.make_async_remote_copy(
                src_ref=ins[w] if from_input else outs[w].at[idx(block)], dst_ref=outs[w].at[idx(block)],
                send_sem=send_sems.at[w, k], recv_sem=recv_sems.at[w, k], device_id=to, device_id_type=MESH_ID)

        own = [pltpu.make_async_copy(ins[w], outs[w].at[idx(me)], loc_sems.at[w]) for w in range(n)]
        for cp in own:
            cp.start()
        first = []
        for w in range(n):
            first.append(copy(w, 0, me, sib, True))
            first += [copy(w, 1 + j, me, (*chip, c), True) for j, chip in enumerate(chips)]
        for cp in first:
            cp.start()
        passed = []
        for j, chip in enumerate(chips):
            for w in range(n):
                copy(w, 1 + j, (*chip, c), me).wait_recv()
                fwd = copy(w, 4 + j, (*chip, c), sib)
                fwd.start()
                passed.append(fwd)
        for w in range(n):
            copy(w, 0, sib, me).wait_recv()
            for j, chip in enumerate(chips):
                copy(w, 4 + j, (*chip, 1 - c), me).wait_recv()
        for cp in first + passed:
            cp.wait_send()
        for cp in own:
            cp.wait()

    return pl.pallas_call(
        body, name=name, in_specs=[hbm] * n, out_specs=[hbm] * n,
        out_shape=[jax.ShapeDtypeStruct((N_DEV, *a.shape), a.dtype) for a in arrs],
        scratch_shapes=[pltpu.SemaphoreType.DMA((n, 7)), pltpu.SemaphoreType.DMA((n, 7)), pltpu.SemaphoreType.DMA((n,))],
    )(*arrs)


_HBM = pl.BlockSpec(memory_space=pltpu.HBM)
_SEM = pl.BlockSpec(memory_space=pltpu.SEMAPHORE)
_ANY = pl.BlockSpec(memory_space=pl.ANY)
_EFFECT = pltpu.SideEffectType.DATAFLOW_SIDE_EFFECTING
N_PEERS = N_DEV - 1


def _related(pos, r):
    x, y, c = pos
    return (1 - x if r & 4 else x, 1 - y if r & 2 else y, 1 - c if r & 1 else c)


def _dev_index(dev):
    return 4 * dev[0] + 2 * dev[1] + dev[2]


def _in_hbm(a):
    return pltpu.with_memory_space_constraint(a, pltpu.HBM)


def _split_copies(kind, srcs, lands, send_sems, recv_sems):
    pos = _mesh_pos()
    me = _dev_index(pos)
    out = []
    for w in range(len(srcs)):
        for r in range(1, N_DEV):
            peer = _related(pos, r)
            if kind == "gather":
                src, dst_here, dst_there = srcs[w], lands[w].at[_dev_index(peer)], lands[w].at[me]
            elif srcs[w].ndim == 2:
                cb = lands[w].shape[2]
                src = srcs[w].at[:, pl.ds(pl.multiple_of(_dev_index(peer) * cb, LANES), cb)]
                dst_here = dst_there = lands[w].at[r - 1]
            else:
                src, dst_here, dst_there = srcs[w].at[_dev_index(peer)], lands[w].at[r - 1], lands[w].at[r - 1]
            out.append((src, dst_here, dst_there, send_sems.at[w * N_PEERS + r - 1], recv_sems.at[w * N_PEERS + r - 1], peer))
    return out


def _copy_start(kind, srcs, land_shapes, *, name, after=None):
    n = len(srcs)
    n_after = 0 if after is None else 1

    def body(*refs):
        src_refs, land_refs = refs[:n], refs[n:2 * n]
        send_sems, recv_sems = refs[2 * n + n_after], refs[2 * n + n_after + 1]
        token = refs[-1]
        for src, _, dst, ssem, rsem, peer in _split_copies(kind, src_refs, land_refs, send_sems, recv_sems):
            pltpu.make_async_remote_copy(src_ref=src, dst_ref=dst, send_sem=ssem, recv_sem=rsem, device_id=peer,
                                         device_id_type=MESH_ID).start()
        token[...] = jnp.zeros_like(token)

    lands = [_in_hbm(lax.empty(shape, s.dtype)) for s, shape in zip(srcs, land_shapes)]
    res = pl.pallas_call(
        body, name=name,
        out_shape=(pltpu.SemaphoreType.DMA((n * N_PEERS,)), pltpu.SemaphoreType.DMA((n * N_PEERS,)),
                   *[pltpu.HBM(s.shape, s.dtype) for s in srcs], *[pltpu.HBM(l.shape, l.dtype) for l in lands],
                   jax.ShapeDtypeStruct((8, 128), F32)),
        in_specs=[_HBM] * (2 * n) + [_ANY] * n_after,
        out_specs=(_SEM, _SEM, *[_HBM] * (2 * n), pl.BlockSpec(memory_space=pltpu.VMEM)),
        input_output_aliases={i: 2 + i for i in range(2 * n)},
        compiler_params=pltpu.CompilerParams(has_side_effects=_EFFECT),
    )(*[_in_hbm(s) for s in srcs], *lands, *([] if after is None else [after]))
    return res[0], res[1], list(res[2:2 + n]), list(res[2 + n:2 + 2 * n]), res[-1]


def _copy_wait(kind, send_sems, recv_sems, srcs, lands, after, *, name):
    n = len(srcs)

    def body(*refs):
        src_refs, land_refs = refs[:n], refs[n:2 * n]
        ssems, rsems = refs[2 * n], refs[2 * n + 1]
        for src, dst, _, ssem, rsem, peer in _split_copies(kind, src_refs, land_refs, ssems, rsems):
            cp = pltpu.make_async_remote_copy(src_ref=src, dst_ref=dst, send_sem=ssem, recv_sem=rsem, device_id=peer,
                                              device_id_type=MESH_ID)
            cp.wait_send()
            cp.wait_recv()

    res = pl.pallas_call(
        body, name=name,
        out_shape=(*[pltpu.HBM(s.shape, s.dtype) for s in srcs], *[pltpu.HBM(l.shape, l.dtype) for l in lands]),
        in_specs=[_HBM] * (2 * n) + [_SEM, _SEM, _ANY], out_specs=tuple([_HBM] * (2 * n)),
        input_output_aliases={i: i for i in range(2 * n)},
        compiler_params=pltpu.CompilerParams(has_side_effects=_EFFECT),
    )(*srcs, *lands, send_sems, recv_sems, after)
    return list(res[:n]), list(res[n:])


def _row_tile(rows):
    return max(tr for tr in range(16, min(rows, 512) + 1, 16) if rows % tr == 0)


def _adamw_math(w, g, m, v):
    m2 = ADAM_B1 * m + (1.0 - ADAM_B1) * g
    v2 = ADAM_B2 * v + (1.0 - ADAM_B2) * (g * g)
    m_hat = m2 / (1.0 - ADAM_B1 ** ADAM_STEP)
    v_hat = v2 / (1.0 - ADAM_B2 ** ADAM_STEP)
    delta = -ADAM_LR * (m_hat / (jnp.sqrt(v_hat) + ADAM_EPS) + ADAM_WD * w)
    return delta, m2, v2


def _adamw_shard(partials, landed, dev, w, m, v, *, name):
    r, c = w.shape
    tr = _row_tile(r)

    def body(dev_ref, p_ref, l_ref, w_ref, m_ref, v_ref, g_out, d_out, m_out, v_out):
        del dev_ref
        g = p_ref[...].astype(F32)
        for k in range(N_PEERS):
            g = g + l_ref[k].astype(F32)
        delta, m2, v2 = _adamw_math(w_ref[...], g, m_ref[...], v_ref[...])
        g_out[...] = g
        d_out[...] = delta
        m_out[...] = m2
        v_out[...] = v2

    blk = pl.BlockSpec((tr, c), lambda i, dev_ref: (i, 0))
    if partials.ndim == 2:
        own = pl.BlockSpec((tr, c), lambda i, dev_ref: (i, dev_ref[0]))
    else:
        own = pl.BlockSpec((None, tr, c), lambda i, dev_ref: (dev_ref[0], i, 0))
    gs = pltpu.PrefetchScalarGridSpec(
        num_scalar_prefetch=1, grid=(r // tr,),
        in_specs=[own, pl.BlockSpec((N_PEERS, tr, c), lambda i, dev_ref: (0, i, 0)), blk, blk, blk],
        out_specs=[blk] * 4)
    return pl.pallas_call(
        body, name=name, grid_spec=gs, out_shape=[jax.ShapeDtypeStruct((r, c), F32)] * 4,
        compiler_params=_cparams(("parallel",)))(dev, partials, landed, w, m, v)


def _sum_devices(p_ref, *idx):
    g = p_ref[(0, *idx)]
    for k in range(1, N_DEV):
        g = g + p_ref[(k, *idx)]
    return g


def _adamw_replicated(parts, states, loss_row, *, name):
    n_parts, n_par = len(parts), len(states)
    n_vec = n_par - (n_parts - 1)

    def body(*refs):
        part_refs, st = refs[:n_parts], refs[n_parts:n_parts + 3 * n_par]
        outs = refs[n_parts + 3 * n_par:]
        outs[0][...] = _sum_devices(part_refs[0], slice(loss_row, loss_row + 1), slice(0, 1))
        for i in range(n_par):
            g = _sum_devices(part_refs[0], slice(i, i + 1)) if i < n_vec else _sum_devices(part_refs[1 + i - n_vec])
            delta, m2, v2 = _adamw_math(st[3 * i][...], g, st[3 * i + 1][...], st[3 * i + 2][...])
            for o, val in zip(outs[1 + 4 * i:5 + 4 * i], (g, delta, m2, v2)):
                o[...] = val

    flat = [a for wmv in states for a in wmv]
    return pl.pallas_call(
        body, name=name,
        out_shape=[jax.ShapeDtypeStruct((1, 1), F32)] + [jax.ShapeDtypeStruct(w.shape, F32) for w, _, _ in states for _ in range(4)],
        compiler_params=pltpu.CompilerParams(vmem_limit_bytes=VMEM_LIMIT))(*parts, *flat)


def _adamw_column_shards(parts, dev, states, row0s, *, name):
    _, rows, _ = parts.shape
    c = states[0][0].shape[1]

    def body(dev_ref, p_ref, *refs):
        del dev_ref
        st, outs = refs[:3 * len(states)], refs[3 * len(states):]
        for j, r0 in enumerate(row0s):
            w_ref = st[3 * j]
            g = _sum_devices(p_ref, slice(r0, r0 + w_ref.shape[0]))
            delta, m2, v2 = _adamw_math(w_ref[...], g, st[3 * j + 1][...], st[3 * j + 2][...])
            for o, val in zip(outs[4 * j:4 * j + 4], (g, delta, m2, v2)):
                o[...] = val

    whole = lambda a: pl.BlockSpec(a.shape, lambda i, dev_ref: (0, 0))
    flat = [a for wmv in states for a in wmv]
    outs = [w for w, _, _ in states for _ in range(4)]
    gs = pltpu.PrefetchScalarGridSpec(
        num_scalar_prefetch=1, grid=(1,),
        in_specs=[pl.BlockSpec((N_DEV, rows, c), lambda i, dev_ref: (0, 0, dev_ref[0]))] + [whole(a) for a in flat],
        out_specs=[whole(a) for a in outs])
    return pl.pallas_call(
        body, name=name, grid_spec=gs, out_shape=[jax.ShapeDtypeStruct(a.shape, F32) for a in outs],
        compiler_params=_cparams(("arbitrary",)))(dev, parts, *flat)


def _pad_rows(a, rows):
    return jnp.pad(a, ((0, rows - a.shape[0]), (0, 0)))


def _unblock_cols(g):
    return jnp.transpose(g, (1, 0, 2)).reshape(g.shape[1], N_DEV * g.shape[2])


def kernel(x, mem, norm_mix, w_in, b_gate, conv_w, conv_b, conv_ln_g, conv_ln_b, w_conv_out, sgu_ln_g, sgu_ln_b, sgu_w, sgu_b, w_sgu_out, w_mix_out, norm_xattn, norm_mem, w_q, w_kv, w_xo, norm_ffn, w_gu, w_down, norm_final, loss_target, m_norm_mix, m_w_in, m_b_gate, m_conv_w, m_conv_b, m_conv_ln_g, m_conv_ln_b, m_w_conv_out, m_sgu_ln_g, m_sgu_ln_b, m_sgu_w, m_sgu_b, m_w_sgu_out, m_w_mix_out, m_norm_xattn, m_norm_mem, m_w_q, m_w_kv, m_w_xo, m_norm_ffn, m_w_gu, m_w_down, m_norm_final, v_norm_mix, v_w_in, v_b_gate, v_conv_w, v_conv_b, v_conv_ln_g, v_conv_ln_b, v_w_conv_out, v_sgu_ln_g, v_sgu_ln_b, v_sgu_w, v_sgu_b, v_w_sgu_out, v_w_mix_out, v_norm_xattn, v_norm_mem, v_w_q, v_w_kv, v_w_xo, v_norm_ffn, v_w_gu, v_w_down, v_norm_final):
    given = dict(locals())
    bl, s, d = x.shape
    t = bl * s
    xf = x.reshape(t, d)
    tgt = loss_target.reshape(t, d)
    memf = mem.reshape(bl * mem.shape[1], d)
    cx, cy, cc = lax.axis_index("x"), lax.axis_index("y"), lax.axis_index("c")
    dev = 4 * cx + 2 * cy + cc
    dev_id = dev.astype(jnp.int32).reshape(1)
    col_sharded = ["w_in", "w_kv"]
    transposed = ["w_gu"]

    def shard_of(name, prefix=""):
        a = given[prefix + name][0]
        return jnp.transpose(a) if name in transposed else a

    def full_weight(name, blocks):
        return _unblock_cols(blocks) if name in col_sharded else blocks.reshape(N_DEV * blocks.shape[1], blocks.shape[2])

    g_bg, g_cw = _all_gather([_pad_rows(b_gate[0], 8), _pad_rows(conv_w[0], CONV_HALO)], name="gather_small_params")
    h1, p, w_in_blocks = _in_proj_gather(xf, norm_mix + g_bg[0, 7:8, 0:1], w_in[0].astype(BF16), name="in_proj")
    early = ["w_conv_out", "w_sgu_out", "w_mix_out", "w_q", "w_kv", "w_xo"]
    late = ["w_gu", "w_down"]
    shards = {n: shard_of(n).astype(BF16) for n in early + late}
    started = {}
    for grp, names in (("early", early), ("late", late)):
        srcs = [shards[n] for n in names]
        started[grp] = _copy_start("gather", srcs, [(N_DEV, *a.shape) for a in srcs], name=f"gather_{grp}_start", after=p)
    token = started["early"][4][0:1, 0:1] + started["late"][4][0:1, 0:1]
    wfull = {}
    bg_full = _unblock_cols(g_bg)
    cw_full = _unblock_cols(g_cw)

    def finish_gather(grp, names, after):
        ssem, rsem, srcs, lands, _ = started[grp]
        _, lands = _copy_wait("gather", ssem, rsem, srcs, lands, after, name=f"gather_{grp}_wait")
        for n, land in zip(names, lands):
            wfull[n] = full_weight(n, lax.dynamic_update_index_in_dim(land, shards[n], dev, 0))

    tri = jnp.tril(jnp.ones((SGU_CHUNK, SGU_CHUNK), bool))
    wm32 = jnp.where(tri[None], sgu_w[0], 0.0)
    wm = wm32.astype(BF16)
    wmt = jnp.transpose(wm32, (0, 2, 1)).astype(BF16)
    sgu_bias = jnp.broadcast_to(sgu_b[0][:, :, None], (SGU_GROUPS, SGU_CHUNK, d // SGU_GROUPS))

    c_conv, a_act = _conv_fwd(p, cw_full, conv_b + token, conv_ln_g, conv_ln_b, bl=bl, s=s, name="conv_fwd")
    sg, vn = _sgu_fwd(p, wm, sgu_bias, sgu_ln_g, sgu_ln_b + token, name="sgu_fwd")
    finish_gather("early", early, a_act[0:16, 0:128] + sg[0:16, 0:128])
    y_a, y_b, merged, x1, h2, q = _mix_out(p, a_act, sg, wfull["w_conv_out"], wfull["w_sgu_out"], bg_full, xf,
                                           wfull["w_mix_out"], norm_xattn, wfull["w_q"], name="mix_out")
    mem_n = _rms_fwd(memf, norm_mem, name="rms_mem")
    kv = _matmul(mem_n, wfull["w_kv"], mode="nn", out_dtype=BF16, name="mm_kv", tm=1024, tn=1024, tk=1024)
    o, x2, h3 = _attn_fwd(q, kv, x1, wfull["w_xo"], norm_ffn, bl=bl, s=s, name="attn_fwd")
    finish_gather("late", late, h3)
    gu, act, dx3, loss_part, d_norm_final = _ffn_fwd(h3, x2, tgt, wfull["w_gu"], wfull["w_down"],
                                                     norm_final.reshape(1, d), name="ffn_fwd")

    grads = {}
    sent = []

    def send_grads(names, tag, after=None):
        blocks, land_shapes = [], []
        for n in names:
            g = grads[n]
            if g.ndim == 2 and n in col_sharded:
                land_shapes.append((N_PEERS, g.shape[0], g.shape[1] // N_DEV))
            else:
                if g.ndim == 2:
                    g = g.reshape(N_DEV, -1, g.shape[1])
                land_shapes.append((N_PEERS, *g.shape[1:]))
            blocks.append(g)
        ssem, rsem, srcs, lands, tok = _copy_start("scatter", blocks, land_shapes, name=f"grads_{tag}_start", after=after)
        sent.append((names, ssem, rsem, srcs, lands))
        return tok[0:1, 0:1]

    dgu, dx2, do, d_norm_ffn = _ffn_bwd(dx3, gu, x2, wfull["w_down"], wfull["w_gu"], norm_ffn, wfull["w_xo"], name="ffn_bwd")
    grads["w_down"] = _matmul(act, dx3, mode="tn", out_dtype=BF16, name="mm_dw_down", tm=1408, tn=1024, tk=2048)
    grads["w_gu"] = _matmul(dgu, h3, mode="tn", out_dtype=BF16, name="mm_dw_gu", tm=1408, tn=1024, tk=2048)
    tok = send_grads(["w_down", "w_gu"], "ffn")
    grads["w_xo"] = _matmul(o, dx2, mode="tn", out_dtype=BF16, name="mm_dw_xo", tm=1024, tn=1024, tk=2048)
    dq, dkv = _attn_bwd(q, kv, do, bl=bl, s=s, name="attn_bwd")
    grads["w_kv"] = _matmul(mem_n, dkv, mode="tn", out_dtype=BF16, name="mm_dw_kv", tm=1024, tn=256, tk=1024,
                            col_blocks=N_DEV)
    tok2 = send_grads(["w_xo", "w_kv"], "attn")
    dmem_n = _matmul(dkv, wfull["w_kv"], mode="nt", out_dtype=F32, name="mm_d_mem", tm=512, tn=1024, tk=2048)
    d_norm_mem = _rms_gain_grad(dmem_n, memf, name="rms_mem_bwd")
    dx1, d_norm_xattn, dw_q = _proj_rms_bwd(dq, dx2, x1, wfull["w_q"], norm_xattn + (tok + tok2), name="q_rms_bwd", h=h2)
    dp, dy_a, dy_b, d_b_gate, dw_mix, dw_in_gates = _gates_bwd_fused(dx1, p, y_a, y_b, bg_full, wfull["w_mix_out"],
                                                                    merged, h1, name="gates_bwd")
    grads["w_q"] = dw_q.astype(BF16)
    grads["w_mix_out"] = dw_mix.astype(BF16)
    grads["w_sgu_out"] = _matmul(sg, dy_b, mode="tn", out_dtype=BF16, name="mm_dw_sgu", tm=1024, tn=1024, tk=2048)
    dc, d_conv_ln_g, d_conv_ln_b, dw_conv = _conv_ln_bwd_fused(dy_a, c_conv, a_act, wfull["w_conv_out"], conv_ln_g,
                                                               conv_ln_b, name="conv_ln_bwd")
    grads["w_conv_out"] = dw_conv.astype(BF16)
    tok = send_grads(["w_q", "w_mix_out", "w_sgu_out", "w_conv_out"], "mixer")
    dp, d_sgu_w, d_sgu_b, d_sgu_ln_g, d_sgu_ln_b = _sgu_bwd(dp, dy_b, wfull["w_sgu_out"], p, vn, wm, wmt, sgu_bias,
                                                             sgu_ln_g + tok, name="sgu_bwd")
    sgw_ssem, sgw_rsem, sgw_src, sgw_land, tok = _copy_start("gather", [d_sgu_w], [(N_DEV, *d_sgu_w.shape)],
                                                             name="gather_sgu_w_start")
    cw_full = cw_full + tok[0:1, 0:1]
    dw_in = _matmul(h1, dp, mode="tn", out_dtype=BF16, name="mm_dw_in_sgu", tm=1024, tn=1024, tk=2048,
                    b_cols=(2 * d, 2 * d), out_into=(dw_in_gates, 2 * d))
    dp, d_conv_w, d_conv_b, dw_in = _conv_bwd(dp, dc, p, cw_full, h1, dw_in, bl=bl, s=s, name="conv_bwd")
    grads["w_in"] = dw_in
    tok = send_grads(["w_in"], "in")
    grad_x, d_norm_mix = _proj_rms_bwd(dp, dx1, xf, w_in_blocks, norm_mix + tok, name="in_proj_bwd")
    out = {}

    vec_names = ["norm_mix", "conv_b", "conv_ln_g", "conv_ln_b", "sgu_ln_g", "sgu_ln_b", "norm_xattn", "norm_mem",
                 "norm_ffn", "norm_final"]
    vec_grads = [d_norm_mix, d_conv_b, d_conv_ln_g, d_conv_ln_b, d_sgu_ln_g, d_sgu_ln_b, d_norm_xattn, d_norm_mem,
                 d_norm_ffn, d_norm_final]
    n_vec = len(vec_names)
    small_vec = jnp.concatenate([g.reshape(1, d) for g in vec_grads]
                                + [jnp.broadcast_to(loss_part, (1, d)), jnp.zeros((16 - n_vec - 1, d), F32)], axis=0)
    small_cols = jnp.concatenate([d_b_gate, d_conv_w], axis=0)
    parts_vec, parts_sb, parts_cols = _all_gather([small_vec, d_sgu_b, small_cols], name="gather_small_grads")
    _, sgw_land = _copy_wait("gather", sgw_ssem, sgw_rsem, sgw_src, sgw_land, parts_vec, name="gather_sgu_w_wait")
    parts_sw = lax.dynamic_update_index_in_dim(sgw_land[0], d_sgu_w, dev, 0)
    rep_names = vec_names + ["sgu_b", "sgu_w"]
    rep_shapes = [(1, d)] * n_vec + [d_sgu_b.shape, d_sgu_w.shape]
    states = [tuple(given[pre + n].reshape(shape) for pre in ("", "m_", "v_")) for n, shape in zip(rep_names, rep_shapes)]
    res_rep = _adamw_replicated([parts_vec, parts_sb, parts_sw], states, n_vec, name="adamw_small")
    for i, n in enumerate(rep_names):
        out[n] = [r.reshape(given[n].shape) for r in res_rep[1 + 4 * i:5 + 4 * i]]
    res_cols = _adamw_column_shards(parts_cols, dev_id, [(b_gate[0], m_b_gate[0], v_b_gate[0]),
                                                        (conv_w[0], m_conv_w[0], v_conv_w[0])], (0, 8),
                                    name="adamw_small_cols")
    out["b_gate"] = [r[None] for r in res_cols[0:4]]
    out["conv_w"] = [r[None] for r in res_cols[4:8]]

    done = res_rep[1]
    for names, ssem, rsem, srcs, lands in sent:
        srcs, lands = _copy_wait("scatter", ssem, rsem, srcs, lands, done, name=f"grads_{names[0]}_wait")
        for n, partials, landed in zip(names, srcs, lands):
            res = _adamw_shard(partials, landed, dev_id, shard_of(n), shard_of(n, "m_"), shard_of(n, "v_"),
                               name=f"adamw_{n}")
            done = res[0]
            out[n] = [(jnp.transpose(r) if n in transposed else r)[None] for r in res]

    order = ["norm_mix", "w_in", "b_gate", "conv_w", "conv_b", "conv_ln_g", "conv_ln_b", "w_conv_out", "sgu_ln_g",
             "sgu_ln_b", "sgu_w", "sgu_b", "w_sgu_out", "w_mix_out", "norm_xattn", "norm_mem", "w_q", "w_kv", "w_xo",
             "norm_ffn", "w_gu", "w_down", "norm_final"]
    loss = res_rep[0][0, 0]
    return (loss, grad_x.reshape(x.shape), *[out[n][0] for n in order], *[out[n][1] for n in order],
            *[out[n][2] for n in order], *[out[n][3] for n in order])
```

```python
import jax
import jax.numpy as jnp
from jax import lax
from jax.experimental import pallas as pl
from jax.experimental.pallas import tpu as pltpu

F32 = jnp.float32
BF16 = jnp.bfloat16
RMS_EPS = 1e-6
LN_EPS = 1e-5
CONV_WIDTH = 31
CONV_HALO = 32
CONV_ROWS = 128
CONV_COLS = 256
LANES = 128
SGU_CHUNK = 128
SGU_GROUPS = 8
SGU_TILE = 512
HEADS = 4
N_DEV = 8
ADAM_LR, ADAM_B1, ADAM_B2, ADAM_EPS, ADAM_WD, ADAM_STEP = 0.001, 0.9, 0.999, 1e-08, 0.01, 10
VMEM_LIMIT = 56 * 1024 * 1024
TOKEN_TILE = 256
ATTN_TILE = 1024
MESH_ID = pl.DeviceIdType.MESH

_GELU_K = 0.7978845608028654
_GELU_C = 0.044715


def _cparams(sem=None):
    return pltpu.CompilerParams(dimension_semantics=sem, vmem_limit_bytes=VMEM_LIMIT)


def _sigmoid(v):
    return 0.5 * jnp.tanh(0.5 * v) + 0.5


def _gelu(v):
    return 0.5 * v * (1.0 + jnp.tanh(_GELU_K * (v + _GELU_C * v * v * v)))


def _gelu_grad(v):
    th = jnp.tanh(_GELU_K * (v + _GELU_C * v * v * v))
    return 0.5 * (1.0 + th) + 0.5 * v * (1.0 - th * th) * _GELU_K * (1.0 + 3.0 * _GELU_C * v * v)


def _dot(a, b, dims):
    return lax.dot_general(a, b, (dims, ((), ())), preferred_element_type=F32)


_NN = ((1,), (0,))
_NT = ((1,), (1,))
_TN = ((0,), (0,))


def _matmul(a, b, *, mode, out_dtype, name, tm=512, tn=512, tk=512, col_blocks=None, b_cols=None, out_into=None):
    if mode == "nn":
        (m, k), (_, n) = a.shape, b.shape
    elif mode == "nt":
        (m, k), (n, _) = a.shape, b.shape
    else:
        (k, m), (_, n) = a.shape, b.shape
    b_first = 0
    if b_cols is not None:
        assert mode == "tn"
        b_first, n = b_cols
    tm, tn, tk = min(tm, m), min(tn, n), min(tk, k)
    assert b_first % tn == 0
    b_first //= tn
    assert m % tm == 0 and n % tn == 0 and k % tk == 0, (name, a.shape, b.shape, tm, tn, tk)
    nk = k // tk
    dims = {"nn": _NN, "nt": _NT, "tn": _TN}[mode]

    def body(*refs):
        a_ref, b_ref = refs[:2]
        o_ref = refs[3] if out_into is not None else refs[2]
        part = _dot(a_ref[...].astype(BF16), b_ref[...].astype(BF16), dims)
        if nk == 1:
            o_ref[...] = part.astype(out_dtype)
        else:
            acc_ref = refs[-1]
            kk = pl.program_id(2)

            @pl.when(kk == 0)
            def _():
                acc_ref[...] = part

            @pl.when(kk > 0)
            def _():
                acc_ref[...] += part

            @pl.when(kk == nk - 1)
            def _():
                o_ref[...] = acc_ref[...].astype(out_dtype)

    resident = dict(pipeline_mode=pl.Buffered(1)) if (n == tn and nk == 1 and mode != "tn" and m > tm) else {}
    if mode == "nn":
        a_spec = pl.BlockSpec((tm, tk), lambda i, j, kk: (i, kk))
        b_spec = pl.BlockSpec((tk, tn), lambda i, j, kk: (kk, j), **resident)
    elif mode == "nt":
        a_spec = pl.BlockSpec((tm, tk), lambda i, j, kk: (i, kk))
        b_spec = pl.BlockSpec((tn, tk), lambda i, j, kk: (j, kk), **resident)
    else:
        a_spec = pl.BlockSpec((tk, tm), lambda i, j, kk: (kk, i))
        b_spec = pl.BlockSpec((tk, tn), lambda i, j, kk: (kk, j + b_first))
    in_specs, args = [a_spec, b_spec], [a, b]
    out_shape = [jax.ShapeDtypeStruct((m, n), out_dtype)]
    out_specs = [pl.BlockSpec((tm, tn), lambda i, j, kk: (i, j))]
    if col_blocks is not None:
        assert (n // col_blocks) % tn == 0
        per = n // col_blocks // tn
        out_shape = [jax.ShapeDtypeStruct((col_blocks, m, n // col_blocks), out_dtype)]
        out_specs = [pl.BlockSpec((None, tm, tn), lambda i, j, kk: (j // per, i, j % per))]
    aliases = {}
    if out_into is not None:
        target, first = out_into
        assert col_blocks is None and first % tn == 0 and target.dtype == out_dtype
        in_specs.append(pl.BlockSpec(memory_space=pl.ANY))
        args.append(target)
        aliases = {len(args) - 1: 0}
        out_shape = [jax.ShapeDtypeStruct(target.shape, target.dtype)]
        out_specs = [pl.BlockSpec((tm, tn), lambda i, j, kk: (i, j + first // tn))]
    res = pl.pallas_call(
        body, name=name, grid=(m // tm, n // tn, nk), in_specs=in_specs, out_specs=out_specs, out_shape=out_shape,
        scratch_shapes=[pltpu.VMEM((tm, tn), F32)] if nk > 1 else [], input_output_aliases=aliases,
        compiler_params=_cparams(("parallel", "parallel", "arbitrary")),
    )(*args)
    return res[0]


def _row_call(name, t, tm, rows_in, residents, rows_out, accs, body):
    n_in, n_res, n_out, n_acc = len(rows_in), len(residents), len(rows_out), len(accs)
    steps = t // tm
    assert t % tm == 0
    narrow = [i for i, a in enumerate(accs) if a[1] != F32]

    def kernel_body(*refs):
        in_refs, res_refs = refs[:n_in], refs[n_in:n_in + n_res]
        out_refs = refs[n_in + n_res:n_in + n_res + n_out]
        acc_out = list(refs[n_in + n_res + n_out:n_in + n_res + n_out + n_acc])
        scratch = refs[n_in + n_res + n_out + n_acc:]
        acc_refs = list(acc_out)
        for s_ref, i in zip(scratch, narrow):
            acc_refs[i] = s_ref
        if accs:
            @pl.when(pl.program_id(0) == 0)
            def _():
                for acc in acc_refs:
                    acc[...] = jnp.zeros_like(acc)
        body(in_refs, res_refs, out_refs, acc_refs)
        if narrow:
            @pl.when(pl.program_id(0) == steps - 1)
            def _():
                for i in narrow:
                    acc_out[i][...] = acc_refs[i][...].astype(acc_out[i].dtype)

    once = dict(pipeline_mode=pl.Buffered(1)) if steps > 1 else {}
    in_specs = [pl.BlockSpec((tm, cols), lambda i, cb=cb: (i, cb)) for _, cols, cb in rows_in]
    in_specs += [pl.BlockSpec(r.shape, lambda i, nd=r.ndim: (0,) * nd, **once) for r in residents]
    out_specs = [pl.BlockSpec((tm, cols), lambda i, cb=cb: (i, cb)) for _, cols, cb, _ in rows_out]
    out_specs += [pl.BlockSpec(a[0], lambda i, nd=len(a[0]), cb=(a[3] if len(a) == 4 else 0): (0,) * (nd - 1) + (cb,))
                  for a in accs]
    out_shape = [jax.ShapeDtypeStruct((t, total), dt) for total, _, _, dt in rows_out]
    out_shape += [jax.ShapeDtypeStruct((a[0][0], a[2]) if len(a) == 4 else a[0], a[1]) for a in accs]
    return pl.pallas_call(
        kernel_body, name=name, grid=(steps,), in_specs=in_specs, out_specs=out_specs, out_shape=out_shape,
        scratch_shapes=[pltpu.VMEM(accs[i][0], F32) for i in narrow],
        compiler_params=_cparams(("arbitrary",) if accs else ("parallel",)),
    )(*[a for a, _, _ in rows_in], *residents)


def _rms_apply(xv, gain):
    return xv * lax.rsqrt(jnp.mean(xv * xv, axis=-1, keepdims=True) + RMS_EPS) * gain


def _rms_grad(dres, dh, xv, gain):
    r = lax.rsqrt(jnp.mean(xv * xv, axis=-1, keepdims=True) + RMS_EPS)
    xhat = xv * r
    dxh = dh * gain
    dx = dres + r * (dxh - xhat * jnp.mean(dxh * xhat, axis=-1, keepdims=True))
    return dx, jnp.sum(dh * xhat, axis=0, keepdims=True)


def _in_proj_gather(xf, gain, w_shard, *, name):
    t, d = xf.shape
    cb = w_shard.shape[1]
    tm = min(1024, t)
    steps = t // tm
    mx, my, _ = _mesh_pos()
    order = jnp.stack([2 * mx + my, 2 * (1 - mx) + my, 2 * mx + (1 - my), 2 * (1 - mx) + (1 - my)]).astype(jnp.int32)

    def body(order_ref, x_ref, g_ref, ws_ref, h_ref, p_ref, wout_ref, w_ref, send_sems, recv_sems, own_sem):
        ps, i = pl.program_id(0), pl.program_id(1)
        x, y, c = _mesh_pos()
        me, sib = (x, y, c), (x, y, 1 - c)
        chips = [(1 - x, y), (x, 1 - y), (1 - x, 1 - y)]

        def copy(k, block, to, from_shard=False):
            return pltpu.make_async_remote_copy(
                src_ref=ws_ref if from_shard else w_ref.at[_dev_index(block)], dst_ref=w_ref.at[_dev_index(block)],
                send_sem=send_sems.at[k], recv_sem=recv_sems.at[k], device_id=to, device_id_type=MESH_ID)

        own = pltpu.make_async_copy(ws_ref, w_ref.at[_dev_index(me)], own_sem)
        first = [copy(0, me, sib, True)] + [copy(1 + j, me, (*chip, c), True) for j, chip in enumerate(chips)]
        passed = [copy(4 + j, (*chip, c), sib) for j, chip in enumerate(chips)]

        @pl.when(jnp.logical_and(ps == 0, i == 0))
        def _():
            own.start()
            for cp in first:
                cp.start()
            own.wait()
            copy(0, sib, me).wait_recv()

        for j, chip in enumerate(chips):
            @pl.when(jnp.logical_and(ps == j + 1, i == 0))
            def _(j=j, chip=chip):
                copy(1 + j, (*chip, c), me).wait_recv()
                passed[j].start()
                copy(4 + j, (*chip, 1 - c), me).wait_recv()

        h = _rms_apply(x_ref[...], g_ref[...]).astype(BF16)
        h_ref[...] = h
        chip_id = order_ref[ps]
        p_ref[:, 0:cb] = _dot(h, w_ref[2 * chip_id], _NN).astype(BF16)
        p_ref[:, cb:2 * cb] = _dot(h, w_ref[2 * chip_id + 1], _NN).astype(BF16)

        @pl.when(jnp.logical_and(ps == 3, i == steps - 1))
        def _():
            for cp in first + passed:
                cp.wait_send()
            keep = pltpu.make_async_copy(w_ref, wout_ref, own_sem)
            keep.start()
            keep.wait()

    gs = pltpu.PrefetchScalarGridSpec(
        num_scalar_prefetch=1, grid=(4, steps),
        in_specs=[pl.BlockSpec((tm, d), lambda ps, i, o: (i, 0)), pl.BlockSpec((1, d), lambda ps, i, o: (0, 0)),
                  pl.BlockSpec(memory_space=pl.ANY)],
        out_specs=[pl.BlockSpec((tm, d), lambda ps, i, o: (jnp.where(ps == 0, i, steps - 1), 0)),
                   pl.BlockSpec((tm, 2 * cb), lambda ps, i, o: (i, o[ps])), pl.BlockSpec(memory_space=pl.ANY)],
        scratch_shapes=[pltpu.VMEM((N_DEV, d, cb), BF16), pltpu.SemaphoreType.DMA((7,)), pltpu.SemaphoreType.DMA((7,)),
                        pltpu.SemaphoreType.DMA(())])
    return pl.pallas_call(
        body, name=name, grid_spec=gs,
        out_shape=[jax.ShapeDtypeStruct((t, d), BF16), jax.ShapeDtypeStruct((t, N_DEV * cb), BF16),
                   jax.ShapeDtypeStruct((N_DEV, d, cb), BF16)],
        compiler_params=_cparams(("arbitrary", "arbitrary")))(order, xf, gain, w_shard)


def _mix_out(p, a_act, sg, w_conv_out, w_sgu_out, b_gate, xf, w_mix, gain, w_q, *, name):
    t, d = xf.shape

    def body(ins, res, outs, accs):
        ga_ref, gb_ref, act_ref, sg_ref, x_ref = ins
        bg_ref, wm_ref, g_ref, wq_ref, wa_ref, wb_ref = res
        ya_ref, yb_ref, m_ref, x1_ref, h_ref, q_ref = outs
        y_a = _dot(act_ref[...], wa_ref[...], _NN).astype(BF16)
        y_b = _dot(sg_ref[...], wb_ref[...], _NN).astype(BF16)
        ya_ref[...] = y_a
        yb_ref[...] = y_b
        sa = _sigmoid(ga_ref[...].astype(F32) + bg_ref[0:1, :])
        sb = _sigmoid(gb_ref[...].astype(F32) + bg_ref[1:2, :])
        merged = (sa * y_a.astype(F32) + sb * y_b.astype(F32)).astype(BF16)
        m_ref[...] = merged
        x1 = x_ref[...] + _dot(merged, wm_ref[...], _NN)
        x1_ref[...] = x1
        h = _rms_apply(x1, g_ref[...]).astype(BF16)
        h_ref[...] = h
        q_ref[...] = _dot(h, wq_ref[...], _NN).astype(BF16)

    bf = (d, d, 0, BF16)
    return _row_call(name, t, min(512, t), [(p, d, 4), (p, d, 5), (a_act, d, 0), (sg, d, 0), (xf, d, 0)],
                     [b_gate, w_mix, gain, w_q, w_conv_out, w_sgu_out], [bf, bf, bf, (d, d, 0, F32), bf, bf], [], body)


def _ffn_fwd(h3, x2, target, w_gu_t, w_down, gain, *, name):
    t, d = x2.shape
    f2 = w_gu_t.shape[0]
    f = f2 // 2
    half = f // 2

    def body(ins, res, outs, accs):
        h_ref, x2_ref, t_ref = ins
        wgu_ref, wd_ref, g_ref = res
        gu_ref, act_ref, dx_ref = outs
        loss_ref, dg_ref = accs
        h = h_ref[...]
        x3 = x2_ref[...]
        for c0 in (0, half):
            gt = _dot(h, wgu_ref[c0:c0 + half, :], _NT).astype(BF16)
            up = _dot(h, wgu_ref[f + c0:f + c0 + half, :], _NT).astype(BF16)
            gu_ref[:, c0:c0 + half] = gt
            gu_ref[:, f + c0:f + c0 + half] = up
            gtf = gt.astype(F32)
            act = (gtf * _sigmoid(gtf) * up.astype(F32)).astype(BF16)
            act_ref[:, c0:c0 + half] = act
            x3 = x3 + _dot(act, wd_ref[c0:c0 + half, :], _NN)
        g = g_ref[...]
        r = lax.rsqrt(jnp.mean(x3 * x3, axis=-1, keepdims=True) + RMS_EPS)
        xhat = x3 * r
        err = xhat * g - t_ref[...]
        loss_ref[...] += 0.5 * jnp.sum(jnp.mean(err * err, axis=-1, keepdims=True), axis=0, keepdims=True)
        dy = err * (1.0 / d)
        dg_ref[...] += jnp.sum(dy * xhat, axis=0, keepdims=True)
        dxh = dy * g
        dx_ref[...] = r * (dxh - xhat * jnp.mean(dxh * xhat, axis=-1, keepdims=True))

    return _row_call(name, t, min(256, t), [(h3, d, 0), (x2, d, 0), (target, d, 0)], [w_gu_t, w_down, gain],
                     [(f2, f2, 0, BF16), (f, f, 0, BF16), (d, d, 0, F32)], [((1, 1), F32), ((1, d), F32)], body)


def _ffn_bwd(dx3, gu, x2, w_down, w_gu_t, gain, w_xo, *, name):
    t, d = x2.shape
    f2 = w_gu_t.shape[0]
    f = f2 // 2
    half = f // 2

    def body(ins, res, outs, accs):
        dx3_ref, gu_ref, x2_ref = ins
        wd_ref, wgu_ref, g_ref, wxo_ref = res
        dgu_ref, dx2_ref, do_ref = outs
        (dg_ref,) = accs
        dx3v = dx3_ref[...]
        dxb = dx3v.astype(BF16)
        dh = jnp.zeros(dx3v.shape, F32)
        for c0 in (0, half):
            dact = _dot(dxb, wd_ref[c0:c0 + half, :], _NT)
            gt = gu_ref[:, c0:c0 + half].astype(F32)
            up = gu_ref[:, f + c0:f + c0 + half].astype(F32)
            sg = _sigmoid(gt)
            dgt = (dact * up * sg * (1.0 + gt * (1.0 - sg))).astype(BF16)
            dup = (dact * gt * sg).astype(BF16)
            dgu_ref[:, c0:c0 + half] = dgt
            dgu_ref[:, f + c0:f + c0 + half] = dup
            dh = dh + _dot(dgt, wgu_ref[c0:c0 + half, :], _NN) + _dot(dup, wgu_ref[f + c0:f + c0 + half, :], _NN)
        dx2, dg = _rms_grad(dx3v, dh, x2_ref[...], g_ref[...])
        dx2_ref[...] = dx2
        dg_ref[...] += dg
        do_ref[...] = _dot(dx2.astype(BF16), wxo_ref[...], _NT).astype(BF16)

    return _row_call(name, t, min(256, t), [(dx3, d, 0), (gu, f2, 0), (x2, d, 0)], [w_down, w_gu_t, gain, w_xo],
                     [(f2, f2, 0, BF16), (d, d, 0, F32), (d, d, 0, BF16)], [((1, d), F32)], body)


def _proj_rms_bwd(dy, dres, x, w, gain, *, name, h=None, h_res=None):
    t, d = x.shape
    k = dy.shape[1]

    def body(ins, res, outs, accs):
        dy_ref, dres_ref, x_ref = ins[:3]
        w_ref, g_ref = res
        if h is not None:
            accs[1][...] += _dot(ins[3][...], dy_ref[...], _TN)
        if h_res is not None:
            accs[-1][...] += _dot(ins[-1][...], dres_ref[...].astype(BF16), _TN)
        if w.ndim == 3:
            cb = w.shape[2]
            dh = _dot(dy_ref[:, 0:cb], w_ref[0], _NT)
            for j in range(1, w.shape[0]):
                dh = dh + _dot(dy_ref[:, j * cb:(j + 1) * cb], w_ref[j], _NT)
        else:
            dh = _dot(dy_ref[...], w_ref[...], _NT)
        dx, dg = _rms_grad(dres_ref[...], dh, x_ref[...], g_ref[...])
        outs[0][...] = dx
        accs[0][...] += dg

    rows_in = [(dy, k, 0), (dres, d, 0), (x, d, 0)] + [(a, d, 0) for a in (h, h_res) if a is not None]
    accs = [((1, d), F32)] + ([((d, k), BF16)] if h is not None else []) + ([((d, d), BF16)] if h_res is not None else [])
    tm = 1024 if (w.ndim == 2 and h_res is None) else 512
    return _row_call(name, t, min(tm, t), rows_in, [w, gain], [(d, d, 0, F32)], accs, body)


def _gates_bwd_fused(dx1, p, y_a, y_b, b_gate, w_mix, merged, h1, *, name):
    t, d = y_a.shape

    def body(ins, res, outs, accs):
        dx_ref, ga_ref, gb_ref, ya_ref, yb_ref, m_ref, h1_ref = ins
        bg_ref, wm_ref = res
        dp_ref, dya_ref, dyb_ref = outs
        dbg_ref, dwm_ref, dwin_ref = accs
        dxb = dx_ref[...].astype(BF16)
        dwm_ref[...] += _dot(m_ref[...], dxb, _TN)
        dm = _dot(dxb, wm_ref[...], _NT)
        sa = _sigmoid(ga_ref[...].astype(F32) + bg_ref[0:1, :])
        sb = _sigmoid(gb_ref[...].astype(F32) + bg_ref[1:2, :])
        dya_ref[...] = (dm * sa).astype(BF16)
        dyb_ref[...] = (dm * sb).astype(BF16)
        dga = dm * ya_ref[...].astype(F32) * sa * (1.0 - sa)
        dgb = dm * yb_ref[...].astype(F32) * sb * (1.0 - sb)
        dp_ref[:, 0:d] = dga.astype(BF16)
        dp_ref[:, d:2 * d] = dgb.astype(BF16)
        dbg_ref[0:1, :] += jnp.sum(dga, axis=0, keepdims=True)
        dbg_ref[1:2, :] += jnp.sum(dgb, axis=0, keepdims=True)
        dwin_ref[...] += _dot(h1_ref[...], dp_ref[...], _TN)

    return _row_call(name, t, min(256, t),
                     [(dx1, d, 0), (p, d, 4), (p, d, 5), (y_a, d, 0), (y_b, d, 0), (merged, d, 0), (h1, d, 0)],
                     [b_gate, w_mix], [(p.shape[1], 2 * d, 2, BF16), (d, d, 0, BF16), (d, d, 0, BF16)],
                     [((8, d), F32), ((d, d), BF16), ((d, 2 * d), BF16, p.shape[1], 2)], body)


def _conv_ln_bwd_fused(dy_a, c, a_act, w_conv_out, ln_g, ln_b, *, name):
    t, d = c.shape

    def body(ins, res, outs, accs):
        dy_ref, c_ref, act_ref = ins
        w_ref, lg_ref, lb_ref = res
        dlg_ref, dlb_ref, dw_ref = accs
        dw_ref[...] += _dot(act_ref[...], dy_ref[...], _TN)
        dact = _dot(dy_ref[...], w_ref[...], _NT)
        cv = c_ref[...].astype(F32)
        g = lg_ref[...]
        mu = jnp.mean(cv, axis=-1, keepdims=True)
        dv = cv - mu
        rstd = lax.rsqrt(jnp.mean(dv * dv, axis=-1, keepdims=True) + LN_EPS)
        chat = dv * rstd
        aln = chat * g + lb_ref[...]
        sg = _sigmoid(aln)
        daln = dact * (sg * (1.0 + aln * (1.0 - sg)))
        dlb_ref[...] += jnp.sum(daln, axis=0, keepdims=True)
        dlg_ref[...] += jnp.sum(daln * chat, axis=0, keepdims=True)
        dchat = daln * g
        dc = rstd * (dchat - jnp.mean(dchat, axis=-1, keepdims=True)
                     - chat * jnp.mean(dchat * chat, axis=-1, keepdims=True))
        outs[0][...] = dc.astype(BF16)

    return _row_call(name, t, min(1024, t), [(dy_a, d, 0), (c, d, 0), (a_act, d, 0)], [w_conv_out, ln_g, ln_b],
                     [(d, d, 0, BF16)], [((1, d), F32), ((1, d), F32), ((d, d), BF16)], body)


def _row_spec(tt, cols, col_block=0):
    return pl.BlockSpec((tt, cols), lambda i: (i, col_block))


def _const_spec(shape):
    return pl.BlockSpec(shape, lambda *_: (0,) * len(shape))


def _rms_fwd(x, gain, *, name):
    t, d = x.shape
    tt = min(TOKEN_TILE, t)

    def body(x_ref, g_ref, h_ref):
        xv = x_ref[...]
        r = lax.rsqrt(jnp.mean(xv * xv, axis=-1, keepdims=True) + RMS_EPS)
        h_ref[...] = (xv * r * g_ref[...]).astype(BF16)

    return pl.pallas_call(
        body, name=name, grid=(t // tt,), in_specs=[_row_spec(tt, d), _const_spec((1, d))],
        out_specs=_row_spec(tt, d), out_shape=jax.ShapeDtypeStruct((t, d), BF16),
        compiler_params=_cparams(("parallel",)))(x, gain)


def _rms_gain_grad(dh, x, *, name):
    t, d = x.shape
    tt = min(TOKEN_TILE, t)

    def body(dh_ref, x_ref, dg_ref):
        @pl.when(pl.program_id(0) == 0)
        def _():
            dg_ref[...] = jnp.zeros_like(dg_ref)

        xv = x_ref[...]
        xhat = xv * lax.rsqrt(jnp.mean(xv * xv, axis=-1, keepdims=True) + RMS_EPS)
        dg_ref[...] += jnp.sum(dh_ref[...].astype(F32) * xhat, axis=0, keepdims=True)

    rs = _row_spec(tt, d)
    return pl.pallas_call(
        body, name=name, grid=(t // tt,), in_specs=[rs, rs], out_specs=_const_spec((1, d)),
        out_shape=jax.ShapeDtypeStruct((1, d), F32), compiler_params=_cparams(("arbitrary",)))(dh, x)


SUBLANES = 8
SHIFT_ROWS = 40


def _conv_apply(sbuf_ref, w_ref, out_ref, tt, offsets, bias_ref=None):
    d = out_ref.shape[1]
    for cc in range(d // LANES):
        cs = slice(cc * LANES, (cc + 1) * LANES)
        taps = [jnp.broadcast_to(w_ref[k:k + 1, cs], (SUBLANES, LANES)) for k in range(CONV_WIDTH)]
        bias = None if bias_ref is None else jnp.broadcast_to(bias_ref[:, cs], (SUBLANES, LANES))

        def row_body(r, carry, cs=cs, taps=taps, bias=bias):
            r0 = pl.multiple_of(r * CONV_ROWS, CONV_ROWS)
            for q in range(CONV_ROWS // SUBLANES):
                acc = _tap(sbuf_ref, r0 + q * SUBLANES, cs, offsets[0]) * taps[0]
                for k in range(1, CONV_WIDTH):
                    acc = acc + _tap(sbuf_ref, r0 + q * SUBLANES, cs, offsets[k]) * taps[k]
                if bias is not None:
                    acc = acc + bias
                out_ref[pl.ds(r0 + q * SUBLANES, SUBLANES), cs] = acc
            return carry

        lax.fori_loop(0, tt // CONV_ROWS, row_body, 0)


def _fill_shifts(sbuf_ref, rows):
    d = sbuf_ref.shape[2]
    assert rows % SHIFT_ROWS == 0

    def row_body(i, carry):
        r0 = pl.multiple_of(i * SHIFT_ROWS, SUBLANES)
        for cc in range(d // CONV_COLS):
            cs = slice(cc * CONV_COLS, (cc + 1) * CONV_COLS)
            win = sbuf_ref[0, pl.ds(r0, SHIFT_ROWS + SUBLANES), cs]
            for sh in range(1, SUBLANES):
                sbuf_ref[sh, pl.ds(r0, SHIFT_ROWS), cs] = win[sh:sh + SHIFT_ROWS, :]
        return carry

    lax.fori_loop(0, rows // SHIFT_ROWS, row_body, 0)


def _tap(sbuf_ref, r0, cs, offset):
    sh = offset % SUBLANES
    return sbuf_ref[sh, pl.ds(pl.multiple_of(r0 + (offset - sh), SUBLANES), SUBLANES), cs]


def _conv_specs(bl, s, tt, d, col_a, col_g):
    nj = s // tt
    per = tt // CONV_HALO
    main_a = pl.BlockSpec((tt, d), lambda b, j: (b * nj + j, col_a))
    main_g = pl.BlockSpec((tt, d), lambda b, j: (b * nj + j, col_g))
    prev = lambda b, j: jnp.maximum((b * nj + j) * per - 1, 0)
    halo_a = pl.BlockSpec((CONV_HALO, d), lambda b, j: (prev(b, j), col_a))
    halo_g = pl.BlockSpec((CONV_HALO, d), lambda b, j: (prev(b, j), col_g))
    return main_a, main_g, halo_a, halo_g


def _fill_glu(sbuf_ref, a_ref, g_ref, ha_ref, hg_ref, tt):
    first = pl.program_id(1) == 0
    ha = ha_ref[...].astype(F32)
    hg = hg_ref[...].astype(F32)
    sbuf_ref[0, pl.ds(0, CONV_HALO), :] = jnp.where(first, 0.0, ha * _sigmoid(hg))
    av = a_ref[...].astype(F32)
    gv = g_ref[...].astype(F32)
    sbuf_ref[0, pl.ds(CONV_HALO, tt), :] = av * _sigmoid(gv)
    _fill_shifts(sbuf_ref, tt + CONV_HALO - SUBLANES)


def _conv_fwd(p, conv_w, conv_b, ln_g, ln_b, *, bl, s, name):
    t = p.shape[0]
    d = conv_w.shape[1]
    tt = min(TOKEN_TILE, s)
    off = CONV_HALO - (CONV_WIDTH - 1)

    def body(a_ref, g_ref, ha_ref, hg_ref, w_ref, b_ref, lg_ref, lb_ref, c_ref, act_ref, sbuf_ref, cbuf_ref):
        _fill_glu(sbuf_ref, a_ref, g_ref, ha_ref, hg_ref, tt)

        _conv_apply(sbuf_ref, w_ref, cbuf_ref, tt, [off + k for k in range(CONV_WIDTH)], bias_ref=b_ref)
        cv = cbuf_ref[...]
        c_ref[...] = cv.astype(BF16)
        mu = jnp.mean(cv, axis=-1, keepdims=True)
        dv = cv - mu
        rstd = lax.rsqrt(jnp.mean(dv * dv, axis=-1, keepdims=True) + LN_EPS)
        aln = dv * rstd * lg_ref[...] + lb_ref[...]
        act_ref[...] = (aln * _sigmoid(aln)).astype(BF16)

    main_a, main_g, halo_a, halo_g = _conv_specs(bl, s, tt, d, 0, 1)
    out_spec = pl.BlockSpec((tt, d), lambda b, j: (b * (s // tt) + j, 0))
    return pl.pallas_call(
        body, name=name, grid=(bl, s // tt),
        in_specs=[main_a, main_g, halo_a, halo_g, _const_spec((CONV_HALO, d)), _const_spec((1, d)), _const_spec((1, d)),
                  _const_spec((1, d))],
        out_specs=[out_spec, out_spec],
        out_shape=[jax.ShapeDtypeStruct((t, d), BF16), jax.ShapeDtypeStruct((t, d), BF16)],
        scratch_shapes=[pltpu.VMEM((SUBLANES, tt + CONV_HALO, d), F32), pltpu.VMEM((tt, d), F32)],
        compiler_params=_cparams(("parallel", "parallel")))(p, p, p, p, conv_w, conv_b, ln_g, ln_b)


def _conv_bwd(dp, dc, p, conv_w, h1, dw_in, *, bl, s, name):
    t = p.shape[0]
    d = conv_w.shape[1]
    tt = min(TOKEN_TILE, s)
    nj = s // tt
    per = tt // CONV_HALO
    off = CONV_HALO - (CONV_WIDTH - 1)
    last_blk = t // CONV_HALO - 1

    def body(dp_in, dc_ref, dcn_ref, a_ref, g_ref, ha_ref, hg_ref, w_ref, h1_ref, dwin_in, dp_ref, dw_ref, db_ref,
             dwin_out, gbuf_ref, dbuf_ref, dglu_ref, acc_ref, dwin_ref):
        del dp_in, dwin_in
        b, j = pl.program_id(0), pl.program_id(1)
        start = jnp.logical_and(b == 0, j == 0)
        end = jnp.logical_and(b == bl - 1, j == nj - 1)

        @pl.when(start)
        def _():
            acc_ref[...] = jnp.zeros_like(acc_ref)
            db_ref[...] = jnp.zeros_like(db_ref)
            dwin_ref[...] = jnp.zeros_like(dwin_ref)

        _fill_glu(gbuf_ref, a_ref, g_ref, ha_ref, hg_ref, tt)
        dcv = dc_ref[...].astype(F32)
        dbuf_ref[0, pl.ds(0, tt), :] = dcv
        dbuf_ref[0, pl.ds(tt, CONV_HALO), :] = jnp.where(j == nj - 1, 0.0, dcn_ref[...].astype(F32))
        _fill_shifts(dbuf_ref, tt + CONV_HALO - SUBLANES)
        db_ref[...] += jnp.sum(dcv, axis=0, keepdims=True)

        for cc in range(d // LANES):
            cs = slice(cc * LANES, (cc + 1) * LANES)

            def row_body(r, accs, cs=cs):
                r0 = pl.multiple_of(r * CONV_ROWS, CONV_ROWS)
                accs = list(accs)
                for q in range(CONV_ROWS // SUBLANES):
                    dcw = dbuf_ref[0, pl.ds(r0 + q * SUBLANES, SUBLANES), cs]
                    for k in range(CONV_WIDTH):
                        accs[k] = accs[k] + dcw * _tap(gbuf_ref, r0 + q * SUBLANES, cs, off + k)
                return tuple(accs)

            zero = jnp.zeros((SUBLANES, LANES), F32)
            accs = lax.fori_loop(0, tt // CONV_ROWS, row_body, (zero,) * CONV_WIDTH)
            for k in range(CONV_WIDTH):
                acc_ref[k, :, cs] += accs[k]

        _conv_apply(dbuf_ref, w_ref, dglu_ref, tt, [CONV_WIDTH - 1 - k for k in range(CONV_WIDTH)])
        dglu = dglu_ref[...]
        av = a_ref[...].astype(F32)
        sg = _sigmoid(g_ref[...].astype(F32))
        dp_ref[:, 0:d] = (dglu * sg).astype(BF16)
        dp_ref[:, d:2 * d] = (dglu * av * sg * (1.0 - sg)).astype(BF16)
        dwin_ref[...] += _dot(h1_ref[...], dp_ref[...], _TN)

        @pl.when(end)
        def _():
            for k in range(CONV_WIDTH):
                dw_ref[k:k + 1, :] = jnp.sum(acc_ref[k], axis=0, keepdims=True)
            dw_ref[CONV_WIDTH:CONV_HALO, :] = jnp.zeros((CONV_HALO - CONV_WIDTH, d), F32)
            dwin_out[...] = dwin_ref[...].astype(dwin_out.dtype)

    main_a, main_g, halo_a, halo_g = _conv_specs(bl, s, tt, d, 0, 1)
    dc_main = pl.BlockSpec((tt, d), lambda b, j: (b * nj + j, 0))
    dc_next = pl.BlockSpec((CONV_HALO, d), lambda b, j: (jnp.minimum((b * nj + j + 1) * per, last_blk), 0))
    hbm = pl.BlockSpec(memory_space=pl.ANY)
    return pl.pallas_call(
        body, name=name, grid=(bl, nj),
        in_specs=[hbm, dc_main, dc_next, main_a, main_g, halo_a, halo_g, _const_spec((CONV_HALO, d)), dc_main, hbm],
        out_specs=[pl.BlockSpec((tt, 2 * d), lambda b, j: (b * nj + j, 0)), _const_spec((CONV_HALO, d)), _const_spec((1, d)),
                   _const_spec((d, 2 * d))],
        out_shape=[jax.ShapeDtypeStruct(dp.shape, dp.dtype), jax.ShapeDtypeStruct((CONV_HALO, d), F32),
                   jax.ShapeDtypeStruct((1, d), F32), jax.ShapeDtypeStruct(dw_in.shape, dw_in.dtype)],
        scratch_shapes=[pltpu.VMEM((SUBLANES, tt + CONV_HALO, d), F32), pltpu.VMEM((SUBLANES, tt + CONV_HALO, d), F32),
                        pltpu.VMEM((tt, d), F32), pltpu.VMEM((CONV_HALO, SUBLANES, d), F32), pltpu.VMEM((d, 2 * d), F32)],
        input_output_aliases={0: 0, 9: 3},
        compiler_params=_cparams(("arbitrary", "arbitrary")))(dp, dc, dc, p, p, p, p, conv_w, h1, dw_in)


def _sgu_stats(bv):
    gv = _gelu(bv)
    mu = jnp.mean(gv, axis=-1, keepdims=True)
    dv = gv - mu
    rstd = lax.rsqrt(jnp.mean(dv * dv, axis=-1, keepdims=True) + LN_EPS)
    return dv * rstd, rstd


def _sgu_fwd(p, wm, bias, ln_g, ln_b, *, name):
    t = p.shape[0]
    d = ln_g.shape[1]
    tt = SGU_TILE
    gd = d // SGU_GROUPS

    def body(u_ref, v_ref, wm_ref, bias_ref, lg_ref, lb_ref, sg_ref, vn_ref):
        u = _gelu(u_ref[...].astype(F32))
        vhat, _ = _sgu_stats(v_ref[...].astype(F32))
        vb = (vhat * lg_ref[...] + lb_ref[...]).astype(BF16)
        vn_ref[...] = vb
        for ci in range(tt // SGU_CHUNK):
            rows = slice(ci * SGU_CHUNK, (ci + 1) * SGU_CHUNK)
            for g in range(SGU_GROUPS):
                gs = slice(g * gd, (g + 1) * gd)
                z = _dot(wm_ref[g], vb[rows, gs], _NN) + bias_ref[g]
                sg_ref[rows, gs] = (u[rows, gs] * z).astype(BF16)

    rs = _row_spec(tt, d)
    return pl.pallas_call(
        body, name=name, grid=(t // tt,),
        in_specs=[_row_spec(tt, d, 2), _row_spec(tt, d, 3), _const_spec(wm.shape), _const_spec(bias.shape),
                  _const_spec((1, d)), _const_spec((1, d))],
        out_specs=[rs, rs], out_shape=[jax.ShapeDtypeStruct((t, d), BF16), jax.ShapeDtypeStruct((t, d), BF16)],
        compiler_params=_cparams(("parallel",)))(p, p, wm, bias, ln_g, ln_b)


def _sgu_bwd(dp, dy_b, w_out, p, vn, wm, wmt, bias, ln_g, *, name):
    t = p.shape[0]
    d = ln_g.shape[1]
    tt = SGU_TILE
    ck = SGU_CHUNK
    gd = d // SGU_GROUPS
    nsteps = t // tt

    def body(dp_in, dyb_ref, wout_ref, u_ref, v_ref, vn_ref, wm_ref, wmt_ref, bias_ref, lg_ref,
             dp_ref, dw_ref, dbs_ref, dlg_ref, dlb_ref, dz_acc):
        del dp_in
        i = pl.program_id(0)

        @pl.when(i == 0)
        def _():
            dw_ref[...] = jnp.zeros_like(dw_ref)
            dlg_ref[...] = jnp.zeros_like(dlg_ref)
            dlb_ref[...] = jnp.zeros_like(dlb_ref)
            dz_acc[...] = jnp.zeros_like(dz_acc)

        bu = u_ref[...].astype(F32)
        bv = v_ref[...].astype(F32)
        u = _gelu(bu)
        vhat, rstd = _sgu_stats(bv)
        vb = vn_ref[...]
        dsg = _dot(dyb_ref[...], wout_ref[...], _NT)
        row = lax.broadcasted_iota(jnp.int32, (ck, ck), 0)
        col = lax.broadcasted_iota(jnp.int32, (ck, ck), 1)
        causal = col <= row
        du_rows, dv_rows = [], []
        for ci in range(tt // ck):
            rows = slice(ci * ck, (ci + 1) * ck)
            du_parts, dv_parts = [], []
            for g in range(SGU_GROUPS):
                gs = slice(g * gd, (g + 1) * gd)
                z = _dot(wm_ref[g], vb[rows, gs], _NN) + bias_ref[g]
                du_parts.append(dsg[rows, gs] * z)
                dz = dsg[rows, gs] * u[rows, gs]
                dz_acc[:, gs] += dz
                dzb = dz.astype(BF16)
                dw_ref[g] += jnp.where(causal, _dot(dzb, vb[rows, gs], _NT), 0.0)
                dv_parts.append(_dot(wmt_ref[g], dzb, _NN))
            du_rows.append(jnp.concatenate(du_parts, axis=1))
            dv_rows.append(jnp.concatenate(dv_parts, axis=1))
        du = jnp.concatenate(du_rows, axis=0)
        dv = jnp.concatenate(dv_rows, axis=0)
        dp_ref[:, 0:d] = (du * _gelu_grad(bu)).astype(BF16)
        dlb_ref[...] += jnp.sum(dv, axis=0, keepdims=True)
        dlg_ref[...] += jnp.sum(dv * vhat, axis=0, keepdims=True)
        dvh = dv * lg_ref[...]
        dgv = rstd * (dvh - jnp.mean(dvh, axis=-1, keepdims=True) - vhat * jnp.mean(dvh * vhat, axis=-1, keepdims=True))
        dp_ref[:, d:2 * d] = (dgv * _gelu_grad(bv)).astype(BF16)

        @pl.when(i == nsteps - 1)
        def _():
            ones = jnp.ones((8, gd), F32)
            for g in range(SGU_GROUPS):
                gs = slice(g * gd, (g + 1) * gd)
                tot = lax.dot_general(ones, dz_acc[:, gs], (_NT, ((), ())), preferred_element_type=F32,
                                      precision=lax.Precision.HIGHEST)
                dbs_ref[g:g + 1, :] = tot[0:1, :]

    rs = _row_spec(tt, d)
    c1 = _const_spec((1, d))
    return pl.pallas_call(
        body, name=name, grid=(nsteps,),
        in_specs=[pl.BlockSpec(memory_space=pl.ANY), rs, _const_spec(w_out.shape), _row_spec(tt, d, 2), _row_spec(tt, d, 3),
                  rs, _const_spec(wm.shape), _const_spec(wmt.shape), _const_spec(bias.shape), c1],
        out_specs=[pl.BlockSpec((tt, 2 * d), lambda i: (i, 1)), _const_spec(wm.shape), _const_spec((SGU_GROUPS, ck)), c1, c1],
        out_shape=[jax.ShapeDtypeStruct(dp.shape, dp.dtype), jax.ShapeDtypeStruct(wm.shape, F32),
                   jax.ShapeDtypeStruct((SGU_GROUPS, ck), F32), jax.ShapeDtypeStruct((1, d), F32),
                   jax.ShapeDtypeStruct((1, d), F32)],
        scratch_shapes=[pltpu.VMEM((ck, d), F32)],
        input_output_aliases={0: 0},
        compiler_params=_cparams(("arbitrary",)))(dp, dy_b, w_out, p, p, vn, wm, wmt, bias, ln_g)


def _softmax_rows(s):
    e = jnp.exp(s - jnp.max(s, axis=-1, keepdims=True))
    return e / jnp.sum(e, axis=-1, keepdims=True)


def _attn_fwd(q, kv, x1, w_xo, gain, *, bl, s, name):
    t, d = q.shape
    mlen = kv.shape[0] // bl
    hd = d // HEADS
    tq = min(ATTN_TILE, s)
    nq = s // tq
    scale = hd ** -0.5

    def body(q_ref, kv_ref, x1_ref, w_ref, g_ref, o_ref, x2_ref, h_ref):
        for h in range(HEADS):
            hs = slice(h * hd, (h + 1) * hd)
            vs = slice(d + h * hd, d + (h + 1) * hd)
            pr = _softmax_rows(_dot(q_ref[:, hs], kv_ref[:, hs], _NT) * scale)
            o_ref[:, hs] = _dot(pr.astype(BF16), kv_ref[:, vs], _NN).astype(BF16)
        x2 = x1_ref[...] + _dot(o_ref[...], w_ref[...], _NN)
        x2_ref[...] = x2
        h_ref[...] = _rms_apply(x2, g_ref[...]).astype(BF16)

    qs = pl.BlockSpec((tq, d), lambda b, j: (b * nq + j, 0))
    return pl.pallas_call(
        body, name=name, grid=(bl, nq),
        in_specs=[qs, pl.BlockSpec((mlen, 2 * d), lambda b, j: (b, 0)), qs, _const_spec(w_xo.shape), _const_spec((1, d))],
        out_specs=[qs, qs, qs],
        out_shape=[jax.ShapeDtypeStruct((t, d), BF16), jax.ShapeDtypeStruct((t, d), F32), jax.ShapeDtypeStruct((t, d), BF16)],
        compiler_params=_cparams(("parallel", "parallel")))(q, kv, x1, w_xo, gain)


def _attn_bwd(q, kv, do, *, bl, s, name):
    t, d = q.shape
    mlen = kv.shape[0] // bl
    hd = d // HEADS
    tq = min(ATTN_TILE, s)
    nq = s // tq
    scale = hd ** -0.5

    def body(q_ref, kv_ref, do_ref, dq_ref, dkv_ref):
        @pl.when(pl.program_id(1) == 0)
        def _():
            dkv_ref[...] = jnp.zeros_like(dkv_ref)

        for h in range(HEADS):
            hs = slice(h * hd, (h + 1) * hd)
            vs = slice(d + h * hd, d + (h + 1) * hd)
            qh, kh, vh, doh = q_ref[:, hs], kv_ref[:, hs], kv_ref[:, vs], do_ref[:, hs]
            pr = _softmax_rows(_dot(qh, kh, _NT) * scale)
            dpr = _dot(doh, vh, _NT)
            dkv_ref[:, vs] += _dot(pr.astype(BF16), doh, _TN)
            ds = (pr * (dpr - jnp.sum(dpr * pr, axis=-1, keepdims=True)) * scale).astype(BF16)
            dq_ref[:, hs] = _dot(ds, kh, _NN).astype(BF16)
            dkv_ref[:, hs] += _dot(ds, qh, _TN)

    qs = pl.BlockSpec((tq, d), lambda b, j: (b * nq + j, 0))
    ks = pl.BlockSpec((mlen, 2 * d), lambda b, j: (b, 0))
    return pl.pallas_call(
        body, name=name, grid=(bl, nq), in_specs=[qs, ks, qs], out_specs=[qs, ks],
        out_shape=[jax.ShapeDtypeStruct((t, d), BF16), jax.ShapeDtypeStruct(kv.shape, F32)],
        compiler_params=_cparams(("parallel", "arbitrary")))(q, kv, do)


def _mesh_pos():
    return lax.axis_index("x"), lax.axis_index("y"), lax.axis_index("c")


def _all_gather(arrs, *, name):
    n = len(arrs)
    hbm = pl.BlockSpec(memory_space=pl.ANY)

    def body(*refs):
        ins, outs = refs[:n], refs[n:2 * n]
        send_sems, recv_sems, loc_sems = refs[2 * n:]
        x, y, c = _mesh_pos()
        me, sib = (x, y, c), (x, y, 1 - c)
        chips = [(1 - x, y), (x, 1 - y), (1 - x, 1 - y)]

        def idx(dev):
            return 4 * dev[0] + 2 * dev[1] + dev[2]

        def copy(w, k, block, to, from_input=False):
            return pltpu.make_async_remote_copy(
                src_ref=ins[w] if from_input else outs[w].at[idx(block)], dst_ref=outs[w].at[idx(block)],
                send_sem=send_sems.at[w, k], recv_sem=recv_sems.at[w, k], device_id=to, device_id_type=MESH_ID)

        own = [pltpu.make_async_copy(ins[w], outs[w].at[idx(me)], loc_sems.at[w]) for w in range(n)]
        for cp in own:
            cp.start()
        first = []
        for w in range(n):
            first.append(copy(w, 0, me, sib, True))
            first += [copy(w, 1 + j, me, (*chip, c), True) for j, chip in enumerate(chips)]
        for cp in first:
            cp.start()
        passed = []
        for j, chip in enumerate(chips):
            for w in range(n):
                copy(w, 1 + j, (*chip, c), me).wait_recv()
                fwd = copy(w, 4 + j, (*chip, c), sib)
                fwd.start()
                passed.append(fwd)
        for w in range(n):
            copy(w, 0, sib, me).wait_recv()
            for j, chip in enumerate(chips):
                copy(w, 4 + j, (*chip, 1 - c), me).wait_recv()
        for cp in first + passed:
            cp.wait_send()
        for cp in own:
            cp.wait()

    return pl.pallas_call(
        body, name=name, in_specs=[hbm] * n, out_specs=[hbm] * n,
        out_shape=[jax.ShapeDtypeStruct((N_DEV, *a.shape), a.dtype) for a in arrs],
        scratch_shapes=[pltpu.SemaphoreType.DMA((n, 7)), pltpu.SemaphoreType.DMA((n, 7)), pltpu.SemaphoreType.DMA((n,))],
    )(*arrs)


_HBM = pl.BlockSpec(memory_space=pltpu.HBM)
_SEM = pl.BlockSpec(memory_space=pltpu.SEMAPHORE)
_ANY = pl.BlockSpec(memory_space=pl.ANY)
_EFFECT = pltpu.SideEffectType.DATAFLOW_SIDE_EFFECTING
N_PEERS = N_DEV - 1


def _related(pos, r):
    x, y, c = pos
    return (1 - x if r & 4 else x, 1 - y if r & 2 else y, 1 - c if r & 1 else c)


def _dev_index(dev):
    return 4 * dev[0] + 2 * dev[1] + dev[2]


def _in_hbm(a):
    return pltpu.with_memory_space_constraint(a, pltpu.HBM)


def _split_copies(kind, srcs, lands, send_sems, recv_sems):
    pos = _mesh_pos()
    me = _dev_index(pos)
    out = []
    for w in range(len(srcs)):
        for r in range(1, N_DEV):
            peer = _related(pos, r)
            if kind == "gather":
                src, dst_here, dst_there = srcs[w], lands[w].at[_dev_index(peer)], lands[w].at[me]
            elif srcs[w].ndim == 2:
                cb = lands[w].shape[2]
                src = srcs[w].at[:, pl.ds(pl.multiple_of(_dev_index(peer) * cb, LANES), cb)]
                dst_here = dst_there = lands[w].at[r - 1]
            else:
                src, dst_here, dst_there = srcs[w].at[_dev_index(peer)], lands[w].at[r - 1], lands[w].at[r - 1]
            out.append((src, dst_here, dst_there, send_sems.at[w * N_PEERS + r - 1], recv_sems.at[w * N_PEERS + r - 1], peer))
    return out


def _copy_start(kind, srcs, land_shapes, *, name, after=None):
    n = len(srcs)
    n_after = 0 if after is None else 1

    def body(*refs):
        src_refs, land_refs = refs[:n], refs[n:2 * n]
        send_sems, recv_sems = refs[2 * n + n_after], refs[2 * n + n_after + 1]
        token = refs[-1]
        for src, _, dst, ssem, rsem, peer in _split_copies(kind, src_refs, land_refs, send_sems, recv_sems):
            pltpu.make_async_remote_copy(src_ref=src, dst_ref=dst, send_sem=ssem, recv_sem=rsem, device_id=peer,
                                         device_id_type=MESH_ID).start()
        token[...] = jnp.zeros_like(token)

    lands = [_in_hbm(lax.empty(shape, s.dtype)) for s, shape in zip(srcs, land_shapes)]
    res = pl.pallas_call(
        body, name=name,
        out_shape=(pltpu.SemaphoreType.DMA((n * N_PEERS,)), pltpu.SemaphoreType.DMA((n * N_PEERS,)),
                   *[pltpu.HBM(s.shape, s.dtype) for s in srcs], *[pltpu.HBM(l.shape, l.dtype) for l in lands],
                   jax.ShapeDtypeStruct((8, 128), F32)),
        in_specs=[_HBM] * (2 * n) + [_ANY] * n_after,
        out_specs=(_SEM, _SEM, *[_HBM] * (2 * n), pl.BlockSpec(memory_space=pltpu.VMEM)),
        input_output_aliases={i: 2 + i for i in range(2 * n)},
        compiler_params=pltpu.CompilerParams(has_side_effects=_EFFECT),
    )(*[_in_hbm(s) for s in srcs], *lands, *([] if after is None else [after]))
    return res[0], res[1], list(res[2:2 + n]), list(res[2 + n:2 + 2 * n]), res[-1]


def _copy_wait(kind, send_sems, recv_sems, srcs, lands, after, *, name):
    n = len(srcs)

    def body(*refs):
        src_refs, land_refs = refs[:n], refs[n:2 * n]
        ssems, rsems = refs[2 * n], refs[2 * n + 1]
        for src, dst, _, ssem, rsem, peer in _split_copies(kind, src_refs, land_refs, ssems, rsems):
            cp = pltpu.make_async_remote_copy(src_ref=src, dst_ref=dst, send_sem=ssem, recv_sem=rsem, device_id=peer,
                                              device_id_type=MESH_ID)
            cp.wait_send()
            cp.wait_recv()

    res = pl.pallas_call(
        body, name=name,
        out_shape=(*[pltpu.HBM(s.shape, s.dtype) for s in srcs], *[pltpu.HBM(l.shape, l.dtype) for l in lands]),
        in_specs=[_HBM] * (2 * n) + [_SEM, _SEM, _ANY], out_specs=tuple([_HBM] * (2 * n)),
        input_output_aliases={i: i for i in range(2 * n)},
        compiler_params=pltpu.CompilerParams(has_side_effects=_EFFECT),
    )(*srcs, *lands, send_sems, recv_sems, after)
    return list(res[:n]), list(res[n:])


def _row_tile(rows):
    return max(tr for tr in range(16, min(rows, 512) + 1, 16) if rows % tr == 0)


def _adamw_math(w, g, m, v):
    m2 = ADAM_B1 * m + (1.0 - ADAM_B1) * g
    v2 = ADAM_B2 * v + (1.0 - ADAM_B2) * (g * g)
    m_hat = m2 / (1.0 - ADAM_B1 ** ADAM_STEP)
    v_hat = v2 / (1.0 - ADAM_B2 ** ADAM_STEP)
    delta = -ADAM_LR * (m_hat / (jnp.sqrt(v_hat) + ADAM_EPS) + ADAM_WD * w)
    return delta, m2, v2


def _adamw_shard(partials, landed, dev, w, m, v, *, name):
    r, c = w.shape
    tr = _row_tile(r)

    def body(dev_ref, p_ref, l_ref, w_ref, m_ref, v_ref, g_out, d_out, m_out, v_out):
        del dev_ref
        g = p_ref[...].astype(F32)
        for k in range(N_PEERS):
            g = g + l_ref[k].astype(F32)
        delta, m2, v2 = _adamw_math(w_ref[...], g, m_ref[...], v_ref[...])
        g_out[...] = g
        d_out[...] = delta
        m_out[...] = m2
        v_out[...] = v2

    blk = pl.BlockSpec((tr, c), lambda i, dev_ref: (i, 0))
    if partials.ndim == 2:
        own = pl.BlockSpec((tr, c), lambda i, dev_ref: (i, dev_ref[0]))
    else:
        own = pl.BlockSpec((None, tr, c), lambda i, dev_ref: (dev_ref[0], i, 0))
    gs = pltpu.PrefetchScalarGridSpec(
        num_scalar_prefetch=1, grid=(r // tr,),
        in_specs=[own, pl.BlockSpec((N_PEERS, tr, c), lambda i, dev_ref: (0, i, 0)), blk, blk, blk],
        out_specs=[blk] * 4)
    return pl.pallas_call(
        body, name=name, grid_spec=gs, out_shape=[jax.ShapeDtypeStruct((r, c), F32)] * 4,
        compiler_params=_cparams(("parallel",)))(dev, partials, landed, w, m, v)


def _sum_devices(p_ref, *idx):
    g = p_ref[(0, *idx)]
    for k in range(1, N_DEV):
        g = g + p_ref[(k, *idx)]
    return g


def _adamw_replicated(parts, states, loss_row, *, name):
    n_parts, n_par = len(parts), len(states)
    n_vec = n_par - (n_parts - 1)

    def body(*refs):
        part_refs, st = refs[:n_parts], refs[n_parts:n_parts + 3 * n_par]
        outs = refs[n_parts + 3 * n_par:]
        outs[0][...] = _sum_devices(part_refs[0], slice(loss_row, loss_row + 1), slice(0, 1))
        for i in range(n_par):
            g = _sum_devices(part_refs[0], slice(i, i + 1)) if i < n_vec else _sum_devices(part_refs[1 + i - n_vec])
            delta, m2, v2 = _adamw_math(st[3 * i][...], g, st[3 * i + 1][...], st[3 * i + 2][...])
            for o, val in zip(outs[1 + 4 * i:5 + 4 * i], (g, delta, m2, v2)):
                o[...] = val

    flat = [a for wmv in states for a in wmv]
    return pl.pallas_call(
        body, name=name,
        out_shape=[jax.ShapeDtypeStruct((1, 1), F32)] + [jax.ShapeDtypeStruct(w.shape, F32) for w, _, _ in states for _ in range(4)],
        compiler_params=pltpu.CompilerParams(vmem_limit_bytes=VMEM_LIMIT))(*parts, *flat)


def _adamw_column_shards(parts, dev, states, row0s, *, name):
    _, rows, _ = parts.shape
    c = states[0][0].shape[1]

    def body(dev_ref, p_ref, *refs):
        del dev_ref
        st, outs = refs[:3 * len(states)], refs[3 * len(states):]
        for j, r0 in enumerate(row0s):
            w_ref = st[3 * j]
            g = _sum_devices(p_ref, slice(r0, r0 + w_ref.shape[0]))
            delta, m2, v2 = _adamw_math(w_ref[...], g, st[3 * j + 1][...], st[3 * j + 2][...])
            for o, val in zip(outs[4 * j:4 * j + 4], (g, delta, m2, v2)):
                o[...] = val

    whole = lambda a: pl.BlockSpec(a.shape, lambda i, dev_ref: (0, 0))
    flat = [a for wmv in states for a in wmv]
    outs = [w for w, _, _ in states for _ in range(4)]
    gs = pltpu.PrefetchScalarGridSpec(
        num_scalar_prefetch=1, grid=(1,),
        in_specs=[pl.BlockSpec((N_DEV, rows, c), lambda i, dev_ref: (0, 0, dev_ref[0]))] + [whole(a) for a in flat],
        out_specs=[whole(a) for a in outs])
    return pl.pallas_call(
        body, name=name, grid_spec=gs, out_shape=[jax.ShapeDtypeStruct(a.shape, F32) for a in outs],
        compiler_params=_cparams(("arbitrary",)))(dev, parts, *flat)


def _pad_rows(a, rows):
    return jnp.pad(a, ((0, rows - a.shape[0]), (0, 0)))


def _unblock_cols(g):
    return jnp.transpose(g, (1, 0, 2)).reshape(g.shape[1], N_DEV * g.shape[2])


def kernel(x, mem, norm_mix, w_in, b_gate, conv_w, conv_b, conv_ln_g, conv_ln_b, w_conv_out, sgu_ln_g, sgu_ln_b, sgu_w, sgu_b, w_sgu_out, w_mix_out, norm_xattn, norm_mem, w_q, w_kv, w_xo, norm_ffn, w_gu, w_down, norm_final, loss_target, m_norm_mix, m_w_in, m_b_gate, m_conv_w, m_conv_b, m_conv_ln_g, m_conv_ln_b, m_w_conv_out, m_sgu_ln_g, m_sgu_ln_b, m_sgu_w, m_sgu_b, m_w_sgu_out, m_w_mix_out, m_norm_xattn, m_norm_mem, m_w_q, m_w_kv, m_w_xo, m_norm_ffn, m_w_gu, m_w_down, m_norm_final, v_norm_mix, v_w_in, v_b_gate, v_conv_w, v_conv_b, v_conv_ln_g, v_conv_ln_b, v_w_conv_out, v_sgu_ln_g, v_sgu_ln_b, v_sgu_w, v_sgu_b, v_w_sgu_out, v_w_mix_out, v_norm_xattn, v_norm_mem, v_w_q, v_w_kv, v_w_xo, v_norm_ffn, v_w_gu, v_w_down, v_norm_final):
    given = dict(locals())
    bl, s, d = x.shape
    t = bl * s
    xf = x.reshape(t, d)
    tgt = loss_target.reshape(t, d)
    memf = mem.reshape(bl * mem.shape[1], d)
    cx, cy, cc = lax.axis_index("x"), lax.axis_index("y"), lax.axis_index("c")
    dev = 4 * cx + 2 * cy + cc
    dev_id = dev.astype(jnp.int32).reshape(1)
    col_sharded = ["w_in", "w_kv"]
    transposed = ["w_gu"]

    def shard_of(name, prefix=""):
        a = given[prefix + name][0]
        return jnp.transpose(a) if name in transposed else a

    def full_weight(name, blocks):
        return _unblock_cols(blocks) if name in col_sharded else blocks.reshape(N_DEV * blocks.shape[1], blocks.shape[2])

    g_bg, g_cw = _all_gather([_pad_rows(b_gate[0], 8), _pad_rows(conv_w[0], CONV_HALO)], name="gather_small_params")
    h1, p, w_in_blocks = _in_proj_gather(xf, norm_mix + g_bg[0, 7:8, 0:1], w_in[0].astype(BF16), name="in_proj")
    early = ["w_conv_out", "w_sgu_out", "w_mix_out", "w_q", "w_kv", "w_xo"]
    late = ["w_gu", "w_down"]
    shards = {n: shard_of(n).astype(BF16) for n in early + late}
    started = {}
    for grp, names in (("early", early), ("late", late)):
        srcs = [shards[n] for n in names]
        started[grp] = _copy_start("gather", srcs, [(N_DEV, *a.shape) for a in srcs], name=f"gather_{grp}_start", after=p)
    token = started["early"][4][0:1, 0:1] + started["late"][4][0:1, 0:1]
    wfull = {}
    bg_full = _unblock_cols(g_bg)
    cw_full = _unblock_cols(g_cw)

    def finish_gather(grp, names, after):
        ssem, rsem, srcs, lands, _ = started[grp]
        _, lands = _copy_wait("gather", ssem, rsem, srcs, lands, after, name=f"gather_{grp}_wait")
        for n, land in zip(names, lands):
            wfull[n] = full_weight(n, lax.dynamic_update_index_in_dim(land, shards[n], dev, 0))

    tri = jnp.tril(jnp.ones((SGU_CHUNK, SGU_CHUNK), bool))
    wm32 = jnp.where(tri[None], sgu_w[0], 0.0)
    wm = wm32.astype(BF16)
    wmt = jnp.transpose(wm32, (0, 2, 1)).astype(BF16)
    sgu_bias = jnp.broadcast_to(sgu_b[0][:, :, None], (SGU_GROUPS, SGU_CHUNK, d // SGU_GROUPS))

    c_conv, a_act = _conv_fwd(p, cw_full, conv_b + token, conv_ln_g, conv_ln_b, bl=bl, s=s, name="conv_fwd")
    sg, vn = _sgu_fwd(p, wm, sgu_bias, sgu_ln_g, sgu_ln_b + token, name="sgu_fwd")
    finish_gather("early", early, a_act[0:16, 0:128] + sg[0:16, 0:128])
    y_a, y_b, merged, x1, h2, q = _mix_out(p, a_act, sg, wfull["w_conv_out"], wfull["w_sgu_out"], bg_full, xf,
                                           wfull["w_mix_out"], norm_xattn, wfull["w_q"], name="mix_out")
    mem_n = _rms_fwd(memf, norm_mem, name="rms_mem")
    kv = _matmul(mem_n, wfull["w_kv"], mode="nn", out_dtype=BF16, name="mm_kv", tm=1024, tn=1024, tk=1024)
    o, x2, h3 = _attn_fwd(q, kv, x1, wfull["w_xo"], norm_ffn, bl=bl, s=s, name="attn_fwd")
    finish_gather("late", late, h3)
    gu, act, dx3, loss_part, d_norm_final = _ffn_fwd(h3, x2, tgt, wfull["w_gu"], wfull["w_down"],
                                                     norm_final.reshape(1, d), name="ffn_fwd")

    grads = {}
    sent = []

    def send_grads(names, tag, after=None):
        blocks, land_shapes = [], []
        for n in names:
            g = grads[n]
            if g.ndim == 2 and n in col_sharded:
                land_shapes.append((N_PEERS, g.shape[0], g.shape[1] // N_DEV))
            else:
                if g.ndim == 2:
                    g = g.reshape(N_DEV, -1, g.shape[1])
                land_shapes.append((N_PEERS, *g.shape[1:]))
            blocks.append(g)
        ssem, rsem, srcs, lands, tok = _copy_start("scatter", blocks, land_shapes, name=f"grads_{tag}_start", after=after)
        sent.append((names, ssem, rsem, srcs, lands))
        return tok[0:1, 0:1]

    dgu, dx2, do, d_norm_ffn = _ffn_bwd(dx3, gu, x2, wfull["w_down"], wfull["w_gu"], norm_ffn, wfull["w_xo"], name="ffn_bwd")
    grads["w_down"] = _matmul(act, dx3, mode="tn", out_dtype=BF16, name="mm_dw_down", tm=1408, tn=1024, tk=2048)
    grads["w_gu"] = _matmul(dgu, h3, mode="tn", out_dtype=BF16, name="mm_dw_gu", tm=1408, tn=1024, tk=2048)
    tok = send_grads(["w_down", "w_gu"], "ffn")
    dq, dkv = _attn_bwd(q, kv, do, bl=bl, s=s, name="attn_bwd")
    grads["w_kv"] = _matmul(mem_n, dkv, mode="tn", out_dtype=BF16, name="mm_dw_kv", tm=1024, tn=256, tk=1024,
                            col_blocks=N_DEV)
    tok2 = send_grads(["w_kv"], "attn")
    dmem_n = _matmul(dkv, wfull["w_kv"], mode="nt", out_dtype=F32, name="mm_d_mem", tm=512, tn=1024, tk=2048)
    d_norm_mem = _rms_gain_grad(dmem_n, memf, name="rms_mem_bwd")
    dx1, d_norm_xattn, dw_q, dw_xo = _proj_rms_bwd(dq, dx2, x1, wfull["w_q"], norm_xattn + (tok + tok2), name="q_rms_bwd",
                                                   h=h2, h_res=o)
    dp, dy_a, dy_b, d_b_gate, dw_mix, dw_in_gates = _gates_bwd_fused(dx1, p, y_a, y_b, bg_full, wfull["w_mix_out"],
                                                                    merged, h1, name="gates_bwd")
    grads["w_xo"] = dw_xo
    grads["w_q"] = dw_q.astype(BF16)
    grads["w_mix_out"] = dw_mix.astype(BF16)
    grads["w_sgu_out"] = _matmul(sg, dy_b, mode="tn", out_dtype=BF16, name="mm_dw_sgu", tm=1024, tn=1024, tk=2048)
    dc, d_conv_ln_g, d_conv_ln_b, dw_conv = _conv_ln_bwd_fused(dy_a, c_conv, a_act, wfull["w_conv_out"], conv_ln_g,
                                                               conv_ln_b, name="conv_ln_bwd")
    grads["w_conv_out"] = dw_conv.astype(BF16)
    tok = send_grads(["w_xo", "w_q", "w_mix_out", "w_sgu_out", "w_conv_out"], "mixer")
    dp, d_sgu_w, d_sgu_b, d_sgu_ln_g, d_sgu_ln_b = _sgu_bwd(dp, dy_b, wfull["w_sgu_out"], p, vn, wm, wmt, sgu_bias,
                                                             sgu_ln_g + tok, name="sgu_bwd")
    sgw_ssem, sgw_rsem, sgw_src, sgw_land, tok = _copy_start("gather", [d_sgu_w], [(N_DEV, *d_sgu_w.shape)],
                                                             name="gather_sgu_w_start")
    cw_full = cw_full + tok[0:1, 0:1]
    dw_in = _matmul(h1, dp, mode="tn", out_dtype=BF16, name="mm_dw_in_sgu", tm=1024, tn=1024, tk=2048,
                    b_cols=(2 * d, 2 * d), out_into=(dw_in_gates, 2 * d))
    dp, d_conv_w, d_conv_b, dw_in = _conv_bwd(dp, dc, p, cw_full, h1, dw_in, bl=bl, s=s, name="conv_bwd")
    grads["w_in"] = dw_in
    tok = send_grads(["w_in"], "in")
    grad_x, d_norm_mix = _proj_rms_bwd(dp, dx1, xf, w_in_blocks, norm_mix + tok, name="in_proj_bwd")
    out = {}

    vec_names = ["norm_mix", "conv_b", "conv_ln_g", "conv_ln_b", "sgu_ln_g", "sgu_ln_b", "norm_xattn", "norm_mem",
                 "norm_ffn", "norm_final"]
    vec_grads = [d_norm_mix, d_conv_b, d_conv_ln_g, d_conv_ln_b, d_sgu_ln_g, d_sgu_ln_b, d_norm_xattn, d_norm_mem,
                 d_norm_ffn, d_norm_final]
    n_vec = len(vec_names)
    small_vec = jnp.concatenate([g.reshape(1, d) for g in vec_grads]
                                + [jnp.broadcast_to(loss_part, (1, d)), jnp.zeros((16 - n_vec - 1, d), F32)], axis=0)
    small_cols = jnp.concatenate([d_b_gate, d_conv_w], axis=0)
    parts_vec, parts_sb, parts_cols = _all_gather([small_vec, d_sgu_b, small_cols], name="gather_small_grads")
    _, sgw_land = _copy_wait("gather", sgw_ssem, sgw_rsem, sgw_src, sgw_land, parts_vec, name="gather_sgu_w_wait")
    parts_sw = lax.dynamic_update_index_in_dim(sgw_land[0], d_sgu_w, dev, 0)
    rep_names = vec_names + ["sgu_b", "sgu_w"]
    rep_shapes = [(1, d)] * n_vec + [d_sgu_b.shape, d_sgu_w.shape]
    states = [tuple(given[pre + n].reshape(shape) for pre in ("", "m_", "v_")) for n, shape in zip(rep_names, rep_shapes)]
    res_rep = _adamw_replicated([parts_vec, parts_sb, parts_sw], states, n_vec, name="adamw_small")
    for i, n in enumerate(rep_names):
        out[n] = [r.reshape(given[n].shape) for r in res_rep[1 + 4 * i:5 + 4 * i]]
    res_cols = _adamw_column_shards(parts_cols, dev_id, [(b_gate[0], m_b_gate[0], v_b_gate[0]),
                                                        (conv_w[0], m_conv_w[0], v_conv_w[0])], (0, 8),
                                    name="adamw_small_cols")
    out["b_gate"] = [r[None] for r in res_cols[0:4]]
    out["conv_w"] = [r[None] for r in res_cols[4:8]]

    done = res_rep[1]
    for names, ssem, rsem, srcs, lands in sent:
        srcs, lands = _copy_wait("scatter", ssem, rsem, srcs, lands, done, name=f"grads_{names[0]}_wait")
        for n, partials, landed in zip(names, srcs, lands):
            res = _adamw_shard(partials, landed, dev_id, shard_of(n), shard_of(n, "m_"), shard_of(n, "v_"),
                               name=f"adamw_{n}")
            done = res[0]
            out[n] = [(jnp.transpose(r) if n in transposed else r)[None] for r in res]

    order = ["norm_mix", "w_in", "b_gate", "conv_w", "conv_b", "conv_ln_g", "conv_ln_b", "w_conv_out", "sgu_ln_g",
             "sgu_ln_b", "sgu_w", "sgu_b", "w_sgu_out", "w_mix_out", "norm_xattn", "norm_mem", "w_q", "w_kv", "w_xo",
             "norm_ffn", "w_gu", "w_down", "norm_final"]
    loss = res_rep[0][0, 0]
    return (loss, grad_x.reshape(x.shape), *[out[n][0] for n in order], *[out[n][1] for n in order],
            *[out[n][2] for n in order], *[out[n][3] for n in order])
```

```python
import jax
import jax.numpy as jnp
from jax import lax
from jax.experimental import pallas as pl
from jax.experimental.pallas import tpu as pltpu

F32 = jnp.float32
BF16 = jnp.bfloat16
RMS_EPS = 1e-6
LN_EPS = 1e-5
CONV_WIDTH = 31
CONV_HALO = 32
CONV_ROWS = 128
CONV_COLS = 256
LANES = 128
SGU_CHUNK = 128
SGU_GROUPS = 8
SGU_TILE = 512
HEADS = 4
N_DEV = 8
ADAM_LR, ADAM_B1, ADAM_B2, ADAM_EPS, ADAM_WD, ADAM_STEP = 0.001, 0.9, 0.999, 1e-08, 0.01, 10
VMEM_LIMIT = 56 * 1024 * 1024
TOKEN_TILE = 256
ATTN_TILE = 1024
MESH_ID = pl.DeviceIdType.MESH

_GELU_K = 0.7978845608028654
_GELU_C = 0.044715


def _cparams(sem=None):
    return pltpu.CompilerParams(dimension_semantics=sem, vmem_limit_bytes=VMEM_LIMIT)


def _sigmoid(v):
    return 0.5 * jnp.tanh(0.5 * v) + 0.5


def _gelu(v):
    return 0.5 * v * (1.0 + jnp.tanh(_GELU_K * (v + _GELU_C * v * v * v)))


def _gelu_grad(v):
    th = jnp.tanh(_GELU_K * (v + _GELU_C * v * v * v))
    return 0.5 * (1.0 + th) + 0.5 * v * (1.0 - th * th) * _GELU_K * (1.0 + 3.0 * _GELU_C * v * v)


def _dot(a, b, dims):
    return lax.dot_general(a, b, (dims, ((), ())), preferred_element_type=F32)


_NN = ((1,), (0,))
_NT = ((1,), (1,))
_TN = ((0,), (0,))


def _matmul(a, b, *, mode, out_dtype, name, tm=512, tn=512, tk=512, col_blocks=None, b_cols=None, out_into=None):
    if mode == "nn":
        (m, k), (_, n) = a.shape, b.shape
    elif mode == "nt":
        (m, k), (n, _) = a.shape, b.shape
    else:
        (k, m), (_, n) = a.shape, b.shape
    b_first = 0
    if b_cols is not None:
        assert mode == "tn"
        b_first, n = b_cols
    tm, tn, tk = min(tm, m), min(tn, n), min(tk, k)
    assert b_first % tn == 0
    b_first //= tn
    assert m % tm == 0 and n % tn == 0 and k % tk == 0, (name, a.shape, b.shape, tm, tn, tk)
    nk = k // tk
    dims = {"nn": _NN, "nt": _NT, "tn": _TN}[mode]

    def body(*refs):
        a_ref, b_ref = refs[:2]
        o_ref = refs[3] if out_into is not None else refs[2]
        part = _dot(a_ref[...].astype(BF16), b_ref[...].astype(BF16), dims)
        if nk == 1:
            o_ref[...] = part.astype(out_dtype)
        else:
            acc_ref = refs[-1]
            kk = pl.program_id(2)

            @pl.when(kk == 0)
            def _():
                acc_ref[...] = part

            @pl.when(kk > 0)
            def _():
                acc_ref[...] += part

            @pl.when(kk == nk - 1)
            def _():
                o_ref[...] = acc_ref[...].astype(out_dtype)

    resident = dict(pipeline_mode=pl.Buffered(1)) if (n == tn and nk == 1 and mode != "tn" and m > tm) else {}
    if mode == "nn":
        a_spec = pl.BlockSpec((tm, tk), lambda i, j, kk: (i, kk))
        b_spec = pl.BlockSpec((tk, tn), lambda i, j, kk: (kk, j), **resident)
    elif mode == "nt":
        a_spec = pl.BlockSpec((tm, tk), lambda i, j, kk: (i, kk))
        b_spec = pl.BlockSpec((tn, tk), lambda i, j, kk: (j, kk), **resident)
    else:
        a_spec = pl.BlockSpec((tk, tm), lambda i, j, kk: (kk, i))
        b_spec = pl.BlockSpec((tk, tn), lambda i, j, kk: (kk, j + b_first))
    in_specs, args = [a_spec, b_spec], [a, b]
    out_shape = [jax.ShapeDtypeStruct((m, n), out_dtype)]
    out_specs = [pl.BlockSpec((tm, tn), lambda i, j, kk: (i, j))]
    if col_blocks is not None:
        assert (n // col_blocks) % tn == 0
        per = n // col_blocks // tn
        out_shape = [jax.ShapeDtypeStruct((col_blocks, m, n // col_blocks), out_dtype)]
        out_specs = [pl.BlockSpec((None, tm, tn), lambda i, j, kk: (j // per, i, j % per))]
    aliases = {}
    if out_into is not None:
        target, first = out_into
        assert col_blocks is None and first % tn == 0 and target.dtype == out_dtype
        in_specs.append(pl.BlockSpec(memory_space=pl.ANY))
        args.append(target)
        aliases = {len(args) - 1: 0}
        out_shape = [jax.ShapeDtypeStruct(target.shape, target.dtype)]
        out_specs = [pl.BlockSpec((tm, tn), lambda i, j, kk: (i, j + first // tn))]
    res = pl.pallas_call(
        body, name=name, grid=(m // tm, n // tn, nk), in_specs=in_specs, out_specs=out_specs, out_shape=out_shape,
        scratch_shapes=[pltpu.VMEM((tm, tn), F32)] if nk > 1 else [], input_output_aliases=aliases,
        compiler_params=_cparams(("parallel", "parallel", "arbitrary")),
    )(*args)
    return res[0]


def _row_call(name, t, tm, rows_in, residents, rows_out, accs, body):
    n_in, n_res, n_out, n_acc = len(rows_in), len(residents), len(rows_out), len(accs)
    steps = t // tm
    assert t % tm == 0
    narrow = [i for i, a in enumerate(accs) if a[1] != F32]

    def kernel_body(*refs):
        in_refs, res_refs = refs[:n_in], refs[n_in:n_in + n_res]
        out_refs = refs[n_in + n_res:n_in + n_res + n_out]
        acc_out = list(refs[n_in + n_res + n_out:n_in + n_res + n_out + n_acc])
        scratch = refs[n_in + n_res + n_out + n_acc:]
        acc_refs = list(acc_out)
        for s_ref, i in zip(scratch, narrow):
            acc_refs[i] = s_ref
        if accs:
            @pl.when(pl.program_id(0) == 0)
            def _():
                for acc in acc_refs:
                    acc[...] = jnp.zeros_like(acc)
        body(in_refs, res_refs, out_refs, acc_refs)
        if narrow:
            @pl.when(pl.program_id(0) == steps - 1)
            def _():
                for i in narrow:
                    acc_out[i][...] = acc_refs[i][...].astype(acc_out[i].dtype)

    once = dict(pipeline_mode=pl.Buffered(1)) if steps > 1 else {}
    in_specs = [pl.BlockSpec((tm, cols), lambda i, cb=cb: (i, cb)) for _, cols, cb in rows_in]
    in_specs += [pl.BlockSpec(r.shape, lambda i, nd=r.ndim: (0,) * nd, **once) for r in residents]
    out_specs = [pl.BlockSpec((tm, cols), lambda i, cb=cb: (i, cb)) for _, cols, cb, _ in rows_out]
    out_specs += [pl.BlockSpec(a[0], lambda i, nd=len(a[0]), cb=(a[3] if len(a) == 4 else 0): (0,) * (nd - 1) + (cb,))
                  for a in accs]
    out_shape = [jax.ShapeDtypeStruct((t, total), dt) for total, _, _, dt in rows_out]
    out_shape += [jax.ShapeDtypeStruct((a[0][0], a[2]) if len(a) == 4 else a[0], a[1]) for a in accs]
    return pl.pallas_call(
        kernel_body, name=name, grid=(steps,), in_specs=in_specs, out_specs=out_specs, out_shape=out_shape,
        scratch_shapes=[pltpu.VMEM(accs[i][0], F32) for i in narrow],
        compiler_params=_cparams(("arbitrary",) if accs else ("parallel",)),
    )(*[a for a, _, _ in rows_in], *residents)


def _rms_apply(xv, gain):
    return xv * lax.rsqrt(jnp.mean(xv * xv, axis=-1, keepdims=True) + RMS_EPS) * gain


def _rms_grad(dres, dh, xv, gain):
    r = lax.rsqrt(jnp.mean(xv * xv, axis=-1, keepdims=True) + RMS_EPS)
    xhat = xv * r
    dxh = dh * gain
    dx = dres + r * (dxh - xhat * jnp.mean(dxh * xhat, axis=-1, keepdims=True))
    return dx, jnp.sum(dh * xhat, axis=0, keepdims=True)


def _in_proj_gather(xf, gain, w_shard, *, name):
    t, d = xf.shape
    cb = w_shard.shape[1]
    tm = min(1024, t)
    steps = t // tm
    mx, my, _ = _mesh_pos()
    order = jnp.stack([2 * mx + my, 2 * (1 - mx) + my, 2 * mx + (1 - my), 2 * (1 - mx) + (1 - my)]).astype(jnp.int32)

    def body(order_ref, x_ref, g_ref, ws_ref, h_ref, p_ref, wout_ref, w_ref, send_sems, recv_sems, own_sem):
        ps, i = pl.program_id(0), pl.program_id(1)
        x, y, c = _mesh_pos()
        me, sib = (x, y, c), (x, y, 1 - c)
        chips = [(1 - x, y), (x, 1 - y), (1 - x, 1 - y)]

        def copy(k, block, to, from_shard=False):
            return pltpu.make_async_remote_copy(
                src_ref=ws_ref if from_shard else w_ref.at[_dev_index(block)], dst_ref=w_ref.at[_dev_index(block)],
                send_sem=send_sems.at[k], recv_sem=recv_sems.at[k], device_id=to, device_id_type=MESH_ID)

        own = pltpu.make_async_copy(ws_ref, w_ref.at[_dev_index(me)], own_sem)
        first = [copy(0, me, sib, True)] + [copy(1 + j, me, (*chip, c), True) for j, chip in enumerate(chips)]
        passed = [copy(4 + j, (*chip, c), sib) for j, chip in enumerate(chips)]

        @pl.when(jnp.logical_and(ps == 0, i == 0))
        def _():
            own.start()
            for cp in first:
                cp.start()
            own.wait()
            copy(0, sib, me).wait_recv()

        for j, chip in enumerate(chips):
            @pl.when(jnp.logical_and(ps == j + 1, i == 0))
            def _(j=j, chip=chip):
                copy(1 + j, (*chip, c), me).wait_recv()
                passed[j].start()
                copy(4 + j, (*chip, 1 - c), me).wait_recv()

        h = _rms_apply(x_ref[...], g_ref[...]).astype(BF16)
        h_ref[...] = h
        chip_id = order_ref[ps]
        p_ref[:, 0:cb] = _dot(h, w_ref[2 * chip_id], _NN).astype(BF16)
        p_ref[:, cb:2 * cb] = _dot(h, w_ref[2 * chip_id + 1], _NN).astype(BF16)

        @pl.when(jnp.logical_and(ps == 3, i == steps - 1))
        def _():
            for cp in first + passed:
                cp.wait_send()
            keep = pltpu.make_async_copy(w_ref, wout_ref, own_sem)
            keep.start()
            keep.wait()

    gs = pltpu.PrefetchScalarGridSpec(
        num_scalar_prefetch=1, grid=(4, steps),
        in_specs=[pl.BlockSpec((tm, d), lambda ps, i, o: (i, 0)), pl.BlockSpec((1, d), lambda ps, i, o: (0, 0)),
                  pl.BlockSpec(memory_space=pl.ANY)],
        out_specs=[pl.BlockSpec((tm, d), lambda ps, i, o: (jnp.where(ps == 0, i, steps - 1), 0)),
                   pl.BlockSpec((tm, 2 * cb), lambda ps, i, o: (i, o[ps])), pl.BlockSpec(memory_space=pl.ANY)],
        scratch_shapes=[pltpu.VMEM((N_DEV, d, cb), BF16), pltpu.SemaphoreType.DMA((7,)), pltpu.SemaphoreType.DMA((7,)),
                        pltpu.SemaphoreType.DMA(())])
    return pl.pallas_call(
        body, name=name, grid_spec=gs,
        out_shape=[jax.ShapeDtypeStruct((t, d), BF16), jax.ShapeDtypeStruct((t, N_DEV * cb), BF16),
                   jax.ShapeDtypeStruct((N_DEV, d, cb), BF16)],
        compiler_params=_cparams(("arbitrary", "arbitrary")))(order, xf, gain, w_shard)


def _mix_out(p, a_act, sg, w_conv_out, w_sgu_out, b_gate, xf, w_mix, gain, w_q, *, name):
    t, d = xf.shape

    def body(ins, res, outs, accs):
        ga_ref, gb_ref, act_ref, sg_ref, x_ref = ins
        bg_ref, wm_ref, g_ref, wq_ref, wa_ref, wb_ref = res
        ya_ref, yb_ref, m_ref, x1_ref, h_ref, q_ref = outs
        y_a = _dot(act_ref[...], wa_ref[...], _NN).astype(BF16)
        y_b = _dot(sg_ref[...], wb_ref[...], _NN).astype(BF16)
        ya_ref[...] = y_a
        yb_ref[...] = y_b
        sa = _sigmoid(ga_ref[...].astype(F32) + bg_ref[0:1, :])
        sb = _sigmoid(gb_ref[...].astype(F32) + bg_ref[1:2, :])
        merged = (sa * y_a.astype(F32) + sb * y_b.astype(F32)).astype(BF16)
        m_ref[...] = merged
        x1 = x_ref[...] + _dot(merged, wm_ref[...], _NN)
        x1_ref[...] = x1
        h = _rms_apply(x1, g_ref[...]).astype(BF16)
        h_ref[...] = h
        q_ref[...] = _dot(h, wq_ref[...], _NN).astype(BF16)

    bf = (d, d, 0, BF16)
    return _row_call(name, t, min(512, t), [(p, d, 4), (p, d, 5), (a_act, d, 0), (sg, d, 0), (xf, d, 0)],
                     [b_gate, w_mix, gain, w_q, w_conv_out, w_sgu_out], [bf, bf, bf, (d, d, 0, F32), bf, bf], [], body)


def _ffn_fwd(h3, x2, target, w_gu_t, w_down, gain, *, name):
    t, d = x2.shape
    f2 = w_gu_t.shape[0]
    f = f2 // 2
    half = f // 2

    def body(ins, res, outs, accs):
        h_ref, x2_ref, t_ref = ins
        wgu_ref, wd_ref, g_ref = res
        gu_ref, act_ref, dx_ref = outs
        loss_ref, dg_ref = accs
        h = h_ref[...]
        x3 = x2_ref[...]
        for c0 in (0, half):
            gt = _dot(h, wgu_ref[c0:c0 + half, :], _NT).astype(BF16)
            up = _dot(h, wgu_ref[f + c0:f + c0 + half, :], _NT).astype(BF16)
            gu_ref[:, c0:c0 + half] = gt
            gu_ref[:, f + c0:f + c0 + half] = up
            gtf = gt.astype(F32)
            act = (gtf * _sigmoid(gtf) * up.astype(F32)).astype(BF16)
            act_ref[:, c0:c0 + half] = act
            x3 = x3 + _dot(act, wd_ref[c0:c0 + half, :], _NN)
        g = g_ref[...]
        r = lax.rsqrt(jnp.mean(x3 * x3, axis=-1, keepdims=True) + RMS_EPS)
        xhat = x3 * r
        err = xhat * g - t_ref[...]
        loss_ref[...] += 0.5 * jnp.sum(jnp.mean(err * err, axis=-1, keepdims=True), axis=0, keepdims=True)
        dy = err * (1.0 / d)
        dg_ref[...] += jnp.sum(dy * xhat, axis=0, keepdims=True)
        dxh = dy * g
        dx_ref[...] = r * (dxh - xhat * jnp.mean(dxh * xhat, axis=-1, keepdims=True))

    return _row_call(name, t, min(256, t), [(h3, d, 0), (x2, d, 0), (target, d, 0)], [w_gu_t, w_down, gain],
                     [(f2, f2, 0, BF16), (f, f, 0, BF16), (d, d, 0, F32)], [((1, 1), F32), ((1, d), F32)], body)


def _ffn_bwd(dx3, gu, x2, w_down, w_gu_t, gain, w_xo, *, name):
    t, d = x2.shape
    f2 = w_gu_t.shape[0]
    f = f2 // 2
    half = f // 2

    def body(ins, res, outs, accs):
        dx3_ref, gu_ref, x2_ref = ins
        wd_ref, wgu_ref, g_ref, wxo_ref = res
        dgu_ref, dx2_ref, do_ref = outs
        (dg_ref,) = accs
        dx3v = dx3_ref[...]
        dxb = dx3v.astype(BF16)
        dh = jnp.zeros(dx3v.shape, F32)
        for c0 in (0, half):
            dact = _dot(dxb, wd_ref[c0:c0 + half, :], _NT)
            gt = gu_ref[:, c0:c0 + half].astype(F32)
            up = gu_ref[:, f + c0:f + c0 + half].astype(F32)
            sg = _sigmoid(gt)
            dgt = (dact * up * sg * (1.0 + gt * (1.0 - sg))).astype(BF16)
            dup = (dact * gt * sg).astype(BF16)
            dgu_ref[:, c0:c0 + half] = dgt
            dgu_ref[:, f + c0:f + c0 + half] = dup
            dh = dh + _dot(dgt, wgu_ref[c0:c0 + half, :], _NN) + _dot(dup, wgu_ref[f + c0:f + c0 + half, :], _NN)
        dx2, dg = _rms_grad(dx3v, dh, x2_ref[...], g_ref[...])
        dx2_ref[...] = dx2
        dg_ref[...] += dg
        do_ref[...] = _dot(dx2.astype(BF16), wxo_ref[...], _NT).astype(BF16)

    return _row_call(name, t, min(256, t), [(dx3, d, 0), (gu, f2, 0), (x2, d, 0)], [w_down, w_gu_t, gain, w_xo],
                     [(f2, f2, 0, BF16), (d, d, 0, F32), (d, d, 0, BF16)], [((1, d), F32)], body)


def _proj_rms_bwd(dy, dres, x, w, gain, *, name, h=None, h_res=None):
    t, d = x.shape
    k = dy.shape[1]

    def body(ins, res, outs, accs):
        dy_ref, dres_ref, x_ref = ins[:3]
        w_ref, g_ref = res
        if h is not None:
            accs[1][...] += _dot(ins[3][...], dy_ref[...], _TN)
        if h_res is not None:
            accs[-1][...] += _dot(ins[-1][...], dres_ref[...].astype(BF16), _TN)
        if w.ndim == 3:
            cb = w.shape[2]
            dh = _dot(dy_ref[:, 0:cb], w_ref[0], _NT)
            for j in range(1, w.shape[0]):
                dh = dh + _dot(dy_ref[:, j * cb:(j + 1) * cb], w_ref[j], _NT)
        else:
            dh = _dot(dy_ref[...], w_ref[...], _NT)
        dx, dg = _rms_grad(dres_ref[...], dh, x_ref[...], g_ref[...])
        outs[0][...] = dx
        accs[0][...] += dg

    rows_in = [(dy, k, 0), (dres, d, 0), (x, d, 0)] + [(a, d, 0) for a in (h, h_res) if a is not None]
    accs = [((1, d), F32)] + ([((d, k), BF16)] if h is not None else []) + ([((d, d), BF16)] if h_res is not None else [])
    tm = 1024 if (w.ndim == 2 and h_res is None) else 512
    return _row_call(name, t, min(tm, t), rows_in, [w, gain], [(d, d, 0, F32)], accs, body)


def _gates_bwd_fused(dx1, p, y_a, y_b, b_gate, w_mix, merged, h1, *, name):
    t, d = y_a.shape

    def body(ins, res, outs, accs):
        dx_ref, ga_ref, gb_ref, ya_ref, yb_ref, m_ref, h1_ref = ins
        bg_ref, wm_ref = res
        dp_ref, dya_ref, dyb_ref = outs
        dbg_ref, dwm_ref, dwin_ref = accs
        dxb = dx_ref[...].astype(BF16)
        dwm_ref[...] += _dot(m_ref[...], dxb, _TN)
        dm = _dot(dxb, wm_ref[...], _NT)
        sa = _sigmoid(ga_ref[...].astype(F32) + bg_ref[0:1, :])
        sb = _sigmoid(gb_ref[...].astype(F32) + bg_ref[1:2, :])
        dya_ref[...] = (dm * sa).astype(BF16)
        dyb_ref[...] = (dm * sb).astype(BF16)
        dga = dm * ya_ref[...].astype(F32) * sa * (1.0 - sa)
        dgb = dm * yb_ref[...].astype(F32) * sb * (1.0 - sb)
        dp_ref[:, 0:d] = dga.astype(BF16)
        dp_ref[:, d:2 * d] = dgb.astype(BF16)
        dbg_ref[0:1, :] += jnp.sum(dga, axis=0, keepdims=True)
        dbg_ref[1:2, :] += jnp.sum(dgb, axis=0, keepdims=True)
        dwin_ref[...] += _dot(h1_ref[...], dp_ref[...], _TN)

    return _row_call(name, t, min(256, t),
                     [(dx1, d, 0), (p, d, 4), (p, d, 5), (y_a, d, 0), (y_b, d, 0), (merged, d, 0), (h1, d, 0)],
                     [b_gate, w_mix], [(p.shape[1], 2 * d, 2, BF16), (d, d, 0, BF16), (d, d, 0, BF16)],
                     [((8, d), F32), ((d, d), BF16), ((d, 2 * d), BF16, p.shape[1], 2)], body)


def _conv_ln_bwd_fused(dy_a, c, a_act, w_conv_out, ln_g, ln_b, *, name):
    t, d = c.shape

    def body(ins, res, outs, accs):
        dy_ref, c_ref, act_ref = ins
        w_ref, lg_ref, lb_ref = res
        dlg_ref, dlb_ref, dw_ref = accs
        dw_ref[...] += _dot(act_ref[...], dy_ref[...], _TN)
        dact = _dot(dy_ref[...], w_ref[...], _NT)
        cv = c_ref[...].astype(F32)
        g = lg_ref[...]
        mu = jnp.mean(cv, axis=-1, keepdims=True)
        dv = cv - mu
        rstd = lax.rsqrt(jnp.mean(dv * dv, axis=-1, keepdims=True) + LN_EPS)
        chat = dv * rstd
        aln = chat * g + lb_ref[...]
        sg = _sigmoid(aln)
        daln = dact * (sg * (1.0 + aln * (1.0 - sg)))
        dlb_ref[...] += jnp.sum(daln, axis=0, keepdims=True)
        dlg_ref[...] += jnp.sum(daln * chat, axis=0, keepdims=True)
        dchat = daln * g
        dc = rstd * (dchat - jnp.mean(dchat, axis=-1, keepdims=True)
                     - chat * jnp.mean(dchat * chat, axis=-1, keepdims=True))
        outs[0][...] = dc.astype(BF16)

    return _row_call(name, t, min(1024, t), [(dy_a, d, 0), (c, d, 0), (a_act, d, 0)], [w_conv_out, ln_g, ln_b],
                     [(d, d, 0, BF16)], [((1, d), F32), ((1, d), F32), ((d, d), BF16)], body)


def _row_spec(tt, cols, col_block=0):
    return pl.BlockSpec((tt, cols), lambda i: (i, col_block))


def _const_spec(shape):
    return pl.BlockSpec(shape, lambda *_: (0,) * len(shape))


def _rms_fwd(x, gain, *, name):
    t, d = x.shape
    tt = min(TOKEN_TILE, t)

    def body(x_ref, g_ref, h_ref):
        xv = x_ref[...]
        r = lax.rsqrt(jnp.mean(xv * xv, axis=-1, keepdims=True) + RMS_EPS)
        h_ref[...] = (xv * r * g_ref[...]).astype(BF16)

    return pl.pallas_call(
        body, name=name, grid=(t // tt,), in_specs=[_row_spec(tt, d), _const_spec((1, d))],
        out_specs=_row_spec(tt, d), out_shape=jax.ShapeDtypeStruct((t, d), BF16),
        compiler_params=_cparams(("parallel",)))(x, gain)


def _rms_gain_grad(dh, x, *, name):
    t, d = x.shape
    tt = min(TOKEN_TILE, t)

    def body(dh_ref, x_ref, dg_ref):
        @pl.when(pl.program_id(0) == 0)
        def _():
            dg_ref[...] = jnp.zeros_like(dg_ref)

        xv = x_ref[...]
        xhat = xv * lax.rsqrt(jnp.mean(xv * xv, axis=-1, keepdims=True) + RMS_EPS)
        dg_ref[...] += jnp.sum(dh_ref[...].astype(F32) * xhat, axis=0, keepdims=True)

    rs = _row_spec(tt, d)
    return pl.pallas_call(
        body, name=name, grid=(t // tt,), in_specs=[rs, rs], out_specs=_const_spec((1, d)),
        out_shape=jax.ShapeDtypeStruct((1, d), F32), compiler_params=_cparams(("arbitrary",)))(dh, x)


SUBLANES = 8
SHIFT_ROWS = 40


def _conv_apply(sbuf_ref, w_ref, out_ref, tt, offsets, bias_ref=None):
    d = out_ref.shape[1]
    for cc in range(d // LANES):
        cs = slice(cc * LANES, (cc + 1) * LANES)
        taps = [jnp.broadcast_to(w_ref[k:k + 1, cs], (SUBLANES, LANES)) for k in range(CONV_WIDTH)]
        bias = None if bias_ref is None else jnp.broadcast_to(bias_ref[:, cs], (SUBLANES, LANES))

        def row_body(r, carry, cs=cs, taps=taps, bias=bias):
            r0 = pl.multiple_of(r * CONV_ROWS, CONV_ROWS)
            for q in range(CONV_ROWS // SUBLANES):
                acc = _tap(sbuf_ref, r0 + q * SUBLANES, cs, offsets[0]) * taps[0]
                for k in range(1, CONV_WIDTH):
                    acc = acc + _tap(sbuf_ref, r0 + q * SUBLANES, cs, offsets[k]) * taps[k]
                if bias is not None:
                    acc = acc + bias
                out_ref[pl.ds(r0 + q * SUBLANES, SUBLANES), cs] = acc
            return carry

        lax.fori_loop(0, tt // CONV_ROWS, row_body, 0)


def _fill_shifts(sbuf_ref, rows):
    d = sbuf_ref.shape[2]
    assert rows % SHIFT_ROWS == 0

    def row_body(i, carry):
        r0 = pl.multiple_of(i * SHIFT_ROWS, SUBLANES)
        for cc in range(d // CONV_COLS):
            cs = slice(cc * CONV_COLS, (cc + 1) * CONV_COLS)
            win = sbuf_ref[0, pl.ds(r0, SHIFT_ROWS + SUBLANES), cs]
            for sh in range(1, SUBLANES):
                sbuf_ref[sh, pl.ds(r0, SHIFT_ROWS), cs] = win[sh:sh + SHIFT_ROWS, :]
        return carry

    lax.fori_loop(0, rows // SHIFT_ROWS, row_body, 0)


def _tap(sbuf_ref, r0, cs, offset):
    sh = offset % SUBLANES
    return sbuf_ref[sh, pl.ds(pl.multiple_of(r0 + (offset - sh), SUBLANES), SUBLANES), cs]


def _conv_specs(bl, s, tt, d, col_a, col_g):
    nj = s // tt
    per = tt // CONV_HALO
    main_a = pl.BlockSpec((tt, d), lambda b, j: (b * nj + j, col_a))
    main_g = pl.BlockSpec((tt, d), lambda b, j: (b * nj + j, col_g))
    prev = lambda b, j: jnp.maximum((b * nj + j) * per - 1, 0)
    halo_a = pl.BlockSpec((CONV_HALO, d), lambda b, j: (prev(b, j), col_a))
    halo_g = pl.BlockSpec((CONV_HALO, d), lambda b, j: (prev(b, j), col_g))
    return main_a, main_g, halo_a, halo_g


def _fill_glu(sbuf_ref, a_ref, g_ref, ha_ref, hg_ref, tt):
    first = pl.program_id(1) == 0
    ha = ha_ref[...].astype(F32)
    hg = hg_ref[...].astype(F32)
    sbuf_ref[0, pl.ds(0, CONV_HALO), :] = jnp.where(first, 0.0, ha * _sigmoid(hg))
    av = a_ref[...].astype(F32)
    gv = g_ref[...].astype(F32)
    sbuf_ref[0, pl.ds(CONV_HALO, tt), :] = av * _sigmoid(gv)
    _fill_shifts(sbuf_ref, tt + CONV_HALO - SUBLANES)


def _conv_fwd(p, conv_w, conv_b, ln_g, ln_b, *, bl, s, name):
    t = p.shape[0]
    d = conv_w.shape[1]
    tt = min(TOKEN_TILE, s)
    off = CONV_HALO - (CONV_WIDTH - 1)

    def body(a_ref, g_ref, ha_ref, hg_ref, w_ref, b_ref, lg_ref, lb_ref, c_ref, act_ref, sbuf_ref, cbuf_ref):
        _fill_glu(sbuf_ref, a_ref, g_ref, ha_ref, hg_ref, tt)

        _conv_apply(sbuf_ref, w_ref, cbuf_ref, tt, [off + k for k in range(CONV_WIDTH)], bias_ref=b_ref)
        cv = cbuf_ref[...]
        c_ref[...] = cv.astype(BF16)
        mu = jnp.mean(cv, axis=-1, keepdims=True)
        dv = cv - mu
        rstd = lax.rsqrt(jnp.mean(dv * dv, axis=-1, keepdims=True) + LN_EPS)
        aln = dv * rstd * lg_ref[...] + lb_ref[...]
        act_ref[...] = (aln * _sigmoid(aln)).astype(BF16)

    main_a, main_g, halo_a, halo_g = _conv_specs(bl, s, tt, d, 0, 1)
    out_spec = pl.BlockSpec((tt, d), lambda b, j: (b * (s // tt) + j, 0))
    return pl.pallas_call(
        body, name=name, grid=(bl, s // tt),
        in_specs=[main_a, main_g, halo_a, halo_g, _const_spec((CONV_HALO, d)), _const_spec((1, d)), _const_spec((1, d)),
                  _const_spec((1, d))],
        out_specs=[out_spec, out_spec],
        out_shape=[jax.ShapeDtypeStruct((t, d), BF16), jax.ShapeDtypeStruct((t, d), BF16)],
        scratch_shapes=[pltpu.VMEM((SUBLANES, tt + CONV_HALO, d), F32), pltpu.VMEM((tt, d), F32)],
        compiler_params=_cparams(("parallel", "parallel")))(p, p, p, p, conv_w, conv_b, ln_g, ln_b)


def _conv_bwd(dp, dc, p, conv_w, h1, dw_in, *, bl, s, name):
    t = p.shape[0]
    d = conv_w.shape[1]
    tt = min(TOKEN_TILE, s)
    nj = s // tt
    per = tt // CONV_HALO
    off = CONV_HALO - (CONV_WIDTH - 1)
    last_blk = t // CONV_HALO - 1

    def body(dp_in, dc_ref, dcn_ref, a_ref, g_ref, ha_ref, hg_ref, w_ref, h1_ref, dwin_in, dp_ref, dw_ref, db_ref,
             dwin_out, gbuf_ref, dbuf_ref, dglu_ref, acc_ref, dwin_ref):
        del dp_in, dwin_in
        b, j = pl.program_id(0), pl.program_id(1)
        start = jnp.logical_and(b == 0, j == 0)
        end = jnp.logical_and(b == bl - 1, j == nj - 1)

        @pl.when(start)
        def _():
            acc_ref[...] = jnp.zeros_like(acc_ref)
            db_ref[...] = jnp.zeros_like(db_ref)
            dwin_ref[...] = jnp.zeros_like(dwin_ref)

        _fill_glu(gbuf_ref, a_ref, g_ref, ha_ref, hg_ref, tt)
        dcv = dc_ref[...].astype(F32)
        dbuf_ref[0, pl.ds(0, tt), :] = dcv
        dbuf_ref[0, pl.ds(tt, CONV_HALO), :] = jnp.where(j == nj - 1, 0.0, dcn_ref[...].astype(F32))
        _fill_shifts(dbuf_ref, tt + CONV_HALO - SUBLANES)
        db_ref[...] += jnp.sum(dcv, axis=0, keepdims=True)

        for cc in range(d // LANES):
            cs = slice(cc * LANES, (cc + 1) * LANES)

            def row_body(r, accs, cs=cs):
                r0 = pl.multiple_of(r * CONV_ROWS, CONV_ROWS)
                accs = list(accs)
                for q in range(CONV_ROWS // SUBLANES):
                    dcw = dbuf_ref[0, pl.ds(r0 + q * SUBLANES, SUBLANES), cs]
                    for k in range(CONV_WIDTH):
                        accs[k] = accs[k] + dcw * _tap(gbuf_ref, r0 + q * SUBLANES, cs, off + k)
                return tuple(accs)

            zero = jnp.zeros((SUBLANES, LANES), F32)
            accs = lax.fori_loop(0, tt // CONV_ROWS, row_body, (zero,) * CONV_WIDTH)
            for k in range(CONV_WIDTH):
                acc_ref[k, :, cs] += accs[k]

        _conv_apply(dbuf_ref, w_ref, dglu_ref, tt, [CONV_WIDTH - 1 - k for k in range(CONV_WIDTH)])
        dglu = dglu_ref[...]
        av = a_ref[...].astype(F32)
        sg = _sigmoid(g_ref[...].astype(F32))
        dp_ref[:, 0:d] = (dglu * sg).astype(BF16)
        dp_ref[:, d:2 * d] = (dglu * av * sg * (1.0 - sg)).astype(BF16)
        dwin_ref[...] += _dot(h1_ref[...], dp_ref[...], _TN)

        @pl.when(end)
        def _():
            for k in range(CONV_WIDTH):
                dw_ref[k:k + 1, :] = jnp.sum(acc_ref[k], axis=0, keepdims=True)
            dw_ref[CONV_WIDTH:CONV_HALO, :] = jnp.zeros((CONV_HALO - CONV_WIDTH, d), F32)
            dwin_out[...] = dwin_ref[...].astype(dwin_out.dtype)

    main_a, main_g, halo_a, halo_g = _conv_specs(bl, s, tt, d, 0, 1)
    dc_main = pl.BlockSpec((tt, d), lambda b, j: (b * nj + j, 0))
    dc_next = pl.BlockSpec((CONV_HALO, d), lambda b, j: (jnp.minimum((b * nj + j + 1) * per, last_blk), 0))
    hbm = pl.BlockSpec(memory_space=pl.ANY)
    return pl.pallas_call(
        body, name=name, grid=(bl, nj),
        in_specs=[hbm, dc_main, dc_next, main_a, main_g, halo_a, halo_g, _const_spec((CONV_HALO, d)), dc_main, hbm],
        out_specs=[pl.BlockSpec((tt, 2 * d), lambda b, j: (b * nj + j, 0)), _const_spec((CONV_HALO, d)), _const_spec((1, d)),
                   _const_spec((d, 2 * d))],
        out_shape=[jax.ShapeDtypeStruct(dp.shape, dp.dtype), jax.ShapeDtypeStruct((CONV_HALO, d), F32),
                   jax.ShapeDtypeStruct((1, d), F32), jax.ShapeDtypeStruct(dw_in.shape, dw_in.dtype)],
        scratch_shapes=[pltpu.VMEM((SUBLANES, tt + CONV_HALO, d), F32), pltpu.VMEM((SUBLANES, tt + CONV_HALO, d), F32),
                        pltpu.VMEM((tt, d), F32), pltpu.VMEM((CONV_HALO, SUBLANES, d), F32), pltpu.VMEM((d, 2 * d), F32)],
        input_output_aliases={0: 0, 9: 3},
        compiler_params=_cparams(("arbitrary", "arbitrary")))(dp, dc, dc, p, p, p, p, conv_w, h1, dw_in)


def _sgu_stats(bv):
    gv = _gelu(bv)
    mu = jnp.mean(gv, axis=-1, keepdims=True)
    dv = gv - mu
    rstd = lax.rsqrt(jnp.mean(dv * dv, axis=-1, keepdims=True) + LN_EPS)
    return dv * rstd, rstd


def _sgu_fwd(p, wm, bias, ln_g, ln_b, *, name):
    t = p.shape[0]
    d = ln_g.shape[1]
    tt = SGU_TILE
    gd = d // SGU_GROUPS

    def body(u_ref, v_ref, wm_ref, bias_ref, lg_ref, lb_ref, sg_ref, vn_ref):
        u = _gelu(u_ref[...].astype(F32))
        vhat, _ = _sgu_stats(v_ref[...].astype(F32))
        vb = (vhat * lg_ref[...] + lb_ref[...]).astype(BF16)
        vn_ref[...] = vb
        for ci in range(tt // SGU_CHUNK):
            rows = slice(ci * SGU_CHUNK, (ci + 1) * SGU_CHUNK)
            for g in range(SGU_GROUPS):
                gs = slice(g * gd, (g + 1) * gd)
                z = _dot(wm_ref[g], vb[rows, gs], _NN) + bias_ref[g]
                sg_ref[rows, gs] = (u[rows, gs] * z).astype(BF16)

    rs = _row_spec(tt, d)
    return pl.pallas_call(
        body, name=name, grid=(t // tt,),
        in_specs=[_row_spec(tt, d, 2), _row_spec(tt, d, 3), _const_spec(wm.shape), _const_spec(bias.shape),
                  _const_spec((1, d)), _const_spec((1, d))],
        out_specs=[rs, rs], out_shape=[jax.ShapeDtypeStruct((t, d), BF16), jax.ShapeDtypeStruct((t, d), BF16)],
        compiler_params=_cparams(("parallel",)))(p, p, wm, bias, ln_g, ln_b)


def _sgu_bwd(dp, dy_b, w_out, p, vn, wm, wmt, bias, ln_g, *, name):
    t = p.shape[0]
    d = ln_g.shape[1]
    tt = SGU_TILE
    ck = SGU_CHUNK
    gd = d // SGU_GROUPS
    nsteps = t // tt

    def body(dp_in, dyb_ref, wout_ref, u_ref, v_ref, vn_ref, wm_ref, wmt_ref, bias_ref, lg_ref,
             dp_ref, dw_ref, dbs_ref, dlg_ref, dlb_ref, dwo_ref, dz_acc):
        del dp_in
        i = pl.program_id(0)

        @pl.when(i == 0)
        def _():
            dw_ref[...] = jnp.zeros_like(dw_ref)
            dwo_ref[...] = jnp.zeros_like(dwo_ref)
            dlg_ref[...] = jnp.zeros_like(dlg_ref)
            dlb_ref[...] = jnp.zeros_like(dlb_ref)
            dz_acc[...] = jnp.zeros_like(dz_acc)

        bu = u_ref[...].astype(F32)
        bv = v_ref[...].astype(F32)
        u = _gelu(bu)
        vhat, rstd = _sgu_stats(bv)
        vb = vn_ref[...]
        dsg = _dot(dyb_ref[...], wout_ref[...], _NT)
        row = lax.broadcasted_iota(jnp.int32, (ck, ck), 0)
        col = lax.broadcasted_iota(jnp.int32, (ck, ck), 1)
        causal = col <= row
        du_rows, dv_rows, sg_rows = [], [], []
        for ci in range(tt // ck):
            rows = slice(ci * ck, (ci + 1) * ck)
            du_parts, dv_parts, sg_parts = [], [], []
            for g in range(SGU_GROUPS):
                gs = slice(g * gd, (g + 1) * gd)
                z = _dot(wm_ref[g], vb[rows, gs], _NN) + bias_ref[g]
                sg_parts.append((u[rows, gs] * z).astype(BF16))
                du_parts.append(dsg[rows, gs] * z)
                dz = dsg[rows, gs] * u[rows, gs]
                dz_acc[:, gs] += dz
                dzb = dz.astype(BF16)
                dw_ref[g] += jnp.where(causal, _dot(dzb, vb[rows, gs], _NT), 0.0)
                dv_parts.append(_dot(wmt_ref[g], dzb, _NN))
            du_rows.append(jnp.concatenate(du_parts, axis=1))
            dv_rows.append(jnp.concatenate(dv_parts, axis=1))
            sg_rows.append(jnp.concatenate(sg_parts, axis=1))
        du = jnp.concatenate(du_rows, axis=0)
        dv = jnp.concatenate(dv_rows, axis=0)
        dwo_ref[...] += _dot(jnp.concatenate(sg_rows, axis=0), dyb_ref[...], _TN)
        dp_ref[:, 0:d] = (du * _gelu_grad(bu)).astype(BF16)
        dlb_ref[...] += jnp.sum(dv, axis=0, keepdims=True)
        dlg_ref[...] += jnp.sum(dv * vhat, axis=0, keepdims=True)
        dvh = dv * lg_ref[...]
        dgv = rstd * (dvh - jnp.mean(dvh, axis=-1, keepdims=True) - vhat * jnp.mean(dvh * vhat, axis=-1, keepdims=True))
        dp_ref[:, d:2 * d] = (dgv * _gelu_grad(bv)).astype(BF16)

        @pl.when(i == nsteps - 1)
        def _():
            ones = jnp.ones((8, gd), F32)
            for g in range(SGU_GROUPS):
                gs = slice(g * gd, (g + 1) * gd)
                tot = lax.dot_general(ones, dz_acc[:, gs], (_NT, ((), ())), preferred_element_type=F32,
                                      precision=lax.Precision.HIGHEST)
                dbs_ref[g:g + 1, :] = tot[0:1, :]

    rs = _row_spec(tt, d)
    c1 = _const_spec((1, d))
    return pl.pallas_call(
        body, name=name, grid=(nsteps,),
        in_specs=[pl.BlockSpec(memory_space=pl.ANY), rs, _const_spec(w_out.shape), _row_spec(tt, d, 2), _row_spec(tt, d, 3),
                  rs, _const_spec(wm.shape), _const_spec(wmt.shape), _const_spec(bias.shape), c1],
        out_specs=[pl.BlockSpec((tt, 2 * d), lambda i: (i, 1)), _const_spec(wm.shape), _const_spec((SGU_GROUPS, ck)), c1, c1,
                   _const_spec((d, d))],
        out_shape=[jax.ShapeDtypeStruct(dp.shape, dp.dtype), jax.ShapeDtypeStruct(wm.shape, F32),
                   jax.ShapeDtypeStruct((SGU_GROUPS, ck), F32), jax.ShapeDtypeStruct((1, d), F32),
                   jax.ShapeDtypeStruct((1, d), F32), jax.ShapeDtypeStruct((d, d), F32)],
        scratch_shapes=[pltpu.VMEM((ck, d), F32)],
        input_output_aliases={0: 0},
        compiler_params=_cparams(("arbitrary",)))(dp, dy_b, w_out, p, p, vn, wm, wmt, bias, ln_g)


def _softmax_rows(s):
    e = jnp.exp(s - jnp.max(s, axis=-1, keepdims=True))
    return e / jnp.sum(e, axis=-1, keepdims=True)


def _attn_fwd(q, kv, x1, w_xo, gain, *, bl, s, name):
    t, d = q.shape
    mlen = kv.shape[0] // bl
    hd = d // HEADS
    tq = min(ATTN_TILE, s)
    nq = s // tq
    scale = hd ** -0.5

    def body(q_ref, kv_ref, x1_ref, w_ref, g_ref, o_ref, x2_ref, h_ref):
        for h in range(HEADS):
            hs = slice(h * hd, (h + 1) * hd)
            vs = slice(d + h * hd, d + (h + 1) * hd)
            pr = _softmax_rows(_dot(q_ref[:, hs], kv_ref[:, hs], _NT) * scale)
            o_ref[:, hs] = _dot(pr.astype(BF16), kv_ref[:, vs], _NN).astype(BF16)
        x2 = x1_ref[...] + _dot(o_ref[...], w_ref[...], _NN)
        x2_ref[...] = x2
        h_ref[...] = _rms_apply(x2, g_ref[...]).astype(BF16)

    qs = pl.BlockSpec((tq, d), lambda b, j: (b * nq + j, 0))
    return pl.pallas_call(
        body, name=name, grid=(bl, nq),
        in_specs=[qs, pl.BlockSpec((mlen, 2 * d), lambda b, j: (b, 0)), qs, _const_spec(w_xo.shape), _const_spec((1, d))],
        out_specs=[qs, qs, qs],
        out_shape=[jax.ShapeDtypeStruct((t, d), BF16), jax.ShapeDtypeStruct((t, d), F32), jax.ShapeDtypeStruct((t, d), BF16)],
        compiler_params=_cparams(("parallel", "parallel")))(q, kv, x1, w_xo, gain)


def _attn_bwd(q, kv, do, *, bl, s, name):
    t, d = q.shape
    mlen = kv.shape[0] // bl
    hd = d // HEADS
    tq = min(ATTN_TILE, s)
    nq = s // tq
    scale = hd ** -0.5

    def body(q_ref, kv_ref, do_ref, dq_ref, dkv_ref):
        @pl.when(pl.program_id(1) == 0)
        def _():
            dkv_ref[...] = jnp.zeros_like(dkv_ref)

        for h in range(HEADS):
            hs = slice(h * hd, (h + 1) * hd)
            vs = slice(d + h * hd, d + (h + 1) * hd)
            qh, kh, vh, doh = q_ref[:, hs], kv_ref[:, hs], kv_ref[:, vs], do_ref[:, hs]
            pr = _softmax_rows(_dot(qh, kh, _NT) * scale)
            dpr = _dot(doh, vh, _NT)
            dkv_ref[:, vs] += _dot(pr.astype(BF16), doh, _TN)
            ds = (pr * (dpr - jnp.sum(dpr * pr, axis=-1, keepdims=True)) * scale).astype(BF16)
            dq_ref[:, hs] = _dot(ds, kh, _NN).astype(BF16)
            dkv_ref[:, hs] += _dot(ds, qh, _TN)

    qs = pl.BlockSpec((tq, d), lambda b, j: (b * nq + j, 0))
    ks = pl.BlockSpec((mlen, 2 * d), lambda b, j: (b, 0))
    return pl.pallas_call(
        body, name=name, grid=(bl, nq), in_specs=[qs, ks, qs], out_specs=[qs, ks],
        out_shape=[jax.ShapeDtypeStruct((t, d), BF16), jax.ShapeDtypeStruct(kv.shape, F32)],
        compiler_params=_cparams(("parallel", "arbitrary")))(q, kv, do)


def _mesh_pos():
    return lax.axis_index("x"), lax.axis_index("y"), lax.axis_index("c")


def _all_gather(arrs, *, name):
    n = len(arrs)
    hbm = pl.BlockSpec(memory_space=pl.ANY)

    def body(*refs):
        ins, outs = refs[:n], refs[n:2 * n]
        send_sems, recv_sems, loc_sems = refs[2 * n:]
        x, y, c = _mesh_pos()
        me, sib = (x, y, c), (x, y, 1 - c)
        chips = [(1 - x, y), (x, 1 - y), (1 - x, 1 - y)]

        def idx(dev):
            return 4 * dev[0] + 2 * dev[1] + dev[2]

        def copy(w, k, block, to, from_input=False):
            return pltpu.make_async_remote_copy(
                src_ref=ins[w] if from_input else outs[w].at[idx(block)], dst_ref=outs[w].at[idx(block)],
                send_sem=send_sems.at[w, k], recv_sem=recv_sems.at[w, k], device_id=to, device_id_type=MESH_ID)

        own = [pltpu.make_async_copy(ins[w], outs[w].at[idx(me)], loc_sems.at[w]) for w in range(n)]
        for cp in own:
            cp.start()
        first = []
        for w in range(n):
            first.append(copy(w, 0, me, sib, True))
            first += [copy(w, 1 + j, me, (*chip, c), True) for j, chip in enumerate(chips)]
        for cp in first:
            cp.start()
        passed = []
        for j, chip in enumerate(chips):
            for w in range(n):
                copy(w, 1 + j, (*chip, c), me).wait_recv()
                fwd = copy(w, 4 + j, (*chip, c), sib)
                fwd.start()
                passed.append(fwd)
        for w in range(n):
            copy(w, 0, sib, me).wait_recv()
            for j, chip in enumerate(chips):
                copy(w, 4 + j, (*chip, 1 - c), me).wait_recv()
        for cp in first + passed:
            cp.wait_send()
        for cp in own:
            cp.wait()

    return pl.pallas_call(
        body, name=name, in_specs=[hbm] * n, out_specs=[hbm] * n,
        out_shape=[jax.ShapeDtypeStruct((N_DEV, *a.shape), a.dtype) for a in arrs],
        scratch_shapes=[pltpu.SemaphoreType.DMA((n, 7)), pltpu.SemaphoreType.DMA((n, 7)), pltpu.SemaphoreType.DMA((n,))],
    )(*arrs)


_HBM = pl.BlockSpec(memory_space=pltpu.HBM)
_SEM = pl.BlockSpec(memory_space=pltpu.SEMAPHORE)
_ANY = pl.BlockSpec(memory_space=pl.ANY)
_EFFECT = pltpu.SideEffectType.DATAFLOW_SIDE_EFFECTING
N_PEERS = N_DEV - 1


def _related(pos, r):
    x, y, c = pos
    return (1 - x if r & 4 else x, 1 - y if r & 2 else y, 1 - c if r & 1 else c)


def _dev_index(dev):
    return 4 * dev[0] + 2 * dev[1] + dev[2]


def _in_hbm(a):
    return pltpu.with_memory_space_constraint(a, pltpu.HBM)


def _split_copies(kind, srcs, lands, send_sems, recv_sems):
    pos = _mesh_pos()
    me = _dev_index(pos)
    out = []
    for w in range(len(srcs)):
        for r in range(1, N_DEV):
            peer = _related(pos, r)
            if kind == "gather":
                src, dst_here, dst_there = srcs[w], lands[w].at[_dev_index(peer)], lands[w].at[me]
            elif srcs[w].ndim == 2:
                cb = lands[w].shape[2]
                src = srcs[w].at[:, pl.ds(pl.multiple_of(_dev_index(peer) * cb, LANES), cb)]
                dst_here = dst_there = lands[w].at[r - 1]
            else:
                src, dst_here, dst_there = srcs[w].at[_dev_index(peer)], lands[w].at[r - 1], lands[w].at[r - 1]
            out.append((src, dst_here, dst_there, send_sems.at[w * N_PEERS + r - 1], recv_sems.at[w * N_PEERS + r - 1], peer))
    return out


def _copy_start(kind, srcs, land_shapes, *, name, after=None):
    n = len(srcs)
    n_after = 0 if after is None else 1

    def body(*refs):
        src_refs, land_refs = refs[:n], refs[n:2 * n]
        send_sems, recv_sems = refs[2 * n + n_after], refs[2 * n + n_after + 1]
        token = refs[-1]
        for src, _, dst, ssem, rsem, peer in _split_copies(kind, src_refs, land_refs, send_sems, recv_sems):
            pltpu.make_async_remote_copy(src_ref=src, dst_ref=dst, send_sem=ssem, recv_sem=rsem, device_id=peer,
                                         device_id_type=MESH_ID).start()
        token[...] = jnp.zeros_like(token)

    lands = [_in_hbm(lax.empty(shape, s.dtype)) for s, shape in zip(srcs, land_shapes)]
    res = pl.pallas_call(
        body, name=name,
        out_shape=(pltpu.SemaphoreType.DMA((n * N_PEERS,)), pltpu.SemaphoreType.DMA((n * N_PEERS,)),
                   *[pltpu.HBM(s.shape, s.dtype) for s in srcs], *[pltpu.HBM(l.shape, l.dtype) for l in lands],
                   jax.ShapeDtypeStruct((8, 128), F32)),
        in_specs=[_HBM] * (2 * n) + [_ANY] * n_after,
        out_specs=(_SEM, _SEM, *[_HBM] * (2 * n), pl.BlockSpec(memory_space=pltpu.VMEM)),
        input_output_aliases={i: 2 + i for i in range(2 * n)},
        compiler_params=pltpu.CompilerParams(has_side_effects=_EFFECT),
    )(*[_in_hbm(s) for s in srcs], *lands, *([] if after is None else [after]))
    return res[0], res[1], list(res[2:2 + n]), list(res[2 + n:2 + 2 * n]), res[-1]


def _copy_wait(kind, send_sems, recv_sems, srcs, lands, after, *, name):
    n = len(srcs)

    def body(*refs):
        src_refs, land_refs = refs[:n], refs[n:2 * n]
        ssems, rsems = refs[2 * n], refs[2 * n + 1]
        for src, dst, _, ssem, rsem, peer in _split_copies(kind, src_refs, land_refs, ssems, rsems):
            cp = pltpu.make_async_remote_copy(src_ref=src, dst_ref=dst, send_sem=ssem, recv_sem=rsem, device_id=peer,
                                              device_id_type=MESH_ID)
            cp.wait_send()
            cp.wait_recv()

    res = pl.pallas_call(
        body, name=name,
        out_shape=(*[pltpu.HBM(s.shape, s.dtype) for s in srcs], *[pltpu.HBM(l.shape, l.dtype) for l in lands]),
        in_specs=[_HBM] * (2 * n) + [_SEM, _SEM, _ANY], out_specs=tuple([_HBM] * (2 * n)),
        input_output_aliases={i: i for i in range(2 * n)},
        compiler_params=pltpu.CompilerParams(has_side_effects=_EFFECT),
    )(*srcs, *lands, send_sems, recv_sems, after)
    return list(res[:n]), list(res[n:])


def _row_tile(rows):
    return max(tr for tr in range(16, min(rows, 512) + 1, 16) if rows % tr == 0)


def _adamw_math(w, g, m, v):
    m2 = ADAM_B1 * m + (1.0 - ADAM_B1) * g
    v2 = ADAM_B2 * v + (1.0 - ADAM_B2) * (g * g)
    m_hat = m2 / (1.0 - ADAM_B1 ** ADAM_STEP)
    v_hat = v2 / (1.0 - ADAM_B2 ** ADAM_STEP)
    delta = -ADAM_LR * (m_hat / (jnp.sqrt(v_hat) + ADAM_EPS) + ADAM_WD * w)
    return delta, m2, v2


def _adamw_shard(partials, landed, dev, w, m, v, *, name):
    r, c = w.shape
    tr = _row_tile(r)

    def body(dev_ref, p_ref, l_ref, w_ref, m_ref, v_ref, g_out, d_out, m_out, v_out):
        del dev_ref
        g = p_ref[...].astype(F32)
        for k in range(N_PEERS):
            g = g + l_ref[k].astype(F32)
        delta, m2, v2 = _adamw_math(w_ref[...], g, m_ref[...], v_ref[...])
        g_out[...] = g
        d_out[...] = delta
        m_out[...] = m2
        v_out[...] = v2

    blk = pl.BlockSpec((tr, c), lambda i, dev_ref: (i, 0))
    if partials.ndim == 2:
        own = pl.BlockSpec((tr, c), lambda i, dev_ref: (i, dev_ref[0]))
    else:
        own = pl.BlockSpec((None, tr, c), lambda i, dev_ref: (dev_ref[0], i, 0))
    gs = pltpu.PrefetchScalarGridSpec(
        num_scalar_prefetch=1, grid=(r // tr,),
        in_specs=[own, pl.BlockSpec((N_PEERS, tr, c), lambda i, dev_ref: (0, i, 0)), blk, blk, blk],
        out_specs=[blk] * 4)
    return pl.pallas_call(
        body, name=name, grid_spec=gs, out_shape=[jax.ShapeDtypeStruct((r, c), F32)] * 4,
        compiler_params=_cparams(("parallel",)))(dev, partials, landed, w, m, v)


def _sum_devices(p_ref, *idx):
    g = p_ref[(0, *idx)]
    for k in range(1, N_DEV):
        g = g + p_ref[(k, *idx)]
    return g


def _adamw_replicated(parts, states, loss_row, *, name):
    n_parts, n_par = len(parts), len(states)
    n_vec = n_par - (n_parts - 1)

    def body(*refs):
        part_refs, st = refs[:n_parts], refs[n_parts:n_parts + 3 * n_par]
        outs = refs[n_parts + 3 * n_par:]
        outs[0][...] = _sum_devices(part_refs[0], slice(loss_row, loss_row + 1), slice(0, 1))
        for i in range(n_par):
            g = _sum_devices(part_refs[0], slice(i, i + 1)) if i < n_vec else _sum_devices(part_refs[1 + i - n_vec])
            delta, m2, v2 = _adamw_math(st[3 * i][...], g, st[3 * i + 1][...], st[3 * i + 2][...])
            for o, val in zip(outs[1 + 4 * i:5 + 4 * i], (g, delta, m2, v2)):
                o[...] = val

    flat = [a for wmv in states for a in wmv]
    return pl.pallas_call(
        body, name=name,
        out_shape=[jax.ShapeDtypeStruct((1, 1), F32)] + [jax.ShapeDtypeStruct(w.shape, F32) for w, _, _ in states for _ in range(4)],
        compiler_params=pltpu.CompilerParams(vmem_limit_bytes=VMEM_LIMIT))(*parts, *flat)


def _adamw_column_shards(parts, dev, states, row0s, *, name):
    _, rows, _ = parts.shape
    c = states[0][0].shape[1]

    def body(dev_ref, p_ref, *refs):
        del dev_ref
        st, outs = refs[:3 * len(states)], refs[3 * len(states):]
        for j, r0 in enumerate(row0s):
            w_ref = st[3 * j]
            g = _sum_devices(p_ref, slice(r0, r0 + w_ref.shape[0]))
            delta, m2, v2 = _adamw_math(w_ref[...], g, st[3 * j + 1][...], st[3 * j + 2][...])
            for o, val in zip(outs[4 * j:4 * j + 4], (g, delta, m2, v2)):
                o[...] = val

    whole = lambda a: pl.BlockSpec(a.shape, lambda i, dev_ref: (0, 0))
    flat = [a for wmv in states for a in wmv]
    outs = [w for w, _, _ in states for _ in range(4)]
    gs = pltpu.PrefetchScalarGridSpec(
        num_scalar_prefetch=1, grid=(1,),
        in_specs=[pl.BlockSpec((N_DEV, rows, c), lambda i, dev_ref: (0, 0, dev_ref[0]))] + [whole(a) for a in flat],
        out_specs=[whole(a) for a in outs])
    return pl.pallas_call(
        body, name=name, grid_spec=gs, out_shape=[jax.ShapeDtypeStruct(a.shape, F32) for a in outs],
        compiler_params=_cparams(("arbitrary",)))(dev, parts, *flat)


def _pad_rows(a, rows):
    return jnp.pad(a, ((0, rows - a.shape[0]), (0, 0)))


def _unblock_cols(g):
    return jnp.transpose(g, (1, 0, 2)).reshape(g.shape[1], N_DEV * g.shape[2])


def kernel(x, mem, norm_mix, w_in, b_gate, conv_w, conv_b, conv_ln_g, conv_ln_b, w_conv_out, sgu_ln_g, sgu_ln_b, sgu_w, sgu_b, w_sgu_out, w_mix_out, norm_xattn, norm_mem, w_q, w_kv, w_xo, norm_ffn, w_gu, w_down, norm_final, loss_target, m_norm_mix, m_w_in, m_b_gate, m_conv_w, m_conv_b, m_conv_ln_g, m_conv_ln_b, m_w_conv_out, m_sgu_ln_g, m_sgu_ln_b, m_sgu_w, m_sgu_b, m_w_sgu_out, m_w_mix_out, m_norm_xattn, m_norm_mem, m_w_q, m_w_kv, m_w_xo, m_norm_ffn, m_w_gu, m_w_down, m_norm_final, v_norm_mix, v_w_in, v_b_gate, v_conv_w, v_conv_b, v_conv_ln_g, v_conv_ln_b, v_w_conv_out, v_sgu_ln_g, v_sgu_ln_b, v_sgu_w, v_sgu_b, v_w_sgu_out, v_w_mix_out, v_norm_xattn, v_norm_mem, v_w_q, v_w_kv, v_w_xo, v_norm_ffn, v_w_gu, v_w_down, v_norm_final):
    given = dict(locals())
    bl, s, d = x.shape
    t = bl * s
    xf = x.reshape(t, d)
    tgt = loss_target.reshape(t, d)
    memf = mem.reshape(bl * mem.shape[1], d)
    cx, cy, cc = lax.axis_index("x"), lax.axis_index("y"), lax.axis_index("c")
    dev = 4 * cx + 2 * cy + cc
    dev_id = dev.astype(jnp.int32).reshape(1)
    col_sharded = ["w_in", "w_kv"]
    transposed = ["w_gu"]

    def shard_of(name, prefix=""):
        a = given[prefix + name][0]
        return jnp.transpose(a) if name in transposed else a

    def full_weight(name, blocks):
        return _unblock_cols(blocks) if name in col_sharded else blocks.reshape(N_DEV * blocks.shape[1], blocks.shape[2])

    g_bg, g_cw = _all_gather([_pad_rows(b_gate[0], 8), _pad_rows(conv_w[0], CONV_HALO)], name="gather_small_params")
    h1, p, w_in_blocks = _in_proj_gather(xf, norm_mix + g_bg[0, 7:8, 0:1], w_in[0].astype(BF16), name="in_proj")
    early = ["w_conv_out", "w_sgu_out", "w_mix_out", "w_q", "w_kv", "w_xo"]
    late = ["w_gu", "w_down"]
    shards = {n: shard_of(n).astype(BF16) for n in early + late}
    started = {}
    for grp, names in (("early", early), ("late", late)):
        srcs = [shards[n] for n in names]
        started[grp] = _copy_start("gather", srcs, [(N_DEV, *a.shape) for a in srcs], name=f"gather_{grp}_start", after=p)
    token = started["early"][4][0:1, 0:1] + started["late"][4][0:1, 0:1]
    wfull = {}
    bg_full = _unblock_cols(g_bg)
    cw_full = _unblock_cols(g_cw)

    def finish_gather(grp, names, after):
        ssem, rsem, srcs, lands, _ = started[grp]
        _, lands = _copy_wait("gather", ssem, rsem, srcs, lands, after, name=f"gather_{grp}_wait")
        for n, land in zip(names, lands):
            wfull[n] = full_weight(n, lax.dynamic_update_index_in_dim(land, shards[n], dev, 0))

    tri = jnp.tril(jnp.ones((SGU_CHUNK, SGU_CHUNK), bool))
    wm32 = jnp.where(tri[None], sgu_w[0], 0.0)
    wm = wm32.astype(BF16)
    wmt = jnp.transpose(wm32, (0, 2, 1)).astype(BF16)
    sgu_bias = jnp.broadcast_to(sgu_b[0][:, :, None], (SGU_GROUPS, SGU_CHUNK, d // SGU_GROUPS))

    c_conv, a_act = _conv_fwd(p, cw_full, conv_b + token, conv_ln_g, conv_ln_b, bl=bl, s=s, name="conv_fwd")
    sg, vn = _sgu_fwd(p, wm, sgu_bias, sgu_ln_g, sgu_ln_b + token, name="sgu_fwd")
    finish_gather("early", early, a_act[0:16, 0:128] + sg[0:16, 0:128])
    y_a, y_b, merged, x1, h2, q = _mix_out(p, a_act, sg, wfull["w_conv_out"], wfull["w_sgu_out"], bg_full, xf,
                                           wfull["w_mix_out"], norm_xattn, wfull["w_q"], name="mix_out")
    mem_n = _rms_fwd(memf, norm_mem, name="rms_mem")
    kv = _matmul(mem_n, wfull["w_kv"], mode="nn", out_dtype=BF16, name="mm_kv", tm=1024, tn=1024, tk=1024)
    o, x2, h3 = _attn_fwd(q, kv, x1, wfull["w_xo"], norm_ffn, bl=bl, s=s, name="attn_fwd")
    finish_gather("late", late, h3)
    gu, act, dx3, loss_part, d_norm_final = _ffn_fwd(h3, x2, tgt, wfull["w_gu"], wfull["w_down"],
                                                     norm_final.reshape(1, d), name="ffn_fwd")

    grads = {}
    sent = []

    def send_grads(names, tag, after=None):
        blocks, land_shapes = [], []
        for n in names:
            g = grads[n]
            if g.ndim == 2 and n in col_sharded:
                land_shapes.append((N_PEERS, g.shape[0], g.shape[1] // N_DEV))
            else:
                if g.ndim == 2:
                    g = g.reshape(N_DEV, -1, g.shape[1])
                land_shapes.append((N_PEERS, *g.shape[1:]))
            blocks.append(g)
        ssem, rsem, srcs, lands, tok = _copy_start("scatter", blocks, land_shapes, name=f"grads_{tag}_start", after=after)
        sent.append((names, ssem, rsem, srcs, lands))
        return tok[0:1, 0:1]

    dgu, dx2, do, d_norm_ffn = _ffn_bwd(dx3, gu, x2, wfull["w_down"], wfull["w_gu"], norm_ffn, wfull["w_xo"], name="ffn_bwd")
    grads["w_down"] = _matmul(act, dx3, mode="tn", out_dtype=BF16, name="mm_dw_down", tm=1408, tn=1024, tk=2048)
    grads["w_gu"] = _matmul(dgu, h3, mode="tn", out_dtype=BF16, name="mm_dw_gu", tm=1408, tn=1024, tk=2048)
    tok = send_grads(["w_down", "w_gu"], "ffn")
    dq, dkv = _attn_bwd(q, kv, do, bl=bl, s=s, name="attn_bwd")
    grads["w_kv"] = _matmul(mem_n, dkv, mode="tn", out_dtype=BF16, name="mm_dw_kv", tm=1024, tn=256, tk=1024,
                            col_blocks=N_DEV)
    tok2 = send_grads(["w_kv"], "attn")
    dmem_n = _matmul(dkv, wfull["w_kv"], mode="nt", out_dtype=F32, name="mm_d_mem", tm=512, tn=1024, tk=2048)
    d_norm_mem = _rms_gain_grad(dmem_n, memf, name="rms_mem_bwd")
    dx1, d_norm_xattn, dw_q, dw_xo = _proj_rms_bwd(dq, dx2, x1, wfull["w_q"], norm_xattn + (tok + tok2), name="q_rms_bwd",
                                                   h=h2, h_res=o)
    dp, dy_a, dy_b, d_b_gate, dw_mix, dw_in_gates = _gates_bwd_fused(dx1, p, y_a, y_b, bg_full, wfull["w_mix_out"],
                                                                    merged, h1, name="gates_bwd")
    grads["w_xo"] = dw_xo
    grads["w_q"] = dw_q.astype(BF16)
    grads["w_mix_out"] = dw_mix.astype(BF16)
    dc, d_conv_ln_g, d_conv_ln_b, dw_conv = _conv_ln_bwd_fused(dy_a, c_conv, a_act, wfull["w_conv_out"], conv_ln_g,
                                                               conv_ln_b, name="conv_ln_bwd")
    grads["w_conv_out"] = dw_conv.astype(BF16)
    tok = send_grads(["w_xo", "w_q", "w_mix_out", "w_conv_out"], "mixer")
    dp, d_sgu_w, d_sgu_b, d_sgu_ln_g, d_sgu_ln_b, dw_sgu = _sgu_bwd(dp, dy_b, wfull["w_sgu_out"], p, vn, wm, wmt, sgu_bias,
                                                             sgu_ln_g + tok, name="sgu_bwd")
    sgw_ssem, sgw_rsem, sgw_src, sgw_land, tok = _copy_start("gather", [d_sgu_w], [(N_DEV, *d_sgu_w.shape)],
                                                             name="gather_sgu_w_start")
    cw_full = cw_full + tok[0:1, 0:1]
    dw_in = _matmul(h1, dp, mode="tn", out_dtype=BF16, name="mm_dw_in_sgu", tm=1024, tn=1024, tk=2048,
                    b_cols=(2 * d, 2 * d), out_into=(dw_in_gates, 2 * d))
    dp, d_conv_w, d_conv_b, dw_in = _conv_bwd(dp, dc, p, cw_full, h1, dw_in, bl=bl, s=s, name="conv_bwd")
    grads["w_in"] = dw_in
    grads["w_sgu_out"] = dw_sgu.astype(BF16)
    tok = send_grads(["w_in", "w_sgu_out"], "in")
    grad_x, d_norm_mix = _proj_rms_bwd(dp, dx1, xf, w_in_blocks, norm_mix + tok, name="in_proj_bwd")
    out = {}

    vec_names = ["norm_mix", "conv_b", "conv_ln_g", "conv_ln_b", "sgu_ln_g", "sgu_ln_b", "norm_xattn", "norm_mem",
                 "norm_ffn", "norm_final"]
    vec_grads = [d_norm_mix, d_conv_b, d_conv_ln_g, d_conv_ln_b, d_sgu_ln_g, d_sgu_ln_b, d_norm_xattn, d_norm_mem,
                 d_norm_ffn, d_norm_final]
    n_vec = len(vec_names)
    small_vec = jnp.concatenate([g.reshape(1, d) for g in vec_grads]
                                + [jnp.broadcast_to(loss_part, (1, d)), jnp.zeros((16 - n_vec - 1, d), F32)], axis=0)
    small_cols = jnp.concatenate([d_b_gate, d_conv_w], axis=0)
    parts_vec, parts_sb, parts_cols = _all_gather([small_vec, d_sgu_b, small_cols], name="gather_small_grads")
    _, sgw_land = _copy_wait("gather", sgw_ssem, sgw_rsem, sgw_src, sgw_land, parts_vec, name="gather_sgu_w_wait")
    parts_sw = lax.dynamic_update_index_in_dim(sgw_land[0], d_sgu_w, dev, 0)
    rep_names = vec_names + ["sgu_b", "sgu_w"]
    rep_shapes = [(1, d)] * n_vec + [d_sgu_b.shape, d_sgu_w.shape]
    states = [tuple(given[pre + n].reshape(shape) for pre in ("", "m_", "v_")) for n, shape in zip(rep_names, rep_shapes)]
    res_rep = _adamw_replicated([parts_vec, parts_sb, parts_sw], states, n_vec, name="adamw_small")
    for i, n in enumerate(rep_names):
        out[n] = [r.reshape(given[n].shape) for r in res_rep[1 + 4 * i:5 + 4 * i]]
    res_cols = _adamw_column_shards(parts_cols, dev_id, [(b_gate[0], m_b_gate[0], v_b_gate[0]),
                                                        (conv_w[0], m_conv_w[0], v_conv_w[0])], (0, 8),
                                    name="adamw_small_cols")
    out["b_gate"] = [r[None] for r in res_cols[0:4]]
    out["conv_w"] = [r[None] for r in res_cols[4:8]]

    done = res_rep[1]
    for names, ssem, rsem, srcs, lands in sent:
        srcs, lands = _copy_wait("scatter", ssem, rsem, srcs, lands, done, name=f"grads_{names[0]}_wait")
        for n, partials, landed in zip(names, srcs, lands):
            res = _adamw_shard(partials, landed, dev_id, shard_of(n), shard_of(n, "m_"), shard_of(n, "v_"),
                               name=f"adamw_{n}")
            done = res[0]
            out[n] = [(jnp.transpose(r) if n in transposed else r)[None] for r in res]

    order = ["norm_mix", "w_in", "b_gate", "conv_w", "conv_b", "conv_ln_g", "conv_ln_b", "w_conv_out", "sgu_ln_g",
             "sgu_ln_b", "sgu_w", "sgu_b", "w_sgu_out", "w_mix_out", "norm_xattn", "norm_mem", "w_q", "w_kv", "w_xo",
             "norm_ffn", "w_gu", "w_down", "norm_final"]
    loss = res_rep[0][0, 0]
    return (loss, grad_x.reshape(x.shape), *[out[n][0] for n in order], *[out[n][1] for n in order],
            *[out[n][2] for n in order], *[out[n][3] for n in order])
```

```python
import jax
import jax.numpy as jnp
from jax import lax
from jax.experimental import pallas as pl
from jax.experimental.pallas import tpu as pltpu

F32 = jnp.float32
BF16 = jnp.bfloat16
RMS_EPS = 1e-6
LN_EPS = 1e-5
CONV_WIDTH = 31
CONV_HALO = 32
CONV_ROWS = 128
CONV_COLS = 256
LANES = 128
SGU_CHUNK = 128
SGU_GROUPS = 8
SGU_TILE = 512
HEADS = 4
N_DEV = 8
ADAM_LR, ADAM_B1, ADAM_B2, ADAM_EPS, ADAM_WD, ADAM_STEP = 0.001, 0.9, 0.999, 1e-08, 0.01, 10
VMEM_LIMIT = 56 * 1024 * 1024
TOKEN_TILE = 256
ATTN_TILE = 1024
MESH_ID = pl.DeviceIdType.MESH

_GELU_K = 0.7978845608028654
_GELU_C = 0.044715


def _cparams(sem=None):
    return pltpu.CompilerParams(dimension_semantics=sem, vmem_limit_bytes=VMEM_LIMIT)


def _sigmoid(v):
    return 0.5 * jnp.tanh(0.5 * v) + 0.5


def _gelu(v):
    return 0.5 * v * (1.0 + jnp.tanh(_GELU_K * (v + _GELU_C * v * v * v)))


def _gelu_grad(v):
    th = jnp.tanh(_GELU_K * (v + _GELU_C * v * v * v))
    return 0.5 * (1.0 + th) + 0.5 * v * (1.0 - th * th) * _GELU_K * (1.0 + 3.0 * _GELU_C * v * v)


def _dot(a, b, dims):
    return lax.dot_general(a, b, (dims, ((), ())), preferred_element_type=F32)


_NN = ((1,), (0,))
_NT = ((1,), (1,))
_TN = ((0,), (0,))


def _matmul(a, b, *, mode, out_dtype, name, tm=512, tn=512, tk=512, col_blocks=None, b_cols=None, out_into=None):
    if mode == "nn":
        (m, k), (_, n) = a.shape, b.shape
    elif mode == "nt":
        (m, k), (n, _) = a.shape, b.shape
    else:
        (k, m), (_, n) = a.shape, b.shape
    b_first = 0
    if b_cols is not None:
        assert mode == "tn"
        b_first, n = b_cols
    tm, tn, tk = min(tm, m), min(tn, n), min(tk, k)
    assert b_first % tn == 0
    b_first //= tn
    assert m % tm == 0 and n % tn == 0 and k % tk == 0, (name, a.shape, b.shape, tm, tn, tk)
    nk = k // tk
    dims = {"nn": _NN, "nt": _NT, "tn": _TN}[mode]

    def body(*refs):
        a_ref, b_ref = refs[:2]
        o_ref = refs[3] if out_into is not None else refs[2]
        part = _dot(a_ref[...].astype(BF16), b_ref[...].astype(BF16), dims)
        if nk == 1:
            o_ref[...] = part.astype(out_dtype)
        else:
            acc_ref = refs[-1]
            kk = pl.program_id(2)

            @pl.when(kk == 0)
            def _():
                acc_ref[...] = part

            @pl.when(kk > 0)
            def _():
                acc_ref[...] += part

            @pl.when(kk == nk - 1)
            def _():
                o_ref[...] = acc_ref[...].astype(out_dtype)

    resident = dict(pipeline_mode=pl.Buffered(1)) if (n == tn and nk == 1 and mode != "tn" and m > tm) else {}
    if mode == "nn":
        a_spec = pl.BlockSpec((tm, tk), lambda i, j, kk: (i, kk))
        b_spec = pl.BlockSpec((tk, tn), lambda i, j, kk: (kk, j), **resident)
    elif mode == "nt":
        a_spec = pl.BlockSpec((tm, tk), lambda i, j, kk: (i, kk))
        b_spec = pl.BlockSpec((tn, tk), lambda i, j, kk: (j, kk), **resident)
    else:
        a_spec = pl.BlockSpec((tk, tm), lambda i, j, kk: (kk, i))
        b_spec = pl.BlockSpec((tk, tn), lambda i, j, kk: (kk, j + b_first))
    in_specs, args = [a_spec, b_spec], [a, b]
    out_shape = [jax.ShapeDtypeStruct((m, n), out_dtype)]
    out_specs = [pl.BlockSpec((tm, tn), lambda i, j, kk: (i, j))]
    if col_blocks is not None:
        assert (n // col_blocks) % tn == 0
        per = n // col_blocks // tn
        out_shape = [jax.ShapeDtypeStruct((col_blocks, m, n // col_blocks), out_dtype)]
        out_specs = [pl.BlockSpec((None, tm, tn), lambda i, j, kk: (j // per, i, j % per))]
    aliases = {}
    if out_into is not None:
        target, first = out_into
        assert col_blocks is None and first % tn == 0 and target.dtype == out_dtype
        in_specs.append(pl.BlockSpec(memory_space=pl.ANY))
        args.append(target)
        aliases = {len(args) - 1: 0}
        out_shape = [jax.ShapeDtypeStruct(target.shape, target.dtype)]
        out_specs = [pl.BlockSpec((tm, tn), lambda i, j, kk: (i, j + first // tn))]
    res = pl.pallas_call(
        body, name=name, grid=(m // tm, n // tn, nk), in_specs=in_specs, out_specs=out_specs, out_shape=out_shape,
        scratch_shapes=[pltpu.VMEM((tm, tn), F32)] if nk > 1 else [], input_output_aliases=aliases,
        compiler_params=_cparams(("parallel", "parallel", "arbitrary")),
    )(*args)
    return res[0]


def _row_call(name, t, tm, rows_in, residents, rows_out, accs, body):
    n_in, n_res, n_out, n_acc = len(rows_in), len(residents), len(rows_out), len(accs)
    steps = t // tm
    assert t % tm == 0
    narrow = [i for i, a in enumerate(accs) if a[1] != F32]

    def kernel_body(*refs):
        in_refs, res_refs = refs[:n_in], refs[n_in:n_in + n_res]
        out_refs = refs[n_in + n_res:n_in + n_res + n_out]
        acc_out = list(refs[n_in + n_res + n_out:n_in + n_res + n_out + n_acc])
        scratch = refs[n_in + n_res + n_out + n_acc:]
        acc_refs = list(acc_out)
        for s_ref, i in zip(scratch, narrow):
            acc_refs[i] = s_ref
        if accs:
            @pl.when(pl.program_id(0) == 0)
            def _():
                for acc in acc_refs:
                    acc[...] = jnp.zeros_like(acc)
        body(in_refs, res_refs, out_refs, acc_refs)
        if narrow:
            @pl.when(pl.program_id(0) == steps - 1)
            def _():
                for i in narrow:
                    acc_out[i][...] = acc_refs[i][...].astype(acc_out[i].dtype)

    once = dict(pipeline_mode=pl.Buffered(1)) if steps > 1 else {}
    in_specs = [pl.BlockSpec((tm, cols), lambda i, cb=cb: (i, cb)) for _, cols, cb in rows_in]
    in_specs += [pl.BlockSpec(r.shape, lambda i, nd=r.ndim: (0,) * nd, **once) for r in residents]
    out_specs = [pl.BlockSpec((tm, cols), lambda i, cb=cb: (i, cb)) for _, cols, cb, _ in rows_out]
    out_specs += [pl.BlockSpec(a[0], lambda i, nd=len(a[0]), cb=(a[3] if len(a) == 4 else 0): (0,) * (nd - 1) + (cb,))
                  for a in accs]
    out_shape = [jax.ShapeDtypeStruct((t, total), dt) for total, _, _, dt in rows_out]
    out_shape += [jax.ShapeDtypeStruct((a[0][0], a[2]) if len(a) == 4 else a[0], a[1]) for a in accs]
    return pl.pallas_call(
        kernel_body, name=name, grid=(steps,), in_specs=in_specs, out_specs=out_specs, out_shape=out_shape,
        scratch_shapes=[pltpu.VMEM(accs[i][0], F32) for i in narrow],
        compiler_params=_cparams(("arbitrary",) if accs else ("parallel",)),
    )(*[a for a, _, _ in rows_in], *residents)


def _rms_apply(xv, gain):
    return xv * lax.rsqrt(jnp.mean(xv * xv, axis=-1, keepdims=True) + RMS_EPS) * gain


def _rms_grad(dres, dh, xv, gain):
    r = lax.rsqrt(jnp.mean(xv * xv, axis=-1, keepdims=True) + RMS_EPS)
    xhat = xv * r
    dxh = dh * gain
    dx = dres + r * (dxh - xhat * jnp.mean(dxh * xhat, axis=-1, keepdims=True))
    return dx, jnp.sum(dh * xhat, axis=0, keepdims=True)


def _in_proj_gather(xf, gain, w_shard, *, name):
    t, d = xf.shape
    cb = w_shard.shape[1]
    tm = min(1024, t)
    steps = t // tm
    mx, my, _ = _mesh_pos()
    order = jnp.stack([2 * mx + my, 2 * (1 - mx) + my, 2 * mx + (1 - my), 2 * (1 - mx) + (1 - my)]).astype(jnp.int32)

    def body(order_ref, x_ref, g_ref, ws_ref, h_ref, p_ref, wout_ref, w_ref, send_sems, recv_sems, own_sem):
        ps, i = pl.program_id(0), pl.program_id(1)
        x, y, c = _mesh_pos()
        me, sib = (x, y, c), (x, y, 1 - c)
        chips = [(1 - x, y), (x, 1 - y), (1 - x, 1 - y)]

        def copy(k, block, to, from_shard=False):
            return pltpu.make_async_remote_copy(
                src_ref=ws_ref if from_shard else w_ref.at[_dev_index(block)], dst_ref=w_ref.at[_dev_index(block)],
                send_sem=send_sems.at[k], recv_sem=recv_sems.at[k], device_id=to, device_id_type=MESH_ID)

        own = pltpu.make_async_copy(ws_ref, w_ref.at[_dev_index(me)], own_sem)
        first = [copy(0, me, sib, True)] + [copy(1 + j, me, (*chip, c), True) for j, chip in enumerate(chips)]
        passed = [copy(4 + j, (*chip, c), sib) for j, chip in enumerate(chips)]

        @pl.when(jnp.logical_and(ps == 0, i == 0))
        def _():
            own.start()
            for cp in first:
                cp.start()
            own.wait()
            copy(0, sib, me).wait_recv()

        for j, chip in enumerate(chips):
            @pl.when(jnp.logical_and(ps == j + 1, i == 0))
            def _(j=j, chip=chip):
                copy(1 + j, (*chip, c), me).wait_recv()
                passed[j].start()
                copy(4 + j, (*chip, 1 - c), me).wait_recv()

        h = _rms_apply(x_ref[...], g_ref[...]).astype(BF16)
        h_ref[...] = h
        chip_id = order_ref[ps]
        p_ref[:, 0:cb] = _dot(h, w_ref[2 * chip_id], _NN).astype(BF16)
        p_ref[:, cb:2 * cb] = _dot(h, w_ref[2 * chip_id + 1], _NN).astype(BF16)

        @pl.when(jnp.logical_and(ps == 3, i == steps - 1))
        def _():
            for cp in first + passed:
                cp.wait_send()
            keep = pltpu.make_async_copy(w_ref, wout_ref, own_sem)
            keep.start()
            keep.wait()

    gs = pltpu.PrefetchScalarGridSpec(
        num_scalar_prefetch=1, grid=(4, steps),
        in_specs=[pl.BlockSpec((tm, d), lambda ps, i, o: (i, 0)), pl.BlockSpec((1, d), lambda ps, i, o: (0, 0)),
                  pl.BlockSpec(memory_space=pl.ANY)],
        out_specs=[pl.BlockSpec((tm, d), lambda ps, i, o: (jnp.where(ps == 0, i, steps - 1), 0)),
                   pl.BlockSpec((tm, 2 * cb), lambda ps, i, o: (i, o[ps])), pl.BlockSpec(memory_space=pl.ANY)],
        scratch_shapes=[pltpu.VMEM((N_DEV, d, cb), BF16), pltpu.SemaphoreType.DMA((7,)), pltpu.SemaphoreType.DMA((7,)),
                        pltpu.SemaphoreType.DMA(())])
    return pl.pallas_call(
        body, name=name, grid_spec=gs,
        out_shape=[jax.ShapeDtypeStruct((t, d), BF16), jax.ShapeDtypeStruct((t, N_DEV * cb), BF16),
                   jax.ShapeDtypeStruct((N_DEV, d, cb), BF16)],
        compiler_params=_cparams(("arbitrary", "arbitrary")))(order, xf, gain, w_shard)


def _mem_proj(memf, gain, w_kv, *, name):
    t, d = memf.shape
    n = w_kv.shape[1]

    def body(ins, res, outs, accs):
        h = _rms_apply(ins[0][...], res[0][...]).astype(BF16)
        outs[0][...] = h
        outs[1][...] = _dot(h, res[1][...], _NN).astype(BF16)

    return _row_call(name, t, min(512, t), [(memf, d, 0)], [gain, w_kv], [(d, d, 0, BF16), (n, n, 0, BF16)], [], body)


def _mem_proj_bwd(dkv, memf, w_kv, *, name):
    t, d = memf.shape
    n = w_kv.shape[1]

    def body(ins, res, outs, accs):
        dh = _dot(ins[0][...].astype(BF16), res[0][...], _NT)
        xv = ins[1][...]
        xhat = xv * lax.rsqrt(jnp.mean(xv * xv, axis=-1, keepdims=True) + RMS_EPS)
        accs[0][...] += jnp.sum(dh * xhat, axis=0, keepdims=True)

    return _row_call(name, t, min(512, t), [(dkv, n, 0), (memf, d, 0)], [w_kv], [], [((1, d), F32)], body)[0]


def _mix_out(p, a_act, sg, w_conv_out, w_sgu_out, b_gate, xf, w_mix, gain, w_q, *, name):
    t, d = xf.shape

    def body(ins, res, outs, accs):
        ga_ref, gb_ref, act_ref, sg_ref, x_ref = ins
        bg_ref, wm_ref, g_ref, wq_ref, wa_ref, wb_ref = res
        ya_ref, yb_ref, m_ref, x1_ref, h_ref, q_ref = outs
        y_a = _dot(act_ref[...], wa_ref[...], _NN).astype(BF16)
        y_b = _dot(sg_ref[...], wb_ref[...], _NN).astype(BF16)
        ya_ref[...] = y_a
        yb_ref[...] = y_b
        sa = _sigmoid(ga_ref[...].astype(F32) + bg_ref[0:1, :])
        sb = _sigmoid(gb_ref[...].astype(F32) + bg_ref[1:2, :])
        merged = (sa * y_a.astype(F32) + sb * y_b.astype(F32)).astype(BF16)
        m_ref[...] = merged
        x1 = x_ref[...] + _dot(merged, wm_ref[...], _NN)
        x1_ref[...] = x1
        h = _rms_apply(x1, g_ref[...]).astype(BF16)
        h_ref[...] = h
        q_ref[...] = _dot(h, wq_ref[...], _NN).astype(BF16)

    bf = (d, d, 0, BF16)
    return _row_call(name, t, min(512, t), [(p, d, 4), (p, d, 5), (a_act, d, 0), (sg, d, 0), (xf, d, 0)],
                     [b_gate, w_mix, gain, w_q, w_conv_out, w_sgu_out], [bf, bf, bf, (d, d, 0, F32), bf, bf], [], body)


def _ffn_fwd(h3, x2, target, w_gu_t, w_down, gain, *, name):
    t, d = x2.shape
    f2 = w_gu_t.shape[0]
    f = f2 // 2
    half = f // 2

    def body(ins, res, outs, accs):
        h_ref, x2_ref, t_ref = ins
        wgu_ref, wd_ref, g_ref = res
        gu_ref, act_ref, dx_ref = outs
        loss_ref, dg_ref = accs
        h = h_ref[...]
        x3 = x2_ref[...]
        for c0 in (0, half):
            gt = _dot(h, wgu_ref[c0:c0 + half, :], _NT).astype(BF16)
            up = _dot(h, wgu_ref[f + c0:f + c0 + half, :], _NT).astype(BF16)
            gu_ref[:, c0:c0 + half] = gt
            gu_ref[:, f + c0:f + c0 + half] = up
            gtf = gt.astype(F32)
            act = (gtf * _sigmoid(gtf) * up.astype(F32)).astype(BF16)
            act_ref[:, c0:c0 + half] = act
            x3 = x3 + _dot(act, wd_ref[c0:c0 + half, :], _NN)
        g = g_ref[...]
        r = lax.rsqrt(jnp.mean(x3 * x3, axis=-1, keepdims=True) + RMS_EPS)
        xhat = x3 * r
        err = xhat * g - t_ref[...]
        loss_ref[...] += 0.5 * jnp.sum(jnp.mean(err * err, axis=-1, keepdims=True), axis=0, keepdims=True)
        dy = err * (1.0 / d)
        dg_ref[...] += jnp.sum(dy * xhat, axis=0, keepdims=True)
        dxh = dy * g
        dx_ref[...] = r * (dxh - xhat * jnp.mean(dxh * xhat, axis=-1, keepdims=True))

    return _row_call(name, t, min(256, t), [(h3, d, 0), (x2, d, 0), (target, d, 0)], [w_gu_t, w_down, gain],
                     [(f2, f2, 0, BF16), (f, f, 0, BF16), (d, d, 0, F32)], [((1, 1), F32), ((1, d), F32)], body)


def _ffn_bwd(dx3, gu, x2, w_down, w_gu_t, gain, w_xo, *, name):
    t, d = x2.shape
    f2 = w_gu_t.shape[0]
    f = f2 // 2
    half = f // 2

    def body(ins, res, outs, accs):
        dx3_ref, gu_ref, x2_ref = ins
        wd_ref, wgu_ref, g_ref, wxo_ref = res
        dgu_ref, dx2_ref, do_ref = outs
        (dg_ref,) = accs
        dx3v = dx3_ref[...]
        dxb = dx3v.astype(BF16)
        dh = jnp.zeros(dx3v.shape, F32)
        for c0 in (0, half):
            dact = _dot(dxb, wd_ref[c0:c0 + half, :], _NT)
            gt = gu_ref[:, c0:c0 + half].astype(F32)
            up = gu_ref[:, f + c0:f + c0 + half].astype(F32)
            sg = _sigmoid(gt)
            dgt = (dact * up * sg * (1.0 + gt * (1.0 - sg))).astype(BF16)
            dup = (dact * gt * sg).astype(BF16)
            dgu_ref[:, c0:c0 + half] = dgt
            dgu_ref[:, f + c0:f + c0 + half] = dup
            dh = dh + _dot(dgt, wgu_ref[c0:c0 + half, :], _NN) + _dot(dup, wgu_ref[f + c0:f + c0 + half, :], _NN)
        dx2, dg = _rms_grad(dx3v, dh, x2_ref[...], g_ref[...])
        dx2_ref[...] = dx2
        dg_ref[...] += dg
        do_ref[...] = _dot(dx2.astype(BF16), wxo_ref[...], _NT).astype(BF16)

    return _row_call(name, t, min(256, t), [(dx3, d, 0), (gu, f2, 0), (x2, d, 0)], [w_down, w_gu_t, gain, w_xo],
                     [(f2, f2, 0, BF16), (d, d, 0, F32), (d, d, 0, BF16)], [((1, d), F32)], body)


def _proj_rms_bwd(dy, dres, x, w, gain, *, name, h=None, h_res=None):
    t, d = x.shape
    k = dy.shape[1]

    def body(ins, res, outs, accs):
        dy_ref, dres_ref, x_ref = ins[:3]
        w_ref, g_ref = res
        if h is not None:
            accs[1][...] += _dot(ins[3][...], dy_ref[...], _TN)
        if h_res is not None:
            accs[-1][...] += _dot(ins[-1][...], dres_ref[...].astype(BF16), _TN)
        if w.ndim == 3:
            cb = w.shape[2]
            dh = _dot(dy_ref[:, 0:cb], w_ref[0], _NT)
            for j in range(1, w.shape[0]):
                dh = dh + _dot(dy_ref[:, j * cb:(j + 1) * cb], w_ref[j], _NT)
        else:
            dh = _dot(dy_ref[...], w_ref[...], _NT)
        dx, dg = _rms_grad(dres_ref[...], dh, x_ref[...], g_ref[...])
        outs[0][...] = dx
        accs[0][...] += dg

    rows_in = [(dy, k, 0), (dres, d, 0), (x, d, 0)] + [(a, d, 0) for a in (h, h_res) if a is not None]
    accs = [((1, d), F32)] + ([((d, k), BF16)] if h is not None else []) + ([((d, d), BF16)] if h_res is not None else [])
    tm = 1024 if (w.ndim == 2 and h_res is None) else 512
    return _row_call(name, t, min(tm, t), rows_in, [w, gain], [(d, d, 0, F32)], accs, body)


def _gates_bwd_fused(dx1, p, y_a, y_b, b_gate, w_mix, merged, h1, *, name):
    t, d = y_a.shape

    def body(ins, res, outs, accs):
        dx_ref, ga_ref, gb_ref, ya_ref, yb_ref, m_ref, h1_ref = ins
        bg_ref, wm_ref = res
        dp_ref, dya_ref, dyb_ref = outs
        dbg_ref, dwm_ref, dwin_ref = accs
        dxb = dx_ref[...].astype(BF16)
        dwm_ref[...] += _dot(m_ref[...], dxb, _TN)
        dm = _dot(dxb, wm_ref[...], _NT)
        sa = _sigmoid(ga_ref[...].astype(F32) + bg_ref[0:1, :])
        sb = _sigmoid(gb_ref[...].astype(F32) + bg_ref[1:2, :])
        dya_ref[...] = (dm * sa).astype(BF16)
        dyb_ref[...] = (dm * sb).astype(BF16)
        dga = dm * ya_ref[...].astype(F32) * sa * (1.0 - sa)
        dgb = dm * yb_ref[...].astype(F32) * sb * (1.0 - sb)
        dp_ref[:, 0:d] = dga.astype(BF16)
        dp_ref[:, d:2 * d] = dgb.astype(BF16)
        dbg_ref[0:1, :] += jnp.sum(dga, axis=0, keepdims=True)
        dbg_ref[1:2, :] += jnp.sum(dgb, axis=0, keepdims=True)
        dwin_ref[...] += _dot(h1_ref[...], dp_ref[...], _TN)

    return _row_call(name, t, min(256, t),
                     [(dx1, d, 0), (p, d, 4), (p, d, 5), (y_a, d, 0), (y_b, d, 0), (merged, d, 0), (h1, d, 0)],
                     [b_gate, w_mix], [(p.shape[1], 2 * d, 2, BF16), (d, d, 0, BF16), (d, d, 0, BF16)],
                     [((8, d), F32), ((d, d), BF16), ((d, 2 * d), BF16, p.shape[1], 2)], body)


def _conv_ln_bwd_fused(dy_a, c, a_act, w_conv_out, ln_g, ln_b, *, name):
    t, d = c.shape

    def body(ins, res, outs, accs):
        dy_ref, c_ref, act_ref = ins
        w_ref, lg_ref, lb_ref = res
        dlg_ref, dlb_ref, dw_ref = accs
        dw_ref[...] += _dot(act_ref[...], dy_ref[...], _TN)
        dact = _dot(dy_ref[...], w_ref[...], _NT)
        cv = c_ref[...].astype(F32)
        g = lg_ref[...]
        mu = jnp.mean(cv, axis=-1, keepdims=True)
        dv = cv - mu
        rstd = lax.rsqrt(jnp.mean(dv * dv, axis=-1, keepdims=True) + LN_EPS)
        chat = dv * rstd
        aln = chat * g + lb_ref[...]
        sg = _sigmoid(aln)
        daln = dact * (sg * (1.0 + aln * (1.0 - sg)))
        dlb_ref[...] += jnp.sum(daln, axis=0, keepdims=True)
        dlg_ref[...] += jnp.sum(daln * chat, axis=0, keepdims=True)
        dchat = daln * g
        dc = rstd * (dchat - jnp.mean(dchat, axis=-1, keepdims=True)
                     - chat * jnp.mean(dchat * chat, axis=-1, keepdims=True))
        outs[0][...] = dc.astype(BF16)

    return _row_call(name, t, min(1024, t), [(dy_a, d, 0), (c, d, 0), (a_act, d, 0)], [w_conv_out, ln_g, ln_b],
                     [(d, d, 0, BF16)], [((1, d), F32), ((1, d), F32), ((d, d), BF16)], body)


def _row_spec(tt, cols, col_block=0):
    return pl.BlockSpec((tt, cols), lambda i: (i, col_block))


def _const_spec(shape):
    return pl.BlockSpec(shape, lambda *_: (0,) * len(shape))


def _rms_fwd(x, gain, *, name):
    t, d = x.shape
    tt = min(TOKEN_TILE, t)

    def body(x_ref, g_ref, h_ref):
        xv = x_ref[...]
        r = lax.rsqrt(jnp.mean(xv * xv, axis=-1, keepdims=True) + RMS_EPS)
        h_ref[...] = (xv * r * g_ref[...]).astype(BF16)

    return pl.pallas_call(
        body, name=name, grid=(t // tt,), in_specs=[_row_spec(tt, d), _const_spec((1, d))],
        out_specs=_row_spec(tt, d), out_shape=jax.ShapeDtypeStruct((t, d), BF16),
        compiler_params=_cparams(("parallel",)))(x, gain)


def _rms_gain_grad(dh, x, *, name):
    t, d = x.shape
    tt = min(TOKEN_TILE, t)

    def body(dh_ref, x_ref, dg_ref):
        @pl.when(pl.program_id(0) == 0)
        def _():
            dg_ref[...] = jnp.zeros_like(dg_ref)

        xv = x_ref[...]
        xhat = xv * lax.rsqrt(jnp.mean(xv * xv, axis=-1, keepdims=True) + RMS_EPS)
        dg_ref[...] += jnp.sum(dh_ref[...].astype(F32) * xhat, axis=0, keepdims=True)

    rs = _row_spec(tt, d)
    return pl.pallas_call(
        body, name=name, grid=(t // tt,), in_specs=[rs, rs], out_specs=_const_spec((1, d)),
        out_shape=jax.ShapeDtypeStruct((1, d), F32), compiler_params=_cparams(("arbitrary",)))(dh, x)


SUBLANES = 8
SHIFT_ROWS = 40


def _conv_apply(sbuf_ref, w_ref, out_ref, tt, offsets, bias_ref=None):
    d = out_ref.shape[1]
    for cc in range(d // LANES):
        cs = slice(cc * LANES, (cc + 1) * LANES)
        taps = [jnp.broadcast_to(w_ref[k:k + 1, cs], (SUBLANES, LANES)) for k in range(CONV_WIDTH)]
        bias = None if bias_ref is None else jnp.broadcast_to(bias_ref[:, cs], (SUBLANES, LANES))

        def row_body(r, carry, cs=cs, taps=taps, bias=bias):
            r0 = pl.multiple_of(r * CONV_ROWS, CONV_ROWS)
            for q in range(CONV_ROWS // SUBLANES):
                acc = _tap(sbuf_ref, r0 + q * SUBLANES, cs, offsets[0]) * taps[0]
                for k in range(1, CONV_WIDTH):
                    acc = acc + _tap(sbuf_ref, r0 + q * SUBLANES, cs, offsets[k]) * taps[k]
                if bias is not None:
                    acc = acc + bias
                out_ref[pl.ds(r0 + q * SUBLANES, SUBLANES), cs] = acc
            return carry

        lax.fori_loop(0, tt // CONV_ROWS, row_body, 0)


def _fill_shifts(sbuf_ref, rows):
    d = sbuf_ref.shape[2]
    assert rows % SHIFT_ROWS == 0

    def row_body(i, carry):
        r0 = pl.multiple_of(i * SHIFT_ROWS, SUBLANES)
        for cc in range(d // CONV_COLS):
            cs = slice(cc * CONV_COLS, (cc + 1) * CONV_COLS)
            win = sbuf_ref[0, pl.ds(r0, SHIFT_ROWS + SUBLANES), cs]
            for sh in range(1, SUBLANES):
                sbuf_ref[sh, pl.ds(r0, SHIFT_ROWS), cs] = win[sh:sh + SHIFT_ROWS, :]
        return carry

    lax.fori_loop(0, rows // SHIFT_ROWS, row_body, 0)


def _tap(sbuf_ref, r0, cs, offset):
    sh = offset % SUBLANES
    return sbuf_ref[sh, pl.ds(pl.multiple_of(r0 + (offset - sh), SUBLANES), SUBLANES), cs]


def _conv_specs(bl, s, tt, d, col_a, col_g):
    nj = s // tt
    per = tt // CONV_HALO
    main_a = pl.BlockSpec((tt, d), lambda b, j: (b * nj + j, col_a))
    main_g = pl.BlockSpec((tt, d), lambda b, j: (b * nj + j, col_g))
    prev = lambda b, j: jnp.maximum((b * nj + j) * per - 1, 0)
    halo_a = pl.BlockSpec((CONV_HALO, d), lambda b, j: (prev(b, j), col_a))
    halo_g = pl.BlockSpec((CONV_HALO, d), lambda b, j: (prev(b, j), col_g))
    return main_a, main_g, halo_a, halo_g


def _fill_glu(sbuf_ref, a_ref, g_ref, ha_ref, hg_ref, tt):
    first = pl.program_id(1) == 0
    ha = ha_ref[...].astype(F32)
    hg = hg_ref[...].astype(F32)
    sbuf_ref[0, pl.ds(0, CONV_HALO), :] = jnp.where(first, 0.0, ha * _sigmoid(hg))
    av = a_ref[...].astype(F32)
    gv = g_ref[...].astype(F32)
    sbuf_ref[0, pl.ds(CONV_HALO, tt), :] = av * _sigmoid(gv)
    _fill_shifts(sbuf_ref, tt + CONV_HALO - SUBLANES)


def _conv_fwd(p, conv_w, conv_b, ln_g, ln_b, *, bl, s, name):
    t = p.shape[0]
    d = conv_w.shape[1]
    tt = min(TOKEN_TILE, s)
    off = CONV_HALO - (CONV_WIDTH - 1)

    def body(a_ref, g_ref, ha_ref, hg_ref, w_ref, b_ref, lg_ref, lb_ref, c_ref, act_ref, sbuf_ref, cbuf_ref):
        _fill_glu(sbuf_ref, a_ref, g_ref, ha_ref, hg_ref, tt)

        _conv_apply(sbuf_ref, w_ref, cbuf_ref, tt, [off + k for k in range(CONV_WIDTH)], bias_ref=b_ref)
        cv = cbuf_ref[...]
        c_ref[...] = cv.astype(BF16)
        mu = jnp.mean(cv, axis=-1, keepdims=True)
        dv = cv - mu
        rstd = lax.rsqrt(jnp.mean(dv * dv, axis=-1, keepdims=True) + LN_EPS)
        aln = dv * rstd * lg_ref[...] + lb_ref[...]
        act_ref[...] = (aln * _sigmoid(aln)).astype(BF16)

    main_a, main_g, halo_a, halo_g = _conv_specs(bl, s, tt, d, 0, 1)
    out_spec = pl.BlockSpec((tt, d), lambda b, j: (b * (s // tt) + j, 0))
    return pl.pallas_call(
        body, name=name, grid=(bl, s // tt),
        in_specs=[main_a, main_g, halo_a, halo_g, _const_spec((CONV_HALO, d)), _const_spec((1, d)), _const_spec((1, d)),
                  _const_spec((1, d))],
        out_specs=[out_spec, out_spec],
        out_shape=[jax.ShapeDtypeStruct((t, d), BF16), jax.ShapeDtypeStruct((t, d), BF16)],
        scratch_shapes=[pltpu.VMEM((SUBLANES, tt + CONV_HALO, d), F32), pltpu.VMEM((tt, d), F32)],
        compiler_params=_cparams(("parallel", "parallel")))(p, p, p, p, conv_w, conv_b, ln_g, ln_b)


def _conv_bwd(dp, dc, p, conv_w, h1, dw_in, *, bl, s, name):
    t = p.shape[0]
    d = conv_w.shape[1]
    tt = min(TOKEN_TILE, s)
    nj = s // tt
    per = tt // CONV_HALO
    off = CONV_HALO - (CONV_WIDTH - 1)
    last_blk = t // CONV_HALO - 1

    def body(dp_in, dc_ref, dcn_ref, a_ref, g_ref, ha_ref, hg_ref, w_ref, h1_ref, dwin_in, dp_ref, dw_ref, db_ref,
             dwin_out, gbuf_ref, dbuf_ref, dglu_ref, acc_ref, dwin_ref):
        del dp_in, dwin_in
        b, j = pl.program_id(0), pl.program_id(1)
        start = jnp.logical_and(b == 0, j == 0)
        end = jnp.logical_and(b == bl - 1, j == nj - 1)

        @pl.when(start)
        def _():
            acc_ref[...] = jnp.zeros_like(acc_ref)
            db_ref[...] = jnp.zeros_like(db_ref)
            dwin_ref[...] = jnp.zeros_like(dwin_ref)

        _fill_glu(gbuf_ref, a_ref, g_ref, ha_ref, hg_ref, tt)
        dcv = dc_ref[...].astype(F32)
        dbuf_ref[0, pl.ds(0, tt), :] = dcv
        dbuf_ref[0, pl.ds(tt, CONV_HALO), :] = jnp.where(j == nj - 1, 0.0, dcn_ref[...].astype(F32))
        _fill_shifts(dbuf_ref, tt + CONV_HALO - SUBLANES)
        db_ref[...] += jnp.sum(dcv, axis=0, keepdims=True)

        for cc in range(d // LANES):
            cs = slice(cc * LANES, (cc + 1) * LANES)

            def row_body(r, accs, cs=cs):
                r0 = pl.multiple_of(r * CONV_ROWS, CONV_ROWS)
                accs = list(accs)
                for q in range(CONV_ROWS // SUBLANES):
                    dcw = dbuf_ref[0, pl.ds(r0 + q * SUBLANES, SUBLANES), cs]
                    for k in range(CONV_WIDTH):
                        accs[k] = accs[k] + dcw * _tap(gbuf_ref, r0 + q * SUBLANES, cs, off + k)
                return tuple(accs)

            zero = jnp.zeros((SUBLANES, LANES), F32)
            accs = lax.fori_loop(0, tt // CONV_ROWS, row_body, (zero,) * CONV_WIDTH)
            for k in range(CONV_WIDTH):
                acc_ref[k, :, cs] += accs[k]

        _conv_apply(dbuf_ref, w_ref, dglu_ref, tt, [CONV_WIDTH - 1 - k for k in range(CONV_WIDTH)])
        dglu = dglu_ref[...]
        av = a_ref[...].astype(F32)
        sg = _sigmoid(g_ref[...].astype(F32))
        dp_ref[:, 0:d] = (dglu * sg).astype(BF16)
        dp_ref[:, d:2 * d] = (dglu * av * sg * (1.0 - sg)).astype(BF16)
        dwin_ref[...] += _dot(h1_ref[...], dp_ref[...], _TN)

        @pl.when(end)
        def _():
            for k in range(CONV_WIDTH):
                dw_ref[k:k + 1, :] = jnp.sum(acc_ref[k], axis=0, keepdims=True)
            dw_ref[CONV_WIDTH:CONV_HALO, :] = jnp.zeros((CONV_HALO - CONV_WIDTH, d), F32)
            dwin_out[...] = dwin_ref[...].astype(dwin_out.dtype)

    main_a, main_g, halo_a, halo_g = _conv_specs(bl, s, tt, d, 0, 1)
    dc_main = pl.BlockSpec((tt, d), lambda b, j: (b * nj + j, 0))
    dc_next = pl.BlockSpec((CONV_HALO, d), lambda b, j: (jnp.minimum((b * nj + j + 1) * per, last_blk), 0))
    hbm = pl.BlockSpec(memory_space=pl.ANY)
    return pl.pallas_call(
        body, name=name, grid=(bl, nj),
        in_specs=[hbm, dc_main, dc_next, main_a, main_g, halo_a, halo_g, _const_spec((CONV_HALO, d)), dc_main, hbm],
        out_specs=[pl.BlockSpec((tt, 2 * d), lambda b, j: (b * nj + j, 0)), _const_spec((CONV_HALO, d)), _const_spec((1, d)),
                   _const_spec((d, 2 * d))],
        out_shape=[jax.ShapeDtypeStruct(dp.shape, dp.dtype), jax.ShapeDtypeStruct((CONV_HALO, d), F32),
                   jax.ShapeDtypeStruct((1, d), F32), jax.ShapeDtypeStruct(dw_in.shape, dw_in.dtype)],
        scratch_shapes=[pltpu.VMEM((SUBLANES, tt + CONV_HALO, d), F32), pltpu.VMEM((SUBLANES, tt + CONV_HALO, d), F32),
                        pltpu.VMEM((tt, d), F32), pltpu.VMEM((CONV_HALO, SUBLANES, d), F32), pltpu.VMEM((d, 2 * d), F32)],
        input_output_aliases={0: 0, 9: 3},
        compiler_params=_cparams(("arbitrary", "arbitrary")))(dp, dc, dc, p, p, p, p, conv_w, h1, dw_in)


def _sgu_stats(bv):
    gv = _gelu(bv)
    mu = jnp.mean(gv, axis=-1, keepdims=True)
    dv = gv - mu
    rstd = lax.rsqrt(jnp.mean(dv * dv, axis=-1, keepdims=True) + LN_EPS)
    return dv * rstd, rstd


def _sgu_fwd(p, wm, bias, ln_g, ln_b, *, name):
    t = p.shape[0]
    d = ln_g.shape[1]
    tt = SGU_TILE
    gd = d // SGU_GROUPS

    def body(u_ref, v_ref, wm_ref, bias_ref, lg_ref, lb_ref, sg_ref, vn_ref):
        u = _gelu(u_ref[...].astype(F32))
        vhat, _ = _sgu_stats(v_ref[...].astype(F32))
        vb = (vhat * lg_ref[...] + lb_ref[...]).astype(BF16)
        vn_ref[...] = vb
        for ci in range(tt // SGU_CHUNK):
            rows = slice(ci * SGU_CHUNK, (ci + 1) * SGU_CHUNK)
            for g in range(SGU_GROUPS):
                gs = slice(g * gd, (g + 1) * gd)
                z = _dot(wm_ref[g], vb[rows, gs], _NN) + bias_ref[g]
                sg_ref[rows, gs] = (u[rows, gs] * z).astype(BF16)

    rs = _row_spec(tt, d)
    return pl.pallas_call(
        body, name=name, grid=(t // tt,),
        in_specs=[_row_spec(tt, d, 2), _row_spec(tt, d, 3), _const_spec(wm.shape), _const_spec(bias.shape),
                  _const_spec((1, d)), _const_spec((1, d))],
        out_specs=[rs, rs], out_shape=[jax.ShapeDtypeStruct((t, d), BF16), jax.ShapeDtypeStruct((t, d), BF16)],
        compiler_params=_cparams(("parallel",)))(p, p, wm, bias, ln_g, ln_b)


def _sgu_bwd(dp, dy_b, w_out, p, vn, wm, wmt, bias, ln_g, *, name):
    t = p.shape[0]
    d = ln_g.shape[1]
    tt = SGU_TILE
    ck = SGU_CHUNK
    gd = d // SGU_GROUPS
    nsteps = t // tt

    def body(dp_in, dyb_ref, wout_ref, u_ref, v_ref, vn_ref, wm_ref, wmt_ref, bias_ref, lg_ref,
             dp_ref, dw_ref, dbs_ref, dlg_ref, dlb_ref, dz_acc):
        del dp_in
        i = pl.program_id(0)

        @pl.when(i == 0)
        def _():
            dw_ref[...] = jnp.zeros_like(dw_ref)
            dlg_ref[...] = jnp.zeros_like(dlg_ref)
            dlb_ref[...] = jnp.zeros_like(dlb_ref)
            dz_acc[...] = jnp.zeros_like(dz_acc)

        bu = u_ref[...].astype(F32)
        bv = v_ref[...].astype(F32)
        u = _gelu(bu)
        vhat, rstd = _sgu_stats(bv)
        vb = vn_ref[...]
        dsg = _dot(dyb_ref[...], wout_ref[...], _NT)
        row = lax.broadcasted_iota(jnp.int32, (ck, ck), 0)
        col = lax.broadcasted_iota(jnp.int32, (ck, ck), 1)
        causal = col <= row
        du_rows, dv_rows = [], []
        for ci in range(tt // ck):
            rows = slice(ci * ck, (ci + 1) * ck)
            du_parts, dv_parts = [], []
            for g in range(SGU_GROUPS):
                gs = slice(g * gd, (g + 1) * gd)
                z = _dot(wm_ref[g], vb[rows, gs], _NN) + bias_ref[g]
                du_parts.append(dsg[rows, gs] * z)
                dz = dsg[rows, gs] * u[rows, gs]
                dz_acc[:, gs] += dz
                dzb = dz.astype(BF16)
                dw_ref[g] += jnp.where(causal, _dot(dzb, vb[rows, gs], _NT), 0.0)
                dv_parts.append(_dot(wmt_ref[g], dzb, _NN))
            du_rows.append(jnp.concatenate(du_parts, axis=1))
            dv_rows.append(jnp.concatenate(dv_parts, axis=1))
        du = jnp.concatenate(du_rows, axis=0)
        dv = jnp.concatenate(dv_rows, axis=0)
        dp_ref[:, 0:d] = (du * _gelu_grad(bu)).astype(BF16)
        dlb_ref[...] += jnp.sum(dv, axis=0, keepdims=True)
        dlg_ref[...] += jnp.sum(dv * vhat, axis=0, keepdims=True)
        dvh = dv * lg_ref[...]
        dgv = rstd * (dvh - jnp.mean(dvh, axis=-1, keepdims=True) - vhat * jnp.mean(dvh * vhat, axis=-1, keepdims=True))
        dp_ref[:, d:2 * d] = (dgv * _gelu_grad(bv)).astype(BF16)

        @pl.when(i == nsteps - 1)
        def _():
            ones = jnp.ones((8, gd), F32)
            for g in range(SGU_GROUPS):
                gs = slice(g * gd, (g + 1) * gd)
                tot = lax.dot_general(ones, dz_acc[:, gs], (_NT, ((), ())), preferred_element_type=F32,
                                      precision=lax.Precision.HIGHEST)
                dbs_ref[g:g + 1, :] = tot[0:1, :]

    rs = _row_spec(tt, d)
    c1 = _const_spec((1, d))
    return pl.pallas_call(
        body, name=name, grid=(nsteps,),
        in_specs=[pl.BlockSpec(memory_space=pl.ANY), rs, _const_spec(w_out.shape), _row_spec(tt, d, 2), _row_spec(tt, d, 3),
                  rs, _const_spec(wm.shape), _const_spec(wmt.shape), _const_spec(bias.shape), c1],
        out_specs=[pl.BlockSpec((tt, 2 * d), lambda i: (i, 1)), _const_spec(wm.shape), _const_spec((SGU_GROUPS, ck)), c1, c1],
        out_shape=[jax.ShapeDtypeStruct(dp.shape, dp.dtype), jax.ShapeDtypeStruct(wm.shape, F32),
                   jax.ShapeDtypeStruct((SGU_GROUPS, ck), F32), jax.ShapeDtypeStruct((1, d), F32),
                   jax.ShapeDtypeStruct((1, d), F32)],
        scratch_shapes=[pltpu.VMEM((ck, d), F32)],
        input_output_aliases={0: 0},
        compiler_params=_cparams(("arbitrary",)))(dp, dy_b, w_out, p, p, vn, wm, wmt, bias, ln_g)


def _softmax_rows(s):
    e = jnp.exp(s - jnp.max(s, axis=-1, keepdims=True))
    return e / jnp.sum(e, axis=-1, keepdims=True)


def _attn_fwd(q, kv, x1, w_xo, gain, *, bl, s, name):
    t, d = q.shape
    mlen = kv.shape[0] // bl
    hd = d // HEADS
    tq = min(ATTN_TILE, s)
    nq = s // tq
    scale = hd ** -0.5

    def body(q_ref, kv_ref, x1_ref, w_ref, g_ref, o_ref, x2_ref, h_ref):
        for h in range(HEADS):
            hs = slice(h * hd, (h + 1) * hd)
            vs = slice(d + h * hd, d + (h + 1) * hd)
            pr = _softmax_rows(_dot(q_ref[:, hs], kv_ref[:, hs], _NT) * scale)
            o_ref[:, hs] = _dot(pr.astype(BF16), kv_ref[:, vs], _NN).astype(BF16)
        x2 = x1_ref[...] + _dot(o_ref[...], w_ref[...], _NN)
        x2_ref[...] = x2
        h_ref[...] = _rms_apply(x2, g_ref[...]).astype(BF16)

    qs = pl.BlockSpec((tq, d), lambda b, j: (b * nq + j, 0))
    return pl.pallas_call(
        body, name=name, grid=(bl, nq),
        in_specs=[qs, pl.BlockSpec((mlen, 2 * d), lambda b, j: (b, 0)), qs, _const_spec(w_xo.shape), _const_spec((1, d))],
        out_specs=[qs, qs, qs],
        out_shape=[jax.ShapeDtypeStruct((t, d), BF16), jax.ShapeDtypeStruct((t, d), F32), jax.ShapeDtypeStruct((t, d), BF16)],
        compiler_params=_cparams(("parallel", "parallel")))(q, kv, x1, w_xo, gain)


def _attn_bwd(q, kv, do, *, bl, s, name):
    t, d = q.shape
    mlen = kv.shape[0] // bl
    hd = d // HEADS
    tq = min(ATTN_TILE, s)
    nq = s // tq
    scale = hd ** -0.5

    def body(q_ref, kv_ref, do_ref, dq_ref, dkv_ref):
        @pl.when(pl.program_id(1) == 0)
        def _():
            dkv_ref[...] = jnp.zeros_like(dkv_ref)

        for h in range(HEADS):
            hs = slice(h * hd, (h + 1) * hd)
            vs = slice(d + h * hd, d + (h + 1) * hd)
            qh, kh, vh, doh = q_ref[:, hs], kv_ref[:, hs], kv_ref[:, vs], do_ref[:, hs]
            pr = _softmax_rows(_dot(qh, kh, _NT) * scale)
            dpr = _dot(doh, vh, _NT)
            dkv_ref[:, vs] += _dot(pr.astype(BF16), doh, _TN)
            ds = (pr * (dpr - jnp.sum(dpr * pr, axis=-1, keepdims=True)) * scale).astype(BF16)
            dq_ref[:, hs] = _dot(ds, kh, _NN).astype(BF16)
            dkv_ref[:, hs] += _dot(ds, qh, _TN)

    qs = pl.BlockSpec((tq, d), lambda b, j: (b * nq + j, 0))
    ks = pl.BlockSpec((mlen, 2 * d), lambda b, j: (b, 0))
    return pl.pallas_call(
        body, name=name, grid=(bl, nq), in_specs=[qs, ks, qs], out_specs=[qs, ks],
        out_shape=[jax.ShapeDtypeStruct((t, d), BF16), jax.ShapeDtypeStruct(kv.shape, F32)],
        compiler_params=_cparams(("parallel", "arbitrary")))(q, kv, do)


def _mesh_pos():
    return lax.axis_index("x"), lax.axis_index("y"), lax.axis_index("c")


def _all_gather(arrs, *, name):
    n = len(arrs)
    hbm = pl.BlockSpec(memory_space=pl.ANY)

    def body(*refs):
        ins, outs = refs[:n], refs[n:2 * n]
        send_sems, recv_sems, loc_sems = refs[2 * n:]
        x, y, c = _mesh_pos()
        me, sib = (x, y, c), (x, y, 1 - c)
        chips = [(1 - x, y), (x, 1 - y), (1 - x, 1 - y)]

        def idx(dev):
            return 4 * dev[0] + 2 * dev[1] + dev[2]

        def copy(w, k, block, to, from_input=False):
            return pltpu.make_async_remote_copy(
                src_ref=ins[w] if from_input else outs[w].at[idx(block)], dst_ref=outs[w].at[idx(block)],
                send_sem=send_sems.at[w, k], recv_sem=recv_sems.at[w, k], device_id=to, device_id_type=MESH_ID)

        own = [pltpu.make_async_copy(ins[w], outs[w].at[idx(me)], loc_sems.at[w]) for w in range(n)]
        for cp in own:
            cp.start()
        first = []
        for w in range(n):
            first.append(copy(w, 0, me, sib, True))
            first += [copy(w, 1 + j, me, (*chip, c), True) for j, chip in enumerate(chips)]
        for cp in first:
            cp.start()
        passed = []
        for j, chip in enumerate(chips):
            for w in range(n):
                copy(w, 1 + j, (*chip, c), me).wait_recv()
                fwd = copy(w, 4 + j, (*chip, c), sib)
                fwd.start()
                passed.append(fwd)
        for w in range(n):
            copy(w, 0, sib, me).wait_recv()
            for j, chip in enumerate(chips):
                copy(w, 4 + j, (*chip, 1 - c), me).wait_recv()
        for cp in first + passed:
            cp.wait_send()
        for cp in own:
            cp.wait()

    return pl.pallas_call(
        body, name=name, in_specs=[hbm] * n, out_specs=[hbm] * n,
        out_shape=[jax.ShapeDtypeStruct((N_DEV, *a.shape), a.dtype) for a in arrs],
        scratch_shapes=[pltpu.SemaphoreType.DMA((n, 7)), pltpu.SemaphoreType.DMA((n, 7)), pltpu.SemaphoreType.DMA((n,))],
    )(*arrs)


_HBM = pl.BlockSpec(memory_space=pltpu.HBM)
_SEM = pl.BlockSpec(memory_space=pltpu.SEMAPHORE)
_ANY = pl.BlockSpec(memory_space=pl.ANY)
_EFFECT = pltpu.SideEffectType.DATAFLOW_SIDE_EFFECTING
N_PEERS = N_DEV - 1


def _related(pos, r):
    x, y, c = pos
    return (1 - x if r & 4 else x, 1 - y if r & 2 else y, 1 - c if r & 1 else c)


def _dev_index(dev):
    return 4 * dev[0] + 2 * dev[1] + dev[2]


def _in_hbm(a):
    return pltpu.with_memory_space_constraint(a, pltpu.HBM)


def _split_copies(kind, srcs, lands, send_sems, recv_sems):
    pos = _mesh_pos()
    me = _dev_index(pos)
    out = []
    for w in range(len(srcs)):
        for r in range(1, N_DEV):
            peer = _related(pos, r)
            if kind == "gather":
                src, dst_here, dst_there = srcs[w], lands[w].at[_dev_index(peer)], lands[w].at[me]
            elif srcs[w].ndim == 2:
                cb = lands[w].shape[2]
                src = srcs[w].at[:, pl.ds(pl.multiple_of(_dev_index(peer) * cb, LANES), cb)]
                dst_here = dst_there = lands[w].at[r - 1]
            else:
                src, dst_here, dst_there = srcs[w].at[_dev_index(peer)], lands[w].at[r - 1], lands[w].at[r - 1]
            out.append((src, dst_here, dst_there, send_sems.at[w * N_PEERS + r - 1], recv_sems.at[w * N_PEERS + r - 1], peer))
    return out


def _copy_start(kind, srcs, land_shapes, *, name, after=None):
    n = len(srcs)
    n_after = 0 if after is None else 1

    def body(*refs):
        src_refs, land_refs = refs[:n], refs[n:2 * n]
        send_sems, recv_sems = refs[2 * n + n_after], refs[2 * n + n_after + 1]
        token = refs[-1]
        for src, _, dst, ssem, rsem, peer in _split_copies(kind, src_refs, land_refs, send_sems, recv_sems):
            pltpu.make_async_remote_copy(src_ref=src, dst_ref=dst, send_sem=ssem, recv_sem=rsem, device_id=peer,
                                         device_id_type=MESH_ID).start()
        token[...] = jnp.zeros_like(token)

    lands = [_in_hbm(lax.empty(shape, s.dtype)) for s, shape in zip(srcs, land_shapes)]
    res = pl.pallas_call(
        body, name=name,
        out_shape=(pltpu.SemaphoreType.DMA((n * N_PEERS,)), pltpu.SemaphoreType.DMA((n * N_PEERS,)),
                   *[pltpu.HBM(s.shape, s.dtype) for s in srcs], *[pltpu.HBM(l.shape, l.dtype) for l in lands],
                   jax.ShapeDtypeStruct((8, 128), F32)),
        in_specs=[_HBM] * (2 * n) + [_ANY] * n_after,
        out_specs=(_SEM, _SEM, *[_HBM] * (2 * n), pl.BlockSpec(memory_space=pltpu.VMEM)),
        input_output_aliases={i: 2 + i for i in range(2 * n)},
        compiler_params=pltpu.CompilerParams(has_side_effects=_EFFECT),
    )(*[_in_hbm(s) for s in srcs], *lands, *([] if after is None else [after]))
    return res[0], res[1], list(res[2:2 + n]), list(res[2 + n:2 + 2 * n]), res[-1]


def _copy_wait(kind, send_sems, recv_sems, srcs, lands, after, *, name):
    n = len(srcs)

    def body(*refs):
        src_refs, land_refs = refs[:n], refs[n:2 * n]
        ssems, rsems = refs[2 * n], refs[2 * n + 1]
        for src, dst, _, ssem, rsem, peer in _split_copies(kind, src_refs, land_refs, ssems, rsems):
            cp = pltpu.make_async_remote_copy(src_ref=src, dst_ref=dst, send_sem=ssem, recv_sem=rsem, device_id=peer,
                                              device_id_type=MESH_ID)
            cp.wait_send()
            cp.wait_recv()

    res = pl.pallas_call(
        body, name=name,
        out_shape=(*[pltpu.HBM(s.shape, s.dtype) for s in srcs], *[pltpu.HBM(l.shape, l.dtype) for l in lands]),
        in_specs=[_HBM] * (2 * n) + [_SEM, _SEM, _ANY], out_specs=tuple([_HBM] * (2 * n)),
        input_output_aliases={i: i for i in range(2 * n)},
        compiler_params=pltpu.CompilerParams(has_side_effects=_EFFECT),
    )(*srcs, *lands, send_sems, recv_sems, after)
    return list(res[:n]), list(res[n:])


def _row_tile(rows):
    return max(tr for tr in range(16, min(rows, 512) + 1, 16) if rows % tr == 0)


def _adamw_math(w, g, m, v):
    m2 = ADAM_B1 * m + (1.0 - ADAM_B1) * g
    v2 = ADAM_B2 * v + (1.0 - ADAM_B2) * (g * g)
    m_hat = m2 / (1.0 - ADAM_B1 ** ADAM_STEP)
    v_hat = v2 / (1.0 - ADAM_B2 ** ADAM_STEP)
    delta = -ADAM_LR * (m_hat / (jnp.sqrt(v_hat) + ADAM_EPS) + ADAM_WD * w)
    return delta, m2, v2


def _adamw_shard(partials, landed, dev, w, m, v, *, name):
    r, c = w.shape
    tr = _row_tile(r)

    def body(dev_ref, p_ref, l_ref, w_ref, m_ref, v_ref, g_out, d_out, m_out, v_out):
        del dev_ref
        g = p_ref[...].astype(F32)
        for k in range(N_PEERS):
            g = g + l_ref[k].astype(F32)
        delta, m2, v2 = _adamw_math(w_ref[...], g, m_ref[...], v_ref[...])
        g_out[...] = g
        d_out[...] = delta
        m_out[...] = m2
        v_out[...] = v2

    blk = pl.BlockSpec((tr, c), lambda i, dev_ref: (i, 0))
    if partials.ndim == 2:
        own = pl.BlockSpec((tr, c), lambda i, dev_ref: (i, dev_ref[0]))
    else:
        own = pl.BlockSpec((None, tr, c), lambda i, dev_ref: (dev_ref[0], i, 0))
    gs = pltpu.PrefetchScalarGridSpec(
        num_scalar_prefetch=1, grid=(r // tr,),
        in_specs=[own, pl.BlockSpec((N_PEERS, tr, c), lambda i, dev_ref: (0, i, 0)), blk, blk, blk],
        out_specs=[blk] * 4)
    return pl.pallas_call(
        body, name=name, grid_spec=gs, out_shape=[jax.ShapeDtypeStruct((r, c), F32)] * 4,
        compiler_params=_cparams(("parallel",)))(dev, partials, landed, w, m, v)


def _sum_devices(p_ref, *idx):
    g = p_ref[(0, *idx)]
    for k in range(1, N_DEV):
        g = g + p_ref[(k, *idx)]
    return g


def _adamw_replicated(parts, states, loss_row, *, name):
    n_parts, n_par = len(parts), len(states)
    n_vec = n_par - (n_parts - 1)

    def body(*refs):
        part_refs, st = refs[:n_parts], refs[n_parts:n_parts + 3 * n_par]
        outs = refs[n_parts + 3 * n_par:]
        outs[0][...] = _sum_devices(part_refs[0], slice(loss_row, loss_row + 1), slice(0, 1))
        for i in range(n_par):
            g = _sum_devices(part_refs[0], slice(i, i + 1)) if i < n_vec else _sum_devices(part_refs[1 + i - n_vec])
            delta, m2, v2 = _adamw_math(st[3 * i][...], g, st[3 * i + 1][...], st[3 * i + 2][...])
            for o, val in zip(outs[1 + 4 * i:5 + 4 * i], (g, delta, m2, v2)):
                o[...] = val

    flat = [a for wmv in states for a in wmv]
    return pl.pallas_call(
        body, name=name,
        out_shape=[jax.ShapeDtypeStruct((1, 1), F32)] + [jax.ShapeDtypeStruct(w.shape, F32) for w, _, _ in states for _ in range(4)],
        compiler_params=pltpu.CompilerParams(vmem_limit_bytes=VMEM_LIMIT))(*parts, *flat)


def _adamw_column_shards(parts, dev, states, row0s, *, name):
    _, rows, _ = parts.shape
    c = states[0][0].shape[1]

    def body(dev_ref, p_ref, *refs):
        del dev_ref
        st, outs = refs[:3 * len(states)], refs[3 * len(states):]
        for j, r0 in enumerate(row0s):
            w_ref = st[3 * j]
            g = _sum_devices(p_ref, slice(r0, r0 + w_ref.shape[0]))
            delta, m2, v2 = _adamw_math(w_ref[...], g, st[3 * j + 1][...], st[3 * j + 2][...])
            for o, val in zip(outs[4 * j:4 * j + 4], (g, delta, m2, v2)):
                o[...] = val

    whole = lambda a: pl.BlockSpec(a.shape, lambda i, dev_ref: (0, 0))
    flat = [a for wmv in states for a in wmv]
    outs = [w for w, _, _ in states for _ in range(4)]
    gs = pltpu.PrefetchScalarGridSpec(
        num_scalar_prefetch=1, grid=(1,),
        in_specs=[pl.BlockSpec((N_DEV, rows, c), lambda i, dev_ref: (0, 0, dev_ref[0]))] + [whole(a) for a in flat],
        out_specs=[whole(a) for a in outs])
    return pl.pallas_call(
        body, name=name, grid_spec=gs, out_shape=[jax.ShapeDtypeStruct(a.shape, F32) for a in outs],
        compiler_params=_cparams(("arbitrary",)))(dev, parts, *flat)


def _pad_rows(a, rows):
    return jnp.pad(a, ((0, rows - a.shape[0]), (0, 0)))


def _unblock_cols(g):
    return jnp.transpose(g, (1, 0, 2)).reshape(g.shape[1], N_DEV * g.shape[2])


def kernel(x, mem, norm_mix, w_in, b_gate, conv_w, conv_b, conv_ln_g, conv_ln_b, w_conv_out, sgu_ln_g, sgu_ln_b, sgu_w, sgu_b, w_sgu_out, w_mix_out, norm_xattn, norm_mem, w_q, w_kv, w_xo, norm_ffn, w_gu, w_down, norm_final, loss_target, m_norm_mix, m_w_in, m_b_gate, m_conv_w, m_conv_b, m_conv_ln_g, m_conv_ln_b, m_w_conv_out, m_sgu_ln_g, m_sgu_ln_b, m_sgu_w, m_sgu_b, m_w_sgu_out, m_w_mix_out, m_norm_xattn, m_norm_mem, m_w_q, m_w_kv, m_w_xo, m_norm_ffn, m_w_gu, m_w_down, m_norm_final, v_norm_mix, v_w_in, v_b_gate, v_conv_w, v_conv_b, v_conv_ln_g, v_conv_ln_b, v_w_conv_out, v_sgu_ln_g, v_sgu_ln_b, v_sgu_w, v_sgu_b, v_w_sgu_out, v_w_mix_out, v_norm_xattn, v_norm_mem, v_w_q, v_w_kv, v_w_xo, v_norm_ffn, v_w_gu, v_w_down, v_norm_final):
    given = dict(locals())
    bl, s, d = x.shape
    t = bl * s
    xf = x.reshape(t, d)
    tgt = loss_target.reshape(t, d)
    memf = mem.reshape(bl * mem.shape[1], d)
    cx, cy, cc = lax.axis_index("x"), lax.axis_index("y"), lax.axis_index("c")
    dev = 4 * cx + 2 * cy + cc
    dev_id = dev.astype(jnp.int32).reshape(1)
    col_sharded = ["w_in", "w_kv"]
    transposed = ["w_gu"]

    def shard_of(name, prefix=""):
        a = given[prefix + name][0]
        return jnp.transpose(a) if name in transposed else a

    def full_weight(name, blocks):
        return _unblock_cols(blocks) if name in col_sharded else blocks.reshape(N_DEV * blocks.shape[1], blocks.shape[2])

    g_bg, g_cw = _all_gather([_pad_rows(b_gate[0], 8), _pad_rows(conv_w[0], CONV_HALO)], name="gather_small_params")
    h1, p, w_in_blocks = _in_proj_gather(xf, norm_mix + g_bg[0, 7:8, 0:1], w_in[0].astype(BF16), name="in_proj")
    early = ["w_conv_out", "w_sgu_out", "w_mix_out", "w_q", "w_kv", "w_xo"]
    late = ["w_gu", "w_down"]
    shards = {n: shard_of(n).astype(BF16) for n in early + late}
    started = {}
    for grp, names in (("early", early), ("late", late)):
        srcs = [shards[n] for n in names]
        started[grp] = _copy_start("gather", srcs, [(N_DEV, *a.shape) for a in srcs], name=f"gather_{grp}_start", after=p)
    token = started["early"][4][0:1, 0:1] + started["late"][4][0:1, 0:1]
    wfull = {}
    bg_full = _unblock_cols(g_bg)
    cw_full = _unblock_cols(g_cw)

    def finish_gather(grp, names, after):
        ssem, rsem, srcs, lands, _ = started[grp]
        _, lands = _copy_wait("gather", ssem, rsem, srcs, lands, after, name=f"gather_{grp}_wait")
        for n, land in zip(names, lands):
            wfull[n] = full_weight(n, lax.dynamic_update_index_in_dim(land, shards[n], dev, 0))

    tri = jnp.tril(jnp.ones((SGU_CHUNK, SGU_CHUNK), bool))
    wm32 = jnp.where(tri[None], sgu_w[0], 0.0)
    wm = wm32.astype(BF16)
    wmt = jnp.transpose(wm32, (0, 2, 1)).astype(BF16)
    sgu_bias = jnp.broadcast_to(sgu_b[0][:, :, None], (SGU_GROUPS, SGU_CHUNK, d // SGU_GROUPS))

    c_conv, a_act = _conv_fwd(p, cw_full, conv_b + token, conv_ln_g, conv_ln_b, bl=bl, s=s, name="conv_fwd")
    sg, vn = _sgu_fwd(p, wm, sgu_bias, sgu_ln_g, sgu_ln_b + token, name="sgu_fwd")
    finish_gather("early", early, a_act[0:16, 0:128] + sg[0:16, 0:128])
    y_a, y_b, merged, x1, h2, q = _mix_out(p, a_act, sg, wfull["w_conv_out"], wfull["w_sgu_out"], bg_full, xf,
                                           wfull["w_mix_out"], norm_xattn, wfull["w_q"], name="mix_out")
    mem_n, kv = _mem_proj(memf, norm_mem, wfull["w_kv"], name="mem_proj")
    o, x2, h3 = _attn_fwd(q, kv, x1, wfull["w_xo"], norm_ffn, bl=bl, s=s, name="attn_fwd")
    finish_gather("late", late, h3)
    gu, act, dx3, loss_part, d_norm_final = _ffn_fwd(h3, x2, tgt, wfull["w_gu"], wfull["w_down"],
                                                     norm_final.reshape(1, d), name="ffn_fwd")

    grads = {}
    sent = []

    def send_grads(names, tag, after=None):
        blocks, land_shapes = [], []
        for n in names:
            g = grads[n]
            if g.ndim == 2 and n in col_sharded:
                land_shapes.append((N_PEERS, g.shape[0], g.shape[1] // N_DEV))
            else:
                if g.ndim == 2:
                    g = g.reshape(N_DEV, -1, g.shape[1])
                land_shapes.append((N_PEERS, *g.shape[1:]))
            blocks.append(g)
        ssem, rsem, srcs, lands, tok = _copy_start("scatter", blocks, land_shapes, name=f"grads_{tag}_start", after=after)
        sent.append((names, ssem, rsem, srcs, lands))
        return tok[0:1, 0:1]

    dgu, dx2, do, d_norm_ffn = _ffn_bwd(dx3, gu, x2, wfull["w_down"], wfull["w_gu"], norm_ffn, wfull["w_xo"], name="ffn_bwd")
    grads["w_down"] = _matmul(act, dx3, mode="tn", out_dtype=BF16, name="mm_dw_down", tm=1408, tn=1024, tk=2048)
    grads["w_gu"] = _matmul(dgu, h3, mode="tn", out_dtype=BF16, name="mm_dw_gu", tm=1408, tn=1024, tk=2048)
    tok = send_grads(["w_down", "w_gu"], "ffn")
    dq, dkv = _attn_bwd(q, kv, do, bl=bl, s=s, name="attn_bwd")
    grads["w_kv"] = _matmul(mem_n, dkv, mode="tn", out_dtype=BF16, name="mm_dw_kv", tm=1024, tn=256, tk=1024,
                            col_blocks=N_DEV)
    tok2 = send_grads(["w_kv"], "attn")
    d_norm_mem = _mem_proj_bwd(dkv, memf, wfull["w_kv"], name="mem_proj_bwd")
    dx1, d_norm_xattn, dw_q, dw_xo = _proj_rms_bwd(dq, dx2, x1, wfull["w_q"], norm_xattn + (tok + tok2), name="q_rms_bwd",
                                                   h=h2, h_res=o)
    dp, dy_a, dy_b, d_b_gate, dw_mix, dw_in_gates = _gates_bwd_fused(dx1, p, y_a, y_b, bg_full, wfull["w_mix_out"],
                                                                    merged, h1, name="gates_bwd")
    grads["w_xo"] = dw_xo
    grads["w_q"] = dw_q.astype(BF16)
    grads["w_mix_out"] = dw_mix.astype(BF16)
    grads["w_sgu_out"] = _matmul(sg, dy_b, mode="tn", out_dtype=BF16, name="mm_dw_sgu", tm=1024, tn=1024, tk=2048)
    dc, d_conv_ln_g, d_conv_ln_b, dw_conv = _conv_ln_bwd_fused(dy_a, c_conv, a_act, wfull["w_conv_out"], conv_ln_g,
                                                               conv_ln_b, name="conv_ln_bwd")
    grads["w_conv_out"] = dw_conv.astype(BF16)
    tok = send_grads(["w_xo", "w_q", "w_mix_out", "w_sgu_out", "w_conv_out"], "mixer")
    dp, d_sgu_w, d_sgu_b, d_sgu_ln_g, d_sgu_ln_b = _sgu_bwd(dp, dy_b, wfull["w_sgu_out"], p, vn, wm, wmt, sgu_bias,
                                                             sgu_ln_g + tok, name="sgu_bwd")
    sgw_ssem, sgw_rsem, sgw_src, sgw_land, tok = _copy_start("gather", [d_sgu_w], [(N_DEV, *d_sgu_w.shape)],
                                                             name="gather_sgu_w_start")
    cw_full = cw_full + tok[0:1, 0:1]
    dw_in = _matmul(h1, dp, mode="tn", out_dtype=BF16, name="mm_dw_in_sgu", tm=1024, tn=1024, tk=2048,
                    b_cols=(2 * d, 2 * d), out_into=(dw_in_gates, 2 * d))
    dp, d_conv_w, d_conv_b, dw_in = _conv_bwd(dp, dc, p, cw_full, h1, dw_in, bl=bl, s=s, name="conv_bwd")
    grads["w_in"] = dw_in
    tok = send_grads(["w_in"], "in")
    grad_x, d_norm_mix = _proj_rms_bwd(dp, dx1, xf, w_in_blocks, norm_mix + tok, name="in_proj_bwd")
    out = {}

    vec_names = ["norm_mix", "conv_b", "conv_ln_g", "conv_ln_b", "sgu_ln_g", "sgu_ln_b", "norm_xattn", "norm_mem",
                 "norm_ffn", "norm_final"]
    vec_grads = [d_norm_mix, d_conv_b, d_conv_ln_g, d_conv_ln_b, d_sgu_ln_g, d_sgu_ln_b, d_norm_xattn, d_norm_mem,
                 d_norm_ffn, d_norm_final]
    n_vec = len(vec_names)
    small_vec = jnp.concatenate([g.reshape(1, d) for g in vec_grads]
                                + [jnp.broadcast_to(loss_part, (1, d)), jnp.zeros((16 - n_vec - 1, d), F32)], axis=0)
    small_cols = jnp.concatenate([d_b_gate, d_conv_w], axis=0)
    parts_vec, parts_sb, parts_cols = _all_gather([small_vec, d_sgu_b, small_cols], name="gather_small_grads")
    _, sgw_land = _copy_wait("gather", sgw_ssem, sgw_rsem, sgw_src, sgw_land, parts_vec, name="gather_sgu_w_wait")
    parts_sw = lax.dynamic_update_index_in_dim(sgw_land[0], d_sgu_w, dev, 0)
    rep_names = vec_names + ["sgu_b", "sgu_w"]
    rep_shapes = [(1, d)] * n_vec + [d_sgu_b.shape, d_sgu_w.shape]
    states = [tuple(given[pre + n].reshape(shape) for pre in ("", "m_", "v_")) for n, shape in zip(rep_names, rep_shapes)]
    res_rep = _adamw_replicated([parts_vec, parts_sb, parts_sw], states, n_vec, name="adamw_small")
    for i, n in enumerate(rep_names):
        out[n] = [r.reshape(given[n].shape) for r in res_rep[1 + 4 * i:5 + 4 * i]]
    res_cols = _adamw_column_shards(parts_cols, dev_id, [(b_gate[0], m_b_gate[0], v_b_gate[0]),
                                                        (conv_w[0], m_conv_w[0], v_conv_w[0])], (0, 8),
                                    name="adamw_small_cols")
    out["b_gate"] = [r[None] for r in res_cols[0:4]]
    out["conv_w"] = [r[None] for r in res_cols[4:8]]

    done = res_rep[1]
    for names, ssem, rsem, srcs, lands in sent:
        srcs, lands = _copy_wait("scatter", ssem, rsem, srcs, lands, done, name=f"grads_{names[0]}_wait")
        for n, partials, landed in zip(names, srcs, lands):
            res = _adamw_shard(partials, landed, dev_id, shard_of(n), shard_of(n, "m_"), shard_of(n, "v_"),
                               name=f"adamw_{n}")
            done = res[0]
            out[n] = [(jnp.transpose(r) if n in transposed else r)[None] for r in res]

    order = ["norm_mix", "w_in", "b_gate", "conv_w", "conv_b", "conv_ln_g", "conv_ln_b", "w_conv_out", "sgu_ln_g",
             "sgu_ln_b", "sgu_w", "sgu_b", "w_sgu_out", "w_mix_out", "norm_xattn", "norm_mem", "w_q", "w_kv", "w_xo",
             "norm_ffn", "w_gu", "w_down", "norm_final"]
    loss = res_rep[0][0, 0]
    return (loss, grad_x.reshape(x.shape), *[out[n][0] for n in order], *[out[n][1] for n in order],
            *[out[n][2] for n in order], *[out[n][3] for n in order])
```
